```python
import jax, jax.numpy as jnp
from jax import lax
import numpy as np

D_MODEL = 1024
BATCH = 8
SEQ = 4096
DEPTH = 4

CHUNK = 64
EPS = 1e-6
N_A = DEPTH // 2
N_B = DEPTH - N_A
A_CHUNK = 128
A_DFF = 2 * D_MODEL
A_GROUPS = 8
A_GROUP_DIM = A_DFF // A_GROUPS
B_HEADS = 16
B_HEAD_DIM = D_MODEL // B_HEADS
N_LEFT_CHUNKS = 8
BAND = (N_LEFT_CHUNKS + 1) * CHUNK
MAX_REL = 256
ATTN_SCALE = B_HEAD_DIM ** -0.5
NEG_INF = -1e30
FFN_HIDDEN = -(-8 * D_MODEL // (3 * 256)) * 256

kernel_name = "yoco_gmlp_chunked_relbias_attention_trunk"


def _rms_norm(x, g):
    xf = x.astype(jnp.float32)
    y = xf * lax.rsqrt(jnp.mean(xf * xf, axis=-1, keepdims=True) + EPS)
    return (y * g.astype(jnp.float32)).astype(x.dtype)


def _spatial_mask():
    pos = jnp.arange(A_CHUNK) // CHUNK
    return pos[:, None] >= pos[None, :]


def _gmlp_mixer(x, g_norm, w_in, g_sgu, w_s, b_s, w_out):
    b, s, _ = x.shape
    h = _rms_norm(x, g_norm)
    z = jax.nn.gelu(h @ w_in, approximate=False)
    u, v = jnp.split(z, 2, axis=-1)
    v = _rms_norm(v, g_sgu)
    v = v.reshape(b, s // A_CHUNK, A_CHUNK, A_GROUPS, A_GROUP_DIM)
    w = w_s * _spatial_mask().astype(w_s.dtype)[None]
    v = jnp.einsum('gij,bnjgc->bnigc', w, v) + b_s.T[None, None, :, :, None]
    v = v.reshape(b, s, A_DFF)
    return (u * v) @ w_out


def _swiglu_ffn(x, g_norm, w_gate_up, w_down):
    h = _rms_norm(x, g_norm)
    gate, up = jnp.split(h @ w_gate_up, 2, axis=-1)
    return (jax.nn.silu(gate) * up) @ w_down


def _shared_kv(x, g_norm, w_kv):
    b, s, _ = x.shape
    h = _rms_norm(x, g_norm)
    k, v = jnp.split(h @ w_kv, 2, axis=-1)
    k = k.reshape(b, s, B_HEADS, B_HEAD_DIM)
    v = v.reshape(b, s, B_HEADS, B_HEAD_DIM)
    pad = ((0, 0), (N_LEFT_CHUNKS * CHUNK, 0), (0, 0), (0, 0))
    return jnp.pad(k, pad), jnp.pad(v, pad)


def _rel_index():
    qi = jnp.arange(CHUNK)[:, None]
    kj = jnp.arange(BAND)[None, :] - N_LEFT_CHUNKS * CHUNK
    return jnp.clip(qi - kj, -MAX_REL, MAX_REL) + MAX_REL


def _chunked_relbias_attention(x, g_norm, w_q, rel_table, w_o, k_pad, v_pad):
    b, s, _ = x.shape
    n_chunks = s // CHUNK
    q = (_rms_norm(x, g_norm) @ w_q).reshape(b, n_chunks, CHUNK, B_HEADS, B_HEAD_DIM)
    q = jnp.moveaxis(q, 1, 0)
    bias = rel_table[:, _rel_index()].astype(jnp.float32)
    key_offset = jnp.arange(BAND) - N_LEFT_CHUNKS * CHUNK

    def attend(args):
        c, q_c = args
        k_c = lax.dynamic_slice_in_dim(k_pad, c * CHUNK, BAND, axis=1)
        v_c = lax.dynamic_slice_in_dim(v_pad, c * CHUNK, BAND, axis=1)
        sc = jnp.einsum('bqhd,bkhd->bhqk', q_c, k_c).astype(jnp.float32) * ATTN_SCALE + bias
        valid = (c * CHUNK + key_offset) >= 0
        sc = jnp.where(valid, sc, NEG_INF)
        p = jax.nn.softmax(sc, axis=-1).astype(v_c.dtype)
        return jnp.einsum('bhqk,bkhd->bqhd', p, v_c)

    o = lax.map(attend, (jnp.arange(n_chunks), q))
    o = jnp.moveaxis(o, 0, 1).reshape(b, s, D_MODEL)
    return o @ w_o


def _normal(key, shape, scale):
    return jax.random.normal(key, shape, jnp.float32) * scale


def _fwd_setup_inputs(seed: int = 0) -> dict:
    key = jax.random.key(seed)
    ks = jax.random.split(key, 18)
    return {
        "x": _normal(ks[0], (BATCH, SEQ, D_MODEL), 1.0),
        "a_norm": 1.0 + _normal(ks[1], (N_A, D_MODEL), 0.02),
        "a_w_in": _normal(ks[2], (N_A, D_MODEL, 2 * A_DFF), D_MODEL ** -0.5),
        "a_sgu_norm": 1.0 + _normal(ks[3], (N_A, A_DFF), 0.02),
        "a_w_spatial": _normal(ks[4], (N_A, A_GROUPS, A_CHUNK, A_CHUNK), A_CHUNK ** -0.5),
        "a_b_spatial": 1.0 + _normal(ks[5], (N_A, A_GROUPS, A_CHUNK), 0.1),
        "a_w_out": _normal(ks[6], (N_A, A_DFF, D_MODEL), A_DFF ** -0.5),
        "kv_norm": 1.0 + _normal(ks[7], (D_MODEL,), 0.02),
        "w_kv": _normal(ks[8], (D_MODEL, 2 * D_MODEL), D_MODEL ** -0.5),
        "b_norm": 1.0 + _normal(ks[9], (N_B, D_MODEL), 0.02),
        "b_w_q": _normal(ks[10], (N_B, D_MODEL, D_MODEL), D_MODEL ** -0.5),
        "b_rel_bias": _normal(ks[11], (N_B, B_HEADS, 2 * MAX_REL + 1), 0.1),
        "b_w_o": _normal(ks[12], (N_B, D_MODEL, D_MODEL), D_MODEL ** -0.5),
        "ffn_norm": 1.0 + _normal(ks[13], (DEPTH, D_MODEL), 0.02),
        "ffn_w_gate_up": _normal(ks[14], (DEPTH, D_MODEL, 2 * FFN_HIDDEN), D_MODEL ** -0.5),
        "ffn_w_down": _normal(ks[15], (DEPTH, FFN_HIDDEN, D_MODEL), FFN_HIDDEN ** -0.5),
        "final_norm": 1.0 + _normal(ks[16], (D_MODEL,), 0.02),
    }


def _fwd_reference(x, a_norm, a_w_in, a_sgu_norm, a_w_spatial, a_b_spatial, a_w_out,
              kv_norm, w_kv, b_norm, b_w_q, b_rel_bias, b_w_o,
              ffn_norm, ffn_w_gate_up, ffn_w_down, final_norm):
    k_pad = v_pad = None
    for layer in range(DEPTH):
        if layer < N_A:
            i = layer
            x = x + _gmlp_mixer(x, a_norm[i], a_w_in[i], a_sgu_norm[i],
                                a_w_spatial[i], a_b_spatial[i], a_w_out[i])
        else:
            if layer == N_A:
                k_pad, v_pad = _shared_kv(x, kv_norm, w_kv)
            i = layer - N_A
            x = x + _chunked_relbias_attention(x, b_norm[i], b_w_q[i], b_rel_bias[i],
                                               b_w_o[i], k_pad, v_pad)
        x = x + _swiglu_ffn(x, ffn_norm[layer], ffn_w_gate_up[layer], ffn_w_down[layer])
    return _rms_norm(x, final_norm)


import jax as _jax
import jax.numpy as _jnp

TWIN_FORMAT = 'train_step'
FWD_PARAMS = ['x', 'a_norm', 'a_w_in', 'a_sgu_norm', 'a_w_spatial', 'a_b_spatial', 'a_w_out', 'kv_norm', 'w_kv', 'b_norm', 'b_w_q', 'b_rel_bias', 'b_w_o', 'ffn_norm', 'ffn_w_gate_up', 'ffn_w_down', 'final_norm']
TWIN_WEIGHTS = ['a_norm', 'a_w_in', 'a_sgu_norm', 'a_w_spatial', 'a_b_spatial', 'a_w_out', 'kv_norm', 'w_kv', 'b_norm', 'b_w_q', 'b_rel_bias', 'b_w_o', 'ffn_norm', 'ffn_w_gate_up', 'ffn_w_down', 'final_norm']
TWIN_DIFF_INPUT = 'x'
TWIN_INPUTS = ['x', 'a_norm', 'a_w_in', 'a_sgu_norm', 'a_w_spatial', 'a_b_spatial', 'a_w_out', 'kv_norm', 'w_kv', 'b_norm', 'b_w_q', 'b_rel_bias', 'b_w_o', 'ffn_norm', 'ffn_w_gate_up', 'ffn_w_down', 'final_norm', 'loss_target', 'm_a_norm', 'm_a_w_in', 'm_a_sgu_norm', 'm_a_w_spatial', 'm_a_b_spatial', 'm_a_w_out', 'm_kv_norm', 'm_w_kv', 'm_b_norm', 'm_b_w_q', 'm_b_rel_bias', 'm_b_w_o', 'm_ffn_norm', 'm_ffn_w_gate_up', 'm_ffn_w_down', 'm_final_norm', 'v_a_norm', 'v_a_w_in', 'v_a_sgu_norm', 'v_a_w_spatial', 'v_a_b_spatial', 'v_a_w_out', 'v_kv_norm', 'v_w_kv', 'v_b_norm', 'v_b_w_q', 'v_b_rel_bias', 'v_b_w_o', 'v_ffn_norm', 'v_ffn_w_gate_up', 'v_ffn_w_down', 'v_final_norm']
TWIN_OUTPUTS = ['loss', 'grad_x', 'grad_a_norm', 'grad_a_w_in', 'grad_a_sgu_norm', 'grad_a_w_spatial', 'grad_a_b_spatial', 'grad_a_w_out', 'grad_kv_norm', 'grad_w_kv', 'grad_b_norm', 'grad_b_w_q', 'grad_b_rel_bias', 'grad_b_w_o', 'grad_ffn_norm', 'grad_ffn_w_gate_up', 'grad_ffn_w_down', 'grad_final_norm', 'delta_a_norm', 'delta_a_w_in', 'delta_a_sgu_norm', 'delta_a_w_spatial', 'delta_a_b_spatial', 'delta_a_w_out', 'delta_kv_norm', 'delta_w_kv', 'delta_b_norm', 'delta_b_w_q', 'delta_b_rel_bias', 'delta_b_w_o', 'delta_ffn_norm', 'delta_ffn_w_gate_up', 'delta_ffn_w_down', 'delta_final_norm', 'new_m_a_norm', 'new_m_a_w_in', 'new_m_a_sgu_norm', 'new_m_a_w_spatial', 'new_m_a_b_spatial', 'new_m_a_w_out', 'new_m_kv_norm', 'new_m_w_kv', 'new_m_b_norm', 'new_m_b_w_q', 'new_m_b_rel_bias', 'new_m_b_w_o', 'new_m_ffn_norm', 'new_m_ffn_w_gate_up', 'new_m_ffn_w_down', 'new_m_final_norm', 'new_v_a_norm', 'new_v_a_w_in', 'new_v_a_sgu_norm', 'new_v_a_w_spatial', 'new_v_a_b_spatial', 'new_v_a_w_out', 'new_v_kv_norm', 'new_v_w_kv', 'new_v_b_norm', 'new_v_b_w_q', 'new_v_b_rel_bias', 'new_v_b_w_o', 'new_v_ffn_norm', 'new_v_ffn_w_gate_up', 'new_v_ffn_w_down', 'new_v_final_norm']
TWIN_LEAF_KINDS = {'loss': 'loss', 'grad_x': 'grad_x', 'grad_a_norm': 'grad_w', 'grad_a_w_in': 'grad_w', 'grad_a_sgu_norm': 'grad_w', 'grad_a_w_spatial': 'grad_w', 'grad_a_b_spatial': 'grad_w', 'grad_a_w_out': 'grad_w', 'grad_kv_norm': 'grad_w', 'grad_w_kv': 'grad_w', 'grad_b_norm': 'grad_w', 'grad_b_w_q': 'grad_w', 'grad_b_rel_bias': 'grad_w', 'grad_b_w_o': 'grad_w', 'grad_ffn_norm': 'grad_w', 'grad_ffn_w_gate_up': 'grad_w', 'grad_ffn_w_down': 'grad_w', 'grad_final_norm': 'grad_w', 'delta_a_norm': 'delta_w', 'delta_a_w_in': 'delta_w', 'delta_a_sgu_norm': 'delta_w', 'delta_a_w_spatial': 'delta_w', 'delta_a_b_spatial': 'delta_w', 'delta_a_w_out': 'delta_w', 'delta_kv_norm': 'delta_w', 'delta_w_kv': 'delta_w', 'delta_b_norm': 'delta_w', 'delta_b_w_q': 'delta_w', 'delta_b_rel_bias': 'delta_w', 'delta_b_w_o': 'delta_w', 'delta_ffn_norm': 'delta_w', 'delta_ffn_w_gate_up': 'delta_w', 'delta_ffn_w_down': 'delta_w', 'delta_final_norm': 'delta_w', 'new_m_a_norm': 'new_m', 'new_m_a_w_in': 'new_m', 'new_m_a_sgu_norm': 'new_m', 'new_m_a_w_spatial': 'new_m', 'new_m_a_b_spatial': 'new_m', 'new_m_a_w_out': 'new_m', 'new_m_kv_norm': 'new_m', 'new_m_w_kv': 'new_m', 'new_m_b_norm': 'new_m', 'new_m_b_w_q': 'new_m', 'new_m_b_rel_bias': 'new_m', 'new_m_b_w_o': 'new_m', 'new_m_ffn_norm': 'new_m', 'new_m_ffn_w_gate_up': 'new_m', 'new_m_ffn_w_down': 'new_m', 'new_m_final_norm': 'new_m', 'new_v_a_norm': 'new_v', 'new_v_a_w_in': 'new_v', 'new_v_a_sgu_norm': 'new_v', 'new_v_a_w_spatial': 'new_v', 'new_v_a_b_spatial': 'new_v', 'new_v_a_w_out': 'new_v', 'new_v_kv_norm': 'new_v', 'new_v_w_kv': 'new_v', 'new_v_b_norm': 'new_v', 'new_v_b_w_q': 'new_v', 'new_v_b_rel_bias': 'new_v', 'new_v_b_w_o': 'new_v', 'new_v_ffn_norm': 'new_v', 'new_v_ffn_w_gate_up': 'new_v', 'new_v_ffn_w_down': 'new_v', 'new_v_final_norm': 'new_v'}


def _forward(args):
    return _fwd_reference(*[args[k] for k in FWD_PARAMS])


def _output_shape():
    def fwd():
        inp = _fwd_setup_inputs(0)
        return _fwd_reference(*[inp[k] for k in FWD_PARAMS])
    out = _jax.eval_shape(fwd)
    return out.shape, out.dtype

N_MICROBATCH = 1
ADAM_LR = 0.001
ADAM_B1 = 0.9
ADAM_B2 = 0.999
ADAM_EPS = 1e-08
ADAM_WD = 0.01
ADAM_STEP = 10
PER_EXAMPLE_BATCH_AXIS = {'x': 0, 'loss_target': 0}
SHARED_INPUTS = []
_WEIGHT_DTYPES = {'a_norm': _jnp.float32, 'a_w_in': _jnp.float32, 'a_sgu_norm': _jnp.float32, 'a_w_spatial': _jnp.float32, 'a_b_spatial': _jnp.float32, 'a_w_out': _jnp.float32, 'kv_norm': _jnp.float32, 'w_kv': _jnp.float32, 'b_norm': _jnp.float32, 'b_w_q': _jnp.float32, 'b_rel_bias': _jnp.float32, 'b_w_o': _jnp.float32, 'ffn_norm': _jnp.float32, 'ffn_w_gate_up': _jnp.float32, 'ffn_w_down': _jnp.float32, 'final_norm': _jnp.float32}
MOMENT_SCALE = {'a_norm': 1.702454e-01, 'a_w_in': 8.149066e-02, 'a_sgu_norm': 5.950526e-02, 'a_w_spatial': 8.537104e-02, 'a_b_spatial': 9.954682e-02, 'a_w_out': 1.414443e-01, 'kv_norm': 4.680554e-02, 'w_kv': 3.322576e-02, 'b_norm': 1.331163e-02, 'b_w_q': 1.343565e-02, 'b_rel_bias': 4.723575e-03, 'b_w_o': 3.445440e-02, 'ffn_norm': 9.906821e-02, 'ffn_w_gate_up': 4.161678e-02, 'ffn_w_down': 6.819876e-02, 'final_norm': 3.218795e+01}


def _to_microbatches(a, axis):
    t = _jnp.moveaxis(a, axis, 0)
    t = t.reshape((N_MICROBATCH, t.shape[0] // N_MICROBATCH) + t.shape[1:])
    return _jnp.moveaxis(t, 1, axis + 1)


def setup_inputs(seed: int = 0) -> dict:
    inp = _fwd_setup_inputs(seed)
    key = _jax.random.fold_in(_jax.random.key(seed), 7919)
    shape, _ = _output_shape()
    out = dict(inp)
    out["loss_target"] = _jax.random.normal(_jax.random.fold_in(key, 0), shape, _jnp.float32)
    for i, name in enumerate(TWIN_WEIGHTS):
        w = inp[name].astype(_jnp.float32)
        if MOMENT_SCALE is None:
            s = _jnp.sqrt(_jnp.mean(_jnp.square(w)) + 1e-30)
        else:
            s = MOMENT_SCALE[name]
        km, kv = _jax.random.split(_jax.random.fold_in(key, i + 1))
        out[name] = w
        out["m_" + name] = s * _jax.random.normal(km, w.shape, _jnp.float32)
        out["v_" + name] = (s * s) * _jax.random.uniform(kv, w.shape, _jnp.float32, 0.5, 1.5)
    if N_MICROBATCH > 1:
        for name, axis in PER_EXAMPLE_BATCH_AXIS.items():
            out[name] = _to_microbatches(out[name], axis)
    return {'x': out['x'], 'a_norm': out['a_norm'], 'a_w_in': out['a_w_in'], 'a_sgu_norm': out['a_sgu_norm'], 'a_w_spatial': out['a_w_spatial'], 'a_b_spatial': out['a_b_spatial'], 'a_w_out': out['a_w_out'], 'kv_norm': out['kv_norm'], 'w_kv': out['w_kv'], 'b_norm': out['b_norm'], 'b_w_q': out['b_w_q'], 'b_rel_bias': out['b_rel_bias'], 'b_w_o': out['b_w_o'], 'ffn_norm': out['ffn_norm'], 'ffn_w_gate_up': out['ffn_w_gate_up'], 'ffn_w_down': out['ffn_w_down'], 'final_norm': out['final_norm'], 'loss_target': out['loss_target'], 'm_a_norm': out['m_a_norm'], 'm_a_w_in': out['m_a_w_in'], 'm_a_sgu_norm': out['m_a_sgu_norm'], 'm_a_w_spatial': out['m_a_w_spatial'], 'm_a_b_spatial': out['m_a_b_spatial'], 'm_a_w_out': out['m_a_w_out'], 'm_kv_norm': out['m_kv_norm'], 'm_w_kv': out['m_w_kv'], 'm_b_norm': out['m_b_norm'], 'm_b_w_q': out['m_b_w_q'], 'm_b_rel_bias': out['m_b_rel_bias'], 'm_b_w_o': out['m_b_w_o'], 'm_ffn_norm': out['m_ffn_norm'], 'm_ffn_w_gate_up': out['m_ffn_w_gate_up'], 'm_ffn_w_down': out['m_ffn_w_down'], 'm_final_norm': out['m_final_norm'], 'v_a_norm': out['v_a_norm'], 'v_a_w_in': out['v_a_w_in'], 'v_a_sgu_norm': out['v_a_sgu_norm'], 'v_a_w_spatial': out['v_a_w_spatial'], 'v_a_b_spatial': out['v_a_b_spatial'], 'v_a_w_out': out['v_a_w_out'], 'v_kv_norm': out['v_kv_norm'], 'v_w_kv': out['v_w_kv'], 'v_b_norm': out['v_b_norm'], 'v_b_w_q': out['v_b_w_q'], 'v_b_rel_bias': out['v_b_rel_bias'], 'v_b_w_o': out['v_b_w_o'], 'v_ffn_norm': out['v_ffn_norm'], 'v_ffn_w_gate_up': out['v_ffn_w_gate_up'], 'v_ffn_w_down': out['v_ffn_w_down'], 'v_final_norm': out['v_final_norm']}


def _loss(weights, diff, rest, loss_target):
    with _jax.named_scope("forward"):
        args = {**rest, TWIN_DIFF_INPUT: diff, **{k: w.astype(_WEIGHT_DTYPES[k]) for k, w in weights.items()}}
        y = _forward(args)
    with _jax.named_scope("loss_head"):
        err = _jnp.square(y.astype(_jnp.float32) - loss_target)
        return 0.5 * _jnp.sum(_jnp.mean(err, axis=-1)) if err.ndim else 0.5 * err


def _adamw(w, g, m, v):
    m = ADAM_B1 * m + (1.0 - ADAM_B1) * g
    v = ADAM_B2 * v + (1.0 - ADAM_B2) * _jnp.square(g)
    m_hat = m / (1.0 - ADAM_B1 ** ADAM_STEP)
    v_hat = v / (1.0 - ADAM_B2 ** ADAM_STEP)
    delta = -ADAM_LR * (m_hat / (_jnp.sqrt(v_hat) + ADAM_EPS) + ADAM_WD * w)
    return delta, m, v


def reference(x, a_norm, a_w_in, a_sgu_norm, a_w_spatial, a_b_spatial, a_w_out, kv_norm, w_kv, b_norm, b_w_q, b_rel_bias, b_w_o, ffn_norm, ffn_w_gate_up, ffn_w_down, final_norm, loss_target, m_a_norm, m_a_w_in, m_a_sgu_norm, m_a_w_spatial, m_a_b_spatial, m_a_w_out, m_kv_norm, m_w_kv, m_b_norm, m_b_w_q, m_b_rel_bias, m_b_w_o, m_ffn_norm, m_ffn_w_gate_up, m_ffn_w_down, m_final_norm, v_a_norm, v_a_w_in, v_a_sgu_norm, v_a_w_spatial, v_a_b_spatial, v_a_w_out, v_kv_norm, v_w_kv, v_b_norm, v_b_w_q, v_b_rel_bias, v_b_w_o, v_ffn_norm, v_ffn_w_gate_up, v_ffn_w_down, v_final_norm):
    given = dict(x=x, a_norm=a_norm, a_w_in=a_w_in, a_sgu_norm=a_sgu_norm, a_w_spatial=a_w_spatial, a_b_spatial=a_b_spatial, a_w_out=a_w_out, kv_norm=kv_norm, w_kv=w_kv, b_norm=b_norm, b_w_q=b_w_q, b_rel_bias=b_rel_bias, b_w_o=b_w_o, ffn_norm=ffn_norm, ffn_w_gate_up=ffn_w_gate_up, ffn_w_down=ffn_w_down, final_norm=final_norm, loss_target=loss_target, m_a_norm=m_a_norm, m_a_w_in=m_a_w_in, m_a_sgu_norm=m_a_sgu_norm, m_a_w_spatial=m_a_w_spatial, m_a_b_spatial=m_a_b_spatial, m_a_w_out=m_a_w_out, m_kv_norm=m_kv_norm, m_w_kv=m_w_kv, m_b_norm=m_b_norm, m_b_w_q=m_b_w_q, m_b_rel_bias=m_b_rel_bias, m_b_w_o=m_b_w_o, m_ffn_norm=m_ffn_norm, m_ffn_w_gate_up=m_ffn_w_gate_up, m_ffn_w_down=m_ffn_w_down, m_final_norm=m_final_norm, v_a_norm=v_a_norm, v_a_w_in=v_a_w_in, v_a_sgu_norm=v_a_sgu_norm, v_a_w_spatial=v_a_w_spatial, v_a_b_spatial=v_a_b_spatial, v_a_w_out=v_a_w_out, v_kv_norm=v_kv_norm, v_w_kv=v_w_kv, v_b_norm=v_b_norm, v_b_w_q=v_b_w_q, v_b_rel_bias=v_b_rel_bias, v_b_w_o=v_b_w_o, v_ffn_norm=v_ffn_norm, v_ffn_w_gate_up=v_ffn_w_gate_up, v_ffn_w_down=v_ffn_w_down, v_final_norm=v_final_norm)
    weights = {n: given[n] for n in TWIN_WEIGHTS}
    shared = {n: given[n] for n in SHARED_INPUTS}
    per_example = {n: given[n] for n in ['x']}
    grad_fn = _jax.value_and_grad(_loss, argnums=(0, 1))

    def one_microbatch(ex, loss_target):
        ex = dict(ex)
        diff = ex.pop(TWIN_DIFF_INPUT)
        return grad_fn(weights, diff, {**shared, **ex}, loss_target)

    if N_MICROBATCH == 1:
        loss, (grad_w, grad_x) = one_microbatch(per_example, given["loss_target"])
    else:
        def body(carry, xs):
            loss_sum, grad_sum = carry
            l_k, (gw_k, gx_k) = one_microbatch(xs[0], xs[1])
            with _jax.named_scope("update"):
                return (loss_sum + l_k, _jax.tree.map(_jnp.add, grad_sum, gw_k)), gx_k

        init = (_jnp.zeros((), _jnp.float32), _jax.tree.map(_jnp.zeros_like, weights))
        (loss, grad_w), grad_x = _jax.lax.scan(body, init, (per_example, given["loss_target"]))
    with _jax.named_scope("update"):
        delta_w, new_m, new_v = {}, {}, {}
        for n in TWIN_WEIGHTS:
            delta_w[n], new_m[n], new_v[n] = _adamw(weights[n], grad_w[n], given["m_" + n], given["v_" + n])
    return (loss, grad_x, *[grad_w[n] for n in TWIN_WEIGHTS], *[delta_w[n] for n in TWIN_WEIGHTS],
            *[new_m[n] for n in TWIN_WEIGHTS], *[new_v[n] for n in TWIN_WEIGHTS])
```

```python
import math

import jax
import jax.numpy as jnp
from jax import lax
from jax.experimental import pallas as pl
from jax.experimental.pallas import tpu as pltpu

F32 = jnp.float32
BF16 = jnp.bfloat16
MESH = pl.DeviceIdType.MESH
HBM_SPEC = pl.BlockSpec(memory_space=pltpu.HBM)

N_DEV = 8
CHUNK = 64
A_CHUNK = 128
A_GROUPS = 8
N_LEFT_CHUNKS = 8
KEYS_SEEN = (N_LEFT_CHUNKS + 1) * CHUNK
BAND = 640
SKIP = BAND - KEYS_SEEN
LEFT = BAND - CHUNK
MAX_REL = 256
N_REL = 2 * MAX_REL + 1
REL_PAD = 640
HEAD_DIM = 64
HEAD_PAIR = 2 * HEAD_DIM
ATTN_SCALE = HEAD_DIM ** -0.5
EPS = 1e-6
NEG_INF = -1e30
ADAM_LR = 0.001
ADAM_B1 = 0.9
ADAM_B2 = 0.999
ADAM_EPS = 1e-08
ADAM_WD = 0.01
ADAM_STEP = 10
FLAT_LANES = 1024
V7X_VMEM_BYTES = 64 * 1024 * 1024
VMEM_FLOOR_BYTES = 32 * 1024 * 1024
VMEM_CEIL_BYTES = V7X_VMEM_BYTES - 8 * 1024 * 1024

NN = (((1,), (0,)), ((), ()))
NT = (((1,), (1,)), ((), ()))
TN = (((0,), (0,)), ((), ()))


def _tile(n, pref):
    return pref if n % pref == 0 else n


def _row_tile(n, pref, mult):
    best = None
    for t in range(mult, min(n, pref) + 1, mult):
        if n % t == 0:
            best = t
    return best if best is not None else n


def _nbytes(shape, dtype):
    n = 1
    for s in shape:
        if s is not None:
            n *= s
    return n * jnp.dtype(dtype).itemsize


def _call(body, name, grid, in_specs, out_specs, out_shape, scratch=(), vmem_bytes=0, aliases=None):
    limit = int(min(max(VMEM_FLOOR_BYTES, vmem_bytes * 5 // 4), VMEM_CEIL_BYTES))
    return pl.pallas_call(
        body,
        name=name,
        grid=grid,
        in_specs=in_specs,
        out_specs=out_specs,
        out_shape=out_shape,
        scratch_shapes=list(scratch),
        input_output_aliases=aliases or {},
        compiler_params=pltpu.CompilerParams(
            dimension_semantics=("arbitrary",) * len(grid), vmem_limit_bytes=limit),
    )


def _erf_parts(x):
    ax = jnp.abs(x) * (1.0 / math.sqrt(2.0))
    t = 1.0 / (1.0 + 0.3275911 * ax)
    poly = ((((1.061405429 * t - 1.453152027) * t + 1.421413741) * t - 0.284496736) * t + 0.254829592) * t
    ex = jnp.exp(-ax * ax)
    erf_abs = 1.0 - poly * ex
    return jnp.where(x < 0, -erf_abs, erf_abs), ex


def _gelu_and_grad(x):
    erf, ex = _erf_parts(x)
    cdf = 0.5 * (1.0 + erf)
    return x * cdf, cdf + x * ex * (1.0 / math.sqrt(2.0 * math.pi))


def _gelu(x):
    erf, _ = _erf_parts(x)
    return x * (0.5 * (1.0 + erf))


def _sigmoid(x):
    return 1.0 / (1.0 + jnp.exp(-x))


def _split3(x):
    hi = x.astype(BF16)
    r1 = x - hi.astype(F32)
    mid = r1.astype(BF16)
    lo = (r1 - mid.astype(F32)).astype(BF16)
    return hi, mid, lo


def _rms_fwd(x, g, name):
    t, d = x.shape
    tm = _tile(t, 512)

    def body(x_ref, g_ref, o_ref):
        xf = x_ref[...]
        r = lax.rsqrt(jnp.mean(xf * xf, axis=-1, keepdims=True) + EPS)
        o_ref[...] = (xf * r * g_ref[...]).astype(o_ref.dtype)

    return _call(
        body, name, (t // tm,),
        [pl.BlockSpec((tm, d), lambda i: (i, 0)), pl.BlockSpec((1, d), lambda i: (0, 0))],
        pl.BlockSpec((tm, d), lambda i: (i, 0)),
        jax.ShapeDtypeStruct((t, d), BF16),
        vmem_bytes=2 * (_nbytes((tm, d), F32) + _nbytes((tm, d), BF16)) + 4 * _nbytes((tm, d), F32),
    )(x, g.reshape(1, d))


def _rms_bwd(x, g, dh, dx_up, name):
    t, d = x.shape
    tm = _tile(t, 512)

    def body(x_ref, g_ref, dh_ref, up_ref, dx_ref, dg_ref):
        @pl.when(pl.program_id(0) == 0)
        def _():
            dg_ref[...] = jnp.zeros_like(dg_ref)

        xf = x_ref[...]
        r = lax.rsqrt(jnp.mean(xf * xf, axis=-1, keepdims=True) + EPS)
        xhat = xf * r
        dy = dh_ref[...].astype(F32)
        dxhat = dy * g_ref[...]
        dg_ref[...] += jnp.sum(dy * xhat, axis=0, keepdims=True)
        dx = r * (dxhat - xhat * jnp.mean(dxhat * xhat, axis=-1, keepdims=True))
        dx_ref[...] = up_ref[...] + dx

    row = pl.BlockSpec((tm, d), lambda i: (i, 0))
    vec = pl.BlockSpec((1, d), lambda i: (0, 0))
    dx, dg = _call(
        body, name, (t // tm,),
        [row, vec, row, row],
        [row, vec],
        [jax.ShapeDtypeStruct((t, d), F32), jax.ShapeDtypeStruct((1, d), F32)],
        vmem_bytes=10 * _nbytes((tm, d), F32),
    )(x, g.reshape(1, d), dh, dx_up)
    return dx, dg.reshape(d)


def _mm(name, dims, a, b, *, grid, a_spec, b_spec, out_shape, out_spec, acc_shape,
        res=None, res_spec=None, alias=None, scale=None):
    nk = grid[2]
    has_res = res is not None
    has_alias = alias is not None

    def body(*refs):
        refs = list(refs)
        a_ref = refs.pop(0)
        b_ref = refs.pop(0)
        r_ref = refs.pop(0) if has_res else None
        if has_alias:
            refs.pop(0)
        o_ref = refs.pop(0)
        part = lax.dot_general(a_ref[...].astype(BF16), b_ref[...].astype(BF16), dims,
                               preferred_element_type=F32)

        def finish(acc):
            if scale is not None:
                acc = acc * scale
            if has_res:
                acc = acc + r_ref[...]
            o_ref[...] = acc.astype(o_ref.dtype)

        if nk == 1:
            finish(part)
        else:
            acc_ref = refs.pop(0)
            k = pl.program_id(2)

            @pl.when(k == 0)
            def _():
                acc_ref[...] = part

            @pl.when(k > 0)
            def _():
                acc_ref[...] += part

            @pl.when(k == nk - 1)
            def _():
                finish(acc_ref[...])

    operands = [a, b]
    in_specs = [a_spec, b_spec]
    vmem = 2 * (_nbytes(a_spec.block_shape, a.dtype) + _nbytes(b_spec.block_shape, b.dtype)
                + _nbytes(out_spec.block_shape, out_shape.dtype))
    vmem += 3 * _nbytes(acc_shape, F32)
    if has_res:
        operands.append(res)
        in_specs.append(res_spec)
        vmem += 2 * _nbytes(res_spec.block_shape, res.dtype)
    aliases = None
    if has_alias:
        aliases = {len(operands): 0}
        operands.append(alias)
        in_specs.append(pl.BlockSpec(memory_space=pl.ANY))
    scratch = [pltpu.VMEM(acc_shape, F32)] if nk > 1 else []
    return _call(body, name, grid, in_specs, out_spec, out_shape, scratch=scratch,
                 vmem_bytes=vmem, aliases=aliases)(*operands)


def _mm_colblock(name, h, w_g, layer, out_dtype=BF16, blocked_out=False):
    t, k = h.shape
    nb = w_g.shape[3]
    tm = _tile(t, 2048)
    if blocked_out:
        out_shape = jax.ShapeDtypeStruct((N_DEV, t, nb), out_dtype)
        out_spec = pl.BlockSpec((None, tm, nb), lambda i, j, kk: (j, i, 0))
    else:
        out_shape = jax.ShapeDtypeStruct((t, N_DEV * nb), out_dtype)
        out_spec = pl.BlockSpec((tm, nb), lambda i, j, kk: (i, j))
    return _mm(
        name, NN, h, w_g, grid=(t // tm, N_DEV, 1),
        a_spec=pl.BlockSpec((tm, k), lambda i, j, kk: (i, 0)),
        b_spec=pl.BlockSpec((None, None, k, nb), lambda i, j, kk: (layer, j, 0, 0)),
        out_shape=out_shape, out_spec=out_spec, acc_shape=(tm, nb))


def _mm_natural(name, a, w, layer, *, res=None, out_dtype=F32, scale=None):
    t, k = a.shape
    n = w.shape[2]
    tm = _tile(t, 1024)
    tn = _tile(n, 512)
    res_spec = None if res is None else pl.BlockSpec((tm, tn), lambda i, j, kk: (i, j))
    return _mm(
        name, NN, a, w, grid=(t // tm, n // tn, 1),
        a_spec=pl.BlockSpec((tm, k), lambda i, j, kk: (i, 0)),
        b_spec=pl.BlockSpec((None, k, tn), lambda i, j, kk: (layer, 0, j)),
        out_shape=jax.ShapeDtypeStruct((t, n), out_dtype),
        out_spec=pl.BlockSpec((tm, tn), lambda i, j, kk: (i, j)),
        acc_shape=(tm, tn), res=res, res_spec=res_spec, scale=scale)


def _mm_down(name, act, w4, layer, res):
    nblk, t, kb = act.shape
    n = w4.shape[3]
    tm = _tile(t, 1024)
    tn = _tile(n, 1024)
    return _mm(
        name, NN, act, w4, grid=(t // tm, n // tn, nblk),
        a_spec=pl.BlockSpec((None, tm, kb), lambda i, j, kk: (kk, i, 0)),
        b_spec=pl.BlockSpec((None, None, kb, tn), lambda i, j, kk: (layer, kk, 0, j)),
        out_shape=jax.ShapeDtypeStruct((t, n), F32),
        out_spec=pl.BlockSpec((tm, tn), lambda i, j, kk: (i, j)),
        acc_shape=(tm, tn), res=res, res_spec=pl.BlockSpec((tm, tn), lambda i, j, kk: (i, j)))


def _mm_t_colblock(name, dz, w_g, layer, blocked_in=False):
    k = w_g.shape[2]
    nb = w_g.shape[3]
    t = dz.shape[1] if blocked_in else dz.shape[0]
    tm = _tile(t, 2048)
    if blocked_in:
        a_spec = pl.BlockSpec((None, tm, nb), lambda i, j, kk: (kk, i, 0))
    else:
        a_spec = pl.BlockSpec((tm, nb), lambda i, j, kk: (i, kk))
    return _mm(
        name, NT, dz, w_g, grid=(t // tm, 1, N_DEV),
        a_spec=a_spec,
        b_spec=pl.BlockSpec((None, None, k, nb), lambda i, j, kk: (layer, kk, 0, 0)),
        out_shape=jax.ShapeDtypeStruct((t, k), BF16),
        out_spec=pl.BlockSpec((tm, k), lambda i, j, kk: (i, 0)),
        acc_shape=(tm, k))


def _mm_t_natural(name, dy, w, layer):
    t, n = dy.shape
    k = w.shape[1]
    tm = _tile(t, 1024)
    tk = _tile(k, 512)
    return _mm(
        name, NT, dy, w, grid=(t // tm, k // tk, 1),
        a_spec=pl.BlockSpec((tm, n), lambda i, j, kk: (i, 0)),
        b_spec=pl.BlockSpec((None, tk, n), lambda i, j, kk: (layer, j, 0)),
        out_shape=jax.ShapeDtypeStruct((t, k), BF16),
        out_spec=pl.BlockSpec((tm, tk), lambda i, j, kk: (i, j)),
        acc_shape=(tm, tk))


def _mm_t_down(name, dy, w4, layer):
    t, n = dy.shape
    nblk, kb = w4.shape[1], w4.shape[2]
    tm = _tile(t, 2048)
    return _mm(
        name, NT, dy, w4, grid=(t // tm, nblk, 1),
        a_spec=pl.BlockSpec((tm, n), lambda i, j, kk: (i, 0)),
        b_spec=pl.BlockSpec((None, None, kb, n), lambda i, j, kk: (layer, j, 0, 0)),
        out_shape=jax.ShapeDtypeStruct((nblk, t, kb), BF16),
        out_spec=pl.BlockSpec((None, tm, kb), lambda i, j, kk: (j, i, 0)),
        acc_shape=(tm, kb))


def _grad_buf(buf, shape):
    return jax.ShapeDtypeStruct(shape, BF16) if buf is None else jax.ShapeDtypeStruct(buf.shape, buf.dtype)


def _mm_dw_colblock(name, h, dz, buf, layer, n_layers, blocked_in=False):
    t, k = h.shape
    nb = dz.shape[2] if blocked_in else dz.shape[1] // N_DEV
    tk = _tile(t, 1024)
    if blocked_in:
        b_spec = pl.BlockSpec((None, tk, nb), lambda i, j, kk: (j, kk, 0))
    else:
        b_spec = pl.BlockSpec((tk, nb), lambda i, j, kk: (kk, j))
    return _mm(
        name, TN, h, dz, grid=(1, N_DEV, t // tk),
        a_spec=pl.BlockSpec((tk, k), lambda i, j, kk: (kk, 0)),
        b_spec=b_spec,
        out_shape=_grad_buf(buf, (n_layers, N_DEV, k, nb)),
        out_spec=pl.BlockSpec((None, None, k, nb), lambda i, j, kk: (layer, j, 0, 0)),
        acc_shape=(k, nb), alias=buf)


def _mm_dw_natural(name, a, dy, buf, layer, n_layers):
    t, k = a.shape
    n = dy.shape[1]
    tko = _tile(k, 1024)
    tt = _tile(t, 1024)
    return _mm(
        name, TN, a, dy, grid=(k // tko, 1, t // tt),
        a_spec=pl.BlockSpec((tt, tko), lambda i, j, kk: (kk, i)),
        b_spec=pl.BlockSpec((tt, n), lambda i, j, kk: (kk, 0)),
        out_shape=_grad_buf(buf, (n_layers, k, n)),
        out_spec=pl.BlockSpec((None, tko, n), lambda i, j, kk: (layer, i, 0)),
        acc_shape=(tko, n), alias=buf)


def _mm_dw_down(name, act, dy, buf, layer, n_layers):
    nblk, t, kb = act.shape
    n = dy.shape[1]
    tt = _tile(t, 1024)
    return _mm(
        name, TN, act, dy, grid=(nblk, 1, t // tt),
        a_spec=pl.BlockSpec((None, tt, kb), lambda i, j, kk: (i, kk, 0)),
        b_spec=pl.BlockSpec((tt, n), lambda i, j, kk: (kk, 0)),
        out_shape=_grad_buf(buf, (n_layers, nblk, kb, n)),
        out_spec=pl.BlockSpec((None, None, kb, n), lambda i, j, kk: (layer, i, 0, 0)),
        acc_shape=(kb, n), alias=buf)


def _spatial_mask(transposed=False):
    r = lax.broadcasted_iota(jnp.int32, (A_CHUNK, A_CHUNK), 0) // CHUNK
    c = lax.broadcasted_iota(jnp.int32, (A_CHUNK, A_CHUNK), 1) // CHUNK
    return c >= r if transposed else r >= c


def _sgu_tile(t):
    return _tile(t, 2 * A_CHUNK)


def _sgu_fwd(zpre, g_sgu, w_sp, b_full, name):
    t, f2 = zpre.shape
    f = f2 // 2
    gd = f // A_GROUPS
    tm = _sgu_tile(t)

    def body(z_ref, g_ref, w_ref, b_ref, p_ref):
        mask = _spatial_mask()
        wm = [jnp.where(mask, w_ref[g], 0.0).astype(BF16) for g in range(A_GROUPS)]
        for c in range(tm // A_CHUNK):
            rows = pl.ds(c * A_CHUNK, A_CHUNK)
            z = _gelu(z_ref[rows, :].astype(F32))
            u = z[:, :f]
            v0 = z[:, f:]
            r = lax.rsqrt(jnp.mean(v0 * v0, axis=-1, keepdims=True) + EPS)
            v1 = (v0 * r * g_ref[...]).astype(BF16)
            for g in range(A_GROUPS):
                cols = slice(g * gd, (g + 1) * gd)
                v2 = jnp.dot(wm[g], v1[:, cols], preferred_element_type=F32) + b_ref[:, cols]
                p_ref[rows, cols] = (u[:, cols] * v2).astype(BF16)

    return _call(
        body, name, (t // tm,),
        [pl.BlockSpec((tm, f2), lambda i: (i, 0)),
         pl.BlockSpec((1, f), lambda i: (0, 0)),
         pl.BlockSpec((A_GROUPS, A_CHUNK, A_CHUNK), lambda i: (0, 0, 0)),
         pl.BlockSpec((A_CHUNK, f), lambda i: (0, 0))],
        pl.BlockSpec((tm, f), lambda i: (i, 0)),
        jax.ShapeDtypeStruct((t, f), BF16),
        vmem_bytes=2 * _nbytes((tm, f2), BF16) + 2 * _nbytes((tm, f), BF16) + 8 * _nbytes((A_CHUNK, f2), F32),
    )(zpre, g_sgu.reshape(1, f), w_sp, b_full)


def _sgu_bwd(zpre, dp, g_sgu, w_sp, w_sp_t, b_full, name):
    t, f2 = zpre.shape
    f = f2 // 2
    gd = f // A_GROUPS
    tm = _sgu_tile(t)
    n_steps = t // tm

    def body(z_ref, dp_ref, g_ref, w_ref, wt_ref, b_ref, dz_ref, dw_ref, db_ref, dg_ref, dv1_ref, dbf_ref):
        step = pl.program_id(0)

        @pl.when(step == 0)
        def _():
            dw_ref[...] = jnp.zeros_like(dw_ref)
            dg_ref[...] = jnp.zeros_like(dg_ref)
            dbf_ref[...] = jnp.zeros_like(dbf_ref)

        mask = _spatial_mask()
        mask_t = _spatial_mask(transposed=True)
        wm = [jnp.where(mask, w_ref[g], 0.0).astype(BF16) for g in range(A_GROUPS)]
        wmt = [jnp.where(mask_t, wt_ref[g], 0.0).astype(BF16) for g in range(A_GROUPS)]
        gain = g_ref[...]
        for c in range(tm // A_CHUNK):
            rows = pl.ds(c * A_CHUNK, A_CHUNK)
            z, dgelu = _gelu_and_grad(z_ref[rows, :].astype(F32))
            u = z[:, :f]
            v0 = z[:, f:]
            r = lax.rsqrt(jnp.mean(v0 * v0, axis=-1, keepdims=True) + EPS)
            xhat = v0 * r
            v1 = (xhat * gain).astype(BF16)
            dpf = dp_ref[rows, :].astype(F32)
            for g in range(A_GROUPS):
                cols = slice(g * gd, (g + 1) * gd)
                v1g = v1[:, cols]
                v2 = jnp.dot(wm[g], v1g, preferred_element_type=F32) + b_ref[:, cols]
                dpg = dpf[:, cols]
                dz_ref[rows, cols] = (dpg * v2 * dgelu[:, cols]).astype(BF16)
                dv2 = dpg * u[:, cols]
                dbf_ref[:, cols] += dv2
                dv2b = dv2.astype(BF16)
                dwg = lax.dot_general(dv2b, v1g, NT, preferred_element_type=F32)
                dw_ref[g] += jnp.where(mask, dwg, 0.0)
                dv1_ref[:, cols] = jnp.dot(wmt[g], dv2b, preferred_element_type=F32)
            dv1 = dv1_ref[...]
            dxhat = dv1 * gain
            dg_ref[...] += jnp.sum(dv1 * xhat, axis=0, keepdims=True)
            dv0 = r * (dxhat - xhat * jnp.mean(dxhat * xhat, axis=-1, keepdims=True))
            dz_ref[rows, pl.ds(f, f)] = (dv0 * dgelu[:, f:]).astype(BF16)

        @pl.when(step == n_steps - 1)
        def _():
            for g in range(A_GROUPS):
                db_ref[g] = jnp.sum(dbf_ref[:, g * gd:(g + 1) * gd], axis=1, keepdims=True)

    wspec = pl.BlockSpec((A_GROUPS, A_CHUNK, A_CHUNK), lambda i: (0, 0, 0))
    dz, dw, db, dg = _call(
        body, name, (n_steps,),
        [pl.BlockSpec((tm, f2), lambda i: (i, 0)),
         pl.BlockSpec((tm, f), lambda i: (i, 0)),
         pl.BlockSpec((1, f), lambda i: (0, 0)),
         wspec, wspec,
         pl.BlockSpec((A_CHUNK, f), lambda i: (0, 0))],
        [pl.BlockSpec((tm, f2), lambda i: (i, 0)),
         wspec,
         pl.BlockSpec((A_GROUPS, A_CHUNK, 1), lambda i: (0, 0, 0)),
         pl.BlockSpec((1, f), lambda i: (0, 0))],
        [jax.ShapeDtypeStruct((t, f2), BF16),
         jax.ShapeDtypeStruct((A_GROUPS, A_CHUNK, A_CHUNK), F32),
         jax.ShapeDtypeStruct((A_GROUPS, A_CHUNK, 1), F32),
         jax.ShapeDtypeStruct((1, f), F32)],
        scratch=[pltpu.VMEM((A_CHUNK, f), F32), pltpu.VMEM((A_CHUNK, f), F32)],
        vmem_bytes=4 * _nbytes((tm, f2), BF16) + 2 * _nbytes((tm, f), BF16) + 12 * _nbytes((A_CHUNK, f2), F32),
    )(zpre, dp, g_sgu.reshape(1, f), w_sp, w_sp_t, b_full)
    return dz, dw, db.reshape(A_GROUPS, A_CHUNK), dg.reshape(f)


def _act_fwd(gu, name):
    _, t, nb = gu.shape
    half = N_DEV // 2
    tm = _tile(t, 1024)
    gu4 = gu.reshape(2, half, t, nb)

    def body(gu_ref, a_ref):
        gate = gu_ref[0].astype(F32)
        up = gu_ref[1].astype(F32)
        a_ref[...] = (gate * _sigmoid(gate) * up).astype(BF16)

    return _call(
        body, name, (half, t // tm),
        [pl.BlockSpec((2, None, tm, nb), lambda j, i: (0, j, i, 0))],
        pl.BlockSpec((None, tm, nb), lambda j, i: (j, i, 0)),
        jax.ShapeDtypeStruct((half, t, nb), BF16),
        vmem_bytes=6 * _nbytes((tm, nb), BF16) + 6 * _nbytes((tm, nb), F32),
    )(gu4)


def _act_bwd(gu, da, name):
    _, t, nb = gu.shape
    half = N_DEV // 2
    tm = _tile(t, 1024)
    gu4 = gu.reshape(2, half, t, nb)

    def body(gu_ref, da_ref, dgu_ref):
        gate = gu_ref[0].astype(F32)
        up = gu_ref[1].astype(F32)
        d = da_ref[...].astype(F32)
        sig = _sigmoid(gate)
        dgu_ref[0] = (d * up * (sig * (1.0 + gate * (1.0 - sig)))).astype(BF16)
        dgu_ref[1] = (d * (gate * sig)).astype(BF16)

    out = _call(
        body, name, (half, t // tm),
        [pl.BlockSpec((2, None, tm, nb), lambda j, i: (0, j, i, 0)),
         pl.BlockSpec((None, tm, nb), lambda j, i: (j, i, 0))],
        pl.BlockSpec((2, None, tm, nb), lambda j, i: (0, j, i, 0)),
        jax.ShapeDtypeStruct((2, half, t, nb), BF16),
        vmem_bytes=10 * _nbytes((tm, nb), BF16) + 8 * _nbytes((tm, nb), F32),
    )(gu4, da)
    return out.reshape(N_DEV, t, nb)


def _rel_onehot(qi):
    col = lax.broadcasted_iota(jnp.int32, (REL_PAD, BAND), 1)
    idx = jnp.clip(qi - (col - LEFT), -MAX_REL, MAX_REL) + MAX_REL
    r = lax.broadcasted_iota(jnp.int32, (REL_PAD, BAND), 0)
    return jnp.where((r == idx) & (col >= SKIP), 1.0, 0.0).astype(BF16)


def _bias_build(table, name):
    h = table.shape[0]
    tab = jnp.pad(table, ((0, 0), (0, REL_PAD - N_REL)))

    def body(t_ref, o_ref):
        parts = _split3(t_ref[...])
        for qi in range(CHUNK):
            oh = _rel_onehot(qi)
            acc = jnp.dot(parts[0], oh, preferred_element_type=F32)
            acc += jnp.dot(parts[1], oh, preferred_element_type=F32)
            acc += jnp.dot(parts[2], oh, preferred_element_type=F32)
            o_ref[qi] = acc

    out = _call(
        body, name, (1,),
        [pl.BlockSpec((h, REL_PAD), lambda i: (0, 0))],
        pl.BlockSpec((CHUNK, h, BAND), lambda i: (0, 0, 0)),
        jax.ShapeDtypeStruct((CHUNK, h, BAND), F32),
        vmem_bytes=4 * _nbytes((CHUNK, h, BAND), F32),
    )(tab)
    return jnp.transpose(out, (1, 0, 2))


def _bias_grad(dbias, name):
    h = dbias.shape[0]
    db_t = jnp.transpose(dbias, (1, 0, 2))

    def body(d_ref, o_ref):
        acc = jnp.zeros((h, REL_PAD), F32)
        for qi in range(CHUNK):
            oh = _rel_onehot(qi)
            for piece in _split3(d_ref[qi]):
                acc += lax.dot_general(piece, oh, NT, preferred_element_type=F32)
        o_ref[...] = acc

    out = _call(
        body, name, (1,),
        [pl.BlockSpec((CHUNK, h, BAND), lambda i: (0, 0, 0))],
        pl.BlockSpec((h, REL_PAD), lambda i: (0, 0)),
        jax.ShapeDtypeStruct((h, REL_PAD), F32),
        vmem_bytes=4 * _nbytes((CHUNK, h, BAND), F32),
    )(db_t)
    return out[:, :N_REL]


def _head_masks():
    lane = lax.broadcasted_iota(jnp.int32, (CHUNK, HEAD_PAIR), 1)
    return lane < HEAD_DIM, lane >= HEAD_DIM


def _chunk_probs(qm, kb, bias, valid):
    s = lax.dot_general(qm, kb, NT, preferred_element_type=F32) + bias
    s = jnp.where(valid, s, NEG_INF)
    e = jnp.exp(s - jnp.max(s, axis=-1, keepdims=True))
    return e / jnp.sum(e, axis=-1, keepdims=True)


def _attn_fwd(q, kvpad, bias, name):
    t, d = q.shape
    n_pairs = d // HEAD_PAIR
    n_chunks = t // CHUNK

    def body(q_ref, k_ref, v_ref, b_ref, o_ref):
        masks = _head_masks()
        key = lax.broadcasted_iota(jnp.int32, (CHUNK, BAND), 1)

        def step(c, carry):
            r0 = pl.multiple_of(c * CHUNK, CHUNK)
            q2 = q_ref[pl.ds(r0, CHUNK), :].astype(F32)
            kb = k_ref[pl.ds(r0, BAND), :]
            vb = v_ref[pl.ds(r0, BAND), :]
            valid = key >= jnp.maximum(SKIP, LEFT - c * CHUNK)
            outs = []
            for a in range(2):
                qm = jnp.where(masks[a], q2, 0.0).astype(BF16)
                p = _chunk_probs(qm, kb, b_ref[a], valid)
                outs.append(jnp.dot(p.astype(BF16), vb, preferred_element_type=F32))
            o_ref[pl.ds(r0, CHUNK), :] = jnp.where(masks[0], outs[0], outs[1]).astype(BF16)
            return carry

        lax.fori_loop(0, n_chunks, step, 0)

    return _call(
        body, name, (n_pairs,),
        [pl.BlockSpec((t, HEAD_PAIR), lambda p: (0, p)),
         pl.BlockSpec((LEFT + t, HEAD_PAIR), lambda p: (0, p)),
         pl.BlockSpec((LEFT + t, HEAD_PAIR), lambda p: (0, n_pairs + p)),
         pl.BlockSpec((2, CHUNK, BAND), lambda p: (p, 0, 0))],
        pl.BlockSpec((t, HEAD_PAIR), lambda p: (0, p)),
        jax.ShapeDtypeStruct((t, d), BF16),
        vmem_bytes=8 * _nbytes((LEFT + t, HEAD_PAIR), BF16) + 4 * _nbytes((2, CHUNK, BAND), F32),
    )(q, kvpad, kvpad, bias)


def _attn_bwd(q, kvpad, bias, do, dk_in, dv_in, name):
    t, d = q.shape
    n_pairs = d // HEAD_PAIR
    n_chunks = t // CHUNK
    has_in = dk_in is not None

    def body(*refs):
        refs = list(refs)
        q_ref, k_ref, v_ref, b_ref, do_ref = refs[:5]
        refs = refs[5:]
        if has_in:
            dki_ref, dvi_ref = refs[:2]
            refs = refs[2:]
        dq_ref, dk_ref, dv_ref, db_ref = refs
        masks = _head_masks()
        key = lax.broadcasted_iota(jnp.int32, (CHUNK, BAND), 1)
        if has_in:
            dk_ref[...] = dki_ref[...]
            dv_ref[...] = dvi_ref[...]
        else:
            dk_ref[...] = jnp.zeros_like(dk_ref)
            dv_ref[...] = jnp.zeros_like(dv_ref)
        db_ref[...] = jnp.zeros_like(db_ref)

        def step(c, carry):
            r0 = pl.multiple_of(c * CHUNK, CHUNK)
            q2 = q_ref[pl.ds(r0, CHUNK), :].astype(F32)
            do2 = do_ref[pl.ds(r0, CHUNK), :].astype(F32)
            kb = k_ref[pl.ds(r0, BAND), :]
            vb = v_ref[pl.ds(r0, BAND), :]
            valid = key >= jnp.maximum(SKIP, LEFT - c * CHUNK)
            dqs = []
            dk_acc = jnp.zeros((BAND, HEAD_PAIR), F32)
            dv_acc = jnp.zeros((BAND, HEAD_PAIR), F32)
            for a in range(2):
                qm = jnp.where(masks[a], q2, 0.0).astype(BF16)
                dom = jnp.where(masks[a], do2, 0.0).astype(BF16)
                p = _chunk_probs(qm, kb, b_ref[a], valid)
                dp = lax.dot_general(dom, vb, NT, preferred_element_type=F32)
                ds = p * (dp - jnp.sum(dp * p, axis=-1, keepdims=True))
                db_ref[a] += ds
                dsb = ds.astype(BF16)
                dqs.append(jnp.dot(dsb, kb, preferred_element_type=F32))
                dk_acc += lax.dot_general(dsb, qm, TN, preferred_element_type=F32)
                dv_acc += lax.dot_general(p.astype(BF16), dom, TN, preferred_element_type=F32)
            dq = jnp.where(masks[0], dqs[0], dqs[1]) * ATTN_SCALE
            dq_ref[pl.ds(r0, CHUNK), :] = dq.astype(BF16)
            dk_ref[pl.ds(r0, BAND), :] += dk_acc
            dv_ref[pl.ds(r0, BAND), :] += dv_acc
            return carry

        lax.fori_loop(0, n_chunks, step, 0)

    q_spec = pl.BlockSpec((t, HEAD_PAIR), lambda p: (0, p))
    kv_spec = pl.BlockSpec((LEFT + t, HEAD_PAIR), lambda p: (0, p))
    b_spec = pl.BlockSpec((2, CHUNK, BAND), lambda p: (p, 0, 0))
    operands = [q, kvpad, kvpad, bias, do]
    in_specs = [q_spec, kv_spec, pl.BlockSpec((LEFT + t, HEAD_PAIR), lambda p: (0, n_pairs + p)), b_spec, q_spec]
    aliases = None
    if has_in:
        operands += [dk_in, dv_in]
        in_specs += [kv_spec, kv_spec]
        aliases = {5: 1, 6: 2}
    return _call(
        body, name, (n_pairs,),
        in_specs,
        [q_spec, kv_spec, kv_spec, b_spec],
        [jax.ShapeDtypeStruct((t, d), BF16),
         jax.ShapeDtypeStruct((LEFT + t, d), F32),
         jax.ShapeDtypeStruct((LEFT + t, d), F32),
         jax.ShapeDtypeStruct((d // HEAD_DIM, CHUNK, BAND), F32)],
        vmem_bytes=10 * _nbytes((LEFT + t, HEAD_PAIR), BF16) + 8 * _nbytes((LEFT + t, HEAD_PAIR), F32),
        aliases=aliases,
    )(*operands)


def _loss_head(x, g, target, name):
    t, d = x.shape
    tm = _tile(t, 512)

    def body(x_ref, g_ref, t_ref, dx_ref, loss_ref, dg_ref):
        @pl.when(pl.program_id(0) == 0)
        def _():
            loss_ref[...] = jnp.zeros_like(loss_ref)
            dg_ref[...] = jnp.zeros_like(dg_ref)

        xf = x_ref[...]
        r = lax.rsqrt(jnp.mean(xf * xf, axis=-1, keepdims=True) + EPS)
        xhat = xf * r
        diff = xhat * g_ref[...] - t_ref[...]
        row_loss = jnp.mean(diff * diff, axis=-1, keepdims=True)
        loss_ref[...] += 0.5 * jnp.sum(row_loss, axis=0, keepdims=True)
        dy = diff * (1.0 / d)
        dg_ref[...] += jnp.sum(dy * xhat, axis=0, keepdims=True)
        dxhat = dy * g_ref[...]
        dx_ref[...] = r * (dxhat - xhat * jnp.mean(dxhat * xhat, axis=-1, keepdims=True))

    row = pl.BlockSpec((tm, d), lambda i: (i, 0))
    vec = pl.BlockSpec((1, d), lambda i: (0, 0))
    dx, loss, dg = _call(
        body, name, (t // tm,),
        [row, vec, row],
        [row, pl.BlockSpec((1, 1), lambda i: (0, 0)), vec],
        [jax.ShapeDtypeStruct((t, d), F32), jax.ShapeDtypeStruct((1, 1), F32), jax.ShapeDtypeStruct((1, d), F32)],
        vmem_bytes=10 * _nbytes((tm, d), F32),
    )(x, g.reshape(1, d), target)
    return dx, loss[0, 0], dg.reshape(d)


def _adamw(parts, w, m, v, name):
    n_layers, n_src, r, c = parts.shape
    mult = 16 if parts.dtype == BF16 else 8
    tr = _row_tile(r, max(mult, (256 * 1024) // c), mult)
    c1 = 1.0 / (1.0 - ADAM_B1 ** ADAM_STEP)
    c2 = 1.0 / (1.0 - ADAM_B2 ** ADAM_STEP)

    def body(p_ref, w_ref, m_ref, v_ref, g_ref, d_ref, nm_ref, nv_ref):
        g = p_ref[0].astype(F32)
        for s in range(1, n_src):
            g = g + p_ref[s].astype(F32)
        nm = ADAM_B1 * m_ref[...] + (1.0 - ADAM_B1) * g
        nv = ADAM_B2 * v_ref[...] + (1.0 - ADAM_B2) * (g * g)
        g_ref[...] = g
        nm_ref[...] = nm
        nv_ref[...] = nv
        d_ref[...] = -ADAM_LR * ((nm * c1) / (jnp.sqrt(nv * c2) + ADAM_EPS) + ADAM_WD * w_ref[...])

    blk = pl.BlockSpec((None, tr, c), lambda l, i: (l, i, 0))
    out = jax.ShapeDtypeStruct((n_layers, r, c), F32)
    return _call(
        body, name, (n_layers, r // tr),
        [pl.BlockSpec((None, n_src, tr, c), lambda l, i: (l, 0, i, 0)), blk, blk, blk],
        [blk, blk, blk, blk],
        [out, out, out, out],
        vmem_bytes=2 * _nbytes((n_src, tr, c), parts.dtype) + 18 * _nbytes((tr, c), F32),
    )(parts, w, m, v)


def _ordered_sum(parts, name):
    n_src, r, c = parts.shape

    def body(p_ref, o_ref):
        acc = p_ref[0]
        for s in range(1, n_src):
            acc = acc + p_ref[s]
        o_ref[...] = acc

    return _call(
        body, name, (1,),
        [pl.BlockSpec((n_src, r, c), lambda i: (0, 0, 0))],
        pl.BlockSpec((r, c), lambda i: (0, 0)),
        jax.ShapeDtypeStruct((r, c), F32),
        vmem_bytes=4 * _nbytes((n_src, r, c), F32),
    )(parts)


def _position():
    return lax.axis_index("x"), lax.axis_index("y"), lax.axis_index("c")


def _linear(p):
    return 4 * p[0] + 2 * p[1] + p[2]


def _all_gather(shards, name):
    n = len(shards)

    def body(*refs):
        ins, outs = refs[:n], refs[n:2 * n]
        send_sems, recv_sems, local_sems = refs[2 * n:]
        x, y, c = _position()
        me, sibling = (x, y, c), (x, y, 1 - c)
        chips = [(1 - x, y), (x, 1 - y), (1 - x, 1 - y)]

        def slab(t, p):
            return outs[t].at[:, _linear(p)]

        def copy(t, k, block, to, src=None):
            return pltpu.make_async_remote_copy(
                src_ref=slab(t, block) if src is None else src,
                dst_ref=slab(t, block),
                send_sem=send_sems.at[t, k],
                recv_sem=recv_sems.at[t, k],
                device_id=to,
                device_id_type=MESH,
            )

        started = []
        for t in range(n):
            mine = pltpu.make_async_copy(ins[t], slab(t, me), local_sems.at[t])
            mine.start()
            started.append(mine)
        sends = []
        for t in range(n):
            first = [copy(t, 0, me, sibling, src=ins[t])]
            first += [copy(t, 1 + j, me, (*chip, c), src=ins[t]) for j, chip in enumerate(chips)]
            for cp in first:
                cp.start()
            sends += first
        for t in range(n):
            for j, chip in enumerate(chips):
                copy(t, 1 + j, (*chip, c), me).wait_recv()
                passed = copy(t, 4 + j, (*chip, c), sibling)
                passed.start()
                sends.append(passed)
        for t in range(n):
            copy(t, 0, sibling, me).wait_recv()
            for j, chip in enumerate(chips):
                copy(t, 4 + j, (*chip, 1 - c), me).wait_recv()
        for cp in sends:
            cp.wait_send()
        for mine in started:
            mine.wait()

    out_shape = [jax.ShapeDtypeStruct((s.shape[0], N_DEV) + s.shape[1:], s.dtype) for s in shards]
    return pl.pallas_call(
        body,
        name=name,
        in_specs=[HBM_SPEC] * n,
        out_specs=[HBM_SPEC] * n,
        out_shape=out_shape,
        scratch_shapes=[
            pltpu.SemaphoreType.DMA((n, N_DEV - 1)),
            pltpu.SemaphoreType.DMA((n, N_DEV - 1)),
            pltpu.SemaphoreType.DMA((n,)),
        ],
    )(*shards)


def _exchange(blocks, name):
    n = len(blocks)

    def body(*refs):
        ins, outs = refs[:n], refs[n:2 * n]
        send_sems, recv_sems, local_sems = refs[2 * n:]
        x, y, c = _position()
        me = _linear((x, y, c))
        flips = [(fx, fy, fc) for fx in (0, 1) for fy in (0, 1) for fc in (0, 1)][1:]

        def peer_of(flip):
            fx, fy, fc = flip
            return (1 - x if fx else x, 1 - y if fy else y, 1 - c if fc else c)

        def copy(t, k, peer):
            return pltpu.make_async_remote_copy(
                src_ref=ins[t].at[:, _linear(peer)],
                dst_ref=outs[t].at[:, me],
                send_sem=send_sems.at[t, k],
                recv_sem=recv_sems.at[t, k],
                device_id=peer,
                device_id_type=MESH,
            )

        def arrival(t, k, peer):
            return pltpu.make_async_remote_copy(
                src_ref=ins[t].at[:, _linear(peer)],
                dst_ref=outs[t].at[:, _linear(peer)],
                send_sem=send_sems.at[t, k],
                recv_sem=recv_sems.at[t, k],
                device_id=peer,
                device_id_type=MESH,
            )

        own = []
        for t in range(n):
            cp = pltpu.make_async_copy(ins[t].at[:, me], outs[t].at[:, me], local_sems.at[t])
            cp.start()
            own.append(cp)
        sends = []
        for t in range(n):
            for k, flip in enumerate(flips):
                cp = copy(t, k, peer_of(flip))
                cp.start()
                sends.append(cp)
        for t in range(n):
            for k, flip in enumerate(flips):
                arrival(t, k, peer_of(flip)).wait_recv()
        for cp in sends:
            cp.wait_send()
        for cp in own:
            cp.wait()

    out_shape = [jax.ShapeDtypeStruct(b.shape, b.dtype) for b in blocks]
    return pl.pallas_call(
        body,
        name=name,
        in_specs=[HBM_SPEC] * n,
        out_specs=[HBM_SPEC] * n,
        out_shape=out_shape,
        scratch_shapes=[
            pltpu.SemaphoreType.DMA((n, N_DEV - 1)),
            pltpu.SemaphoreType.DMA((n, N_DEV - 1)),
            pltpu.SemaphoreType.DMA((n,)),
        ],
    )(*blocks)


def _pack(arrays, row_multiple):
    flat = jnp.concatenate([a.reshape(-1) for a in arrays])
    quantum = row_multiple * FLAT_LANES
    padded = -(-flat.shape[0] // quantum) * quantum
    return jnp.pad(flat, (0, padded - flat.shape[0])).reshape(-1, FLAT_LANES)


def _unpack(flat, like):
    flat = flat.reshape(-1)
    out, at = [], 0
    for a in like:
        size = math.prod(a.shape)
        out.append(flat[at:at + size].reshape(a.shape))
        at += size
    return out


def kernel(x, a_norm, a_w_in, a_sgu_norm, a_w_spatial, a_b_spatial, a_w_out, kv_norm, w_kv, b_norm, b_w_q, b_rel_bias, b_w_o, ffn_norm, ffn_w_gate_up, ffn_w_down, final_norm, loss_target, m_a_norm, m_a_w_in, m_a_sgu_norm, m_a_w_spatial, m_a_b_spatial, m_a_w_out, m_kv_norm, m_w_kv, m_b_norm, m_b_w_q, m_b_rel_bias, m_b_w_o, m_ffn_norm, m_ffn_w_gate_up, m_ffn_w_down, m_final_norm, v_a_norm, v_a_w_in, v_a_sgu_norm, v_a_w_spatial, v_a_b_spatial, v_a_w_out, v_kv_norm, v_w_kv, v_b_norm, v_b_w_q, v_b_rel_bias, v_b_w_o, v_ffn_norm, v_ffn_w_gate_up, v_ffn_w_down, v_final_norm):
    xs = x[0]
    target = loss_target[0]
    t, d = xs.shape
    n_a = a_w_in.shape[0]
    n_b = b_w_q.shape[0]
    depth = ffn_w_gate_up.shape[0]
    f_a = a_w_out.shape[1] * N_DEV
    gd = f_a // A_GROUPS
    nb_ffn = ffn_w_gate_up.shape[2]
    me = _linear(_position())

    small_rows = -(-(a_norm.size + a_sgu_norm.size) // (8 * 128)) * 8
    small = jnp.pad(jnp.concatenate([a_norm.reshape(-1), a_sgu_norm.reshape(-1)]),
                    (0, small_rows * 128 - a_norm.size - a_sgu_norm.size)).reshape(1, small_rows, 128)
    big = [a_w_in, a_w_out, w_kv[None], b_w_q, b_w_o, ffn_w_gate_up, ffn_w_down]
    gathered = _all_gather([w.astype(BF16) for w in big] + [small], "gather_weights")
    win_g, wout_g, wkv_g, wq_g, wo_g, wgu_g, wd_g, small_g = gathered
    wout_n = wout_g.reshape(n_a, f_a, d)
    wq_n = wq_g.reshape(n_b, d, d)
    wo_n = wo_g.reshape(n_b, d, d)
    wd_4 = wd_g.reshape(depth, N_DEV // 2, 2 * ffn_w_down.shape[1], d)
    small_g = small_g.reshape(N_DEV, -1)
    a_norm_full = small_g[:, :a_norm.size].reshape(N_DEV, n_a, -1).transpose(1, 0, 2).reshape(n_a, d)
    a_sgu_full = small_g[:, a_norm.size:a_norm.size + a_sgu_norm.size].reshape(
        N_DEV, n_a, -1).transpose(1, 0, 2).reshape(n_a, f_a)

    w_sp_t = jnp.swapaxes(a_w_spatial, -1, -2)
    b_full = jnp.repeat(jnp.swapaxes(a_b_spatial, -1, -2), gd, axis=-1)

    saved = []

    def ffn_fwd(xin, layer):
        hf = _rms_fwd(xin, ffn_norm[layer], f"ffn_norm_fwd_{layer}")
        gu = _mm_colblock(f"ffn_gate_up_{layer}", hf, wgu_g, layer, blocked_out=True)
        act = _act_fwd(gu, f"ffn_act_fwd_{layer}")
        xout = _mm_down(f"ffn_down_{layer}", act, wd_4, layer, xin)
        return xout, (xin, hf, gu, act)

    for i in range(n_a):
        h = _rms_fwd(xs, a_norm_full[i], f"a_norm_fwd_{i}")
        zpre = _mm_colblock(f"a_in_{i}", h, win_g, i)
        p = _sgu_fwd(zpre, a_sgu_full[i], a_w_spatial[i], b_full[i], f"a_sgu_fwd_{i}")
        x_mid = _mm_natural(f"a_out_{i}", p, wout_n, i, res=xs)
        x_out, ffn_saved = ffn_fwd(x_mid, i)
        saved.append((xs, h, zpre, p, ffn_saved))
        xs = x_out

    x_kv = xs
    h_kv = _rms_fwd(x_kv, kv_norm, "kv_norm_fwd")
    kv = _mm_colblock("kv_proj", h_kv, wkv_g, 0)
    kvpad = jnp.pad(kv, ((LEFT, 0), (0, 0)))

    biases = [_bias_build(b_rel_bias[i], f"rel_bias_{i}") for i in range(n_b)]
    for i in range(n_b):
        layer = n_a + i
        hb = _rms_fwd(xs, b_norm[i], f"b_norm_fwd_{i}")
        q = _mm_natural(f"b_q_{i}", hb, wq_n, i, out_dtype=BF16, scale=ATTN_SCALE)
        o = _attn_fwd(q, kvpad, biases[i], f"b_attn_fwd_{i}")
        x_mid = _mm_natural(f"b_o_{i}", o, wo_n, i, res=xs)
        x_out, ffn_saved = ffn_fwd(x_mid, layer)
        saved.append((xs, hb, q, o, ffn_saved))
        xs = x_out

    dx, loss_local, g_final = _loss_head(xs, final_norm, target, "loss_head")
    loss = lax.psum(loss_local, ("x", "y", "c"))

    g_win = g_wout = g_wkv = g_wq = g_wo = g_wgu = g_wd = None
    g_ffn_norm = [None] * depth
    g_a_norm = [None] * n_a
    g_a_sgu = [None] * n_a
    g_w_sp = [None] * n_a
    g_b_sp = [None] * n_a
    g_b_norm = [None] * n_b
    g_rel = [None] * n_b

    def ffn_bwd(dx, layer, ffn_saved):
        nonlocal g_wgu, g_wd
        xin, hf, gu, act = ffn_saved
        g_wd = _mm_dw_down(f"ffn_down_dw_{layer}", act, dx, g_wd, layer, depth)
        da = _mm_t_down(f"ffn_down_dx_{layer}", dx, wd_4, layer)
        dgu = _act_bwd(gu, da, f"ffn_act_bwd_{layer}")
        g_wgu = _mm_dw_colblock(f"ffn_gate_up_dw_{layer}", hf, dgu, g_wgu, layer, depth, blocked_in=True)
        dh = _mm_t_colblock(f"ffn_gate_up_dx_{layer}", dgu, wgu_g, layer, blocked_in=True)
        dx, g_ffn_norm[layer] = _rms_bwd(xin, ffn_norm[layer], dh, dx, f"ffn_norm_bwd_{layer}")
        return dx

    dk = dv = None
    for i in reversed(range(n_b)):
        layer = n_a + i
        x_in, hb, q, o, ffn_saved = saved[layer]
        dx = ffn_bwd(dx, layer, ffn_saved)
        g_wo = _mm_dw_natural(f"b_o_dw_{i}", o, dx, g_wo, i, n_b)
        do = _mm_t_natural(f"b_o_dx_{i}", dx, wo_n, i)
        dq, dk, dv, dbias = _attn_bwd(q, kvpad, biases[i], do, dk, dv, f"b_attn_bwd_{i}")
        g_rel[i] = _bias_grad(dbias, f"rel_bias_grad_{i}")
        g_wq = _mm_dw_natural(f"b_q_dw_{i}", hb, dq, g_wq, i, n_b)
        dh = _mm_t_natural(f"b_q_dx_{i}", dq, wq_n, i)
        dx, g_b_norm[i] = _rms_bwd(x_in, b_norm[i], dh, dx, f"b_norm_bwd_{i}")

    dkv = jnp.concatenate([dk[LEFT:], dv[LEFT:]], axis=1).astype(BF16)
    g_wkv = _mm_dw_colblock("kv_proj_dw", h_kv, dkv, None, 0, 1)
    dh = _mm_t_colblock("kv_proj_dx", dkv, wkv_g, 0)
    dx, g_kv_norm = _rms_bwd(x_kv, kv_norm, dh, dx, "kv_norm_bwd")

    for i in reversed(range(n_a)):
        x_in, h, zpre, p, ffn_saved = saved[i]
        dx = ffn_bwd(dx, i, ffn_saved)
        g_wout = _mm_dw_natural(f"a_out_dw_{i}", p, dx, g_wout, i, n_a)
        dp = _mm_t_natural(f"a_out_dx_{i}", dx, wout_n, i)
        dz, g_w_sp[i], g_b_sp[i], g_a_sgu[i] = _sgu_bwd(
            zpre, dp, a_sgu_full[i], a_w_spatial[i], w_sp_t[i], b_full[i], f"a_sgu_bwd_{i}")
        g_win = _mm_dw_colblock(f"a_in_dw_{i}", h, dz, g_win, i, n_a)
        dh = _mm_t_colblock(f"a_in_dx_{i}", dz, win_g, i)
        dx, g_a_norm[i] = _rms_bwd(x_in, a_norm_full[i], dh, dx, f"a_norm_bwd_{i}")
    grad_x = dx[None]

    small_like = [jax.ShapeDtypeStruct((n_a, d), F32), jax.ShapeDtypeStruct((n_a, f_a), F32),
                  a_w_spatial, a_b_spatial, kv_norm, b_norm, b_rel_bias, ffn_norm, final_norm]
    small_partial = _pack(
        [jnp.stack(g_a_norm), jnp.stack(g_a_sgu), jnp.stack(g_w_sp), jnp.stack(g_b_sp), g_kv_norm,
         jnp.stack(g_b_norm), jnp.stack(g_rel), jnp.stack(g_ffn_norm), g_final], N_DEV * 8)
    chunk_rows = small_partial.shape[0] // N_DEV
    big_grads = [
        g_win,
        g_wout.reshape(n_a, N_DEV, f_a // N_DEV, d),
        g_wkv,
        g_wq.reshape(n_b, N_DEV, d // N_DEV, d),
        g_wo.reshape(n_b, N_DEV, d // N_DEV, d),
        g_wgu,
        g_wd.reshape(depth, N_DEV, ffn_w_down.shape[1], d),
    ]
    received = _exchange(big_grads + [small_partial.reshape(1, N_DEV, chunk_rows, FLAT_LANES)], "exchange_grads")
    small_sum = _ordered_sum(received[-1][0], "small_grad_sum")
    small_all = _all_gather([small_sum[None]], "gather_small_grads")[0]
    (ga_norm, ga_sgu, gw_sp, gb_sp, gkv_norm, gb_norm, g_relb, gffn_norm, gfinal) = _unpack(small_all, small_like)

    results = {}
    big_names = ["a_w_in", "a_w_out", "w_kv", "b_w_q", "b_w_o", "ffn_w_gate_up", "ffn_w_down"]
    big_wmv = [(a_w_in, m_a_w_in, v_a_w_in), (a_w_out, m_a_w_out, v_a_w_out),
               (w_kv[None], m_w_kv[None], v_w_kv[None]), (b_w_q, m_b_w_q, v_b_w_q), (b_w_o, m_b_w_o, v_b_w_o),
               (ffn_w_gate_up, m_ffn_w_gate_up, v_ffn_w_gate_up), (ffn_w_down, m_ffn_w_down, v_ffn_w_down)]
    for name, parts, (w, m, v) in zip(big_names, received[:-1], big_wmv):
        outs = _adamw(parts, w, m, v, f"adamw_{name}")
        if name == "w_kv":
            outs = [o[0] for o in outs]
        results[name] = outs

    n_cols = a_norm.shape[1]
    s_cols = a_sgu_norm.shape[1]
    small_g_list = [lax.dynamic_slice(ga_norm, (0, me * n_cols), (n_a, n_cols)),
                    lax.dynamic_slice(ga_sgu, (0, me * s_cols), (n_a, s_cols)),
                    gw_sp, gb_sp, gkv_norm, gb_norm, g_relb, gffn_norm, gfinal]
    small_names = ["a_norm", "a_sgu_norm", "a_w_spatial", "a_b_spatial", "kv_norm", "b_norm", "b_rel_bias",
                   "ffn_norm", "final_norm"]
    small_w = [a_norm, a_sgu_norm, a_w_spatial, a_b_spatial, kv_norm, b_norm, b_rel_bias, ffn_norm, final_norm]
    small_m = [m_a_norm, m_a_sgu_norm, m_a_w_spatial, m_a_b_spatial, m_kv_norm, m_b_norm, m_b_rel_bias,
               m_ffn_norm, m_final_norm]
    small_v = [v_a_norm, v_a_sgu_norm, v_a_w_spatial, v_a_b_spatial, v_kv_norm, v_b_norm, v_b_rel_bias,
               v_ffn_norm, v_final_norm]
    flat_g = _pack(small_g_list, 8)
    flat_out = _adamw(flat_g[None, None], _pack(small_w, 8)[None], _pack(small_m, 8)[None],
                      _pack(small_v, 8)[None], "adamw_small")
    unpacked = [_unpack(o[0], small_w) for o in flat_out]
    for idx, name in enumerate(small_names):
        results[name] = [unpacked[kind][idx] for kind in range(4)]

    order = ["a_norm", "a_w_in", "a_sgu_norm", "a_w_spatial", "a_b_spatial", "a_w_out", "kv_norm", "w_kv",
             "b_norm", "b_w_q", "b_rel_bias", "b_w_o", "ffn_norm", "ffn_w_gate_up", "ffn_w_down", "final_norm"]
    outputs = [loss, grad_x]
    for kind in range(4):
        outputs += [results[name][kind] for name in order]
    return tuple(outputs)
```

```python
import math

import jax
import jax.numpy as jnp
from jax import lax
from jax.experimental import pallas as pl
from jax.experimental.pallas import tpu as pltpu

F32 = jnp.float32
BF16 = jnp.bfloat16
MESH = pl.DeviceIdType.MESH
HBM_SPEC = pl.BlockSpec(memory_space=pltpu.HBM)

N_DEV = 8
CHUNK = 64
A_CHUNK = 128
A_GROUPS = 8
N_LEFT_CHUNKS = 8
LEFT = N_LEFT_CHUNKS * CHUNK
PAIR_ROWS = 2 * CHUNK
PAIR_BAND = PAIR_ROWS + LEFT
Q_BLOCK = 2 * PAIR_ROWS
K_BLOCK = Q_BLOCK + LEFT
MAX_REL = 256
N_REL = 2 * MAX_REL + 1
REL_PAD = 640
HEAD_DIM = 64
HEAD_PAIR = 2 * HEAD_DIM
ATTN_SCALE = HEAD_DIM ** -0.5
EPS = 1e-6
NEG_INF = -1e30
ADAM_LR = 0.001
ADAM_B1 = 0.9
ADAM_B2 = 0.999
ADAM_EPS = 1e-08
ADAM_WD = 0.01
ADAM_STEP = 10
FLAT_LANES = 1024
V7X_VMEM_BYTES = 64 * 1024 * 1024
VMEM_FLOOR_BYTES = 32 * 1024 * 1024
VMEM_CEIL_BYTES = V7X_VMEM_BYTES - 8 * 1024 * 1024

NN = (((1,), (0,)), ((), ()))
NT = (((1,), (1,)), ((), ()))
TN = (((0,), (0,)), ((), ()))


def _tile(n, pref):
    return pref if n % pref == 0 else n


def _row_tile(n, pref, mult):
    best = None
    for t in range(mult, min(n, pref) + 1, mult):
        if n % t == 0:
            best = t
    return best if best is not None else n


def _nbytes(shape, dtype):
    n = 1
    for s in shape:
        if s is not None:
            n *= s
    return n * jnp.dtype(dtype).itemsize


def _call(body, name, grid, in_specs, out_specs, out_shape, scratch=(), vmem_bytes=0, aliases=None):
    limit = int(min(max(VMEM_FLOOR_BYTES, vmem_bytes * 5 // 4), VMEM_CEIL_BYTES))
    return pl.pallas_call(
        body,
        name=name,
        grid=grid,
        in_specs=in_specs,
        out_specs=out_specs,
        out_shape=out_shape,
        scratch_shapes=list(scratch),
        input_output_aliases=aliases or {},
        compiler_params=pltpu.CompilerParams(
            dimension_semantics=("arbitrary",) * len(grid), vmem_limit_bytes=limit),
    )


def _erf_parts(x):
    ax = jnp.abs(x) * (1.0 / math.sqrt(2.0))
    t = 1.0 / (1.0 + 0.3275911 * ax)
    poly = ((((1.061405429 * t - 1.453152027) * t + 1.421413741) * t - 0.284496736) * t + 0.254829592) * t
    ex = jnp.exp(-ax * ax)
    erf_abs = 1.0 - poly * ex
    return jnp.where(x < 0, -erf_abs, erf_abs), ex


def _gelu_and_grad(x):
    erf, ex = _erf_parts(x)
    cdf = 0.5 * (1.0 + erf)
    return x * cdf, cdf + x * ex * (1.0 / math.sqrt(2.0 * math.pi))


def _gelu(x):
    erf, _ = _erf_parts(x)
    return x * (0.5 * (1.0 + erf))


def _sigmoid(x):
    return 1.0 / (1.0 + jnp.exp(-x))


def _split3(x):
    hi = x.astype(BF16)
    r1 = x - hi.astype(F32)
    mid = r1.astype(BF16)
    lo = (r1 - mid.astype(F32)).astype(BF16)
    return hi, mid, lo


def _rms_fwd(x, g, name):
    t, d = x.shape
    tm = _tile(t, 512)

    def body(x_ref, g_ref, o_ref):
        xf = x_ref[...]
        r = lax.rsqrt(jnp.mean(xf * xf, axis=-1, keepdims=True) + EPS)
        o_ref[...] = (xf * r * g_ref[...]).astype(o_ref.dtype)

    return _call(
        body, name, (t // tm,),
        [pl.BlockSpec((tm, d), lambda i: (i, 0)), pl.BlockSpec((1, d), lambda i: (0, 0))],
        pl.BlockSpec((tm, d), lambda i: (i, 0)),
        jax.ShapeDtypeStruct((t, d), BF16),
        vmem_bytes=2 * (_nbytes((tm, d), F32) + _nbytes((tm, d), BF16)) + 4 * _nbytes((tm, d), F32),
    )(x, g.reshape(1, d))


def _rms_bwd(x, g, dh, dx_up, name):
    t, d = x.shape
    tm = _tile(t, 512)

    def body(x_ref, g_ref, dh_ref, up_ref, dx_ref, dg_ref):
        @pl.when(pl.program_id(0) == 0)
        def _():
            dg_ref[...] = jnp.zeros_like(dg_ref)

        xf = x_ref[...]
        r = lax.rsqrt(jnp.mean(xf * xf, axis=-1, keepdims=True) + EPS)
        xhat = xf * r
        dy = dh_ref[...].astype(F32)
        dxhat = dy * g_ref[...]
        dg_ref[...] += jnp.sum(dy * xhat, axis=0, keepdims=True)
        dx = r * (dxhat - xhat * jnp.mean(dxhat * xhat, axis=-1, keepdims=True))
        dx_ref[...] = up_ref[...] + dx

    row = pl.BlockSpec((tm, d), lambda i: (i, 0))
    vec = pl.BlockSpec((1, d), lambda i: (0, 0))
    dx, dg = _call(
        body, name, (t // tm,),
        [row, vec, row, row],
        [row, vec],
        [jax.ShapeDtypeStruct((t, d), F32), jax.ShapeDtypeStruct((1, d), F32)],
        vmem_bytes=10 * _nbytes((tm, d), F32),
    )(x, g.reshape(1, d), dh, dx_up)
    return dx, dg.reshape(d)


def _mm(name, dims, a, b, *, grid, a_spec, b_spec, out_shape, out_spec, acc_shape,
        res=None, res_spec=None, alias=None, scale=None):
    nk = grid[2]
    has_res = res is not None
    has_alias = alias is not None

    def body(*refs):
        refs = list(refs)
        a_ref = refs.pop(0)
        b_ref = refs.pop(0)
        r_ref = refs.pop(0) if has_res else None
        if has_alias:
            refs.pop(0)
        o_ref = refs.pop(0)
        part = lax.dot_general(a_ref[...].astype(BF16), b_ref[...].astype(BF16), dims,
                               preferred_element_type=F32)

        def finish(acc):
            if scale is not None:
                acc = acc * scale
            if has_res:
                acc = acc + r_ref[...]
            o_ref[...] = acc.astype(o_ref.dtype)

        if nk == 1:
            finish(part)
        else:
            acc_ref = refs.pop(0)
            k = pl.program_id(2)

            @pl.when(k == 0)
            def _():
                acc_ref[...] = part

            @pl.when(k > 0)
            def _():
                acc_ref[...] += part

            @pl.when(k == nk - 1)
            def _():
                finish(acc_ref[...])

    operands = [a, b]
    in_specs = [a_spec, b_spec]
    vmem = 2 * (_nbytes(a_spec.block_shape, a.dtype) + _nbytes(b_spec.block_shape, b.dtype)
                + _nbytes(out_spec.block_shape, out_shape.dtype))
    vmem += 3 * _nbytes(acc_shape, F32)
    if has_res:
        operands.append(res)
        in_specs.append(res_spec)
        vmem += 2 * _nbytes(res_spec.block_shape, res.dtype)
    aliases = None
    if has_alias:
        aliases = {len(operands): 0}
        operands.append(alias)
        in_specs.append(pl.BlockSpec(memory_space=pl.ANY))
    scratch = [pltpu.VMEM(acc_shape, F32)] if nk > 1 else []
    return _call(body, name, grid, in_specs, out_spec, out_shape, scratch=scratch,
                 vmem_bytes=vmem, aliases=aliases)(*operands)


def _mm_colblock(name, h, w_g, layer, out_dtype=BF16, blocked_out=False):
    t, k = h.shape
    nb = w_g.shape[3]
    tm = _tile(t, 2048)
    if blocked_out:
        out_shape = jax.ShapeDtypeStruct((N_DEV, t, nb), out_dtype)
        out_spec = pl.BlockSpec((None, tm, nb), lambda i, j, kk: (j, i, 0))
    else:
        out_shape = jax.ShapeDtypeStruct((t, N_DEV * nb), out_dtype)
        out_spec = pl.BlockSpec((tm, nb), lambda i, j, kk: (i, j))
    return _mm(
        name, NN, h, w_g, grid=(t // tm, N_DEV, 1),
        a_spec=pl.BlockSpec((tm, k), lambda i, j, kk: (i, 0)),
        b_spec=pl.BlockSpec((None, None, k, nb), lambda i, j, kk: (layer, j, 0, 0)),
        out_shape=out_shape, out_spec=out_spec, acc_shape=(tm, nb))


def _mm_natural(name, a, w, layer, *, res=None, out_dtype=F32, scale=None):
    t, k = a.shape
    n = w.shape[2]
    tm = _tile(t, 1024)
    tn = _tile(n, 512)
    res_spec = None if res is None else pl.BlockSpec((tm, tn), lambda i, j, kk: (i, j))
    return _mm(
        name, NN, a, w, grid=(t // tm, n // tn, 1),
        a_spec=pl.BlockSpec((tm, k), lambda i, j, kk: (i, 0)),
        b_spec=pl.BlockSpec((None, k, tn), lambda i, j, kk: (layer, 0, j)),
        out_shape=jax.ShapeDtypeStruct((t, n), out_dtype),
        out_spec=pl.BlockSpec((tm, tn), lambda i, j, kk: (i, j)),
        acc_shape=(tm, tn), res=res, res_spec=res_spec, scale=scale)


def _mm_down(name, act, w4, layer, res):
    nblk, t, kb = act.shape
    n = w4.shape[3]
    tm = _tile(t, 1024)
    tn = _tile(n, 1024)
    return _mm(
        name, NN, act, w4, grid=(t // tm, n // tn, nblk),
        a_spec=pl.BlockSpec((None, tm, kb), lambda i, j, kk: (kk, i, 0)),
        b_spec=pl.BlockSpec((None, None, kb, tn), lambda i, j, kk: (layer, kk, 0, j)),
        out_shape=jax.ShapeDtypeStruct((t, n), F32),
        out_spec=pl.BlockSpec((tm, tn), lambda i, j, kk: (i, j)),
        acc_shape=(tm, tn), res=res, res_spec=pl.BlockSpec((tm, tn), lambda i, j, kk: (i, j)))


def _mm_t_colblock(name, dz, w_g, layer, blocked_in=False):
    k = w_g.shape[2]
    nb = w_g.shape[3]
    t = dz.shape[1] if blocked_in else dz.shape[0]
    tm = _tile(t, 2048)
    if blocked_in:
        a_spec = pl.BlockSpec((None, tm, nb), lambda i, j, kk: (kk, i, 0))
    else:
        a_spec = pl.BlockSpec((tm, nb), lambda i, j, kk: (i, kk))
    return _mm(
        name, NT, dz, w_g, grid=(t // tm, 1, N_DEV),
        a_spec=a_spec,
        b_spec=pl.BlockSpec((None, None, k, nb), lambda i, j, kk: (layer, kk, 0, 0)),
        out_shape=jax.ShapeDtypeStruct((t, k), BF16),
        out_spec=pl.BlockSpec((tm, k), lambda i, j, kk: (i, 0)),
        acc_shape=(tm, k))


def _mm_t_natural(name, dy, w, layer):
    t, n = dy.shape
    k = w.shape[1]
    tm = _tile(t, 1024)
    tk = _tile(k, 512)
    return _mm(
        name, NT, dy, w, grid=(t // tm, k // tk, 1),
        a_spec=pl.BlockSpec((tm, n), lambda i, j, kk: (i, 0)),
        b_spec=pl.BlockSpec((None, tk, n), lambda i, j, kk: (layer, j, 0)),
        out_shape=jax.ShapeDtypeStruct((t, k), BF16),
        out_spec=pl.BlockSpec((tm, tk), lambda i, j, kk: (i, j)),
        acc_shape=(tm, tk))


def _mm_t_down(name, dy, w4, layer):
    t, n = dy.shape
    nblk, kb = w4.shape[1], w4.shape[2]
    tm = _tile(t, 2048)
    return _mm(
        name, NT, dy, w4, grid=(t // tm, nblk, 1),
        a_spec=pl.BlockSpec((tm, n), lambda i, j, kk: (i, 0)),
        b_spec=pl.BlockSpec((None, None, kb, n), lambda i, j, kk: (layer, j, 0, 0)),
        out_shape=jax.ShapeDtypeStruct((nblk, t, kb), BF16),
        out_spec=pl.BlockSpec((None, tm, kb), lambda i, j, kk: (j, i, 0)),
        acc_shape=(tm, kb))


def _grad_buf(buf, shape):
    return jax.ShapeDtypeStruct(shape, BF16) if buf is None else jax.ShapeDtypeStruct(buf.shape, buf.dtype)


def _mm_dw_colblock(name, h, dz, buf, layer, n_layers, blocked_in=False):
    t, k = h.shape
    nb = dz.shape[2] if blocked_in else dz.shape[1] // N_DEV
    tk = _tile(t, 1024)
    if blocked_in:
        b_spec = pl.BlockSpec((None, tk, nb), lambda i, j, kk: (j, kk, 0))
    else:
        b_spec = pl.BlockSpec((tk, nb), lambda i, j, kk: (kk, j))
    return _mm(
        name, TN, h, dz, grid=(1, N_DEV, t // tk),
        a_spec=pl.BlockSpec((tk, k), lambda i, j, kk: (kk, 0)),
        b_spec=b_spec,
        out_shape=_grad_buf(buf, (n_layers, N_DEV, k, nb)),
        out_spec=pl.BlockSpec((None, None, k, nb), lambda i, j, kk: (layer, j, 0, 0)),
        acc_shape=(k, nb), alias=buf)


def _mm_dw_natural(name, a, dy, buf, layer, n_layers):
    t, k = a.shape
    n = dy.shape[1]
    tko = _tile(k, 1024)
    tt = _tile(t, 1024)
    return _mm(
        name, TN, a, dy, grid=(k // tko, 1, t // tt),
        a_spec=pl.BlockSpec((tt, tko), lambda i, j, kk: (kk, i)),
        b_spec=pl.BlockSpec((tt, n), lambda i, j, kk: (kk, 0)),
        out_shape=_grad_buf(buf, (n_layers, k, n)),
        out_spec=pl.BlockSpec((None, tko, n), lambda i, j, kk: (layer, i, 0)),
        acc_shape=(tko, n), alias=buf)


def _mm_dw_down(name, act, dy, buf, layer, n_layers):
    nblk, t, kb = act.shape
    n = dy.shape[1]
    tt = _tile(t, 1024)
    return _mm(
        name, TN, act, dy, grid=(nblk, 1, t // tt),
        a_spec=pl.BlockSpec((None, tt, kb), lambda i, j, kk: (i, kk, 0)),
        b_spec=pl.BlockSpec((tt, n), lambda i, j, kk: (kk, 0)),
        out_shape=_grad_buf(buf, (n_layers, nblk, kb, n)),
        out_spec=pl.BlockSpec((None, None, kb, n), lambda i, j, kk: (layer, i, 0, 0)),
        acc_shape=(kb, n), alias=buf)


def _spatial_mask(transposed=False):
    r = lax.broadcasted_iota(jnp.int32, (A_CHUNK, A_CHUNK), 0) // CHUNK
    c = lax.broadcasted_iota(jnp.int32, (A_CHUNK, A_CHUNK), 1) // CHUNK
    return c >= r if transposed else r >= c


def _sgu_tile(t):
    return _tile(t, 2 * A_CHUNK)


def _sgu_fwd(zpre, g_sgu, w_sp, b_full, name):
    t, f2 = zpre.shape
    f = f2 // 2
    gd = f // A_GROUPS
    tm = _sgu_tile(t)

    def body(z_ref, g_ref, w_ref, b_ref, p_ref):
        mask = _spatial_mask()
        wm = [jnp.where(mask, w_ref[g], 0.0).astype(BF16) for g in range(A_GROUPS)]
        for c in range(tm // A_CHUNK):
            rows = pl.ds(c * A_CHUNK, A_CHUNK)
            z = _gelu(z_ref[rows, :].astype(F32))
            u = z[:, :f]
            v0 = z[:, f:]
            r = lax.rsqrt(jnp.mean(v0 * v0, axis=-1, keepdims=True) + EPS)
            v1 = (v0 * r * g_ref[...]).astype(BF16)
            for g in range(A_GROUPS):
                cols = slice(g * gd, (g + 1) * gd)
                v2 = jnp.dot(wm[g], v1[:, cols], preferred_element_type=F32) + b_ref[:, cols]
                p_ref[rows, cols] = (u[:, cols] * v2).astype(BF16)

    return _call(
        body, name, (t // tm,),
        [pl.BlockSpec((tm, f2), lambda i: (i, 0)),
         pl.BlockSpec((1, f), lambda i: (0, 0)),
         pl.BlockSpec((A_GROUPS, A_CHUNK, A_CHUNK), lambda i: (0, 0, 0)),
         pl.BlockSpec((A_CHUNK, f), lambda i: (0, 0))],
        pl.BlockSpec((tm, f), lambda i: (i, 0)),
        jax.ShapeDtypeStruct((t, f), BF16),
        vmem_bytes=2 * _nbytes((tm, f2), BF16) + 2 * _nbytes((tm, f), BF16) + 8 * _nbytes((A_CHUNK, f2), F32),
    )(zpre, g_sgu.reshape(1, f), w_sp, b_full)


def _sgu_bwd(zpre, dp, g_sgu, w_sp, w_sp_t, b_full, name):
    t, f2 = zpre.shape
    f = f2 // 2
    gd = f // A_GROUPS
    tm = _sgu_tile(t)
    n_steps = t // tm

    def body(z_ref, dp_ref, g_ref, w_ref, wt_ref, b_ref, dz_ref, dw_ref, db_ref, dg_ref, dv1_ref, dbf_ref):
        step = pl.program_id(0)

        @pl.when(step == 0)
        def _():
            dw_ref[...] = jnp.zeros_like(dw_ref)
            dg_ref[...] = jnp.zeros_like(dg_ref)
            dbf_ref[...] = jnp.zeros_like(dbf_ref)

        mask = _spatial_mask()
        mask_t = _spatial_mask(transposed=True)
        wm = [jnp.where(mask, w_ref[g], 0.0).astype(BF16) for g in range(A_GROUPS)]
        wmt = [jnp.where(mask_t, wt_ref[g], 0.0).astype(BF16) for g in range(A_GROUPS)]
        gain = g_ref[...]
        for c in range(tm // A_CHUNK):
            rows = pl.ds(c * A_CHUNK, A_CHUNK)
            z, dgelu = _gelu_and_grad(z_ref[rows, :].astype(F32))
            u = z[:, :f]
            v0 = z[:, f:]
            r = lax.rsqrt(jnp.mean(v0 * v0, axis=-1, keepdims=True) + EPS)
            xhat = v0 * r
            v1 = (xhat * gain).astype(BF16)
            dpf = dp_ref[rows, :].astype(F32)
            for g in range(A_GROUPS):
                cols = slice(g * gd, (g + 1) * gd)
                v1g = v1[:, cols]
                v2 = jnp.dot(wm[g], v1g, preferred_element_type=F32) + b_ref[:, cols]
                dpg = dpf[:, cols]
                dz_ref[rows, cols] = (dpg * v2 * dgelu[:, cols]).astype(BF16)
                dv2 = dpg * u[:, cols]
                dbf_ref[:, cols] += dv2
                dv2b = dv2.astype(BF16)
                dwg = lax.dot_general(dv2b, v1g, NT, preferred_element_type=F32)
                dw_ref[g] += jnp.where(mask, dwg, 0.0)
                dv1_ref[:, cols] = jnp.dot(wmt[g], dv2b, preferred_element_type=F32)
            dv1 = dv1_ref[...]
            dxhat = dv1 * gain
            dg_ref[...] += jnp.sum(dv1 * xhat, axis=0, keepdims=True)
            dv0 = r * (dxhat - xhat * jnp.mean(dxhat * xhat, axis=-1, keepdims=True))
            dz_ref[rows, pl.ds(f, f)] = (dv0 * dgelu[:, f:]).astype(BF16)

        @pl.when(step == n_steps - 1)
        def _():
            for g in range(A_GROUPS):
                db_ref[g] = jnp.sum(dbf_ref[:, g * gd:(g + 1) * gd], axis=1, keepdims=True)

    wspec = pl.BlockSpec((A_GROUPS, A_CHUNK, A_CHUNK), lambda i: (0, 0, 0))
    dz, dw, db, dg = _call(
        body, name, (n_steps,),
        [pl.BlockSpec((tm, f2), lambda i: (i, 0)),
         pl.BlockSpec((tm, f), lambda i: (i, 0)),
         pl.BlockSpec((1, f), lambda i: (0, 0)),
         wspec, wspec,
         pl.BlockSpec((A_CHUNK, f), lambda i: (0, 0))],
        [pl.BlockSpec((tm, f2), lambda i: (i, 0)),
         wspec,
         pl.BlockSpec((A_GROUPS, A_CHUNK, 1), lambda i: (0, 0, 0)),
         pl.BlockSpec((1, f), lambda i: (0, 0))],
        [jax.ShapeDtypeStruct((t, f2), BF16),
         jax.ShapeDtypeStruct((A_GROUPS, A_CHUNK, A_CHUNK), F32),
         jax.ShapeDtypeStruct((A_GROUPS, A_CHUNK, 1), F32),
         jax.ShapeDtypeStruct((1, f), F32)],
        scratch=[pltpu.VMEM((A_CHUNK, f), F32), pltpu.VMEM((A_CHUNK, f), F32)],
        vmem_bytes=4 * _nbytes((tm, f2), BF16) + 2 * _nbytes((tm, f), BF16) + 12 * _nbytes((A_CHUNK, f2), F32),
    )(zpre, dp, g_sgu.reshape(1, f), w_sp, w_sp_t, b_full)
    return dz, dw, db.reshape(A_GROUPS, A_CHUNK), dg.reshape(f)


def _act_fwd(gu, name):
    _, t, nb = gu.shape
    half = N_DEV // 2
    tm = _tile(t, 1024)
    gu4 = gu.reshape(2, half, t, nb)

    def body(gu_ref, a_ref):
        gate = gu_ref[0].astype(F32)
        up = gu_ref[1].astype(F32)
        a_ref[...] = (gate * _sigmoid(gate) * up).astype(BF16)

    return _call(
        body, name, (half, t // tm),
        [pl.BlockSpec((2, None, tm, nb), lambda j, i: (0, j, i, 0))],
        pl.BlockSpec((None, tm, nb), lambda j, i: (j, i, 0)),
        jax.ShapeDtypeStruct((half, t, nb), BF16),
        vmem_bytes=6 * _nbytes((tm, nb), BF16) + 6 * _nbytes((tm, nb), F32),
    )(gu4)


def _act_bwd(gu, da, name):
    _, t, nb = gu.shape
    half = N_DEV // 2
    tm = _tile(t, 1024)
    gu4 = gu.reshape(2, half, t, nb)

    def body(gu_ref, da_ref, dgu_ref):
        gate = gu_ref[0].astype(F32)
        up = gu_ref[1].astype(F32)
        d = da_ref[...].astype(F32)
        sig = _sigmoid(gate)
        dgu_ref[0] = (d * up * (sig * (1.0 + gate * (1.0 - sig)))).astype(BF16)
        dgu_ref[1] = (d * (gate * sig)).astype(BF16)

    out = _call(
        body, name, (half, t // tm),
        [pl.BlockSpec((2, None, tm, nb), lambda j, i: (0, j, i, 0)),
         pl.BlockSpec((None, tm, nb), lambda j, i: (j, i, 0))],
        pl.BlockSpec((2, None, tm, nb), lambda j, i: (0, j, i, 0)),
        jax.ShapeDtypeStruct((2, half, t, nb), BF16),
        vmem_bytes=10 * _nbytes((tm, nb), BF16) + 8 * _nbytes((tm, nb), F32),
    )(gu4, da)
    return out.reshape(N_DEV, t, nb)


def _pair_valid(qi, col):
    qc = qi // CHUNK
    kc = col // CHUNK
    return (kc >= qc) & (kc <= qc + N_LEFT_CHUNKS)


def _pair_onehot(qi):
    col = lax.broadcasted_iota(jnp.int32, (REL_PAD, PAIR_BAND), 1)
    idx = jnp.clip(qi - (col - LEFT), -MAX_REL, MAX_REL) + MAX_REL
    r = lax.broadcasted_iota(jnp.int32, (REL_PAD, PAIR_BAND), 0)
    return jnp.where((r == idx) & _pair_valid(qi, col), 1.0, 0.0).astype(BF16)


def _bias_build(table, name):
    h = table.shape[0]
    tab = jnp.pad(table, ((0, 0), (0, REL_PAD - N_REL)))

    def body(t_ref, o_ref):
        parts = _split3(t_ref[...])
        col = lax.broadcasted_iota(jnp.int32, (h, PAIR_BAND), 1)

        def step(qi, carry):
            oh = _pair_onehot(qi)
            acc = jnp.dot(parts[0], oh, preferred_element_type=F32)
            acc += jnp.dot(parts[1], oh, preferred_element_type=F32)
            acc += jnp.dot(parts[2], oh, preferred_element_type=F32)
            o_ref[qi] = jnp.where(_pair_valid(qi, col), acc, NEG_INF)
            return carry

        lax.fori_loop(0, PAIR_ROWS, step, 0)

    out = _call(
        body, name, (1,),
        [pl.BlockSpec((h, REL_PAD), lambda i: (0, 0))],
        pl.BlockSpec((PAIR_ROWS, h, PAIR_BAND), lambda i: (0, 0, 0)),
        jax.ShapeDtypeStruct((PAIR_ROWS, h, PAIR_BAND), F32),
        vmem_bytes=4 * _nbytes((PAIR_ROWS, h, PAIR_BAND), F32),
    )(tab)
    return jnp.transpose(out, (1, 0, 2))


def _bias_block(pair_bias):
    rest = K_BLOCK - PAIR_BAND
    top = jnp.pad(pair_bias, ((0, 0), (0, 0), (0, rest)), constant_values=NEG_INF)
    bottom = jnp.pad(pair_bias, ((0, 0), (0, 0), (rest, 0)), constant_values=NEG_INF)
    return jnp.concatenate([top, bottom], axis=1)


def _bias_grad(dbias, name):
    h = dbias.shape[0]
    db_t = jnp.transpose(dbias, (1, 0, 2))

    def body(d_ref, o_ref):
        def step(qi, acc):
            oh = _pair_onehot(qi)
            for piece in _split3(d_ref[qi]):
                acc = acc + lax.dot_general(piece, oh, NT, preferred_element_type=F32)
            return acc

        o_ref[...] = lax.fori_loop(0, PAIR_ROWS, step, jnp.zeros((h, REL_PAD), F32))

    out = _call(
        body, name, (1,),
        [pl.BlockSpec((PAIR_ROWS, h, PAIR_BAND), lambda i: (0, 0, 0))],
        pl.BlockSpec((h, REL_PAD), lambda i: (0, 0)),
        jax.ShapeDtypeStruct((h, REL_PAD), F32),
        vmem_bytes=4 * _nbytes((PAIR_ROWS, h, PAIR_BAND), F32),
    )(db_t)
    return out[:, :N_REL]


def _head_masks():
    lane = lax.broadcasted_iota(jnp.int32, (Q_BLOCK, HEAD_PAIR), 1)
    return lane < HEAD_DIM, lane >= HEAD_DIM


def _block_probs(qm, kb, bias, valid):
    s = lax.dot_general(qm, kb, NT, preferred_element_type=F32) + bias
    s = jnp.where(valid, s, NEG_INF)
    e = jnp.exp(s - jnp.max(s, axis=-1, keepdims=True))
    return e * (1.0 / jnp.sum(e, axis=-1, keepdims=True))


def _attn_fwd(q, kvpad, bias, name):
    t, d = q.shape
    n_pairs = d // HEAD_PAIR
    n_blocks = t // Q_BLOCK

    def body(q_ref, k_ref, v_ref, b_ref, o_ref):
        masks = _head_masks()
        key = lax.broadcasted_iota(jnp.int32, (Q_BLOCK, K_BLOCK), 1)

        def step(j, carry):
            r0 = pl.multiple_of(j * Q_BLOCK, Q_BLOCK)
            q2 = q_ref[pl.ds(r0, Q_BLOCK), :].astype(F32)
            kb = k_ref[pl.ds(r0, K_BLOCK), :]
            vb = v_ref[pl.ds(r0, K_BLOCK), :]
            valid = key >= LEFT - j * Q_BLOCK
            outs = []
            for a in range(2):
                qm = jnp.where(masks[a], q2, 0.0).astype(BF16)
                p = _block_probs(qm, kb, b_ref[a], valid)
                outs.append(jnp.dot(p.astype(BF16), vb, preferred_element_type=F32))
            o_ref[pl.ds(r0, Q_BLOCK), :] = jnp.where(masks[0], outs[0], outs[1]).astype(BF16)
            return carry

        lax.fori_loop(0, n_blocks, step, 0)

    return _call(
        body, name, (n_pairs,),
        [pl.BlockSpec((t, HEAD_PAIR), lambda p: (0, p)),
         pl.BlockSpec((LEFT + t, HEAD_PAIR), lambda p: (0, p)),
         pl.BlockSpec((LEFT + t, HEAD_PAIR), lambda p: (0, n_pairs + p)),
         pl.BlockSpec((2, Q_BLOCK, K_BLOCK), lambda p: (p, 0, 0))],
        pl.BlockSpec((t, HEAD_PAIR), lambda p: (0, p)),
        jax.ShapeDtypeStruct((t, d), BF16),
        vmem_bytes=8 * _nbytes((LEFT + t, HEAD_PAIR), BF16) + 12 * _nbytes((2, Q_BLOCK, K_BLOCK), F32),
    )(q, kvpad, kvpad, bias)


def _attn_bwd(q, kvpad, bias, do, dk_in, dv_in, name):
    t, d = q.shape
    n_pairs = d // HEAD_PAIR
    n_blocks = t // Q_BLOCK
    has_in = dk_in is not None

    def body(*refs):
        refs = list(refs)
        q_ref, k_ref, v_ref, b_ref, do_ref = refs[:5]
        refs = refs[5:]
        if has_in:
            dki_ref, dvi_ref = refs[:2]
            refs = refs[2:]
        dq_ref, dk_ref, dv_ref, db_ref = refs
        masks = _head_masks()
        key = lax.broadcasted_iota(jnp.int32, (Q_BLOCK, K_BLOCK), 1)
        if has_in:
            dk_ref[...] = dki_ref[...]
            dv_ref[...] = dvi_ref[...]
        else:
            dk_ref[...] = jnp.zeros_like(dk_ref)
            dv_ref[...] = jnp.zeros_like(dv_ref)
        db_ref[...] = jnp.zeros_like(db_ref)

        def step(j, carry):
            r0 = pl.multiple_of(j * Q_BLOCK, Q_BLOCK)
            q2 = q_ref[pl.ds(r0, Q_BLOCK), :].astype(F32)
            do2 = do_ref[pl.ds(r0, Q_BLOCK), :].astype(F32)
            kb = k_ref[pl.ds(r0, K_BLOCK), :]
            vb = v_ref[pl.ds(r0, K_BLOCK), :]
            valid = key >= LEFT - j * Q_BLOCK
            dqs = []
            dk_acc = jnp.zeros((K_BLOCK, HEAD_PAIR), F32)
            dv_acc = jnp.zeros((K_BLOCK, HEAD_PAIR), F32)
            for a in range(2):
                qm = jnp.where(masks[a], q2, 0.0).astype(BF16)
                dom = jnp.where(masks[a], do2, 0.0).astype(BF16)
                p = _block_probs(qm, kb, b_ref[a], valid)
                dp = lax.dot_general(dom, vb, NT, preferred_element_type=F32)
                ds = p * (dp - jnp.sum(dp * p, axis=-1, keepdims=True))
                db_ref[a] += ds[:PAIR_ROWS, :PAIR_BAND] + ds[PAIR_ROWS:, K_BLOCK - PAIR_BAND:]
                dsb = ds.astype(BF16)
                dqs.append(jnp.dot(dsb, kb, preferred_element_type=F32))
                dk_acc += lax.dot_general(dsb, qm, TN, preferred_element_type=F32)
                dv_acc += lax.dot_general(p.astype(BF16), dom, TN, preferred_element_type=F32)
            dq = jnp.where(masks[0], dqs[0], dqs[1]) * ATTN_SCALE
            dq_ref[pl.ds(r0, Q_BLOCK), :] = dq.astype(BF16)
            dk_ref[pl.ds(r0, K_BLOCK), :] += dk_acc
            dv_ref[pl.ds(r0, K_BLOCK), :] += dv_acc
            return carry

        lax.fori_loop(0, n_blocks, step, 0)

    q_spec = pl.BlockSpec((t, HEAD_PAIR), lambda p: (0, p))
    kv_spec = pl.BlockSpec((LEFT + t, HEAD_PAIR), lambda p: (0, p))
    operands = [q, kvpad, kvpad, bias, do]
    in_specs = [q_spec, kv_spec, pl.BlockSpec((LEFT + t, HEAD_PAIR), lambda p: (0, n_pairs + p)),
                pl.BlockSpec((2, Q_BLOCK, K_BLOCK), lambda p: (p, 0, 0)), q_spec]
    aliases = None
    if has_in:
        operands += [dk_in, dv_in]
        in_specs += [kv_spec, kv_spec]
        aliases = {5: 1, 6: 2}
    return _call(
        body, name, (n_pairs,),
        in_specs,
        [q_spec, kv_spec, kv_spec, pl.BlockSpec((2, PAIR_ROWS, PAIR_BAND), lambda p: (p, 0, 0))],
        [jax.ShapeDtypeStruct((t, d), BF16),
         jax.ShapeDtypeStruct((LEFT + t, d), F32),
         jax.ShapeDtypeStruct((LEFT + t, d), F32),
         jax.ShapeDtypeStruct((d // HEAD_DIM, PAIR_ROWS, PAIR_BAND), F32)],
        vmem_bytes=10 * _nbytes((LEFT + t, HEAD_PAIR), BF16) + 8 * _nbytes((LEFT + t, HEAD_PAIR), F32)
        + 16 * _nbytes((2, Q_BLOCK, K_BLOCK), F32),
        aliases=aliases,
    )(*operands)


def _loss_head(x, g, target, name):
    t, d = x.shape
    tm = _tile(t, 512)

    def body(x_ref, g_ref, t_ref, dx_ref, loss_ref, dg_ref):
        @pl.when(pl.program_id(0) == 0)
        def _():
            loss_ref[...] = jnp.zeros_like(loss_ref)
            dg_ref[...] = jnp.zeros_like(dg_ref)

        xf = x_ref[...]
        r = lax.rsqrt(jnp.mean(xf * xf, axis=-1, keepdims=True) + EPS)
        xhat = xf * r
        diff = xhat * g_ref[...] - t_ref[...]
        row_loss = jnp.mean(diff * diff, axis=-1, keepdims=True)
        loss_ref[...] += 0.5 * jnp.sum(row_loss, axis=0, keepdims=True)
        dy = diff * (1.0 / d)
        dg_ref[...] += jnp.sum(dy * xhat, axis=0, keepdims=True)
        dxhat = dy * g_ref[...]
        dx_ref[...] = r * (dxhat - xhat * jnp.mean(dxhat * xhat, axis=-1, keepdims=True))

    row = pl.BlockSpec((tm, d), lambda i: (i, 0))
    vec = pl.BlockSpec((1, d), lambda i: (0, 0))
    dx, loss, dg = _call(
        body, name, (t // tm,),
        [row, vec, row],
        [row, pl.BlockSpec((1, 1), lambda i: (0, 0)), vec],
        [jax.ShapeDtypeStruct((t, d), F32), jax.ShapeDtypeStruct((1, 1), F32), jax.ShapeDtypeStruct((1, d), F32)],
        vmem_bytes=10 * _nbytes((tm, d), F32),
    )(x, g.reshape(1, d), target)
    return dx, loss[0, 0], dg.reshape(d)


def _adamw(parts, w, m, v, name):
    n_layers, n_src, r, c = parts.shape
    mult = 16 if parts.dtype == BF16 else 8
    tr = _row_tile(r, max(mult, (256 * 1024) // c), mult)
    c1 = 1.0 / (1.0 - ADAM_B1 ** ADAM_STEP)
    c2 = 1.0 / (1.0 - ADAM_B2 ** ADAM_STEP)

    def body(p_ref, w_ref, m_ref, v_ref, g_ref, d_ref, nm_ref, nv_ref):
        g = p_ref[0].astype(F32)
        for s in range(1, n_src):
            g = g + p_ref[s].astype(F32)
        nm = ADAM_B1 * m_ref[...] + (1.0 - ADAM_B1) * g
        nv = ADAM_B2 * v_ref[...] + (1.0 - ADAM_B2) * (g * g)
        g_ref[...] = g
        nm_ref[...] = nm
        nv_ref[...] = nv
        d_ref[...] = -ADAM_LR * ((nm * c1) / (jnp.sqrt(nv * c2) + ADAM_EPS) + ADAM_WD * w_ref[...])

    blk = pl.BlockSpec((None, tr, c), lambda l, i: (l, i, 0))
    out = jax.ShapeDtypeStruct((n_layers, r, c), F32)
    return _call(
        body, name, (n_layers, r // tr),
        [pl.BlockSpec((None, n_src, tr, c), lambda l, i: (l, 0, i, 0)), blk, blk, blk],
        [blk, blk, blk, blk],
        [out, out, out, out],
        vmem_bytes=2 * _nbytes((n_src, tr, c), parts.dtype) + 18 * _nbytes((tr, c), F32),
    )(parts, w, m, v)


def _ordered_sum(parts, name):
    n_src, r, c = parts.shape

    def body(p_ref, o_ref):
        acc = p_ref[0]
        for s in range(1, n_src):
            acc = acc + p_ref[s]
        o_ref[...] = acc

    return _call(
        body, name, (1,),
        [pl.BlockSpec((n_src, r, c), lambda i: (0, 0, 0))],
        pl.BlockSpec((r, c), lambda i: (0, 0)),
        jax.ShapeDtypeStruct((r, c), F32),
        vmem_bytes=4 * _nbytes((n_src, r, c), F32),
    )(parts)


def _position():
    return lax.axis_index("x"), lax.axis_index("y"), lax.axis_index("c")


def _linear(p):
    return 4 * p[0] + 2 * p[1] + p[2]


def _all_gather(shards, name):
    n = len(shards)

    def body(*refs):
        ins, outs = refs[:n], refs[n:2 * n]
        send_sems, recv_sems, local_sems = refs[2 * n:]
        x, y, c = _position()
        me, sibling = (x, y, c), (x, y, 1 - c)
        chips = [(1 - x, y), (x, 1 - y), (1 - x, 1 - y)]

        def slab(t, p):
            return outs[t].at[:, _linear(p)]

        def copy(t, k, block, to, src=None):
            return pltpu.make_async_remote_copy(
                src_ref=slab(t, block) if src is None else src,
                dst_ref=slab(t, block),
                send_sem=send_sems.at[t, k],
                recv_sem=recv_sems.at[t, k],
                device_id=to,
                device_id_type=MESH,
            )

        started = []
        for t in range(n):
            mine = pltpu.make_async_copy(ins[t], slab(t, me), local_sems.at[t])
            mine.start()
            started.append(mine)
        sends = []
        for t in range(n):
            first = [copy(t, 0, me, sibling, src=ins[t])]
            first += [copy(t, 1 + j, me, (*chip, c), src=ins[t]) for j, chip in enumerate(chips)]
            for cp in first:
                cp.start()
            sends += first
        for t in range(n):
            for j, chip in enumerate(chips):
                copy(t, 1 + j, (*chip, c), me).wait_recv()
                passed = copy(t, 4 + j, (*chip, c), sibling)
                passed.start()
                sends.append(passed)
        for t in range(n):
            copy(t, 0, sibling, me).wait_recv()
            for j, chip in enumerate(chips):
                copy(t, 4 + j, (*chip, 1 - c), me).wait_recv()
        for cp in sends:
            cp.wait_send()
        for mine in started:
            mine.wait()

    out_shape = [jax.ShapeDtypeStruct((s.shape[0], N_DEV) + s.shape[1:], s.dtype) for s in shards]
    return pl.pallas_call(
        body,
        name=name,
        in_specs=[HBM_SPEC] * n,
        out_specs=[HBM_SPEC] * n,
        out_shape=out_shape,
        scratch_shapes=[
            pltpu.SemaphoreType.DMA((n, N_DEV - 1)),
            pltpu.SemaphoreType.DMA((n, N_DEV - 1)),
            pltpu.SemaphoreType.DMA((n,)),
        ],
    )(*shards)


def _exchange(blocks, name):
    n = len(blocks)

    def body(*refs):
        ins, outs = refs[:n], refs[n:2 * n]
        send_sems, recv_sems, local_sems = refs[2 * n:]
        x, y, c = _position()
        me = _linear((x, y, c))
        flips = [(fx, fy, fc) for fx in (0, 1) for fy in (0, 1) for fc in (0, 1)][1:]

        def peer_of(flip):
            fx, fy, fc = flip
            return (1 - x if fx else x, 1 - y if fy else y, 1 - c if fc else c)

        def copy(t, k, peer):
            return pltpu.make_async_remote_copy(
                src_ref=ins[t].at[:, _linear(peer)],
                dst_ref=outs[t].at[:, me],
                send_sem=send_sems.at[t, k],
                recv_sem=recv_sems.at[t, k],
                device_id=peer,
                device_id_type=MESH,
            )

        def arrival(t, k, peer):
            return pltpu.make_async_remote_copy(
                src_ref=ins[t].at[:, _linear(peer)],
                dst_ref=outs[t].at[:, _linear(peer)],
                send_sem=send_sems.at[t, k],
                recv_sem=recv_sems.at[t, k],
                device_id=peer,
                device_id_type=MESH,
            )

        own = []
        for t in range(n):
            cp = pltpu.make_async_copy(ins[t].at[:, me], outs[t].at[:, me], local_sems.at[t])
            cp.start()
            own.append(cp)
        sends = []
        for t in range(n):
            for k, flip in enumerate(flips):
                cp = copy(t, k, peer_of(flip))
                cp.start()
                sends.append(cp)
        for t in range(n):
            for k, flip in enumerate(flips):
                arrival(t, k, peer_of(flip)).wait_recv()
        for cp in sends:
            cp.wait_send()
        for cp in own:
            cp.wait()

    out_shape = [jax.ShapeDtypeStruct(b.shape, b.dtype) for b in blocks]
    return pl.pallas_call(
        body,
        name=name,
        in_specs=[HBM_SPEC] * n,
        out_specs=[HBM_SPEC] * n,
        out_shape=out_shape,
        scratch_shapes=[
            pltpu.SemaphoreType.DMA((n, N_DEV - 1)),
            pltpu.SemaphoreType.DMA((n, N_DEV - 1)),
            pltpu.SemaphoreType.DMA((n,)),
        ],
    )(*blocks)


def _pack(arrays, row_multiple):
    flat = jnp.concatenate([a.reshape(-1) for a in arrays])
    quantum = row_multiple * FLAT_LANES
    padded = -(-flat.shape[0] // quantum) * quantum
    return jnp.pad(flat, (0, padded - flat.shape[0])).reshape(-1, FLAT_LANES)


def _unpack(flat, like):
    flat = flat.reshape(-1)
    out, at = [], 0
    for a in like:
        size = math.prod(a.shape)
        out.append(flat[at:at + size].reshape(a.shape))
        at += size
    return out


def kernel(x, a_norm, a_w_in, a_sgu_norm, a_w_spatial, a_b_spatial, a_w_out, kv_norm, w_kv, b_norm, b_w_q, b_rel_bias, b_w_o, ffn_norm, ffn_w_gate_up, ffn_w_down, final_norm, loss_target, m_a_norm, m_a_w_in, m_a_sgu_norm, m_a_w_spatial, m_a_b_spatial, m_a_w_out, m_kv_norm, m_w_kv, m_b_norm, m_b_w_q, m_b_rel_bias, m_b_w_o, m_ffn_norm, m_ffn_w_gate_up, m_ffn_w_down, m_final_norm, v_a_norm, v_a_w_in, v_a_sgu_norm, v_a_w_spatial, v_a_b_spatial, v_a_w_out, v_kv_norm, v_w_kv, v_b_norm, v_b_w_q, v_b_rel_bias, v_b_w_o, v_ffn_norm, v_ffn_w_gate_up, v_ffn_w_down, v_final_norm):
    xs = x[0]
    target = loss_target[0]
    t, d = xs.shape
    n_a = a_w_in.shape[0]
    n_b = b_w_q.shape[0]
    depth = ffn_w_gate_up.shape[0]
    f_a = a_w_out.shape[1] * N_DEV
    gd = f_a // A_GROUPS
    nb_ffn = ffn_w_gate_up.shape[2]
    me = _linear(_position())

    small_rows = -(-(a_norm.size + a_sgu_norm.size) // (8 * 128)) * 8
    small = jnp.pad(jnp.concatenate([a_norm.reshape(-1), a_sgu_norm.reshape(-1)]),
                    (0, small_rows * 128 - a_norm.size - a_sgu_norm.size)).reshape(1, small_rows, 128)
    big = [a_w_in, a_w_out, w_kv[None], b_w_q, b_w_o, ffn_w_gate_up, ffn_w_down]
    gathered = _all_gather([w.astype(BF16) for w in big] + [small], "gather_weights")
    win_g, wout_g, wkv_g, wq_g, wo_g, wgu_g, wd_g, small_g = gathered
    wout_n = wout_g.reshape(n_a, f_a, d)
    wq_n = wq_g.reshape(n_b, d, d)
    wo_n = wo_g.reshape(n_b, d, d)
    wd_4 = wd_g.reshape(depth, N_DEV // 2, 2 * ffn_w_down.shape[1], d)
    small_g = small_g.reshape(N_DEV, -1)
    a_norm_full = small_g[:, :a_norm.size].reshape(N_DEV, n_a, -1).transpose(1, 0, 2).reshape(n_a, d)
    a_sgu_full = small_g[:, a_norm.size:a_norm.size + a_sgu_norm.size].reshape(
        N_DEV, n_a, -1).transpose(1, 0, 2).reshape(n_a, f_a)

    w_sp_t = jnp.swapaxes(a_w_spatial, -1, -2)
    b_full = jnp.repeat(jnp.swapaxes(a_b_spatial, -1, -2), gd, axis=-1)

    saved = []

    def ffn_fwd(xin, layer):
        hf = _rms_fwd(xin, ffn_norm[layer], f"ffn_norm_fwd_{layer}")
        gu = _mm_colblock(f"ffn_gate_up_{layer}", hf, wgu_g, layer, blocked_out=True)
        act = _act_fwd(gu, f"ffn_act_fwd_{layer}")
        xout = _mm_down(f"ffn_down_{layer}", act, wd_4, layer, xin)
        return xout, (xin, hf, gu, act)

    for i in range(n_a):
        h = _rms_fwd(xs, a_norm_full[i], f"a_norm_fwd_{i}")
        zpre = _mm_colblock(f"a_in_{i}", h, win_g, i)
        p = _sgu_fwd(zpre, a_sgu_full[i], a_w_spatial[i], b_full[i], f"a_sgu_fwd_{i}")
        x_mid = _mm_natural(f"a_out_{i}", p, wout_n, i, res=xs)
        x_out, ffn_saved = ffn_fwd(x_mid, i)
        saved.append((xs, h, zpre, p, ffn_saved))
        xs = x_out

    x_kv = xs
    h_kv = _rms_fwd(x_kv, kv_norm, "kv_norm_fwd")
    kv = _mm_colblock("kv_proj", h_kv, wkv_g, 0)
    kvpad = jnp.pad(kv, ((LEFT, 0), (0, 0)))

    biases = [_bias_block(_bias_build(b_rel_bias[i], f"rel_bias_{i}")) for i in range(n_b)]
    for i in range(n_b):
        layer = n_a + i
        hb = _rms_fwd(xs, b_norm[i], f"b_norm_fwd_{i}")
        q = _mm_natural(f"b_q_{i}", hb, wq_n, i, out_dtype=BF16, scale=ATTN_SCALE)
        o = _attn_fwd(q, kvpad, biases[i], f"b_attn_fwd_{i}")
        x_mid = _mm_natural(f"b_o_{i}", o, wo_n, i, res=xs)
        x_out, ffn_saved = ffn_fwd(x_mid, layer)
        saved.append((xs, hb, q, o, ffn_saved))
        xs = x_out

    dx, loss_local, g_final = _loss_head(xs, final_norm, target, "loss_head")
    loss = lax.psum(loss_local, ("x", "y", "c"))

    g_win = g_wout = g_wkv = g_wq = g_wo = g_wgu = g_wd = None
    g_ffn_norm = [None] * depth
    g_a_norm = [None] * n_a
    g_a_sgu = [None] * n_a
    g_w_sp = [None] * n_a
    g_b_sp = [None] * n_a
    g_b_norm = [None] * n_b
    g_rel = [None] * n_b

    def ffn_bwd(dx, layer, ffn_saved):
        nonlocal g_wgu, g_wd
        xin, hf, gu, act = ffn_saved
        g_wd = _mm_dw_down(f"ffn_down_dw_{layer}", act, dx, g_wd, layer, depth)
        da = _mm_t_down(f"ffn_down_dx_{layer}", dx, wd_4, layer)
        dgu = _act_bwd(gu, da, f"ffn_act_bwd_{layer}")
        g_wgu = _mm_dw_colblock(f"ffn_gate_up_dw_{layer}", hf, dgu, g_wgu, layer, depth, blocked_in=True)
        dh = _mm_t_colblock(f"ffn_gate_up_dx_{layer}", dgu, wgu_g, layer, blocked_in=True)
        dx, g_ffn_norm[layer] = _rms_bwd(xin, ffn_norm[layer], dh, dx, f"ffn_norm_bwd_{layer}")
        return dx

    dk = dv = None
    for i in reversed(range(n_b)):
        layer = n_a + i
        x_in, hb, q, o, ffn_saved = saved[layer]
        dx = ffn_bwd(dx, layer, ffn_saved)
        g_wo = _mm_dw_natural(f"b_o_dw_{i}", o, dx, g_wo, i, n_b)
        do = _mm_t_natural(f"b_o_dx_{i}", dx, wo_n, i)
        dq, dk, dv, dbias = _attn_bwd(q, kvpad, biases[i], do, dk, dv, f"b_attn_bwd_{i}")
        g_rel[i] = _bias_grad(dbias, f"rel_bias_grad_{i}")
        g_wq = _mm_dw_natural(f"b_q_dw_{i}", hb, dq, g_wq, i, n_b)
        dh = _mm_t_natural(f"b_q_dx_{i}", dq, wq_n, i)
        dx, g_b_norm[i] = _rms_bwd(x_in, b_norm[i], dh, dx, f"b_norm_bwd_{i}")

    dkv = jnp.concatenate([dk[LEFT:], dv[LEFT:]], axis=1).astype(BF16)
    g_wkv = _mm_dw_colblock("kv_proj_dw", h_kv, dkv, None, 0, 1)
    dh = _mm_t_colblock("kv_proj_dx", dkv, wkv_g, 0)
    dx, g_kv_norm = _rms_bwd(x_kv, kv_norm, dh, dx, "kv_norm_bwd")

    for i in reversed(range(n_a)):
        x_in, h, zpre, p, ffn_saved = saved[i]
        dx = ffn_bwd(dx, i, ffn_saved)
        g_wout = _mm_dw_natural(f"a_out_dw_{i}", p, dx, g_wout, i, n_a)
        dp = _mm_t_natural(f"a_out_dx_{i}", dx, wout_n, i)
        dz, g_w_sp[i], g_b_sp[i], g_a_sgu[i] = _sgu_bwd(
            zpre, dp, a_sgu_full[i], a_w_spatial[i], w_sp_t[i], b_full[i], f"a_sgu_bwd_{i}")
        g_win = _mm_dw_colblock(f"a_in_dw_{i}", h, dz, g_win, i, n_a)
        dh = _mm_t_colblock(f"a_in_dx_{i}", dz, win_g, i)
        dx, g_a_norm[i] = _rms_bwd(x_in, a_norm_full[i], dh, dx, f"a_norm_bwd_{i}")
    grad_x = dx[None]

    small_like = [jax.ShapeDtypeStruct((n_a, d), F32), jax.ShapeDtypeStruct((n_a, f_a), F32),
                  a_w_spatial, a_b_spatial, kv_norm, b_norm, b_rel_bias, ffn_norm, final_norm]
    small_partial = _pack(
        [jnp.stack(g_a_norm), jnp.stack(g_a_sgu), jnp.stack(g_w_sp), jnp.stack(g_b_sp), g_kv_norm,
         jnp.stack(g_b_norm), jnp.stack(g_rel), jnp.stack(g_ffn_norm), g_final], N_DEV * 8)
    chunk_rows = small_partial.shape[0] // N_DEV
    big_grads = [
        g_win,
        g_wout.reshape(n_a, N_DEV, f_a // N_DEV, d),
        g_wkv,
        g_wq.reshape(n_b, N_DEV, d // N_DEV, d),
        g_wo.reshape(n_b, N_DEV, d // N_DEV, d),
        g_wgu,
        g_wd.reshape(depth, N_DEV, ffn_w_down.shape[1], d),
    ]
    received = _exchange(big_grads + [small_partial.reshape(1, N_DEV, chunk_rows, FLAT_LANES)], "exchange_grads")
    small_sum = _ordered_sum(received[-1][0], "small_grad_sum")
    small_all = _all_gather([small_sum[None]], "gather_small_grads")[0]
    (ga_norm, ga_sgu, gw_sp, gb_sp, gkv_norm, gb_norm, g_relb, gffn_norm, gfinal) = _unpack(small_all, small_like)

    results = {}
    big_names = ["a_w_in", "a_w_out", "w_kv", "b_w_q", "b_w_o", "ffn_w_gate_up", "ffn_w_down"]
    big_wmv = [(a_w_in, m_a_w_in, v_a_w_in), (a_w_out, m_a_w_out, v_a_w_out),
               (w_kv[None], m_w_kv[None], v_w_kv[None]), (b_w_q, m_b_w_q, v_b_w_q), (b_w_o, m_b_w_o, v_b_w_o),
               (ffn_w_gate_up, m_ffn_w_gate_up, v_ffn_w_gate_up), (ffn_w_down, m_ffn_w_down, v_ffn_w_down)]
    for name, parts, (w, m, v) in zip(big_names, received[:-1], big_wmv):
        outs = _adamw(parts, w, m, v, f"adamw_{name}")
        if name == "w_kv":
            outs = [o[0] for o in outs]
        results[name] = outs

    n_cols = a_norm.shape[1]
    s_cols = a_sgu_norm.shape[1]
    small_g_list = [lax.dynamic_slice(ga_norm, (0, me * n_cols), (n_a, n_cols)),
                    lax.dynamic_slice(ga_sgu, (0, me * s_cols), (n_a, s_cols)),
                    gw_sp, gb_sp, gkv_norm, gb_norm, g_relb, gffn_norm, gfinal]
    small_names = ["a_norm", "a_sgu_norm", "a_w_spatial", "a_b_spatial", "kv_norm", "b_norm", "b_rel_bias",
                   "ffn_norm", "final_norm"]
    small_w = [a_norm, a_sgu_norm, a_w_spatial, a_b_spatial, kv_norm, b_norm, b_rel_bias, ffn_norm, final_norm]
    small_m = [m_a_norm, m_a_sgu_norm, m_a_w_spatial, m_a_b_spatial, m_kv_norm, m_b_norm, m_b_rel_bias,
               m_ffn_norm, m_final_norm]
    small_v = [v_a_norm, v_a_sgu_norm, v_a_w_spatial, v_a_b_spatial, v_kv_norm, v_b_norm, v_b_rel_bias,
               v_ffn_norm, v_final_norm]
    flat_g = _pack(small_g_list, 8)
    flat_out = _adamw(flat_g[None, None], _pack(small_w, 8)[None], _pack(small_m, 8)[None],
                      _pack(small_v, 8)[None], "adamw_small")
    unpacked = [_unpack(o[0], small_w) for o in flat_out]
    for idx, name in enumerate(small_names):
        results[name] = [unpacked[kind][idx] for kind in range(4)]

    order = ["a_norm", "a_w_in", "a_sgu_norm", "a_w_spatial", "a_b_spatial", "a_w_out", "kv_norm", "w_kv",
             "b_norm", "b_w_q", "b_rel_bias", "b_w_o", "ffn_norm", "ffn_w_gate_up", "ffn_w_down", "final_norm"]
    outputs = [loss, grad_x]
    for kind in range(4):
        outputs += [results[name][kind] for name in order]
    return tuple(outputs)
```

```python
import math

import jax
import jax.numpy as jnp
from jax import lax
from jax.experimental import pallas as pl
from jax.experimental.pallas import tpu as pltpu

F32 = jnp.float32
BF16 = jnp.bfloat16
MESH = pl.DeviceIdType.MESH
HBM_SPEC = pl.BlockSpec(memory_space=pltpu.HBM)

N_DEV = 8
CHUNK = 64
A_CHUNK = 128
A_GROUPS = 8
N_LEFT_CHUNKS = 8
LEFT = N_LEFT_CHUNKS * CHUNK
PAIR_ROWS = 2 * CHUNK
PAIR_BAND = PAIR_ROWS + LEFT
Q_BLOCK = 2 * PAIR_ROWS
K_BLOCK = Q_BLOCK + LEFT
MAX_REL = 256
N_REL = 2 * MAX_REL + 1
REL_PAD = 640
HEAD_DIM = 64
HEAD_PAIR = 2 * HEAD_DIM
ATTN_SCALE = HEAD_DIM ** -0.5
EPS = 1e-6
NEG_INF = -1e30
ADAM_LR = 0.001
ADAM_B1 = 0.9
ADAM_B2 = 0.999
ADAM_EPS = 1e-08
ADAM_WD = 0.01
ADAM_STEP = 10
FLAT_LANES = 1024
V7X_VMEM_BYTES = 64 * 1024 * 1024
VMEM_FLOOR_BYTES = 32 * 1024 * 1024
VMEM_CEIL_BYTES = V7X_VMEM_BYTES - 8 * 1024 * 1024

NN = (((1,), (0,)), ((), ()))
NT = (((1,), (1,)), ((), ()))
TN = (((0,), (0,)), ((), ()))


def _tile(n, pref):
    return pref if n % pref == 0 else n


def _row_tile(n, pref, mult):
    best = None
    for t in range(mult, min(n, pref) + 1, mult):
        if n % t == 0:
            best = t
    return best if best is not None else n


def _nbytes(shape, dtype):
    n = 1
    for s in shape:
        if s is not None:
            n *= s
    return n * jnp.dtype(dtype).itemsize


def _call(body, name, grid, in_specs, out_specs, out_shape, scratch=(), vmem_bytes=0, aliases=None):
    limit = int(min(max(VMEM_FLOOR_BYTES, vmem_bytes * 5 // 4), VMEM_CEIL_BYTES))
    return pl.pallas_call(
        body,
        name=name,
        grid=grid,
        in_specs=in_specs,
        out_specs=out_specs,
        out_shape=out_shape,
        scratch_shapes=list(scratch),
        input_output_aliases=aliases or {},
        compiler_params=pltpu.CompilerParams(
            dimension_semantics=("arbitrary",) * len(grid), vmem_limit_bytes=limit),
    )


def _erf_parts(x):
    ax = jnp.abs(x) * (1.0 / math.sqrt(2.0))
    t = 1.0 / (1.0 + 0.3275911 * ax)
    poly = ((((1.061405429 * t - 1.453152027) * t + 1.421413741) * t - 0.284496736) * t + 0.254829592) * t
    ex = jnp.exp(-ax * ax)
    erf_abs = 1.0 - poly * ex
    return jnp.where(x < 0, -erf_abs, erf_abs), ex


def _gelu_and_grad(x):
    erf, ex = _erf_parts(x)
    cdf = 0.5 * (1.0 + erf)
    return x * cdf, cdf + x * ex * (1.0 / math.sqrt(2.0 * math.pi))


def _gelu(x):
    erf, _ = _erf_parts(x)
    return x * (0.5 * (1.0 + erf))


def _sigmoid(x):
    return 1.0 / (1.0 + jnp.exp(-x))


def _split3(x):
    hi = x.astype(BF16)
    r1 = x - hi.astype(F32)
    mid = r1.astype(BF16)
    lo = (r1 - mid.astype(F32)).astype(BF16)
    return hi, mid, lo


def _rms_fwd(x, g, name):
    t, d = x.shape
    tm = _tile(t, 512)

    def body(x_ref, g_ref, o_ref):
        xf = x_ref[...]
        r = lax.rsqrt(jnp.mean(xf * xf, axis=-1, keepdims=True) + EPS)
        o_ref[...] = (xf * r * g_ref[...]).astype(o_ref.dtype)

    return _call(
        body, name, (t // tm,),
        [pl.BlockSpec((tm, d), lambda i: (i, 0)), pl.BlockSpec((1, d), lambda i: (0, 0))],
        pl.BlockSpec((tm, d), lambda i: (i, 0)),
        jax.ShapeDtypeStruct((t, d), BF16),
        vmem_bytes=2 * (_nbytes((tm, d), F32) + _nbytes((tm, d), BF16)) + 4 * _nbytes((tm, d), F32),
    )(x, g.reshape(1, d))


def _rms_bwd(x, g, dh, dx_up, name):
    t, d = x.shape
    tm = _tile(t, 512)

    def body(x_ref, g_ref, dh_ref, up_ref, dx_ref, dg_ref):
        @pl.when(pl.program_id(0) == 0)
        def _():
            dg_ref[...] = jnp.zeros_like(dg_ref)

        xf = x_ref[...]
        r = lax.rsqrt(jnp.mean(xf * xf, axis=-1, keepdims=True) + EPS)
        xhat = xf * r
        dy = dh_ref[...].astype(F32)
        dxhat = dy * g_ref[...]
        dg_ref[...] += jnp.sum(dy * xhat, axis=0, keepdims=True)
        dx = r * (dxhat - xhat * jnp.mean(dxhat * xhat, axis=-1, keepdims=True))
        dx_ref[...] = up_ref[...] + dx

    row = pl.BlockSpec((tm, d), lambda i: (i, 0))
    vec = pl.BlockSpec((1, d), lambda i: (0, 0))
    dx, dg = _call(
        body, name, (t // tm,),
        [row, vec, row, row],
        [row, vec],
        [jax.ShapeDtypeStruct((t, d), F32), jax.ShapeDtypeStruct((1, d), F32)],
        vmem_bytes=10 * _nbytes((tm, d), F32),
    )(x, g.reshape(1, d), dh, dx_up)
    return dx, dg.reshape(d)


def _mm(name, dims, a, b, *, grid, a_spec, b_spec, out_shape, out_spec, acc_shape,
        res=None, res_spec=None, alias=None, scale=None):
    nk = grid[2]
    has_res = res is not None
    has_alias = alias is not None

    def body(*refs):
        refs = list(refs)
        a_ref = refs.pop(0)
        b_ref = refs.pop(0)
        r_ref = refs.pop(0) if has_res else None
        if has_alias:
            refs.pop(0)
        o_ref = refs.pop(0)
        part = lax.dot_general(a_ref[...].astype(BF16), b_ref[...].astype(BF16), dims,
                               preferred_element_type=F32)

        def finish(acc):
            if scale is not None:
                acc = acc * scale
            if has_res:
                acc = acc + r_ref[...]
            o_ref[...] = acc.astype(o_ref.dtype)

        if nk == 1:
            finish(part)
        else:
            acc_ref = refs.pop(0)
            k = pl.program_id(2)

            @pl.when(k == 0)
            def _():
                acc_ref[...] = part

            @pl.when(k > 0)
            def _():
                acc_ref[...] += part

            @pl.when(k == nk - 1)
            def _():
                finish(acc_ref[...])

    operands = [a, b]
    in_specs = [a_spec, b_spec]
    vmem = 2 * (_nbytes(a_spec.block_shape, a.dtype) + _nbytes(b_spec.block_shape, b.dtype)
                + _nbytes(out_spec.block_shape, out_shape.dtype))
    vmem += 3 * _nbytes(acc_shape, F32)
    if has_res:
        operands.append(res)
        in_specs.append(res_spec)
        vmem += 2 * _nbytes(res_spec.block_shape, res.dtype)
    aliases = None
    if has_alias:
        aliases = {len(operands): 0}
        operands.append(alias)
        in_specs.append(pl.BlockSpec(memory_space=pl.ANY))
    scratch = [pltpu.VMEM(acc_shape, F32)] if nk > 1 else []
    return _call(body, name, grid, in_specs, out_spec, out_shape, scratch=scratch,
                 vmem_bytes=vmem, aliases=aliases)(*operands)


def _mm_colblock(name, h, w_g, layer):
    t, k = h.shape
    nb = w_g.shape[3]
    tm = _tile(t, 2048)
    return _mm(
        name, NN, h, w_g, grid=(t // tm, N_DEV, 1),
        a_spec=pl.BlockSpec((tm, k), lambda i, j, kk: (i, 0)),
        b_spec=pl.BlockSpec((None, None, k, nb), lambda i, j, kk: (layer, j, 0, 0)),
        out_shape=jax.ShapeDtypeStruct((t, N_DEV * nb), BF16),
        out_spec=pl.BlockSpec((tm, nb), lambda i, j, kk: (i, j)), acc_shape=(tm, nb))


def _mm_natural(name, a, w, layer, *, res=None, out_dtype=F32, scale=None):
    t, k = a.shape
    n = w.shape[2]
    tm = _tile(t, 1024)
    tn = _tile(n, 512)
    res_spec = None if res is None else pl.BlockSpec((tm, tn), lambda i, j, kk: (i, j))
    return _mm(
        name, NN, a, w, grid=(t // tm, n // tn, 1),
        a_spec=pl.BlockSpec((tm, k), lambda i, j, kk: (i, 0)),
        b_spec=pl.BlockSpec((None, k, tn), lambda i, j, kk: (layer, 0, j)),
        out_shape=jax.ShapeDtypeStruct((t, n), out_dtype),
        out_spec=pl.BlockSpec((tm, tn), lambda i, j, kk: (i, j)),
        acc_shape=(tm, tn), res=res, res_spec=res_spec, scale=scale)


def _mm_down(name, act, w4, layer, res):
    nblk, t, kb = act.shape
    n = w4.shape[3]
    tm = _tile(t, 1024)
    tn = _tile(n, 1024)
    return _mm(
        name, NN, act, w4, grid=(t // tm, n // tn, nblk),
        a_spec=pl.BlockSpec((None, tm, kb), lambda i, j, kk: (kk, i, 0)),
        b_spec=pl.BlockSpec((None, None, kb, tn), lambda i, j, kk: (layer, kk, 0, j)),
        out_shape=jax.ShapeDtypeStruct((t, n), F32),
        out_spec=pl.BlockSpec((tm, tn), lambda i, j, kk: (i, j)),
        acc_shape=(tm, tn), res=res, res_spec=pl.BlockSpec((tm, tn), lambda i, j, kk: (i, j)))


def _mm_t_colblock_norm_bwd(name, dz, w_g, layer, x, g, dx_up, blocked_in=False):
    k = w_g.shape[2]
    nb = w_g.shape[3]
    t = x.shape[0]
    tm = _tile(t, 1024)
    if blocked_in:
        a_spec = pl.BlockSpec((None, tm, nb), lambda i, kk: (kk, i, 0))
    else:
        a_spec = pl.BlockSpec((tm, nb), lambda i, kk: (i, kk))

    def body(a_ref, b_ref, x_ref, g_ref, up_ref, dx_ref, dg_ref, acc_ref):
        i = pl.program_id(0)
        kk = pl.program_id(1)
        part = lax.dot_general(a_ref[...].astype(BF16), b_ref[...].astype(BF16), NT, preferred_element_type=F32)

        @pl.when(kk == 0)
        def _():
            acc_ref[...] = part

        @pl.when(kk > 0)
        def _():
            acc_ref[...] += part

        @pl.when((i == 0) & (kk == 0))
        def _():
            dg_ref[...] = jnp.zeros_like(dg_ref)

        @pl.when(kk == N_DEV - 1)
        def _():
            dy = acc_ref[...]
            xf = x_ref[...]
            r = lax.rsqrt(jnp.mean(xf * xf, axis=-1, keepdims=True) + EPS)
            xhat = xf * r
            dxhat = dy * g_ref[...]
            dg_ref[...] += jnp.sum(dy * xhat, axis=0, keepdims=True)
            dx_ref[...] = up_ref[...] + r * (dxhat - xhat * jnp.mean(dxhat * xhat, axis=-1, keepdims=True))

    row = pl.BlockSpec((tm, k), lambda i, kk: (i, 0))
    vec = pl.BlockSpec((1, k), lambda i, kk: (0, 0))
    dx, dg = _call(
        body, name, (t // tm, N_DEV),
        [a_spec, pl.BlockSpec((None, None, k, nb), lambda i, kk: (layer, kk, 0, 0)), row, vec, row],
        [row, vec],
        [jax.ShapeDtypeStruct((t, k), F32), jax.ShapeDtypeStruct((1, k), F32)],
        scratch=[pltpu.VMEM((tm, k), F32)],
        vmem_bytes=2 * (_nbytes((tm, nb), BF16) + _nbytes((k, nb), BF16)) + 10 * _nbytes((tm, k), F32),
    )(dz, w_g, x, g.reshape(1, k), dx_up)
    return dx, dg.reshape(k)


def _ffn_gate_up(name, h, w_g, layer):
    t, k = h.shape
    nb = w_g.shape[3]
    half = N_DEV // 2
    tm = _tile(t, 1024)

    def body(h_ref, wg_ref, wu_ref, gu_ref, act_ref):
        hb = h_ref[...]
        gate = jnp.dot(hb, wg_ref[...], preferred_element_type=F32)
        up = jnp.dot(hb, wu_ref[...], preferred_element_type=F32)
        gu_ref[0] = gate.astype(BF16)
        gu_ref[1] = up.astype(BF16)
        act_ref[...] = (gate * _sigmoid(gate) * up).astype(BF16)

    return _call(
        body, name, (t // tm, half),
        [pl.BlockSpec((tm, k), lambda i, j: (i, 0)),
         pl.BlockSpec((None, None, k, nb), lambda i, j: (layer, j, 0, 0)),
         pl.BlockSpec((None, None, k, nb), lambda i, j: (layer, half + j, 0, 0))],
        [pl.BlockSpec((2, None, tm, nb), lambda i, j: (0, j, i, 0)),
         pl.BlockSpec((None, tm, nb), lambda i, j: (j, i, 0))],
        [jax.ShapeDtypeStruct((2, half, t, nb), BF16), jax.ShapeDtypeStruct((half, t, nb), BF16)],
        vmem_bytes=2 * (_nbytes((tm, k), BF16) + 2 * _nbytes((k, nb), BF16) + 3 * _nbytes((tm, nb), BF16))
        + 6 * _nbytes((tm, nb), F32),
    )(h, w_g, w_g)


def _ffn_down_dx(name, dy, w4, layer, gu4):
    t, n = dy.shape
    nblk, kb = w4.shape[1], w4.shape[2]
    tm = _tile(t, 1024)

    def body(dy_ref, w_ref, gu_ref, dgu_ref):
        da = lax.dot_general(dy_ref[...].astype(BF16), w_ref[...], NT, preferred_element_type=F32)
        gate = gu_ref[0].astype(F32)
        up = gu_ref[1].astype(F32)
        sig = _sigmoid(gate)
        dgu_ref[0] = (da * up * (sig * (1.0 + gate * (1.0 - sig)))).astype(BF16)
        dgu_ref[1] = (da * (gate * sig)).astype(BF16)

    blk = pl.BlockSpec((2, None, tm, kb), lambda i, j: (0, j, i, 0))
    return _call(
        body, name, (t // tm, nblk),
        [pl.BlockSpec((tm, n), lambda i, j: (i, 0)),
         pl.BlockSpec((None, None, kb, n), lambda i, j: (layer, j, 0, 0)),
         blk],
        blk,
        jax.ShapeDtypeStruct((2, nblk, t, kb), BF16),
        vmem_bytes=2 * (_nbytes((tm, n), F32) + _nbytes((kb, n), BF16) + 4 * _nbytes((tm, kb), BF16))
        + 8 * _nbytes((tm, kb), F32),
    )(dy, w4, gu4)


def _mm_t_natural(name, dy, w, layer):
    t, n = dy.shape
    k = w.shape[1]
    tm = _tile(t, 1024)
    tk = _tile(k, 512)
    return _mm(
        name, NT, dy, w, grid=(t // tm, k // tk, 1),
        a_spec=pl.BlockSpec((tm, n), lambda i, j, kk: (i, 0)),
        b_spec=pl.BlockSpec((None, tk, n), lambda i, j, kk: (layer, j, 0)),
        out_shape=jax.ShapeDtypeStruct((t, k), BF16),
        out_spec=pl.BlockSpec((tm, tk), lambda i, j, kk: (i, j)),
        acc_shape=(tm, tk))


def _grad_buf(buf, shape):
    return jax.ShapeDtypeStruct(shape, BF16) if buf is None else jax.ShapeDtypeStruct(buf.shape, buf.dtype)


def _mm_dw_colblock(name, h, dz, buf, layer, n_layers, blocked_in=False):
    t, k = h.shape
    nb = dz.shape[2] if blocked_in else dz.shape[1] // N_DEV
    tk = _tile(t, 1024)
    if blocked_in:
        b_spec = pl.BlockSpec((None, tk, nb), lambda i, j, kk: (j, kk, 0))
    else:
        b_spec = pl.BlockSpec((tk, nb), lambda i, j, kk: (kk, j))
    return _mm(
        name, TN, h, dz, grid=(1, N_DEV, t // tk),
        a_spec=pl.BlockSpec((tk, k), lambda i, j, kk: (kk, 0)),
        b_spec=b_spec,
        out_shape=_grad_buf(buf, (n_layers, N_DEV, k, nb)),
        out_spec=pl.BlockSpec((None, None, k, nb), lambda i, j, kk: (layer, j, 0, 0)),
        acc_shape=(k, nb), alias=buf)


def _mm_dw_natural(name, a, dy, buf, layer, n_layers):
    t, k = a.shape
    n = dy.shape[1]
    tko = _tile(k, 1024)
    tt = _tile(t, 1024)
    return _mm(
        name, TN, a, dy, grid=(k // tko, 1, t // tt),
        a_spec=pl.BlockSpec((tt, tko), lambda i, j, kk: (kk, i)),
        b_spec=pl.BlockSpec((tt, n), lambda i, j, kk: (kk, 0)),
        out_shape=_grad_buf(buf, (n_layers, k, n)),
        out_spec=pl.BlockSpec((None, tko, n), lambda i, j, kk: (layer, i, 0)),
        acc_shape=(tko, n), alias=buf)


def _mm_dw_down(name, act, dy, buf, layer, n_layers):
    nblk, t, kb = act.shape
    n = dy.shape[1]
    tt = _tile(t, 1024)
    return _mm(
        name, TN, act, dy, grid=(nblk, 1, t // tt),
        a_spec=pl.BlockSpec((None, tt, kb), lambda i, j, kk: (i, kk, 0)),
        b_spec=pl.BlockSpec((tt, n), lambda i, j, kk: (kk, 0)),
        out_shape=_grad_buf(buf, (n_layers, nblk, kb, n)),
        out_spec=pl.BlockSpec((None, None, kb, n), lambda i, j, kk: (layer, i, 0, 0)),
        acc_shape=(kb, n), alias=buf)


def _spatial_mask(transposed=False):
    r = lax.broadcasted_iota(jnp.int32, (A_CHUNK, A_CHUNK), 0) // CHUNK
    c = lax.broadcasted_iota(jnp.int32, (A_CHUNK, A_CHUNK), 1) // CHUNK
    return c >= r if transposed else r >= c


def _sgu_tile(t):
    return _tile(t, 2 * A_CHUNK)


def _sgu_fwd(zpre, g_sgu, w_sp, b_full, name):
    t, f2 = zpre.shape
    f = f2 // 2
    gd = f // A_GROUPS
    tm = _sgu_tile(t)

    def body(z_ref, g_ref, w_ref, b_ref, p_ref):
        mask = _spatial_mask()
        wm = [jnp.where(mask, w_ref[g], 0.0).astype(BF16) for g in range(A_GROUPS)]
        for c in range(tm // A_CHUNK):
            rows = pl.ds(c * A_CHUNK, A_CHUNK)
            z = _gelu(z_ref[rows, :].astype(F32))
            u = z[:, :f]
            v0 = z[:, f:]
            r = lax.rsqrt(jnp.mean(v0 * v0, axis=-1, keepdims=True) + EPS)
            v1 = (v0 * r * g_ref[...]).astype(BF16)
            for g in range(A_GROUPS):
                cols = slice(g * gd, (g + 1) * gd)
                v2 = jnp.dot(wm[g], v1[:, cols], preferred_element_type=F32) + b_ref[:, cols]
                p_ref[rows, cols] = (u[:, cols] * v2).astype(BF16)

    return _call(
        body, name, (t // tm,),
        [pl.BlockSpec((tm, f2), lambda i: (i, 0)),
         pl.BlockSpec((1, f), lambda i: (0, 0)),
         pl.BlockSpec((A_GROUPS, A_CHUNK, A_CHUNK), lambda i: (0, 0, 0)),
         pl.BlockSpec((A_CHUNK, f), lambda i: (0, 0))],
        pl.BlockSpec((tm, f), lambda i: (i, 0)),
        jax.ShapeDtypeStruct((t, f), BF16),
        vmem_bytes=2 * _nbytes((tm, f2), BF16) + 2 * _nbytes((tm, f), BF16) + 8 * _nbytes((A_CHUNK, f2), F32),
    )(zpre, g_sgu.reshape(1, f), w_sp, b_full)


def _sgu_bwd(zpre, dp, g_sgu, w_sp, w_sp_t, b_full, name):
    t, f2 = zpre.shape
    f = f2 // 2
    gd = f // A_GROUPS
    tm = _sgu_tile(t)
    n_steps = t // tm

    def body(z_ref, dp_ref, g_ref, w_ref, wt_ref, b_ref, dz_ref, dw_ref, db_ref, dg_ref, dv1_ref, dbf_ref):
        step = pl.program_id(0)

        @pl.when(step == 0)
        def _():
            dw_ref[...] = jnp.zeros_like(dw_ref)
            dg_ref[...] = jnp.zeros_like(dg_ref)
            dbf_ref[...] = jnp.zeros_like(dbf_ref)

        mask = _spatial_mask()
        mask_t = _spatial_mask(transposed=True)
        wm = [jnp.where(mask, w_ref[g], 0.0).astype(BF16) for g in range(A_GROUPS)]
        wmt = [jnp.where(mask_t, wt_ref[g], 0.0).astype(BF16) for g in range(A_GROUPS)]
        gain = g_ref[...]
        for c in range(tm // A_CHUNK):
            rows = pl.ds(c * A_CHUNK, A_CHUNK)
            z, dgelu = _gelu_and_grad(z_ref[rows, :].astype(F32))
            u = z[:, :f]
            v0 = z[:, f:]
            r = lax.rsqrt(jnp.mean(v0 * v0, axis=-1, keepdims=True) + EPS)
            xhat = v0 * r
            v1 = (xhat * gain).astype(BF16)
            dpf = dp_ref[rows, :].astype(F32)
            for g in range(A_GROUPS):
                cols = slice(g * gd, (g + 1) * gd)
                v1g = v1[:, cols]
                v2 = jnp.dot(wm[g], v1g, preferred_element_type=F32) + b_ref[:, cols]
                dpg = dpf[:, cols]
                dz_ref[rows, cols] = (dpg * v2 * dgelu[:, cols]).astype(BF16)
                dv2 = dpg * u[:, cols]
                dbf_ref[:, cols] += dv2
                dv2b = dv2.astype(BF16)
                dwg = lax.dot_general(dv2b, v1g, NT, preferred_element_type=F32)
                dw_ref[g] += jnp.where(mask, dwg, 0.0)
                dv1_ref[:, cols] = jnp.dot(wmt[g], dv2b, preferred_element_type=F32)
            dv1 = dv1_ref[...]
            dxhat = dv1 * gain
            dg_ref[...] += jnp.sum(dv1 * xhat, axis=0, keepdims=True)
            dv0 = r * (dxhat - xhat * jnp.mean(dxhat * xhat, axis=-1, keepdims=True))
            dz_ref[rows, pl.ds(f, f)] = (dv0 * dgelu[:, f:]).astype(BF16)

        @pl.when(step == n_steps - 1)
        def _():
            for g in range(A_GROUPS):
                db_ref[g] = jnp.sum(dbf_ref[:, g * gd:(g + 1) * gd], axis=1, keepdims=True)

    wspec = pl.BlockSpec((A_GROUPS, A_CHUNK, A_CHUNK), lambda i: (0, 0, 0))
    dz, dw, db, dg = _call(
        body, name, (n_steps,),
        [pl.BlockSpec((tm, f2), lambda i: (i, 0)),
         pl.BlockSpec((tm, f), lambda i: (i, 0)),
         pl.BlockSpec((1, f), lambda i: (0, 0)),
         wspec, wspec,
         pl.BlockSpec((A_CHUNK, f), lambda i: (0, 0))],
        [pl.BlockSpec((tm, f2), lambda i: (i, 0)),
         wspec,
         pl.BlockSpec((A_GROUPS, A_CHUNK, 1), lambda i: (0, 0, 0)),
         pl.BlockSpec((1, f), lambda i: (0, 0))],
        [jax.ShapeDtypeStruct((t, f2), BF16),
         jax.ShapeDtypeStruct((A_GROUPS, A_CHUNK, A_CHUNK), F32),
         jax.ShapeDtypeStruct((A_GROUPS, A_CHUNK, 1), F32),
         jax.ShapeDtypeStruct((1, f), F32)],
        scratch=[pltpu.VMEM((A_CHUNK, f), F32), pltpu.VMEM((A_CHUNK, f), F32)],
        vmem_bytes=4 * _nbytes((tm, f2), BF16) + 2 * _nbytes((tm, f), BF16) + 12 * _nbytes((A_CHUNK, f2), F32),
    )(zpre, dp, g_sgu.reshape(1, f), w_sp, w_sp_t, b_full)
    return dz, dw, db.reshape(A_GROUPS, A_CHUNK), dg.reshape(f)


def _pair_valid(qi, col):
    qc = qi // CHUNK
    kc = col // CHUNK
    return (kc >= qc) & (kc <= qc + N_LEFT_CHUNKS)


def _diagonal_onehot():
    e = lax.broadcasted_iota(jnp.int32, (REL_PAD, K_BLOCK), 1)
    idx = jnp.clip(PAIR_BAND - 1 - e, -MAX_REL, MAX_REL) + MAX_REL
    r = lax.broadcasted_iota(jnp.int32, (REL_PAD, K_BLOCK), 0)
    return jnp.where(r == idx, 1.0, 0.0).astype(BF16)


def _bias_build(table, name):
    h = table.shape[0]
    tab = jnp.pad(table, ((0, 0), (0, REL_PAD - N_REL)))

    def body(t_ref, o_ref):
        oh = _diagonal_onehot()
        diag = jnp.zeros((h, K_BLOCK), F32)
        for piece in _split3(t_ref[...]):
            diag += jnp.dot(piece, oh, preferred_element_type=F32)
        col = lax.broadcasted_iota(jnp.int32, (h, PAIR_BAND), 1)
        for qi in range(PAIR_ROWS):
            row = pltpu.roll(diag, (qi - (PAIR_ROWS - 1)) % K_BLOCK, 1)[:, :PAIR_BAND]
            o_ref[qi] = jnp.where(_pair_valid(qi, col), row, NEG_INF)

    out = _call(
        body, name, (1,),
        [pl.BlockSpec((h, REL_PAD), lambda i: (0, 0))],
        pl.BlockSpec((PAIR_ROWS, h, PAIR_BAND), lambda i: (0, 0, 0)),
        jax.ShapeDtypeStruct((PAIR_ROWS, h, PAIR_BAND), F32),
        vmem_bytes=4 * _nbytes((PAIR_ROWS, h, PAIR_BAND), F32),
    )(tab)
    return jnp.transpose(out, (1, 0, 2))


def _bias_block(pair_bias):
    rest = K_BLOCK - PAIR_BAND
    top = jnp.pad(pair_bias, ((0, 0), (0, 0), (0, rest)), constant_values=NEG_INF)
    bottom = jnp.pad(pair_bias, ((0, 0), (0, 0), (rest, 0)), constant_values=NEG_INF)
    return jnp.concatenate([top, bottom], axis=1)


def _bias_grad(dbias, name):
    h = dbias.shape[0]
    db_t = jnp.transpose(dbias, (1, 0, 2))

    def body(d_ref, o_ref):
        diag = jnp.zeros((h, K_BLOCK), F32)
        for qi in range(PAIR_ROWS):
            diag += pltpu.roll(d_ref[qi], PAIR_ROWS - 1 - qi, 1)
        oh = _diagonal_onehot()
        acc = jnp.zeros((h, REL_PAD), F32)
        for piece in _split3(diag):
            acc += lax.dot_general(piece, oh, NT, preferred_element_type=F32)
        o_ref[...] = acc

    out = _call(
        body, name, (1,),
        [pl.BlockSpec((PAIR_ROWS, h, K_BLOCK), lambda i: (0, 0, 0))],
        pl.BlockSpec((h, REL_PAD), lambda i: (0, 0)),
        jax.ShapeDtypeStruct((h, REL_PAD), F32),
        vmem_bytes=4 * _nbytes((PAIR_ROWS, h, K_BLOCK), F32),
    )(db_t)
    return out[:, :N_REL]


def _head_masks():
    lane = lax.broadcasted_iota(jnp.int32, (Q_BLOCK, HEAD_PAIR), 1)
    return lane < HEAD_DIM, lane >= HEAD_DIM


def _block_probs(qm, kb, bias, valid):
    s = lax.dot_general(qm, kb, NT, preferred_element_type=F32) + bias
    s = jnp.where(valid, s, NEG_INF)
    e = jnp.exp(s - jnp.max(s, axis=-1, keepdims=True))
    return e * (1.0 / jnp.sum(e, axis=-1, keepdims=True))


def _attn_fwd(q, kvpad, bias, name):
    t, d = q.shape
    n_pairs = d // HEAD_PAIR
    n_blocks = t // Q_BLOCK

    def body(q_ref, k_ref, v_ref, b_ref, o_ref):
        masks = _head_masks()
        key = lax.broadcasted_iota(jnp.int32, (Q_BLOCK, K_BLOCK), 1)

        def step(j, carry):
            r0 = pl.multiple_of(j * Q_BLOCK, Q_BLOCK)
            q2 = q_ref[pl.ds(r0, Q_BLOCK), :].astype(F32)
            kb = k_ref[pl.ds(r0, K_BLOCK), :]
            vb = v_ref[pl.ds(r0, K_BLOCK), :]
            valid = key >= LEFT - j * Q_BLOCK
            outs = []
            for a in range(2):
                qm = jnp.where(masks[a], q2, 0.0).astype(BF16)
                p = _block_probs(qm, kb, b_ref[a], valid)
                outs.append(jnp.dot(p.astype(BF16), vb, preferred_element_type=F32))
            o_ref[pl.ds(r0, Q_BLOCK), :] = jnp.where(masks[0], outs[0], outs[1]).astype(BF16)
            return carry

        lax.fori_loop(0, n_blocks, step, 0)

    return _call(
        body, name, (n_pairs,),
        [pl.BlockSpec((t, HEAD_PAIR), lambda p: (0, p)),
         pl.BlockSpec((LEFT + t, HEAD_PAIR), lambda p: (0, p)),
         pl.BlockSpec((LEFT + t, HEAD_PAIR), lambda p: (0, n_pairs + p)),
         pl.BlockSpec((2, Q_BLOCK, K_BLOCK), lambda p: (p, 0, 0))],
        pl.BlockSpec((t, HEAD_PAIR), lambda p: (0, p)),
        jax.ShapeDtypeStruct((t, d), BF16),
        vmem_bytes=8 * _nbytes((LEFT + t, HEAD_PAIR), BF16) + 12 * _nbytes((2, Q_BLOCK, K_BLOCK), F32),
    )(q, kvpad, kvpad, bias)


def _attn_bwd(q, kvpad, bias, do, dk_in, dv_in, name):
    t, d = q.shape
    n_pairs = d // HEAD_PAIR
    n_blocks = t // Q_BLOCK
    has_in = dk_in is not None

    def body(*refs):
        refs = list(refs)
        q_ref, k_ref, v_ref, b_ref, do_ref = refs[:5]
        refs = refs[5:]
        if has_in:
            dki_ref, dvi_ref = refs[:2]
            refs = refs[2:]
        dq_ref, dk_ref, dv_ref, db_ref = refs
        masks = _head_masks()
        key = lax.broadcasted_iota(jnp.int32, (Q_BLOCK, K_BLOCK), 1)
        if has_in:
            dk_ref[...] = dki_ref[...]
            dv_ref[...] = dvi_ref[...]
        else:
            dk_ref[...] = jnp.zeros_like(dk_ref)
            dv_ref[...] = jnp.zeros_like(dv_ref)
        db_ref[...] = jnp.zeros_like(db_ref)

        def step(j, carry):
            r0 = pl.multiple_of(j * Q_BLOCK, Q_BLOCK)
            q2 = q_ref[pl.ds(r0, Q_BLOCK), :].astype(F32)
            do2 = do_ref[pl.ds(r0, Q_BLOCK), :].astype(F32)
            kb = k_ref[pl.ds(r0, K_BLOCK), :]
            vb = v_ref[pl.ds(r0, K_BLOCK), :]
            valid = key >= LEFT - j * Q_BLOCK
            dqs = []
            dk_acc = jnp.zeros((K_BLOCK, HEAD_PAIR), F32)
            dv_acc = jnp.zeros((K_BLOCK, HEAD_PAIR), F32)
            for a in range(2):
                qm = jnp.where(masks[a], q2, 0.0).astype(BF16)
                dom = jnp.where(masks[a], do2, 0.0).astype(BF16)
                p = _block_probs(qm, kb, b_ref[a], valid)
                dp = lax.dot_general(dom, vb, NT, preferred_element_type=F32)
                ds = p * (dp - jnp.sum(dp * p, axis=-1, keepdims=True))
                db_ref[a] += ds[:PAIR_ROWS, :]
                db_ref[a, :, pl.ds(0, PAIR_BAND)] += ds[PAIR_ROWS:, K_BLOCK - PAIR_BAND:]
                dsb = ds.astype(BF16)
                dqs.append(jnp.dot(dsb, kb, preferred_element_type=F32))
                dk_acc += lax.dot_general(dsb, qm, TN, preferred_element_type=F32)
                dv_acc += lax.dot_general(p.astype(BF16), dom, TN, preferred_element_type=F32)
            dq = jnp.where(masks[0], dqs[0], dqs[1]) * ATTN_SCALE
            dq_ref[pl.ds(r0, Q_BLOCK), :] = dq.astype(BF16)
            dk_ref[pl.ds(r0, K_BLOCK), :] += dk_acc
            dv_ref[pl.ds(r0, K_BLOCK), :] += dv_acc
            return carry

        lax.fori_loop(0, n_blocks, step, 0)

    q_spec = pl.BlockSpec((t, HEAD_PAIR), lambda p: (0, p))
    kv_spec = pl.BlockSpec((LEFT + t, HEAD_PAIR), lambda p: (0, p))
    operands = [q, kvpad, kvpad, bias, do]
    in_specs = [q_spec, kv_spec, pl.BlockSpec((LEFT + t, HEAD_PAIR), lambda p: (0, n_pairs + p)),
                pl.BlockSpec((2, Q_BLOCK, K_BLOCK), lambda p: (p, 0, 0)), q_spec]
    aliases = None
    if has_in:
        operands += [dk_in, dv_in]
        in_specs += [kv_spec, kv_spec]
        aliases = {5: 1, 6: 2}
    return _call(
        body, name, (n_pairs,),
        in_specs,
        [q_spec, kv_spec, kv_spec, pl.BlockSpec((2, PAIR_ROWS, K_BLOCK), lambda p: (p, 0, 0))],
        [jax.ShapeDtypeStruct((t, d), BF16),
         jax.ShapeDtypeStruct((LEFT + t, d), F32),
         jax.ShapeDtypeStruct((LEFT + t, d), F32),
         jax.ShapeDtypeStruct((d // HEAD_DIM, PAIR_ROWS, K_BLOCK), F32)],
        vmem_bytes=10 * _nbytes((LEFT + t, HEAD_PAIR), BF16) + 8 * _nbytes((LEFT + t, HEAD_PAIR), F32)
        + 16 * _nbytes((2, Q_BLOCK, K_BLOCK), F32),
        aliases=aliases,
    )(*operands)


def _loss_head(x, g, target, name):
    t, d = x.shape
    tm = _tile(t, 512)

    def body(x_ref, g_ref, t_ref, dx_ref, loss_ref, dg_ref):
        @pl.when(pl.program_id(0) == 0)
        def _():
            loss_ref[...] = jnp.zeros_like(loss_ref)
            dg_ref[...] = jnp.zeros_like(dg_ref)

        xf = x_ref[...]
        r = lax.rsqrt(jnp.mean(xf * xf, axis=-1, keepdims=True) + EPS)
        xhat = xf * r
        diff = xhat * g_ref[...] - t_ref[...]
        row_loss = jnp.mean(diff * diff, axis=-1, keepdims=True)
        loss_ref[...] += 0.5 * jnp.sum(row_loss, axis=0, keepdims=True)
        dy = diff * (1.0 / d)
        dg_ref[...] += jnp.sum(dy * xhat, axis=0, keepdims=True)
        dxhat = dy * g_ref[...]
        dx_ref[...] = r * (dxhat - xhat * jnp.mean(dxhat * xhat, axis=-1, keepdims=True))

    row = pl.BlockSpec((tm, d), lambda i: (i, 0))
    vec = pl.BlockSpec((1, d), lambda i: (0, 0))
    dx, loss, dg = _call(
        body, name, (t // tm,),
        [row, vec, row],
        [row, pl.BlockSpec((1, 1), lambda i: (0, 0)), vec],
        [jax.ShapeDtypeStruct((t, d), F32), jax.ShapeDtypeStruct((1, 1), F32), jax.ShapeDtypeStruct((1, d), F32)],
        vmem_bytes=10 * _nbytes((tm, d), F32),
    )(x, g.reshape(1, d), target)
    return dx, loss[0, 0], dg.reshape(d)


def _adamw(parts, w, m, v, name):
    n_layers, n_src, r, c = parts.shape
    mult = 16 if parts.dtype == BF16 else 8
    tr = _row_tile(r, max(mult, (256 * 1024) // c), mult)
    c1 = 1.0 / (1.0 - ADAM_B1 ** ADAM_STEP)
    c2 = 1.0 / (1.0 - ADAM_B2 ** ADAM_STEP)

    def body(p_ref, w_ref, m_ref, v_ref, g_ref, d_ref, nm_ref, nv_ref):
        g = p_ref[0].astype(F32)
        for s in range(1, n_src):
            g = g + p_ref[s].astype(F32)
        nm = ADAM_B1 * m_ref[...] + (1.0 - ADAM_B1) * g
        nv = ADAM_B2 * v_ref[...] + (1.0 - ADAM_B2) * (g * g)
        g_ref[...] = g
        nm_ref[...] = nm
        nv_ref[...] = nv
        d_ref[...] = -ADAM_LR * ((nm * c1) / (jnp.sqrt(nv * c2) + ADAM_EPS) + ADAM_WD * w_ref[...])

    blk = pl.BlockSpec((None, tr, c), lambda l, i: (l, i, 0))
    out = jax.ShapeDtypeStruct((n_layers, r, c), F32)
    return _call(
        body, name, (n_layers, r // tr),
        [pl.BlockSpec((None, n_src, tr, c), lambda l, i: (l, 0, i, 0)), blk, blk, blk],
        [blk, blk, blk, blk],
        [out, out, out, out],
        vmem_bytes=2 * _nbytes((n_src, tr, c), parts.dtype) + 18 * _nbytes((tr, c), F32),
    )(parts, w, m, v)


def _ordered_sum(parts, name):
    n_src, r, c = parts.shape

    def body(p_ref, o_ref):
        acc = p_ref[0]
        for s in range(1, n_src):
            acc = acc + p_ref[s]
        o_ref[...] = acc

    return _call(
        body, name, (1,),
        [pl.BlockSpec((n_src, r, c), lambda i: (0, 0, 0))],
        pl.BlockSpec((r, c), lambda i: (0, 0)),
        jax.ShapeDtypeStruct((r, c), F32),
        vmem_bytes=4 * _nbytes((n_src, r, c), F32),
    )(parts)


def _position():
    return lax.axis_index("x"), lax.axis_index("y"), lax.axis_index("c")


def _linear(p):
    return 4 * p[0] + 2 * p[1] + p[2]


def _all_gather(shards, name):
    n = len(shards)

    def body(*refs):
        ins, outs = refs[:n], refs[n:2 * n]
        send_sems, recv_sems, local_sems = refs[2 * n:]
        x, y, c = _position()
        me, sibling = (x, y, c), (x, y, 1 - c)
        chips = [(1 - x, y), (x, 1 - y), (1 - x, 1 - y)]

        def slab(t, p):
            return outs[t].at[:, _linear(p)]

        def copy(t, k, block, to, src=None):
            return pltpu.make_async_remote_copy(
                src_ref=slab(t, block) if src is None else src,
                dst_ref=slab(t, block),
                send_sem=send_sems.at[t, k],
                recv_sem=recv_sems.at[t, k],
                device_id=to,
                device_id_type=MESH,
            )

        started = []
        for t in range(n):
            mine = pltpu.make_async_copy(ins[t], slab(t, me), local_sems.at[t])
            mine.start()
            started.append(mine)
        sends = []
        for t in range(n):
            first = [copy(t, 0, me, sibling, src=ins[t])]
            first += [copy(t, 1 + j, me, (*chip, c), src=ins[t]) for j, chip in enumerate(chips)]
            for cp in first:
                cp.start()
            sends += first
        for t in range(n):
            for j, chip in enumerate(chips):
                copy(t, 1 + j, (*chip, c), me).wait_recv()
                passed = copy(t, 4 + j, (*chip, c), sibling)
                passed.start()
                sends.append(passed)
        for t in range(n):
            copy(t, 0, sibling, me).wait_recv()
            for j, chip in enumerate(chips):
                copy(t, 4 + j, (*chip, 1 - c), me).wait_recv()
        for cp in sends:
            cp.wait_send()
        for mine in started:
            mine.wait()

    out_shape = [jax.ShapeDtypeStruct((s.shape[0], N_DEV) + s.shape[1:], s.dtype) for s in shards]
    return pl.pallas_call(
        body,
        name=name,
        in_specs=[HBM_SPEC] * n,
        out_specs=[HBM_SPEC] * n,
        out_shape=out_shape,
        scratch_shapes=[
            pltpu.SemaphoreType.DMA((n, N_DEV - 1)),
            pltpu.SemaphoreType.DMA((n, N_DEV - 1)),
            pltpu.SemaphoreType.DMA((n,)),
        ],
    )(*shards)


def _exchange(blocks, name):
    n = len(blocks)

    def body(*refs):
        ins, outs = refs[:n], refs[n:2 * n]
        send_sems, recv_sems, local_sems = refs[2 * n:]
        x, y, c = _position()
        me = _linear((x, y, c))
        flips = [(fx, fy, fc) for fx in (0, 1) for fy in (0, 1) for fc in (0, 1)][1:]

        def peer_of(flip):
            fx, fy, fc = flip
            return (1 - x if fx else x, 1 - y if fy else y, 1 - c if fc else c)

        def copy(t, k, peer):
            return pltpu.make_async_remote_copy(
                src_ref=ins[t].at[:, _linear(peer)],
                dst_ref=outs[t].at[:, me],
                send_sem=send_sems.at[t, k],
                recv_sem=recv_sems.at[t, k],
                device_id=peer,
                device_id_type=MESH,
            )

        def arrival(t, k, peer):
            return pltpu.make_async_remote_copy(
                src_ref=ins[t].at[:, _linear(peer)],
                dst_ref=outs[t].at[:, _linear(peer)],
                send_sem=send_sems.at[t, k],
                recv_sem=recv_sems.at[t, k],
                device_id=peer,
                device_id_type=MESH,
            )

        own = []
        for t in range(n):
            cp = pltpu.make_async_copy(ins[t].at[:, me], outs[t].at[:, me], local_sems.at[t])
            cp.start()
            own.append(cp)
        sends = []
        for t in range(n):
            for k, flip in enumerate(flips):
                cp = copy(t, k, peer_of(flip))
                cp.start()
                sends.append(cp)
        for t in range(n):
            for k, flip in enumerate(flips):
                arrival(t, k, peer_of(flip)).wait_recv()
        for cp in sends:
            cp.wait_send()
        for cp in own:
            cp.wait()

    out_shape = [jax.ShapeDtypeStruct(b.shape, b.dtype) for b in blocks]
    return pl.pallas_call(
        body,
        name=name,
        in_specs=[HBM_SPEC] * n,
        out_specs=[HBM_SPEC] * n,
        out_shape=out_shape,
        scratch_shapes=[
            pltpu.SemaphoreType.DMA((n, N_DEV - 1)),
            pltpu.SemaphoreType.DMA((n, N_DEV - 1)),
            pltpu.SemaphoreType.DMA((n,)),
        ],
    )(*blocks)


def _pack(arrays, row_multiple):
    flat = jnp.concatenate([a.reshape(-1) for a in arrays])
    quantum = row_multiple * FLAT_LANES
    padded = -(-flat.shape[0] // quantum) * quantum
    return jnp.pad(flat, (0, padded - flat.shape[0])).reshape(-1, FLAT_LANES)


def _unpack(flat, like):
    flat = flat.reshape(-1)
    out, at = [], 0
    for a in like:
        size = math.prod(a.shape)
        out.append(flat[at:at + size].reshape(a.shape))
        at += size
    return out


def kernel(x, a_norm, a_w_in, a_sgu_norm, a_w_spatial, a_b_spatial, a_w_out, kv_norm, w_kv, b_norm, b_w_q, b_rel_bias, b_w_o, ffn_norm, ffn_w_gate_up, ffn_w_down, final_norm, loss_target, m_a_norm, m_a_w_in, m_a_sgu_norm, m_a_w_spatial, m_a_b_spatial, m_a_w_out, m_kv_norm, m_w_kv, m_b_norm, m_b_w_q, m_b_rel_bias, m_b_w_o, m_ffn_norm, m_ffn_w_gate_up, m_ffn_w_down, m_final_norm, v_a_norm, v_a_w_in, v_a_sgu_norm, v_a_w_spatial, v_a_b_spatial, v_a_w_out, v_kv_norm, v_w_kv, v_b_norm, v_b_w_q, v_b_rel_bias, v_b_w_o, v_ffn_norm, v_ffn_w_gate_up, v_ffn_w_down, v_final_norm):
    xs = x[0]
    target = loss_target[0]
    t, d = xs.shape
    n_a = a_w_in.shape[0]
    n_b = b_w_q.shape[0]
    depth = ffn_w_gate_up.shape[0]
    f_a = a_w_out.shape[1] * N_DEV
    gd = f_a // A_GROUPS
    nb_ffn = ffn_w_gate_up.shape[2]
    me = _linear(_position())

    small_rows = -(-(a_norm.size + a_sgu_norm.size) // (8 * 128)) * 8
    small = jnp.pad(jnp.concatenate([a_norm.reshape(-1), a_sgu_norm.reshape(-1)]),
                    (0, small_rows * 128 - a_norm.size - a_sgu_norm.size)).reshape(1, small_rows, 128)
    big = [a_w_in, a_w_out, w_kv[None], b_w_q, b_w_o, ffn_w_gate_up, ffn_w_down]
    gathered = _all_gather([w.astype(BF16) for w in big] + [small], "gather_weights")
    win_g, wout_g, wkv_g, wq_g, wo_g, wgu_g, wd_g, small_g = gathered
    wout_n = wout_g.reshape(n_a, f_a, d)
    wq_n = wq_g.reshape(n_b, d, d)
    wo_n = wo_g.reshape(n_b, d, d)
    wd_4 = wd_g.reshape(depth, N_DEV // 2, 2 * ffn_w_down.shape[1], d)
    small_g = small_g.reshape(N_DEV, -1)
    a_norm_full = small_g[:, :a_norm.size].reshape(N_DEV, n_a, -1).transpose(1, 0, 2).reshape(n_a, d)
    a_sgu_full = small_g[:, a_norm.size:a_norm.size + a_sgu_norm.size].reshape(
        N_DEV, n_a, -1).transpose(1, 0, 2).reshape(n_a, f_a)

    w_sp_t = jnp.swapaxes(a_w_spatial, -1, -2)
    b_full = jnp.repeat(jnp.swapaxes(a_b_spatial, -1, -2), gd, axis=-1)

    saved = []

    def ffn_fwd(xin, layer):
        hf = _rms_fwd(xin, ffn_norm[layer], f"ffn_norm_fwd_{layer}")
        gu, act = _ffn_gate_up(f"ffn_gate_up_{layer}", hf, wgu_g, layer)
        xout = _mm_down(f"ffn_down_{layer}", act, wd_4, layer, xin)
        return xout, (xin, hf, gu, act)

    for i in range(n_a):
        h = _rms_fwd(xs, a_norm_full[i], f"a_norm_fwd_{i}")
        zpre = _mm_colblock(f"a_in_{i}", h, win_g, i)
        p = _sgu_fwd(zpre, a_sgu_full[i], a_w_spatial[i], b_full[i], f"a_sgu_fwd_{i}")
        x_mid = _mm_natural(f"a_out_{i}", p, wout_n, i, res=xs)
        x_out, ffn_saved = ffn_fwd(x_mid, i)
        saved.append((xs, h, zpre, p, ffn_saved))
        xs = x_out

    x_kv = xs
    h_kv = _rms_fwd(x_kv, kv_norm, "kv_norm_fwd")
    kv = _mm_colblock("kv_proj", h_kv, wkv_g, 0)
    kvpad = jnp.pad(kv, ((LEFT, 0), (0, 0)))

    biases = [_bias_block(_bias_build(b_rel_bias[i], f"rel_bias_{i}")) for i in range(n_b)]
    for i in range(n_b):
        layer = n_a + i
        hb = _rms_fwd(xs, b_norm[i], f"b_norm_fwd_{i}")
        q = _mm_natural(f"b_q_{i}", hb, wq_n, i, out_dtype=BF16, scale=ATTN_SCALE)
        o = _attn_fwd(q, kvpad, biases[i], f"b_attn_fwd_{i}")
        x_mid = _mm_natural(f"b_o_{i}", o, wo_n, i, res=xs)
        x_out, ffn_saved = ffn_fwd(x_mid, layer)
        saved.append((xs, hb, q, o, ffn_saved))
        xs = x_out

    dx, loss_local, g_final = _loss_head(xs, final_norm, target, "loss_head")
    loss = lax.psum(loss_local, ("x", "y", "c"))

    g_win = g_wout = g_wkv = g_wq = g_wo = g_wgu = g_wd = None
    g_ffn_norm = [None] * depth
    g_a_norm = [None] * n_a
    g_a_sgu = [None] * n_a
    g_w_sp = [None] * n_a
    g_b_sp = [None] * n_a
    g_b_norm = [None] * n_b
    g_rel = [None] * n_b

    def ffn_bwd(dx, layer, ffn_saved):
        nonlocal g_wgu, g_wd
        xin, hf, gu, act = ffn_saved
        g_wd = _mm_dw_down(f"ffn_down_dw_{layer}", act, dx, g_wd, layer, depth)
        dgu = _ffn_down_dx(f"ffn_down_dx_{layer}", dx, wd_4, layer, gu).reshape(N_DEV, t, nb_ffn)
        g_wgu = _mm_dw_colblock(f"ffn_gate_up_dw_{layer}", hf, dgu, g_wgu, layer, depth, blocked_in=True)
        dx, g_ffn_norm[layer] = _mm_t_colblock_norm_bwd(
            f"ffn_gate_up_dx_{layer}", dgu, wgu_g, layer, xin, ffn_norm[layer], dx, blocked_in=True)
        return dx

    dk = dv = None
    for i in reversed(range(n_b)):
        layer = n_a + i
        x_in, hb, q, o, ffn_saved = saved[layer]
        dx = ffn_bwd(dx, layer, ffn_saved)
        g_wo = _mm_dw_natural(f"b_o_dw_{i}", o, dx, g_wo, i, n_b)
        do = _mm_t_natural(f"b_o_dx_{i}", dx, wo_n, i)
        dq, dk, dv, dbias = _attn_bwd(q, kvpad, biases[i], do, dk, dv, f"b_attn_bwd_{i}")
        g_rel[i] = _bias_grad(dbias, f"rel_bias_grad_{i}")
        g_wq = _mm_dw_natural(f"b_q_dw_{i}", hb, dq, g_wq, i, n_b)
        dh = _mm_t_natural(f"b_q_dx_{i}", dq, wq_n, i)
        dx, g_b_norm[i] = _rms_bwd(x_in, b_norm[i], dh, dx, f"b_norm_bwd_{i}")

    dkv = jnp.concatenate([dk[LEFT:], dv[LEFT:]], axis=1).astype(BF16)
    g_wkv = _mm_dw_colblock("kv_proj_dw", h_kv, dkv, None, 0, 1)
    dx, g_kv_norm = _mm_t_colblock_norm_bwd("kv_proj_dx", dkv, wkv_g, 0, x_kv, kv_norm, dx)

    for i in reversed(range(n_a)):
        x_in, h, zpre, p, ffn_saved = saved[i]
        dx = ffn_bwd(dx, i, ffn_saved)
        g_wout = _mm_dw_natural(f"a_out_dw_{i}", p, dx, g_wout, i, n_a)
        dp = _mm_t_natural(f"a_out_dx_{i}", dx, wout_n, i)
        dz, g_w_sp[i], g_b_sp[i], g_a_sgu[i] = _sgu_bwd(
            zpre, dp, a_sgu_full[i], a_w_spatial[i], w_sp_t[i], b_full[i], f"a_sgu_bwd_{i}")
        g_win = _mm_dw_colblock(f"a_in_dw_{i}", h, dz, g_win, i, n_a)
        dx, g_a_norm[i] = _mm_t_colblock_norm_bwd(f"a_in_dx_{i}", dz, win_g, i, x_in, a_norm_full[i], dx)
    grad_x = dx[None]

    small_like = [jax.ShapeDtypeStruct((n_a, d), F32), jax.ShapeDtypeStruct((n_a, f_a), F32),
                  a_w_spatial, a_b_spatial, kv_norm, b_norm, b_rel_bias, ffn_norm, final_norm]
    small_partial = _pack(
        [jnp.stack(g_a_norm), jnp.stack(g_a_sgu), jnp.stack(g_w_sp), jnp.stack(g_b_sp), g_kv_norm,
         jnp.stack(g_b_norm), jnp.stack(g_rel), jnp.stack(g_ffn_norm), g_final], N_DEV * 8)
    chunk_rows = small_partial.shape[0] // N_DEV
    big_grads = [
        g_win,
        g_wout.reshape(n_a, N_DEV, f_a // N_DEV, d),
        g_wkv,
        g_wq.reshape(n_b, N_DEV, d // N_DEV, d),
        g_wo.reshape(n_b, N_DEV, d // N_DEV, d),
        g_wgu,
        g_wd.reshape(depth, N_DEV, ffn_w_down.shape[1], d),
    ]
    received = _exchange(big_grads + [small_partial.reshape(1, N_DEV, chunk_rows, FLAT_LANES)], "exchange_grads")
    small_sum = _ordered_sum(received[-1][0], "small_grad_sum")
    small_all = _all_gather([small_sum[None]], "gather_small_grads")[0]
    (ga_norm, ga_sgu, gw_sp, gb_sp, gkv_norm, gb_norm, g_relb, gffn_norm, gfinal) = _unpack(small_all, small_like)

    results = {}
    big_names = ["a_w_in", "a_w_out", "w_kv", "b_w_q", "b_w_o", "ffn_w_gate_up", "ffn_w_down"]
    big_wmv = [(a_w_in, m_a_w_in, v_a_w_in), (a_w_out, m_a_w_out, v_a_w_out),
               (w_kv[None], m_w_kv[None], v_w_kv[None]), (b_w_q, m_b_w_q, v_b_w_q), (b_w_o, m_b_w_o, v_b_w_o),
               (ffn_w_gate_up, m_ffn_w_gate_up, v_ffn_w_gate_up), (ffn_w_down, m_ffn_w_down, v_ffn_w_down)]
    for name, parts, (w, m, v) in zip(big_names, received[:-1], big_wmv):
        outs = _adamw(parts, w, m, v, f"adamw_{name}")
        if name == "w_kv":
            outs = [o[0] for o in outs]
        results[name] = outs

    n_cols = a_norm.shape[1]
    s_cols = a_sgu_norm.shape[1]
    small_g_list = [lax.dynamic_slice(ga_norm, (0, me * n_cols), (n_a, n_cols)),
                    lax.dynamic_slice(ga_sgu, (0, me * s_cols), (n_a, s_cols)),
                    gw_sp, gb_sp, gkv_norm, gb_norm, g_relb, gffn_norm, gfinal]
    small_names = ["a_norm", "a_sgu_norm", "a_w_spatial", "a_b_spatial", "kv_norm", "b_norm", "b_rel_bias",
                   "ffn_norm", "final_norm"]
    small_w = [a_norm, a_sgu_norm, a_w_spatial, a_b_spatial, kv_norm, b_norm, b_rel_bias, ffn_norm, final_norm]
    small_m = [m_a_norm, m_a_sgu_norm, m_a_w_spatial, m_a_b_spatial, m_kv_norm, m_b_norm, m_b_rel_bias,
               m_ffn_norm, m_final_norm]
    small_v = [v_a_norm, v_a_sgu_norm, v_a_w_spatial, v_a_b_spatial, v_kv_norm, v_b_norm, v_b_rel_bias,
               v_ffn_norm, v_final_norm]
    flat_g = _pack(small_g_list, 8)
    flat_out = _adamw(flat_g[None, None], _pack(small_w, 8)[None], _pack(small_m, 8)[None],
                      _pack(small_v, 8)[None], "adamw_small")
    unpacked = [_unpack(o[0], small_w) for o in flat_out]
    for idx, name in enumerate(small_names):
        results[name] = [unpacked[kind][idx] for kind in range(4)]

    order = ["a_norm", "a_w_in", "a_sgu_norm", "a_w_spatial", "a_b_spatial", "a_w_out", "kv_norm", "w_kv",
             "b_norm", "b_w_q", "b_rel_bias", "b_w_o", "ffn_norm", "ffn_w_gate_up", "ffn_w_down", "final_norm"]
    outputs = [loss, grad_x]
    for kind in range(4):
        outputs += [results[name][kind] for name in order]
    return tuple(outputs)
```

```python
import math

import jax
import jax.numpy as jnp
from jax import lax
from jax.experimental import pallas as pl
from jax.experimental.pallas import tpu as pltpu

F32 = jnp.float32
BF16 = jnp.bfloat16
MESH = pl.DeviceIdType.MESH
HBM_SPEC = pl.BlockSpec(memory_space=pltpu.HBM)
SEM_SPEC = pl.BlockSpec(memory_space=pltpu.SEMAPHORE)

N_DEV = 8
CHUNK = 64
A_CHUNK = 128
A_GROUPS = 8
N_LEFT_CHUNKS = 8
LEFT = N_LEFT_CHUNKS * CHUNK
PAIR_ROWS = 2 * CHUNK
PAIR_BAND = PAIR_ROWS + LEFT
Q_BLOCK = 2 * PAIR_ROWS
K_BLOCK = Q_BLOCK + LEFT
MAX_REL = 256
N_REL = 2 * MAX_REL + 1
REL_PAD = 640
HEAD_DIM = 64
HEAD_PAIR = 2 * HEAD_DIM
ATTN_SCALE = HEAD_DIM ** -0.5
EPS = 1e-6
NEG_INF = -1e30
ADAM_LR = 0.001
ADAM_B1 = 0.9
ADAM_B2 = 0.999
ADAM_EPS = 1e-08
ADAM_WD = 0.01
ADAM_STEP = 10
FLAT_LANES = 1024
V7X_VMEM_BYTES = 64 * 1024 * 1024
VMEM_FLOOR_BYTES = 32 * 1024 * 1024
VMEM_CEIL_BYTES = V7X_VMEM_BYTES - 8 * 1024 * 1024

NN = (((1,), (0,)), ((), ()))
NT = (((1,), (1,)), ((), ()))
TN = (((0,), (0,)), ((), ()))


def _tile(n, pref):
    return pref if n % pref == 0 else n


def _row_tile(n, pref, mult):
    best = None
    for t in range(mult, min(n, pref) + 1, mult):
        if n % t == 0:
            best = t
    return best if best is not None else n


def _nbytes(shape, dtype):
    n = 1
    for s in shape:
        if s is not None:
            n *= s
    return n * jnp.dtype(dtype).itemsize


def _call(body, name, grid, in_specs, out_specs, out_shape, scratch=(), vmem_bytes=0, aliases=None):
    limit = int(min(max(VMEM_FLOOR_BYTES, vmem_bytes * 5 // 4), VMEM_CEIL_BYTES))
    return pl.pallas_call(
        body,
        name=name,
        grid=grid,
        in_specs=in_specs,
        out_specs=out_specs,
        out_shape=out_shape,
        scratch_shapes=list(scratch),
        input_output_aliases=aliases or {},
        compiler_params=pltpu.CompilerParams(
            dimension_semantics=("arbitrary",) * len(grid), vmem_limit_bytes=limit),
    )


def _erf_parts(x):
    ax = jnp.abs(x) * (1.0 / math.sqrt(2.0))
    t = 1.0 / (1.0 + 0.3275911 * ax)
    poly = ((((1.061405429 * t - 1.453152027) * t + 1.421413741) * t - 0.284496736) * t + 0.254829592) * t
    ex = jnp.exp(-ax * ax)
    erf_abs = 1.0 - poly * ex
    return jnp.where(x < 0, -erf_abs, erf_abs), ex


def _gelu_and_grad(x):
    erf, ex = _erf_parts(x)
    cdf = 0.5 * (1.0 + erf)
    return x * cdf, cdf + x * ex * (1.0 / math.sqrt(2.0 * math.pi))


def _gelu(x):
    erf, _ = _erf_parts(x)
    return x * (0.5 * (1.0 + erf))


def _sigmoid(x):
    return 1.0 / (1.0 + jnp.exp(-x))


def _split3(x):
    hi = x.astype(BF16)
    r1 = x - hi.astype(F32)
    mid = r1.astype(BF16)
    lo = (r1 - mid.astype(F32)).astype(BF16)
    return hi, mid, lo


def _rms_fwd(x, g, name):
    t, d = x.shape
    tm = _tile(t, 512)

    def body(x_ref, g_ref, o_ref):
        xf = x_ref[...]
        r = lax.rsqrt(jnp.mean(xf * xf, axis=-1, keepdims=True) + EPS)
        o_ref[...] = (xf * r * g_ref[...]).astype(o_ref.dtype)

    return _call(
        body, name, (t // tm,),
        [pl.BlockSpec((tm, d), lambda i: (i, 0)), pl.BlockSpec((1, d), lambda i: (0, 0))],
        pl.BlockSpec((tm, d), lambda i: (i, 0)),
        jax.ShapeDtypeStruct((t, d), BF16),
        vmem_bytes=2 * (_nbytes((tm, d), F32) + _nbytes((tm, d), BF16)) + 4 * _nbytes((tm, d), F32),
    )(x, g.reshape(1, d))


def _rms_bwd(x, g, dh, dx_up, name):
    t, d = x.shape
    tm = _tile(t, 512)

    def body(x_ref, g_ref, dh_ref, up_ref, dx_ref, dg_ref):
        @pl.when(pl.program_id(0) == 0)
        def _():
            dg_ref[...] = jnp.zeros_like(dg_ref)

        xf = x_ref[...]
        r = lax.rsqrt(jnp.mean(xf * xf, axis=-1, keepdims=True) + EPS)
        xhat = xf * r
        dy = dh_ref[...].astype(F32)
        dxhat = dy * g_ref[...]
        dg_ref[...] += jnp.sum(dy * xhat, axis=0, keepdims=True)
        dx = r * (dxhat - xhat * jnp.mean(dxhat * xhat, axis=-1, keepdims=True))
        dx_ref[...] = up_ref[...] + dx

    row = pl.BlockSpec((tm, d), lambda i: (i, 0))
    vec = pl.BlockSpec((1, d), lambda i: (0, 0))
    dx, dg = _call(
        body, name, (t // tm,),
        [row, vec, row, row],
        [row, vec],
        [jax.ShapeDtypeStruct((t, d), F32), jax.ShapeDtypeStruct((1, d), F32)],
        vmem_bytes=10 * _nbytes((tm, d), F32),
    )(x, g.reshape(1, d), dh, dx_up)
    return dx, dg.reshape(d)


def _mm(name, dims, a, b, *, grid, a_spec, b_spec, out_shape, out_spec, acc_shape,
        res=None, res_spec=None, alias=None, scale=None):
    nk = grid[2]
    has_res = res is not None
    has_alias = alias is not None

    def body(*refs):
        refs = list(refs)
        a_ref = refs.pop(0)
        b_ref = refs.pop(0)
        r_ref = refs.pop(0) if has_res else None
        if has_alias:
            refs.pop(0)
        o_ref = refs.pop(0)
        part = lax.dot_general(a_ref[...].astype(BF16), b_ref[...].astype(BF16), dims,
                               preferred_element_type=F32)

        def finish(acc):
            if scale is not None:
                acc = acc * scale
            if has_res:
                acc = acc + r_ref[...]
            o_ref[...] = acc.astype(o_ref.dtype)

        if nk == 1:
            finish(part)
        else:
            acc_ref = refs.pop(0)
            k = pl.program_id(2)

            @pl.when(k == 0)
            def _():
                acc_ref[...] = part

            @pl.when(k > 0)
            def _():
                acc_ref[...] += part

            @pl.when(k == nk - 1)
            def _():
                finish(acc_ref[...])

    operands = [a, b]
    in_specs = [a_spec, b_spec]
    vmem = 2 * (_nbytes(a_spec.block_shape, a.dtype) + _nbytes(b_spec.block_shape, b.dtype)
                + _nbytes(out_spec.block_shape, out_shape.dtype))
    vmem += 3 * _nbytes(acc_shape, F32)
    if has_res:
        operands.append(res)
        in_specs.append(res_spec)
        vmem += 2 * _nbytes(res_spec.block_shape, res.dtype)
    aliases = None
    if has_alias:
        aliases = {len(operands): 0}
        operands.append(alias)
        in_specs.append(pl.BlockSpec(memory_space=pl.ANY))
    scratch = [pltpu.VMEM(acc_shape, F32)] if nk > 1 else []
    return _call(body, name, grid, in_specs, out_spec, out_shape, scratch=scratch,
                 vmem_bytes=vmem, aliases=aliases)(*operands)


def _mm_colblock(name, h, w_g, layer):
    t, k = h.shape
    nb = w_g.shape[3]
    tm = _tile(t, 2048)
    return _mm(
        name, NN, h, w_g, grid=(t // tm, N_DEV, 1),
        a_spec=pl.BlockSpec((tm, k), lambda i, j, kk: (i, 0)),
        b_spec=pl.BlockSpec((None, None, k, nb), lambda i, j, kk: (layer, j, 0, 0)),
        out_shape=jax.ShapeDtypeStruct((t, N_DEV * nb), BF16),
        out_spec=pl.BlockSpec((tm, nb), lambda i, j, kk: (i, j)), acc_shape=(tm, nb))


def _mm_natural(name, a, w, layer, *, res=None, out_dtype=F32, scale=None):
    t, k = a.shape
    n = w.shape[2]
    tm = _tile(t, 1024)
    tn = _tile(n, 512)
    res_spec = None if res is None else pl.BlockSpec((tm, tn), lambda i, j, kk: (i, j))
    return _mm(
        name, NN, a, w, grid=(t // tm, n // tn, 1),
        a_spec=pl.BlockSpec((tm, k), lambda i, j, kk: (i, 0)),
        b_spec=pl.BlockSpec((None, k, tn), lambda i, j, kk: (layer, 0, j)),
        out_shape=jax.ShapeDtypeStruct((t, n), out_dtype),
        out_spec=pl.BlockSpec((tm, tn), lambda i, j, kk: (i, j)),
        acc_shape=(tm, tn), res=res, res_spec=res_spec, scale=scale)


def _mm_down(name, act, w4, layer, res):
    nblk, t, kb = act.shape
    n = w4.shape[3]
    tm = _tile(t, 1024)
    tn = _tile(n, 1024)
    return _mm(
        name, NN, act, w4, grid=(t // tm, n // tn, nblk),
        a_spec=pl.BlockSpec((None, tm, kb), lambda i, j, kk: (kk, i, 0)),
        b_spec=pl.BlockSpec((None, None, kb, tn), lambda i, j, kk: (layer, kk, 0, j)),
        out_shape=jax.ShapeDtypeStruct((t, n), F32),
        out_spec=pl.BlockSpec((tm, tn), lambda i, j, kk: (i, j)),
        acc_shape=(tm, tn), res=res, res_spec=pl.BlockSpec((tm, tn), lambda i, j, kk: (i, j)))


def _mm_t_colblock_norm_bwd(name, dz, w_g, layer, x, g, dx_up, blocked_in=False):
    k = w_g.shape[2]
    nb = w_g.shape[3]
    t = x.shape[0]
    tm = _tile(t, 1024)
    if blocked_in:
        a_spec = pl.BlockSpec((None, tm, nb), lambda i, kk: (kk, i, 0))
    else:
        a_spec = pl.BlockSpec((tm, nb), lambda i, kk: (i, kk))

    def body(a_ref, b_ref, x_ref, g_ref, up_ref, dx_ref, dg_ref, acc_ref):
        i = pl.program_id(0)
        kk = pl.program_id(1)
        part = lax.dot_general(a_ref[...].astype(BF16), b_ref[...].astype(BF16), NT, preferred_element_type=F32)

        @pl.when(kk == 0)
        def _():
            acc_ref[...] = part

        @pl.when(kk > 0)
        def _():
            acc_ref[...] += part

        @pl.when((i == 0) & (kk == 0))
        def _():
            dg_ref[...] = jnp.zeros_like(dg_ref)

        @pl.when(kk == N_DEV - 1)
        def _():
            dy = acc_ref[...]
            xf = x_ref[...]
            r = lax.rsqrt(jnp.mean(xf * xf, axis=-1, keepdims=True) + EPS)
            xhat = xf * r
            dxhat = dy * g_ref[...]
            dg_ref[...] += jnp.sum(dy * xhat, axis=0, keepdims=True)
            dx_ref[...] = up_ref[...] + r * (dxhat - xhat * jnp.mean(dxhat * xhat, axis=-1, keepdims=True))

    row = pl.BlockSpec((tm, k), lambda i, kk: (i, 0))
    vec = pl.BlockSpec((1, k), lambda i, kk: (0, 0))
    dx, dg = _call(
        body, name, (t // tm, N_DEV),
        [a_spec, pl.BlockSpec((None, None, k, nb), lambda i, kk: (layer, kk, 0, 0)), row, vec, row],
        [row, vec],
        [jax.ShapeDtypeStruct((t, k), F32), jax.ShapeDtypeStruct((1, k), F32)],
        scratch=[pltpu.VMEM((tm, k), F32)],
        vmem_bytes=2 * (_nbytes((tm, nb), BF16) + _nbytes((k, nb), BF16)) + 10 * _nbytes((tm, k), F32),
    )(dz, w_g, x, g.reshape(1, k), dx_up)
    return dx, dg.reshape(k)


def _ffn_gate_up(name, h, w_g, layer):
    t, k = h.shape
    nb = w_g.shape[3]
    half = N_DEV // 2
    tm = _tile(t, 1024)

    def body(h_ref, wg_ref, wu_ref, gu_ref, act_ref):
        hb = h_ref[...]
        gate = jnp.dot(hb, wg_ref[...], preferred_element_type=F32)
        up = jnp.dot(hb, wu_ref[...], preferred_element_type=F32)
        gu_ref[0] = gate.astype(BF16)
        gu_ref[1] = up.astype(BF16)
        act_ref[...] = (gate * _sigmoid(gate) * up).astype(BF16)

    return _call(
        body, name, (t // tm, half),
        [pl.BlockSpec((tm, k), lambda i, j: (i, 0)),
         pl.BlockSpec((None, None, k, nb), lambda i, j: (layer, j, 0, 0)),
         pl.BlockSpec((None, None, k, nb), lambda i, j: (layer, half + j, 0, 0))],
        [pl.BlockSpec((2, None, tm, nb), lambda i, j: (0, j, i, 0)),
         pl.BlockSpec((None, tm, nb), lambda i, j: (j, i, 0))],
        [jax.ShapeDtypeStruct((2, half, t, nb), BF16), jax.ShapeDtypeStruct((half, t, nb), BF16)],
        vmem_bytes=2 * (_nbytes((tm, k), BF16) + 2 * _nbytes((k, nb), BF16) + 3 * _nbytes((tm, nb), BF16))
        + 6 * _nbytes((tm, nb), F32),
    )(h, w_g, w_g)


def _ffn_down_dx(name, dy, w4, layer, gu4):
    t, n = dy.shape
    nblk, kb = w4.shape[1], w4.shape[2]
    tm = _tile(t, 1024)

    def body(dy_ref, w_ref, gu_ref, dgu_ref):
        da = lax.dot_general(dy_ref[...].astype(BF16), w_ref[...], NT, preferred_element_type=F32)
        gate = gu_ref[0].astype(F32)
        up = gu_ref[1].astype(F32)
        sig = _sigmoid(gate)
        dgu_ref[0] = (da * up * (sig * (1.0 + gate * (1.0 - sig)))).astype(BF16)
        dgu_ref[1] = (da * (gate * sig)).astype(BF16)

    blk = pl.BlockSpec((2, None, tm, kb), lambda i, j: (0, j, i, 0))
    return _call(
        body, name, (t // tm, nblk),
        [pl.BlockSpec((tm, n), lambda i, j: (i, 0)),
         pl.BlockSpec((None, None, kb, n), lambda i, j: (layer, j, 0, 0)),
         blk],
        blk,
        jax.ShapeDtypeStruct((2, nblk, t, kb), BF16),
        vmem_bytes=2 * (_nbytes((tm, n), F32) + _nbytes((kb, n), BF16) + 4 * _nbytes((tm, kb), BF16))
        + 8 * _nbytes((tm, kb), F32),
    )(dy, w4, gu4)


def _mm_t_natural(name, dy, w, layer):
    t, n = dy.shape
    k = w.shape[1]
    tm = _tile(t, 1024)
    tk = _tile(k, 512)
    return _mm(
        name, NT, dy, w, grid=(t // tm, k // tk, 1),
        a_spec=pl.BlockSpec((tm, n), lambda i, j, kk: (i, 0)),
        b_spec=pl.BlockSpec((None, tk, n), lambda i, j, kk: (layer, j, 0)),
        out_shape=jax.ShapeDtypeStruct((t, k), BF16),
        out_spec=pl.BlockSpec((tm, tk), lambda i, j, kk: (i, j)),
        acc_shape=(tm, tk))


def _grad_buf(buf, shape):
    return jax.ShapeDtypeStruct(shape, BF16) if buf is None else jax.ShapeDtypeStruct(buf.shape, buf.dtype)


def _mm_dw_colblock(name, h, dz, buf, layer, n_layers, blocked_in=False):
    t, k = h.shape
    nb = dz.shape[2] if blocked_in else dz.shape[1] // N_DEV
    tk = _tile(t, 1024)
    if blocked_in:
        b_spec = pl.BlockSpec((None, tk, nb), lambda i, j, kk: (j, kk, 0))
    else:
        b_spec = pl.BlockSpec((tk, nb), lambda i, j, kk: (kk, j))
    return _mm(
        name, TN, h, dz, grid=(1, N_DEV, t // tk),
        a_spec=pl.BlockSpec((tk, k), lambda i, j, kk: (kk, 0)),
        b_spec=b_spec,
        out_shape=_grad_buf(buf, (n_layers, N_DEV, k, nb)),
        out_spec=pl.BlockSpec((None, None, k, nb), lambda i, j, kk: (layer, j, 0, 0)),
        acc_shape=(k, nb), alias=buf)


def _mm_dw_natural(name, a, dy, buf, layer, n_layers):
    t, k = a.shape
    n = dy.shape[1]
    tko = _tile(k, 1024)
    tt = _tile(t, 1024)
    return _mm(
        name, TN, a, dy, grid=(k // tko, 1, t // tt),
        a_spec=pl.BlockSpec((tt, tko), lambda i, j, kk: (kk, i)),
        b_spec=pl.BlockSpec((tt, n), lambda i, j, kk: (kk, 0)),
        out_shape=_grad_buf(buf, (n_layers, k, n)),
        out_spec=pl.BlockSpec((None, tko, n), lambda i, j, kk: (layer, i, 0)),
        acc_shape=(tko, n), alias=buf)


def _mm_dw_down(name, act, dy, buf, layer, n_layers):
    nblk, t, kb = act.shape
    n = dy.shape[1]
    tt = _tile(t, 1024)
    return _mm(
        name, TN, act, dy, grid=(nblk, 1, t // tt),
        a_spec=pl.BlockSpec((None, tt, kb), lambda i, j, kk: (i, kk, 0)),
        b_spec=pl.BlockSpec((tt, n), lambda i, j, kk: (kk, 0)),
        out_shape=_grad_buf(buf, (n_layers, nblk, kb, n)),
        out_spec=pl.BlockSpec((None, None, kb, n), lambda i, j, kk: (layer, i, 0, 0)),
        acc_shape=(kb, n), alias=buf)


def _spatial_mask(transposed=False):
    r = lax.broadcasted_iota(jnp.int32, (A_CHUNK, A_CHUNK), 0) // CHUNK
    c = lax.broadcasted_iota(jnp.int32, (A_CHUNK, A_CHUNK), 1) // CHUNK
    return c >= r if transposed else r >= c


def _sgu_tile(t):
    return _tile(t, 2 * A_CHUNK)


def _sgu_fwd(zpre, g_sgu, w_sp, b_full, name):
    t, f2 = zpre.shape
    f = f2 // 2
    gd = f // A_GROUPS
    tm = _sgu_tile(t)

    def body(z_ref, g_ref, w_ref, b_ref, p_ref):
        mask = _spatial_mask()
        wm = [jnp.where(mask, w_ref[g], 0.0).astype(BF16) for g in range(A_GROUPS)]
        for c in range(tm // A_CHUNK):
            rows = pl.ds(c * A_CHUNK, A_CHUNK)
            z = _gelu(z_ref[rows, :].astype(F32))
            u = z[:, :f]
            v0 = z[:, f:]
            r = lax.rsqrt(jnp.mean(v0 * v0, axis=-1, keepdims=True) + EPS)
            v1 = (v0 * r * g_ref[...]).astype(BF16)
            for g in range(A_GROUPS):
                cols = slice(g * gd, (g + 1) * gd)
                v2 = jnp.dot(wm[g], v1[:, cols], preferred_element_type=F32) + b_ref[:, cols]
                p_ref[rows, cols] = (u[:, cols] * v2).astype(BF16)

    return _call(
        body, name, (t // tm,),
        [pl.BlockSpec((tm, f2), lambda i: (i, 0)),
         pl.BlockSpec((1, f), lambda i: (0, 0)),
         pl.BlockSpec((A_GROUPS, A_CHUNK, A_CHUNK), lambda i: (0, 0, 0)),
         pl.BlockSpec((A_CHUNK, f), lambda i: (0, 0))],
        pl.BlockSpec((tm, f), lambda i: (i, 0)),
        jax.ShapeDtypeStruct((t, f), BF16),
        vmem_bytes=2 * _nbytes((tm, f2), BF16) + 2 * _nbytes((tm, f), BF16) + 8 * _nbytes((A_CHUNK, f2), F32),
    )(zpre, g_sgu.reshape(1, f), w_sp, b_full)


def _sgu_bwd(zpre, dp, g_sgu, w_sp, w_sp_t, b_full, name):
    t, f2 = zpre.shape
    f = f2 // 2
    gd = f // A_GROUPS
    tm = _sgu_tile(t)
    n_steps = t // tm

    def body(z_ref, dp_ref, g_ref, w_ref, wt_ref, b_ref, dz_ref, dw_ref, db_ref, dg_ref, dv1_ref, dbf_ref):
        step = pl.program_id(0)

        @pl.when(step == 0)
        def _():
            dw_ref[...] = jnp.zeros_like(dw_ref)
            dg_ref[...] = jnp.zeros_like(dg_ref)
            dbf_ref[...] = jnp.zeros_like(dbf_ref)

        mask = _spatial_mask()
        mask_t = _spatial_mask(transposed=True)
        wm = [jnp.where(mask, w_ref[g], 0.0).astype(BF16) for g in range(A_GROUPS)]
        wmt = [jnp.where(mask_t, wt_ref[g], 0.0).astype(BF16) for g in range(A_GROUPS)]
        gain = g_ref[...]
        for c in range(tm // A_CHUNK):
            rows = pl.ds(c * A_CHUNK, A_CHUNK)
            z, dgelu = _gelu_and_grad(z_ref[rows, :].astype(F32))
            u = z[:, :f]
            v0 = z[:, f:]
            r = lax.rsqrt(jnp.mean(v0 * v0, axis=-1, keepdims=True) + EPS)
            xhat = v0 * r
            v1 = (xhat * gain).astype(BF16)
            dpf = dp_ref[rows, :].astype(F32)
            for g in range(A_GROUPS):
                cols = slice(g * gd, (g + 1) * gd)
                v1g = v1[:, cols]
                v2 = jnp.dot(wm[g], v1g, preferred_element_type=F32) + b_ref[:, cols]
                dpg = dpf[:, cols]
                dz_ref[rows, cols] = (dpg * v2 * dgelu[:, cols]).astype(BF16)
                dv2 = dpg * u[:, cols]
                dbf_ref[:, cols] += dv2
                dv2b = dv2.astype(BF16)
                dwg = lax.dot_general(dv2b, v1g, NT, preferred_element_type=F32)
                dw_ref[g] += jnp.where(mask, dwg, 0.0)
                dv1_ref[:, cols] = jnp.dot(wmt[g], dv2b, preferred_element_type=F32)
            dv1 = dv1_ref[...]
            dxhat = dv1 * gain
            dg_ref[...] += jnp.sum(dv1 * xhat, axis=0, keepdims=True)
            dv0 = r * (dxhat - xhat * jnp.mean(dxhat * xhat, axis=-1, keepdims=True))
            dz_ref[rows, pl.ds(f, f)] = (dv0 * dgelu[:, f:]).astype(BF16)

        @pl.when(step == n_steps - 1)
        def _():
            for g in range(A_GROUPS):
                db_ref[g] = jnp.sum(dbf_ref[:, g * gd:(g + 1) * gd], axis=1, keepdims=True)

    wspec = pl.BlockSpec((A_GROUPS, A_CHUNK, A_CHUNK), lambda i: (0, 0, 0))
    dz, dw, db, dg = _call(
        body, name, (n_steps,),
        [pl.BlockSpec((tm, f2), lambda i: (i, 0)),
         pl.BlockSpec((tm, f), lambda i: (i, 0)),
         pl.BlockSpec((1, f), lambda i: (0, 0)),
         wspec, wspec,
         pl.BlockSpec((A_CHUNK, f), lambda i: (0, 0))],
        [pl.BlockSpec((tm, f2), lambda i: (i, 0)),
         wspec,
         pl.BlockSpec((A_GROUPS, A_CHUNK, 1), lambda i: (0, 0, 0)),
         pl.BlockSpec((1, f), lambda i: (0, 0))],
        [jax.ShapeDtypeStruct((t, f2), BF16),
         jax.ShapeDtypeStruct((A_GROUPS, A_CHUNK, A_CHUNK), F32),
         jax.ShapeDtypeStruct((A_GROUPS, A_CHUNK, 1), F32),
         jax.ShapeDtypeStruct((1, f), F32)],
        scratch=[pltpu.VMEM((A_CHUNK, f), F32), pltpu.VMEM((A_CHUNK, f), F32)],
        vmem_bytes=4 * _nbytes((tm, f2), BF16) + 2 * _nbytes((tm, f), BF16) + 12 * _nbytes((A_CHUNK, f2), F32),
    )(zpre, dp, g_sgu.reshape(1, f), w_sp, w_sp_t, b_full)
    return dz, dw, db.reshape(A_GROUPS, A_CHUNK), dg.reshape(f)


def _pair_valid(qi, col):
    qc = qi // CHUNK
    kc = col // CHUNK
    return (kc >= qc) & (kc <= qc + N_LEFT_CHUNKS)


def _diagonal_onehot():
    e = lax.broadcasted_iota(jnp.int32, (REL_PAD, K_BLOCK), 1)
    idx = jnp.clip(PAIR_BAND - 1 - e, -MAX_REL, MAX_REL) + MAX_REL
    r = lax.broadcasted_iota(jnp.int32, (REL_PAD, K_BLOCK), 0)
    return jnp.where(r == idx, 1.0, 0.0).astype(BF16)


def _bias_build(table, name):
    h = table.shape[0]
    tab = jnp.pad(table, ((0, 0), (0, REL_PAD - N_REL)))

    def body(t_ref, o_ref):
        oh = _diagonal_onehot()
        diag = jnp.zeros((h, K_BLOCK), F32)
        for piece in _split3(t_ref[...]):
            diag += jnp.dot(piece, oh, preferred_element_type=F32)
        col = lax.broadcasted_iota(jnp.int32, (h, PAIR_BAND), 1)
        for qi in range(PAIR_ROWS):
            row = pltpu.roll(diag, (qi - (PAIR_ROWS - 1)) % K_BLOCK, 1)[:, :PAIR_BAND]
            o_ref[qi] = jnp.where(_pair_valid(qi, col), row, NEG_INF)

    out = _call(
        body, name, (1,),
        [pl.BlockSpec((h, REL_PAD), lambda i: (0, 0))],
        pl.BlockSpec((PAIR_ROWS, h, PAIR_BAND), lambda i: (0, 0, 0)),
        jax.ShapeDtypeStruct((PAIR_ROWS, h, PAIR_BAND), F32),
        vmem_bytes=4 * _nbytes((PAIR_ROWS, h, PAIR_BAND), F32),
    )(tab)
    return jnp.transpose(out, (1, 0, 2))


def _bias_block(pair_bias):
    rest = K_BLOCK - PAIR_BAND
    top = jnp.pad(pair_bias, ((0, 0), (0, 0), (0, rest)), constant_values=NEG_INF)
    bottom = jnp.pad(pair_bias, ((0, 0), (0, 0), (rest, 0)), constant_values=NEG_INF)
    return jnp.concatenate([top, bottom], axis=1)


def _bias_grad(dbias, name):
    h = dbias.shape[0]
    db_t = jnp.transpose(dbias, (1, 0, 2))

    def body(d_ref, o_ref):
        diag = jnp.zeros((h, K_BLOCK), F32)
        for qi in range(PAIR_ROWS):
            diag += pltpu.roll(d_ref[qi], PAIR_ROWS - 1 - qi, 1)
        oh = _diagonal_onehot()
        acc = jnp.zeros((h, REL_PAD), F32)
        for piece in _split3(diag):
            acc += lax.dot_general(piece, oh, NT, preferred_element_type=F32)
        o_ref[...] = acc

    out = _call(
        body, name, (1,),
        [pl.BlockSpec((PAIR_ROWS, h, K_BLOCK), lambda i: (0, 0, 0))],
        pl.BlockSpec((h, REL_PAD), lambda i: (0, 0)),
        jax.ShapeDtypeStruct((h, REL_PAD), F32),
        vmem_bytes=4 * _nbytes((PAIR_ROWS, h, K_BLOCK), F32),
    )(db_t)
    return out[:, :N_REL]


def _head_masks():
    lane = lax.broadcasted_iota(jnp.int32, (Q_BLOCK, HEAD_PAIR), 1)
    return lane < HEAD_DIM, lane >= HEAD_DIM


def _block_probs(qm, kb, bias, valid):
    s = lax.dot_general(qm, kb, NT, preferred_element_type=F32) + bias
    s = jnp.where(valid, s, NEG_INF)
    e = jnp.exp(s - jnp.max(s, axis=-1, keepdims=True))
    return e * (1.0 / jnp.sum(e, axis=-1, keepdims=True))


def _attn_fwd(q, kvpad, bias, name):
    t, d = q.shape
    n_pairs = d // HEAD_PAIR
    n_blocks = t // Q_BLOCK

    def body(q_ref, k_ref, v_ref, b_ref, o_ref):
        masks = _head_masks()
        key = lax.broadcasted_iota(jnp.int32, (Q_BLOCK, K_BLOCK), 1)

        def step(j, carry):
            r0 = pl.multiple_of(j * Q_BLOCK, Q_BLOCK)
            q2 = q_ref[pl.ds(r0, Q_BLOCK), :].astype(F32)
            kb = k_ref[pl.ds(r0, K_BLOCK), :]
            vb = v_ref[pl.ds(r0, K_BLOCK), :]
            valid = key >= LEFT - j * Q_BLOCK
            outs = []
            for a in range(2):
                qm = jnp.where(masks[a], q2, 0.0).astype(BF16)
                p = _block_probs(qm, kb, b_ref[a], valid)
                outs.append(jnp.dot(p.astype(BF16), vb, preferred_element_type=F32))
            o_ref[pl.ds(r0, Q_BLOCK), :] = jnp.where(masks[0], outs[0], outs[1]).astype(BF16)
            return carry

        lax.fori_loop(0, n_blocks, step, 0)

    return _call(
        body, name, (n_pairs,),
        [pl.BlockSpec((t, HEAD_PAIR), lambda p: (0, p)),
         pl.BlockSpec((LEFT + t, HEAD_PAIR), lambda p: (0, p)),
         pl.BlockSpec((LEFT + t, HEAD_PAIR), lambda p: (0, n_pairs + p)),
         pl.BlockSpec((2, Q_BLOCK, K_BLOCK), lambda p: (p, 0, 0))],
        pl.BlockSpec((t, HEAD_PAIR), lambda p: (0, p)),
        jax.ShapeDtypeStruct((t, d), BF16),
        vmem_bytes=8 * _nbytes((LEFT + t, HEAD_PAIR), BF16) + 12 * _nbytes((2, Q_BLOCK, K_BLOCK), F32),
    )(q, kvpad, kvpad, bias)


def _attn_bwd(q, kvpad, bias, do, dk_in, dv_in, name):
    t, d = q.shape
    n_pairs = d // HEAD_PAIR
    n_blocks = t // Q_BLOCK
    has_in = dk_in is not None

    def body(*refs):
        refs = list(refs)
        q_ref, k_ref, v_ref, b_ref, do_ref = refs[:5]
        refs = refs[5:]
        if has_in:
            dki_ref, dvi_ref = refs[:2]
            refs = refs[2:]
        dq_ref, dk_ref, dv_ref, db_ref = refs
        masks = _head_masks()
        key = lax.broadcasted_iota(jnp.int32, (Q_BLOCK, K_BLOCK), 1)
        if has_in:
            dk_ref[...] = dki_ref[...]
            dv_ref[...] = dvi_ref[...]
        else:
            dk_ref[...] = jnp.zeros_like(dk_ref)
            dv_ref[...] = jnp.zeros_like(dv_ref)
        db_ref[...] = jnp.zeros_like(db_ref)

        def step(j, carry):
            r0 = pl.multiple_of(j * Q_BLOCK, Q_BLOCK)
            q2 = q_ref[pl.ds(r0, Q_BLOCK), :].astype(F32)
            do2 = do_ref[pl.ds(r0, Q_BLOCK), :].astype(F32)
            kb = k_ref[pl.ds(r0, K_BLOCK), :]
            vb = v_ref[pl.ds(r0, K_BLOCK), :]
            valid = key >= LEFT - j * Q_BLOCK
            dqs = []
            dk_acc = jnp.zeros((K_BLOCK, HEAD_PAIR), F32)
            dv_acc = jnp.zeros((K_BLOCK, HEAD_PAIR), F32)
            for a in range(2):
                qm = jnp.where(masks[a], q2, 0.0).astype(BF16)
                dom = jnp.where(masks[a], do2, 0.0).astype(BF16)
                p = _block_probs(qm, kb, b_ref[a], valid)
                dp = lax.dot_general(dom, vb, NT, preferred_element_type=F32)
                ds = p * (dp - jnp.sum(dp * p, axis=-1, keepdims=True))
                db_ref[a] += ds[:PAIR_ROWS, :]
                db_ref[a, :, pl.ds(0, PAIR_BAND)] += ds[PAIR_ROWS:, K_BLOCK - PAIR_BAND:]
                dsb = ds.astype(BF16)
                dqs.append(jnp.dot(dsb, kb, preferred_element_type=F32))
                dk_acc += lax.dot_general(dsb, qm, TN, preferred_element_type=F32)
                dv_acc += lax.dot_general(p.astype(BF16), dom, TN, preferred_element_type=F32)
            dq = jnp.where(masks[0], dqs[0], dqs[1]) * ATTN_SCALE
            dq_ref[pl.ds(r0, Q_BLOCK), :] = dq.astype(BF16)
            dk_ref[pl.ds(r0, K_BLOCK), :] += dk_acc
            dv_ref[pl.ds(r0, K_BLOCK), :] += dv_acc
            return carry

        lax.fori_loop(0, n_blocks, step, 0)

    q_spec = pl.BlockSpec((t, HEAD_PAIR), lambda p: (0, p))
    kv_spec = pl.BlockSpec((LEFT + t, HEAD_PAIR), lambda p: (0, p))
    operands = [q, kvpad, kvpad, bias, do]
    in_specs = [q_spec, kv_spec, pl.BlockSpec((LEFT + t, HEAD_PAIR), lambda p: (0, n_pairs + p)),
                pl.BlockSpec((2, Q_BLOCK, K_BLOCK), lambda p: (p, 0, 0)), q_spec]
    aliases = None
    if has_in:
        operands += [dk_in, dv_in]
        in_specs += [kv_spec, kv_spec]
        aliases = {5: 1, 6: 2}
    return _call(
        body, name, (n_pairs,),
        in_specs,
        [q_spec, kv_spec, kv_spec, pl.BlockSpec((2, PAIR_ROWS, K_BLOCK), lambda p: (p, 0, 0))],
        [jax.ShapeDtypeStruct((t, d), BF16),
         jax.ShapeDtypeStruct((LEFT + t, d), F32),
         jax.ShapeDtypeStruct((LEFT + t, d), F32),
         jax.ShapeDtypeStruct((d // HEAD_DIM, PAIR_ROWS, K_BLOCK), F32)],
        vmem_bytes=10 * _nbytes((LEFT + t, HEAD_PAIR), BF16) + 8 * _nbytes((LEFT + t, HEAD_PAIR), F32)
        + 16 * _nbytes((2, Q_BLOCK, K_BLOCK), F32),
        aliases=aliases,
    )(*operands)


def _loss_head(x, g, target, name):
    t, d = x.shape
    tm = _tile(t, 512)

    def body(x_ref, g_ref, t_ref, dx_ref, loss_ref, dg_ref):
        @pl.when(pl.program_id(0) == 0)
        def _():
            loss_ref[...] = jnp.zeros_like(loss_ref)
            dg_ref[...] = jnp.zeros_like(dg_ref)

        xf = x_ref[...]
        r = lax.rsqrt(jnp.mean(xf * xf, axis=-1, keepdims=True) + EPS)
        xhat = xf * r
        diff = xhat * g_ref[...] - t_ref[...]
        row_loss = jnp.mean(diff * diff, axis=-1, keepdims=True)
        loss_ref[...] += 0.5 * jnp.sum(row_loss, axis=0, keepdims=True)
        dy = diff * (1.0 / d)
        dg_ref[...] += jnp.sum(dy * xhat, axis=0, keepdims=True)
        dxhat = dy * g_ref[...]
        dx_ref[...] = r * (dxhat - xhat * jnp.mean(dxhat * xhat, axis=-1, keepdims=True))

    row = pl.BlockSpec((tm, d), lambda i: (i, 0))
    vec = pl.BlockSpec((1, d), lambda i: (0, 0))
    dx, loss, dg = _call(
        body, name, (t // tm,),
        [row, vec, row],
        [row, pl.BlockSpec((1, 1), lambda i: (0, 0)), vec],
        [jax.ShapeDtypeStruct((t, d), F32), jax.ShapeDtypeStruct((1, 1), F32), jax.ShapeDtypeStruct((1, d), F32)],
        vmem_bytes=10 * _nbytes((tm, d), F32),
    )(x, g.reshape(1, d), target)
    return dx, loss[0, 0], dg.reshape(d)


def _adamw(parts, w, m, v, name):
    n_layers, n_src, r, c = parts.shape
    mult = 16 if parts.dtype == BF16 else 8
    tr = _row_tile(r, max(mult, (256 * 1024) // c), mult)
    c1 = 1.0 / (1.0 - ADAM_B1 ** ADAM_STEP)
    c2 = 1.0 / (1.0 - ADAM_B2 ** ADAM_STEP)

    def body(p_ref, w_ref, m_ref, v_ref, g_ref, d_ref, nm_ref, nv_ref):
        g = p_ref[0].astype(F32)
        for s in range(1, n_src):
            g = g + p_ref[s].astype(F32)
        nm = ADAM_B1 * m_ref[...] + (1.0 - ADAM_B1) * g
        nv = ADAM_B2 * v_ref[...] + (1.0 - ADAM_B2) * (g * g)
        g_ref[...] = g
        nm_ref[...] = nm
        nv_ref[...] = nv
        d_ref[...] = -ADAM_LR * ((nm * c1) / (jnp.sqrt(nv * c2) + ADAM_EPS) + ADAM_WD * w_ref[...])

    blk = pl.BlockSpec((None, tr, c), lambda l, i: (l, i, 0))
    out = jax.ShapeDtypeStruct((n_layers, r, c), F32)
    return _call(
        body, name, (n_layers, r // tr),
        [pl.BlockSpec((None, n_src, tr, c), lambda l, i: (l, 0, i, 0)), blk, blk, blk],
        [blk, blk, blk, blk],
        [out, out, out, out],
        vmem_bytes=2 * _nbytes((n_src, tr, c), parts.dtype) + 18 * _nbytes((tr, c), F32),
    )(parts, w, m, v)


def _ordered_sum(parts, name):
    n_src, r, c = parts.shape

    def body(p_ref, o_ref):
        acc = p_ref[0]
        for s in range(1, n_src):
            acc = acc + p_ref[s]
        o_ref[...] = acc

    return _call(
        body, name, (1,),
        [pl.BlockSpec((n_src, r, c), lambda i: (0, 0, 0))],
        pl.BlockSpec((r, c), lambda i: (0, 0)),
        jax.ShapeDtypeStruct((r, c), F32),
        vmem_bytes=4 * _nbytes((n_src, r, c), F32),
    )(parts)


def _position():
    return lax.axis_index("x"), lax.axis_index("y"), lax.axis_index("c")


def _linear(p):
    return 4 * p[0] + 2 * p[1] + p[2]


def _all_gather(shards, name):
    n = len(shards)

    def body(*refs):
        ins, outs = refs[:n], refs[n:2 * n]
        send_sems, recv_sems, local_sems = refs[2 * n:]
        x, y, c = _position()
        me, sibling = (x, y, c), (x, y, 1 - c)
        chips = [(1 - x, y), (x, 1 - y), (1 - x, 1 - y)]

        def slab(t, p):
            return outs[t].at[:, _linear(p)]

        def copy(t, k, block, to, src=None):
            return pltpu.make_async_remote_copy(
                src_ref=slab(t, block) if src is None else src,
                dst_ref=slab(t, block),
                send_sem=send_sems.at[t, k],
                recv_sem=recv_sems.at[t, k],
                device_id=to,
                device_id_type=MESH,
            )

        started = []
        for t in range(n):
            mine = pltpu.make_async_copy(ins[t], slab(t, me), local_sems.at[t])
            mine.start()
            started.append(mine)
        sends = []
        for t in range(n):
            first = [copy(t, 0, me, sibling, src=ins[t])]
            first += [copy(t, 1 + j, me, (*chip, c), src=ins[t]) for j, chip in enumerate(chips)]
            for cp in first:
                cp.start()
            sends += first
        for t in range(n):
            for j, chip in enumerate(chips):
                copy(t, 1 + j, (*chip, c), me).wait_recv()
                passed = copy(t, 4 + j, (*chip, c), sibling)
                passed.start()
                sends.append(passed)
        for t in range(n):
            copy(t, 0, sibling, me).wait_recv()
            for j, chip in enumerate(chips):
                copy(t, 4 + j, (*chip, 1 - c), me).wait_recv()
        for cp in sends:
            cp.wait_send()
        for mine in started:
            mine.wait()

    out_shape = [jax.ShapeDtypeStruct((s.shape[0], N_DEV) + s.shape[1:], s.dtype) for s in shards]
    return pl.pallas_call(
        body,
        name=name,
        in_specs=[HBM_SPEC] * n,
        out_specs=[HBM_SPEC] * n,
        out_shape=out_shape,
        scratch_shapes=[
            pltpu.SemaphoreType.DMA((n, N_DEV - 1)),
            pltpu.SemaphoreType.DMA((n, N_DEV - 1)),
            pltpu.SemaphoreType.DMA((n,)),
        ],
    )(*shards)


def _exchange(blocks, name):
    n = len(blocks)

    def body(*refs):
        ins, outs = refs[:n], refs[n:2 * n]
        send_sems, recv_sems, local_sems = refs[2 * n:]
        x, y, c = _position()
        me = _linear((x, y, c))
        flips = [(fx, fy, fc) for fx in (0, 1) for fy in (0, 1) for fc in (0, 1)][1:]

        def peer_of(flip):
            fx, fy, fc = flip
            return (1 - x if fx else x, 1 - y if fy else y, 1 - c if fc else c)

        def copy(t, k, peer):
            return pltpu.make_async_remote_copy(
                src_ref=ins[t].at[:, _linear(peer)],
                dst_ref=outs[t].at[:, me],
                send_sem=send_sems.at[t, k],
                recv_sem=recv_sems.at[t, k],
                device_id=peer,
                device_id_type=MESH,
            )

        def arrival(t, k, peer):
            return pltpu.make_async_remote_copy(
                src_ref=ins[t].at[:, _linear(peer)],
                dst_ref=outs[t].at[:, _linear(peer)],
                send_sem=send_sems.at[t, k],
                recv_sem=recv_sems.at[t, k],
                device_id=peer,
                device_id_type=MESH,
            )

        own = []
        for t in range(n):
            cp = pltpu.make_async_copy(ins[t].at[:, me], outs[t].at[:, me], local_sems.at[t])
            cp.start()
            own.append(cp)
        sends = []
        for t in range(n):
            for k, flip in enumerate(flips):
                cp = copy(t, k, peer_of(flip))
                cp.start()
                sends.append(cp)
        for t in range(n):
            for k, flip in enumerate(flips):
                arrival(t, k, peer_of(flip)).wait_recv()
        for cp in sends:
            cp.wait_send()
        for cp in own:
            cp.wait()

    out_shape = [jax.ShapeDtypeStruct(b.shape, b.dtype) for b in blocks]
    return pl.pallas_call(
        body,
        name=name,
        in_specs=[HBM_SPEC] * n,
        out_specs=[HBM_SPEC] * n,
        out_shape=out_shape,
        scratch_shapes=[
            pltpu.SemaphoreType.DMA((n, N_DEV - 1)),
            pltpu.SemaphoreType.DMA((n, N_DEV - 1)),
            pltpu.SemaphoreType.DMA((n,)),
        ],
    )(*blocks)


def _peers():
    x, y, c = _position()
    flips = [(fx, fy, fc) for fx in (0, 1) for fy in (0, 1) for fc in (0, 1)][1:]
    return [(1 - x if fx else x, 1 - y if fy else y, 1 - c if fc else c) for fx, fy, fc in flips]


def _split_start(srcs, lands, carry, name, exchange=False):
    n = len(srcs)

    def body(*refs):
        src_refs, land_refs = refs[:n], refs[n:2 * n]
        send_sems, recv_sems = refs[2 * n + 1], refs[2 * n + 2]
        me = _linear(_position())
        for t in range(n):
            for k, peer in enumerate(_peers()):
                pltpu.make_async_remote_copy(
                    src_ref=src_refs[t].at[_linear(peer)] if exchange else src_refs[t],
                    dst_ref=land_refs[t].at[me],
                    send_sem=send_sems.at[t * (N_DEV - 1) + k],
                    recv_sem=recv_sems.at[t * (N_DEV - 1) + k],
                    device_id=peer,
                    device_id_type=MESH,
                ).start()

    operands = list(srcs) + list(lands) + [carry]
    sems = pltpu.SemaphoreType.DMA((n * (N_DEV - 1),))
    out = pl.pallas_call(
        body,
        name=name,
        in_specs=[HBM_SPEC] * len(operands),
        out_specs=[SEM_SPEC, SEM_SPEC] + [HBM_SPEC] * len(operands),
        out_shape=[sems, sems] + [pltpu.HBM(a.shape, a.dtype) for a in operands],
        input_output_aliases={i: 2 + i for i in range(len(operands))},
        compiler_params=pltpu.CompilerParams(has_side_effects=pltpu.SideEffectType.DATAFLOW_SIDE_EFFECTING),
    )(*[pltpu.with_memory_space_constraint(a, pltpu.HBM) for a in operands])
    return out[0], out[1], out[2:2 + n], out[2 + n:2 + 2 * n], out[2 + 2 * n]


def _split_wait(send_sems, recv_sems, srcs, lands, after, name, exchange=False):
    n = len(srcs)

    def body(*refs):
        src_refs, land_refs = refs[:n], refs[n:2 * n]
        send_ref, recv_ref = refs[2 * n], refs[2 * n + 1]
        for t in range(n):
            for k, peer in enumerate(_peers()):
                copy = pltpu.make_async_remote_copy(
                    src_ref=src_refs[t].at[0] if exchange else src_refs[t],
                    dst_ref=land_refs[t].at[0],
                    send_sem=send_ref.at[t * (N_DEV - 1) + k],
                    recv_sem=recv_ref.at[t * (N_DEV - 1) + k],
                    device_id=peer,
                    device_id_type=MESH,
                )
                copy.wait_send()
                copy.wait_recv()

    arrays = list(srcs) + list(lands)
    out = pl.pallas_call(
        body,
        name=name,
        in_specs=[HBM_SPEC] * len(arrays) + [SEM_SPEC, SEM_SPEC, pl.BlockSpec(memory_space=pl.ANY)],
        out_specs=[HBM_SPEC] * len(arrays),
        out_shape=[pltpu.HBM(a.shape, a.dtype) for a in arrays],
        input_output_aliases={i: i for i in range(len(arrays))},
        compiler_params=pltpu.CompilerParams(has_side_effects=pltpu.SideEffectType.DATAFLOW_SIDE_EFFECTING),
    )(*arrays, send_sems, recv_sems, after)
    return out[:n], out[n:]


def _pack(arrays, row_multiple):
    flat = jnp.concatenate([a.reshape(-1) for a in arrays])
    quantum = row_multiple * FLAT_LANES
    padded = -(-flat.shape[0] // quantum) * quantum
    return jnp.pad(flat, (0, padded - flat.shape[0])).reshape(-1, FLAT_LANES)


def _unpack(flat, like):
    flat = flat.reshape(-1)
    out, at = [], 0
    for a in like:
        size = math.prod(a.shape)
        out.append(flat[at:at + size].reshape(a.shape))
        at += size
    return out


def kernel(x, a_norm, a_w_in, a_sgu_norm, a_w_spatial, a_b_spatial, a_w_out, kv_norm, w_kv, b_norm, b_w_q, b_rel_bias, b_w_o, ffn_norm, ffn_w_gate_up, ffn_w_down, final_norm, loss_target, m_a_norm, m_a_w_in, m_a_sgu_norm, m_a_w_spatial, m_a_b_spatial, m_a_w_out, m_kv_norm, m_w_kv, m_b_norm, m_b_w_q, m_b_rel_bias, m_b_w_o, m_ffn_norm, m_ffn_w_gate_up, m_ffn_w_down, m_final_norm, v_a_norm, v_a_w_in, v_a_sgu_norm, v_a_w_spatial, v_a_b_spatial, v_a_w_out, v_kv_norm, v_w_kv, v_b_norm, v_b_w_q, v_b_rel_bias, v_b_w_o, v_ffn_norm, v_ffn_w_gate_up, v_ffn_w_down, v_final_norm):
    xs = x[0]
    target = loss_target[0]
    t, d = xs.shape
    n_a = a_w_in.shape[0]
    n_b = b_w_q.shape[0]
    depth = ffn_w_gate_up.shape[0]
    f_a = a_w_out.shape[1] * N_DEV
    gd = f_a // A_GROUPS
    nb_ffn = ffn_w_gate_up.shape[2]
    me = _linear(_position())

    small_rows = -(-(a_norm.size + a_sgu_norm.size) // (8 * 128)) * 8
    small = jnp.pad(jnp.concatenate([a_norm.reshape(-1), a_sgu_norm.reshape(-1)]),
                    (0, small_rows * 128 - a_norm.size - a_sgu_norm.size)).reshape(1, small_rows, 128)

    def shard(w, layer=None):
        return (w if layer is None else w[layer]).astype(BF16)

    stages = []
    for layer in range(depth):
        if layer < n_a:
            stages.append((f"a{layer}", [shard(a_w_in, layer), shard(a_w_out, layer)]))
        else:
            i = layer - n_a
            shared = [shard(w_kv)] if i == 0 else []
            stages.append((f"b{i}", shared + [shard(b_w_q, i), shard(b_w_o, i)]))
        stages.append((f"f{layer}", [shard(ffn_w_gate_up, layer), shard(ffn_w_down, layer)]))
    first = _all_gather([s[None] for s in stages[0][1]] + [small], "gather_first")
    gathered = {stages[0][0]: [g[0] for g in first[:-1]]}
    small_g = first[-1].reshape(N_DEV, -1)
    a_norm_full = small_g[:, :a_norm.size].reshape(N_DEV, n_a, -1).transpose(1, 0, 2).reshape(n_a, d)
    a_sgu_full = small_g[:, a_norm.size:a_norm.size + a_sgu_norm.size].reshape(
        N_DEV, n_a, -1).transpose(1, 0, 2).reshape(n_a, f_a)
    in_flight = {}
    for key, shards in stages[1:]:
        lands = [lax.dynamic_update_slice(lax.empty((N_DEV,) + s.shape, BF16), s[None], (me, 0, 0)) for s in shards]
        send, recv, srcs, lands, a_norm_full = _split_start(shards, lands, a_norm_full, f"gather_start_{key}")
        in_flight[key] = (send, recv, srcs, lands)

    def weights(key, after):
        if key not in gathered:
            _, gathered[key] = _split_wait(*in_flight.pop(key), after, f"gather_wait_{key}")
        return gathered[key]

    rows_down = ffn_w_down.shape[1]

    def mixer_a_weights(i, after):
        w_in, w_out = weights(f"a{i}", after)
        return w_in[None], w_out.reshape(1, f_a, d)

    def mixer_b_weights(i, after):
        ws = weights(f"b{i}", after)
        return ws[-2].reshape(1, d, d), ws[-1].reshape(1, d, d)

    def ffn_weights(layer, after):
        w_gu, w_dn = weights(f"f{layer}", after)
        return w_gu[None], w_dn.reshape(1, N_DEV // 2, 2 * rows_down, d)

    w_sp_t = jnp.swapaxes(a_w_spatial, -1, -2)
    b_full = jnp.repeat(jnp.swapaxes(a_b_spatial, -1, -2), gd, axis=-1)

    saved = []

    def ffn_fwd(xin, layer):
        hf = _rms_fwd(xin, ffn_norm[layer], f"ffn_norm_fwd_{layer}")
        w_gu, w_dn = ffn_weights(layer, xin)
        gu, act = _ffn_gate_up(f"ffn_gate_up_{layer}", hf, w_gu, 0)
        xout = _mm_down(f"ffn_down_{layer}", act, w_dn, 0, xin)
        return xout, (xin, hf, gu, act)

    for i in range(n_a):
        w_in, w_out = mixer_a_weights(i, xs)
        h = _rms_fwd(xs, a_norm_full[i], f"a_norm_fwd_{i}")
        zpre = _mm_colblock(f"a_in_{i}", h, w_in, 0)
        p = _sgu_fwd(zpre, a_sgu_full[i], a_w_spatial[i], b_full[i], f"a_sgu_fwd_{i}")
        x_mid = _mm_natural(f"a_out_{i}", p, w_out, 0, res=xs)
        x_out, ffn_saved = ffn_fwd(x_mid, i)
        saved.append((xs, h, zpre, p, ffn_saved))
        xs = x_out

    x_kv = xs
    w_kv_g = weights("b0", x_kv)[0][None]
    h_kv = _rms_fwd(x_kv, kv_norm, "kv_norm_fwd")
    kv = _mm_colblock("kv_proj", h_kv, w_kv_g, 0)
    kvpad = jnp.pad(kv, ((LEFT, 0), (0, 0)))

    biases = [_bias_block(_bias_build(b_rel_bias[i], f"rel_bias_{i}")) for i in range(n_b)]
    for i in range(n_b):
        layer = n_a + i
        w_q, w_o = mixer_b_weights(i, xs)
        hb = _rms_fwd(xs, b_norm[i], f"b_norm_fwd_{i}")
        q = _mm_natural(f"b_q_{i}", hb, w_q, 0, out_dtype=BF16, scale=ATTN_SCALE)
        o = _attn_fwd(q, kvpad, biases[i], f"b_attn_fwd_{i}")
        x_mid = _mm_natural(f"b_o_{i}", o, w_o, 0, res=xs)
        x_out, ffn_saved = ffn_fwd(x_mid, layer)
        saved.append((xs, hb, q, o, ffn_saved))
        xs = x_out

    dx, loss_local, g_final = _loss_head(xs, final_norm, target, "loss_head")
    loss = lax.psum(loss_local, ("x", "y", "c"))

    g_win = g_wout = g_wkv = g_wq = g_wo = g_wgu = g_wd = None
    g_ffn_norm = [None] * depth
    g_a_norm = [None] * n_a
    g_a_sgu = [None] * n_a
    g_w_sp = [None] * n_a
    g_b_sp = [None] * n_a
    g_b_norm = [None] * n_b
    g_rel = [None] * n_b

    def ffn_bwd(dx, layer, ffn_saved):
        nonlocal g_wgu, g_wd
        xin, hf, gu, act = ffn_saved
        g_wd = _mm_dw_down(f"ffn_down_dw_{layer}", act, dx, g_wd, layer, depth)
        w_gu, w_dn = ffn_weights(layer, xin)
        dgu = _ffn_down_dx(f"ffn_down_dx_{layer}", dx, w_dn, 0, gu).reshape(N_DEV, t, nb_ffn)
        g_wgu = _mm_dw_colblock(f"ffn_gate_up_dw_{layer}", hf, dgu, g_wgu, layer, depth, blocked_in=True)
        dx, g_ffn_norm[layer] = _mm_t_colblock_norm_bwd(
            f"ffn_gate_up_dx_{layer}", dgu, w_gu, 0, xin, ffn_norm[layer], dx, blocked_in=True)
        return dx

    dk = dv = None
    for i in reversed(range(n_b)):
        layer = n_a + i
        x_in, hb, q, o, ffn_saved = saved[layer]
        dx = ffn_bwd(dx, layer, ffn_saved)
        g_wo = _mm_dw_natural(f"b_o_dw_{i}", o, dx, g_wo, i, n_b)
        w_q, w_o = mixer_b_weights(i, x_in)
        do = _mm_t_natural(f"b_o_dx_{i}", dx, w_o, 0)
        dq, dk, dv, dbias = _attn_bwd(q, kvpad, biases[i], do, dk, dv, f"b_attn_bwd_{i}")
        g_rel[i] = _bias_grad(dbias, f"rel_bias_grad_{i}")
        g_wq = _mm_dw_natural(f"b_q_dw_{i}", hb, dq, g_wq, i, n_b)
        dh = _mm_t_natural(f"b_q_dx_{i}", dq, w_q, 0)
        dx, g_b_norm[i] = _rms_bwd(x_in, b_norm[i], dh, dx, f"b_norm_bwd_{i}")

    dkv = jnp.concatenate([dk[LEFT:], dv[LEFT:]], axis=1).astype(BF16)
    g_wkv = _mm_dw_colblock("kv_proj_dw", h_kv, dkv, None, 0, 1)
    dx, g_kv_norm = _mm_t_colblock_norm_bwd("kv_proj_dx", dkv, w_kv_g, 0, x_kv, kv_norm, dx)

    for i in reversed(range(n_a)):
        x_in, h, zpre, p, ffn_saved = saved[i]
        dx = ffn_bwd(dx, i, ffn_saved)
        g_wout = _mm_dw_natural(f"a_out_dw_{i}", p, dx, g_wout, i, n_a)
        w_in, w_out = mixer_a_weights(i, x_in)
        dp = _mm_t_natural(f"a_out_dx_{i}", dx, w_out, 0)
        dz, g_w_sp[i], g_b_sp[i], g_a_sgu[i] = _sgu_bwd(
            zpre, dp, a_sgu_full[i], a_w_spatial[i], w_sp_t[i], b_full[i], f"a_sgu_bwd_{i}")
        g_win = _mm_dw_colblock(f"a_in_dw_{i}", h, dz, g_win, i, n_a)
        dx, g_a_norm[i] = _mm_t_colblock_norm_bwd(f"a_in_dx_{i}", dz, w_in, 0, x_in, a_norm_full[i], dx)
    grad_x = dx[None]

    small_like = [jax.ShapeDtypeStruct((n_a, d), F32), jax.ShapeDtypeStruct((n_a, f_a), F32),
                  a_w_spatial, a_b_spatial, kv_norm, b_norm, b_rel_bias, ffn_norm, final_norm]
    small_partial = _pack(
        [jnp.stack(g_a_norm), jnp.stack(g_a_sgu), jnp.stack(g_w_sp), jnp.stack(g_b_sp), g_kv_norm,
         jnp.stack(g_b_norm), jnp.stack(g_rel), jnp.stack(g_ffn_norm), g_final], N_DEV * 8)
    chunk_rows = small_partial.shape[0] // N_DEV
    big_grads = [
        g_win,
        g_wout.reshape(n_a, N_DEV, f_a // N_DEV, d),
        g_wkv,
        g_wq.reshape(n_b, N_DEV, d // N_DEV, d),
        g_wo.reshape(n_b, N_DEV, d // N_DEV, d),
        g_wgu,
        g_wd.reshape(depth, N_DEV, ffn_w_down.shape[1], d),
    ]
    received = _exchange(big_grads + [small_partial.reshape(1, N_DEV, chunk_rows, FLAT_LANES)], "exchange_grads")
    small_sum = _ordered_sum(received[-1][0], "small_grad_sum")
    small_all = _all_gather([small_sum[None]], "gather_small_grads")[0]
    (ga_norm, ga_sgu, gw_sp, gb_sp, gkv_norm, gb_norm, g_relb, gffn_norm, gfinal) = _unpack(small_all, small_like)

    results = {}
    big_names = ["a_w_in", "a_w_out", "w_kv", "b_w_q", "b_w_o", "ffn_w_gate_up", "ffn_w_down"]
    big_wmv = [(a_w_in, m_a_w_in, v_a_w_in), (a_w_out, m_a_w_out, v_a_w_out),
               (w_kv[None], m_w_kv[None], v_w_kv[None]), (b_w_q, m_b_w_q, v_b_w_q), (b_w_o, m_b_w_o, v_b_w_o),
               (ffn_w_gate_up, m_ffn_w_gate_up, v_ffn_w_gate_up), (ffn_w_down, m_ffn_w_down, v_ffn_w_down)]
    for name, parts, (w, m, v) in zip(big_names, received[:-1], big_wmv):
        outs = _adamw(parts, w, m, v, f"adamw_{name}")
        if name == "w_kv":
            outs = [o[0] for o in outs]
        results[name] = outs

    n_cols = a_norm.shape[1]
    s_cols = a_sgu_norm.shape[1]
    small_g_list = [lax.dynamic_slice(ga_norm, (0, me * n_cols), (n_a, n_cols)),
                    lax.dynamic_slice(ga_sgu, (0, me * s_cols), (n_a, s_cols)),
                    gw_sp, gb_sp, gkv_norm, gb_norm, g_relb, gffn_norm, gfinal]
    small_names = ["a_norm", "a_sgu_norm", "a_w_spatial", "a_b_spatial", "kv_norm", "b_norm", "b_rel_bias",
                   "ffn_norm", "final_norm"]
    small_w = [a_norm, a_sgu_norm, a_w_spatial, a_b_spatial, kv_norm, b_norm, b_rel_bias, ffn_norm, final_norm]
    small_m = [m_a_norm, m_a_sgu_norm, m_a_w_spatial, m_a_b_spatial, m_kv_norm, m_b_norm, m_b_rel_bias,
               m_ffn_norm, m_final_norm]
    small_v = [v_a_norm, v_a_sgu_norm, v_a_w_spatial, v_a_b_spatial, v_kv_norm, v_b_norm, v_b_rel_bias,
               v_ffn_norm, v_final_norm]
    flat_g = _pack(small_g_list, 8)
    flat_out = _adamw(flat_g[None, None], _pack(small_w, 8)[None], _pack(small_m, 8)[None],
                      _pack(small_v, 8)[None], "adamw_small")
    unpacked = [_unpack(o[0], small_w) for o in flat_out]
    for idx, name in enumerate(small_names):
        results[name] = [unpacked[kind][idx] for kind in range(4)]

    order = ["a_norm", "a_w_in", "a_sgu_norm", "a_w_spatial", "a_b_spatial", "a_w_out", "kv_norm", "w_kv",
             "b_norm", "b_w_q", "b_rel_bias", "b_w_o", "ffn_norm", "ffn_w_gate_up", "ffn_w_down", "final_norm"]
    outputs = [loss, grad_x]
    for kind in range(4):
        outputs += [results[name][kind] for name in order]
    return tuple(outputs)
```

```python
import math

import jax
import jax.numpy as jnp
from jax import lax
from jax.experimental import pallas as pl
from jax.experimental.pallas import tpu as pltpu

F32 = jnp.float32
BF16 = jnp.bfloat16
MESH = pl.DeviceIdType.MESH
HBM_SPEC = pl.BlockSpec(memory_space=pltpu.HBM)
SEM_SPEC = pl.BlockSpec(memory_space=pltpu.SEMAPHORE)

N_DEV = 8
CHUNK = 64
A_CHUNK = 128
A_GROUPS = 8
N_LEFT_CHUNKS = 8
LEFT = N_LEFT_CHUNKS * CHUNK
PAIR_ROWS = 2 * CHUNK
PAIR_BAND = PAIR_ROWS + LEFT
Q_BLOCK = 2 * PAIR_ROWS
K_BLOCK = Q_BLOCK + LEFT
MAX_REL = 256
N_REL = 2 * MAX_REL + 1
REL_PAD = 640
HEAD_DIM = 64
HEAD_PAIR = 2 * HEAD_DIM
ATTN_SCALE = HEAD_DIM ** -0.5
EPS = 1e-6
NEG_INF = -1e30
ADAM_LR = 0.001
ADAM_B1 = 0.9
ADAM_B2 = 0.999
ADAM_EPS = 1e-08
ADAM_WD = 0.01
ADAM_STEP = 10
FLAT_LANES = 1024
V7X_VMEM_BYTES = 64 * 1024 * 1024
VMEM_FLOOR_BYTES = 32 * 1024 * 1024
VMEM_CEIL_BYTES = V7X_VMEM_BYTES - 8 * 1024 * 1024

NN = (((1,), (0,)), ((), ()))
NT = (((1,), (1,)), ((), ()))
TN = (((0,), (0,)), ((), ()))


def _tile(n, pref):
    return pref if n % pref == 0 else n


def _row_tile(n, pref, mult):
    best = None
    for t in range(mult, min(n, pref) + 1, mult):
        if n % t == 0:
            best = t
    return best if best is not None else n


def _nbytes(shape, dtype):
    n = 1
    for s in shape:
        if s is not None:
            n *= s
    return n * jnp.dtype(dtype).itemsize


def _call(body, name, grid, in_specs, out_specs, out_shape, scratch=(), vmem_bytes=0, aliases=None):
    limit = int(min(max(VMEM_FLOOR_BYTES, vmem_bytes * 5 // 4), VMEM_CEIL_BYTES))
    return pl.pallas_call(
        body,
        name=name,
        grid=grid,
        in_specs=in_specs,
        out_specs=out_specs,
        out_shape=out_shape,
        scratch_shapes=list(scratch),
        input_output_aliases=aliases or {},
        compiler_params=pltpu.CompilerParams(
            dimension_semantics=("arbitrary",) * len(grid), vmem_limit_bytes=limit),
    )


def _erf_parts(x):
    ax = jnp.abs(x) * (1.0 / math.sqrt(2.0))
    t = 1.0 / (1.0 + 0.3275911 * ax)
    poly = ((((1.061405429 * t - 1.453152027) * t + 1.421413741) * t - 0.284496736) * t + 0.254829592) * t
    ex = jnp.exp(-ax * ax)
    erf_abs = 1.0 - poly * ex
    return jnp.where(x < 0, -erf_abs, erf_abs), ex


def _gelu_and_grad(x):
    erf, ex = _erf_parts(x)
    cdf = 0.5 * (1.0 + erf)
    return x * cdf, cdf + x * ex * (1.0 / math.sqrt(2.0 * math.pi))


def _gelu(x):
    erf, _ = _erf_parts(x)
    return x * (0.5 * (1.0 + erf))


def _sigmoid(x):
    return 1.0 / (1.0 + jnp.exp(-x))


def _split3(x):
    hi = x.astype(BF16)
    r1 = x - hi.astype(F32)
    mid = r1.astype(BF16)
    lo = (r1 - mid.astype(F32)).astype(BF16)
    return hi, mid, lo


def _rms_fwd(x, g, name):
    t, d = x.shape
    tm = _tile(t, 512)

    def body(x_ref, g_ref, o_ref):
        xf = x_ref[...]
        r = lax.rsqrt(jnp.mean(xf * xf, axis=-1, keepdims=True) + EPS)
        o_ref[...] = (xf * r * g_ref[...]).astype(o_ref.dtype)

    return _call(
        body, name, (t // tm,),
        [pl.BlockSpec((tm, d), lambda i: (i, 0)), pl.BlockSpec((1, d), lambda i: (0, 0))],
        pl.BlockSpec((tm, d), lambda i: (i, 0)),
        jax.ShapeDtypeStruct((t, d), BF16),
        vmem_bytes=2 * (_nbytes((tm, d), F32) + _nbytes((tm, d), BF16)) + 4 * _nbytes((tm, d), F32),
    )(x, g.reshape(1, d))


def _rms_bwd(x, g, dh, dx_up, name):
    t, d = x.shape
    tm = _tile(t, 512)

    def body(x_ref, g_ref, dh_ref, up_ref, dx_ref, dg_ref):
        @pl.when(pl.program_id(0) == 0)
        def _():
            dg_ref[...] = jnp.zeros_like(dg_ref)

        xf = x_ref[...]
        r = lax.rsqrt(jnp.mean(xf * xf, axis=-1, keepdims=True) + EPS)
        xhat = xf * r
        dy = dh_ref[...].astype(F32)
        dxhat = dy * g_ref[...]
        dg_ref[...] += jnp.sum(dy * xhat, axis=0, keepdims=True)
        dx = r * (dxhat - xhat * jnp.mean(dxhat * xhat, axis=-1, keepdims=True))
        dx_ref[...] = up_ref[...] + dx

    row = pl.BlockSpec((tm, d), lambda i: (i, 0))
    vec = pl.BlockSpec((1, d), lambda i: (0, 0))
    dx, dg = _call(
        body, name, (t // tm,),
        [row, vec, row, row],
        [row, vec],
        [jax.ShapeDtypeStruct((t, d), F32), jax.ShapeDtypeStruct((1, d), F32)],
        vmem_bytes=10 * _nbytes((tm, d), F32),
    )(x, g.reshape(1, d), dh, dx_up)
    return dx, dg.reshape(d)


def _mm(name, dims, a, b, *, grid, a_spec, b_spec, out_shape, out_spec, acc_shape,
        res=None, res_spec=None, scale=None):
    nk = grid[2]
    has_res = res is not None

    def body(*refs):
        refs = list(refs)
        a_ref = refs.pop(0)
        b_ref = refs.pop(0)
        r_ref = refs.pop(0) if has_res else None
        o_ref = refs.pop(0)
        part = lax.dot_general(a_ref[...].astype(BF16), b_ref[...].astype(BF16), dims,
                               preferred_element_type=F32)

        def finish(acc):
            if scale is not None:
                acc = acc * scale
            if has_res:
                acc = acc + r_ref[...]
            o_ref[...] = acc.astype(o_ref.dtype)

        if nk == 1:
            finish(part)
        else:
            acc_ref = refs.pop(0)
            k = pl.program_id(2)

            @pl.when(k == 0)
            def _():
                acc_ref[...] = part

            @pl.when(k > 0)
            def _():
                acc_ref[...] += part

            @pl.when(k == nk - 1)
            def _():
                finish(acc_ref[...])

    operands = [a, b]
    in_specs = [a_spec, b_spec]
    vmem = 2 * (_nbytes(a_spec.block_shape, a.dtype) + _nbytes(b_spec.block_shape, b.dtype)
                + _nbytes(out_spec.block_shape, out_shape.dtype))
    vmem += 3 * _nbytes(acc_shape, F32)
    if has_res:
        operands.append(res)
        in_specs.append(res_spec)
        vmem += 2 * _nbytes(res_spec.block_shape, res.dtype)
    scratch = [pltpu.VMEM(acc_shape, F32)] if nk > 1 else []
    return _call(body, name, grid, in_specs, out_spec, out_shape, scratch=scratch, vmem_bytes=vmem)(*operands)


def _mm_colblock(name, h, w_g, layer):
    t, k = h.shape
    nb = w_g.shape[3]
    tm = _tile(t, 2048)
    return _mm(
        name, NN, h, w_g, grid=(t // tm, N_DEV, 1),
        a_spec=pl.BlockSpec((tm, k), lambda i, j, kk: (i, 0)),
        b_spec=pl.BlockSpec((None, None, k, nb), lambda i, j, kk: (layer, j, 0, 0)),
        out_shape=jax.ShapeDtypeStruct((t, N_DEV * nb), BF16),
        out_spec=pl.BlockSpec((tm, nb), lambda i, j, kk: (i, j)), acc_shape=(tm, nb))


def _mm_natural(name, a, w, layer, *, res=None, out_dtype=F32, scale=None):
    t, k = a.shape
    n = w.shape[2]
    tm = _tile(t, 1024)
    tn = _tile(n, 512)
    res_spec = None if res is None else pl.BlockSpec((tm, tn), lambda i, j, kk: (i, j))
    return _mm(
        name, NN, a, w, grid=(t // tm, n // tn, 1),
        a_spec=pl.BlockSpec((tm, k), lambda i, j, kk: (i, 0)),
        b_spec=pl.BlockSpec((None, k, tn), lambda i, j, kk: (layer, 0, j)),
        out_shape=jax.ShapeDtypeStruct((t, n), out_dtype),
        out_spec=pl.BlockSpec((tm, tn), lambda i, j, kk: (i, j)),
        acc_shape=(tm, tn), res=res, res_spec=res_spec, scale=scale)


def _mm_down(name, act, w4, layer, res):
    nblk, t, kb = act.shape
    n = w4.shape[3]
    tm = _tile(t, 1024)
    tn = _tile(n, 1024)
    return _mm(
        name, NN, act, w4, grid=(t // tm, n // tn, nblk),
        a_spec=pl.BlockSpec((None, tm, kb), lambda i, j, kk: (kk, i, 0)),
        b_spec=pl.BlockSpec((None, None, kb, tn), lambda i, j, kk: (layer, kk, 0, j)),
        out_shape=jax.ShapeDtypeStruct((t, n), F32),
        out_spec=pl.BlockSpec((tm, tn), lambda i, j, kk: (i, j)),
        acc_shape=(tm, tn), res=res, res_spec=pl.BlockSpec((tm, tn), lambda i, j, kk: (i, j)))


def _mm_t_colblock_norm_bwd(name, dz, w_g, layer, x, g, dx_up, blocked_in=False):
    k = w_g.shape[2]
    nb = w_g.shape[3]
    t = x.shape[0]
    tm = _tile(t, 1024)
    if blocked_in:
        a_spec = pl.BlockSpec((None, tm, nb), lambda i, kk: (kk, i, 0))
    else:
        a_spec = pl.BlockSpec((tm, nb), lambda i, kk: (i, kk))

    def body(a_ref, b_ref, x_ref, g_ref, up_ref, dx_ref, dg_ref, acc_ref):
        i = pl.program_id(0)
        kk = pl.program_id(1)
        part = lax.dot_general(a_ref[...].astype(BF16), b_ref[...].astype(BF16), NT, preferred_element_type=F32)

        @pl.when(kk == 0)
        def _():
            acc_ref[...] = part

        @pl.when(kk > 0)
        def _():
            acc_ref[...] += part

        @pl.when((i == 0) & (kk == 0))
        def _():
            dg_ref[...] = jnp.zeros_like(dg_ref)

        @pl.when(kk == N_DEV - 1)
        def _():
            dy = acc_ref[...]
            xf = x_ref[...]
            r = lax.rsqrt(jnp.mean(xf * xf, axis=-1, keepdims=True) + EPS)
            xhat = xf * r
            dxhat = dy * g_ref[...]
            dg_ref[...] += jnp.sum(dy * xhat, axis=0, keepdims=True)
            dx_ref[...] = up_ref[...] + r * (dxhat - xhat * jnp.mean(dxhat * xhat, axis=-1, keepdims=True))

    row = pl.BlockSpec((tm, k), lambda i, kk: (i, 0))
    vec = pl.BlockSpec((1, k), lambda i, kk: (0, 0))
    dx, dg = _call(
        body, name, (t // tm, N_DEV),
        [a_spec, pl.BlockSpec((None, None, k, nb), lambda i, kk: (layer, kk, 0, 0)), row, vec, row],
        [row, vec],
        [jax.ShapeDtypeStruct((t, k), F32), jax.ShapeDtypeStruct((1, k), F32)],
        scratch=[pltpu.VMEM((tm, k), F32)],
        vmem_bytes=2 * (_nbytes((tm, nb), BF16) + _nbytes((k, nb), BF16)) + 10 * _nbytes((tm, k), F32),
    )(dz, w_g, x, g.reshape(1, k), dx_up)
    return dx, dg.reshape(k)


def _ffn_gate_up(name, h, w_g, layer):
    t, k = h.shape
    nb = w_g.shape[3]
    half = N_DEV // 2
    tm = _tile(t, 1024)

    def body(h_ref, wg_ref, wu_ref, gu_ref, act_ref):
        hb = h_ref[...]
        gate = jnp.dot(hb, wg_ref[...], preferred_element_type=F32)
        up = jnp.dot(hb, wu_ref[...], preferred_element_type=F32)
        gu_ref[0] = gate.astype(BF16)
        gu_ref[1] = up.astype(BF16)
        act_ref[...] = (gate * _sigmoid(gate) * up).astype(BF16)

    return _call(
        body, name, (t // tm, half),
        [pl.BlockSpec((tm, k), lambda i, j: (i, 0)),
         pl.BlockSpec((None, None, k, nb), lambda i, j: (layer, j, 0, 0)),
         pl.BlockSpec((None, None, k, nb), lambda i, j: (layer, half + j, 0, 0))],
        [pl.BlockSpec((2, None, tm, nb), lambda i, j: (0, j, i, 0)),
         pl.BlockSpec((None, tm, nb), lambda i, j: (j, i, 0))],
        [jax.ShapeDtypeStruct((2, half, t, nb), BF16), jax.ShapeDtypeStruct((half, t, nb), BF16)],
        vmem_bytes=2 * (_nbytes((tm, k), BF16) + 2 * _nbytes((k, nb), BF16) + 3 * _nbytes((tm, nb), BF16))
        + 6 * _nbytes((tm, nb), F32),
    )(h, w_g, w_g)


def _ffn_down_dx(name, dy, w4, layer, gu4):
    t, n = dy.shape
    nblk, kb = w4.shape[1], w4.shape[2]
    tm = _tile(t, 1024)

    def body(dy_ref, w_ref, gu_ref, dgu_ref):
        da = lax.dot_general(dy_ref[...].astype(BF16), w_ref[...], NT, preferred_element_type=F32)
        gate = gu_ref[0].astype(F32)
        up = gu_ref[1].astype(F32)
        sig = _sigmoid(gate)
        dgu_ref[0] = (da * up * (sig * (1.0 + gate * (1.0 - sig)))).astype(BF16)
        dgu_ref[1] = (da * (gate * sig)).astype(BF16)

    blk = pl.BlockSpec((2, None, tm, kb), lambda i, j: (0, j, i, 0))
    return _call(
        body, name, (t // tm, nblk),
        [pl.BlockSpec((tm, n), lambda i, j: (i, 0)),
         pl.BlockSpec((None, None, kb, n), lambda i, j: (layer, j, 0, 0)),
         blk],
        blk,
        jax.ShapeDtypeStruct((2, nblk, t, kb), BF16),
        vmem_bytes=2 * (_nbytes((tm, n), F32) + _nbytes((kb, n), BF16) + 4 * _nbytes((tm, kb), BF16))
        + 8 * _nbytes((tm, kb), F32),
    )(dy, w4, gu4)


def _mm_t_natural(name, dy, w, layer):
    t, n = dy.shape
    k = w.shape[1]
    tm = _tile(t, 1024)
    tk = _tile(k, 512)
    return _mm(
        name, NT, dy, w, grid=(t // tm, k // tk, 1),
        a_spec=pl.BlockSpec((tm, n), lambda i, j, kk: (i, 0)),
        b_spec=pl.BlockSpec((None, tk, n), lambda i, j, kk: (layer, j, 0)),
        out_shape=jax.ShapeDtypeStruct((t, k), BF16),
        out_spec=pl.BlockSpec((tm, tk), lambda i, j, kk: (i, j)),
        acc_shape=(tm, tk))


def _mm_dw_colblock(name, h, dz, blocked_in=False):
    t, k = h.shape
    nb = dz.shape[2] if blocked_in else dz.shape[1] // N_DEV
    tk = _tile(t, 1024)
    if blocked_in:
        b_spec = pl.BlockSpec((None, tk, nb), lambda i, j, kk: (j, kk, 0))
    else:
        b_spec = pl.BlockSpec((tk, nb), lambda i, j, kk: (kk, j))
    return _mm(
        name, TN, h, dz, grid=(1, N_DEV, t // tk),
        a_spec=pl.BlockSpec((tk, k), lambda i, j, kk: (kk, 0)),
        b_spec=b_spec,
        out_shape=jax.ShapeDtypeStruct((N_DEV, k, nb), BF16),
        out_spec=pl.BlockSpec((None, k, nb), lambda i, j, kk: (j, 0, 0)),
        acc_shape=(k, nb))


def _mm_dw_natural(name, a, dy):
    t, k = a.shape
    n = dy.shape[1]
    tko = _tile(k, 1024)
    tt = _tile(t, 1024)
    out = _mm(
        name, TN, a, dy, grid=(k // tko, 1, t // tt),
        a_spec=pl.BlockSpec((tt, tko), lambda i, j, kk: (kk, i)),
        b_spec=pl.BlockSpec((tt, n), lambda i, j, kk: (kk, 0)),
        out_shape=jax.ShapeDtypeStruct((k, n), BF16),
        out_spec=pl.BlockSpec((tko, n), lambda i, j, kk: (i, 0)),
        acc_shape=(tko, n))
    return out.reshape(N_DEV, k // N_DEV, n)


def _mm_dw_down(name, act, dy):
    nblk, t, kb = act.shape
    n = dy.shape[1]
    tt = _tile(t, 1024)
    out = _mm(
        name, TN, act, dy, grid=(nblk, 1, t // tt),
        a_spec=pl.BlockSpec((None, tt, kb), lambda i, j, kk: (i, kk, 0)),
        b_spec=pl.BlockSpec((tt, n), lambda i, j, kk: (kk, 0)),
        out_shape=jax.ShapeDtypeStruct((nblk, kb, n), BF16),
        out_spec=pl.BlockSpec((None, kb, n), lambda i, j, kk: (i, 0, 0)),
        acc_shape=(kb, n))
    return out.reshape(N_DEV, (nblk * kb) // N_DEV, n)


def _spatial_mask(transposed=False):
    r = lax.broadcasted_iota(jnp.int32, (A_CHUNK, A_CHUNK), 0) // CHUNK
    c = lax.broadcasted_iota(jnp.int32, (A_CHUNK, A_CHUNK), 1) // CHUNK
    return c >= r if transposed else r >= c


def _sgu_tile(t):
    return _tile(t, 2 * A_CHUNK)


def _sgu_fwd(zpre, g_sgu, w_sp, b_full, name):
    t, f2 = zpre.shape
    f = f2 // 2
    gd = f // A_GROUPS
    tm = _sgu_tile(t)

    def body(z_ref, g_ref, w_ref, b_ref, p_ref):
        mask = _spatial_mask()
        wm = [jnp.where(mask, w_ref[g], 0.0).astype(BF16) for g in range(A_GROUPS)]
        for c in range(tm // A_CHUNK):
            rows = pl.ds(c * A_CHUNK, A_CHUNK)
            z = _gelu(z_ref[rows, :].astype(F32))
            u = z[:, :f]
            v0 = z[:, f:]
            r = lax.rsqrt(jnp.mean(v0 * v0, axis=-1, keepdims=True) + EPS)
            v1 = (v0 * r * g_ref[...]).astype(BF16)
            for g in range(A_GROUPS):
                cols = slice(g * gd, (g + 1) * gd)
                v2 = jnp.dot(wm[g], v1[:, cols], preferred_element_type=F32) + b_ref[:, cols]
                p_ref[rows, cols] = (u[:, cols] * v2).astype(BF16)

    return _call(
        body, name, (t // tm,),
        [pl.BlockSpec((tm, f2), lambda i: (i, 0)),
         pl.BlockSpec((1, f), lambda i: (0, 0)),
         pl.BlockSpec((A_GROUPS, A_CHUNK, A_CHUNK), lambda i: (0, 0, 0)),
         pl.BlockSpec((A_CHUNK, f), lambda i: (0, 0))],
        pl.BlockSpec((tm, f), lambda i: (i, 0)),
        jax.ShapeDtypeStruct((t, f), BF16),
        vmem_bytes=2 * _nbytes((tm, f2), BF16) + 2 * _nbytes((tm, f), BF16) + 8 * _nbytes((A_CHUNK, f2), F32),
    )(zpre, g_sgu.reshape(1, f), w_sp, b_full)


def _sgu_bwd(zpre, dp, g_sgu, w_sp, w_sp_t, b_full, name):
    t, f2 = zpre.shape
    f = f2 // 2
    gd = f // A_GROUPS
    tm = _sgu_tile(t)
    n_steps = t // tm

    def body(z_ref, dp_ref, g_ref, w_ref, wt_ref, b_ref, dz_ref, dw_ref, db_ref, dg_ref, dv1_ref, dbf_ref):
        step = pl.program_id(0)

        @pl.when(step == 0)
        def _():
            dw_ref[...] = jnp.zeros_like(dw_ref)
            dg_ref[...] = jnp.zeros_like(dg_ref)
            dbf_ref[...] = jnp.zeros_like(dbf_ref)

        mask = _spatial_mask()
        mask_t = _spatial_mask(transposed=True)
        wm = [jnp.where(mask, w_ref[g], 0.0).astype(BF16) for g in range(A_GROUPS)]
        wmt = [jnp.where(mask_t, wt_ref[g], 0.0).astype(BF16) for g in range(A_GROUPS)]
        gain = g_ref[...]
        for c in range(tm // A_CHUNK):
            rows = pl.ds(c * A_CHUNK, A_CHUNK)
            z, dgelu = _gelu_and_grad(z_ref[rows, :].astype(F32))
            u = z[:, :f]
            v0 = z[:, f:]
            r = lax.rsqrt(jnp.mean(v0 * v0, axis=-1, keepdims=True) + EPS)
            xhat = v0 * r
            v1 = (xhat * gain).astype(BF16)
            dpf = dp_ref[rows, :].astype(F32)
            for g in range(A_GROUPS):
                cols = slice(g * gd, (g + 1) * gd)
                v1g = v1[:, cols]
                v2 = jnp.dot(wm[g], v1g, preferred_element_type=F32) + b_ref[:, cols]
                dpg = dpf[:, cols]
                dz_ref[rows, cols] = (dpg * v2 * dgelu[:, cols]).astype(BF16)
                dv2 = dpg * u[:, cols]
                dbf_ref[:, cols] += dv2
                dv2b = dv2.astype(BF16)
                dwg = lax.dot_general(dv2b, v1g, NT, preferred_element_type=F32)
                dw_ref[g] += jnp.where(mask, dwg, 0.0)
                dv1_ref[:, cols] = jnp.dot(wmt[g], dv2b, preferred_element_type=F32)
            dv1 = dv1_ref[...]
            dxhat = dv1 * gain
            dg_ref[...] += jnp.sum(dv1 * xhat, axis=0, keepdims=True)
            dv0 = r * (dxhat - xhat * jnp.mean(dxhat * xhat, axis=-1, keepdims=True))
            dz_ref[rows, pl.ds(f, f)] = (dv0 * dgelu[:, f:]).astype(BF16)

        @pl.when(step == n_steps - 1)
        def _():
            for g in range(A_GROUPS):
                db_ref[g] = jnp.sum(dbf_ref[:, g * gd:(g + 1) * gd], axis=1, keepdims=True)

    wspec = pl.BlockSpec((A_GROUPS, A_CHUNK, A_CHUNK), lambda i: (0, 0, 0))
    dz, dw, db, dg = _call(
        body, name, (n_steps,),
        [pl.BlockSpec((tm, f2), lambda i: (i, 0)),
         pl.BlockSpec((tm, f), lambda i: (i, 0)),
         pl.BlockSpec((1, f), lambda i: (0, 0)),
         wspec, wspec,
         pl.BlockSpec((A_CHUNK, f), lambda i: (0, 0))],
        [pl.BlockSpec((tm, f2), lambda i: (i, 0)),
         wspec,
         pl.BlockSpec((A_GROUPS, A_CHUNK, 1), lambda i: (0, 0, 0)),
         pl.BlockSpec((1, f), lambda i: (0, 0))],
        [jax.ShapeDtypeStruct((t, f2), BF16),
         jax.ShapeDtypeStruct((A_GROUPS, A_CHUNK, A_CHUNK), F32),
         jax.ShapeDtypeStruct((A_GROUPS, A_CHUNK, 1), F32),
         jax.ShapeDtypeStruct((1, f), F32)],
        scratch=[pltpu.VMEM((A_CHUNK, f), F32), pltpu.VMEM((A_CHUNK, f), F32)],
        vmem_bytes=4 * _nbytes((tm, f2), BF16) + 2 * _nbytes((tm, f), BF16) + 12 * _nbytes((A_CHUNK, f2), F32),
    )(zpre, dp, g_sgu.reshape(1, f), w_sp, w_sp_t, b_full)
    return dz, dw, db.reshape(A_GROUPS, A_CHUNK), dg.reshape(f)


def _pair_valid(qi, col):
    qc = qi // CHUNK
    kc = col // CHUNK
    return (kc >= qc) & (kc <= qc + N_LEFT_CHUNKS)


def _diagonal_onehot():
    e = lax.broadcasted_iota(jnp.int32, (REL_PAD, K_BLOCK), 1)
    idx = jnp.clip(PAIR_BAND - 1 - e, -MAX_REL, MAX_REL) + MAX_REL
    r = lax.broadcasted_iota(jnp.int32, (REL_PAD, K_BLOCK), 0)
    return jnp.where(r == idx, 1.0, 0.0).astype(BF16)


def _bias_build(table, name):
    h = table.shape[0]
    tab = jnp.pad(table, ((0, 0), (0, REL_PAD - N_REL)))

    def body(t_ref, o_ref):
        oh = _diagonal_onehot()
        diag = jnp.zeros((h, K_BLOCK), F32)
        for piece in _split3(t_ref[...]):
            diag += jnp.dot(piece, oh, preferred_element_type=F32)
        col = lax.broadcasted_iota(jnp.int32, (h, PAIR_BAND), 1)
        for qi in range(PAIR_ROWS):
            row = pltpu.roll(diag, (qi - (PAIR_ROWS - 1)) % K_BLOCK, 1)[:, :PAIR_BAND]
            o_ref[qi] = jnp.where(_pair_valid(qi, col), row, NEG_INF)

    out = _call(
        body, name, (1,),
        [pl.BlockSpec((h, REL_PAD), lambda i: (0, 0))],
        pl.BlockSpec((PAIR_ROWS, h, PAIR_BAND), lambda i: (0, 0, 0)),
        jax.ShapeDtypeStruct((PAIR_ROWS, h, PAIR_BAND), F32),
        vmem_bytes=4 * _nbytes((PAIR_ROWS, h, PAIR_BAND), F32),
    )(tab)
    return jnp.transpose(out, (1, 0, 2))


def _bias_block(pair_bias):
    rest = K_BLOCK - PAIR_BAND
    top = jnp.pad(pair_bias, ((0, 0), (0, 0), (0, rest)), constant_values=NEG_INF)
    bottom = jnp.pad(pair_bias, ((0, 0), (0, 0), (rest, 0)), constant_values=NEG_INF)
    return jnp.concatenate([top, bottom], axis=1)


def _bias_grad(dbias, name):
    h = dbias.shape[0]
    db_t = jnp.transpose(dbias, (1, 0, 2))

    def body(d_ref, o_ref):
        diag = jnp.zeros((h, K_BLOCK), F32)
        for qi in range(PAIR_ROWS):
            diag += pltpu.roll(d_ref[qi], PAIR_ROWS - 1 - qi, 1)
        oh = _diagonal_onehot()
        acc = jnp.zeros((h, REL_PAD), F32)
        for piece in _split3(diag):
            acc += lax.dot_general(piece, oh, NT, preferred_element_type=F32)
        o_ref[...] = acc

    out = _call(
        body, name, (1,),
        [pl.BlockSpec((PAIR_ROWS, h, K_BLOCK), lambda i: (0, 0, 0))],
        pl.BlockSpec((h, REL_PAD), lambda i: (0, 0)),
        jax.ShapeDtypeStruct((h, REL_PAD), F32),
        vmem_bytes=4 * _nbytes((PAIR_ROWS, h, K_BLOCK), F32),
    )(db_t)
    return out[:, :N_REL]


def _head_masks():
    lane = lax.broadcasted_iota(jnp.int32, (Q_BLOCK, HEAD_PAIR), 1)
    return lane < HEAD_DIM, lane >= HEAD_DIM


def _block_probs(qm, kb, bias, valid):
    s = lax.dot_general(qm, kb, NT, preferred_element_type=F32) + bias
    s = jnp.where(valid, s, NEG_INF)
    e = jnp.exp(s - jnp.max(s, axis=-1, keepdims=True))
    return e * (1.0 / jnp.sum(e, axis=-1, keepdims=True))


def _attn_fwd(q, kvpad, bias, name):
    t, d = q.shape
    n_pairs = d // HEAD_PAIR
    n_blocks = t // Q_BLOCK

    def body(q_ref, k_ref, v_ref, b_ref, o_ref):
        masks = _head_masks()
        key = lax.broadcasted_iota(jnp.int32, (Q_BLOCK, K_BLOCK), 1)

        def step(j, carry):
            r0 = pl.multiple_of(j * Q_BLOCK, Q_BLOCK)
            q2 = q_ref[pl.ds(r0, Q_BLOCK), :].astype(F32)
            kb = k_ref[pl.ds(r0, K_BLOCK), :]
            vb = v_ref[pl.ds(r0, K_BLOCK), :]
            valid = key >= LEFT - j * Q_BLOCK
            outs = []
            for a in range(2):
                qm = jnp.where(masks[a], q2, 0.0).astype(BF16)
                p = _block_probs(qm, kb, b_ref[a], valid)
                outs.append(jnp.dot(p.astype(BF16), vb, preferred_element_type=F32))
            o_ref[pl.ds(r0, Q_BLOCK), :] = jnp.where(masks[0], outs[0], outs[1]).astype(BF16)
            return carry

        lax.fori_loop(0, n_blocks, step, 0)

    return _call(
        body, name, (n_pairs,),
        [pl.BlockSpec((t, HEAD_PAIR), lambda p: (0, p)),
         pl.BlockSpec((LEFT + t, HEAD_PAIR), lambda p: (0, p)),
         pl.BlockSpec((LEFT + t, HEAD_PAIR), lambda p: (0, n_pairs + p)),
         pl.BlockSpec((2, Q_BLOCK, K_BLOCK), lambda p: (p, 0, 0))],
        pl.BlockSpec((t, HEAD_PAIR), lambda p: (0, p)),
        jax.ShapeDtypeStruct((t, d), BF16),
        vmem_bytes=8 * _nbytes((LEFT + t, HEAD_PAIR), BF16) + 12 * _nbytes((2, Q_BLOCK, K_BLOCK), F32),
    )(q, kvpad, kvpad, bias)


def _attn_bwd(q, kvpad, bias, do, dk_in, dv_in, name):
    t, d = q.shape
    n_pairs = d // HEAD_PAIR
    n_blocks = t // Q_BLOCK
    has_in = dk_in is not None

    def body(*refs):
        refs = list(refs)
        q_ref, k_ref, v_ref, b_ref, do_ref = refs[:5]
        refs = refs[5:]
        if has_in:
            dki_ref, dvi_ref = refs[:2]
            refs = refs[2:]
        dq_ref, dk_ref, dv_ref, db_ref = refs
        masks = _head_masks()
        key = lax.broadcasted_iota(jnp.int32, (Q_BLOCK, K_BLOCK), 1)
        if has_in:
            dk_ref[...] = dki_ref[...]
            dv_ref[...] = dvi_ref[...]
        else:
            dk_ref[...] = jnp.zeros_like(dk_ref)
            dv_ref[...] = jnp.zeros_like(dv_ref)
        db_ref[...] = jnp.zeros_like(db_ref)

        def step(j, carry):
            r0 = pl.multiple_of(j * Q_BLOCK, Q_BLOCK)
            q2 = q_ref[pl.ds(r0, Q_BLOCK), :].astype(F32)
            do2 = do_ref[pl.ds(r0, Q_BLOCK), :].astype(F32)
            kb = k_ref[pl.ds(r0, K_BLOCK), :]
            vb = v_ref[pl.ds(r0, K_BLOCK), :]
            valid = key >= LEFT - j * Q_BLOCK
            dqs = []
            dk_acc = jnp.zeros((K_BLOCK, HEAD_PAIR), F32)
            dv_acc = jnp.zeros((K_BLOCK, HEAD_PAIR), F32)
            for a in range(2):
                qm = jnp.where(masks[a], q2, 0.0).astype(BF16)
                dom = jnp.where(masks[a], do2, 0.0).astype(BF16)
                p = _block_probs(qm, kb, b_ref[a], valid)
                dp = lax.dot_general(dom, vb, NT, preferred_element_type=F32)
                ds = p * (dp - jnp.sum(dp * p, axis=-1, keepdims=True))
                db_ref[a] += ds[:PAIR_ROWS, :]
                db_ref[a, :, pl.ds(0, PAIR_BAND)] += ds[PAIR_ROWS:, K_BLOCK - PAIR_BAND:]
                dsb = ds.astype(BF16)
                dqs.append(jnp.dot(dsb, kb, preferred_element_type=F32))
                dk_acc += lax.dot_general(dsb, qm, TN, preferred_element_type=F32)
                dv_acc += lax.dot_general(p.astype(BF16), dom, TN, preferred_element_type=F32)
            dq = jnp.where(masks[0], dqs[0], dqs[1]) * ATTN_SCALE
            dq_ref[pl.ds(r0, Q_BLOCK), :] = dq.astype(BF16)
            dk_ref[pl.ds(r0, K_BLOCK), :] += dk_acc
            dv_ref[pl.ds(r0, K_BLOCK), :] += dv_acc
            return carry

        lax.fori_loop(0, n_blocks, step, 0)

    q_spec = pl.BlockSpec((t, HEAD_PAIR), lambda p: (0, p))
    kv_spec = pl.BlockSpec((LEFT + t, HEAD_PAIR), lambda p: (0, p))
    operands = [q, kvpad, kvpad, bias, do]
    in_specs = [q_spec, kv_spec, pl.BlockSpec((LEFT + t, HEAD_PAIR), lambda p: (0, n_pairs + p)),
                pl.BlockSpec((2, Q_BLOCK, K_BLOCK), lambda p: (p, 0, 0)), q_spec]
    aliases = None
    if has_in:
        operands += [dk_in, dv_in]
        in_specs += [kv_spec, kv_spec]
        aliases = {5: 1, 6: 2}
    return _call(
        body, name, (n_pairs,),
        in_specs,
        [q_spec, kv_spec, kv_spec, pl.BlockSpec((2, PAIR_ROWS, K_BLOCK), lambda p: (p, 0, 0))],
        [jax.ShapeDtypeStruct((t, d), BF16),
         jax.ShapeDtypeStruct((LEFT + t, d), F32),
         jax.ShapeDtypeStruct((LEFT + t, d), F32),
         jax.ShapeDtypeStruct((d // HEAD_DIM, PAIR_ROWS, K_BLOCK), F32)],
        vmem_bytes=10 * _nbytes((LEFT + t, HEAD_PAIR), BF16) + 8 * _nbytes((LEFT + t, HEAD_PAIR), F32)
        + 16 * _nbytes((2, Q_BLOCK, K_BLOCK), F32),
        aliases=aliases,
    )(*operands)


def _loss_head(x, g, target, name):
    t, d = x.shape
    tm = _tile(t, 512)

    def body(x_ref, g_ref, t_ref, dx_ref, loss_ref, dg_ref):
        @pl.when(pl.program_id(0) == 0)
        def _():
            loss_ref[...] = jnp.zeros_like(loss_ref)
            dg_ref[...] = jnp.zeros_like(dg_ref)

        xf = x_ref[...]
        r = lax.rsqrt(jnp.mean(xf * xf, axis=-1, keepdims=True) + EPS)
        xhat = xf * r
        diff = xhat * g_ref[...] - t_ref[...]
        row_loss = jnp.mean(diff * diff, axis=-1, keepdims=True)
        loss_ref[...] += 0.5 * jnp.sum(row_loss, axis=0, keepdims=True)
        dy = diff * (1.0 / d)
        dg_ref[...] += jnp.sum(dy * xhat, axis=0, keepdims=True)
        dxhat = dy * g_ref[...]
        dx_ref[...] = r * (dxhat - xhat * jnp.mean(dxhat * xhat, axis=-1, keepdims=True))

    row = pl.BlockSpec((tm, d), lambda i: (i, 0))
    vec = pl.BlockSpec((1, d), lambda i: (0, 0))
    dx, loss, dg = _call(
        body, name, (t // tm,),
        [row, vec, row],
        [row, pl.BlockSpec((1, 1), lambda i: (0, 0)), vec],
        [jax.ShapeDtypeStruct((t, d), F32), jax.ShapeDtypeStruct((1, 1), F32), jax.ShapeDtypeStruct((1, d), F32)],
        vmem_bytes=10 * _nbytes((tm, d), F32),
    )(x, g.reshape(1, d), target)
    return dx, loss[0, 0], dg.reshape(d)


def _adamw_store(g, w_ref, m_ref, v_ref, g_ref, d_ref, nm_ref, nv_ref):
    c1 = 1.0 / (1.0 - ADAM_B1 ** ADAM_STEP)
    c2 = 1.0 / (1.0 - ADAM_B2 ** ADAM_STEP)
    nm = ADAM_B1 * m_ref[...] + (1.0 - ADAM_B1) * g
    nv = ADAM_B2 * v_ref[...] + (1.0 - ADAM_B2) * (g * g)
    g_ref[...] = g
    nm_ref[...] = nm
    nv_ref[...] = nv
    d_ref[...] = -ADAM_LR * ((nm * c1) / (jnp.sqrt(nv * c2) + ADAM_EPS) + ADAM_WD * w_ref[...])


def _adamw_layer(recv, own, w, m, v, layer, prev, me, name):
    n_src, r, c = recv.shape
    tr = _row_tile(r, max(16, (256 * 1024) // c), 16)
    first = prev is None

    def body(me_ref, recv_ref, own_ref, w_ref, m_ref, v_ref, *rest):
        mine = me_ref[0]
        own_part = own_ref[...].astype(F32)
        g = None
        for s in range(n_src):
            part = jnp.where(mine == s, own_part, recv_ref[s].astype(F32))
            g = part if g is None else g + part
        _adamw_store(g, w_ref, m_ref, v_ref, *rest[-4:])

    blk = pl.BlockSpec((None, tr, c), lambda i, me_ref: (layer, i, 0))
    any_spec = pl.BlockSpec(memory_space=pl.ANY)
    out = jax.ShapeDtypeStruct(w.shape, F32)
    operands = [me, recv, own, w, m, v] + ([] if first else list(prev))
    vmem = 2 * _nbytes((n_src + 1, tr, c), BF16) + 18 * _nbytes((tr, c), F32)
    return pl.pallas_call(
        body,
        name=name,
        grid_spec=pltpu.PrefetchScalarGridSpec(
            num_scalar_prefetch=1,
            grid=(r // tr,),
            in_specs=[pl.BlockSpec((n_src, tr, c), lambda i, me_ref: (0, i, 0)),
                      pl.BlockSpec((None, tr, c), lambda i, me_ref: (me_ref[0], i, 0)),
                      blk, blk, blk] + ([] if first else [any_spec] * 4),
            out_specs=[blk, blk, blk, blk],
        ),
        out_shape=[out, out, out, out],
        input_output_aliases={} if first else {6 + j: j for j in range(4)},
        compiler_params=pltpu.CompilerParams(
            dimension_semantics=("arbitrary",),
            vmem_limit_bytes=int(min(max(VMEM_FLOOR_BYTES, vmem * 5 // 4), VMEM_CEIL_BYTES))),
    )(*operands)


def _adamw(parts, w, m, v, name):
    n_layers, n_src, r, c = parts.shape
    mult = 16 if parts.dtype == BF16 else 8
    tr = _row_tile(r, max(mult, (256 * 1024) // c), mult)

    def body(p_ref, w_ref, m_ref, v_ref, g_ref, d_ref, nm_ref, nv_ref):
        g = p_ref[0].astype(F32)
        for s in range(1, n_src):
            g = g + p_ref[s].astype(F32)
        _adamw_store(g, w_ref, m_ref, v_ref, g_ref, d_ref, nm_ref, nv_ref)

    blk = pl.BlockSpec((None, tr, c), lambda l, i: (l, i, 0))
    out = jax.ShapeDtypeStruct((n_layers, r, c), F32)
    return _call(
        body, name, (n_layers, r // tr),
        [pl.BlockSpec((None, n_src, tr, c), lambda l, i: (l, 0, i, 0)), blk, blk, blk],
        [blk, blk, blk, blk],
        [out, out, out, out],
        vmem_bytes=2 * _nbytes((n_src, tr, c), parts.dtype) + 18 * _nbytes((tr, c), F32),
    )(parts, w, m, v)


def _ordered_sum(parts, name):
    n_src, r, c = parts.shape

    def body(p_ref, o_ref):
        acc = p_ref[0]
        for s in range(1, n_src):
            acc = acc + p_ref[s]
        o_ref[...] = acc

    return _call(
        body, name, (1,),
        [pl.BlockSpec((n_src, r, c), lambda i: (0, 0, 0))],
        pl.BlockSpec((r, c), lambda i: (0, 0)),
        jax.ShapeDtypeStruct((r, c), F32),
        vmem_bytes=4 * _nbytes((n_src, r, c), F32),
    )(parts)


def _position():
    return lax.axis_index("x"), lax.axis_index("y"), lax.axis_index("c")


def _linear(p):
    return 4 * p[0] + 2 * p[1] + p[2]


def _all_gather(shards, name):
    n = len(shards)

    def body(*refs):
        ins, outs = refs[:n], refs[n:2 * n]
        send_sems, recv_sems, local_sems = refs[2 * n:]
        x, y, c = _position()
        me, sibling = (x, y, c), (x, y, 1 - c)
        chips = [(1 - x, y), (x, 1 - y), (1 - x, 1 - y)]

        def slab(t, p):
            return outs[t].at[:, _linear(p)]

        def copy(t, k, block, to, src=None):
            return pltpu.make_async_remote_copy(
                src_ref=slab(t, block) if src is None else src,
                dst_ref=slab(t, block),
                send_sem=send_sems.at[t, k],
                recv_sem=recv_sems.at[t, k],
                device_id=to,
                device_id_type=MESH,
            )

        started = []
        for t in range(n):
            mine = pltpu.make_async_copy(ins[t], slab(t, me), local_sems.at[t])
            mine.start()
            started.append(mine)
        sends = []
        for t in range(n):
            first = [copy(t, 0, me, sibling, src=ins[t])]
            first += [copy(t, 1 + j, me, (*chip, c), src=ins[t]) for j, chip in enumerate(chips)]
            for cp in first:
                cp.start()
            sends += first
        for t in range(n):
            for j, chip in enumerate(chips):
                copy(t, 1 + j, (*chip, c), me).wait_recv()
                passed = copy(t, 4 + j, (*chip, c), sibling)
                passed.start()
                sends.append(passed)
        for t in range(n):
            copy(t, 0, sibling, me).wait_recv()
            for j, chip in enumerate(chips):
                copy(t, 4 + j, (*chip, 1 - c), me).wait_recv()
        for cp in sends:
            cp.wait_send()
        for mine in started:
            mine.wait()

    out_shape = [jax.ShapeDtypeStruct((s.shape[0], N_DEV) + s.shape[1:], s.dtype) for s in shards]
    return pl.pallas_call(
        body,
        name=name,
        in_specs=[HBM_SPEC] * n,
        out_specs=[HBM_SPEC] * n,
        out_shape=out_shape,
        scratch_shapes=[
            pltpu.SemaphoreType.DMA((n, N_DEV - 1)),
            pltpu.SemaphoreType.DMA((n, N_DEV - 1)),
            pltpu.SemaphoreType.DMA((n,)),
        ],
    )(*shards)


def _exchange(blocks, name):
    n = len(blocks)

    def body(*refs):
        ins, outs = refs[:n], refs[n:2 * n]
        send_sems, recv_sems, local_sems = refs[2 * n:]
        x, y, c = _position()
        me = _linear((x, y, c))
        flips = [(fx, fy, fc) for fx in (0, 1) for fy in (0, 1) for fc in (0, 1)][1:]

        def peer_of(flip):
            fx, fy, fc = flip
            return (1 - x if fx else x, 1 - y if fy else y, 1 - c if fc else c)

        def copy(t, k, peer):
            return pltpu.make_async_remote_copy(
                src_ref=ins[t].at[:, _linear(peer)],
                dst_ref=outs[t].at[:, me],
                send_sem=send_sems.at[t, k],
                recv_sem=recv_sems.at[t, k],
                device_id=peer,
                device_id_type=MESH,
            )

        def arrival(t, k, peer):
            return pltpu.make_async_remote_copy(
                src_ref=ins[t].at[:, _linear(peer)],
                dst_ref=outs[t].at[:, _linear(peer)],
                send_sem=send_sems.at[t, k],
                recv_sem=recv_sems.at[t, k],
                device_id=peer,
                device_id_type=MESH,
            )

        own = []
        for t in range(n):
            cp = pltpu.make_async_copy(ins[t].at[:, me], outs[t].at[:, me], local_sems.at[t])
            cp.start()
            own.append(cp)
        sends = []
        for t in range(n):
            for k, flip in enumerate(flips):
                cp = copy(t, k, peer_of(flip))
                cp.start()
                sends.append(cp)
        for t in range(n):
            for k, flip in enumerate(flips):
                arrival(t, k, peer_of(flip)).wait_recv()
        for cp in sends:
            cp.wait_send()
        for cp in own:
            cp.wait()

    out_shape = [jax.ShapeDtypeStruct(b.shape, b.dtype) for b in blocks]
    return pl.pallas_call(
        body,
        name=name,
        in_specs=[HBM_SPEC] * n,
        out_specs=[HBM_SPEC] * n,
        out_shape=out_shape,
        scratch_shapes=[
            pltpu.SemaphoreType.DMA((n, N_DEV - 1)),
            pltpu.SemaphoreType.DMA((n, N_DEV - 1)),
            pltpu.SemaphoreType.DMA((n,)),
        ],
    )(*blocks)


def _peers():
    x, y, c = _position()
    flips = [(fx, fy, fc) for fx in (0, 1) for fy in (0, 1) for fc in (0, 1)][1:]
    return [(1 - x if fx else x, 1 - y if fy else y, 1 - c if fc else c) for fx, fy, fc in flips]


def _split_start(srcs, lands, carry, name, exchange=False):
    n = len(srcs)

    def body(*refs):
        src_refs, land_refs = refs[:n], refs[n:2 * n]
        send_sems, recv_sems = refs[2 * n + 1], refs[2 * n + 2]
        me = _linear(_position())
        for t in range(n):
            for k, peer in enumerate(_peers()):
                pltpu.make_async_remote_copy(
                    src_ref=src_refs[t].at[_linear(peer)] if exchange else src_refs[t],
                    dst_ref=land_refs[t].at[me],
                    send_sem=send_sems.at[t * (N_DEV - 1) + k],
                    recv_sem=recv_sems.at[t * (N_DEV - 1) + k],
                    device_id=peer,
                    device_id_type=MESH,
                ).start()

    operands = list(srcs) + list(lands) + [carry]
    sems = pltpu.SemaphoreType.DMA((n * (N_DEV - 1),))
    out = pl.pallas_call(
        body,
        name=name,
        in_specs=[HBM_SPEC] * len(operands),
        out_specs=[SEM_SPEC, SEM_SPEC] + [HBM_SPEC] * len(operands),
        out_shape=[sems, sems] + [pltpu.HBM(a.shape, a.dtype) for a in operands],
        input_output_aliases={i: 2 + i for i in range(len(operands))},
        compiler_params=pltpu.CompilerParams(has_side_effects=pltpu.SideEffectType.DATAFLOW_SIDE_EFFECTING),
    )(*[pltpu.with_memory_space_constraint(a, pltpu.HBM) for a in operands])
    return out[0], out[1], out[2:2 + n], out[2 + n:2 + 2 * n], out[2 + 2 * n]


def _split_wait(send_sems, recv_sems, srcs, lands, after, name, exchange=False):
    n = len(srcs)

    def body(*refs):
        src_refs, land_refs = refs[:n], refs[n:2 * n]
        send_ref, recv_ref = refs[2 * n], refs[2 * n + 1]
        for t in range(n):
            for k, peer in enumerate(_peers()):
                copy = pltpu.make_async_remote_copy(
                    src_ref=src_refs[t].at[0] if exchange else src_refs[t],
                    dst_ref=land_refs[t].at[0],
                    send_sem=send_ref.at[t * (N_DEV - 1) + k],
                    recv_sem=recv_ref.at[t * (N_DEV - 1) + k],
                    device_id=peer,
                    device_id_type=MESH,
                )
                copy.wait_send()
                copy.wait_recv()

    arrays = list(srcs) + list(lands)
    out = pl.pallas_call(
        body,
        name=name,
        in_specs=[HBM_SPEC] * len(arrays) + [SEM_SPEC, SEM_SPEC, pl.BlockSpec(memory_space=pl.ANY)],
        out_specs=[HBM_SPEC] * len(arrays),
        out_shape=[pltpu.HBM(a.shape, a.dtype) for a in arrays],
        input_output_aliases={i: i for i in range(len(arrays))},
        compiler_params=pltpu.CompilerParams(has_side_effects=pltpu.SideEffectType.DATAFLOW_SIDE_EFFECTING),
    )(*arrays, send_sems, recv_sems, after)
    return out[:n], out[n:]


def _pack(arrays, row_multiple):
    flat = jnp.concatenate([a.reshape(-1) for a in arrays])
    quantum = row_multiple * FLAT_LANES
    padded = -(-flat.shape[0] // quantum) * quantum
    return jnp.pad(flat, (0, padded - flat.shape[0])).reshape(-1, FLAT_LANES)


def _unpack(flat, like):
    flat = flat.reshape(-1)
    out, at = [], 0
    for a in like:
        size = math.prod(a.shape)
        out.append(flat[at:at + size].reshape(a.shape))
        at += size
    return out


def kernel(x, a_norm, a_w_in, a_sgu_norm, a_w_spatial, a_b_spatial, a_w_out, kv_norm, w_kv, b_norm, b_w_q, b_rel_bias, b_w_o, ffn_norm, ffn_w_gate_up, ffn_w_down, final_norm, loss_target, m_a_norm, m_a_w_in, m_a_sgu_norm, m_a_w_spatial, m_a_b_spatial, m_a_w_out, m_kv_norm, m_w_kv, m_b_norm, m_b_w_q, m_b_rel_bias, m_b_w_o, m_ffn_norm, m_ffn_w_gate_up, m_ffn_w_down, m_final_norm, v_a_norm, v_a_w_in, v_a_sgu_norm, v_a_w_spatial, v_a_b_spatial, v_a_w_out, v_kv_norm, v_w_kv, v_b_norm, v_b_w_q, v_b_rel_bias, v_b_w_o, v_ffn_norm, v_ffn_w_gate_up, v_ffn_w_down, v_final_norm):
    xs = x[0]
    target = loss_target[0]
    t, d = xs.shape
    n_a = a_w_in.shape[0]
    n_b = b_w_q.shape[0]
    depth = ffn_w_gate_up.shape[0]
    f_a = a_w_out.shape[1] * N_DEV
    gd = f_a // A_GROUPS
    nb_ffn = ffn_w_gate_up.shape[2]
    me = _linear(_position())

    small_rows = -(-(a_norm.size + a_sgu_norm.size) // (8 * 128)) * 8
    small = jnp.pad(jnp.concatenate([a_norm.reshape(-1), a_sgu_norm.reshape(-1)]),
                    (0, small_rows * 128 - a_norm.size - a_sgu_norm.size)).reshape(1, small_rows, 128)

    def shard(w, layer=None):
        return (w if layer is None else w[layer]).astype(BF16)

    stages = []
    for layer in range(depth):
        if layer < n_a:
            stages.append((f"a{layer}", [shard(a_w_in, layer), shard(a_w_out, layer)]))
        else:
            i = layer - n_a
            shared = [shard(w_kv)] if i == 0 else []
            stages.append((f"b{i}", shared + [shard(b_w_q, i), shard(b_w_o, i)]))
        stages.append((f"f{layer}", [shard(ffn_w_gate_up, layer), shard(ffn_w_down, layer)]))
    first = _all_gather([s[None] for s in stages[0][1]] + [small], "gather_first")
    gathered = {stages[0][0]: [g[0] for g in first[:-1]]}
    small_g = first[-1].reshape(N_DEV, -1)
    a_norm_full = small_g[:, :a_norm.size].reshape(N_DEV, n_a, -1).transpose(1, 0, 2).reshape(n_a, d)
    a_sgu_full = small_g[:, a_norm.size:a_norm.size + a_sgu_norm.size].reshape(
        N_DEV, n_a, -1).transpose(1, 0, 2).reshape(n_a, f_a)
    in_flight = {}
    for key, shards in stages[1:]:
        lands = [lax.dynamic_update_slice(lax.empty((N_DEV,) + s.shape, BF16), s[None], (me, 0, 0)) for s in shards]
        send, recv, srcs, lands, a_norm_full = _split_start(shards, lands, a_norm_full, f"gather_start_{key}")
        in_flight[key] = (send, recv, srcs, lands)

    def weights(key, after):
        if key not in gathered:
            _, gathered[key] = _split_wait(*in_flight.pop(key), after, f"gather_wait_{key}")
        return gathered[key]

    rows_down = ffn_w_down.shape[1]

    def mixer_a_weights(i, after):
        w_in, w_out = weights(f"a{i}", after)
        return w_in[None], w_out.reshape(1, f_a, d)

    def mixer_b_weights(i, after):
        ws = weights(f"b{i}", after)
        return ws[-2].reshape(1, d, d), ws[-1].reshape(1, d, d)

    def ffn_weights(layer, after):
        w_gu, w_dn = weights(f"f{layer}", after)
        return w_gu[None], w_dn.reshape(1, N_DEV // 2, 2 * rows_down, d)

    w_sp_t = jnp.swapaxes(a_w_spatial, -1, -2)
    b_full = jnp.repeat(jnp.swapaxes(a_b_spatial, -1, -2), gd, axis=-1)

    saved = []

    def ffn_fwd(xin, layer):
        hf = _rms_fwd(xin, ffn_norm[layer], f"ffn_norm_fwd_{layer}")
        w_gu, w_dn = ffn_weights(layer, xin)
        gu, act = _ffn_gate_up(f"ffn_gate_up_{layer}", hf, w_gu, 0)
        xout = _mm_down(f"ffn_down_{layer}", act, w_dn, 0, xin)
        return xout, (xin, hf, gu, act)

    for i in range(n_a):
        w_in, w_out = mixer_a_weights(i, xs)
        h = _rms_fwd(xs, a_norm_full[i], f"a_norm_fwd_{i}")
        zpre = _mm_colblock(f"a_in_{i}", h, w_in, 0)
        p = _sgu_fwd(zpre, a_sgu_full[i], a_w_spatial[i], b_full[i], f"a_sgu_fwd_{i}")
        x_mid = _mm_natural(f"a_out_{i}", p, w_out, 0, res=xs)
        x_out, ffn_saved = ffn_fwd(x_mid, i)
        saved.append((xs, h, zpre, p, ffn_saved))
        xs = x_out

    x_kv = xs
    w_kv_g = weights("b0", x_kv)[0][None]
    h_kv = _rms_fwd(x_kv, kv_norm, "kv_norm_fwd")
    kv = _mm_colblock("kv_proj", h_kv, w_kv_g, 0)
    kvpad = jnp.pad(kv, ((LEFT, 0), (0, 0)))

    biases = [_bias_block(_bias_build(b_rel_bias[i], f"rel_bias_{i}")) for i in range(n_b)]
    for i in range(n_b):
        layer = n_a + i
        w_q, w_o = mixer_b_weights(i, xs)
        hb = _rms_fwd(xs, b_norm[i], f"b_norm_fwd_{i}")
        q = _mm_natural(f"b_q_{i}", hb, w_q, 0, out_dtype=BF16, scale=ATTN_SCALE)
        o = _attn_fwd(q, kvpad, biases[i], f"b_attn_fwd_{i}")
        x_mid = _mm_natural(f"b_o_{i}", o, w_o, 0, res=xs)
        x_out, ffn_saved = ffn_fwd(x_mid, layer)
        saved.append((xs, hb, q, o, ffn_saved))
        xs = x_out

    dx, loss_local, g_final = _loss_head(xs, final_norm, target, "loss_head")
    loss = lax.psum(loss_local, ("x", "y", "c"))

    big_grads = {}
    pending = []
    in_flight_grads = []

    def start_exchange(dx, tag):
        srcs = [big_grads[key] for key in pending]
        lands = [lax.empty(s.shape, BF16) for s in srcs]
        send, recv, srcs, lands, dx = _split_start(srcs, lands, dx, f"exchange_start_{tag}", exchange=True)
        in_flight_grads.append((list(pending), send, recv, srcs, lands, tag))
        pending.clear()
        return dx

    g_ffn_norm = [None] * depth
    g_a_norm = [None] * n_a
    g_a_sgu = [None] * n_a
    g_w_sp = [None] * n_a
    g_b_sp = [None] * n_a
    g_b_norm = [None] * n_b
    g_rel = [None] * n_b

    def ffn_bwd(dx, layer, ffn_saved):
        xin, hf, gu, act = ffn_saved
        big_grads["ffn_w_down", layer] = _mm_dw_down(f"ffn_down_dw_{layer}", act, dx)
        w_gu, w_dn = ffn_weights(layer, xin)
        dgu = _ffn_down_dx(f"ffn_down_dx_{layer}", dx, w_dn, 0, gu).reshape(N_DEV, t, nb_ffn)
        big_grads["ffn_w_gate_up", layer] = _mm_dw_colblock(f"ffn_gate_up_dw_{layer}", hf, dgu, blocked_in=True)
        pending.extend([("ffn_w_gate_up", layer), ("ffn_w_down", layer)])
        dx, g_ffn_norm[layer] = _mm_t_colblock_norm_bwd(
            f"ffn_gate_up_dx_{layer}", dgu, w_gu, 0, xin, ffn_norm[layer], dx, blocked_in=True)
        return dx

    dk = dv = None
    for i in reversed(range(n_b)):
        layer = n_a + i
        x_in, hb, q, o, ffn_saved = saved[layer]
        dx = ffn_bwd(dx, layer, ffn_saved)
        big_grads["b_w_o", i] = _mm_dw_natural(f"b_o_dw_{i}", o, dx)
        w_q, w_o = mixer_b_weights(i, x_in)
        do = _mm_t_natural(f"b_o_dx_{i}", dx, w_o, 0)
        dq, dk, dv, dbias = _attn_bwd(q, kvpad, biases[i], do, dk, dv, f"b_attn_bwd_{i}")
        g_rel[i] = _bias_grad(dbias, f"rel_bias_grad_{i}")
        big_grads["b_w_q", i] = _mm_dw_natural(f"b_q_dw_{i}", hb, dq)
        pending.extend([("b_w_o", i), ("b_w_q", i)])
        dh = _mm_t_natural(f"b_q_dx_{i}", dq, w_q, 0)
        dx, g_b_norm[i] = _rms_bwd(x_in, b_norm[i], dh, dx, f"b_norm_bwd_{i}")
        if i > 0:
            dx = start_exchange(dx, f"b{i}")

    dkv = jnp.concatenate([dk[LEFT:], dv[LEFT:]], axis=1).astype(BF16)
    big_grads["w_kv", 0] = _mm_dw_colblock("kv_proj_dw", h_kv, dkv)
    pending.append(("w_kv", 0))
    dx, g_kv_norm = _mm_t_colblock_norm_bwd("kv_proj_dx", dkv, w_kv_g, 0, x_kv, kv_norm, dx)
    dx = start_exchange(dx, "kv")

    for i in reversed(range(n_a)):
        x_in, h, zpre, p, ffn_saved = saved[i]
        dx = ffn_bwd(dx, i, ffn_saved)
        if i == 0:
            dx = start_exchange(dx, "f0")
        big_grads["a_w_out", i] = _mm_dw_natural(f"a_out_dw_{i}", p, dx)
        w_in, w_out = mixer_a_weights(i, x_in)
        dp = _mm_t_natural(f"a_out_dx_{i}", dx, w_out, 0)
        dz, g_w_sp[i], g_b_sp[i], g_a_sgu[i] = _sgu_bwd(
            zpre, dp, a_sgu_full[i], a_w_spatial[i], w_sp_t[i], b_full[i], f"a_sgu_bwd_{i}")
        big_grads["a_w_in", i] = _mm_dw_colblock(f"a_in_dw_{i}", h, dz)
        pending.extend([("a_w_out", i), ("a_w_in", i)])
        dx, g_a_norm[i] = _mm_t_colblock_norm_bwd(f"a_in_dx_{i}", dz, w_in, 0, x_in, a_norm_full[i], dx)
        if i > 0:
            dx = start_exchange(dx, f"a{i}")
    grad_x = dx[None]

    small_like = [jax.ShapeDtypeStruct((n_a, d), F32), jax.ShapeDtypeStruct((n_a, f_a), F32),
                  a_w_spatial, a_b_spatial, kv_norm, b_norm, b_rel_bias, ffn_norm, final_norm]
    small_partial = _pack(
        [jnp.stack(g_a_norm), jnp.stack(g_a_sgu), jnp.stack(g_w_sp), jnp.stack(g_b_sp), g_kv_norm,
         jnp.stack(g_b_norm), jnp.stack(g_rel), jnp.stack(g_ffn_norm), g_final], N_DEV * 8)
    chunk_rows = small_partial.shape[0] // N_DEV
    arrived = {}
    for keys, send, recv, srcs, lands, tag in in_flight_grads:
        srcs, lands = _split_wait(send, recv, srcs, lands, dx, f"exchange_wait_{tag}", exchange=True)
        for key, src, land in zip(keys, srcs, lands):
            arrived[key] = (land, src)
    last = _exchange([big_grads[key][None] for key in pending]
                     + [small_partial.reshape(1, N_DEV, chunk_rows, FLAT_LANES)], "exchange_last")
    for key, got in zip(pending, last[:-1]):
        arrived[key] = (got[0], got[0])
    small_sum = _ordered_sum(last[-1][0], "small_grad_sum")
    small_all = _all_gather([small_sum[None]], "gather_small_grads")[0]
    (ga_norm, ga_sgu, gw_sp, gb_sp, gkv_norm, gb_norm, g_relb, gffn_norm, gfinal) = _unpack(small_all, small_like)

    results = {}
    big_names = ["a_w_in", "a_w_out", "w_kv", "b_w_q", "b_w_o", "ffn_w_gate_up", "ffn_w_down"]
    big_wmv = [(a_w_in, m_a_w_in, v_a_w_in), (a_w_out, m_a_w_out, v_a_w_out),
               (w_kv[None], m_w_kv[None], v_w_kv[None]), (b_w_q, m_b_w_q, v_b_w_q), (b_w_o, m_b_w_o, v_b_w_o),
               (ffn_w_gate_up, m_ffn_w_gate_up, v_ffn_w_gate_up), (ffn_w_down, m_ffn_w_down, v_ffn_w_down)]
    me_arr = jnp.reshape(me, (1,)).astype(jnp.int32)
    for name, (w, m, v) in zip(big_names, big_wmv):
        outs = None
        for layer in range(w.shape[0]):
            got, own = arrived[name, layer]
            outs = _adamw_layer(got, own, w, m, v, layer, outs, me_arr, f"adamw_{name}_{layer}")
        if name == "w_kv":
            outs = [o[0] for o in outs]
        results[name] = outs

    n_cols = a_norm.shape[1]
    s_cols = a_sgu_norm.shape[1]
    small_g_list = [lax.dynamic_slice(ga_norm, (0, me * n_cols), (n_a, n_cols)),
                    lax.dynamic_slice(ga_sgu, (0, me * s_cols), (n_a, s_cols)),
                    gw_sp, gb_sp, gkv_norm, gb_norm, g_relb, gffn_norm, gfinal]
    small_names = ["a_norm", "a_sgu_norm", "a_w_spatial", "a_b_spatial", "kv_norm", "b_norm", "b_rel_bias",
                   "ffn_norm", "final_norm"]
    small_w = [a_norm, a_sgu_norm, a_w_spatial, a_b_spatial, kv_norm, b_norm, b_rel_bias, ffn_norm, final_norm]
    small_m = [m_a_norm, m_a_sgu_norm, m_a_w_spatial, m_a_b_spatial, m_kv_norm, m_b_norm, m_b_rel_bias,
               m_ffn_norm, m_final_norm]
    small_v = [v_a_norm, v_a_sgu_norm, v_a_w_spatial, v_a_b_spatial, v_kv_norm, v_b_norm, v_b_rel_bias,
               v_ffn_norm, v_final_norm]
    flat_g = _pack(small_g_list, 8)
    flat_out = _adamw(flat_g[None, None], _pack(small_w, 8)[None], _pack(small_m, 8)[None],
                      _pack(small_v, 8)[None], "adamw_small")
    unpacked = [_unpack(o[0], small_w) for o in flat_out]
    for idx, name in enumerate(small_names):
        results[name] = [unpacked[kind][idx] for kind in range(4)]

    order = ["a_norm", "a_w_in", "a_sgu_norm", "a_w_spatial", "a_b_spatial", "a_w_out", "kv_norm", "w_kv",
             "b_norm", "b_w_q", "b_rel_bias", "b_w_o", "ffn_norm", "ffn_w_gate_up", "ffn_w_down", "final_norm"]
    outputs = [loss, grad_x]
    for kind in range(4):
        outputs += [results[name][kind] for name in order]
    return tuple(outputs)
```

```python
import math

import jax
import jax.numpy as jnp
from jax import lax
from jax.experimental import pallas as pl
from jax.experimental.pallas import tpu as pltpu

F32 = jnp.float32
BF16 = jnp.bfloat16
MESH = pl.DeviceIdType.MESH
HBM_SPEC = pl.BlockSpec(memory_space=pltpu.HBM)
SEM_SPEC = pl.BlockSpec(memory_space=pltpu.SEMAPHORE)

N_DEV = 8
CHUNK = 64
A_CHUNK = 128
A_GROUPS = 8
N_LEFT_CHUNKS = 8
LEFT = N_LEFT_CHUNKS * CHUNK
PAIR_ROWS = 2 * CHUNK
PAIR_BAND = PAIR_ROWS + LEFT
Q_BLOCK = 2 * PAIR_ROWS
K_BLOCK = Q_BLOCK + LEFT
MAX_REL = 256
N_REL = 2 * MAX_REL + 1
REL_PAD = 640
HEAD_DIM = 64
HEAD_PAIR = 2 * HEAD_DIM
ATTN_SCALE = HEAD_DIM ** -0.5
EPS = 1e-6
NEG_INF = -1e30
ADAM_LR = 0.001
ADAM_B1 = 0.9
ADAM_B2 = 0.999
ADAM_EPS = 1e-08
ADAM_WD = 0.01
ADAM_STEP = 10
FLAT_LANES = 1024
V7X_VMEM_BYTES = 64 * 1024 * 1024
VMEM_FLOOR_BYTES = 32 * 1024 * 1024
VMEM_CEIL_BYTES = V7X_VMEM_BYTES - 8 * 1024 * 1024

NN = (((1,), (0,)), ((), ()))
NT = (((1,), (1,)), ((), ()))
TN = (((0,), (0,)), ((), ()))


def _tile(n, pref):
    return pref if n % pref == 0 else n


def _row_tile(n, pref, mult):
    best = None
    for t in range(mult, min(n, pref) + 1, mult):
        if n % t == 0:
            best = t
    return best if best is not None else n


def _nbytes(shape, dtype):
    n = 1
    for s in shape:
        if s is not None:
            n *= s
    return n * jnp.dtype(dtype).itemsize


def _call(body, name, grid, in_specs, out_specs, out_shape, scratch=(), vmem_bytes=0, aliases=None):
    limit = int(min(max(VMEM_FLOOR_BYTES, vmem_bytes * 5 // 4), VMEM_CEIL_BYTES))
    return pl.pallas_call(
        body,
        name=name,
        grid=grid,
        in_specs=in_specs,
        out_specs=out_specs,
        out_shape=out_shape,
        scratch_shapes=list(scratch),
        input_output_aliases=aliases or {},
        compiler_params=pltpu.CompilerParams(
            dimension_semantics=("arbitrary",) * len(grid), vmem_limit_bytes=limit),
    )


def _erf_parts(x):
    ax = jnp.abs(x) * (1.0 / math.sqrt(2.0))
    t = 1.0 / (1.0 + 0.3275911 * ax)
    poly = ((((1.061405429 * t - 1.453152027) * t + 1.421413741) * t - 0.284496736) * t + 0.254829592) * t
    ex = jnp.exp(-ax * ax)
    erf_abs = 1.0 - poly * ex
    return jnp.where(x < 0, -erf_abs, erf_abs), ex


def _gelu_and_grad(x):
    erf, ex = _erf_parts(x)
    cdf = 0.5 * (1.0 + erf)
    return x * cdf, cdf + x * ex * (1.0 / math.sqrt(2.0 * math.pi))


def _gelu(x):
    erf, _ = _erf_parts(x)
    return x * (0.5 * (1.0 + erf))


def _sigmoid(x):
    return 1.0 / (1.0 + jnp.exp(-x))


def _split3(x):
    hi = x.astype(BF16)
    r1 = x - hi.astype(F32)
    mid = r1.astype(BF16)
    lo = (r1 - mid.astype(F32)).astype(BF16)
    return hi, mid, lo


def _rms_fwd(x, g, name):
    t, d = x.shape
    tm = _tile(t, 512)

    def body(x_ref, g_ref, o_ref):
        xf = x_ref[...]
        r = lax.rsqrt(jnp.mean(xf * xf, axis=-1, keepdims=True) + EPS)
        o_ref[...] = (xf * r * g_ref[...]).astype(o_ref.dtype)

    return _call(
        body, name, (t // tm,),
        [pl.BlockSpec((tm, d), lambda i: (i, 0)), pl.BlockSpec((1, d), lambda i: (0, 0))],
        pl.BlockSpec((tm, d), lambda i: (i, 0)),
        jax.ShapeDtypeStruct((t, d), BF16),
        vmem_bytes=2 * (_nbytes((tm, d), F32) + _nbytes((tm, d), BF16)) + 4 * _nbytes((tm, d), F32),
    )(x, g.reshape(1, d))


def _rms_bwd(x, g, dh, dx_up, name):
    t, d = x.shape
    tm = _tile(t, 512)

    def body(x_ref, g_ref, dh_ref, up_ref, dx_ref, dg_ref):
        @pl.when(pl.program_id(0) == 0)
        def _():
            dg_ref[...] = jnp.zeros_like(dg_ref)

        xf = x_ref[...]
        r = lax.rsqrt(jnp.mean(xf * xf, axis=-1, keepdims=True) + EPS)
        xhat = xf * r
        dy = dh_ref[...].astype(F32)
        dxhat = dy * g_ref[...]
        dg_ref[...] += jnp.sum(dy * xhat, axis=0, keepdims=True)
        dx = r * (dxhat - xhat * jnp.mean(dxhat * xhat, axis=-1, keepdims=True))
        dx_ref[...] = up_ref[...] + dx

    row = pl.BlockSpec((tm, d), lambda i: (i, 0))
    vec = pl.BlockSpec((1, d), lambda i: (0, 0))
    dx, dg = _call(
        body, name, (t // tm,),
        [row, vec, row, row],
        [row, vec],
        [jax.ShapeDtypeStruct((t, d), F32), jax.ShapeDtypeStruct((1, d), F32)],
        vmem_bytes=10 * _nbytes((tm, d), F32),
    )(x, g.reshape(1, d), dh, dx_up)
    return dx, dg.reshape(d)


def _mm(name, dims, a, b, *, grid, a_spec, b_spec, out_shape, out_spec, acc_shape,
        res=None, res_spec=None, scale=None):
    nk = grid[2]
    has_res = res is not None

    def body(*refs):
        refs = list(refs)
        a_ref = refs.pop(0)
        b_ref = refs.pop(0)
        r_ref = refs.pop(0) if has_res else None
        o_ref = refs.pop(0)
        part = lax.dot_general(a_ref[...].astype(BF16), b_ref[...].astype(BF16), dims,
                               preferred_element_type=F32)

        def finish(acc):
            if scale is not None:
                acc = acc * scale
            if has_res:
                acc = acc + r_ref[...]
            o_ref[...] = acc.astype(o_ref.dtype)

        if nk == 1:
            finish(part)
        else:
            acc_ref = refs.pop(0)
            k = pl.program_id(2)

            @pl.when(k == 0)
            def _():
                acc_ref[...] = part

            @pl.when(k > 0)
            def _():
                acc_ref[...] += part

            @pl.when(k == nk - 1)
            def _():
                finish(acc_ref[...])

    operands = [a, b]
    in_specs = [a_spec, b_spec]
    vmem = 2 * (_nbytes(a_spec.block_shape, a.dtype) + _nbytes(b_spec.block_shape, b.dtype)
                + _nbytes(out_spec.block_shape, out_shape.dtype))
    vmem += 3 * _nbytes(acc_shape, F32)
    if has_res:
        operands.append(res)
        in_specs.append(res_spec)
        vmem += 2 * _nbytes(res_spec.block_shape, res.dtype)
    scratch = [pltpu.VMEM(acc_shape, F32)] if nk > 1 else []
    return _call(body, name, grid, in_specs, out_spec, out_shape, scratch=scratch, vmem_bytes=vmem)(*operands)


def _mm_colblock(name, h, w_g, layer):
    t, k = h.shape
    nb = w_g.shape[3]
    tm = _tile(t, 2048)
    return _mm(
        name, NN, h, w_g, grid=(t // tm, N_DEV, 1),
        a_spec=pl.BlockSpec((tm, k), lambda i, j, kk: (i, 0)),
        b_spec=pl.BlockSpec((None, None, k, nb), lambda i, j, kk: (layer, j, 0, 0)),
        out_shape=jax.ShapeDtypeStruct((t, N_DEV * nb), BF16),
        out_spec=pl.BlockSpec((tm, nb), lambda i, j, kk: (i, j)), acc_shape=(tm, nb))


def _mm_natural(name, a, w, layer, *, res=None, out_dtype=F32, scale=None):
    t, k = a.shape
    n = w.shape[2]
    tm = _tile(t, 1024)
    tn = _tile(n, 512)
    res_spec = None if res is None else pl.BlockSpec((tm, tn), lambda i, j, kk: (i, j))
    return _mm(
        name, NN, a, w, grid=(t // tm, n // tn, 1),
        a_spec=pl.BlockSpec((tm, k), lambda i, j, kk: (i, 0)),
        b_spec=pl.BlockSpec((None, k, tn), lambda i, j, kk: (layer, 0, j)),
        out_shape=jax.ShapeDtypeStruct((t, n), out_dtype),
        out_spec=pl.BlockSpec((tm, tn), lambda i, j, kk: (i, j)),
        acc_shape=(tm, tn), res=res, res_spec=res_spec, scale=scale)


def _mm_down(name, act, w4, layer, res):
    nblk, t, kb = act.shape
    n = w4.shape[3]
    tm = _tile(t, 1024)
    tn = _tile(n, 1024)
    return _mm(
        name, NN, act, w4, grid=(t // tm, n // tn, nblk),
        a_spec=pl.BlockSpec((None, tm, kb), lambda i, j, kk: (kk, i, 0)),
        b_spec=pl.BlockSpec((None, None, kb, tn), lambda i, j, kk: (layer, kk, 0, j)),
        out_shape=jax.ShapeDtypeStruct((t, n), F32),
        out_spec=pl.BlockSpec((tm, tn), lambda i, j, kk: (i, j)),
        acc_shape=(tm, tn), res=res, res_spec=pl.BlockSpec((tm, tn), lambda i, j, kk: (i, j)))


def _mm_t_colblock_norm_bwd(name, dz, w_g, layer, x, g, dx_up, blocked_in=False):
    k = w_g.shape[2]
    nb = w_g.shape[3]
    t = x.shape[0]
    tm = _tile(t, 1024)
    per_step = 2
    n_steps = N_DEV // per_step
    if blocked_in:
        a_spec = pl.BlockSpec((per_step, tm, nb), lambda i, kk: (kk, i, 0))
    else:
        a_spec = pl.BlockSpec((tm, per_step * nb), lambda i, kk: (i, kk))

    def body(a_ref, b_ref, x_ref, g_ref, up_ref, dx_ref, dg_ref, acc_ref):
        i = pl.program_id(0)
        kk = pl.program_id(1)
        part = None
        for u in range(per_step):
            a = a_ref[u] if blocked_in else a_ref[:, u * nb:(u + 1) * nb]
            term = lax.dot_general(a.astype(BF16), b_ref[u].astype(BF16), NT, preferred_element_type=F32)
            part = term if part is None else part + term

        @pl.when(kk == 0)
        def _():
            acc_ref[...] = part

        @pl.when(kk > 0)
        def _():
            acc_ref[...] += part

        @pl.when((i == 0) & (kk == 0))
        def _():
            dg_ref[...] = jnp.zeros_like(dg_ref)

        @pl.when(kk == n_steps - 1)
        def _():
            dy = acc_ref[...]
            xf = x_ref[...]
            r = lax.rsqrt(jnp.mean(xf * xf, axis=-1, keepdims=True) + EPS)
            xhat = xf * r
            dxhat = dy * g_ref[...]
            dg_ref[...] += jnp.sum(dy * xhat, axis=0, keepdims=True)
            dx_ref[...] = up_ref[...] + r * (dxhat - xhat * jnp.mean(dxhat * xhat, axis=-1, keepdims=True))

    row = pl.BlockSpec((tm, k), lambda i, kk: (i, 0))
    vec = pl.BlockSpec((1, k), lambda i, kk: (0, 0))
    dx, dg = _call(
        body, name, (t // tm, n_steps),
        [a_spec, pl.BlockSpec((None, per_step, k, nb), lambda i, kk: (layer, kk, 0, 0)), row, vec, row],
        [row, vec],
        [jax.ShapeDtypeStruct((t, k), F32), jax.ShapeDtypeStruct((1, k), F32)],
        scratch=[pltpu.VMEM((tm, k), F32)],
        vmem_bytes=2 * per_step * (_nbytes((tm, nb), BF16) + _nbytes((k, nb), BF16)) + 10 * _nbytes((tm, k), F32),
    )(dz, w_g, x, g.reshape(1, k), dx_up)
    return dx, dg.reshape(k)


def _ffn_gate_up(name, h, w_g, layer):
    t, k = h.shape
    nb = w_g.shape[3]
    half = N_DEV // 2
    tm = _tile(t, 1024)

    def body(h_ref, wg_ref, wu_ref, gu_ref, act_ref):
        hb = h_ref[...]
        gate = jnp.dot(hb, wg_ref[...], preferred_element_type=F32)
        up = jnp.dot(hb, wu_ref[...], preferred_element_type=F32)
        gu_ref[0] = gate.astype(BF16)
        gu_ref[1] = up.astype(BF16)
        act_ref[...] = (gate * _sigmoid(gate) * up).astype(BF16)

    return _call(
        body, name, (t // tm, half),
        [pl.BlockSpec((tm, k), lambda i, j: (i, 0)),
         pl.BlockSpec((None, None, k, nb), lambda i, j: (layer, j, 0, 0)),
         pl.BlockSpec((None, None, k, nb), lambda i, j: (layer, half + j, 0, 0))],
        [pl.BlockSpec((2, None, tm, nb), lambda i, j: (0, j, i, 0)),
         pl.BlockSpec((None, tm, nb), lambda i, j: (j, i, 0))],
        [jax.ShapeDtypeStruct((2, half, t, nb), BF16), jax.ShapeDtypeStruct((half, t, nb), BF16)],
        vmem_bytes=2 * (_nbytes((tm, k), BF16) + 2 * _nbytes((k, nb), BF16) + 3 * _nbytes((tm, nb), BF16))
        + 6 * _nbytes((tm, nb), F32),
    )(h, w_g, w_g)


def _ffn_down_dx(name, dy, w4, layer, gu4):
    t, n = dy.shape
    nblk, kb = w4.shape[1], w4.shape[2]
    tm = _tile(t, 1024)

    def body(dy_ref, w_ref, gu_ref, dgu_ref):
        da = lax.dot_general(dy_ref[...].astype(BF16), w_ref[...], NT, preferred_element_type=F32)
        gate = gu_ref[0].astype(F32)
        up = gu_ref[1].astype(F32)
        sig = _sigmoid(gate)
        dgu_ref[0] = (da * up * (sig * (1.0 + gate * (1.0 - sig)))).astype(BF16)
        dgu_ref[1] = (da * (gate * sig)).astype(BF16)

    blk = pl.BlockSpec((2, None, tm, kb), lambda i, j: (0, j, i, 0))
    return _call(
        body, name, (t // tm, nblk),
        [pl.BlockSpec((tm, n), lambda i, j: (i, 0)),
         pl.BlockSpec((None, None, kb, n), lambda i, j: (layer, j, 0, 0)),
         blk],
        blk,
        jax.ShapeDtypeStruct((2, nblk, t, kb), BF16),
        vmem_bytes=2 * (_nbytes((tm, n), F32) + _nbytes((kb, n), BF16) + 4 * _nbytes((tm, kb), BF16))
        + 8 * _nbytes((tm, kb), F32),
    )(dy, w4, gu4)


def _mm_t_natural(name, dy, w, layer):
    t, n = dy.shape
    k = w.shape[1]
    tm = _tile(t, 1024)
    tk = _tile(k, 512)
    return _mm(
        name, NT, dy, w, grid=(t // tm, k // tk, 1),
        a_spec=pl.BlockSpec((tm, n), lambda i, j, kk: (i, 0)),
        b_spec=pl.BlockSpec((None, tk, n), lambda i, j, kk: (layer, j, 0)),
        out_shape=jax.ShapeDtypeStruct((t, k), BF16),
        out_spec=pl.BlockSpec((tm, tk), lambda i, j, kk: (i, j)),
        acc_shape=(tm, tk))


def _mm_dw_colblock(name, h, dz, blocked_in=False, transposed=False):
    t, k = h.shape
    nb = dz.shape[2] if blocked_in else dz.shape[1] // N_DEV
    tk = _tile(t, 1024)
    h_spec = pl.BlockSpec((tk, k), lambda i, j, kk: (kk, 0))
    if blocked_in:
        dz_spec = pl.BlockSpec((None, tk, nb), lambda i, j, kk: (j, kk, 0))
    else:
        dz_spec = pl.BlockSpec((tk, nb), lambda i, j, kk: (kk, j))
    rows, cols = (nb, k) if transposed else (k, nb)
    return _mm(
        name, TN, *((dz, h) if transposed else (h, dz)), grid=(1, N_DEV, t // tk),
        a_spec=dz_spec if transposed else h_spec,
        b_spec=h_spec if transposed else dz_spec,
        out_shape=jax.ShapeDtypeStruct((N_DEV, rows, cols), BF16),
        out_spec=pl.BlockSpec((None, rows, cols), lambda i, j, kk: (j, 0, 0)),
        acc_shape=(rows, cols))


def _mm_dw_natural(name, a, dy):
    t, k = a.shape
    n = dy.shape[1]
    tko = _tile(k, 1024)
    tt = _tile(t, 1024)
    out = _mm(
        name, TN, a, dy, grid=(k // tko, 1, t // tt),
        a_spec=pl.BlockSpec((tt, tko), lambda i, j, kk: (kk, i)),
        b_spec=pl.BlockSpec((tt, n), lambda i, j, kk: (kk, 0)),
        out_shape=jax.ShapeDtypeStruct((k, n), BF16),
        out_spec=pl.BlockSpec((tko, n), lambda i, j, kk: (i, 0)),
        acc_shape=(tko, n))
    return out.reshape(N_DEV, k // N_DEV, n)


def _mm_dw_down(name, act, dy):
    nblk, t, kb = act.shape
    n = dy.shape[1]
    tt = _tile(t, 1024)
    out = _mm(
        name, TN, act, dy, grid=(nblk, 1, t // tt),
        a_spec=pl.BlockSpec((None, tt, kb), lambda i, j, kk: (i, kk, 0)),
        b_spec=pl.BlockSpec((tt, n), lambda i, j, kk: (kk, 0)),
        out_shape=jax.ShapeDtypeStruct((nblk, kb, n), BF16),
        out_spec=pl.BlockSpec((None, kb, n), lambda i, j, kk: (i, 0, 0)),
        acc_shape=(kb, n))
    return out.reshape(N_DEV, (nblk * kb) // N_DEV, n)


def _spatial_mask(transposed=False):
    r = lax.broadcasted_iota(jnp.int32, (A_CHUNK, A_CHUNK), 0) // CHUNK
    c = lax.broadcasted_iota(jnp.int32, (A_CHUNK, A_CHUNK), 1) // CHUNK
    return c >= r if transposed else r >= c


def _sgu_tile(t):
    return _tile(t, 2 * A_CHUNK)


def _sgu_fwd(zpre, g_sgu, w_sp, b_full, name):
    t, f2 = zpre.shape
    f = f2 // 2
    gd = f // A_GROUPS
    tm = _sgu_tile(t)

    def body(z_ref, g_ref, w_ref, b_ref, p_ref):
        mask = _spatial_mask()
        wm = [jnp.where(mask, w_ref[g], 0.0).astype(BF16) for g in range(A_GROUPS)]
        for c in range(tm // A_CHUNK):
            rows = pl.ds(c * A_CHUNK, A_CHUNK)
            z = _gelu(z_ref[rows, :].astype(F32))
            u = z[:, :f]
            v0 = z[:, f:]
            r = lax.rsqrt(jnp.mean(v0 * v0, axis=-1, keepdims=True) + EPS)
            v1 = (v0 * r * g_ref[...]).astype(BF16)
            for g in range(A_GROUPS):
                cols = slice(g * gd, (g + 1) * gd)
                v2 = jnp.dot(wm[g], v1[:, cols], preferred_element_type=F32) + b_ref[:, cols]
                p_ref[rows, cols] = (u[:, cols] * v2).astype(BF16)

    return _call(
        body, name, (t // tm,),
        [pl.BlockSpec((tm, f2), lambda i: (i, 0)),
         pl.BlockSpec((1, f), lambda i: (0, 0)),
         pl.BlockSpec((A_GROUPS, A_CHUNK, A_CHUNK), lambda i: (0, 0, 0)),
         pl.BlockSpec((A_CHUNK, f), lambda i: (0, 0))],
        pl.BlockSpec((tm, f), lambda i: (i, 0)),
        jax.ShapeDtypeStruct((t, f), BF16),
        vmem_bytes=2 * _nbytes((tm, f2), BF16) + 2 * _nbytes((tm, f), BF16) + 8 * _nbytes((A_CHUNK, f2), F32),
    )(zpre, g_sgu.reshape(1, f), w_sp, b_full)


def _sgu_bwd(zpre, dp, g_sgu, w_sp, w_sp_t, b_full, name):
    t, f2 = zpre.shape
    f = f2 // 2
    gd = f // A_GROUPS
    tm = _sgu_tile(t)
    n_steps = t // tm

    def body(z_ref, dp_ref, g_ref, w_ref, wt_ref, b_ref, dz_ref, dw_ref, db_ref, dg_ref, dv1_ref, dbf_ref):
        step = pl.program_id(0)

        @pl.when(step == 0)
        def _():
            dw_ref[...] = jnp.zeros_like(dw_ref)
            dg_ref[...] = jnp.zeros_like(dg_ref)
            dbf_ref[...] = jnp.zeros_like(dbf_ref)

        mask = _spatial_mask()
        mask_t = _spatial_mask(transposed=True)
        wm = [jnp.where(mask, w_ref[g], 0.0).astype(BF16) for g in range(A_GROUPS)]
        wmt = [jnp.where(mask_t, wt_ref[g], 0.0).astype(BF16) for g in range(A_GROUPS)]
        gain = g_ref[...]
        for c in range(tm // A_CHUNK):
            rows = pl.ds(c * A_CHUNK, A_CHUNK)
            z, dgelu = _gelu_and_grad(z_ref[rows, :].astype(F32))
            u = z[:, :f]
            v0 = z[:, f:]
            r = lax.rsqrt(jnp.mean(v0 * v0, axis=-1, keepdims=True) + EPS)
            xhat = v0 * r
            v1 = (xhat * gain).astype(BF16)
            dpf = dp_ref[rows, :].astype(F32)
            for g in range(A_GROUPS):
                cols = slice(g * gd, (g + 1) * gd)
                v1g = v1[:, cols]
                v2 = jnp.dot(wm[g], v1g, preferred_element_type=F32) + b_ref[:, cols]
                dpg = dpf[:, cols]
                dz_ref[rows, cols] = (dpg * v2 * dgelu[:, cols]).astype(BF16)
                dv2 = dpg * u[:, cols]
                dbf_ref[:, cols] += dv2
                dv2b = dv2.astype(BF16)
                dwg = lax.dot_general(dv2b, v1g, NT, preferred_element_type=F32)
                dw_ref[g] += jnp.where(mask, dwg, 0.0)
                dv1_ref[:, cols] = jnp.dot(wmt[g], dv2b, preferred_element_type=F32)
            dv1 = dv1_ref[...]
            dxhat = dv1 * gain
            dg_ref[...] += jnp.sum(dv1 * xhat, axis=0, keepdims=True)
            dv0 = r * (dxhat - xhat * jnp.mean(dxhat * xhat, axis=-1, keepdims=True))
            dz_ref[rows, pl.ds(f, f)] = (dv0 * dgelu[:, f:]).astype(BF16)

        @pl.when(step == n_steps - 1)
        def _():
            for g in range(A_GROUPS):
                db_ref[g] = jnp.sum(dbf_ref[:, g * gd:(g + 1) * gd], axis=1, keepdims=True)

    wspec = pl.BlockSpec((A_GROUPS, A_CHUNK, A_CHUNK), lambda i: (0, 0, 0))
    dz, dw, db, dg = _call(
        body, name, (n_steps,),
        [pl.BlockSpec((tm, f2), lambda i: (i, 0)),
         pl.BlockSpec((tm, f), lambda i: (i, 0)),
         pl.BlockSpec((1, f), lambda i: (0, 0)),
         wspec, wspec,
         pl.BlockSpec((A_CHUNK, f), lambda i: (0, 0))],
        [pl.BlockSpec((tm, f2), lambda i: (i, 0)),
         wspec,
         pl.BlockSpec((A_GROUPS, A_CHUNK, 1), lambda i: (0, 0, 0)),
         pl.BlockSpec((1, f), lambda i: (0, 0))],
        [jax.ShapeDtypeStruct((t, f2), BF16),
         jax.ShapeDtypeStruct((A_GROUPS, A_CHUNK, A_CHUNK), F32),
         jax.ShapeDtypeStruct((A_GROUPS, A_CHUNK, 1), F32),
         jax.ShapeDtypeStruct((1, f), F32)],
        scratch=[pltpu.VMEM((A_CHUNK, f), F32), pltpu.VMEM((A_CHUNK, f), F32)],
        vmem_bytes=4 * _nbytes((tm, f2), BF16) + 2 * _nbytes((tm, f), BF16) + 12 * _nbytes((A_CHUNK, f2), F32),
    )(zpre, dp, g_sgu.reshape(1, f), w_sp, w_sp_t, b_full)
    return dz, dw, db.reshape(A_GROUPS, A_CHUNK), dg.reshape(f)


def _pair_valid(qi, col):
    qc = qi // CHUNK
    kc = col // CHUNK
    return (kc >= qc) & (kc <= qc + N_LEFT_CHUNKS)


def _diagonal_onehot():
    e = lax.broadcasted_iota(jnp.int32, (REL_PAD, K_BLOCK), 1)
    idx = jnp.clip(PAIR_BAND - 1 - e, -MAX_REL, MAX_REL) + MAX_REL
    r = lax.broadcasted_iota(jnp.int32, (REL_PAD, K_BLOCK), 0)
    return jnp.where(r == idx, 1.0, 0.0).astype(BF16)


def _bias_build(table, name):
    h = table.shape[0]
    tab = jnp.pad(table, ((0, 0), (0, REL_PAD - N_REL)))

    def body(t_ref, o_ref):
        oh = _diagonal_onehot()
        diag = jnp.zeros((h, K_BLOCK), F32)
        for piece in _split3(t_ref[...]):
            diag += jnp.dot(piece, oh, preferred_element_type=F32)
        col = lax.broadcasted_iota(jnp.int32, (h, PAIR_BAND), 1)
        for qi in range(PAIR_ROWS):
            row = pltpu.roll(diag, (qi - (PAIR_ROWS - 1)) % K_BLOCK, 1)[:, :PAIR_BAND]
            o_ref[qi] = jnp.where(_pair_valid(qi, col), row, NEG_INF)

    out = _call(
        body, name, (1,),
        [pl.BlockSpec((h, REL_PAD), lambda i: (0, 0))],
        pl.BlockSpec((PAIR_ROWS, h, PAIR_BAND), lambda i: (0, 0, 0)),
        jax.ShapeDtypeStruct((PAIR_ROWS, h, PAIR_BAND), F32),
        vmem_bytes=4 * _nbytes((PAIR_ROWS, h, PAIR_BAND), F32),
    )(tab)
    return jnp.transpose(out, (1, 0, 2))


def _bias_block(pair_bias):
    rest = K_BLOCK - PAIR_BAND
    top = jnp.pad(pair_bias, ((0, 0), (0, 0), (0, rest)), constant_values=NEG_INF)
    bottom = jnp.pad(pair_bias, ((0, 0), (0, 0), (rest, 0)), constant_values=NEG_INF)
    return jnp.concatenate([top, bottom], axis=1)


def _bias_grad(dbias, name):
    h = dbias.shape[0]
    db_t = jnp.transpose(dbias, (1, 0, 2))

    def body(d_ref, o_ref):
        diag = jnp.zeros((h, K_BLOCK), F32)
        for qi in range(PAIR_ROWS):
            diag += pltpu.roll(d_ref[qi], PAIR_ROWS - 1 - qi, 1)
        oh = _diagonal_onehot()
        acc = jnp.zeros((h, REL_PAD), F32)
        for piece in _split3(diag):
            acc += lax.dot_general(piece, oh, NT, preferred_element_type=F32)
        o_ref[...] = acc

    out = _call(
        body, name, (1,),
        [pl.BlockSpec((PAIR_ROWS, h, K_BLOCK), lambda i: (0, 0, 0))],
        pl.BlockSpec((h, REL_PAD), lambda i: (0, 0)),
        jax.ShapeDtypeStruct((h, REL_PAD), F32),
        vmem_bytes=4 * _nbytes((PAIR_ROWS, h, K_BLOCK), F32),
    )(db_t)
    return out[:, :N_REL]


def _head_masks():
    lane = lax.broadcasted_iota(jnp.int32, (Q_BLOCK, HEAD_PAIR), 1)
    return lane < HEAD_DIM, lane >= HEAD_DIM


def _block_probs(qm, kb, bias, valid):
    s = lax.dot_general(qm, kb, NT, preferred_element_type=F32) + bias
    if valid is not None:
        s = jnp.where(valid, s, NEG_INF)
    e = jnp.exp(s - jnp.max(s, axis=-1, keepdims=True))
    return e * (1.0 / jnp.sum(e, axis=-1, keepdims=True))


def _padded_then_plain(step, n_blocks):
    n_padded = min(LEFT // Q_BLOCK, n_blocks)
    lax.fori_loop(0, n_padded, lambda j, c: step(j, c, True), 0)
    lax.fori_loop(n_padded, n_blocks, lambda j, c: step(j, c, False), 0)


def _attn_fwd(q, kvpad, bias, name):
    t, d = q.shape
    n_pairs = d // HEAD_PAIR
    n_blocks = t // Q_BLOCK

    def body(q_ref, k_ref, v_ref, b_ref, o_ref):
        masks = _head_masks()
        key = lax.broadcasted_iota(jnp.int32, (Q_BLOCK, K_BLOCK), 1)

        def step(j, carry, padded):
            r0 = pl.multiple_of(j * Q_BLOCK, Q_BLOCK)
            q2 = q_ref[pl.ds(r0, Q_BLOCK), :].astype(F32)
            kb = k_ref[pl.ds(r0, K_BLOCK), :]
            vb = v_ref[pl.ds(r0, K_BLOCK), :]
            valid = key >= LEFT - j * Q_BLOCK if padded else None
            outs = []
            for a in range(2):
                qm = jnp.where(masks[a], q2, 0.0).astype(BF16)
                p = _block_probs(qm, kb, b_ref[a], valid)
                outs.append(jnp.dot(p.astype(BF16), vb, preferred_element_type=F32))
            o_ref[pl.ds(r0, Q_BLOCK), :] = jnp.where(masks[0], outs[0], outs[1]).astype(BF16)
            return carry

        _padded_then_plain(step, n_blocks)

    return _call(
        body, name, (n_pairs,),
        [pl.BlockSpec((t, HEAD_PAIR), lambda p: (0, p)),
         pl.BlockSpec((LEFT + t, HEAD_PAIR), lambda p: (0, p)),
         pl.BlockSpec((LEFT + t, HEAD_PAIR), lambda p: (0, n_pairs + p)),
         pl.BlockSpec((2, Q_BLOCK, K_BLOCK), lambda p: (p, 0, 0))],
        pl.BlockSpec((t, HEAD_PAIR), lambda p: (0, p)),
        jax.ShapeDtypeStruct((t, d), BF16),
        vmem_bytes=8 * _nbytes((LEFT + t, HEAD_PAIR), BF16) + 12 * _nbytes((2, Q_BLOCK, K_BLOCK), F32),
    )(q, kvpad, kvpad, bias)


def _attn_bwd(q, kvpad, bias, do, dk_in, dv_in, name):
    t, d = q.shape
    n_pairs = d // HEAD_PAIR
    n_blocks = t // Q_BLOCK
    has_in = dk_in is not None

    def body(*refs):
        refs = list(refs)
        q_ref, k_ref, v_ref, b_ref, do_ref = refs[:5]
        refs = refs[5:]
        if has_in:
            dki_ref, dvi_ref = refs[:2]
            refs = refs[2:]
        dq_ref, dk_ref, dv_ref, db_ref = refs
        masks = _head_masks()
        key = lax.broadcasted_iota(jnp.int32, (Q_BLOCK, K_BLOCK), 1)
        if has_in:
            dk_ref[...] = dki_ref[...]
            dv_ref[...] = dvi_ref[...]
        else:
            dk_ref[...] = jnp.zeros_like(dk_ref)
            dv_ref[...] = jnp.zeros_like(dv_ref)
        db_ref[...] = jnp.zeros_like(db_ref)

        def step(j, carry, padded):
            r0 = pl.multiple_of(j * Q_BLOCK, Q_BLOCK)
            q2 = q_ref[pl.ds(r0, Q_BLOCK), :].astype(F32)
            do2 = do_ref[pl.ds(r0, Q_BLOCK), :].astype(F32)
            kb = k_ref[pl.ds(r0, K_BLOCK), :]
            vb = v_ref[pl.ds(r0, K_BLOCK), :]
            valid = key >= LEFT - j * Q_BLOCK if padded else None
            dqs = []
            dk_acc = jnp.zeros((K_BLOCK, HEAD_PAIR), F32)
            dv_acc = jnp.zeros((K_BLOCK, HEAD_PAIR), F32)
            for a in range(2):
                qm = jnp.where(masks[a], q2, 0.0).astype(BF16)
                dom = jnp.where(masks[a], do2, 0.0).astype(BF16)
                p = _block_probs(qm, kb, b_ref[a], valid)
                dp = lax.dot_general(dom, vb, NT, preferred_element_type=F32)
                ds = p * (dp - jnp.sum(dp * p, axis=-1, keepdims=True))
                db_ref[a] += ds[:PAIR_ROWS, :]
                db_ref[a, :, pl.ds(0, PAIR_BAND)] += ds[PAIR_ROWS:, K_BLOCK - PAIR_BAND:]
                dsb = ds.astype(BF16)
                dqs.append(jnp.dot(dsb, kb, preferred_element_type=F32))
                dk_acc += lax.dot_general(dsb, qm, TN, preferred_element_type=F32)
                dv_acc += lax.dot_general(p.astype(BF16), dom, TN, preferred_element_type=F32)
            dq = jnp.where(masks[0], dqs[0], dqs[1]) * ATTN_SCALE
            dq_ref[pl.ds(r0, Q_BLOCK), :] = dq.astype(BF16)
            dk_ref[pl.ds(r0, K_BLOCK), :] += dk_acc
            dv_ref[pl.ds(r0, K_BLOCK), :] += dv_acc
            return carry

        _padded_then_plain(step, n_blocks)

    q_spec = pl.BlockSpec((t, HEAD_PAIR), lambda p: (0, p))
    kv_spec = pl.BlockSpec((LEFT + t, HEAD_PAIR), lambda p: (0, p))
    operands = [q, kvpad, kvpad, bias, do]
    in_specs = [q_spec, kv_spec, pl.BlockSpec((LEFT + t, HEAD_PAIR), lambda p: (0, n_pairs + p)),
                pl.BlockSpec((2, Q_BLOCK, K_BLOCK), lambda p: (p, 0, 0)), q_spec]
    aliases = None
    if has_in:
        operands += [dk_in, dv_in]
        in_specs += [kv_spec, kv_spec]
        aliases = {5: 1, 6: 2}
    return _call(
        body, name, (n_pairs,),
        in_specs,
        [q_spec, kv_spec, kv_spec, pl.BlockSpec((2, PAIR_ROWS, K_BLOCK), lambda p: (p, 0, 0))],
        [jax.ShapeDtypeStruct((t, d), BF16),
         jax.ShapeDtypeStruct((LEFT + t, d), F32),
         jax.ShapeDtypeStruct((LEFT + t, d), F32),
         jax.ShapeDtypeStruct((d // HEAD_DIM, PAIR_ROWS, K_BLOCK), F32)],
        vmem_bytes=10 * _nbytes((LEFT + t, HEAD_PAIR), BF16) + 8 * _nbytes((LEFT + t, HEAD_PAIR), F32)
        + 16 * _nbytes((2, Q_BLOCK, K_BLOCK), F32),
        aliases=aliases,
    )(*operands)


def _loss_head(x, g, target, name):
    t, d = x.shape
    tm = _tile(t, 512)

    def body(x_ref, g_ref, t_ref, dx_ref, loss_ref, dg_ref):
        @pl.when(pl.program_id(0) == 0)
        def _():
            loss_ref[...] = jnp.zeros_like(loss_ref)
            dg_ref[...] = jnp.zeros_like(dg_ref)

        xf = x_ref[...]
        r = lax.rsqrt(jnp.mean(xf * xf, axis=-1, keepdims=True) + EPS)
        xhat = xf * r
        diff = xhat * g_ref[...] - t_ref[...]
        row_loss = jnp.mean(diff * diff, axis=-1, keepdims=True)
        loss_ref[...] += 0.5 * jnp.sum(row_loss, axis=0, keepdims=True)
        dy = diff * (1.0 / d)
        dg_ref[...] += jnp.sum(dy * xhat, axis=0, keepdims=True)
        dxhat = dy * g_ref[...]
        dx_ref[...] = r * (dxhat - xhat * jnp.mean(dxhat * xhat, axis=-1, keepdims=True))

    row = pl.BlockSpec((tm, d), lambda i: (i, 0))
    vec = pl.BlockSpec((1, d), lambda i: (0, 0))
    dx, loss, dg = _call(
        body, name, (t // tm,),
        [row, vec, row],
        [row, pl.BlockSpec((1, 1), lambda i: (0, 0)), vec],
        [jax.ShapeDtypeStruct((t, d), F32), jax.ShapeDtypeStruct((1, 1), F32), jax.ShapeDtypeStruct((1, d), F32)],
        vmem_bytes=10 * _nbytes((tm, d), F32),
    )(x, g.reshape(1, d), target)
    return dx, loss[0, 0], dg.reshape(d)


def _adamw_store(g, w_ref, m_ref, v_ref, g_ref, d_ref, nm_ref, nv_ref):
    c1 = 1.0 / (1.0 - ADAM_B1 ** ADAM_STEP)
    c2 = 1.0 / (1.0 - ADAM_B2 ** ADAM_STEP)
    nm = ADAM_B1 * m_ref[...] + (1.0 - ADAM_B1) * g
    nv = ADAM_B2 * v_ref[...] + (1.0 - ADAM_B2) * (g * g)
    g_ref[...] = g
    nm_ref[...] = nm
    nv_ref[...] = nv
    d_ref[...] = -ADAM_LR * ((nm * c1) / (jnp.sqrt(nv * c2) + ADAM_EPS) + ADAM_WD * w_ref[...])


def _adamw_layer(recv, own, w, m, v, layer, prev, me, name):
    n_src, r, c = recv.shape
    tr = _row_tile(r, max(16, (256 * 1024) // c), 16)
    first = prev is None

    def body(me_ref, recv_ref, own_ref, w_ref, m_ref, v_ref, *rest):
        mine = me_ref[0]
        own_part = own_ref[...].astype(F32)
        g = None
        for s in range(n_src):
            part = jnp.where(mine == s, own_part, recv_ref[s].astype(F32))
            g = part if g is None else g + part
        _adamw_store(g, w_ref, m_ref, v_ref, *rest[-4:])

    blk = pl.BlockSpec((None, tr, c), lambda i, me_ref: (layer, i, 0))
    any_spec = pl.BlockSpec(memory_space=pl.ANY)
    out = jax.ShapeDtypeStruct(w.shape, F32)
    operands = [me, recv, own, w, m, v] + ([] if first else list(prev))
    vmem = 2 * _nbytes((n_src + 1, tr, c), BF16) + 18 * _nbytes((tr, c), F32)
    return pl.pallas_call(
        body,
        name=name,
        grid_spec=pltpu.PrefetchScalarGridSpec(
            num_scalar_prefetch=1,
            grid=(r // tr,),
            in_specs=[pl.BlockSpec((n_src, tr, c), lambda i, me_ref: (0, i, 0)),
                      pl.BlockSpec((None, tr, c), lambda i, me_ref: (me_ref[0], i, 0)),
                      blk, blk, blk] + ([] if first else [any_spec] * 4),
            out_specs=[blk, blk, blk, blk],
        ),
        out_shape=[out, out, out, out],
        input_output_aliases={} if first else {6 + j: j for j in range(4)},
        compiler_params=pltpu.CompilerParams(
            dimension_semantics=("arbitrary",),
            vmem_limit_bytes=int(min(max(VMEM_FLOOR_BYTES, vmem * 5 // 4), VMEM_CEIL_BYTES))),
    )(*operands)


def _adamw(parts, w, m, v, name):
    n_layers, n_src, r, c = parts.shape
    mult = 16 if parts.dtype == BF16 else 8
    tr = _row_tile(r, max(mult, (256 * 1024) // c), mult)

    def body(p_ref, w_ref, m_ref, v_ref, g_ref, d_ref, nm_ref, nv_ref):
        g = p_ref[0].astype(F32)
        for s in range(1, n_src):
            g = g + p_ref[s].astype(F32)
        _adamw_store(g, w_ref, m_ref, v_ref, g_ref, d_ref, nm_ref, nv_ref)

    blk = pl.BlockSpec((None, tr, c), lambda l, i: (l, i, 0))
    out = jax.ShapeDtypeStruct((n_layers, r, c), F32)
    return _call(
        body, name, (n_layers, r // tr),
        [pl.BlockSpec((None, n_src, tr, c), lambda l, i: (l, 0, i, 0)), blk, blk, blk],
        [blk, blk, blk, blk],
        [out, out, out, out],
        vmem_bytes=2 * _nbytes((n_src, tr, c), parts.dtype) + 18 * _nbytes((tr, c), F32),
    )(parts, w, m, v)


def _ordered_sum(parts, name):
    n_src, r, c = parts.shape

    def body(p_ref, o_ref):
        acc = p_ref[0]
        for s in range(1, n_src):
            acc = acc + p_ref[s]
        o_ref[...] = acc

    return _call(
        body, name, (1,),
        [pl.BlockSpec((n_src, r, c), lambda i: (0, 0, 0))],
        pl.BlockSpec((r, c), lambda i: (0, 0)),
        jax.ShapeDtypeStruct((r, c), F32),
        vmem_bytes=4 * _nbytes((n_src, r, c), F32),
    )(parts)


def _position():
    return lax.axis_index("x"), lax.axis_index("y"), lax.axis_index("c")


def _linear(p):
    return 4 * p[0] + 2 * p[1] + p[2]


def _all_gather(shards, name):
    n = len(shards)

    def body(*refs):
        ins, outs = refs[:n], refs[n:2 * n]
        send_sems, recv_sems, local_sems = refs[2 * n:]
        x, y, c = _position()
        me, sibling = (x, y, c), (x, y, 1 - c)
        chips = [(1 - x, y), (x, 1 - y), (1 - x, 1 - y)]

        def slab(t, p):
            return outs[t].at[:, _linear(p)]

        def copy(t, k, block, to, src=None):
            return pltpu.make_async_remote_copy(
                src_ref=slab(t, block) if src is None else src,
                dst_ref=slab(t, block),
                send_sem=send_sems.at[t, k],
                recv_sem=recv_sems.at[t, k],
                device_id=to,
                device_id_type=MESH,
            )

        started = []
        for t in range(n):
            mine = pltpu.make_async_copy(ins[t], slab(t, me), local_sems.at[t])
            mine.start()
            started.append(mine)
        sends = []
        for t in range(n):
            first = [copy(t, 0, me, sibling, src=ins[t])]
            first += [copy(t, 1 + j, me, (*chip, c), src=ins[t]) for j, chip in enumerate(chips)]
            for cp in first:
                cp.start()
            sends += first
        for t in range(n):
            for j, chip in enumerate(chips):
                copy(t, 1 + j, (*chip, c), me).wait_recv()
                passed = copy(t, 4 + j, (*chip, c), sibling)
                passed.start()
                sends.append(passed)
        for t in range(n):
            copy(t, 0, sibling, me).wait_recv()
            for j, chip in enumerate(chips):
                copy(t, 4 + j, (*chip, 1 - c), me).wait_recv()
        for cp in sends:
            cp.wait_send()
        for mine in started:
            mine.wait()

    out_shape = [jax.ShapeDtypeStruct((s.shape[0], N_DEV) + s.shape[1:], s.dtype) for s in shards]
    return pl.pallas_call(
        body,
        name=name,
        in_specs=[HBM_SPEC] * n,
        out_specs=[HBM_SPEC] * n,
        out_shape=out_shape,
        scratch_shapes=[
            pltpu.SemaphoreType.DMA((n, N_DEV - 1)),
            pltpu.SemaphoreType.DMA((n, N_DEV - 1)),
            pltpu.SemaphoreType.DMA((n,)),
        ],
    )(*shards)


def _exchange(blocks, name):
    n = len(blocks)

    def body(*refs):
        ins, outs = refs[:n], refs[n:2 * n]
        send_sems, recv_sems, local_sems = refs[2 * n:]
        x, y, c = _position()
        me = _linear((x, y, c))
        flips = [(fx, fy, fc) for fx in (0, 1) for fy in (0, 1) for fc in (0, 1)][1:]

        def peer_of(flip):
            fx, fy, fc = flip
            return (1 - x if fx else x, 1 - y if fy else y, 1 - c if fc else c)

        def copy(t, k, peer):
            return pltpu.make_async_remote_copy(
                src_ref=ins[t].at[:, _linear(peer)],
                dst_ref=outs[t].at[:, me],
                send_sem=send_sems.at[t, k],
                recv_sem=recv_sems.at[t, k],
                device_id=peer,
                device_id_type=MESH,
            )

        def arrival(t, k, peer):
            return pltpu.make_async_remote_copy(
                src_ref=ins[t].at[:, _linear(peer)],
                dst_ref=outs[t].at[:, _linear(peer)],
                send_sem=send_sems.at[t, k],
                recv_sem=recv_sems.at[t, k],
                device_id=peer,
                device_id_type=MESH,
            )

        own = []
        for t in range(n):
            cp = pltpu.make_async_copy(ins[t].at[:, me], outs[t].at[:, me], local_sems.at[t])
            cp.start()
            own.append(cp)
        sends = []
        for t in range(n):
            for k, flip in enumerate(flips):
                cp = copy(t, k, peer_of(flip))
                cp.start()
                sends.append(cp)
        for t in range(n):
            for k, flip in enumerate(flips):
                arrival(t, k, peer_of(flip)).wait_recv()
        for cp in sends:
            cp.wait_send()
        for cp in own:
            cp.wait()

    out_shape = [jax.ShapeDtypeStruct(b.shape, b.dtype) for b in blocks]
    return pl.pallas_call(
        body,
        name=name,
        in_specs=[HBM_SPEC] * n,
        out_specs=[HBM_SPEC] * n,
        out_shape=out_shape,
        scratch_shapes=[
            pltpu.SemaphoreType.DMA((n, N_DEV - 1)),
            pltpu.SemaphoreType.DMA((n, N_DEV - 1)),
            pltpu.SemaphoreType.DMA((n,)),
        ],
    )(*blocks)


def _peers():
    x, y, c = _position()
    flips = [(fx, fy, fc) for fx in (0, 1) for fy in (0, 1) for fc in (0, 1)][1:]
    return [(1 - x if fx else x, 1 - y if fy else y, 1 - c if fc else c) for fx, fy, fc in flips]


def _split_start(srcs, lands, carry, name, exchange=False):
    n = len(srcs)

    def body(*refs):
        src_refs, land_refs = refs[:n], refs[n:2 * n]
        send_sems, recv_sems = refs[2 * n + 1], refs[2 * n + 2]
        me = _linear(_position())
        for t in range(n):
            for k, peer in enumerate(_peers()):
                pltpu.make_async_remote_copy(
                    src_ref=src_refs[t].at[_linear(peer)] if exchange else src_refs[t],
                    dst_ref=land_refs[t].at[me],
                    send_sem=send_sems.at[t * (N_DEV - 1) + k],
                    recv_sem=recv_sems.at[t * (N_DEV - 1) + k],
                    device_id=peer,
                    device_id_type=MESH,
                ).start()

    operands = list(srcs) + list(lands) + [carry]
    sems = pltpu.SemaphoreType.DMA((n * (N_DEV - 1),))
    out = pl.pallas_call(
        body,
        name=name,
        in_specs=[HBM_SPEC] * len(operands),
        out_specs=[SEM_SPEC, SEM_SPEC] + [HBM_SPEC] * len(operands),
        out_shape=[sems, sems] + [pltpu.HBM(a.shape, a.dtype) for a in operands],
        input_output_aliases={i: 2 + i for i in range(len(operands))},
        compiler_params=pltpu.CompilerParams(has_side_effects=pltpu.SideEffectType.DATAFLOW_SIDE_EFFECTING),
    )(*[pltpu.with_memory_space_constraint(a, pltpu.HBM) for a in operands])
    return out[0], out[1], out[2:2 + n], out[2 + n:2 + 2 * n], out[2 + 2 * n]


def _split_wait(send_sems, recv_sems, srcs, lands, after, name, exchange=False):
    n = len(srcs)

    def body(*refs):
        src_refs, land_refs = refs[:n], refs[n:2 * n]
        send_ref, recv_ref = refs[2 * n], refs[2 * n + 1]
        for t in range(n):
            for k, peer in enumerate(_peers()):
                copy = pltpu.make_async_remote_copy(
                    src_ref=src_refs[t].at[0] if exchange else src_refs[t],
                    dst_ref=land_refs[t].at[0],
                    send_sem=send_ref.at[t * (N_DEV - 1) + k],
                    recv_sem=recv_ref.at[t * (N_DEV - 1) + k],
                    device_id=peer,
                    device_id_type=MESH,
                )
                copy.wait_send()
                copy.wait_recv()

    arrays = list(srcs) + list(lands)
    out = pl.pallas_call(
        body,
        name=name,
        in_specs=[HBM_SPEC] * len(arrays) + [SEM_SPEC, SEM_SPEC, pl.BlockSpec(memory_space=pl.ANY)],
        out_specs=[HBM_SPEC] * len(arrays),
        out_shape=[pltpu.HBM(a.shape, a.dtype) for a in arrays],
        input_output_aliases={i: i for i in range(len(arrays))},
        compiler_params=pltpu.CompilerParams(has_side_effects=pltpu.SideEffectType.DATAFLOW_SIDE_EFFECTING),
    )(*arrays, send_sems, recv_sems, after)
    return out[:n], out[n:]


def _pack(arrays, row_multiple):
    flat = jnp.concatenate([a.reshape(-1) for a in arrays])
    quantum = row_multiple * FLAT_LANES
    padded = -(-flat.shape[0] // quantum) * quantum
    return jnp.pad(flat, (0, padded - flat.shape[0])).reshape(-1, FLAT_LANES)


def _unpack(flat, like):
    flat = flat.reshape(-1)
    out, at = [], 0
    for a in like:
        size = math.prod(a.shape)
        out.append(flat[at:at + size].reshape(a.shape))
        at += size
    return out


def kernel(x, a_norm, a_w_in, a_sgu_norm, a_w_spatial, a_b_spatial, a_w_out, kv_norm, w_kv, b_norm, b_w_q, b_rel_bias, b_w_o, ffn_norm, ffn_w_gate_up, ffn_w_down, final_norm, loss_target, m_a_norm, m_a_w_in, m_a_sgu_norm, m_a_w_spatial, m_a_b_spatial, m_a_w_out, m_kv_norm, m_w_kv, m_b_norm, m_b_w_q, m_b_rel_bias, m_b_w_o, m_ffn_norm, m_ffn_w_gate_up, m_ffn_w_down, m_final_norm, v_a_norm, v_a_w_in, v_a_sgu_norm, v_a_w_spatial, v_a_b_spatial, v_a_w_out, v_kv_norm, v_w_kv, v_b_norm, v_b_w_q, v_b_rel_bias, v_b_w_o, v_ffn_norm, v_ffn_w_gate_up, v_ffn_w_down, v_final_norm):
    xs = x[0]
    target = loss_target[0]
    t, d = xs.shape
    n_a = a_w_in.shape[0]
    n_b = b_w_q.shape[0]
    depth = ffn_w_gate_up.shape[0]
    f_a = a_w_out.shape[1] * N_DEV
    gd = f_a // A_GROUPS
    nb_ffn = ffn_w_gate_up.shape[2]
    me = _linear(_position())

    small_rows = -(-(a_norm.size + a_sgu_norm.size) // (8 * 128)) * 8
    small = jnp.pad(jnp.concatenate([a_norm.reshape(-1), a_sgu_norm.reshape(-1)]),
                    (0, small_rows * 128 - a_norm.size - a_sgu_norm.size)).reshape(1, small_rows, 128)

    def shard(w, layer=None):
        return (w if layer is None else w[layer]).astype(BF16)

    stages = []
    for layer in range(depth):
        if layer == 0:
            stages += [("a0", [shard(a_w_in, 0)]), ("a0_out", [shard(a_w_out, 0)])]
        elif layer < n_a:
            stages.append((f"a{layer}", [shard(a_w_in, layer), shard(a_w_out, layer)]))
        else:
            i = layer - n_a
            shared = [shard(w_kv)] if i == 0 else []
            stages.append((f"b{i}", shared + [shard(b_w_q, i), shard(b_w_o, i)]))
        stages.append((f"f{layer}", [shard(ffn_w_gate_up, layer), shard(ffn_w_down, layer)]))
    first = _all_gather([s[None] for s in stages[0][1]] + [small], "gather_first")
    gathered = {stages[0][0]: [g[0] for g in first[:-1]]}
    small_g = first[-1].reshape(N_DEV, -1)
    a_norm_full = small_g[:, :a_norm.size].reshape(N_DEV, n_a, -1).transpose(1, 0, 2).reshape(n_a, d)
    a_sgu_full = small_g[:, a_norm.size:a_norm.size + a_sgu_norm.size].reshape(
        N_DEV, n_a, -1).transpose(1, 0, 2).reshape(n_a, f_a)
    in_flight = {}
    for key, shards in stages[1:]:
        lands = [lax.dynamic_update_slice(lax.empty((N_DEV,) + s.shape, BF16), s[None], (me, 0, 0)) for s in shards]
        send, recv, srcs, lands, a_norm_full = _split_start(shards, lands, a_norm_full, f"gather_start_{key}")
        in_flight[key] = (send, recv, srcs, lands)

    def weights(key, after):
        if key not in gathered:
            _, gathered[key] = _split_wait(*in_flight.pop(key), after, f"gather_wait_{key}")
        return gathered[key]

    rows_down = ffn_w_down.shape[1]

    def mixer_a_weights(i, after):
        if i == 0:
            (w_in,), (w_out,) = weights("a0", after[0]), weights("a0_out", after[1])
        else:
            w_in, w_out = weights(f"a{i}", after[0])
        return w_in[None], w_out.reshape(1, f_a, d)

    def mixer_b_weights(i, after):
        ws = weights(f"b{i}", after)
        return ws[-2].reshape(1, d, d), ws[-1].reshape(1, d, d)

    def ffn_weights(layer, after):
        w_gu, w_dn = weights(f"f{layer}", after)
        return w_gu[None], w_dn.reshape(1, N_DEV // 2, 2 * rows_down, d)

    w_sp_t = jnp.swapaxes(a_w_spatial, -1, -2)
    b_full = jnp.repeat(jnp.swapaxes(a_b_spatial, -1, -2), gd, axis=-1)

    saved = []

    def ffn_fwd(xin, layer):
        hf = _rms_fwd(xin, ffn_norm[layer], f"ffn_norm_fwd_{layer}")
        w_gu, w_dn = ffn_weights(layer, xin)
        gu, act = _ffn_gate_up(f"ffn_gate_up_{layer}", hf, w_gu, 0)
        xout = _mm_down(f"ffn_down_{layer}", act, w_dn, 0, xin)
        return xout, (xin, hf, gu, act)

    for i in range(n_a):
        h = _rms_fwd(xs, a_norm_full[i], f"a_norm_fwd_{i}")
        zpre = _mm_colblock(f"a_in_{i}", h, weights(f"a{i}", xs)[0][None], 0)
        p = _sgu_fwd(zpre, a_sgu_full[i], a_w_spatial[i], b_full[i], f"a_sgu_fwd_{i}")
        w_in, w_out = mixer_a_weights(i, (xs, p))
        x_mid = _mm_natural(f"a_out_{i}", p, w_out, 0, res=xs)
        x_out, ffn_saved = ffn_fwd(x_mid, i)
        saved.append((xs, h, zpre, p, ffn_saved))
        xs = x_out

    x_kv = xs
    w_kv_g = weights("b0", x_kv)[0][None]
    h_kv = _rms_fwd(x_kv, kv_norm, "kv_norm_fwd")
    kv = _mm_colblock("kv_proj", h_kv, w_kv_g, 0)
    kvpad = jnp.pad(kv, ((LEFT, 0), (0, 0)))

    biases = [_bias_block(_bias_build(b_rel_bias[i], f"rel_bias_{i}")) for i in range(n_b)]
    for i in range(n_b):
        layer = n_a + i
        w_q, w_o = mixer_b_weights(i, xs)
        hb = _rms_fwd(xs, b_norm[i], f"b_norm_fwd_{i}")
        q = _mm_natural(f"b_q_{i}", hb, w_q, 0, out_dtype=BF16, scale=ATTN_SCALE)
        o = _attn_fwd(q, kvpad, biases[i], f"b_attn_fwd_{i}")
        x_mid = _mm_natural(f"b_o_{i}", o, w_o, 0, res=xs)
        x_out, ffn_saved = ffn_fwd(x_mid, layer)
        saved.append((xs, hb, q, o, ffn_saved))
        xs = x_out

    dx, loss_local, g_final = _loss_head(xs, final_norm, target, "loss_head")
    loss = lax.psum(loss_local, ("x", "y", "c"))

    big_grads = {}
    pending = []
    in_flight_grads = []

    def start_exchange(dx, tag):
        srcs = [big_grads[key] for key in pending]
        lands = [lax.empty(s.shape, BF16) for s in srcs]
        send, recv, srcs, lands, dx = _split_start(srcs, lands, dx, f"exchange_start_{tag}", exchange=True)
        in_flight_grads.append((list(pending), send, recv, srcs, lands, tag))
        pending.clear()
        return dx

    g_ffn_norm = [None] * depth
    g_a_norm = [None] * n_a
    g_a_sgu = [None] * n_a
    g_w_sp = [None] * n_a
    g_b_sp = [None] * n_a
    g_b_norm = [None] * n_b
    g_rel = [None] * n_b

    def ffn_bwd(dx, layer, ffn_saved):
        xin, hf, gu, act = ffn_saved
        big_grads["ffn_w_down", layer] = _mm_dw_down(f"ffn_down_dw_{layer}", act, dx)
        w_gu, w_dn = ffn_weights(layer, xin)
        dgu = _ffn_down_dx(f"ffn_down_dx_{layer}", dx, w_dn, 0, gu).reshape(N_DEV, t, nb_ffn)
        big_grads["ffn_w_gate_up", layer] = _mm_dw_colblock(
            f"ffn_gate_up_dw_{layer}", hf, dgu, blocked_in=True, transposed=True)
        pending.extend([("ffn_w_gate_up", layer), ("ffn_w_down", layer)])
        dx, g_ffn_norm[layer] = _mm_t_colblock_norm_bwd(
            f"ffn_gate_up_dx_{layer}", dgu, w_gu, 0, xin, ffn_norm[layer], dx, blocked_in=True)
        return dx

    dk = dv = None
    for i in reversed(range(n_b)):
        layer = n_a + i
        x_in, hb, q, o, ffn_saved = saved[layer]
        dx = ffn_bwd(dx, layer, ffn_saved)
        big_grads["b_w_o", i] = _mm_dw_natural(f"b_o_dw_{i}", o, dx)
        w_q, w_o = mixer_b_weights(i, x_in)
        do = _mm_t_natural(f"b_o_dx_{i}", dx, w_o, 0)
        dq, dk, dv, dbias = _attn_bwd(q, kvpad, biases[i], do, dk, dv, f"b_attn_bwd_{i}")
        g_rel[i] = _bias_grad(dbias, f"rel_bias_grad_{i}")
        big_grads["b_w_q", i] = _mm_dw_natural(f"b_q_dw_{i}", hb, dq)
        pending.extend([("b_w_o", i), ("b_w_q", i)])
        dh = _mm_t_natural(f"b_q_dx_{i}", dq, w_q, 0)
        dx, g_b_norm[i] = _rms_bwd(x_in, b_norm[i], dh, dx, f"b_norm_bwd_{i}")
        if i > 0:
            dx = start_exchange(dx, f"b{i}")

    dkv = jnp.concatenate([dk[LEFT:], dv[LEFT:]], axis=1).astype(BF16)
    big_grads["w_kv", 0] = _mm_dw_colblock("kv_proj_dw", h_kv, dkv)
    pending.append(("w_kv", 0))
    dx, g_kv_norm = _mm_t_colblock_norm_bwd("kv_proj_dx", dkv, w_kv_g, 0, x_kv, kv_norm, dx)
    dx = start_exchange(dx, "kv")

    for i in reversed(range(n_a)):
        x_in, h, zpre, p, ffn_saved = saved[i]
        dx = ffn_bwd(dx, i, ffn_saved)
        if i == 0:
            dx = start_exchange(dx, "f0")
        big_grads["a_w_out", i] = _mm_dw_natural(f"a_out_dw_{i}", p, dx)
        pending.append(("a_w_out", i))
        if i == 0:
            dx = start_exchange(dx, "a0_out")
        w_in, w_out = mixer_a_weights(i, (x_in, p))
        dp = _mm_t_natural(f"a_out_dx_{i}", dx, w_out, 0)
        dz, g_w_sp[i], g_b_sp[i], g_a_sgu[i] = _sgu_bwd(
            zpre, dp, a_sgu_full[i], a_w_spatial[i], w_sp_t[i], b_full[i], f"a_sgu_bwd_{i}")
        big_grads["a_w_in", i] = _mm_dw_colblock(f"a_in_dw_{i}", h, dz)
        pending.append(("a_w_in", i))
        if i == 0:
            dx = start_exchange(dx, "a0_in")
        dx, g_a_norm[i] = _mm_t_colblock_norm_bwd(f"a_in_dx_{i}", dz, w_in, 0, x_in, a_norm_full[i], dx)
        if i > 0:
            dx = start_exchange(dx, f"a{i}")
    grad_x = dx[None]

    small_like = [jax.ShapeDtypeStruct((n_a, d), F32), jax.ShapeDtypeStruct((n_a, f_a), F32),
                  a_w_spatial, a_b_spatial, kv_norm, b_norm, b_rel_bias, ffn_norm, final_norm]
    small_partial = _pack(
        [jnp.stack(g_a_norm), jnp.stack(g_a_sgu), jnp.stack(g_w_sp), jnp.stack(g_b_sp), g_kv_norm,
         jnp.stack(g_b_norm), jnp.stack(g_rel), jnp.stack(g_ffn_norm), g_final], N_DEV * 8)
    chunk_rows = small_partial.shape[0] // N_DEV
    arrived = {}
    for keys, send, recv, srcs, lands, tag in in_flight_grads:
        srcs, lands = _split_wait(send, recv, srcs, lands, dx, f"exchange_wait_{tag}", exchange=True)
        for key, src, land in zip(keys, srcs, lands):
            arrived[key] = (land, src)
    small_got = _exchange([small_partial.reshape(1, N_DEV, chunk_rows, FLAT_LANES)], "exchange_small")[0]
    small_sum = _ordered_sum(small_got[0], "small_grad_sum")
    small_all = _all_gather([small_sum[None]], "gather_small_grads")[0]
    (ga_norm, ga_sgu, gw_sp, gb_sp, gkv_norm, gb_norm, g_relb, gffn_norm, gfinal) = _unpack(small_all, small_like)

    results = {}
    big_names = ["a_w_in", "a_w_out", "w_kv", "b_w_q", "b_w_o", "ffn_w_gate_up", "ffn_w_down"]
    big_wmv = [(a_w_in, m_a_w_in, v_a_w_in), (a_w_out, m_a_w_out, v_a_w_out),
               (w_kv[None], m_w_kv[None], v_w_kv[None]), (b_w_q, m_b_w_q, v_b_w_q), (b_w_o, m_b_w_o, v_b_w_o),
               tuple(jnp.swapaxes(a, 1, 2) for a in (ffn_w_gate_up, m_ffn_w_gate_up, v_ffn_w_gate_up)),
               (ffn_w_down, m_ffn_w_down, v_ffn_w_down)]
    me_arr = jnp.reshape(me, (1,)).astype(jnp.int32)
    for name, (w, m, v) in zip(big_names, big_wmv):
        outs = None
        for layer in range(w.shape[0]):
            got, own = arrived[name, layer]
            outs = _adamw_layer(got, own, w, m, v, layer, outs, me_arr, f"adamw_{name}_{layer}")
        if name == "w_kv":
            outs = [o[0] for o in outs]
        if name == "ffn_w_gate_up":
            outs = [jnp.swapaxes(o, 1, 2) for o in outs]
        results[name] = outs

    n_cols = a_norm.shape[1]
    s_cols = a_sgu_norm.shape[1]
    small_g_list = [lax.dynamic_slice(ga_norm, (0, me * n_cols), (n_a, n_cols)),
                    lax.dynamic_slice(ga_sgu, (0, me * s_cols), (n_a, s_cols)),
                    gw_sp, gb_sp, gkv_norm, gb_norm, g_relb, gffn_norm, gfinal]
    small_names = ["a_norm", "a_sgu_norm", "a_w_spatial", "a_b_spatial", "kv_norm", "b_norm", "b_rel_bias",
                   "ffn_norm", "final_norm"]
    small_w = [a_norm, a_sgu_norm, a_w_spatial, a_b_spatial, kv_norm, b_norm, b_rel_bias, ffn_norm, final_norm]
    small_m = [m_a_norm, m_a_sgu_norm, m_a_w_spatial, m_a_b_spatial, m_kv_norm, m_b_norm, m_b_rel_bias,
               m_ffn_norm, m_final_norm]
    small_v = [v_a_norm, v_a_sgu_norm, v_a_w_spatial, v_a_b_spatial, v_kv_norm, v_b_norm, v_b_rel_bias,
               v_ffn_norm, v_final_norm]
    flat_g = _pack(small_g_list, 8)
    flat_out = _adamw(flat_g[None, None], _pack(small_w, 8)[None], _pack(small_m, 8)[None],
                      _pack(small_v, 8)[None], "adamw_small")
    unpacked = [_unpack(o[0], small_w) for o in flat_out]
    for idx, name in enumerate(small_names):
        results[name] = [unpacked[kind][idx] for kind in range(4)]

    order = ["a_norm", "a_w_in", "a_sgu_norm", "a_w_spatial", "a_b_spatial", "a_w_out", "kv_norm", "w_kv",
             "b_norm", "b_w_q", "b_rel_bias", "b_w_o", "ffn_norm", "ffn_w_gate_up", "ffn_w_down", "final_norm"]
    outputs = [loss, grad_x]
    for kind in range(4):
        outputs += [results[name][kind] for name in order]
    return tuple(outputs)
```

```python
import math

import jax
import jax.numpy as jnp
from jax import lax
from jax.experimental import pallas as pl
from jax.experimental.pallas import tpu as pltpu

F32 = jnp.float32
BF16 = jnp.bfloat16
MESH = pl.DeviceIdType.MESH
HBM_SPEC = pl.BlockSpec(memory_space=pltpu.HBM)
SEM_SPEC = pl.BlockSpec(memory_space=pltpu.SEMAPHORE)

N_DEV = 8
CHUNK = 64
A_CHUNK = 128
A_GROUPS = 8
N_LEFT_CHUNKS = 8
LEFT = N_LEFT_CHUNKS * CHUNK
PAIR_ROWS = 2 * CHUNK
PAIR_BAND = PAIR_ROWS + LEFT
DIAGONALS = PAIR_BAND + PAIR_ROWS
PAIRS_PER_BLOCK = 2
Q_BLOCK = PAIRS_PER_BLOCK * PAIR_ROWS
K_BLOCK = Q_BLOCK + LEFT
ATTN_UNROLL = 2
MAX_REL = 256
N_REL = 2 * MAX_REL + 1
REL_PAD = 640
HEAD_DIM = 64
HEAD_PAIR = 2 * HEAD_DIM
ATTN_SCALE = HEAD_DIM ** -0.5
EPS = 1e-6
NEG_INF = -1e30
ADAM_LR = 0.001
ADAM_B1 = 0.9
ADAM_B2 = 0.999
ADAM_EPS = 1e-08
ADAM_WD = 0.01
ADAM_STEP = 10
FLAT_LANES = 1024
V7X_VMEM_BYTES = 64 * 1024 * 1024
VMEM_FLOOR_BYTES = 32 * 1024 * 1024
VMEM_CEIL_BYTES = V7X_VMEM_BYTES - 8 * 1024 * 1024

NN = (((1,), (0,)), ((), ()))
NT = (((1,), (1,)), ((), ()))
TN = (((0,), (0,)), ((), ()))


def _tile(n, pref):
    return pref if n % pref == 0 else n


def _row_tile(n, pref, mult):
    best = None
    for t in range(mult, min(n, pref) + 1, mult):
        if n % t == 0:
            best = t
    return best if best is not None else n


def _nbytes(shape, dtype):
    n = 1
    for s in shape:
        if s is not None:
            n *= s
    return n * jnp.dtype(dtype).itemsize


def _call(body, name, grid, in_specs, out_specs, out_shape, scratch=(), vmem_bytes=0, aliases=None):
    limit = int(min(max(VMEM_FLOOR_BYTES, vmem_bytes * 5 // 4), VMEM_CEIL_BYTES))
    call = pl.pallas_call(
        body,
        name=name,
        grid=grid,
        in_specs=in_specs,
        out_specs=out_specs,
        out_shape=out_shape,
        scratch_shapes=list(scratch),
        input_output_aliases=aliases or {},
        compiler_params=pltpu.CompilerParams(
            dimension_semantics=("arbitrary",) * len(grid), vmem_limit_bytes=limit),
    )
    return lambda *operands: call(*[pltpu.with_memory_space_constraint(a, pltpu.HBM) for a in operands])


def _erf_parts(x):
    ax = jnp.abs(x) * (1.0 / math.sqrt(2.0))
    t = 1.0 / (1.0 + 0.3275911 * ax)
    poly = ((((1.061405429 * t - 1.453152027) * t + 1.421413741) * t - 0.284496736) * t + 0.254829592) * t
    ex = jnp.exp(-ax * ax)
    erf_abs = 1.0 - poly * ex
    return jnp.where(x < 0, -erf_abs, erf_abs), ex


def _gelu_and_grad(x):
    erf, ex = _erf_parts(x)
    cdf = 0.5 * (1.0 + erf)
    return x * cdf, cdf + x * ex * (1.0 / math.sqrt(2.0 * math.pi))


def _gelu(x):
    erf, _ = _erf_parts(x)
    return x * (0.5 * (1.0 + erf))


def _sigmoid(x):
    return 1.0 / (1.0 + jnp.exp(-x))


def _split3(x):
    hi = x.astype(BF16)
    r1 = x - hi.astype(F32)
    mid = r1.astype(BF16)
    lo = (r1 - mid.astype(F32)).astype(BF16)
    return hi, mid, lo


def _rms_fwd(x, g, name):
    t, d = x.shape
    tm = _tile(t, 512)

    def body(x_ref, g_ref, o_ref):
        xf = x_ref[...]
        r = lax.rsqrt(jnp.mean(xf * xf, axis=-1, keepdims=True) + EPS)
        o_ref[...] = (xf * r * g_ref[...]).astype(o_ref.dtype)

    return _call(
        body, name, (t // tm,),
        [pl.BlockSpec((tm, d), lambda i: (i, 0)), pl.BlockSpec((1, d), lambda i: (0, 0))],
        pl.BlockSpec((tm, d), lambda i: (i, 0)),
        jax.ShapeDtypeStruct((t, d), BF16),
        vmem_bytes=2 * (_nbytes((tm, d), F32) + _nbytes((tm, d), BF16)) + 4 * _nbytes((tm, d), F32),
    )(x, g.reshape(1, d))


def _rms_bwd(x, g, dh, dx_up, name):
    t, d = x.shape
    tm = _tile(t, 512)

    def body(x_ref, g_ref, dh_ref, up_ref, dx_ref, dg_ref):
        @pl.when(pl.program_id(0) == 0)
        def _():
            dg_ref[...] = jnp.zeros_like(dg_ref)

        xf = x_ref[...]
        r = lax.rsqrt(jnp.mean(xf * xf, axis=-1, keepdims=True) + EPS)
        xhat = xf * r
        dy = dh_ref[...].astype(F32)
        dxhat = dy * g_ref[...]
        dg_ref[...] += jnp.sum(dy * xhat, axis=0, keepdims=True)
        dx = r * (dxhat - xhat * jnp.mean(dxhat * xhat, axis=-1, keepdims=True))
        dx_ref[...] = up_ref[...] + dx

    row = pl.BlockSpec((tm, d), lambda i: (i, 0))
    vec = pl.BlockSpec((1, d), lambda i: (0, 0))
    dx, dg = _call(
        body, name, (t // tm,),
        [row, vec, row, row],
        [row, vec],
        [jax.ShapeDtypeStruct((t, d), F32), jax.ShapeDtypeStruct((1, d), F32)],
        vmem_bytes=10 * _nbytes((tm, d), F32),
    )(x, g.reshape(1, d), dh, dx_up)
    return dx, dg.reshape(d)


def _mm(name, dims, a, b, *, grid, a_spec, b_spec, out_shape, out_spec, acc_shape,
        res=None, res_spec=None, scale=None):
    nk = grid[2]
    has_res = res is not None

    def body(*refs):
        refs = list(refs)
        a_ref = refs.pop(0)
        b_ref = refs.pop(0)
        r_ref = refs.pop(0) if has_res else None
        o_ref = refs.pop(0)
        part = lax.dot_general(a_ref[...].astype(BF16), b_ref[...].astype(BF16), dims,
                               preferred_element_type=F32)

        def finish(acc):
            if scale is not None:
                acc = acc * scale
            if has_res:
                acc = acc + r_ref[...]
            o_ref[...] = acc.astype(o_ref.dtype)

        if nk == 1:
            finish(part)
        else:
            acc_ref = refs.pop(0)
            k = pl.program_id(2)

            @pl.when(k == 0)
            def _():
                acc_ref[...] = part

            @pl.when(k > 0)
            def _():
                acc_ref[...] += part

            @pl.when(k == nk - 1)
            def _():
                finish(acc_ref[...])

    operands = [a, b]
    in_specs = [a_spec, b_spec]
    vmem = 2 * (_nbytes(a_spec.block_shape, a.dtype) + _nbytes(b_spec.block_shape, b.dtype)
                + _nbytes(out_spec.block_shape, out_shape.dtype))
    vmem += 3 * _nbytes(acc_shape, F32)
    if has_res:
        operands.append(res)
        in_specs.append(res_spec)
        vmem += 2 * _nbytes(res_spec.block_shape, res.dtype)
    scratch = [pltpu.VMEM(acc_shape, F32)] if nk > 1 else []
    return _call(body, name, grid, in_specs, out_spec, out_shape, scratch=scratch, vmem_bytes=vmem)(*operands)


def _mm_colblock(name, h, w_g, layer):
    t, k = h.shape
    nb = w_g.shape[3]
    tm = _tile(t, 2048)
    return _mm(
        name, NN, h, w_g, grid=(t // tm, N_DEV, 1),
        a_spec=pl.BlockSpec((tm, k), lambda i, j, kk: (i, 0)),
        b_spec=pl.BlockSpec((None, None, k, nb), lambda i, j, kk: (layer, j, 0, 0)),
        out_shape=jax.ShapeDtypeStruct((t, N_DEV * nb), BF16),
        out_spec=pl.BlockSpec((tm, nb), lambda i, j, kk: (i, j)), acc_shape=(tm, nb))


def _mm_natural(name, a, w, layer, *, res=None, out_dtype=F32, scale=None):
    t, k = a.shape
    n = w.shape[2]
    tm = _tile(t, 1024)
    tn = _tile(n, 512)
    res_spec = None if res is None else pl.BlockSpec((tm, tn), lambda i, j, kk: (i, j))
    return _mm(
        name, NN, a, w, grid=(t // tm, n // tn, 1),
        a_spec=pl.BlockSpec((tm, k), lambda i, j, kk: (i, 0)),
        b_spec=pl.BlockSpec((None, k, tn), lambda i, j, kk: (layer, 0, j)),
        out_shape=jax.ShapeDtypeStruct((t, n), out_dtype),
        out_spec=pl.BlockSpec((tm, tn), lambda i, j, kk: (i, j)),
        acc_shape=(tm, tn), res=res, res_spec=res_spec, scale=scale)


def _mm_down(name, act, w4, layer, res):
    nblk, t, kb = act.shape
    n = w4.shape[3]
    tm = _tile(t, 1024)
    tn = _tile(n, 1024)
    return _mm(
        name, NN, act, w4, grid=(t // tm, n // tn, nblk),
        a_spec=pl.BlockSpec((None, tm, kb), lambda i, j, kk: (kk, i, 0)),
        b_spec=pl.BlockSpec((None, None, kb, tn), lambda i, j, kk: (layer, kk, 0, j)),
        out_shape=jax.ShapeDtypeStruct((t, n), F32),
        out_spec=pl.BlockSpec((tm, tn), lambda i, j, kk: (i, j)),
        acc_shape=(tm, tn), res=res, res_spec=pl.BlockSpec((tm, tn), lambda i, j, kk: (i, j)))


def _mm_t_colblock_norm_bwd(name, dz, w_g, layer, x, g, dx_up, blocked_in=False):
    k = w_g.shape[2]
    nb = w_g.shape[3]
    t = x.shape[0]
    tm = _tile(t, 1024)
    per_step = 2
    n_steps = N_DEV // per_step
    if blocked_in:
        a_spec = pl.BlockSpec((per_step, tm, nb), lambda i, kk: (kk, i, 0))
    else:
        a_spec = pl.BlockSpec((tm, per_step * nb), lambda i, kk: (i, kk))

    def body(a_ref, b_ref, x_ref, g_ref, up_ref, dx_ref, dg_ref, acc_ref):
        i = pl.program_id(0)
        kk = pl.program_id(1)
        part = None
        for u in range(per_step):
            a = a_ref[u] if blocked_in else a_ref[:, u * nb:(u + 1) * nb]
            term = lax.dot_general(a.astype(BF16), b_ref[u].astype(BF16), NT, preferred_element_type=F32)
            part = term if part is None else part + term

        @pl.when(kk == 0)
        def _():
            acc_ref[...] = part

        @pl.when(kk > 0)
        def _():
            acc_ref[...] += part

        @pl.when((i == 0) & (kk == 0))
        def _():
            dg_ref[...] = jnp.zeros_like(dg_ref)

        @pl.when(kk == n_steps - 1)
        def _():
            dy = acc_ref[...]
            xf = x_ref[...]
            r = lax.rsqrt(jnp.mean(xf * xf, axis=-1, keepdims=True) + EPS)
            xhat = xf * r
            dxhat = dy * g_ref[...]
            dg_ref[...] += jnp.sum(dy * xhat, axis=0, keepdims=True)
            dx_ref[...] = up_ref[...] + r * (dxhat - xhat * jnp.mean(dxhat * xhat, axis=-1, keepdims=True))

    row = pl.BlockSpec((tm, k), lambda i, kk: (i, 0))
    vec = pl.BlockSpec((1, k), lambda i, kk: (0, 0))
    dx, dg = _call(
        body, name, (t // tm, n_steps),
        [a_spec, pl.BlockSpec((None, per_step, k, nb), lambda i, kk: (layer, kk, 0, 0)), row, vec, row],
        [row, vec],
        [jax.ShapeDtypeStruct((t, k), F32), jax.ShapeDtypeStruct((1, k), F32)],
        scratch=[pltpu.VMEM((tm, k), F32)],
        vmem_bytes=2 * per_step * (_nbytes((tm, nb), BF16) + _nbytes((k, nb), BF16)) + 10 * _nbytes((tm, k), F32),
    )(dz, w_g, x, g.reshape(1, k), dx_up)
    return dx, dg.reshape(k)


def _ffn_gate_up(name, h, w_g, layer):
    t, k = h.shape
    nb = w_g.shape[3]
    half = N_DEV // 2
    tm = _tile(t, 1024)

    def body(h_ref, wg_ref, wu_ref, gu_ref, act_ref):
        hb = h_ref[...]
        gate = jnp.dot(hb, wg_ref[...], preferred_element_type=F32)
        up = jnp.dot(hb, wu_ref[...], preferred_element_type=F32)
        gu_ref[0] = gate.astype(BF16)
        gu_ref[1] = up.astype(BF16)
        act_ref[...] = (gate * _sigmoid(gate) * up).astype(BF16)

    return _call(
        body, name, (t // tm, half),
        [pl.BlockSpec((tm, k), lambda i, j: (i, 0)),
         pl.BlockSpec((None, None, k, nb), lambda i, j: (layer, j, 0, 0)),
         pl.BlockSpec((None, None, k, nb), lambda i, j: (layer, half + j, 0, 0))],
        [pl.BlockSpec((2, None, tm, nb), lambda i, j: (0, j, i, 0)),
         pl.BlockSpec((None, tm, nb), lambda i, j: (j, i, 0))],
        [jax.ShapeDtypeStruct((2, half, t, nb), BF16), jax.ShapeDtypeStruct((half, t, nb), BF16)],
        vmem_bytes=2 * (_nbytes((tm, k), BF16) + 2 * _nbytes((k, nb), BF16) + 3 * _nbytes((tm, nb), BF16))
        + 6 * _nbytes((tm, nb), F32),
    )(h, w_g, w_g)


def _ffn_down_dx(name, dy, w4, layer, gu4):
    t, n = dy.shape
    nblk, kb = w4.shape[1], w4.shape[2]
    tm = _tile(t, 1024)

    def body(dy_ref, w_ref, gu_ref, dgu_ref):
        da = lax.dot_general(dy_ref[...].astype(BF16), w_ref[...], NT, preferred_element_type=F32)
        gate = gu_ref[0].astype(F32)
        up = gu_ref[1].astype(F32)
        sig = _sigmoid(gate)
        dgu_ref[0] = (da * up * (sig * (1.0 + gate * (1.0 - sig)))).astype(BF16)
        dgu_ref[1] = (da * (gate * sig)).astype(BF16)

    blk = pl.BlockSpec((2, None, tm, kb), lambda i, j: (0, j, i, 0))
    return _call(
        body, name, (t // tm, nblk),
        [pl.BlockSpec((tm, n), lambda i, j: (i, 0)),
         pl.BlockSpec((None, None, kb, n), lambda i, j: (layer, j, 0, 0)),
         blk],
        blk,
        jax.ShapeDtypeStruct((2, nblk, t, kb), BF16),
        vmem_bytes=2 * (_nbytes((tm, n), F32) + _nbytes((kb, n), BF16) + 4 * _nbytes((tm, kb), BF16))
        + 8 * _nbytes((tm, kb), F32),
    )(dy, w4, gu4)


def _mm_t_natural(name, dy, w, layer):
    t, n = dy.shape
    k = w.shape[1]
    tm = _tile(t, 1024)
    tk = _tile(k, 512)
    return _mm(
        name, NT, dy, w, grid=(t // tm, k // tk, 1),
        a_spec=pl.BlockSpec((tm, n), lambda i, j, kk: (i, 0)),
        b_spec=pl.BlockSpec((None, tk, n), lambda i, j, kk: (layer, j, 0)),
        out_shape=jax.ShapeDtypeStruct((t, k), BF16),
        out_spec=pl.BlockSpec((tm, tk), lambda i, j, kk: (i, j)),
        acc_shape=(tm, tk))


def _mm_dw_colblock(name, h, dz, blocked_in=False, transposed=False):
    t, k = h.shape
    nb = dz.shape[2] if blocked_in else dz.shape[1] // N_DEV
    tk = _tile(t, 1024)
    h_spec = pl.BlockSpec((tk, k), lambda i, j, kk: (kk, 0))
    if blocked_in:
        dz_spec = pl.BlockSpec((None, tk, nb), lambda i, j, kk: (j, kk, 0))
    else:
        dz_spec = pl.BlockSpec((tk, nb), lambda i, j, kk: (kk, j))
    rows, cols = (nb, k) if transposed else (k, nb)
    return _mm(
        name, TN, *((dz, h) if transposed else (h, dz)), grid=(1, N_DEV, t // tk),
        a_spec=dz_spec if transposed else h_spec,
        b_spec=h_spec if transposed else dz_spec,
        out_shape=jax.ShapeDtypeStruct((N_DEV, rows, cols), BF16),
        out_spec=pl.BlockSpec((None, rows, cols), lambda i, j, kk: (j, 0, 0)),
        acc_shape=(rows, cols))


def _mm_dw_natural(name, a, dy):
    t, k = a.shape
    n = dy.shape[1]
    tko = _tile(k, 1024)
    tt = _tile(t, 1024)
    out = _mm(
        name, TN, a, dy, grid=(k // tko, 1, t // tt),
        a_spec=pl.BlockSpec((tt, tko), lambda i, j, kk: (kk, i)),
        b_spec=pl.BlockSpec((tt, n), lambda i, j, kk: (kk, 0)),
        out_shape=jax.ShapeDtypeStruct((k, n), BF16),
        out_spec=pl.BlockSpec((tko, n), lambda i, j, kk: (i, 0)),
        acc_shape=(tko, n))
    return out.reshape(N_DEV, k // N_DEV, n)


def _mm_dw_down(name, act, dy):
    nblk, t, kb = act.shape
    n = dy.shape[1]
    tt = _tile(t, 1024)
    out = _mm(
        name, TN, act, dy, grid=(nblk, 1, t // tt),
        a_spec=pl.BlockSpec((None, tt, kb), lambda i, j, kk: (i, kk, 0)),
        b_spec=pl.BlockSpec((tt, n), lambda i, j, kk: (kk, 0)),
        out_shape=jax.ShapeDtypeStruct((nblk, kb, n), BF16),
        out_spec=pl.BlockSpec((None, kb, n), lambda i, j, kk: (i, 0, 0)),
        acc_shape=(kb, n))
    return out.reshape(N_DEV, (nblk * kb) // N_DEV, n)


def _spatial_mask(transposed=False):
    r = lax.broadcasted_iota(jnp.int32, (A_CHUNK, A_CHUNK), 0) // CHUNK
    c = lax.broadcasted_iota(jnp.int32, (A_CHUNK, A_CHUNK), 1) // CHUNK
    return c >= r if transposed else r >= c


def _sgu_tile(t):
    return _tile(t, 2 * A_CHUNK)


def _sgu_fwd(zpre, g_sgu, w_sp, b_full, name):
    t, f2 = zpre.shape
    f = f2 // 2
    gd = f // A_GROUPS
    tm = _sgu_tile(t)

    def body(z_ref, g_ref, w_ref, b_ref, p_ref):
        mask = _spatial_mask()
        wm = [jnp.where(mask, w_ref[g], 0.0).astype(BF16) for g in range(A_GROUPS)]
        for c in range(tm // A_CHUNK):
            rows = pl.ds(c * A_CHUNK, A_CHUNK)
            z = _gelu(z_ref[rows, :].astype(F32))
            u = z[:, :f]
            v0 = z[:, f:]
            r = lax.rsqrt(jnp.mean(v0 * v0, axis=-1, keepdims=True) + EPS)
            v1 = (v0 * r * g_ref[...]).astype(BF16)
            for g in range(A_GROUPS):
                cols = slice(g * gd, (g + 1) * gd)
                v2 = jnp.dot(wm[g], v1[:, cols], preferred_element_type=F32) + b_ref[:, cols]
                p_ref[rows, cols] = (u[:, cols] * v2).astype(BF16)

    return _call(
        body, name, (t // tm,),
        [pl.BlockSpec((tm, f2), lambda i: (i, 0)),
         pl.BlockSpec((1, f), lambda i: (0, 0)),
         pl.BlockSpec((A_GROUPS, A_CHUNK, A_CHUNK), lambda i: (0, 0, 0)),
         pl.BlockSpec((A_CHUNK, f), lambda i: (0, 0))],
        pl.BlockSpec((tm, f), lambda i: (i, 0)),
        jax.ShapeDtypeStruct((t, f), BF16),
        vmem_bytes=2 * _nbytes((tm, f2), BF16) + 2 * _nbytes((tm, f), BF16) + 8 * _nbytes((A_CHUNK, f2), F32),
    )(zpre, g_sgu.reshape(1, f), w_sp, b_full)


def _sgu_bwd(zpre, dp, g_sgu, w_sp, w_sp_t, b_full, name):
    t, f2 = zpre.shape
    f = f2 // 2
    gd = f // A_GROUPS
    tm = _sgu_tile(t)
    n_steps = t // tm

    def body(z_ref, dp_ref, g_ref, w_ref, wt_ref, b_ref, dz_ref, dw_ref, db_ref, dg_ref, dv1_ref, dbf_ref):
        step = pl.program_id(0)

        @pl.when(step == 0)
        def _():
            dw_ref[...] = jnp.zeros_like(dw_ref)
            dg_ref[...] = jnp.zeros_like(dg_ref)
            dbf_ref[...] = jnp.zeros_like(dbf_ref)

        mask = _spatial_mask()
        mask_t = _spatial_mask(transposed=True)
        wm = [jnp.where(mask, w_ref[g], 0.0).astype(BF16) for g in range(A_GROUPS)]
        wmt = [jnp.where(mask_t, wt_ref[g], 0.0).astype(BF16) for g in range(A_GROUPS)]
        gain = g_ref[...]
        for c in range(tm // A_CHUNK):
            rows = pl.ds(c * A_CHUNK, A_CHUNK)
            z, dgelu = _gelu_and_grad(z_ref[rows, :].astype(F32))
            u = z[:, :f]
            v0 = z[:, f:]
            r = lax.rsqrt(jnp.mean(v0 * v0, axis=-1, keepdims=True) + EPS)
            xhat = v0 * r
            v1 = (xhat * gain).astype(BF16)
            dpf = dp_ref[rows, :].astype(F32)
            for g in range(A_GROUPS):
                cols = slice(g * gd, (g + 1) * gd)
                v1g = v1[:, cols]
                v2 = jnp.dot(wm[g], v1g, preferred_element_type=F32) + b_ref[:, cols]
                dpg = dpf[:, cols]
                dz_ref[rows, cols] = (dpg * v2 * dgelu[:, cols]).astype(BF16)
                dv2 = dpg * u[:, cols]
                dbf_ref[:, cols] += dv2
                dv2b = dv2.astype(BF16)
                dwg = lax.dot_general(dv2b, v1g, NT, preferred_element_type=F32)
                dw_ref[g] += jnp.where(mask, dwg, 0.0)
                dv1_ref[:, cols] = jnp.dot(wmt[g], dv2b, preferred_element_type=F32)
            dv1 = dv1_ref[...]
            dxhat = dv1 * gain
            dg_ref[...] += jnp.sum(dv1 * xhat, axis=0, keepdims=True)
            dv0 = r * (dxhat - xhat * jnp.mean(dxhat * xhat, axis=-1, keepdims=True))
            dz_ref[rows, pl.ds(f, f)] = (dv0 * dgelu[:, f:]).astype(BF16)

        @pl.when(step == n_steps - 1)
        def _():
            for g in range(A_GROUPS):
                db_ref[g] = jnp.sum(dbf_ref[:, g * gd:(g + 1) * gd], axis=1, keepdims=True)

    wspec = pl.BlockSpec((A_GROUPS, A_CHUNK, A_CHUNK), lambda i: (0, 0, 0))
    dz, dw, db, dg = _call(
        body, name, (n_steps,),
        [pl.BlockSpec((tm, f2), lambda i: (i, 0)),
         pl.BlockSpec((tm, f), lambda i: (i, 0)),
         pl.BlockSpec((1, f), lambda i: (0, 0)),
         wspec, wspec,
         pl.BlockSpec((A_CHUNK, f), lambda i: (0, 0))],
        [pl.BlockSpec((tm, f2), lambda i: (i, 0)),
         wspec,
         pl.BlockSpec((A_GROUPS, A_CHUNK, 1), lambda i: (0, 0, 0)),
         pl.BlockSpec((1, f), lambda i: (0, 0))],
        [jax.ShapeDtypeStruct((t, f2), BF16),
         jax.ShapeDtypeStruct((A_GROUPS, A_CHUNK, A_CHUNK), F32),
         jax.ShapeDtypeStruct((A_GROUPS, A_CHUNK, 1), F32),
         jax.ShapeDtypeStruct((1, f), F32)],
        scratch=[pltpu.VMEM((A_CHUNK, f), F32), pltpu.VMEM((A_CHUNK, f), F32)],
        vmem_bytes=4 * _nbytes((tm, f2), BF16) + 2 * _nbytes((tm, f), BF16) + 12 * _nbytes((A_CHUNK, f2), F32),
    )(zpre, dp, g_sgu.reshape(1, f), w_sp, w_sp_t, b_full)
    return dz, dw, db.reshape(A_GROUPS, A_CHUNK), dg.reshape(f)


def _pair_valid(qi, col):
    qc = qi // CHUNK
    kc = col // CHUNK
    return (kc >= qc) & (kc <= qc + N_LEFT_CHUNKS)


def _diagonal_onehot():
    e = lax.broadcasted_iota(jnp.int32, (REL_PAD, DIAGONALS), 1)
    idx = jnp.clip(PAIR_BAND - 1 - e, -MAX_REL, MAX_REL) + MAX_REL
    r = lax.broadcasted_iota(jnp.int32, (REL_PAD, DIAGONALS), 0)
    return jnp.where(r == idx, 1.0, 0.0).astype(BF16)


def _bias_build(table, name):
    h = table.shape[0]
    tab = jnp.pad(table, ((0, 0), (0, REL_PAD - N_REL)))

    def body(t_ref, o_ref):
        oh = _diagonal_onehot()
        diag = jnp.zeros((h, DIAGONALS), F32)
        for piece in _split3(t_ref[...]):
            diag += jnp.dot(piece, oh, preferred_element_type=F32)
        col = lax.broadcasted_iota(jnp.int32, (h, PAIR_BAND), 1)
        for qi in range(PAIR_ROWS):
            row = pltpu.roll(diag, (qi - (PAIR_ROWS - 1)) % DIAGONALS, 1)[:, :PAIR_BAND]
            o_ref[qi] = jnp.where(_pair_valid(qi, col), row, NEG_INF)

    out = _call(
        body, name, (1,),
        [pl.BlockSpec((h, REL_PAD), lambda i: (0, 0))],
        pl.BlockSpec((PAIR_ROWS, h, PAIR_BAND), lambda i: (0, 0, 0)),
        jax.ShapeDtypeStruct((PAIR_ROWS, h, PAIR_BAND), F32),
        vmem_bytes=4 * _nbytes((PAIR_ROWS, h, PAIR_BAND), F32),
    )(tab)
    return jnp.transpose(out, (1, 0, 2))


def _bias_block(pair_bias):
    rest = K_BLOCK - PAIR_BAND
    return jnp.concatenate(
        [jnp.pad(pair_bias, ((0, 0), (0, 0), (p * PAIR_ROWS, rest - p * PAIR_ROWS)), constant_values=NEG_INF)
         for p in range(PAIRS_PER_BLOCK)], axis=1)


def _bias_grad(dbias, name):
    h = dbias.shape[0]
    db_t = jnp.transpose(dbias, (1, 0, 2))

    def body(d_ref, o_ref):
        diag = jnp.zeros((h, DIAGONALS), F32)
        for qi in range(PAIR_ROWS):
            diag += pltpu.roll(d_ref[qi], PAIR_ROWS - 1 - qi, 1)
        oh = _diagonal_onehot()
        acc = jnp.zeros((h, REL_PAD), F32)
        for piece in _split3(diag):
            acc += lax.dot_general(piece, oh, NT, preferred_element_type=F32)
        o_ref[...] = acc

    out = _call(
        body, name, (1,),
        [pl.BlockSpec((PAIR_ROWS, h, DIAGONALS), lambda i: (0, 0, 0))],
        pl.BlockSpec((h, REL_PAD), lambda i: (0, 0)),
        jax.ShapeDtypeStruct((h, REL_PAD), F32),
        vmem_bytes=4 * _nbytes((PAIR_ROWS, h, DIAGONALS), F32),
    )(db_t)
    return out[:, :N_REL]


def _head_masks():
    lane = lax.broadcasted_iota(jnp.int32, (Q_BLOCK, HEAD_PAIR), 1)
    return lane < HEAD_DIM, lane >= HEAD_DIM


def _block_scores(qm, kb, bias, valid):
    s = lax.dot_general(qm, kb, NT, preferred_element_type=F32) + bias
    return s if valid is None else jnp.where(valid, s, NEG_INF)


def _softmax_rows(s):
    e = jnp.exp(s - jnp.max(s, axis=-1, keepdims=True))
    return e * (1.0 / jnp.sum(e, axis=-1, keepdims=True))


def _block_probs(qm, kb, bias, valid):
    return _softmax_rows(_block_scores(qm, kb, bias, valid))


def _padded_then_plain(step, n_blocks):
    n_padded = min(LEFT // Q_BLOCK, n_blocks)
    lax.fori_loop(0, n_padded, lambda j, c: step(j, c, True), 0)
    lax.fori_loop(n_padded, n_blocks, lambda j, c: step(j, c, False), 0, unroll=ATTN_UNROLL)


def _attn_fwd(q, kvpad, bias, name):
    t, d = q.shape
    n_pairs = d // HEAD_PAIR
    n_blocks = t // Q_BLOCK

    def body(q_ref, k_ref, v_ref, b_ref, o_ref):
        masks = _head_masks()
        key = lax.broadcasted_iota(jnp.int32, (Q_BLOCK, K_BLOCK), 1)

        def step(j, carry, padded):
            r0 = pl.multiple_of(j * Q_BLOCK, Q_BLOCK)
            q2 = q_ref[pl.ds(r0, Q_BLOCK), :].astype(F32)
            kb = k_ref[pl.ds(r0, K_BLOCK), :]
            vb = v_ref[pl.ds(r0, K_BLOCK), :]
            valid = key >= LEFT - j * Q_BLOCK if padded else None
            scores = [_block_scores(jnp.where(masks[a], q2, 0.0).astype(BF16), kb, b_ref[a], valid) for a in range(2)]
            probs = [_softmax_rows(s).astype(BF16) for s in scores]
            outs = [jnp.dot(p, vb, preferred_element_type=F32) for p in probs]
            o_ref[pl.ds(r0, Q_BLOCK), :] = jnp.where(masks[0], outs[0], outs[1]).astype(BF16)
            return carry

        _padded_then_plain(step, n_blocks)

    return _call(
        body, name, (n_pairs,),
        [pl.BlockSpec((t, HEAD_PAIR), lambda p: (0, p)),
         pl.BlockSpec((LEFT + t, HEAD_PAIR), lambda p: (0, p)),
         pl.BlockSpec((LEFT + t, HEAD_PAIR), lambda p: (0, n_pairs + p)),
         pl.BlockSpec((2, Q_BLOCK, K_BLOCK), lambda p: (p, 0, 0))],
        pl.BlockSpec((t, HEAD_PAIR), lambda p: (0, p)),
        jax.ShapeDtypeStruct((t, d), BF16),
        vmem_bytes=8 * _nbytes((LEFT + t, HEAD_PAIR), BF16) + 12 * _nbytes((2, Q_BLOCK, K_BLOCK), F32),
    )(q, kvpad, kvpad, bias)


def _attn_bwd(q, kvpad, bias, do, dk_in, dv_in, name):
    t, d = q.shape
    n_pairs = d // HEAD_PAIR
    n_blocks = t // Q_BLOCK
    has_in = dk_in is not None

    def body(*refs):
        refs = list(refs)
        q_ref, k_ref, v_ref, b_ref, do_ref = refs[:5]
        refs = refs[5:]
        if has_in:
            dki_ref, dvi_ref = refs[:2]
            refs = refs[2:]
        dq_ref, dk_ref, dv_ref, db_ref = refs
        masks = _head_masks()
        key = lax.broadcasted_iota(jnp.int32, (Q_BLOCK, K_BLOCK), 1)
        if has_in:
            dk_ref[...] = dki_ref[...]
            dv_ref[...] = dvi_ref[...]
        else:
            dk_ref[...] = jnp.zeros_like(dk_ref)
            dv_ref[...] = jnp.zeros_like(dv_ref)
        db_ref[...] = jnp.zeros_like(db_ref)

        def step(j, carry, padded):
            r0 = pl.multiple_of(j * Q_BLOCK, Q_BLOCK)
            q2 = q_ref[pl.ds(r0, Q_BLOCK), :].astype(F32)
            do2 = do_ref[pl.ds(r0, Q_BLOCK), :].astype(F32)
            kb = k_ref[pl.ds(r0, K_BLOCK), :]
            vb = v_ref[pl.ds(r0, K_BLOCK), :]
            valid = key >= LEFT - j * Q_BLOCK if padded else None
            heads = range(2)
            qms = [jnp.where(masks[a], q2, 0.0).astype(BF16) for a in heads]
            doms = [jnp.where(masks[a], do2, 0.0).astype(BF16) for a in heads]
            scores = [_block_scores(qms[a], kb, b_ref[a], valid) for a in heads]
            dps = [lax.dot_general(doms[a], vb, NT, preferred_element_type=F32) for a in heads]
            ps = [_softmax_rows(s) for s in scores]
            dss = [ps[a] * (dps[a] - jnp.sum(dps[a] * ps[a], axis=-1, keepdims=True)) for a in heads]
            for a in heads:
                for pair in range(PAIRS_PER_BLOCK):
                    lo = pair * PAIR_ROWS
                    db_ref[a, :, pl.ds(0, PAIR_BAND)] += dss[a][lo:lo + PAIR_ROWS, lo:lo + PAIR_BAND]
            dsbs = [ds.astype(BF16) for ds in dss]
            pbs = [p.astype(BF16) for p in ps]
            dqs = [jnp.dot(dsbs[a], kb, preferred_element_type=F32) for a in heads]
            dk_acc = sum(lax.dot_general(dsbs[a], qms[a], TN, preferred_element_type=F32) for a in heads)
            dv_acc = sum(lax.dot_general(pbs[a], doms[a], TN, preferred_element_type=F32) for a in heads)
            dq = jnp.where(masks[0], dqs[0], dqs[1]) * ATTN_SCALE
            dq_ref[pl.ds(r0, Q_BLOCK), :] = dq.astype(BF16)
            dk_ref[pl.ds(r0, K_BLOCK), :] += dk_acc
            dv_ref[pl.ds(r0, K_BLOCK), :] += dv_acc
            return carry

        _padded_then_plain(step, n_blocks)

    q_spec = pl.BlockSpec((t, HEAD_PAIR), lambda p: (0, p))
    kv_spec = pl.BlockSpec((LEFT + t, HEAD_PAIR), lambda p: (0, p))
    operands = [q, kvpad, kvpad, bias, do]
    in_specs = [q_spec, kv_spec, pl.BlockSpec((LEFT + t, HEAD_PAIR), lambda p: (0, n_pairs + p)),
                pl.BlockSpec((2, Q_BLOCK, K_BLOCK), lambda p: (p, 0, 0)), q_spec]
    aliases = None
    if has_in:
        operands += [dk_in, dv_in]
        in_specs += [kv_spec, kv_spec]
        aliases = {5: 1, 6: 2}
    return _call(
        body, name, (n_pairs,),
        in_specs,
        [q_spec, kv_spec, kv_spec, pl.BlockSpec((2, PAIR_ROWS, DIAGONALS), lambda p: (p, 0, 0))],
        [jax.ShapeDtypeStruct((t, d), BF16),
         jax.ShapeDtypeStruct((LEFT + t, d), F32),
         jax.ShapeDtypeStruct((LEFT + t, d), F32),
         jax.ShapeDtypeStruct((d // HEAD_DIM, PAIR_ROWS, DIAGONALS), F32)],
        vmem_bytes=10 * _nbytes((LEFT + t, HEAD_PAIR), BF16) + 8 * _nbytes((LEFT + t, HEAD_PAIR), F32)
        + 16 * _nbytes((2, Q_BLOCK, K_BLOCK), F32),
        aliases=aliases,
    )(*operands)


def _loss_head(x, g, target, name):
    t, d = x.shape
    tm = _tile(t, 512)

    def body(x_ref, g_ref, t_ref, dx_ref, loss_ref, dg_ref):
        @pl.when(pl.program_id(0) == 0)
        def _():
            loss_ref[...] = jnp.zeros_like(loss_ref)
            dg_ref[...] = jnp.zeros_like(dg_ref)

        xf = x_ref[...]
        r = lax.rsqrt(jnp.mean(xf * xf, axis=-1, keepdims=True) + EPS)
        xhat = xf * r
        diff = xhat * g_ref[...] - t_ref[...]
        row_loss = jnp.mean(diff * diff, axis=-1, keepdims=True)
        loss_ref[...] += 0.5 * jnp.sum(row_loss, axis=0, keepdims=True)
        dy = diff * (1.0 / d)
        dg_ref[...] += jnp.sum(dy * xhat, axis=0, keepdims=True)
        dxhat = dy * g_ref[...]
        dx_ref[...] = r * (dxhat - xhat * jnp.mean(dxhat * xhat, axis=-1, keepdims=True))

    row = pl.BlockSpec((tm, d), lambda i: (i, 0))
    vec = pl.BlockSpec((1, d), lambda i: (0, 0))
    dx, loss, dg = _call(
        body, name, (t // tm,),
        [row, vec, row],
        [row, pl.BlockSpec((1, 1), lambda i: (0, 0)), vec],
        [jax.ShapeDtypeStruct((t, d), F32), jax.ShapeDtypeStruct((1, 1), F32), jax.ShapeDtypeStruct((1, d), F32)],
        vmem_bytes=10 * _nbytes((tm, d), F32),
    )(x, g.reshape(1, d), target)
    return dx, loss[0, 0], dg.reshape(d)


def _adamw_store(g, w_ref, m_ref, v_ref, g_ref, d_ref, nm_ref, nv_ref):
    c1 = 1.0 / (1.0 - ADAM_B1 ** ADAM_STEP)
    c2 = 1.0 / (1.0 - ADAM_B2 ** ADAM_STEP)
    nm = ADAM_B1 * m_ref[...] + (1.0 - ADAM_B1) * g
    nv = ADAM_B2 * v_ref[...] + (1.0 - ADAM_B2) * (g * g)
    g_ref[...] = g
    nm_ref[...] = nm
    nv_ref[...] = nv
    d_ref[...] = -ADAM_LR * ((nm * c1) / (jnp.sqrt(nv * c2) + ADAM_EPS) + ADAM_WD * w_ref[...])


def _adamw_layer(recv, own, w, m, v, layer, prev, me, name):
    n_src, r, c = recv.shape
    tr = _row_tile(r, max(16, (256 * 1024) // c), 16)
    first = prev is None

    def body(me_ref, recv_ref, own_ref, w_ref, m_ref, v_ref, *rest):
        mine = me_ref[0]
        own_part = own_ref[...].astype(F32)
        g = None
        for s in range(n_src):
            part = jnp.where(mine == s, own_part, recv_ref[s].astype(F32))
            g = part if g is None else g + part
        _adamw_store(g, w_ref, m_ref, v_ref, *rest[-4:])

    blk = pl.BlockSpec((None, tr, c), lambda i, me_ref: (layer, i, 0))
    any_spec = pl.BlockSpec(memory_space=pl.ANY)
    out = jax.ShapeDtypeStruct(w.shape, F32)
    operands = [me, recv, own, w, m, v] + ([] if first else list(prev))
    vmem = 2 * _nbytes((n_src + 1, tr, c), BF16) + 18 * _nbytes((tr, c), F32)
    return pl.pallas_call(
        body,
        name=name,
        grid_spec=pltpu.PrefetchScalarGridSpec(
            num_scalar_prefetch=1,
            grid=(r // tr,),
            in_specs=[pl.BlockSpec((n_src, tr, c), lambda i, me_ref: (0, i, 0)),
                      pl.BlockSpec((None, tr, c), lambda i, me_ref: (me_ref[0], i, 0)),
                      blk, blk, blk] + ([] if first else [any_spec] * 4),
            out_specs=[blk, blk, blk, blk],
        ),
        out_shape=[out, out, out, out],
        input_output_aliases={} if first else {6 + j: j for j in range(4)},
        compiler_params=pltpu.CompilerParams(
            dimension_semantics=("arbitrary",),
            vmem_limit_bytes=int(min(max(VMEM_FLOOR_BYTES, vmem * 5 // 4), VMEM_CEIL_BYTES))),
    )(*operands)


def _adamw(parts, w, m, v, name):
    n_layers, n_src, r, c = parts.shape
    mult = 16 if parts.dtype == BF16 else 8
    tr = _row_tile(r, max(mult, (256 * 1024) // c), mult)

    def body(p_ref, w_ref, m_ref, v_ref, g_ref, d_ref, nm_ref, nv_ref):
        g = p_ref[0].astype(F32)
        for s in range(1, n_src):
            g = g + p_ref[s].astype(F32)
        _adamw_store(g, w_ref, m_ref, v_ref, g_ref, d_ref, nm_ref, nv_ref)

    blk = pl.BlockSpec((None, tr, c), lambda l, i: (l, i, 0))
    out = jax.ShapeDtypeStruct((n_layers, r, c), F32)
    return _call(
        body, name, (n_layers, r // tr),
        [pl.BlockSpec((None, n_src, tr, c), lambda l, i: (l, 0, i, 0)), blk, blk, blk],
        [blk, blk, blk, blk],
        [out, out, out, out],
        vmem_bytes=2 * _nbytes((n_src, tr, c), parts.dtype) + 18 * _nbytes((tr, c), F32),
    )(parts, w, m, v)


def _ordered_sum(parts, name):
    n_src, r, c = parts.shape

    def body(p_ref, o_ref):
        acc = p_ref[0]
        for s in range(1, n_src):
            acc = acc + p_ref[s]
        o_ref[...] = acc

    return _call(
        body, name, (1,),
        [pl.BlockSpec((n_src, r, c), lambda i: (0, 0, 0))],
        pl.BlockSpec((r, c), lambda i: (0, 0)),
        jax.ShapeDtypeStruct((r, c), F32),
        vmem_bytes=4 * _nbytes((n_src, r, c), F32),
    )(parts)


def _position():
    return lax.axis_index("x"), lax.axis_index("y"), lax.axis_index("c")


def _linear(p):
    return 4 * p[0] + 2 * p[1] + p[2]


def _all_gather(shards, name):
    n = len(shards)

    def body(*refs):
        ins, outs = refs[:n], refs[n:2 * n]
        send_sems, recv_sems, local_sems = refs[2 * n:]
        x, y, c = _position()
        me, sibling = (x, y, c), (x, y, 1 - c)
        chips = [(1 - x, y), (x, 1 - y), (1 - x, 1 - y)]

        def slab(t, p):
            return outs[t].at[:, _linear(p)]

        def copy(t, k, block, to, src=None):
            return pltpu.make_async_remote_copy(
                src_ref=slab(t, block) if src is None else src,
                dst_ref=slab(t, block),
                send_sem=send_sems.at[t, k],
                recv_sem=recv_sems.at[t, k],
                device_id=to,
                device_id_type=MESH,
            )

        started = []
        for t in range(n):
            mine = pltpu.make_async_copy(ins[t], slab(t, me), local_sems.at[t])
            mine.start()
            started.append(mine)
        sends = []
        for t in range(n):
            first = [copy(t, 0, me, sibling, src=ins[t])]
            first += [copy(t, 1 + j, me, (*chip, c), src=ins[t]) for j, chip in enumerate(chips)]
            for cp in first:
                cp.start()
            sends += first
        for t in range(n):
            for j, chip in enumerate(chips):
                copy(t, 1 + j, (*chip, c), me).wait_recv()
                passed = copy(t, 4 + j, (*chip, c), sibling)
                passed.start()
                sends.append(passed)
        for t in range(n):
            copy(t, 0, sibling, me).wait_recv()
            for j, chip in enumerate(chips):
                copy(t, 4 + j, (*chip, 1 - c), me).wait_recv()
        for cp in sends:
            cp.wait_send()
        for mine in started:
            mine.wait()

    out_shape = [jax.ShapeDtypeStruct((s.shape[0], N_DEV) + s.shape[1:], s.dtype) for s in shards]
    return pl.pallas_call(
        body,
        name=name,
        in_specs=[HBM_SPEC] * n,
        out_specs=[HBM_SPEC] * n,
        out_shape=out_shape,
        scratch_shapes=[
            pltpu.SemaphoreType.DMA((n, N_DEV - 1)),
            pltpu.SemaphoreType.DMA((n, N_DEV - 1)),
            pltpu.SemaphoreType.DMA((n,)),
        ],
    )(*shards)


def _exchange(blocks, name):
    n = len(blocks)

    def body(*refs):
        ins, outs = refs[:n], refs[n:2 * n]
        send_sems, recv_sems, local_sems = refs[2 * n:]
        x, y, c = _position()
        me = _linear((x, y, c))
        flips = [(fx, fy, fc) for fx in (0, 1) for fy in (0, 1) for fc in (0, 1)][1:]

        def peer_of(flip):
            fx, fy, fc = flip
            return (1 - x if fx else x, 1 - y if fy else y, 1 - c if fc else c)

        def copy(t, k, peer):
            return pltpu.make_async_remote_copy(
                src_ref=ins[t].at[:, _linear(peer)],
                dst_ref=outs[t].at[:, me],
                send_sem=send_sems.at[t, k],
                recv_sem=recv_sems.at[t, k],
                device_id=peer,
                device_id_type=MESH,
            )

        def arrival(t, k, peer):
            return pltpu.make_async_remote_copy(
                src_ref=ins[t].at[:, _linear(peer)],
                dst_ref=outs[t].at[:, _linear(peer)],
                send_sem=send_sems.at[t, k],
                recv_sem=recv_sems.at[t, k],
                device_id=peer,
                device_id_type=MESH,
            )

        own = []
        for t in range(n):
            cp = pltpu.make_async_copy(ins[t].at[:, me], outs[t].at[:, me], local_sems.at[t])
            cp.start()
            own.append(cp)
        sends = []
        for t in range(n):
            for k, flip in enumerate(flips):
                cp = copy(t, k, peer_of(flip))
                cp.start()
                sends.append(cp)
        for t in range(n):
            for k, flip in enumerate(flips):
                arrival(t, k, peer_of(flip)).wait_recv()
        for cp in sends:
            cp.wait_send()
        for cp in own:
            cp.wait()

    out_shape = [jax.ShapeDtypeStruct(b.shape, b.dtype) for b in blocks]
    return pl.pallas_call(
        body,
        name=name,
        in_specs=[HBM_SPEC] * n,
        out_specs=[HBM_SPEC] * n,
        out_shape=out_shape,
        scratch_shapes=[
            pltpu.SemaphoreType.DMA((n, N_DEV - 1)),
            pltpu.SemaphoreType.DMA((n, N_DEV - 1)),
            pltpu.SemaphoreType.DMA((n,)),
        ],
    )(*blocks)


def _peers():
    x, y, c = _position()
    flips = [(fx, fy, fc) for fx in (0, 1) for fy in (0, 1) for fc in (0, 1)][1:]
    return [(1 - x if fx else x, 1 - y if fy else y, 1 - c if fc else c) for fx, fy, fc in flips]


def _split_start(srcs, lands, carry, name, exchange=False):
    n = len(srcs)

    def body(*refs):
        src_refs, land_refs = refs[:n], refs[n:2 * n]
        send_sems, recv_sems = refs[2 * n + 1], refs[2 * n + 2]
        me = _linear(_position())
        for t in range(n):
            for k, peer in enumerate(_peers()):
                pltpu.make_async_remote_copy(
                    src_ref=src_refs[t].at[_linear(peer)] if exchange else src_refs[t],
                    dst_ref=land_refs[t].at[me],
                    send_sem=send_sems.at[t * (N_DEV - 1) + k],
                    recv_sem=recv_sems.at[t * (N_DEV - 1) + k],
                    device_id=peer,
                    device_id_type=MESH,
                ).start()

    operands = list(srcs) + list(lands) + [carry]
    sems = pltpu.SemaphoreType.DMA((n * (N_DEV - 1),))
    out = pl.pallas_call(
        body,
        name=name,
        in_specs=[HBM_SPEC] * len(operands),
        out_specs=[SEM_SPEC, SEM_SPEC] + [HBM_SPEC] * len(operands),
        out_shape=[sems, sems] + [pltpu.HBM(a.shape, a.dtype) for a in operands],
        input_output_aliases={i: 2 + i for i in range(len(operands))},
        compiler_params=pltpu.CompilerParams(has_side_effects=pltpu.SideEffectType.DATAFLOW_SIDE_EFFECTING),
    )(*[pltpu.with_memory_space_constraint(a, pltpu.HBM) for a in operands])
    return out[0], out[1], out[2:2 + n], out[2 + n:2 + 2 * n], out[2 + 2 * n]


def _split_wait(send_sems, recv_sems, srcs, lands, after, name, exchange=False):
    n = len(srcs)

    def body(*refs):
        src_refs, land_refs = refs[:n], refs[n:2 * n]
        send_ref, recv_ref = refs[2 * n], refs[2 * n + 1]
        for t in range(n):
            for k, peer in enumerate(_peers()):
                copy = pltpu.make_async_remote_copy(
                    src_ref=src_refs[t].at[0] if exchange else src_refs[t],
                    dst_ref=land_refs[t].at[0],
                    send_sem=send_ref.at[t * (N_DEV - 1) + k],
                    recv_sem=recv_ref.at[t * (N_DEV - 1) + k],
                    device_id=peer,
                    device_id_type=MESH,
                )
                copy.wait_send()
                copy.wait_recv()

    arrays = list(srcs) + list(lands)
    out = pl.pallas_call(
        body,
        name=name,
        in_specs=[HBM_SPEC] * len(arrays) + [SEM_SPEC, SEM_SPEC, pl.BlockSpec(memory_space=pl.ANY)],
        out_specs=[HBM_SPEC] * len(arrays),
        out_shape=[pltpu.HBM(a.shape, a.dtype) for a in arrays],
        input_output_aliases={i: i for i in range(len(arrays))},
        compiler_params=pltpu.CompilerParams(has_side_effects=pltpu.SideEffectType.DATAFLOW_SIDE_EFFECTING),
    )(*arrays, send_sems, recv_sems, after)
    return out[:n], out[n:]


def _pack(arrays, row_multiple):
    flat = jnp.concatenate([a.reshape(-1) for a in arrays])
    quantum = row_multiple * FLAT_LANES
    padded = -(-flat.shape[0] // quantum) * quantum
    return jnp.pad(flat, (0, padded - flat.shape[0])).reshape(-1, FLAT_LANES)


def _unpack(flat, like):
    flat = flat.reshape(-1)
    out, at = [], 0
    for a in like:
        size = math.prod(a.shape)
        out.append(flat[at:at + size].reshape(a.shape))
        at += size
    return out


def kernel(x, a_norm, a_w_in, a_sgu_norm, a_w_spatial, a_b_spatial, a_w_out, kv_norm, w_kv, b_norm, b_w_q, b_rel_bias, b_w_o, ffn_norm, ffn_w_gate_up, ffn_w_down, final_norm, loss_target, m_a_norm, m_a_w_in, m_a_sgu_norm, m_a_w_spatial, m_a_b_spatial, m_a_w_out, m_kv_norm, m_w_kv, m_b_norm, m_b_w_q, m_b_rel_bias, m_b_w_o, m_ffn_norm, m_ffn_w_gate_up, m_ffn_w_down, m_final_norm, v_a_norm, v_a_w_in, v_a_sgu_norm, v_a_w_spatial, v_a_b_spatial, v_a_w_out, v_kv_norm, v_w_kv, v_b_norm, v_b_w_q, v_b_rel_bias, v_b_w_o, v_ffn_norm, v_ffn_w_gate_up, v_ffn_w_down, v_final_norm):
    xs = x[0]
    target = loss_target[0]
    t, d = xs.shape
    n_a = a_w_in.shape[0]
    n_b = b_w_q.shape[0]
    depth = ffn_w_gate_up.shape[0]
    f_a = a_w_out.shape[1] * N_DEV
    gd = f_a // A_GROUPS
    nb_ffn = ffn_w_gate_up.shape[2]
    me = _linear(_position())

    small_rows = -(-(a_norm.size + a_sgu_norm.size) // (8 * 128)) * 8
    small = jnp.pad(jnp.concatenate([a_norm.reshape(-1), a_sgu_norm.reshape(-1)]),
                    (0, small_rows * 128 - a_norm.size - a_sgu_norm.size)).reshape(1, small_rows, 128)

    def shard(w, layer=None):
        return (w if layer is None else w[layer]).astype(BF16)

    stages = []
    for layer in range(depth):
        if layer == 0:
            stages += [("a0", [shard(a_w_in, 0)]), ("a0_out", [shard(a_w_out, 0)])]
        elif layer < n_a:
            stages.append((f"a{layer}", [shard(a_w_in, layer), shard(a_w_out, layer)]))
        else:
            i = layer - n_a
            shared = [shard(w_kv)] if i == 0 else []
            stages.append((f"b{i}", shared + [shard(b_w_q, i), shard(b_w_o, i)]))
        stages.append((f"f{layer}", [shard(ffn_w_gate_up, layer), shard(ffn_w_down, layer)]))
    first = _all_gather([s[None] for s in stages[0][1]] + [small], "gather_first")
    gathered = {stages[0][0]: [g[0] for g in first[:-1]]}
    small_g = first[-1].reshape(N_DEV, -1)
    a_norm_full = small_g[:, :a_norm.size].reshape(N_DEV, n_a, -1).transpose(1, 0, 2).reshape(n_a, d)
    a_sgu_full = small_g[:, a_norm.size:a_norm.size + a_sgu_norm.size].reshape(
        N_DEV, n_a, -1).transpose(1, 0, 2).reshape(n_a, f_a)
    in_flight = {}
    for key, shards in stages[1:]:
        lands = [lax.dynamic_update_slice(lax.empty((N_DEV,) + s.shape, BF16), s[None], (me, 0, 0)) for s in shards]
        send, recv, srcs, lands, a_norm_full = _split_start(shards, lands, a_norm_full, f"gather_start_{key}")
        in_flight[key] = (send, recv, srcs, lands)

    def weights(key, after):
        if key not in gathered:
            _, gathered[key] = _split_wait(*in_flight.pop(key), after, f"gather_wait_{key}")
        return gathered[key]

    rows_down = ffn_w_down.shape[1]

    def mixer_a_weights(i, after):
        if i == 0:
            (w_in,), (w_out,) = weights("a0", after[0]), weights("a0_out", after[1])
        else:
            w_in, w_out = weights(f"a{i}", after[0])
        return w_in[None], w_out.reshape(1, f_a, d)

    def mixer_b_weights(i, after):
        ws = weights(f"b{i}", after)
        return ws[-2].reshape(1, d, d), ws[-1].reshape(1, d, d)

    def ffn_weights(layer, after):
        w_gu, w_dn = weights(f"f{layer}", after)
        return w_gu[None], w_dn.reshape(1, N_DEV // 2, 2 * rows_down, d)

    w_sp_t = jnp.swapaxes(a_w_spatial, -1, -2)
    b_full = jnp.repeat(jnp.swapaxes(a_b_spatial, -1, -2), gd, axis=-1)

    saved = []

    def ffn_fwd(xin, layer):
        hf = _rms_fwd(xin, ffn_norm[layer], f"ffn_norm_fwd_{layer}")
        w_gu, w_dn = ffn_weights(layer, xin)
        gu, act = _ffn_gate_up(f"ffn_gate_up_{layer}", hf, w_gu, 0)
        xout = _mm_down(f"ffn_down_{layer}", act, w_dn, 0, xin)
        return xout, (xin, hf, gu, act)

    for i in range(n_a):
        h = _rms_fwd(xs, a_norm_full[i], f"a_norm_fwd_{i}")
        zpre = _mm_colblock(f"a_in_{i}", h, weights(f"a{i}", xs)[0][None], 0)
        p = _sgu_fwd(zpre, a_sgu_full[i], a_w_spatial[i], b_full[i], f"a_sgu_fwd_{i}")
        w_in, w_out = mixer_a_weights(i, (xs, p))
        x_mid = _mm_natural(f"a_out_{i}", p, w_out, 0, res=xs)
        x_out, ffn_saved = ffn_fwd(x_mid, i)
        saved.append((xs, h, zpre, p, ffn_saved))
        xs = x_out

    x_kv = xs
    w_kv_g = weights("b0", x_kv)[0][None]
    h_kv = _rms_fwd(x_kv, kv_norm, "kv_norm_fwd")
    kv = _mm_colblock("kv_proj", h_kv, w_kv_g, 0)
    kvpad = jnp.pad(kv, ((LEFT, 0), (0, 0)))

    biases = [_bias_block(_bias_build(b_rel_bias[i], f"rel_bias_{i}")) for i in range(n_b)]
    for i in range(n_b):
        layer = n_a + i
        w_q, w_o = mixer_b_weights(i, xs)
        hb = _rms_fwd(xs, b_norm[i], f"b_norm_fwd_{i}")
        q = _mm_natural(f"b_q_{i}", hb, w_q, 0, out_dtype=BF16, scale=ATTN_SCALE)
        o = _attn_fwd(q, kvpad, biases[i], f"b_attn_fwd_{i}")
        x_mid = _mm_natural(f"b_o_{i}", o, w_o, 0, res=xs)
        x_out, ffn_saved = ffn_fwd(x_mid, layer)
        saved.append((xs, hb, q, o, ffn_saved))
        xs = x_out

    dx, loss_local, g_final = _loss_head(xs, final_norm, target, "loss_head")
    loss = lax.psum(loss_local, ("x", "y", "c"))

    big_grads = {}
    pending = []
    in_flight_grads = []

    def start_exchange(dx, tag):
        srcs = [big_grads[key] for key in pending]
        lands = [lax.empty(s.shape, BF16) for s in srcs]
        send, recv, srcs, lands, dx = _split_start(srcs, lands, dx, f"exchange_start_{tag}", exchange=True)
        in_flight_grads.append((list(pending), send, recv, srcs, lands, tag))
        pending.clear()
        return dx

    g_ffn_norm = [None] * depth
    g_a_norm = [None] * n_a
    g_a_sgu = [None] * n_a
    g_w_sp = [None] * n_a
    g_b_sp = [None] * n_a
    g_b_norm = [None] * n_b
    g_rel = [None] * n_b

    def ffn_bwd(dx, layer, ffn_saved):
        xin, hf, gu, act = ffn_saved
        big_grads["ffn_w_down", layer] = _mm_dw_down(f"ffn_down_dw_{layer}", act, dx)
        w_gu, w_dn = ffn_weights(layer, xin)
        dgu = _ffn_down_dx(f"ffn_down_dx_{layer}", dx, w_dn, 0, gu).reshape(N_DEV, t, nb_ffn)
        big_grads["ffn_w_gate_up", layer] = _mm_dw_colblock(
            f"ffn_gate_up_dw_{layer}", hf, dgu, blocked_in=True, transposed=True)
        pending.extend([("ffn_w_gate_up", layer), ("ffn_w_down", layer)])
        dx, g_ffn_norm[layer] = _mm_t_colblock_norm_bwd(
            f"ffn_gate_up_dx_{layer}", dgu, w_gu, 0, xin, ffn_norm[layer], dx, blocked_in=True)
        return dx

    dk = dv = None
    for i in reversed(range(n_b)):
        layer = n_a + i
        x_in, hb, q, o, ffn_saved = saved[layer]
        dx = ffn_bwd(dx, layer, ffn_saved)
        big_grads["b_w_o", i] = _mm_dw_natural(f"b_o_dw_{i}", o, dx)
        w_q, w_o = mixer_b_weights(i, x_in)
        do = _mm_t_natural(f"b_o_dx_{i}", dx, w_o, 0)
        dq, dk, dv, dbias = _attn_bwd(q, kvpad, biases[i], do, dk, dv, f"b_attn_bwd_{i}")
        g_rel[i] = _bias_grad(dbias, f"rel_bias_grad_{i}")
        big_grads["b_w_q", i] = _mm_dw_natural(f"b_q_dw_{i}", hb, dq)
        pending.extend([("b_w_o", i), ("b_w_q", i)])
        dh = _mm_t_natural(f"b_q_dx_{i}", dq, w_q, 0)
        dx, g_b_norm[i] = _rms_bwd(x_in, b_norm[i], dh, dx, f"b_norm_bwd_{i}")
        if i > 0:
            dx = start_exchange(dx, f"b{i}")

    dkv = jnp.concatenate([dk[LEFT:], dv[LEFT:]], axis=1).astype(BF16)
    big_grads["w_kv", 0] = _mm_dw_colblock("kv_proj_dw", h_kv, dkv)
    pending.append(("w_kv", 0))
    dx, g_kv_norm = _mm_t_colblock_norm_bwd("kv_proj_dx", dkv, w_kv_g, 0, x_kv, kv_norm, dx)
    dx = start_exchange(dx, "kv")

    for i in reversed(range(n_a)):
        x_in, h, zpre, p, ffn_saved = saved[i]
        dx = ffn_bwd(dx, i, ffn_saved)
        if i == 0:
            dx = start_exchange(dx, "f0")
        big_grads["a_w_out", i] = _mm_dw_natural(f"a_out_dw_{i}", p, dx)
        pending.append(("a_w_out", i))
        if i == 0:
            dx = start_exchange(dx, "a0_out")
        w_in, w_out = mixer_a_weights(i, (x_in, p))
        dp = _mm_t_natural(f"a_out_dx_{i}", dx, w_out, 0)
        dz, g_w_sp[i], g_b_sp[i], g_a_sgu[i] = _sgu_bwd(
            zpre, dp, a_sgu_full[i], a_w_spatial[i], w_sp_t[i], b_full[i], f"a_sgu_bwd_{i}")
        big_grads["a_w_in", i] = _mm_dw_colblock(f"a_in_dw_{i}", h, dz)
        pending.append(("a_w_in", i))
        if i == 0:
            dx = start_exchange(dx, "a0_in")
        dx, g_a_norm[i] = _mm_t_colblock_norm_bwd(f"a_in_dx_{i}", dz, w_in, 0, x_in, a_norm_full[i], dx)
        if i > 0:
            dx = start_exchange(dx, f"a{i}")
    grad_x = dx[None]

    small_like = [jax.ShapeDtypeStruct((n_a, d), F32), jax.ShapeDtypeStruct((n_a, f_a), F32),
                  a_w_spatial, a_b_spatial, kv_norm, b_norm, b_rel_bias, ffn_norm, final_norm]
    small_partial = _pack(
        [jnp.stack(g_a_norm), jnp.stack(g_a_sgu), jnp.stack(g_w_sp), jnp.stack(g_b_sp), g_kv_norm,
         jnp.stack(g_b_norm), jnp.stack(g_rel), jnp.stack(g_ffn_norm), g_final], N_DEV * 8)
    chunk_rows = small_partial.shape[0] // N_DEV
    arrived = {}
    for keys, send, recv, srcs, lands, tag in in_flight_grads:
        srcs, lands = _split_wait(send, recv, srcs, lands, dx, f"exchange_wait_{tag}", exchange=True)
        for key, src, land in zip(keys, srcs, lands):
            arrived[key] = (land, src)
    small_got = _exchange([small_partial.reshape(1, N_DEV, chunk_rows, FLAT_LANES)], "exchange_small")[0]
    small_sum = _ordered_sum(small_got[0], "small_grad_sum")
    small_all = _all_gather([small_sum[None]], "gather_small_grads")[0]
    (ga_norm, ga_sgu, gw_sp, gb_sp, gkv_norm, gb_norm, g_relb, gffn_norm, gfinal) = _unpack(small_all, small_like)

    results = {}
    big_names = ["a_w_in", "a_w_out", "w_kv", "b_w_q", "b_w_o", "ffn_w_gate_up", "ffn_w_down"]
    big_wmv = [(a_w_in, m_a_w_in, v_a_w_in), (a_w_out, m_a_w_out, v_a_w_out),
               (w_kv[None], m_w_kv[None], v_w_kv[None]), (b_w_q, m_b_w_q, v_b_w_q), (b_w_o, m_b_w_o, v_b_w_o),
               tuple(jnp.swapaxes(a, 1, 2) for a in (ffn_w_gate_up, m_ffn_w_gate_up, v_ffn_w_gate_up)),
               (ffn_w_down, m_ffn_w_down, v_ffn_w_down)]
    me_arr = jnp.reshape(me, (1,)).astype(jnp.int32)
    for name, (w, m, v) in zip(big_names, big_wmv):
        outs = None
        for layer in range(w.shape[0]):
            got, own = arrived[name, layer]
            outs = _adamw_layer(got, own, w, m, v, layer, outs, me_arr, f"adamw_{name}_{layer}")
        if name == "w_kv":
            outs = [o[0] for o in outs]
        if name == "ffn_w_gate_up":
            outs = [jnp.swapaxes(o, 1, 2) for o in outs]
        results[name] = outs

    n_cols = a_norm.shape[1]
    s_cols = a_sgu_norm.shape[1]
    small_g_list = [lax.dynamic_slice(ga_norm, (0, me * n_cols), (n_a, n_cols)),
                    lax.dynamic_slice(ga_sgu, (0, me * s_cols), (n_a, s_cols)),
                    gw_sp, gb_sp, gkv_norm, gb_norm, g_relb, gffn_norm, gfinal]
    small_names = ["a_norm", "a_sgu_norm", "a_w_spatial", "a_b_spatial", "kv_norm", "b_norm", "b_rel_bias",
                   "ffn_norm", "final_norm"]
    small_w = [a_norm, a_sgu_norm, a_w_spatial, a_b_spatial, kv_norm, b_norm, b_rel_bias, ffn_norm, final_norm]
    small_m = [m_a_norm, m_a_sgu_norm, m_a_w_spatial, m_a_b_spatial, m_kv_norm, m_b_norm, m_b_rel_bias,
               m_ffn_norm, m_final_norm]
    small_v = [v_a_norm, v_a_sgu_norm, v_a_w_spatial, v_a_b_spatial, v_kv_norm, v_b_norm, v_b_rel_bias,
               v_ffn_norm, v_final_norm]
    flat_g = _pack(small_g_list, 8)
    flat_out = _adamw(flat_g[None, None], _pack(small_w, 8)[None], _pack(small_m, 8)[None],
                      _pack(small_v, 8)[None], "adamw_small")
    unpacked = [_unpack(o[0], small_w) for o in flat_out]
    for idx, name in enumerate(small_names):
        results[name] = [unpacked[kind][idx] for kind in range(4)]

    order = ["a_norm", "a_w_in", "a_sgu_norm", "a_w_spatial", "a_b_spatial", "a_w_out", "kv_norm", "w_kv",
             "b_norm", "b_w_q", "b_rel_bias", "b_w_o", "ffn_norm", "ffn_w_gate_up", "ffn_w_down", "final_norm"]
    outputs = [loss, grad_x]
    for kind in range(4):
        outputs += [results[name][kind] for name in order]
    return tuple(outputs)
```

```python
import math

import jax
import jax.numpy as jnp
from jax import lax
from jax.experimental import pallas as pl
from jax.experimental.pallas import tpu as pltpu

F32 = jnp.float32
BF16 = jnp.bfloat16
MESH = pl.DeviceIdType.MESH
HBM_SPEC = pl.BlockSpec(memory_space=pltpu.HBM)
SEM_SPEC = pl.BlockSpec(memory_space=pltpu.SEMAPHORE)

N_DEV = 8
CHUNK = 64
A_CHUNK = 128
A_GROUPS = 8
N_LEFT_CHUNKS = 8
LEFT = N_LEFT_CHUNKS * CHUNK
PAIR_ROWS = 2 * CHUNK
PAIR_BAND = PAIR_ROWS + LEFT
DIAGONALS = PAIR_BAND + PAIR_ROWS
PAIRS_PER_BLOCK = 2
Q_BLOCK = PAIRS_PER_BLOCK * PAIR_ROWS
K_BLOCK = Q_BLOCK + LEFT
ATTN_UNROLL = 2
MAX_REL = 256
N_REL = 2 * MAX_REL + 1
REL_PAD = 640
HEAD_DIM = 64
HEAD_PAIR = 2 * HEAD_DIM
ATTN_SCALE = HEAD_DIM ** -0.5
EPS = 1e-6
NEG_INF = -1e30
ADAM_LR = 0.001
ADAM_B1 = 0.9
ADAM_B2 = 0.999
ADAM_EPS = 1e-08
ADAM_WD = 0.01
ADAM_STEP = 10
FLAT_LANES = 1024
V7X_VMEM_BYTES = 64 * 1024 * 1024
VMEM_FLOOR_BYTES = 32 * 1024 * 1024
VMEM_CEIL_BYTES = V7X_VMEM_BYTES - 8 * 1024 * 1024

NN = (((1,), (0,)), ((), ()))
NT = (((1,), (1,)), ((), ()))
TN = (((0,), (0,)), ((), ()))


def _tile(n, pref):
    return pref if n % pref == 0 else n


def _row_tile(n, pref, mult):
    best = None
    for t in range(mult, min(n, pref) + 1, mult):
        if n % t == 0:
            best = t
    return best if best is not None else n


def _nbytes(shape, dtype):
    n = 1
    for s in shape:
        if s is not None:
            n *= s
    return n * jnp.dtype(dtype).itemsize


def _call(body, name, grid, in_specs, out_specs, out_shape, scratch=(), vmem_bytes=0, aliases=None):
    limit = int(min(max(VMEM_FLOOR_BYTES, vmem_bytes * 5 // 4), VMEM_CEIL_BYTES))
    return pl.pallas_call(
        body,
        name=name,
        grid=grid,
        in_specs=in_specs,
        out_specs=out_specs,
        out_shape=out_shape,
        scratch_shapes=list(scratch),
        input_output_aliases=aliases or {},
        compiler_params=pltpu.CompilerParams(
            dimension_semantics=("arbitrary",) * len(grid), vmem_limit_bytes=limit),
    )


ERFC_P = 0.3275911 / math.sqrt(2.0)
ERFC_HALF_COEFFS = tuple(0.5 * a for a in (1.061405429, -1.453152027, 1.421413741, -0.284496736, 0.254829592))


def _gelu_and_grad(x):
    d = 1.0 + ERFC_P * jnp.abs(x)
    r = pl.reciprocal(d, approx=True)
    t = r * (2.0 - d * r)
    a5, a4, a3, a2, a1 = ERFC_HALF_COEFFS
    ex = jnp.exp(-0.5 * (x * x))
    tail = ((((a5 * t + a4) * t + a3) * t + a2) * t + a1) * t * ex
    cdf = jnp.where(x < 0, tail, 1.0 - tail)
    return x * cdf, cdf + x * ex * (1.0 / math.sqrt(2.0 * math.pi))


def _sigmoid(x):
    return 1.0 / (1.0 + jnp.exp(-x))


def _split3(x):
    hi = x.astype(BF16)
    r1 = x - hi.astype(F32)
    mid = r1.astype(BF16)
    lo = (r1 - mid.astype(F32)).astype(BF16)
    return hi, mid, lo


def _rms_fwd(x, g, name):
    t, d = x.shape
    tm = _tile(t, 512)

    def body(x_ref, g_ref, o_ref):
        xf = x_ref[...]
        r = lax.rsqrt(jnp.mean(xf * xf, axis=-1, keepdims=True) + EPS)
        o_ref[...] = (xf * r * g_ref[...]).astype(o_ref.dtype)

    return _call(
        body, name, (t // tm,),
        [pl.BlockSpec((tm, d), lambda i: (i, 0)), pl.BlockSpec((1, d), lambda i: (0, 0))],
        pl.BlockSpec((tm, d), lambda i: (i, 0)),
        jax.ShapeDtypeStruct((t, d), BF16),
        vmem_bytes=2 * (_nbytes((tm, d), F32) + _nbytes((tm, d), BF16)) + 4 * _nbytes((tm, d), F32),
    )(x, g.reshape(1, d))


def _rms_bwd(x, g, dh, dx_up, name):
    t, d = x.shape
    tm = _tile(t, 512)

    def body(x_ref, g_ref, dh_ref, up_ref, dx_ref, dg_ref):
        @pl.when(pl.program_id(0) == 0)
        def _():
            dg_ref[...] = jnp.zeros_like(dg_ref)

        xf = x_ref[...]
        r = lax.rsqrt(jnp.mean(xf * xf, axis=-1, keepdims=True) + EPS)
        xhat = xf * r
        dy = dh_ref[...].astype(F32)
        dxhat = dy * g_ref[...]
        dg_ref[...] += jnp.sum(dy * xhat, axis=0, keepdims=True)
        dx = r * (dxhat - xhat * jnp.mean(dxhat * xhat, axis=-1, keepdims=True))
        dx_ref[...] = up_ref[...] + dx

    row = pl.BlockSpec((tm, d), lambda i: (i, 0))
    vec = pl.BlockSpec((1, d), lambda i: (0, 0))
    dx, dg = _call(
        body, name, (t // tm,),
        [row, vec, row, row],
        [row, vec],
        [jax.ShapeDtypeStruct((t, d), F32), jax.ShapeDtypeStruct((1, d), F32)],
        vmem_bytes=10 * _nbytes((tm, d), F32),
    )(x, g.reshape(1, d), dh, dx_up)
    return dx, dg.reshape(d)


def _mm(name, dims, a, b, *, grid, a_spec, b_spec, out_shape, out_spec, acc_shape,
        res=None, res_spec=None, scale=None):
    nk = grid[2]
    has_res = res is not None

    def body(*refs):
        refs = list(refs)
        a_ref = refs.pop(0)
        b_ref = refs.pop(0)
        r_ref = refs.pop(0) if has_res else None
        o_ref = refs.pop(0)
        part = lax.dot_general(a_ref[...].astype(BF16), b_ref[...].astype(BF16), dims,
                               preferred_element_type=F32)

        def finish(acc):
            if scale is not None:
                acc = acc * scale
            if has_res:
                acc = acc + r_ref[...]
            o_ref[...] = acc.astype(o_ref.dtype)

        if nk == 1:
            finish(part)
        else:
            acc_ref = refs.pop(0)
            k = pl.program_id(2)

            @pl.when(k == 0)
            def _():
                acc_ref[...] = part

            @pl.when(k > 0)
            def _():
                acc_ref[...] += part

            @pl.when(k == nk - 1)
            def _():
                finish(acc_ref[...])

    operands = [a, b]
    in_specs = [a_spec, b_spec]
    vmem = 2 * (_nbytes(a_spec.block_shape, a.dtype) + _nbytes(b_spec.block_shape, b.dtype)
                + _nbytes(out_spec.block_shape, out_shape.dtype))
    vmem += 3 * _nbytes(acc_shape, F32)
    if has_res:
        operands.append(res)
        in_specs.append(res_spec)
        vmem += 2 * _nbytes(res_spec.block_shape, res.dtype)
    scratch = [pltpu.VMEM(acc_shape, F32)] if nk > 1 else []
    return _call(body, name, grid, in_specs, out_spec, out_shape, scratch=scratch, vmem_bytes=vmem)(*operands)


def _mm_colblock(name, h, w_g, layer):
    t, k = h.shape
    nb = w_g.shape[3]
    tm = _tile(t, 2048)
    return _mm(
        name, NN, h, w_g, grid=(t // tm, N_DEV, 1),
        a_spec=pl.BlockSpec((tm, k), lambda i, j, kk: (i, 0)),
        b_spec=pl.BlockSpec((None, None, k, nb), lambda i, j, kk: (layer, j, 0, 0)),
        out_shape=jax.ShapeDtypeStruct((t, N_DEV * nb), BF16),
        out_spec=pl.BlockSpec((tm, nb), lambda i, j, kk: (i, j)), acc_shape=(tm, nb))


def _mm_natural(name, a, w, layer, *, res=None, out_dtype=F32, scale=None):
    t, k = a.shape
    n = w.shape[2]
    tm = _tile(t, 1024)
    tn = _tile(n, 512)
    res_spec = None if res is None else pl.BlockSpec((tm, tn), lambda i, j, kk: (i, j))
    return _mm(
        name, NN, a, w, grid=(t // tm, n // tn, 1),
        a_spec=pl.BlockSpec((tm, k), lambda i, j, kk: (i, 0)),
        b_spec=pl.BlockSpec((None, k, tn), lambda i, j, kk: (layer, 0, j)),
        out_shape=jax.ShapeDtypeStruct((t, n), out_dtype),
        out_spec=pl.BlockSpec((tm, tn), lambda i, j, kk: (i, j)),
        acc_shape=(tm, tn), res=res, res_spec=res_spec, scale=scale)


def _mm_down(name, act, w4, layer, res):
    nblk, t, kb = act.shape
    n = w4.shape[3]
    tm = _tile(t, 1024)

    def body(a_ref, b_ref, r_ref, o_ref):
        acc = r_ref[...]
        for u in range(nblk):
            acc = acc + jnp.dot(a_ref[u], b_ref[u], preferred_element_type=F32)
        o_ref[...] = acc

    row = pl.BlockSpec((tm, n), lambda i: (i, 0))
    return _call(
        body, name, (t // tm,),
        [pl.BlockSpec((nblk, tm, kb), lambda i: (0, i, 0)),
         pl.BlockSpec((None, nblk, kb, n), lambda i: (layer, 0, 0, 0)),
         row],
        row,
        jax.ShapeDtypeStruct((t, n), F32),
        vmem_bytes=2 * (_nbytes((nblk, tm, kb), BF16) + _nbytes((nblk, kb, n), BF16)) + 6 * _nbytes((tm, n), F32),
    )(act, w4, res)


def _mm_t_colblock_norm_bwd(name, dz, w_g, layer, x, g, dx_up, blocked_in=False):
    k = w_g.shape[2]
    nb = w_g.shape[3]
    t = x.shape[0]
    tm = _tile(t, 1024)
    per_step = 2
    n_steps = N_DEV // per_step
    if blocked_in:
        a_spec = pl.BlockSpec((per_step, tm, nb), lambda i, kk: (kk, i, 0))
    else:
        a_spec = pl.BlockSpec((tm, per_step * nb), lambda i, kk: (i, kk))

    def body(a_ref, b_ref, x_ref, g_ref, up_ref, dx_ref, dg_ref, acc_ref):
        i = pl.program_id(0)
        kk = pl.program_id(1)
        part = None
        for u in range(per_step):
            a = a_ref[u] if blocked_in else a_ref[:, u * nb:(u + 1) * nb]
            term = lax.dot_general(a.astype(BF16), b_ref[u].astype(BF16), NT, preferred_element_type=F32)
            part = term if part is None else part + term

        @pl.when(kk == 0)
        def _():
            acc_ref[...] = part

        @pl.when(kk > 0)
        def _():
            acc_ref[...] += part

        @pl.when((i == 0) & (kk == 0))
        def _():
            dg_ref[...] = jnp.zeros_like(dg_ref)

        @pl.when(kk == n_steps - 1)
        def _():
            dy = acc_ref[...]
            xf = x_ref[...]
            r = lax.rsqrt(jnp.mean(xf * xf, axis=-1, keepdims=True) + EPS)
            xhat = xf * r
            dxhat = dy * g_ref[...]
            dg_ref[...] += jnp.sum(dy * xhat, axis=0, keepdims=True)
            dx_ref[...] = up_ref[...] + r * (dxhat - xhat * jnp.mean(dxhat * xhat, axis=-1, keepdims=True))

    row = pl.BlockSpec((tm, k), lambda i, kk: (i, 0))
    vec = pl.BlockSpec((1, k), lambda i, kk: (0, 0))
    dx, dg = _call(
        body, name, (t // tm, n_steps),
        [a_spec, pl.BlockSpec((None, per_step, k, nb), lambda i, kk: (layer, kk, 0, 0)), row, vec, row],
        [row, vec],
        [jax.ShapeDtypeStruct((t, k), F32), jax.ShapeDtypeStruct((1, k), F32)],
        scratch=[pltpu.VMEM((tm, k), F32)],
        vmem_bytes=2 * per_step * (_nbytes((tm, nb), BF16) + _nbytes((k, nb), BF16)) + 10 * _nbytes((tm, k), F32),
    )(dz, w_g, x, g.reshape(1, k), dx_up)
    return dx, dg.reshape(k)


def _ffn_gate_up(name, h, w_g, layer):
    t, k = h.shape
    nb = w_g.shape[3]
    half = N_DEV // 2
    tm = _tile(t, 1024)

    def body(h_ref, wg_ref, wu_ref, gu_ref, act_ref):
        hb = h_ref[...]
        gate = jnp.dot(hb, wg_ref[...], preferred_element_type=F32)
        up = jnp.dot(hb, wu_ref[...], preferred_element_type=F32)
        gu_ref[0] = gate.astype(BF16)
        gu_ref[1] = up.astype(BF16)
        act_ref[...] = (gate * _sigmoid(gate) * up).astype(BF16)

    return _call(
        body, name, (t // tm, half),
        [pl.BlockSpec((tm, k), lambda i, j: (i, 0)),
         pl.BlockSpec((None, None, k, nb), lambda i, j: (layer, j, 0, 0)),
         pl.BlockSpec((None, None, k, nb), lambda i, j: (layer, half + j, 0, 0))],
        [pl.BlockSpec((2, None, tm, nb), lambda i, j: (0, j, i, 0)),
         pl.BlockSpec((None, tm, nb), lambda i, j: (j, i, 0))],
        [jax.ShapeDtypeStruct((2, half, t, nb), BF16), jax.ShapeDtypeStruct((half, t, nb), BF16)],
        vmem_bytes=2 * (_nbytes((tm, k), BF16) + 2 * _nbytes((k, nb), BF16) + 3 * _nbytes((tm, nb), BF16))
        + 6 * _nbytes((tm, nb), F32),
    )(h, w_g, w_g)


def _ffn_down_dx(name, dy, w4, layer, gu4):
    t, n = dy.shape
    nblk, kb = w4.shape[1], w4.shape[2]
    tm = _tile(t, 1024)

    def body(dy_ref, w_ref, gu_ref, dgu_ref):
        da = lax.dot_general(dy_ref[...].astype(BF16), w_ref[...], NT, preferred_element_type=F32)
        gate = gu_ref[0].astype(F32)
        up = gu_ref[1].astype(F32)
        sig = _sigmoid(gate)
        dgu_ref[0] = (da * up * (sig * (1.0 + gate * (1.0 - sig)))).astype(BF16)
        dgu_ref[1] = (da * (gate * sig)).astype(BF16)

    blk = pl.BlockSpec((2, None, tm, kb), lambda i, j: (0, j, i, 0))
    return _call(
        body, name, (t // tm, nblk),
        [pl.BlockSpec((tm, n), lambda i, j: (i, 0)),
         pl.BlockSpec((None, None, kb, n), lambda i, j: (layer, j, 0, 0)),
         blk],
        blk,
        jax.ShapeDtypeStruct((2, nblk, t, kb), BF16),
        vmem_bytes=2 * (_nbytes((tm, n), F32) + _nbytes((kb, n), BF16) + 4 * _nbytes((tm, kb), BF16))
        + 8 * _nbytes((tm, kb), F32),
    )(dy, w4, gu4)


def _mm_t_natural(name, dy, w, layer):
    t, n = dy.shape
    k = w.shape[1]
    tm = _tile(t, 1024)
    tk = _tile(k, 512)
    return _mm(
        name, NT, dy, w, grid=(t // tm, k // tk, 1),
        a_spec=pl.BlockSpec((tm, n), lambda i, j, kk: (i, 0)),
        b_spec=pl.BlockSpec((None, tk, n), lambda i, j, kk: (layer, j, 0)),
        out_shape=jax.ShapeDtypeStruct((t, k), BF16),
        out_spec=pl.BlockSpec((tm, tk), lambda i, j, kk: (i, j)),
        acc_shape=(tm, tk))


def _mm_dw_colblock(name, h, dz, blocked_in=False, transposed=False):
    t, k = h.shape
    nb = dz.shape[2] if blocked_in else dz.shape[1] // N_DEV
    tk = _tile(t, 2048)
    h_spec = pl.BlockSpec((tk, k), lambda i, j, kk: (kk, 0))
    if blocked_in:
        dz_spec = pl.BlockSpec((None, tk, nb), lambda i, j, kk: (j, kk, 0))
    else:
        dz_spec = pl.BlockSpec((tk, nb), lambda i, j, kk: (kk, j))
    rows, cols = (nb, k) if transposed else (k, nb)
    return _mm(
        name, TN, *((dz, h) if transposed else (h, dz)), grid=(1, N_DEV, t // tk),
        a_spec=dz_spec if transposed else h_spec,
        b_spec=h_spec if transposed else dz_spec,
        out_shape=jax.ShapeDtypeStruct((N_DEV, rows, cols), BF16),
        out_spec=pl.BlockSpec((None, rows, cols), lambda i, j, kk: (j, 0, 0)),
        acc_shape=(rows, cols))


def _mm_dw_natural(name, a, dy):
    t, k = a.shape
    n = dy.shape[1]
    tko = _tile(k, 1024)
    tt = _tile(t, 2048)
    out = _mm(
        name, TN, a, dy, grid=(k // tko, 1, t // tt),
        a_spec=pl.BlockSpec((tt, tko), lambda i, j, kk: (kk, i)),
        b_spec=pl.BlockSpec((tt, n), lambda i, j, kk: (kk, 0)),
        out_shape=jax.ShapeDtypeStruct((k, n), BF16),
        out_spec=pl.BlockSpec((tko, n), lambda i, j, kk: (i, 0)),
        acc_shape=(tko, n))
    return out.reshape(N_DEV, k // N_DEV, n)


def _mm_dw_down(name, act, dy):
    nblk, t, kb = act.shape
    n = dy.shape[1]
    tt = _tile(t, 2048)
    out = _mm(
        name, TN, act, dy, grid=(nblk, 1, t // tt),
        a_spec=pl.BlockSpec((None, tt, kb), lambda i, j, kk: (i, kk, 0)),
        b_spec=pl.BlockSpec((tt, n), lambda i, j, kk: (kk, 0)),
        out_shape=jax.ShapeDtypeStruct((nblk, kb, n), BF16),
        out_spec=pl.BlockSpec((None, kb, n), lambda i, j, kk: (i, 0, 0)),
        acc_shape=(kb, n))
    return out.reshape(N_DEV, (nblk * kb) // N_DEV, n)


def _spatial_mask(transposed=False):
    r = lax.broadcasted_iota(jnp.int32, (A_CHUNK, A_CHUNK), 0) // CHUNK
    c = lax.broadcasted_iota(jnp.int32, (A_CHUNK, A_CHUNK), 1) // CHUNK
    return c >= r if transposed else r >= c


def _sgu_tile(t):
    return _tile(t, 2 * A_CHUNK)


def _sgu_fwd(zpre, g_sgu, w_sp, b_full, name):
    t, f2 = zpre.shape
    f = f2 // 2
    gd = f // A_GROUPS
    tm = _sgu_tile(t)

    def body(z_ref, g_ref, w_ref, b_ref, p_ref, zs_ref, dg_ref):
        mask = _spatial_mask()
        wm = [jnp.where(mask, w_ref[g], 0.0).astype(BF16) for g in range(A_GROUPS)]
        for c in range(tm // A_CHUNK):
            rows = pl.ds(c * A_CHUNK, A_CHUNK)
            z, dgelu = _gelu_and_grad(z_ref[rows, :].astype(F32))
            zs_ref[rows, :] = z.astype(BF16)
            dg_ref[rows, :] = dgelu.astype(BF16)
            u = z[:, :f]
            v0 = z[:, f:]
            r = lax.rsqrt(jnp.mean(v0 * v0, axis=-1, keepdims=True) + EPS)
            v1 = (v0 * r * g_ref[...]).astype(BF16)
            for g in range(A_GROUPS):
                cols = slice(g * gd, (g + 1) * gd)
                v2 = jnp.dot(wm[g], v1[:, cols], preferred_element_type=F32) + b_ref[:, cols]
                p_ref[rows, cols] = (u[:, cols] * v2).astype(BF16)

    return _call(
        body, name, (t // tm,),
        [pl.BlockSpec((tm, f2), lambda i: (i, 0)),
         pl.BlockSpec((1, f), lambda i: (0, 0)),
         pl.BlockSpec((A_GROUPS, A_CHUNK, A_CHUNK), lambda i: (0, 0, 0)),
         pl.BlockSpec((A_CHUNK, f), lambda i: (0, 0))],
        [pl.BlockSpec((tm, f), lambda i: (i, 0)), pl.BlockSpec((tm, f2), lambda i: (i, 0)),
         pl.BlockSpec((tm, f2), lambda i: (i, 0))],
        [jax.ShapeDtypeStruct((t, f), BF16), jax.ShapeDtypeStruct((t, f2), BF16), jax.ShapeDtypeStruct((t, f2), BF16)],
        vmem_bytes=6 * _nbytes((tm, f2), BF16) + 2 * _nbytes((tm, f), BF16) + 8 * _nbytes((A_CHUNK, f2), F32),
    )(zpre, g_sgu.reshape(1, f), w_sp, b_full)


def _sgu_bwd(zs, dgs, dp, g_sgu, w_sp, w_sp_t, b_full, name):
    t, f2 = zs.shape
    f = f2 // 2
    gd = f // A_GROUPS
    tm = _sgu_tile(t)
    n_steps = t // tm

    def body(z_ref, dgelu_ref, dp_ref, g_ref, w_ref, wt_ref, b_ref, dz_ref, dw_ref, db_ref, dg_ref, dv1_ref, dbf_ref):
        step = pl.program_id(0)

        @pl.when(step == 0)
        def _():
            dw_ref[...] = jnp.zeros_like(dw_ref)
            dg_ref[...] = jnp.zeros_like(dg_ref)
            dbf_ref[...] = jnp.zeros_like(dbf_ref)

        mask = _spatial_mask()
        mask_t = _spatial_mask(transposed=True)
        wm = [jnp.where(mask, w_ref[g], 0.0).astype(BF16) for g in range(A_GROUPS)]
        wmt = [jnp.where(mask_t, wt_ref[g], 0.0).astype(BF16) for g in range(A_GROUPS)]
        gain = g_ref[...]
        for c in range(tm // A_CHUNK):
            rows = pl.ds(c * A_CHUNK, A_CHUNK)
            z = z_ref[rows, :].astype(F32)
            dgelu = dgelu_ref[rows, :].astype(F32)
            u = z[:, :f]
            v0 = z[:, f:]
            r = lax.rsqrt(jnp.mean(v0 * v0, axis=-1, keepdims=True) + EPS)
            xhat = v0 * r
            v1 = (xhat * gain).astype(BF16)
            dpf = dp_ref[rows, :].astype(F32)
            for g in range(A_GROUPS):
                cols = slice(g * gd, (g + 1) * gd)
                v1g = v1[:, cols]
                v2 = jnp.dot(wm[g], v1g, preferred_element_type=F32) + b_ref[:, cols]
                dpg = dpf[:, cols]
                dz_ref[rows, cols] = (dpg * v2 * dgelu[:, cols]).astype(BF16)
                dv2 = dpg * u[:, cols]
                dbf_ref[:, cols] += dv2
                dv2b = dv2.astype(BF16)
                dwg = lax.dot_general(dv2b, v1g, NT, preferred_element_type=F32)
                dw_ref[g] += jnp.where(mask, dwg, 0.0)
                dv1_ref[:, cols] = jnp.dot(wmt[g], dv2b, preferred_element_type=F32)
            dv1 = dv1_ref[...]
            dxhat = dv1 * gain
            dg_ref[...] += jnp.sum(dv1 * xhat, axis=0, keepdims=True)
            dv0 = r * (dxhat - xhat * jnp.mean(dxhat * xhat, axis=-1, keepdims=True))
            dz_ref[rows, pl.ds(f, f)] = (dv0 * dgelu[:, f:]).astype(BF16)

        @pl.when(step == n_steps - 1)
        def _():
            for g in range(A_GROUPS):
                db_ref[g] = jnp.sum(dbf_ref[:, g * gd:(g + 1) * gd], axis=1, keepdims=True)

    wspec = pl.BlockSpec((A_GROUPS, A_CHUNK, A_CHUNK), lambda i: (0, 0, 0))
    dz, dw, db, dg = _call(
        body, name, (n_steps,),
        [pl.BlockSpec((tm, f2), lambda i: (i, 0)),
         pl.BlockSpec((tm, f2), lambda i: (i, 0)),
         pl.BlockSpec((tm, f), lambda i: (i, 0)),
         pl.BlockSpec((1, f), lambda i: (0, 0)),
         wspec, wspec,
         pl.BlockSpec((A_CHUNK, f), lambda i: (0, 0))],
        [pl.BlockSpec((tm, f2), lambda i: (i, 0)),
         wspec,
         pl.BlockSpec((A_GROUPS, A_CHUNK, 1), lambda i: (0, 0, 0)),
         pl.BlockSpec((1, f), lambda i: (0, 0))],
        [jax.ShapeDtypeStruct((t, f2), BF16),
         jax.ShapeDtypeStruct((A_GROUPS, A_CHUNK, A_CHUNK), F32),
         jax.ShapeDtypeStruct((A_GROUPS, A_CHUNK, 1), F32),
         jax.ShapeDtypeStruct((1, f), F32)],
        scratch=[pltpu.VMEM((A_CHUNK, f), F32), pltpu.VMEM((A_CHUNK, f), F32)],
        vmem_bytes=6 * _nbytes((tm, f2), BF16) + 2 * _nbytes((tm, f), BF16) + 12 * _nbytes((A_CHUNK, f2), F32),
    )(zs, dgs, dp, g_sgu.reshape(1, f), w_sp, w_sp_t, b_full)
    return dz, dw, db.reshape(A_GROUPS, A_CHUNK), dg.reshape(f)


def _pair_valid(qi, col):
    qc = qi // CHUNK
    kc = col // CHUNK
    return (kc >= qc) & (kc <= qc + N_LEFT_CHUNKS)


def _diagonal_onehot():
    e = lax.broadcasted_iota(jnp.int32, (REL_PAD, DIAGONALS), 1)
    idx = jnp.clip(PAIR_BAND - 1 - e, -MAX_REL, MAX_REL) + MAX_REL
    r = lax.broadcasted_iota(jnp.int32, (REL_PAD, DIAGONALS), 0)
    return jnp.where(r == idx, 1.0, 0.0).astype(BF16)


def _bias_build(table, name):
    h = table.shape[0]
    tab = jnp.pad(table, ((0, 0), (0, REL_PAD - N_REL)))

    def body(t_ref, o_ref):
        oh = _diagonal_onehot()
        diag = jnp.zeros((h, DIAGONALS), F32)
        for piece in _split3(t_ref[...]):
            diag += jnp.dot(piece, oh, preferred_element_type=F32)
        col = lax.broadcasted_iota(jnp.int32, (h, PAIR_BAND), 1)
        for qi in range(PAIR_ROWS):
            row = pltpu.roll(diag, (qi - (PAIR_ROWS - 1)) % DIAGONALS, 1)[:, :PAIR_BAND]
            o_ref[qi] = jnp.where(_pair_valid(qi, col), row, NEG_INF)

    out = _call(
        body, name, (1,),
        [pl.BlockSpec((h, REL_PAD), lambda i: (0, 0))],
        pl.BlockSpec((PAIR_ROWS, h, PAIR_BAND), lambda i: (0, 0, 0)),
        jax.ShapeDtypeStruct((PAIR_ROWS, h, PAIR_BAND), F32),
        vmem_bytes=4 * _nbytes((PAIR_ROWS, h, PAIR_BAND), F32),
    )(tab)
    return jnp.transpose(out, (1, 0, 2))


def _bias_block(pair_bias):
    rest = K_BLOCK - PAIR_BAND
    return jnp.concatenate(
        [jnp.pad(pair_bias, ((0, 0), (0, 0), (p * PAIR_ROWS, rest - p * PAIR_ROWS)), constant_values=NEG_INF)
         for p in range(PAIRS_PER_BLOCK)], axis=1)


def _bias_grad(dbias, name):
    h = dbias.shape[0]
    db_t = jnp.transpose(dbias, (1, 0, 2))

    def body(d_ref, o_ref):
        diag = jnp.zeros((h, DIAGONALS), F32)
        for qi in range(PAIR_ROWS):
            diag += pltpu.roll(d_ref[qi], PAIR_ROWS - 1 - qi, 1)
        oh = _diagonal_onehot()
        acc = jnp.zeros((h, REL_PAD), F32)
        for piece in _split3(diag):
            acc += lax.dot_general(piece, oh, NT, preferred_element_type=F32)
        o_ref[...] = acc

    out = _call(
        body, name, (1,),
        [pl.BlockSpec((PAIR_ROWS, h, DIAGONALS), lambda i: (0, 0, 0))],
        pl.BlockSpec((h, REL_PAD), lambda i: (0, 0)),
        jax.ShapeDtypeStruct((h, REL_PAD), F32),
        vmem_bytes=4 * _nbytes((PAIR_ROWS, h, DIAGONALS), F32),
    )(db_t)
    return out[:, :N_REL]


def _head_masks():
    lane = lax.broadcasted_iota(jnp.int32, (Q_BLOCK, HEAD_PAIR), 1)
    return lane < HEAD_DIM, lane >= HEAD_DIM


def _block_scores(qm, kb, bias, valid):
    s = lax.dot_general(qm, kb, NT, preferred_element_type=F32) + bias
    return s if valid is None else jnp.where(valid, s, NEG_INF)


def _softmax_rows(s):
    e = jnp.exp(s - jnp.max(s, axis=-1, keepdims=True))
    return e * (1.0 / jnp.sum(e, axis=-1, keepdims=True))


def _block_probs(qm, kb, bias, valid):
    return _softmax_rows(_block_scores(qm, kb, bias, valid))


def _padded_then_plain(step, n_blocks):
    n_padded = min(LEFT // Q_BLOCK, n_blocks)
    lax.fori_loop(0, n_padded, lambda j, c: step(j, c, True), 0)
    lax.fori_loop(n_padded, n_blocks, lambda j, c: step(j, c, False), 0, unroll=ATTN_UNROLL)


def _attn_fwd(q, kvpad, bias, name):
    t, d = q.shape
    n_pairs = d // HEAD_PAIR
    n_blocks = t // Q_BLOCK

    def body(q_ref, k_ref, v_ref, b_ref, o_ref):
        masks = _head_masks()
        key = lax.broadcasted_iota(jnp.int32, (Q_BLOCK, K_BLOCK), 1)

        def step(j, carry, padded):
            r0 = pl.multiple_of(j * Q_BLOCK, Q_BLOCK)
            q2 = q_ref[pl.ds(r0, Q_BLOCK), :].astype(F32)
            kb = k_ref[pl.ds(r0, K_BLOCK), :]
            vb = v_ref[pl.ds(r0, K_BLOCK), :]
            valid = key >= LEFT - j * Q_BLOCK if padded else None
            scores = [_block_scores(jnp.where(masks[a], q2, 0.0).astype(BF16), kb, b_ref[a], valid) for a in range(2)]
            probs = [_softmax_rows(s).astype(BF16) for s in scores]
            outs = [jnp.dot(p, vb, preferred_element_type=F32) for p in probs]
            o_ref[pl.ds(r0, Q_BLOCK), :] = jnp.where(masks[0], outs[0], outs[1]).astype(BF16)
            return carry

        _padded_then_plain(step, n_blocks)

    return _call(
        body, name, (n_pairs,),
        [pl.BlockSpec((t, HEAD_PAIR), lambda p: (0, p)),
         pl.BlockSpec((LEFT + t, HEAD_PAIR), lambda p: (0, p)),
         pl.BlockSpec((LEFT + t, HEAD_PAIR), lambda p: (0, n_pairs + p)),
         pl.BlockSpec((2, Q_BLOCK, K_BLOCK), lambda p: (p, 0, 0))],
        pl.BlockSpec((t, HEAD_PAIR), lambda p: (0, p)),
        jax.ShapeDtypeStruct((t, d), BF16),
        vmem_bytes=8 * _nbytes((LEFT + t, HEAD_PAIR), BF16) + 12 * _nbytes((2, Q_BLOCK, K_BLOCK), F32),
    )(q, kvpad, kvpad, bias)


def _attn_bwd(q, kvpad, bias, do, dk_in, dv_in, name):
    t, d = q.shape
    n_pairs = d // HEAD_PAIR
    n_blocks = t // Q_BLOCK
    has_in = dk_in is not None

    def body(*refs):
        refs = list(refs)
        q_ref, k_ref, v_ref, b_ref, do_ref = refs[:5]
        refs = refs[5:]
        if has_in:
            dki_ref, dvi_ref = refs[:2]
            refs = refs[2:]
        dq_ref, dk_ref, dv_ref, db_ref = refs
        masks = _head_masks()
        key = lax.broadcasted_iota(jnp.int32, (Q_BLOCK, K_BLOCK), 1)
        if has_in:
            dk_ref[...] = dki_ref[...]
            dv_ref[...] = dvi_ref[...]
        else:
            dk_ref[...] = jnp.zeros_like(dk_ref)
            dv_ref[...] = jnp.zeros_like(dv_ref)
        db_ref[...] = jnp.zeros_like(db_ref)

        def step(j, carry, padded):
            r0 = pl.multiple_of(j * Q_BLOCK, Q_BLOCK)
            q2 = q_ref[pl.ds(r0, Q_BLOCK), :].astype(F32)
            do2 = do_ref[pl.ds(r0, Q_BLOCK), :].astype(F32)
            kb = k_ref[pl.ds(r0, K_BLOCK), :]
            vb = v_ref[pl.ds(r0, K_BLOCK), :]
            valid = key >= LEFT - j * Q_BLOCK if padded else None
            heads = range(2)
            qms = [jnp.where(masks[a], q2, 0.0).astype(BF16) for a in heads]
            doms = [jnp.where(masks[a], do2, 0.0).astype(BF16) for a in heads]
            scores = [_block_scores(qms[a], kb, b_ref[a], valid) for a in heads]
            dps = [lax.dot_general(doms[a], vb, NT, preferred_element_type=F32) for a in heads]
            ps = [_softmax_rows(s) for s in scores]
            dss = [ps[a] * (dps[a] - jnp.sum(dps[a] * ps[a], axis=-1, keepdims=True)) for a in heads]
            for a in heads:
                for pair in range(PAIRS_PER_BLOCK):
                    lo = pair * PAIR_ROWS
                    db_ref[a, :, pl.ds(0, PAIR_BAND)] += dss[a][lo:lo + PAIR_ROWS, lo:lo + PAIR_BAND]
            dsbs = [ds.astype(BF16) for ds in dss]
            pbs = [p.astype(BF16) for p in ps]
            dqs = [jnp.dot(dsbs[a], kb, preferred_element_type=F32) for a in heads]
            dk_acc = sum(lax.dot_general(dsbs[a], qms[a], TN, preferred_element_type=F32) for a in heads)
            dv_acc = sum(lax.dot_general(pbs[a], doms[a], TN, preferred_element_type=F32) for a in heads)
            dq = jnp.where(masks[0], dqs[0], dqs[1]) * ATTN_SCALE
            dq_ref[pl.ds(r0, Q_BLOCK), :] = dq.astype(BF16)
            dk_ref[pl.ds(r0, K_BLOCK), :] += dk_acc
            dv_ref[pl.ds(r0, K_BLOCK), :] += dv_acc
            return carry

        _padded_then_plain(step, n_blocks)

    q_spec = pl.BlockSpec((t, HEAD_PAIR), lambda p: (0, p))
    kv_spec = pl.BlockSpec((LEFT + t, HEAD_PAIR), lambda p: (0, p))
    operands = [q, kvpad, kvpad, bias, do]
    in_specs = [q_spec, kv_spec, pl.BlockSpec((LEFT + t, HEAD_PAIR), lambda p: (0, n_pairs + p)),
                pl.BlockSpec((2, Q_BLOCK, K_BLOCK), lambda p: (p, 0, 0)), q_spec]
    aliases = None
    if has_in:
        operands += [dk_in, dv_in]
        in_specs += [kv_spec, kv_spec]
        aliases = {5: 1, 6: 2}
    return _call(
        body, name, (n_pairs,),
        in_specs,
        [q_spec, kv_spec, kv_spec, pl.BlockSpec((2, PAIR_ROWS, DIAGONALS), lambda p: (p, 0, 0))],
        [jax.ShapeDtypeStruct((t, d), BF16),
         jax.ShapeDtypeStruct((LEFT + t, d), F32),
         jax.ShapeDtypeStruct((LEFT + t, d), F32),
         jax.ShapeDtypeStruct((d // HEAD_DIM, PAIR_ROWS, DIAGONALS), F32)],
        vmem_bytes=10 * _nbytes((LEFT + t, HEAD_PAIR), BF16) + 8 * _nbytes((LEFT + t, HEAD_PAIR), F32)
        + 16 * _nbytes((2, Q_BLOCK, K_BLOCK), F32),
        aliases=aliases,
    )(*operands)


def _loss_head(x, g, target, name):
    t, d = x.shape
    tm = _tile(t, 512)

    def body(x_ref, g_ref, t_ref, dx_ref, loss_ref, dg_ref):
        @pl.when(pl.program_id(0) == 0)
        def _():
            loss_ref[...] = jnp.zeros_like(loss_ref)
            dg_ref[...] = jnp.zeros_like(dg_ref)

        xf = x_ref[...]
        r = lax.rsqrt(jnp.mean(xf * xf, axis=-1, keepdims=True) + EPS)
        xhat = xf * r
        diff = xhat * g_ref[...] - t_ref[...]
        row_loss = jnp.mean(diff * diff, axis=-1, keepdims=True)
        loss_ref[...] += 0.5 * jnp.sum(row_loss, axis=0, keepdims=True)
        dy = diff * (1.0 / d)
        dg_ref[...] += jnp.sum(dy * xhat, axis=0, keepdims=True)
        dxhat = dy * g_ref[...]
        dx_ref[...] = r * (dxhat - xhat * jnp.mean(dxhat * xhat, axis=-1, keepdims=True))

    row = pl.BlockSpec((tm, d), lambda i: (i, 0))
    vec = pl.BlockSpec((1, d), lambda i: (0, 0))
    dx, loss, dg = _call(
        body, name, (t // tm,),
        [row, vec, row],
        [row, pl.BlockSpec((1, 1), lambda i: (0, 0)), vec],
        [jax.ShapeDtypeStruct((t, d), F32), jax.ShapeDtypeStruct((1, 1), F32), jax.ShapeDtypeStruct((1, d), F32)],
        vmem_bytes=10 * _nbytes((tm, d), F32),
    )(x, g.reshape(1, d), target)
    return dx, loss[0, 0], dg.reshape(d)


def _adamw_store(g, w_ref, m_ref, v_ref, g_ref, d_ref, nm_ref, nv_ref):
    c1 = 1.0 / (1.0 - ADAM_B1 ** ADAM_STEP)
    c2 = 1.0 / (1.0 - ADAM_B2 ** ADAM_STEP)
    nm = ADAM_B1 * m_ref[...] + (1.0 - ADAM_B1) * g
    nv = ADAM_B2 * v_ref[...] + (1.0 - ADAM_B2) * (g * g)
    g_ref[...] = g
    nm_ref[...] = nm
    nv_ref[...] = nv
    d_ref[...] = -ADAM_LR * ((nm * c1) / (jnp.sqrt(nv * c2) + ADAM_EPS) + ADAM_WD * w_ref[...])


def _adamw_layer(recv, own, w, m, v, layer, prev, me, name):
    n_src, r, c = recv.shape
    tr = _row_tile(r, max(16, (256 * 1024) // c), 16)
    first = prev is None

    def body(me_ref, recv_ref, own_ref, w_ref, m_ref, v_ref, *rest):
        mine = me_ref[0]
        own_part = own_ref[...].astype(F32)
        g = None
        for s in range(n_src):
            part = jnp.where(mine == s, own_part, recv_ref[s].astype(F32))
            g = part if g is None else g + part
        _adamw_store(g, w_ref, m_ref, v_ref, *rest[-4:])

    blk = pl.BlockSpec((None, tr, c), lambda i, me_ref: (layer, i, 0))
    any_spec = pl.BlockSpec(memory_space=pl.ANY)
    out = jax.ShapeDtypeStruct(w.shape, F32)
    operands = [me, recv, own, w, m, v] + ([] if first else list(prev))
    vmem = 2 * _nbytes((n_src + 1, tr, c), BF16) + 18 * _nbytes((tr, c), F32)
    return pl.pallas_call(
        body,
        name=name,
        grid_spec=pltpu.PrefetchScalarGridSpec(
            num_scalar_prefetch=1,
            grid=(r // tr,),
            in_specs=[pl.BlockSpec((n_src, tr, c), lambda i, me_ref: (0, i, 0)),
                      pl.BlockSpec((None, tr, c), lambda i, me_ref: (me_ref[0], i, 0)),
                      blk, blk, blk] + ([] if first else [any_spec] * 4),
            out_specs=[blk, blk, blk, blk],
        ),
        out_shape=[out, out, out, out],
        input_output_aliases={} if first else {6 + j: j for j in range(4)},
        compiler_params=pltpu.CompilerParams(
            dimension_semantics=("arbitrary",),
            vmem_limit_bytes=int(min(max(VMEM_FLOOR_BYTES, vmem * 5 // 4), VMEM_CEIL_BYTES))),
    )(*operands)


def _adamw(parts, w, m, v, name):
    n_layers, n_src, r, c = parts.shape
    mult = 16 if parts.dtype == BF16 else 8
    tr = _row_tile(r, max(mult, (256 * 1024) // c), mult)

    def body(p_ref, w_ref, m_ref, v_ref, g_ref, d_ref, nm_ref, nv_ref):
        g = p_ref[0].astype(F32)
        for s in range(1, n_src):
            g = g + p_ref[s].astype(F32)
        _adamw_store(g, w_ref, m_ref, v_ref, g_ref, d_ref, nm_ref, nv_ref)

    blk = pl.BlockSpec((None, tr, c), lambda l, i: (l, i, 0))
    out = jax.ShapeDtypeStruct((n_layers, r, c), F32)
    return _call(
        body, name, (n_layers, r // tr),
        [pl.BlockSpec((None, n_src, tr, c), lambda l, i: (l, 0, i, 0)), blk, blk, blk],
        [blk, blk, blk, blk],
        [out, out, out, out],
        vmem_bytes=2 * _nbytes((n_src, tr, c), parts.dtype) + 18 * _nbytes((tr, c), F32),
    )(parts, w, m, v)


def _ordered_sum(parts, name):
    n_src, r, c = parts.shape

    def body(p_ref, o_ref):
        acc = p_ref[0]
        for s in range(1, n_src):
            acc = acc + p_ref[s]
        o_ref[...] = acc

    return _call(
        body, name, (1,),
        [pl.BlockSpec((n_src, r, c), lambda i: (0, 0, 0))],
        pl.BlockSpec((r, c), lambda i: (0, 0)),
        jax.ShapeDtypeStruct((r, c), F32),
        vmem_bytes=4 * _nbytes((n_src, r, c), F32),
    )(parts)


def _position():
    return lax.axis_index("x"), lax.axis_index("y"), lax.axis_index("c")


def _linear(p):
    return 4 * p[0] + 2 * p[1] + p[2]


def _all_gather(shards, name):
    n = len(shards)

    def body(*refs):
        ins, outs = refs[:n], refs[n:2 * n]
        send_sems, recv_sems, local_sems = refs[2 * n:]
        x, y, c = _position()
        me, sibling = (x, y, c), (x, y, 1 - c)
        chips = [(1 - x, y), (x, 1 - y), (1 - x, 1 - y)]

        def slab(t, p):
            return outs[t].at[:, _linear(p)]

        def copy(t, k, block, to, src=None):
            return pltpu.make_async_remote_copy(
                src_ref=slab(t, block) if src is None else src,
                dst_ref=slab(t, block),
                send_sem=send_sems.at[t, k],
                recv_sem=recv_sems.at[t, k],
                device_id=to,
                device_id_type=MESH,
            )

        started = []
        for t in range(n):
            mine = pltpu.make_async_copy(ins[t], slab(t, me), local_sems.at[t])
            mine.start()
            started.append(mine)
        sends = []
        for t in range(n):
            first = [copy(t, 0, me, sibling, src=ins[t])]
            first += [copy(t, 1 + j, me, (*chip, c), src=ins[t]) for j, chip in enumerate(chips)]
            for cp in first:
                cp.start()
            sends += first
        for t in range(n):
            for j, chip in enumerate(chips):
                copy(t, 1 + j, (*chip, c), me).wait_recv()
                passed = copy(t, 4 + j, (*chip, c), sibling)
                passed.start()
                sends.append(passed)
        for t in range(n):
            copy(t, 0, sibling, me).wait_recv()
            for j, chip in enumerate(chips):
                copy(t, 4 + j, (*chip, 1 - c), me).wait_recv()
        for cp in sends:
            cp.wait_send()
        for mine in started:
            mine.wait()

    out_shape = [jax.ShapeDtypeStruct((s.shape[0], N_DEV) + s.shape[1:], s.dtype) for s in shards]
    return pl.pallas_call(
        body,
        name=name,
        in_specs=[HBM_SPEC] * n,
        out_specs=[HBM_SPEC] * n,
        out_shape=out_shape,
        scratch_shapes=[
            pltpu.SemaphoreType.DMA((n, N_DEV - 1)),
            pltpu.SemaphoreType.DMA((n, N_DEV - 1)),
            pltpu.SemaphoreType.DMA((n,)),
        ],
    )(*shards)


def _exchange(blocks, name):
    n = len(blocks)

    def body(*refs):
        ins, outs = refs[:n], refs[n:2 * n]
        send_sems, recv_sems, local_sems = refs[2 * n:]
        x, y, c = _position()
        me = _linear((x, y, c))
        flips = [(fx, fy, fc) for fx in (0, 1) for fy in (0, 1) for fc in (0, 1)][1:]

        def peer_of(flip):
            fx, fy, fc = flip
            return (1 - x if fx else x, 1 - y if fy else y, 1 - c if fc else c)

        def copy(t, k, peer):
            return pltpu.make_async_remote_copy(
                src_ref=ins[t].at[:, _linear(peer)],
                dst_ref=outs[t].at[:, me],
                send_sem=send_sems.at[t, k],
                recv_sem=recv_sems.at[t, k],
                device_id=peer,
                device_id_type=MESH,
            )

        def arrival(t, k, peer):
            return pltpu.make_async_remote_copy(
                src_ref=ins[t].at[:, _linear(peer)],
                dst_ref=outs[t].at[:, _linear(peer)],
                send_sem=send_sems.at[t, k],
                recv_sem=recv_sems.at[t, k],
                device_id=peer,
                device_id_type=MESH,
            )

        own = []
        for t in range(n):
            cp = pltpu.make_async_copy(ins[t].at[:, me], outs[t].at[:, me], local_sems.at[t])
            cp.start()
            own.append(cp)
        sends = []
        for t in range(n):
            for k, flip in enumerate(flips):
                cp = copy(t, k, peer_of(flip))
                cp.start()
                sends.append(cp)
        for t in range(n):
            for k, flip in enumerate(flips):
                arrival(t, k, peer_of(flip)).wait_recv()
        for cp in sends:
            cp.wait_send()
        for cp in own:
            cp.wait()

    out_shape = [jax.ShapeDtypeStruct(b.shape, b.dtype) for b in blocks]
    return pl.pallas_call(
        body,
        name=name,
        in_specs=[HBM_SPEC] * n,
        out_specs=[HBM_SPEC] * n,
        out_shape=out_shape,
        scratch_shapes=[
            pltpu.SemaphoreType.DMA((n, N_DEV - 1)),
            pltpu.SemaphoreType.DMA((n, N_DEV - 1)),
            pltpu.SemaphoreType.DMA((n,)),
        ],
    )(*blocks)


def _peers():
    x, y, c = _position()
    flips = [(fx, fy, fc) for fx in (0, 1) for fy in (0, 1) for fc in (0, 1)][1:]
    return [(1 - x if fx else x, 1 - y if fy else y, 1 - c if fc else c) for fx, fy, fc in flips]


def _split_start(srcs, lands, carry, name, exchange=False):
    n = len(srcs)

    def body(*refs):
        src_refs, land_refs = refs[:n], refs[n:2 * n]
        send_sems, recv_sems = refs[2 * n + 1], refs[2 * n + 2]
        me = _linear(_position())
        for t in range(n):
            for k, peer in enumerate(_peers()):
                pltpu.make_async_remote_copy(
                    src_ref=src_refs[t].at[_linear(peer)] if exchange else src_refs[t],
                    dst_ref=land_refs[t].at[me],
                    send_sem=send_sems.at[t * (N_DEV - 1) + k],
                    recv_sem=recv_sems.at[t * (N_DEV - 1) + k],
                    device_id=peer,
                    device_id_type=MESH,
                ).start()

    operands = list(srcs) + list(lands) + [carry]
    sems = pltpu.SemaphoreType.DMA((n * (N_DEV - 1),))
    out = pl.pallas_call(
        body,
        name=name,
        in_specs=[HBM_SPEC] * len(operands),
        out_specs=[SEM_SPEC, SEM_SPEC] + [HBM_SPEC] * len(operands),
        out_shape=[sems, sems] + [pltpu.HBM(a.shape, a.dtype) for a in operands],
        input_output_aliases={i: 2 + i for i in range(len(operands))},
        compiler_params=pltpu.CompilerParams(has_side_effects=pltpu.SideEffectType.DATAFLOW_SIDE_EFFECTING),
    )(*[pltpu.with_memory_space_constraint(a, pltpu.HBM) for a in operands])
    return out[0], out[1], out[2:2 + n], out[2 + n:2 + 2 * n], out[2 + 2 * n]


def _split_wait(send_sems, recv_sems, srcs, lands, after, name, exchange=False):
    n = len(srcs)

    def body(*refs):
        src_refs, land_refs = refs[:n], refs[n:2 * n]
        send_ref, recv_ref = refs[2 * n], refs[2 * n + 1]
        for t in range(n):
            for k, peer in enumerate(_peers()):
                copy = pltpu.make_async_remote_copy(
                    src_ref=src_refs[t].at[0] if exchange else src_refs[t],
                    dst_ref=land_refs[t].at[0],
                    send_sem=send_ref.at[t * (N_DEV - 1) + k],
                    recv_sem=recv_ref.at[t * (N_DEV - 1) + k],
                    device_id=peer,
                    device_id_type=MESH,
                )
                copy.wait_send()
                copy.wait_recv()

    arrays = list(srcs) + list(lands)
    out = pl.pallas_call(
        body,
        name=name,
        in_specs=[HBM_SPEC] * len(arrays) + [SEM_SPEC, SEM_SPEC, pl.BlockSpec(memory_space=pl.ANY)],
        out_specs=[HBM_SPEC] * len(arrays),
        out_shape=[pltpu.HBM(a.shape, a.dtype) for a in arrays],
        input_output_aliases={i: i for i in range(len(arrays))},
        compiler_params=pltpu.CompilerParams(has_side_effects=pltpu.SideEffectType.DATAFLOW_SIDE_EFFECTING),
    )(*arrays, send_sems, recv_sems, after)
    return out[:n], out[n:]


def _pack(arrays, row_multiple):
    flat = jnp.concatenate([a.reshape(-1) for a in arrays])
    quantum = row_multiple * FLAT_LANES
    padded = -(-flat.shape[0] // quantum) * quantum
    return jnp.pad(flat, (0, padded - flat.shape[0])).reshape(-1, FLAT_LANES)


def _unpack(flat, like):
    flat = flat.reshape(-1)
    out, at = [], 0
    for a in like:
        size = math.prod(a.shape)
        out.append(flat[at:at + size].reshape(a.shape))
        at += size
    return out


def kernel(x, a_norm, a_w_in, a_sgu_norm, a_w_spatial, a_b_spatial, a_w_out, kv_norm, w_kv, b_norm, b_w_q, b_rel_bias, b_w_o, ffn_norm, ffn_w_gate_up, ffn_w_down, final_norm, loss_target, m_a_norm, m_a_w_in, m_a_sgu_norm, m_a_w_spatial, m_a_b_spatial, m_a_w_out, m_kv_norm, m_w_kv, m_b_norm, m_b_w_q, m_b_rel_bias, m_b_w_o, m_ffn_norm, m_ffn_w_gate_up, m_ffn_w_down, m_final_norm, v_a_norm, v_a_w_in, v_a_sgu_norm, v_a_w_spatial, v_a_b_spatial, v_a_w_out, v_kv_norm, v_w_kv, v_b_norm, v_b_w_q, v_b_rel_bias, v_b_w_o, v_ffn_norm, v_ffn_w_gate_up, v_ffn_w_down, v_final_norm):
    xs = x[0]
    target = loss_target[0]
    t, d = xs.shape
    n_a = a_w_in.shape[0]
    n_b = b_w_q.shape[0]
    depth = ffn_w_gate_up.shape[0]
    f_a = a_w_out.shape[1] * N_DEV
    gd = f_a // A_GROUPS
    nb_ffn = ffn_w_gate_up.shape[2]
    me = _linear(_position())

    small_rows = -(-(a_norm.size + a_sgu_norm.size) // (8 * 128)) * 8
    small = jnp.pad(jnp.concatenate([a_norm.reshape(-1), a_sgu_norm.reshape(-1)]),
                    (0, small_rows * 128 - a_norm.size - a_sgu_norm.size)).reshape(1, small_rows, 128)

    def shard(w, layer=None):
        return (w if layer is None else w[layer]).astype(BF16)

    stages = []
    for layer in range(depth):
        if layer == 0:
            stages += [("a0", [shard(a_w_in, 0)]), ("a0_out", [shard(a_w_out, 0)])]
        elif layer < n_a:
            stages.append((f"a{layer}", [shard(a_w_in, layer), shard(a_w_out, layer)]))
        else:
            i = layer - n_a
            shared = [shard(w_kv)] if i == 0 else []
            stages.append((f"b{i}", shared + [shard(b_w_q, i), shard(b_w_o, i)]))
        stages.append((f"f{layer}", [shard(ffn_w_gate_up, layer), shard(ffn_w_down, layer)]))
    first = _all_gather([s[None] for s in stages[0][1]] + [small], "gather_first")
    gathered = {stages[0][0]: [g[0] for g in first[:-1]]}
    small_g = first[-1].reshape(N_DEV, -1)
    a_norm_full = small_g[:, :a_norm.size].reshape(N_DEV, n_a, -1).transpose(1, 0, 2).reshape(n_a, d)
    a_sgu_full = small_g[:, a_norm.size:a_norm.size + a_sgu_norm.size].reshape(
        N_DEV, n_a, -1).transpose(1, 0, 2).reshape(n_a, f_a)
    in_flight = {}
    for key, shards in stages[1:]:
        lands = [lax.dynamic_update_slice(lax.empty((N_DEV,) + s.shape, BF16), s[None], (me, 0, 0)) for s in shards]
        send, recv, srcs, lands, a_norm_full = _split_start(shards, lands, a_norm_full, f"gather_start_{key}")
        in_flight[key] = (send, recv, srcs, lands)

    def weights(key, after):
        if key not in gathered:
            _, gathered[key] = _split_wait(*in_flight.pop(key), after, f"gather_wait_{key}")
        return gathered[key]

    rows_down = ffn_w_down.shape[1]

    def mixer_a_weights(i, after):
        if i == 0:
            (w_in,), (w_out,) = weights("a0", after[0]), weights("a0_out", after[1])
        else:
            w_in, w_out = weights(f"a{i}", after[0])
        return w_in[None], w_out.reshape(1, f_a, d)

    def mixer_b_weights(i, after):
        ws = weights(f"b{i}", after)
        return ws[-2].reshape(1, d, d), ws[-1].reshape(1, d, d)

    def ffn_weights(layer, after):
        w_gu, w_dn = weights(f"f{layer}", after)
        return w_gu[None], w_dn.reshape(1, N_DEV // 2, 2 * rows_down, d)

    w_sp_t = jnp.swapaxes(a_w_spatial, -1, -2)
    b_full = jnp.repeat(jnp.swapaxes(a_b_spatial, -1, -2), gd, axis=-1)

    saved = []

    def ffn_fwd(xin, layer):
        hf = _rms_fwd(xin, ffn_norm[layer], f"ffn_norm_fwd_{layer}")
        w_gu, w_dn = ffn_weights(layer, xin)
        gu, act = _ffn_gate_up(f"ffn_gate_up_{layer}", hf, w_gu, 0)
        xout = _mm_down(f"ffn_down_{layer}", act, w_dn, 0, xin)
        return xout, (xin, hf, gu, act)

    for i in range(n_a):
        h = _rms_fwd(xs, a_norm_full[i], f"a_norm_fwd_{i}")
        zpre = _mm_colblock(f"a_in_{i}", h, weights(f"a{i}", xs)[0][None], 0)
        p, zs, dgs = _sgu_fwd(zpre, a_sgu_full[i], a_w_spatial[i], b_full[i], f"a_sgu_fwd_{i}")
        w_in, w_out = mixer_a_weights(i, (xs, p))
        x_mid = _mm_natural(f"a_out_{i}", p, w_out, 0, res=xs)
        x_out, ffn_saved = ffn_fwd(x_mid, i)
        saved.append((xs, h, zs, dgs, p, ffn_saved))
        xs = x_out

    x_kv = xs
    w_kv_g = weights("b0", x_kv)[0][None]
    h_kv = _rms_fwd(x_kv, kv_norm, "kv_norm_fwd")
    kv = _mm_colblock("kv_proj", h_kv, w_kv_g, 0)
    kvpad = jnp.pad(kv, ((LEFT, 0), (0, 0)))

    biases = [_bias_block(_bias_build(b_rel_bias[i], f"rel_bias_{i}")) for i in range(n_b)]
    for i in range(n_b):
        layer = n_a + i
        w_q, w_o = mixer_b_weights(i, xs)
        hb = _rms_fwd(xs, b_norm[i], f"b_norm_fwd_{i}")
        q = _mm_natural(f"b_q_{i}", hb, w_q, 0, out_dtype=BF16, scale=ATTN_SCALE)
        o = _attn_fwd(q, kvpad, biases[i], f"b_attn_fwd_{i}")
        x_mid = _mm_natural(f"b_o_{i}", o, w_o, 0, res=xs)
        x_out, ffn_saved = ffn_fwd(x_mid, layer)
        saved.append((xs, hb, q, o, ffn_saved))
        xs = x_out

    dx, loss_local, g_final = _loss_head(xs, final_norm, target, "loss_head")
    loss = lax.psum(loss_local, ("x", "y", "c"))

    big_grads = {}
    pending = []
    in_flight_grads = []

    def start_exchange(dx, tag):
        srcs = [big_grads[key] for key in pending]
        lands = [lax.empty(s.shape, BF16) for s in srcs]
        send, recv, srcs, lands, dx = _split_start(srcs, lands, dx, f"exchange_start_{tag}", exchange=True)
        in_flight_grads.append((list(pending), send, recv, srcs, lands, tag))
        pending.clear()
        return dx

    g_ffn_norm = [None] * depth
    g_a_norm = [None] * n_a
    g_a_sgu = [None] * n_a
    g_w_sp = [None] * n_a
    g_b_sp = [None] * n_a
    g_b_norm = [None] * n_b
    g_rel = [None] * n_b

    def ffn_bwd(dx, layer, ffn_saved):
        xin, hf, gu, act = ffn_saved
        big_grads["ffn_w_down", layer] = _mm_dw_down(f"ffn_down_dw_{layer}", act, dx)
        w_gu, w_dn = ffn_weights(layer, xin)
        dgu = _ffn_down_dx(f"ffn_down_dx_{layer}", dx, w_dn, 0, gu).reshape(N_DEV, t, nb_ffn)
        big_grads["ffn_w_gate_up", layer] = _mm_dw_colblock(
            f"ffn_gate_up_dw_{layer}", hf, dgu, blocked_in=True, transposed=True)
        pending.extend([("ffn_w_gate_up", layer), ("ffn_w_down", layer)])
        dx, g_ffn_norm[layer] = _mm_t_colblock_norm_bwd(
            f"ffn_gate_up_dx_{layer}", dgu, w_gu, 0, xin, ffn_norm[layer], dx, blocked_in=True)
        return dx

    dk = dv = None
    for i in reversed(range(n_b)):
        layer = n_a + i
        x_in, hb, q, o, ffn_saved = saved[layer]
        dx = ffn_bwd(dx, layer, ffn_saved)
        big_grads["b_w_o", i] = _mm_dw_natural(f"b_o_dw_{i}", o, dx)
        w_q, w_o = mixer_b_weights(i, x_in)
        do = _mm_t_natural(f"b_o_dx_{i}", dx, w_o, 0)
        dq, dk, dv, dbias = _attn_bwd(q, kvpad, biases[i], do, dk, dv, f"b_attn_bwd_{i}")
        g_rel[i] = _bias_grad(dbias, f"rel_bias_grad_{i}")
        big_grads["b_w_q", i] = _mm_dw_natural(f"b_q_dw_{i}", hb, dq)
        pending.extend([("b_w_o", i), ("b_w_q", i)])
        dh = _mm_t_natural(f"b_q_dx_{i}", dq, w_q, 0)
        dx, g_b_norm[i] = _rms_bwd(x_in, b_norm[i], dh, dx, f"b_norm_bwd_{i}")
        if i > 0:
            dx = start_exchange(dx, f"b{i}")

    dkv = jnp.concatenate([dk[LEFT:], dv[LEFT:]], axis=1).astype(BF16)
    big_grads["w_kv", 0] = _mm_dw_colblock("kv_proj_dw", h_kv, dkv)
    pending.append(("w_kv", 0))
    dx, g_kv_norm = _mm_t_colblock_norm_bwd("kv_proj_dx", dkv, w_kv_g, 0, x_kv, kv_norm, dx)
    dx = start_exchange(dx, "kv")

    for i in reversed(range(n_a)):
        x_in, h, zs, dgs, p, ffn_saved = saved[i]
        dx = ffn_bwd(dx, i, ffn_saved)
        if i == 0:
            dx = start_exchange(dx, "f0")
        big_grads["a_w_out", i] = _mm_dw_natural(f"a_out_dw_{i}", p, dx)
        pending.append(("a_w_out", i))
        if i == 0:
            dx = start_exchange(dx, "a0_out")
        w_in, w_out = mixer_a_weights(i, (x_in, p))
        dp = _mm_t_natural(f"a_out_dx_{i}", dx, w_out, 0)
        dz, g_w_sp[i], g_b_sp[i], g_a_sgu[i] = _sgu_bwd(
            zs, dgs, dp, a_sgu_full[i], a_w_spatial[i], w_sp_t[i], b_full[i], f"a_sgu_bwd_{i}")
        big_grads["a_w_in", i] = _mm_dw_colblock(f"a_in_dw_{i}", h, dz)
        pending.append(("a_w_in", i))
        if i == 0:
            dx = start_exchange(dx, "a0_in")
        dx, g_a_norm[i] = _mm_t_colblock_norm_bwd(f"a_in_dx_{i}", dz, w_in, 0, x_in, a_norm_full[i], dx)
        if i > 0:
            dx = start_exchange(dx, f"a{i}")
    grad_x = dx[None]

    small_like = [jax.ShapeDtypeStruct((n_a, d), F32), jax.ShapeDtypeStruct((n_a, f_a), F32),
                  a_w_spatial, a_b_spatial, kv_norm, b_norm, b_rel_bias, ffn_norm, final_norm]
    small_partial = _pack(
        [jnp.stack(g_a_norm), jnp.stack(g_a_sgu), jnp.stack(g_w_sp), jnp.stack(g_b_sp), g_kv_norm,
         jnp.stack(g_b_norm), jnp.stack(g_rel), jnp.stack(g_ffn_norm), g_final], N_DEV * 8)
    chunk_rows = small_partial.shape[0] // N_DEV
    arrived = {}
    for keys, send, recv, srcs, lands, tag in in_flight_grads:
        srcs, lands = _split_wait(send, recv, srcs, lands, dx, f"exchange_wait_{tag}", exchange=True)
        for key, src, land in zip(keys, srcs, lands):
            arrived[key] = (land, src)
    small_got = _exchange([small_partial.reshape(1, N_DEV, chunk_rows, FLAT_LANES)], "exchange_small")[0]
    small_sum = _ordered_sum(small_got[0], "small_grad_sum")
    small_all = _all_gather([small_sum[None]], "gather_small_grads")[0]
    (ga_norm, ga_sgu, gw_sp, gb_sp, gkv_norm, gb_norm, g_relb, gffn_norm, gfinal) = _unpack(small_all, small_like)

    results = {}
    big_names = ["a_w_in", "a_w_out", "w_kv", "b_w_q", "b_w_o", "ffn_w_gate_up", "ffn_w_down"]
    big_wmv = [(a_w_in, m_a_w_in, v_a_w_in), (a_w_out, m_a_w_out, v_a_w_out),
               (w_kv[None], m_w_kv[None], v_w_kv[None]), (b_w_q, m_b_w_q, v_b_w_q), (b_w_o, m_b_w_o, v_b_w_o),
               tuple(jnp.swapaxes(a, 1, 2) for a in (ffn_w_gate_up, m_ffn_w_gate_up, v_ffn_w_gate_up)),
               (ffn_w_down, m_ffn_w_down, v_ffn_w_down)]
    me_arr = jnp.reshape(me, (1,)).astype(jnp.int32)
    for name, (w, m, v) in zip(big_names, big_wmv):
        outs = None
        for layer in range(w.shape[0]):
            got, own = arrived[name, layer]
            outs = _adamw_layer(got, own, w, m, v, layer, outs, me_arr, f"adamw_{name}_{layer}")
        if name == "w_kv":
            outs = [o[0] for o in outs]
        if name == "ffn_w_gate_up":
            outs = [jnp.swapaxes(o, 1, 2) for o in outs]
        results[name] = outs

    n_cols = a_norm.shape[1]
    s_cols = a_sgu_norm.shape[1]
    small_g_list = [lax.dynamic_slice(ga_norm, (0, me * n_cols), (n_a, n_cols)),
                    lax.dynamic_slice(ga_sgu, (0, me * s_cols), (n_a, s_cols)),
                    gw_sp, gb_sp, gkv_norm, gb_norm, g_relb, gffn_norm, gfinal]
    small_names = ["a_norm", "a_sgu_norm", "a_w_spatial", "a_b_spatial", "kv_norm", "b_norm", "b_rel_bias",
                   "ffn_norm", "final_norm"]
    small_w = [a_norm, a_sgu_norm, a_w_spatial, a_b_spatial, kv_norm, b_norm, b_rel_bias, ffn_norm, final_norm]
    small_m = [m_a_norm, m_a_sgu_norm, m_a_w_spatial, m_a_b_spatial, m_kv_norm, m_b_norm, m_b_rel_bias,
               m_ffn_norm, m_final_norm]
    small_v = [v_a_norm, v_a_sgu_norm, v_a_w_spatial, v_a_b_spatial, v_kv_norm, v_b_norm, v_b_rel_bias,
               v_ffn_norm, v_final_norm]
    flat_g = _pack(small_g_list, 8)
    flat_out = _adamw(flat_g[None, None], _pack(small_w, 8)[None], _pack(small_m, 8)[None],
                      _pack(small_v, 8)[None], "adamw_small")
    unpacked = [_unpack(o[0], small_w) for o in flat_out]
    for idx, name in enumerate(small_names):
        results[name] = [unpacked[kind][idx] for kind in range(4)]

    order = ["a_norm", "a_w_in", "a_sgu_norm", "a_w_spatial", "a_b_spatial", "a_w_out", "kv_norm", "w_kv",
             "b_norm", "b_w_q", "b_rel_bias", "b_w_o", "ffn_norm", "ffn_w_gate_up", "ffn_w_down", "final_norm"]
    outputs = [loss, grad_x]
    for kind in range(4):
        outputs += [results[name][kind] for name in order]
    return tuple(outputs)
```

```python
import math

import jax
import jax.numpy as jnp
from jax import lax
from jax.experimental import pallas as pl
from jax.experimental.pallas import tpu as pltpu

F32 = jnp.float32
BF16 = jnp.bfloat16
MESH = pl.DeviceIdType.MESH
HBM_SPEC = pl.BlockSpec(memory_space=pltpu.HBM)
SEM_SPEC = pl.BlockSpec(memory_space=pltpu.SEMAPHORE)

N_DEV = 8
CHUNK = 64
A_CHUNK = 128
A_GROUPS = 8
N_LEFT_CHUNKS = 8
LEFT = N_LEFT_CHUNKS * CHUNK
PAIR_ROWS = 2 * CHUNK
PAIR_BAND = PAIR_ROWS + LEFT
DIAGONALS = PAIR_BAND + PAIR_ROWS
PAIRS_PER_BLOCK = 2
Q_BLOCK = PAIRS_PER_BLOCK * PAIR_ROWS
K_BLOCK = Q_BLOCK + LEFT
ATTN_UNROLL = 2
MAX_REL = 256
N_REL = 2 * MAX_REL + 1
REL_PAD = 640
HEAD_DIM = 64
HEAD_PAIR = 2 * HEAD_DIM
ATTN_SCALE = HEAD_DIM ** -0.5
EPS = 1e-6
NEG_INF = -1e30
ADAM_LR = 0.001
ADAM_B1 = 0.9
ADAM_B2 = 0.999
ADAM_EPS = 1e-08
ADAM_WD = 0.01
ADAM_STEP = 10
FLAT_LANES = 1024
V7X_VMEM_BYTES = 64 * 1024 * 1024
VMEM_FLOOR_BYTES = 32 * 1024 * 1024
VMEM_CEIL_BYTES = V7X_VMEM_BYTES - 8 * 1024 * 1024

NN = (((1,), (0,)), ((), ()))
NT = (((1,), (1,)), ((), ()))
TN = (((0,), (0,)), ((), ()))


def _tile(n, pref):
    return pref if n % pref == 0 else n


def _row_tile(n, pref, mult):
    best = None
    for t in range(mult, min(n, pref) + 1, mult):
        if n % t == 0:
            best = t
    return best if best is not None else n


def _nbytes(shape, dtype):
    n = 1
    for s in shape:
        if s is not None:
            n *= s
    return n * jnp.dtype(dtype).itemsize


def _call(body, name, grid, in_specs, out_specs, out_shape, scratch=(), vmem_bytes=0, aliases=None):
    limit = int(min(max(VMEM_FLOOR_BYTES, vmem_bytes * 5 // 4), VMEM_CEIL_BYTES))
    return pl.pallas_call(
        body,
        name=name,
        grid=grid,
        in_specs=in_specs,
        out_specs=out_specs,
        out_shape=out_shape,
        scratch_shapes=list(scratch),
        input_output_aliases=aliases or {},
        compiler_params=pltpu.CompilerParams(
            dimension_semantics=("arbitrary",) * len(grid), vmem_limit_bytes=limit),
    )


ERFC_P = 0.3275911 / math.sqrt(2.0)
ERFC_HALF_COEFFS = tuple(0.5 * a for a in (1.061405429, -1.453152027, 1.421413741, -0.284496736, 0.254829592))


def _gelu_and_grad(x):
    d = 1.0 + ERFC_P * jnp.abs(x)
    r = pl.reciprocal(d, approx=True)
    t = r * (2.0 - d * r)
    a5, a4, a3, a2, a1 = ERFC_HALF_COEFFS
    ex = jnp.exp(-0.5 * (x * x))
    tail = ((((a5 * t + a4) * t + a3) * t + a2) * t + a1) * t * ex
    cdf = jnp.where(x < 0, tail, 1.0 - tail)
    return x * cdf, cdf + x * ex * (1.0 / math.sqrt(2.0 * math.pi))


def _sigmoid(x):
    return 1.0 / (1.0 + jnp.exp(-x))


def _split3(x):
    hi = x.astype(BF16)
    r1 = x - hi.astype(F32)
    mid = r1.astype(BF16)
    lo = (r1 - mid.astype(F32)).astype(BF16)
    return hi, mid, lo


def _rms_fwd(x, g, name):
    t, d = x.shape
    tm = _tile(t, 512)

    def body(x_ref, g_ref, o_ref):
        xf = x_ref[...]
        r = lax.rsqrt(jnp.mean(xf * xf, axis=-1, keepdims=True) + EPS)
        o_ref[...] = (xf * r * g_ref[...]).astype(o_ref.dtype)

    return _call(
        body, name, (t // tm,),
        [pl.BlockSpec((tm, d), lambda i: (i, 0)), pl.BlockSpec((1, d), lambda i: (0, 0))],
        pl.BlockSpec((tm, d), lambda i: (i, 0)),
        jax.ShapeDtypeStruct((t, d), BF16),
        vmem_bytes=2 * (_nbytes((tm, d), F32) + _nbytes((tm, d), BF16)) + 4 * _nbytes((tm, d), F32),
    )(x, g.reshape(1, d))


def _rms_bwd(x, g, dh, dx_up, name):
    t, d = x.shape
    tm = _tile(t, 512)

    def body(x_ref, g_ref, dh_ref, up_ref, dx_ref, dg_ref):
        @pl.when(pl.program_id(0) == 0)
        def _():
            dg_ref[...] = jnp.zeros_like(dg_ref)

        xf = x_ref[...]
        r = lax.rsqrt(jnp.mean(xf * xf, axis=-1, keepdims=True) + EPS)
        xhat = xf * r
        dy = dh_ref[...].astype(F32)
        dxhat = dy * g_ref[...]
        dg_ref[...] += jnp.sum(dy * xhat, axis=0, keepdims=True)
        dx = r * (dxhat - xhat * jnp.mean(dxhat * xhat, axis=-1, keepdims=True))
        dx_ref[...] = up_ref[...] + dx

    row = pl.BlockSpec((tm, d), lambda i: (i, 0))
    vec = pl.BlockSpec((1, d), lambda i: (0, 0))
    dx, dg = _call(
        body, name, (t // tm,),
        [row, vec, row, row],
        [row, vec],
        [jax.ShapeDtypeStruct((t, d), F32), jax.ShapeDtypeStruct((1, d), F32)],
        vmem_bytes=10 * _nbytes((tm, d), F32),
    )(x, g.reshape(1, d), dh, dx_up)
    return dx, dg.reshape(d)


def _mm(name, dims, a, b, *, grid, a_spec, b_spec, out_shape, out_spec, acc_shape,
        res=None, res_spec=None, scale=None):
    nk = grid[2]
    has_res = res is not None

    def body(*refs):
        refs = list(refs)
        a_ref = refs.pop(0)
        b_ref = refs.pop(0)
        r_ref = refs.pop(0) if has_res else None
        o_ref = refs.pop(0)
        part = lax.dot_general(a_ref[...].astype(BF16), b_ref[...].astype(BF16), dims,
                               preferred_element_type=F32)

        def finish(acc):
            if scale is not None:
                acc = acc * scale
            if has_res:
                acc = acc + r_ref[...]
            o_ref[...] = acc.astype(o_ref.dtype)

        if nk == 1:
            finish(part)
        else:
            acc_ref = refs.pop(0)
            k = pl.program_id(2)

            @pl.when(k == 0)
            def _():
                acc_ref[...] = part

            @pl.when(k > 0)
            def _():
                acc_ref[...] += part

            @pl.when(k == nk - 1)
            def _():
                finish(acc_ref[...])

    operands = [a, b]
    in_specs = [a_spec, b_spec]
    vmem = 2 * (_nbytes(a_spec.block_shape, a.dtype) + _nbytes(b_spec.block_shape, b.dtype)
                + _nbytes(out_spec.block_shape, out_shape.dtype))
    vmem += 3 * _nbytes(acc_shape, F32)
    if has_res:
        operands.append(res)
        in_specs.append(res_spec)
        vmem += 2 * _nbytes(res_spec.block_shape, res.dtype)
    scratch = [pltpu.VMEM(acc_shape, F32)] if nk > 1 else []
    return _call(body, name, grid, in_specs, out_spec, out_shape, scratch=scratch, vmem_bytes=vmem)(*operands)


def _mm_colblock(name, h, w_g, layer):
    t, k = h.shape
    nb = w_g.shape[3]
    tm = _tile(t, 2048)
    return _mm(
        name, NN, h, w_g, grid=(t // tm, N_DEV, 1),
        a_spec=pl.BlockSpec((tm, k), lambda i, j, kk: (i, 0)),
        b_spec=pl.BlockSpec((None, None, k, nb), lambda i, j, kk: (layer, j, 0, 0)),
        out_shape=jax.ShapeDtypeStruct((t, N_DEV * nb), BF16),
        out_spec=pl.BlockSpec((tm, nb), lambda i, j, kk: (i, j)), acc_shape=(tm, nb))


def _mm_natural(name, a, w, layer, *, res=None, out_dtype=F32, scale=None):
    t, k = a.shape
    n = w.shape[2]
    tm = _tile(t, 1024)
    tn = _tile(n, 512)
    res_spec = None if res is None else pl.BlockSpec((tm, tn), lambda i, j, kk: (i, j))
    return _mm(
        name, NN, a, w, grid=(t // tm, n // tn, 1),
        a_spec=pl.BlockSpec((tm, k), lambda i, j, kk: (i, 0)),
        b_spec=pl.BlockSpec((None, k, tn), lambda i, j, kk: (layer, 0, j)),
        out_shape=jax.ShapeDtypeStruct((t, n), out_dtype),
        out_spec=pl.BlockSpec((tm, tn), lambda i, j, kk: (i, j)),
        acc_shape=(tm, tn), res=res, res_spec=res_spec, scale=scale)


def _mm_down(name, act, w4, layer, res):
    nblk, t, kb = act.shape
    n = w4.shape[3]
    tm = _tile(t, 1024)

    def body(a_ref, b_ref, r_ref, o_ref):
        acc = r_ref[...]
        for u in range(nblk):
            acc = acc + jnp.dot(a_ref[u], b_ref[u], preferred_element_type=F32)
        o_ref[...] = acc

    row = pl.BlockSpec((tm, n), lambda i: (i, 0))
    return _call(
        body, name, (t // tm,),
        [pl.BlockSpec((nblk, tm, kb), lambda i: (0, i, 0)),
         pl.BlockSpec((None, nblk, kb, n), lambda i: (layer, 0, 0, 0)),
         row],
        row,
        jax.ShapeDtypeStruct((t, n), F32),
        vmem_bytes=2 * (_nbytes((nblk, tm, kb), BF16) + _nbytes((nblk, kb, n), BF16)) + 6 * _nbytes((tm, n), F32),
    )(act, w4, res)


def _mm_t_colblock_norm_bwd(name, dz, w_g, layer, x, g, dx_up, blocked_in=False):
    k = w_g.shape[2]
    nb = w_g.shape[3]
    t = x.shape[0]
    tm = _tile(t, 1024)
    per_step = 2
    n_steps = N_DEV // per_step
    if blocked_in:
        a_spec = pl.BlockSpec((per_step, tm, nb), lambda i, kk: (kk, i, 0))
    else:
        a_spec = pl.BlockSpec((tm, per_step * nb), lambda i, kk: (i, kk))

    def body(a_ref, b_ref, x_ref, g_ref, up_ref, dx_ref, dg_ref, acc_ref):
        i = pl.program_id(0)
        kk = pl.program_id(1)
        part = None
        for u in range(per_step):
            a = a_ref[u] if blocked_in else a_ref[:, u * nb:(u + 1) * nb]
            term = lax.dot_general(a.astype(BF16), b_ref[u].astype(BF16), NT, preferred_element_type=F32)
            part = term if part is None else part + term

        @pl.when(kk == 0)
        def _():
            acc_ref[...] = part

        @pl.when(kk > 0)
        def _():
            acc_ref[...] += part

        @pl.when((i == 0) & (kk == 0))
        def _():
            dg_ref[...] = jnp.zeros_like(dg_ref)

        @pl.when(kk == n_steps - 1)
        def _():
            dy = acc_ref[...]
            xf = x_ref[...]
            r = lax.rsqrt(jnp.mean(xf * xf, axis=-1, keepdims=True) + EPS)
            xhat = xf * r
            dxhat = dy * g_ref[...]
            dg_ref[...] += jnp.sum(dy * xhat, axis=0, keepdims=True)
            dx_ref[...] = up_ref[...] + r * (dxhat - xhat * jnp.mean(dxhat * xhat, axis=-1, keepdims=True))

    row = pl.BlockSpec((tm, k), lambda i, kk: (i, 0))
    vec = pl.BlockSpec((1, k), lambda i, kk: (0, 0))
    dx, dg = _call(
        body, name, (t // tm, n_steps),
        [a_spec, pl.BlockSpec((None, per_step, k, nb), lambda i, kk: (layer, kk, 0, 0)), row, vec, row],
        [row, vec],
        [jax.ShapeDtypeStruct((t, k), F32), jax.ShapeDtypeStruct((1, k), F32)],
        scratch=[pltpu.VMEM((tm, k), F32)],
        vmem_bytes=2 * per_step * (_nbytes((tm, nb), BF16) + _nbytes((k, nb), BF16)) + 10 * _nbytes((tm, k), F32),
    )(dz, w_g, x, g.reshape(1, k), dx_up)
    return dx, dg.reshape(k)


def _ffn_gate_up(name, h, w_g, layer):
    t, k = h.shape
    nb = w_g.shape[3]
    half = N_DEV // 2
    tm = _tile(t, 1024)

    def body(h_ref, wg_ref, wu_ref, gu_ref, act_ref):
        hb = h_ref[...]
        gate = jnp.dot(hb, wg_ref[...], preferred_element_type=F32)
        up = jnp.dot(hb, wu_ref[...], preferred_element_type=F32)
        gu_ref[0] = gate.astype(BF16)
        gu_ref[1] = up.astype(BF16)
        act_ref[...] = (gate * _sigmoid(gate) * up).astype(BF16)

    return _call(
        body, name, (t // tm, half),
        [pl.BlockSpec((tm, k), lambda i, j: (i, 0)),
         pl.BlockSpec((None, None, k, nb), lambda i, j: (layer, j, 0, 0)),
         pl.BlockSpec((None, None, k, nb), lambda i, j: (layer, half + j, 0, 0))],
        [pl.BlockSpec((2, None, tm, nb), lambda i, j: (0, j, i, 0)),
         pl.BlockSpec((None, tm, nb), lambda i, j: (j, i, 0))],
        [jax.ShapeDtypeStruct((2, half, t, nb), BF16), jax.ShapeDtypeStruct((half, t, nb), BF16)],
        vmem_bytes=2 * (_nbytes((tm, k), BF16) + 2 * _nbytes((k, nb), BF16) + 3 * _nbytes((tm, nb), BF16))
        + 6 * _nbytes((tm, nb), F32),
    )(h, w_g, w_g)


def _ffn_down_dx(name, dy, w4, layer, gu4):
    t, n = dy.shape
    nblk, kb = w4.shape[1], w4.shape[2]
    tm = _tile(t, 1024)

    def body(dy_ref, w_ref, gu_ref, dgu_ref):
        da = lax.dot_general(dy_ref[...].astype(BF16), w_ref[...], NT, preferred_element_type=F32)
        gate = gu_ref[0].astype(F32)
        up = gu_ref[1].astype(F32)
        sig = _sigmoid(gate)
        dgu_ref[0] = (da * up * (sig * (1.0 + gate * (1.0 - sig)))).astype(BF16)
        dgu_ref[1] = (da * (gate * sig)).astype(BF16)

    blk = pl.BlockSpec((2, None, tm, kb), lambda i, j: (0, j, i, 0))
    return _call(
        body, name, (t // tm, nblk),
        [pl.BlockSpec((tm, n), lambda i, j: (i, 0)),
         pl.BlockSpec((None, None, kb, n), lambda i, j: (layer, j, 0, 0)),
         blk],
        blk,
        jax.ShapeDtypeStruct((2, nblk, t, kb), BF16),
        vmem_bytes=2 * (_nbytes((tm, n), F32) + _nbytes((kb, n), BF16) + 4 * _nbytes((tm, kb), BF16))
        + 8 * _nbytes((tm, kb), F32),
    )(dy, w4, gu4)


def _mm_t_natural(name, dy, w, layer):
    t, n = dy.shape
    k = w.shape[1]
    tm = _tile(t, 1024)
    tk = _tile(k, 512)
    return _mm(
        name, NT, dy, w, grid=(t // tm, k // tk, 1),
        a_spec=pl.BlockSpec((tm, n), lambda i, j, kk: (i, 0)),
        b_spec=pl.BlockSpec((None, tk, n), lambda i, j, kk: (layer, j, 0)),
        out_shape=jax.ShapeDtypeStruct((t, k), BF16),
        out_spec=pl.BlockSpec((tm, tk), lambda i, j, kk: (i, j)),
        acc_shape=(tm, tk))


def _mm_dw_colblock(name, h, dz, blocked_in=False, transposed=False):
    t, k = h.shape
    nb = dz.shape[2] if blocked_in else dz.shape[1] // N_DEV
    tk = _tile(t, 2048)
    h_spec = pl.BlockSpec((tk, k), lambda i, j, kk: (kk, 0))
    if blocked_in:
        dz_spec = pl.BlockSpec((None, tk, nb), lambda i, j, kk: (j, kk, 0))
    else:
        dz_spec = pl.BlockSpec((tk, nb), lambda i, j, kk: (kk, j))
    rows, cols = (nb, k) if transposed else (k, nb)
    return _mm(
        name, TN, *((dz, h) if transposed else (h, dz)), grid=(1, N_DEV, t // tk),
        a_spec=dz_spec if transposed else h_spec,
        b_spec=h_spec if transposed else dz_spec,
        out_shape=jax.ShapeDtypeStruct((N_DEV, rows, cols), BF16),
        out_spec=pl.BlockSpec((None, rows, cols), lambda i, j, kk: (j, 0, 0)),
        acc_shape=(rows, cols))


def _mm_dw_natural(name, a, dy):
    t, k = a.shape
    n = dy.shape[1]
    tko = _tile(k, 1024)
    tt = _tile(t, 2048)
    out = _mm(
        name, TN, a, dy, grid=(k // tko, 1, t // tt),
        a_spec=pl.BlockSpec((tt, tko), lambda i, j, kk: (kk, i)),
        b_spec=pl.BlockSpec((tt, n), lambda i, j, kk: (kk, 0)),
        out_shape=jax.ShapeDtypeStruct((k, n), BF16),
        out_spec=pl.BlockSpec((tko, n), lambda i, j, kk: (i, 0)),
        acc_shape=(tko, n))
    return out.reshape(N_DEV, k // N_DEV, n)


def _mm_dw_down(name, act, dy):
    nblk, t, kb = act.shape
    n = dy.shape[1]
    tt = _tile(t, 2048)
    out = _mm(
        name, TN, act, dy, grid=(nblk, 1, t // tt),
        a_spec=pl.BlockSpec((None, tt, kb), lambda i, j, kk: (i, kk, 0)),
        b_spec=pl.BlockSpec((tt, n), lambda i, j, kk: (kk, 0)),
        out_shape=jax.ShapeDtypeStruct((nblk, kb, n), BF16),
        out_spec=pl.BlockSpec((None, kb, n), lambda i, j, kk: (i, 0, 0)),
        acc_shape=(kb, n))
    return out.reshape(N_DEV, (nblk * kb) // N_DEV, n)


def _spatial_mask(transposed=False):
    r = lax.broadcasted_iota(jnp.int32, (A_CHUNK, A_CHUNK), 0) // CHUNK
    c = lax.broadcasted_iota(jnp.int32, (A_CHUNK, A_CHUNK), 1) // CHUNK
    return c >= r if transposed else r >= c


def _sgu_tile(t):
    return _tile(t, 2 * A_CHUNK)


def _sgu_fwd(zpre, g_sgu, w_sp, b_full, name):
    t, f2 = zpre.shape
    f = f2 // 2
    gd = f // A_GROUPS
    tm = _sgu_tile(t)

    def body(z_ref, g_ref, w_ref, b_ref, p_ref, zs_ref, dg_ref):
        mask = _spatial_mask()
        wm = [jnp.where(mask, w_ref[g], 0.0).astype(BF16) for g in range(A_GROUPS)]
        for c in range(tm // A_CHUNK):
            rows = pl.ds(c * A_CHUNK, A_CHUNK)
            z, dgelu = _gelu_and_grad(z_ref[rows, :].astype(F32))
            zs_ref[rows, :] = z.astype(BF16)
            dg_ref[rows, :] = dgelu.astype(BF16)
            u = z[:, :f]
            v0 = z[:, f:]
            r = lax.rsqrt(jnp.mean(v0 * v0, axis=-1, keepdims=True) + EPS)
            v1 = (v0 * r * g_ref[...]).astype(BF16)
            for g in range(A_GROUPS):
                cols = slice(g * gd, (g + 1) * gd)
                v2 = jnp.dot(wm[g], v1[:, cols], preferred_element_type=F32) + b_ref[:, cols]
                p_ref[rows, cols] = (u[:, cols] * v2).astype(BF16)

    return _call(
        body, name, (t // tm,),
        [pl.BlockSpec((tm, f2), lambda i: (i, 0)),
         pl.BlockSpec((1, f), lambda i: (0, 0)),
         pl.BlockSpec((A_GROUPS, A_CHUNK, A_CHUNK), lambda i: (0, 0, 0)),
         pl.BlockSpec((A_CHUNK, f), lambda i: (0, 0))],
        [pl.BlockSpec((tm, f), lambda i: (i, 0)), pl.BlockSpec((tm, f2), lambda i: (i, 0)),
         pl.BlockSpec((tm, f2), lambda i: (i, 0))],
        [jax.ShapeDtypeStruct((t, f), BF16), jax.ShapeDtypeStruct((t, f2), BF16), jax.ShapeDtypeStruct((t, f2), BF16)],
        vmem_bytes=6 * _nbytes((tm, f2), BF16) + 2 * _nbytes((tm, f), BF16) + 8 * _nbytes((A_CHUNK, f2), F32),
    )(zpre, g_sgu.reshape(1, f), w_sp, b_full)


def _sgu_bwd(zs, dgs, dp, g_sgu, w_sp, w_sp_t, b_full, name):
    t, f2 = zs.shape
    f = f2 // 2
    gd = f // A_GROUPS
    tm = _sgu_tile(t)
    n_steps = t // tm

    def body(z_ref, dgelu_ref, dp_ref, g_ref, w_ref, wt_ref, b_ref, dz_ref, dw_ref, db_ref, dg_ref, dv1_ref, dbf_ref):
        step = pl.program_id(0)

        @pl.when(step == 0)
        def _():
            dw_ref[...] = jnp.zeros_like(dw_ref)
            dg_ref[...] = jnp.zeros_like(dg_ref)
            dbf_ref[...] = jnp.zeros_like(dbf_ref)

        mask = _spatial_mask()
        mask_t = _spatial_mask(transposed=True)
        wm = [jnp.where(mask, w_ref[g], 0.0).astype(BF16) for g in range(A_GROUPS)]
        wmt = [jnp.where(mask_t, wt_ref[g], 0.0).astype(BF16) for g in range(A_GROUPS)]
        gain = g_ref[...]
        for c in range(tm // A_CHUNK):
            rows = pl.ds(c * A_CHUNK, A_CHUNK)
            z = z_ref[rows, :].astype(F32)
            dgelu = dgelu_ref[rows, :].astype(F32)
            u = z[:, :f]
            v0 = z[:, f:]
            r = lax.rsqrt(jnp.mean(v0 * v0, axis=-1, keepdims=True) + EPS)
            xhat = v0 * r
            v1 = (xhat * gain).astype(BF16)
            dpf = dp_ref[rows, :].astype(F32)
            for g in range(A_GROUPS):
                cols = slice(g * gd, (g + 1) * gd)
                v1g = v1[:, cols]
                v2 = jnp.dot(wm[g], v1g, preferred_element_type=F32) + b_ref[:, cols]
                dpg = dpf[:, cols]
                dz_ref[rows, cols] = (dpg * v2 * dgelu[:, cols]).astype(BF16)
                dv2 = dpg * u[:, cols]
                dbf_ref[:, cols] += dv2
                dv2b = dv2.astype(BF16)
                dwg = lax.dot_general(dv2b, v1g, NT, preferred_element_type=F32)
                dw_ref[g] += jnp.where(mask, dwg, 0.0)
                dv1_ref[:, cols] = jnp.dot(wmt[g], dv2b, preferred_element_type=F32)
            dv1 = dv1_ref[...]
            dxhat = dv1 * gain
            dg_ref[...] += jnp.sum(dv1 * xhat, axis=0, keepdims=True)
            dv0 = r * (dxhat - xhat * jnp.mean(dxhat * xhat, axis=-1, keepdims=True))
            dz_ref[rows, pl.ds(f, f)] = (dv0 * dgelu[:, f:]).astype(BF16)

        @pl.when(step == n_steps - 1)
        def _():
            for g in range(A_GROUPS):
                db_ref[g] = jnp.sum(dbf_ref[:, g * gd:(g + 1) * gd], axis=1, keepdims=True)

    wspec = pl.BlockSpec((A_GROUPS, A_CHUNK, A_CHUNK), lambda i: (0, 0, 0))
    dz, dw, db, dg = _call(
        body, name, (n_steps,),
        [pl.BlockSpec((tm, f2), lambda i: (i, 0)),
         pl.BlockSpec((tm, f2), lambda i: (i, 0)),
         pl.BlockSpec((tm, f), lambda i: (i, 0)),
         pl.BlockSpec((1, f), lambda i: (0, 0)),
         wspec, wspec,
         pl.BlockSpec((A_CHUNK, f), lambda i: (0, 0))],
        [pl.BlockSpec((tm, f2), lambda i: (i, 0)),
         wspec,
         pl.BlockSpec((A_GROUPS, A_CHUNK, 1), lambda i: (0, 0, 0)),
         pl.BlockSpec((1, f), lambda i: (0, 0))],
        [jax.ShapeDtypeStruct((t, f2), BF16),
         jax.ShapeDtypeStruct((A_GROUPS, A_CHUNK, A_CHUNK), F32),
         jax.ShapeDtypeStruct((A_GROUPS, A_CHUNK, 1), F32),
         jax.ShapeDtypeStruct((1, f), F32)],
        scratch=[pltpu.VMEM((A_CHUNK, f), F32), pltpu.VMEM((A_CHUNK, f), F32)],
        vmem_bytes=6 * _nbytes((tm, f2), BF16) + 2 * _nbytes((tm, f), BF16) + 12 * _nbytes((A_CHUNK, f2), F32),
    )(zs, dgs, dp, g_sgu.reshape(1, f), w_sp, w_sp_t, b_full)
    return dz, dw, db.reshape(A_GROUPS, A_CHUNK), dg.reshape(f)


def _pair_valid(qi, col):
    qc = qi // CHUNK
    kc = col // CHUNK
    return (kc >= qc) & (kc <= qc + N_LEFT_CHUNKS)


def _diagonal_onehot():
    e = lax.broadcasted_iota(jnp.int32, (REL_PAD, DIAGONALS), 1)
    idx = jnp.clip(PAIR_BAND - 1 - e, -MAX_REL, MAX_REL) + MAX_REL
    r = lax.broadcasted_iota(jnp.int32, (REL_PAD, DIAGONALS), 0)
    return jnp.where(r == idx, 1.0, 0.0).astype(BF16)


def _bias_build(table, name):
    h = table.shape[0]
    tab = jnp.pad(table, ((0, 0), (0, REL_PAD - N_REL)))

    def body(t_ref, o_ref):
        oh = _diagonal_onehot()
        diag = jnp.zeros((h, DIAGONALS), F32)
        for piece in _split3(t_ref[...]):
            diag += jnp.dot(piece, oh, preferred_element_type=F32)
        col = lax.broadcasted_iota(jnp.int32, (h, PAIR_BAND), 1)
        for qi in range(PAIR_ROWS):
            row = pltpu.roll(diag, (qi - (PAIR_ROWS - 1)) % DIAGONALS, 1)[:, :PAIR_BAND]
            o_ref[qi] = jnp.where(_pair_valid(qi, col), row, NEG_INF)

    out = _call(
        body, name, (1,),
        [pl.BlockSpec((h, REL_PAD), lambda i: (0, 0))],
        pl.BlockSpec((PAIR_ROWS, h, PAIR_BAND), lambda i: (0, 0, 0)),
        jax.ShapeDtypeStruct((PAIR_ROWS, h, PAIR_BAND), F32),
        vmem_bytes=4 * _nbytes((PAIR_ROWS, h, PAIR_BAND), F32),
    )(tab)
    return jnp.transpose(out, (1, 0, 2))


def _bias_block(pair_bias):
    rest = K_BLOCK - PAIR_BAND
    return jnp.concatenate(
        [jnp.pad(pair_bias, ((0, 0), (0, 0), (p * PAIR_ROWS, rest - p * PAIR_ROWS)), constant_values=NEG_INF)
         for p in range(PAIRS_PER_BLOCK)], axis=1)


def _bias_grad(dbias, name):
    h = dbias.shape[0]
    db_t = jnp.transpose(dbias, (1, 0, 2))

    def body(d_ref, o_ref):
        diag = jnp.zeros((h, DIAGONALS), F32)
        for qi in range(PAIR_ROWS):
            diag += pltpu.roll(d_ref[qi], PAIR_ROWS - 1 - qi, 1)
        oh = _diagonal_onehot()
        acc = jnp.zeros((h, REL_PAD), F32)
        for piece in _split3(diag):
            acc += lax.dot_general(piece, oh, NT, preferred_element_type=F32)
        o_ref[...] = acc

    out = _call(
        body, name, (1,),
        [pl.BlockSpec((PAIR_ROWS, h, DIAGONALS), lambda i: (0, 0, 0))],
        pl.BlockSpec((h, REL_PAD), lambda i: (0, 0)),
        jax.ShapeDtypeStruct((h, REL_PAD), F32),
        vmem_bytes=4 * _nbytes((PAIR_ROWS, h, DIAGONALS), F32),
    )(db_t)
    return out[:, :N_REL]


def _head_masks():
    lane = lax.broadcasted_iota(jnp.int32, (Q_BLOCK, HEAD_PAIR), 1)
    return lane < HEAD_DIM, lane >= HEAD_DIM


def _block_scores(qm, kb, bias, valid):
    s = lax.dot_general(qm, kb, NT, preferred_element_type=F32) + bias
    return s if valid is None else jnp.where(valid, s, NEG_INF)


def _softmax_rows(s):
    e = jnp.exp(s - jnp.max(s, axis=-1, keepdims=True))
    return e * (1.0 / jnp.sum(e, axis=-1, keepdims=True))


def _block_probs(qm, kb, bias, valid):
    return _softmax_rows(_block_scores(qm, kb, bias, valid))


def _padded_then_plain(step, n_blocks):
    n_padded = min(LEFT // Q_BLOCK, n_blocks)
    lax.fori_loop(0, n_padded, lambda j, c: step(j, c, True), 0)
    lax.fori_loop(n_padded, n_blocks, lambda j, c: step(j, c, False), 0, unroll=ATTN_UNROLL)


def _attn_fwd(q, kvpad, bias, name):
    t, d = q.shape
    n_pairs = d // HEAD_PAIR
    n_blocks = t // Q_BLOCK

    def body(q_ref, k_ref, v_ref, b_ref, o_ref):
        masks = _head_masks()
        key = lax.broadcasted_iota(jnp.int32, (Q_BLOCK, K_BLOCK), 1)

        def step(j, carry, padded):
            r0 = pl.multiple_of(j * Q_BLOCK, Q_BLOCK)
            q2 = q_ref[pl.ds(r0, Q_BLOCK), :].astype(F32)
            kb = k_ref[pl.ds(r0, K_BLOCK), :]
            vb = v_ref[pl.ds(r0, K_BLOCK), :]
            valid = key >= LEFT - j * Q_BLOCK if padded else None
            scores = [_block_scores(jnp.where(masks[a], q2, 0.0).astype(BF16), kb, b_ref[a], valid) for a in range(2)]
            probs = [_softmax_rows(s).astype(BF16) for s in scores]
            outs = [jnp.dot(p, vb, preferred_element_type=F32) for p in probs]
            o_ref[pl.ds(r0, Q_BLOCK), :] = jnp.where(masks[0], outs[0], outs[1]).astype(BF16)
            return carry

        _padded_then_plain(step, n_blocks)

    return _call(
        body, name, (n_pairs,),
        [pl.BlockSpec((t, HEAD_PAIR), lambda p: (0, p)),
         pl.BlockSpec((LEFT + t, HEAD_PAIR), lambda p: (0, p)),
         pl.BlockSpec((LEFT + t, HEAD_PAIR), lambda p: (0, n_pairs + p)),
         pl.BlockSpec((2, Q_BLOCK, K_BLOCK), lambda p: (p, 0, 0))],
        pl.BlockSpec((t, HEAD_PAIR), lambda p: (0, p)),
        jax.ShapeDtypeStruct((t, d), BF16),
        vmem_bytes=8 * _nbytes((LEFT + t, HEAD_PAIR), BF16) + 12 * _nbytes((2, Q_BLOCK, K_BLOCK), F32),
    )(q, kvpad, kvpad, bias)


def _attn_bwd(q, kvpad, bias, do, dk_in, dv_in, name):
    t, d = q.shape
    n_pairs = d // HEAD_PAIR
    n_blocks = t // Q_BLOCK
    has_in = dk_in is not None

    def body(*refs):
        refs = list(refs)
        q_ref, k_ref, v_ref, b_ref, do_ref = refs[:5]
        refs = refs[5:]
        if has_in:
            dki_ref, dvi_ref = refs[:2]
            refs = refs[2:]
        dq_ref, dk_ref, dv_ref, db_ref = refs
        masks = _head_masks()
        key = lax.broadcasted_iota(jnp.int32, (Q_BLOCK, K_BLOCK), 1)
        if has_in:
            dk_ref[...] = dki_ref[...]
            dv_ref[...] = dvi_ref[...]
        else:
            dk_ref[...] = jnp.zeros_like(dk_ref)
            dv_ref[...] = jnp.zeros_like(dv_ref)
        db_ref[...] = jnp.zeros_like(db_ref)

        def step(j, carry, padded):
            r0 = pl.multiple_of(j * Q_BLOCK, Q_BLOCK)
            q2 = q_ref[pl.ds(r0, Q_BLOCK), :].astype(F32)
            do2 = do_ref[pl.ds(r0, Q_BLOCK), :].astype(F32)
            kb = k_ref[pl.ds(r0, K_BLOCK), :]
            vb = v_ref[pl.ds(r0, K_BLOCK), :]
            valid = key >= LEFT - j * Q_BLOCK if padded else None
            heads = range(2)
            qms = [jnp.where(masks[a], q2, 0.0).astype(BF16) for a in heads]
            doms = [jnp.where(masks[a], do2, 0.0).astype(BF16) for a in heads]
            scores = [_block_scores(qms[a], kb, b_ref[a], valid) for a in heads]
            dps = [lax.dot_general(doms[a], vb, NT, preferred_element_type=F32) for a in heads]
            ps = [_softmax_rows(s) for s in scores]
            dss = [ps[a] * (dps[a] - jnp.sum(dps[a] * ps[a], axis=-1, keepdims=True)) for a in heads]
            for a in heads:
                for pair in range(PAIRS_PER_BLOCK):
                    lo = pair * PAIR_ROWS
                    db_ref[a, :, pl.ds(0, PAIR_BAND)] += dss[a][lo:lo + PAIR_ROWS, lo:lo + PAIR_BAND]
            dsbs = [ds.astype(BF16) for ds in dss]
            pbs = [p.astype(BF16) for p in ps]
            dqs = [jnp.dot(dsbs[a], kb, preferred_element_type=F32) for a in heads]
            dk_acc = sum(lax.dot_general(dsbs[a], qms[a], TN, preferred_element_type=F32) for a in heads)
            dv_acc = sum(lax.dot_general(pbs[a], doms[a], TN, preferred_element_type=F32) for a in heads)
            dq = jnp.where(masks[0], dqs[0], dqs[1]) * ATTN_SCALE
            dq_ref[pl.ds(r0, Q_BLOCK), :] = dq.astype(BF16)
            dk_ref[pl.ds(r0, K_BLOCK), :] += dk_acc
            dv_ref[pl.ds(r0, K_BLOCK), :] += dv_acc
            return carry

        _padded_then_plain(step, n_blocks)

    q_spec = pl.BlockSpec((t, HEAD_PAIR), lambda p: (0, p))
    kv_spec = pl.BlockSpec((LEFT + t, HEAD_PAIR), lambda p: (0, p))
    operands = [q, kvpad, kvpad, bias, do]
    in_specs = [q_spec, kv_spec, pl.BlockSpec((LEFT + t, HEAD_PAIR), lambda p: (0, n_pairs + p)),
                pl.BlockSpec((2, Q_BLOCK, K_BLOCK), lambda p: (p, 0, 0)), q_spec]
    aliases = None
    if has_in:
        operands += [dk_in, dv_in]
        in_specs += [kv_spec, kv_spec]
        aliases = {5: 1, 6: 2}
    return _call(
        body, name, (n_pairs,),
        in_specs,
        [q_spec, kv_spec, kv_spec, pl.BlockSpec((2, PAIR_ROWS, DIAGONALS), lambda p: (p, 0, 0))],
        [jax.ShapeDtypeStruct((t, d), BF16),
         jax.ShapeDtypeStruct((LEFT + t, d), F32),
         jax.ShapeDtypeStruct((LEFT + t, d), F32),
         jax.ShapeDtypeStruct((d // HEAD_DIM, PAIR_ROWS, DIAGONALS), F32)],
        vmem_bytes=10 * _nbytes((LEFT + t, HEAD_PAIR), BF16) + 8 * _nbytes((LEFT + t, HEAD_PAIR), F32)
        + 16 * _nbytes((2, Q_BLOCK, K_BLOCK), F32),
        aliases=aliases,
    )(*operands)


def _loss_head(x, g, target, name):
    t, d = x.shape
    tm = _tile(t, 512)

    def body(x_ref, g_ref, t_ref, dx_ref, loss_ref, dg_ref):
        @pl.when(pl.program_id(0) == 0)
        def _():
            loss_ref[...] = jnp.zeros_like(loss_ref)
            dg_ref[...] = jnp.zeros_like(dg_ref)

        xf = x_ref[...]
        r = lax.rsqrt(jnp.mean(xf * xf, axis=-1, keepdims=True) + EPS)
        xhat = xf * r
        diff = xhat * g_ref[...] - t_ref[...]
        row_loss = jnp.mean(diff * diff, axis=-1, keepdims=True)
        loss_ref[...] += 0.5 * jnp.sum(row_loss, axis=0, keepdims=True)
        dy = diff * (1.0 / d)
        dg_ref[...] += jnp.sum(dy * xhat, axis=0, keepdims=True)
        dxhat = dy * g_ref[...]
        dx_ref[...] = r * (dxhat - xhat * jnp.mean(dxhat * xhat, axis=-1, keepdims=True))

    row = pl.BlockSpec((tm, d), lambda i: (i, 0))
    vec = pl.BlockSpec((1, d), lambda i: (0, 0))
    dx, loss, dg = _call(
        body, name, (t // tm,),
        [row, vec, row],
        [row, pl.BlockSpec((1, 1), lambda i: (0, 0)), vec],
        [jax.ShapeDtypeStruct((t, d), F32), jax.ShapeDtypeStruct((1, 1), F32), jax.ShapeDtypeStruct((1, d), F32)],
        vmem_bytes=10 * _nbytes((tm, d), F32),
    )(x, g.reshape(1, d), target)
    return dx, loss[0, 0], dg.reshape(d)


def _adamw_store(g, w_ref, m_ref, v_ref, g_ref, d_ref, nm_ref, nv_ref):
    c1 = 1.0 / (1.0 - ADAM_B1 ** ADAM_STEP)
    c2 = 1.0 / (1.0 - ADAM_B2 ** ADAM_STEP)
    nm = ADAM_B1 * m_ref[...] + (1.0 - ADAM_B1) * g
    nv = ADAM_B2 * v_ref[...] + (1.0 - ADAM_B2) * (g * g)
    g_ref[...] = g
    nm_ref[...] = nm
    nv_ref[...] = nv
    d_ref[...] = -ADAM_LR * ((nm * c1) / (jnp.sqrt(nv * c2) + ADAM_EPS) + ADAM_WD * w_ref[...])


def _adamw_layer(recv, own, w, m, v, layer, prev, me, name):
    n_src, r, c = recv.shape
    tr = _row_tile(r, max(16, (256 * 1024) // c), 16)
    first = prev is None

    def body(me_ref, recv_ref, own_ref, w_ref, m_ref, v_ref, *rest):
        mine = me_ref[0]
        own_part = own_ref[...].astype(F32)
        g = None
        for s in range(n_src):
            part = jnp.where(mine == s, own_part, recv_ref[s].astype(F32))
            g = part if g is None else g + part
        _adamw_store(g, w_ref, m_ref, v_ref, *rest[-4:])

    blk = pl.BlockSpec((None, tr, c), lambda i, me_ref: (layer, i, 0))
    any_spec = pl.BlockSpec(memory_space=pl.ANY)
    out = jax.ShapeDtypeStruct(w.shape, F32)
    operands = [me, recv, own, w, m, v] + ([] if first else list(prev))
    vmem = 2 * _nbytes((n_src + 1, tr, c), BF16) + 18 * _nbytes((tr, c), F32)
    return pl.pallas_call(
        body,
        name=name,
        grid_spec=pltpu.PrefetchScalarGridSpec(
            num_scalar_prefetch=1,
            grid=(r // tr,),
            in_specs=[pl.BlockSpec((n_src, tr, c), lambda i, me_ref: (0, i, 0)),
                      pl.BlockSpec((None, tr, c), lambda i, me_ref: (me_ref[0], i, 0)),
                      blk, blk, blk] + ([] if first else [any_spec] * 4),
            out_specs=[blk, blk, blk, blk],
        ),
        out_shape=[out, out, out, out],
        input_output_aliases={} if first else {6 + j: j for j in range(4)},
        compiler_params=pltpu.CompilerParams(
            dimension_semantics=("arbitrary",),
            vmem_limit_bytes=int(min(max(VMEM_FLOOR_BYTES, vmem * 5 // 4), VMEM_CEIL_BYTES))),
    )(*operands)


def _adamw(parts, w, m, v, name):
    n_layers, n_src, r, c = parts.shape
    mult = 16 if parts.dtype == BF16 else 8
    tr = _row_tile(r, max(mult, (256 * 1024) // c), mult)

    def body(p_ref, w_ref, m_ref, v_ref, g_ref, d_ref, nm_ref, nv_ref):
        g = p_ref[0].astype(F32)
        for s in range(1, n_src):
            g = g + p_ref[s].astype(F32)
        _adamw_store(g, w_ref, m_ref, v_ref, g_ref, d_ref, nm_ref, nv_ref)

    blk = pl.BlockSpec((None, tr, c), lambda l, i: (l, i, 0))
    out = jax.ShapeDtypeStruct((n_layers, r, c), F32)
    return _call(
        body, name, (n_layers, r // tr),
        [pl.BlockSpec((None, n_src, tr, c), lambda l, i: (l, 0, i, 0)), blk, blk, blk],
        [blk, blk, blk, blk],
        [out, out, out, out],
        vmem_bytes=2 * _nbytes((n_src, tr, c), parts.dtype) + 18 * _nbytes((tr, c), F32),
    )(parts, w, m, v)


def _ordered_sum(parts, name):
    n_src, r, c = parts.shape

    def body(p_ref, o_ref):
        acc = p_ref[0]
        for s in range(1, n_src):
            acc = acc + p_ref[s]
        o_ref[...] = acc

    return _call(
        body, name, (1,),
        [pl.BlockSpec((n_src, r, c), lambda i: (0, 0, 0))],
        pl.BlockSpec((r, c), lambda i: (0, 0)),
        jax.ShapeDtypeStruct((r, c), F32),
        vmem_bytes=4 * _nbytes((n_src, r, c), F32),
    )(parts)


def _position():
    return lax.axis_index("x"), lax.axis_index("y"), lax.axis_index("c")


def _linear(p):
    return 4 * p[0] + 2 * p[1] + p[2]


def _all_gather(shards, name):
    n = len(shards)

    def body(*refs):
        ins, outs = refs[:n], refs[n:2 * n]
        send_sems, recv_sems, local_sems = refs[2 * n:]
        x, y, c = _position()
        me, sibling = (x, y, c), (x, y, 1 - c)
        chips = [(1 - x, y), (x, 1 - y), (1 - x, 1 - y)]

        def slab(t, p):
            return outs[t].at[:, _linear(p)]

        def copy(t, k, block, to, src=None):
            return pltpu.make_async_remote_copy(
                src_ref=slab(t, block) if src is None else src,
                dst_ref=slab(t, block),
                send_sem=send_sems.at[t, k],
                recv_sem=recv_sems.at[t, k],
                device_id=to,
                device_id_type=MESH,
            )

        started = []
        for t in range(n):
            mine = pltpu.make_async_copy(ins[t], slab(t, me), local_sems.at[t])
            mine.start()
            started.append(mine)
        sends = []
        for t in range(n):
            first = [copy(t, 0, me, sibling, src=ins[t])]
            first += [copy(t, 1 + j, me, (*chip, c), src=ins[t]) for j, chip in enumerate(chips)]
            for cp in first:
                cp.start()
            sends += first
        for t in range(n):
            for j, chip in enumerate(chips):
                copy(t, 1 + j, (*chip, c), me).wait_recv()
                passed = copy(t, 4 + j, (*chip, c), sibling)
                passed.start()
                sends.append(passed)
        for t in range(n):
            copy(t, 0, sibling, me).wait_recv()
            for j, chip in enumerate(chips):
                copy(t, 4 + j, (*chip, 1 - c), me).wait_recv()
        for cp in sends:
            cp.wait_send()
        for mine in started:
            mine.wait()

    out_shape = [jax.ShapeDtypeStruct((s.shape[0], N_DEV) + s.shape[1:], s.dtype) for s in shards]
    return pl.pallas_call(
        body,
        name=name,
        in_specs=[HBM_SPEC] * n,
        out_specs=[HBM_SPEC] * n,
        out_shape=out_shape,
        scratch_shapes=[
            pltpu.SemaphoreType.DMA((n, N_DEV - 1)),
            pltpu.SemaphoreType.DMA((n, N_DEV - 1)),
            pltpu.SemaphoreType.DMA((n,)),
        ],
    )(*shards)


def _exchange(blocks, name):
    n = len(blocks)

    def body(*refs):
        ins, outs = refs[:n], refs[n:2 * n]
        send_sems, recv_sems, local_sems = refs[2 * n:]
        x, y, c = _position()
        me = _linear((x, y, c))
        flips = [(fx, fy, fc) for fx in (0, 1) for fy in (0, 1) for fc in (0, 1)][1:]

        def peer_of(flip):
            fx, fy, fc = flip
            return (1 - x if fx else x, 1 - y if fy else y, 1 - c if fc else c)

        def copy(t, k, peer):
            return pltpu.make_async_remote_copy(
                src_ref=ins[t].at[:, _linear(peer)],
                dst_ref=outs[t].at[:, me],
                send_sem=send_sems.at[t, k],
                recv_sem=recv_sems.at[t, k],
                device_id=peer,
                device_id_type=MESH,
            )

        def arrival(t, k, peer):
            return pltpu.make_async_remote_copy(
                src_ref=ins[t].at[:, _linear(peer)],
                dst_ref=outs[t].at[:, _linear(peer)],
                send_sem=send_sems.at[t, k],
                recv_sem=recv_sems.at[t, k],
                device_id=peer,
                device_id_type=MESH,
            )

        own = []
        for t in range(n):
            cp = pltpu.make_async_copy(ins[t].at[:, me], outs[t].at[:, me], local_sems.at[t])
            cp.start()
            own.append(cp)
        sends = []
        for t in range(n):
            for k, flip in enumerate(flips):
                cp = copy(t, k, peer_of(flip))
                cp.start()
                sends.append(cp)
        for t in range(n):
            for k, flip in enumerate(flips):
                arrival(t, k, peer_of(flip)).wait_recv()
        for cp in sends:
            cp.wait_send()
        for cp in own:
            cp.wait()

    out_shape = [jax.ShapeDtypeStruct(b.shape, b.dtype) for b in blocks]
    return pl.pallas_call(
        body,
        name=name,
        in_specs=[HBM_SPEC] * n,
        out_specs=[HBM_SPEC] * n,
        out_shape=out_shape,
        scratch_shapes=[
            pltpu.SemaphoreType.DMA((n, N_DEV - 1)),
            pltpu.SemaphoreType.DMA((n, N_DEV - 1)),
            pltpu.SemaphoreType.DMA((n,)),
        ],
    )(*blocks)


def _peers():
    x, y, c = _position()
    flips = [(fx, fy, fc) for fx in (0, 1) for fy in (0, 1) for fc in (0, 1)][1:]
    return [(1 - x if fx else x, 1 - y if fy else y, 1 - c if fc else c) for fx, fy, fc in flips]


def _split_start(srcs, lands, carry, name, exchange=False):
    n = len(srcs)

    def body(*refs):
        src_refs, land_refs = refs[:n], refs[n:2 * n]
        send_sems, recv_sems = refs[2 * n + 1], refs[2 * n + 2]
        me = _linear(_position())
        for t in range(n):
            for k, peer in enumerate(_peers()):
                pltpu.make_async_remote_copy(
                    src_ref=src_refs[t].at[_linear(peer)] if exchange else src_refs[t],
                    dst_ref=land_refs[t].at[me],
                    send_sem=send_sems.at[t * (N_DEV - 1) + k],
                    recv_sem=recv_sems.at[t * (N_DEV - 1) + k],
                    device_id=peer,
                    device_id_type=MESH,
                ).start()

    operands = list(srcs) + list(lands) + [carry]
    sems = pltpu.SemaphoreType.DMA((n * (N_DEV - 1),))
    out = pl.pallas_call(
        body,
        name=name,
        in_specs=[HBM_SPEC] * len(operands),
        out_specs=[SEM_SPEC, SEM_SPEC] + [HBM_SPEC] * len(operands),
        out_shape=[sems, sems] + [pltpu.HBM(a.shape, a.dtype) for a in operands],
        input_output_aliases={i: 2 + i for i in range(len(operands))},
        compiler_params=pltpu.CompilerParams(has_side_effects=pltpu.SideEffectType.DATAFLOW_SIDE_EFFECTING),
    )(*[pltpu.with_memory_space_constraint(a, pltpu.HBM) for a in operands])
    return out[0], out[1], out[2:2 + n], out[2 + n:2 + 2 * n], out[2 + 2 * n]


def _split_wait(send_sems, recv_sems, srcs, lands, after, name, exchange=False):
    n = len(srcs)

    def body(*refs):
        src_refs, land_refs = refs[:n], refs[n:2 * n]
        send_ref, recv_ref = refs[2 * n], refs[2 * n + 1]
        for t in range(n):
            for k, peer in enumerate(_peers()):
                copy = pltpu.make_async_remote_copy(
                    src_ref=src_refs[t].at[0] if exchange else src_refs[t],
                    dst_ref=land_refs[t].at[0],
                    send_sem=send_ref.at[t * (N_DEV - 1) + k],
                    recv_sem=recv_ref.at[t * (N_DEV - 1) + k],
                    device_id=peer,
                    device_id_type=MESH,
                )
                copy.wait_send()
                copy.wait_recv()

    arrays = list(srcs) + list(lands)
    out = pl.pallas_call(
        body,
        name=name,
        in_specs=[HBM_SPEC] * len(arrays) + [SEM_SPEC, SEM_SPEC, pl.BlockSpec(memory_space=pl.ANY)],
        out_specs=[HBM_SPEC] * len(arrays),
        out_shape=[pltpu.HBM(a.shape, a.dtype) for a in arrays],
        input_output_aliases={i: i for i in range(len(arrays))},
        compiler_params=pltpu.CompilerParams(has_side_effects=pltpu.SideEffectType.DATAFLOW_SIDE_EFFECTING),
    )(*arrays, send_sems, recv_sems, after)
    return out[:n], out[n:]


def _pack(arrays, row_multiple):
    flat = jnp.concatenate([a.reshape(-1) for a in arrays])
    quantum = row_multiple * FLAT_LANES
    padded = -(-flat.shape[0] // quantum) * quantum
    return jnp.pad(flat, (0, padded - flat.shape[0])).reshape(-1, FLAT_LANES)


def _unpack(flat, like):
    flat = flat.reshape(-1)
    out, at = [], 0
    for a in like:
        size = math.prod(a.shape)
        out.append(flat[at:at + size].reshape(a.shape))
        at += size
    return out


def kernel(x, a_norm, a_w_in, a_sgu_norm, a_w_spatial, a_b_spatial, a_w_out, kv_norm, w_kv, b_norm, b_w_q, b_rel_bias, b_w_o, ffn_norm, ffn_w_gate_up, ffn_w_down, final_norm, loss_target, m_a_norm, m_a_w_in, m_a_sgu_norm, m_a_w_spatial, m_a_b_spatial, m_a_w_out, m_kv_norm, m_w_kv, m_b_norm, m_b_w_q, m_b_rel_bias, m_b_w_o, m_ffn_norm, m_ffn_w_gate_up, m_ffn_w_down, m_final_norm, v_a_norm, v_a_w_in, v_a_sgu_norm, v_a_w_spatial, v_a_b_spatial, v_a_w_out, v_kv_norm, v_w_kv, v_b_norm, v_b_w_q, v_b_rel_bias, v_b_w_o, v_ffn_norm, v_ffn_w_gate_up, v_ffn_w_down, v_final_norm):
    xs = x[0]
    target = loss_target[0]
    t, d = xs.shape
    n_a = a_w_in.shape[0]
    n_b = b_w_q.shape[0]
    depth = ffn_w_gate_up.shape[0]
    f_a = a_w_out.shape[1] * N_DEV
    gd = f_a // A_GROUPS
    nb_ffn = ffn_w_gate_up.shape[2]
    me = _linear(_position())

    small_rows = -(-(a_norm.size + a_sgu_norm.size) // (8 * 128)) * 8
    small = jnp.pad(jnp.concatenate([a_norm.reshape(-1), a_sgu_norm.reshape(-1)]),
                    (0, small_rows * 128 - a_norm.size - a_sgu_norm.size)).reshape(1, small_rows, 128)

    def shard(w, layer=None):
        return (w if layer is None else w[layer]).astype(BF16)

    stages = []
    for layer in range(depth):
        if layer == 0:
            stages += [("a0", [shard(a_w_in, 0)]), ("a0_out", [shard(a_w_out, 0)])]
        elif layer < n_a:
            stages.append((f"a{layer}", [shard(a_w_in, layer), shard(a_w_out, layer)]))
        else:
            i = layer - n_a
            shared = [shard(w_kv)] if i == 0 else []
            stages.append((f"b{i}", shared + [shard(b_w_q, i), shard(b_w_o, i)]))
        stages.append((f"f{layer}", [shard(ffn_w_gate_up, layer), shard(ffn_w_down, layer)]))
    first = _all_gather([s[None] for s in stages[0][1]] + [small], "gather_first")
    gathered = {stages[0][0]: [g[0] for g in first[:-1]]}
    small_g = first[-1].reshape(N_DEV, -1)
    a_norm_full = small_g[:, :a_norm.size].reshape(N_DEV, n_a, -1).transpose(1, 0, 2).reshape(n_a, d)
    a_sgu_full = small_g[:, a_norm.size:a_norm.size + a_sgu_norm.size].reshape(
        N_DEV, n_a, -1).transpose(1, 0, 2).reshape(n_a, f_a)
    in_flight = {}
    for key, shards in stages[1:]:
        lands = [lax.dynamic_update_slice(lax.empty((N_DEV,) + s.shape, BF16), s[None], (me, 0, 0)) for s in shards]
        send, recv, srcs, lands, a_norm_full = _split_start(shards, lands, a_norm_full, f"gather_start_{key}")
        in_flight[key] = (send, recv, srcs, lands)

    def weights(key, after):
        if key not in gathered:
            _, gathered[key] = _split_wait(*in_flight.pop(key), after, f"gather_wait_{key}")
        return gathered[key]

    rows_down = ffn_w_down.shape[1]

    def mixer_a_weights(i, after):
        if i == 0:
            (w_in,), (w_out,) = weights("a0", after[0]), weights("a0_out", after[1])
        else:
            w_in, w_out = weights(f"a{i}", after[0])
        return w_in[None], w_out.reshape(1, f_a, d)

    def mixer_b_weights(i, after):
        ws = weights(f"b{i}", after)
        return ws[-2].reshape(1, d, d), ws[-1].reshape(1, d, d)

    def ffn_weights(layer, after):
        w_gu, w_dn = weights(f"f{layer}", after)
        return w_gu[None], w_dn.reshape(1, N_DEV // 2, 2 * rows_down, d)

    w_sp_t = jnp.swapaxes(a_w_spatial, -1, -2)
    b_full = jnp.repeat(jnp.swapaxes(a_b_spatial, -1, -2), gd, axis=-1)

    saved = []

    def ffn_fwd(xin, layer):
        hf = _rms_fwd(xin, ffn_norm[layer], f"ffn_norm_fwd_{layer}")
        w_gu, w_dn = ffn_weights(layer, xin)
        gu, act = _ffn_gate_up(f"ffn_gate_up_{layer}", hf, w_gu, 0)
        xout = _mm_down(f"ffn_down_{layer}", act, w_dn, 0, xin)
        return xout, (xin, hf, gu, act)

    for i in range(n_a):
        h = _rms_fwd(xs, a_norm_full[i], f"a_norm_fwd_{i}")
        zpre = _mm_colblock(f"a_in_{i}", h, weights(f"a{i}", xs)[0][None], 0)
        p, zs, dgs = _sgu_fwd(zpre, a_sgu_full[i], a_w_spatial[i], b_full[i], f"a_sgu_fwd_{i}")
        w_in, w_out = mixer_a_weights(i, (xs, p))
        x_mid = _mm_natural(f"a_out_{i}", p, w_out, 0, res=xs)
        x_out, ffn_saved = ffn_fwd(x_mid, i)
        saved.append((xs, h, zs, dgs, p, ffn_saved))
        xs = x_out

    x_kv = xs
    w_kv_g = weights("b0", x_kv)[0][None]
    h_kv = _rms_fwd(x_kv, kv_norm, "kv_norm_fwd")
    kv = _mm_colblock("kv_proj", h_kv, w_kv_g, 0)
    kvpad = jnp.pad(kv, ((LEFT, 0), (0, 0)))

    biases = [_bias_block(_bias_build(b_rel_bias[i], f"rel_bias_{i}")) for i in range(n_b)]
    for i in range(n_b):
        layer = n_a + i
        w_q, w_o = mixer_b_weights(i, xs)
        hb = _rms_fwd(xs, b_norm[i], f"b_norm_fwd_{i}")
        q = _mm_natural(f"b_q_{i}", hb, w_q, 0, out_dtype=BF16, scale=ATTN_SCALE)
        o = _attn_fwd(q, kvpad, biases[i], f"b_attn_fwd_{i}")
        x_mid = _mm_natural(f"b_o_{i}", o, w_o, 0, res=xs)
        x_out, ffn_saved = ffn_fwd(x_mid, layer)
        saved.append((xs, hb, q, o, ffn_saved))
        xs = x_out

    dx, loss_local, g_final = _loss_head(xs, final_norm, target, "loss_head")
    loss = lax.psum(loss_local, ("x", "y", "c"))

    big_grads = {}
    pending = []
    in_flight_grads = []

    def start_exchange(dx, tag):
        srcs = [big_grads[key] for key in pending]
        lands = [lax.empty(s.shape, BF16) for s in srcs]
        send, recv, srcs, lands, dx = _split_start(srcs, lands, dx, f"exchange_start_{tag}", exchange=True)
        in_flight_grads.append((list(pending), send, recv, srcs, lands, tag))
        pending.clear()
        return dx

    g_ffn_norm = [None] * depth
    g_a_norm = [None] * n_a
    g_a_sgu = [None] * n_a
    g_w_sp = [None] * n_a
    g_b_sp = [None] * n_a
    g_b_norm = [None] * n_b
    g_rel = [None] * n_b

    def ffn_bwd(dx, layer, ffn_saved):
        eager = layer < n_a
        xin, hf, gu, act = ffn_saved
        big_grads["ffn_w_down", layer] = _mm_dw_down(f"ffn_down_dw_{layer}", act, dx)
        pending.append(("ffn_w_down", layer))
        if eager:
            dx = start_exchange(dx, f"f{layer}_down")
        w_gu, w_dn = ffn_weights(layer, xin)
        dgu = _ffn_down_dx(f"ffn_down_dx_{layer}", dx, w_dn, 0, gu).reshape(N_DEV, t, nb_ffn)
        big_grads["ffn_w_gate_up", layer] = _mm_dw_colblock(
            f"ffn_gate_up_dw_{layer}", hf, dgu, blocked_in=True, transposed=True)
        pending.append(("ffn_w_gate_up", layer))
        if eager:
            dx = start_exchange(dx, f"f{layer}_gate_up")
        dx, g_ffn_norm[layer] = _mm_t_colblock_norm_bwd(
            f"ffn_gate_up_dx_{layer}", dgu, w_gu, 0, xin, ffn_norm[layer], dx, blocked_in=True)
        return dx

    dk = dv = None
    for i in reversed(range(n_b)):
        layer = n_a + i
        x_in, hb, q, o, ffn_saved = saved[layer]
        dx = ffn_bwd(dx, layer, ffn_saved)
        big_grads["b_w_o", i] = _mm_dw_natural(f"b_o_dw_{i}", o, dx)
        w_q, w_o = mixer_b_weights(i, x_in)
        do = _mm_t_natural(f"b_o_dx_{i}", dx, w_o, 0)
        dq, dk, dv, dbias = _attn_bwd(q, kvpad, biases[i], do, dk, dv, f"b_attn_bwd_{i}")
        g_rel[i] = _bias_grad(dbias, f"rel_bias_grad_{i}")
        big_grads["b_w_q", i] = _mm_dw_natural(f"b_q_dw_{i}", hb, dq)
        pending.extend([("b_w_o", i), ("b_w_q", i)])
        dh = _mm_t_natural(f"b_q_dx_{i}", dq, w_q, 0)
        dx, g_b_norm[i] = _rms_bwd(x_in, b_norm[i], dh, dx, f"b_norm_bwd_{i}")
        if i > 0:
            dx = start_exchange(dx, f"b{i}")

    dkv = jnp.concatenate([dk[LEFT:], dv[LEFT:]], axis=1).astype(BF16)
    big_grads["w_kv", 0] = _mm_dw_colblock("kv_proj_dw", h_kv, dkv)
    pending.append(("w_kv", 0))
    dx, g_kv_norm = _mm_t_colblock_norm_bwd("kv_proj_dx", dkv, w_kv_g, 0, x_kv, kv_norm, dx)
    dx = start_exchange(dx, "kv")

    for i in reversed(range(n_a)):
        x_in, h, zs, dgs, p, ffn_saved = saved[i]
        dx = ffn_bwd(dx, i, ffn_saved)
        big_grads["a_w_out", i] = _mm_dw_natural(f"a_out_dw_{i}", p, dx)
        pending.append(("a_w_out", i))
        dx = start_exchange(dx, f"a{i}_out")
        w_in, w_out = mixer_a_weights(i, (x_in, p))
        dp = _mm_t_natural(f"a_out_dx_{i}", dx, w_out, 0)
        dz, g_w_sp[i], g_b_sp[i], g_a_sgu[i] = _sgu_bwd(
            zs, dgs, dp, a_sgu_full[i], a_w_spatial[i], w_sp_t[i], b_full[i], f"a_sgu_bwd_{i}")
        big_grads["a_w_in", i] = _mm_dw_colblock(f"a_in_dw_{i}", h, dz)
        pending.append(("a_w_in", i))
        dx = start_exchange(dx, f"a{i}_in")
        dx, g_a_norm[i] = _mm_t_colblock_norm_bwd(f"a_in_dx_{i}", dz, w_in, 0, x_in, a_norm_full[i], dx)
    grad_x = dx[None]

    small_like = [jax.ShapeDtypeStruct((n_a, d), F32), jax.ShapeDtypeStruct((n_a, f_a), F32),
                  a_w_spatial, a_b_spatial, kv_norm, b_norm, b_rel_bias, ffn_norm, final_norm]
    small_partial = _pack(
        [jnp.stack(g_a_norm), jnp.stack(g_a_sgu), jnp.stack(g_w_sp), jnp.stack(g_b_sp), g_kv_norm,
         jnp.stack(g_b_norm), jnp.stack(g_rel), jnp.stack(g_ffn_norm), g_final], N_DEV * 8)
    chunk_rows = small_partial.shape[0] // N_DEV
    arrived = {}
    for keys, send, recv, srcs, lands, tag in in_flight_grads:
        srcs, lands = _split_wait(send, recv, srcs, lands, dx, f"exchange_wait_{tag}", exchange=True)
        for key, src, land in zip(keys, srcs, lands):
            arrived[key] = (land, src)
    small_got = _exchange([small_partial.reshape(1, N_DEV, chunk_rows, FLAT_LANES)], "exchange_small")[0]
    small_sum = _ordered_sum(small_got[0], "small_grad_sum")
    small_all = _all_gather([small_sum[None]], "gather_small_grads")[0]
    (ga_norm, ga_sgu, gw_sp, gb_sp, gkv_norm, gb_norm, g_relb, gffn_norm, gfinal) = _unpack(small_all, small_like)

    results = {}
    big_names = ["a_w_in", "a_w_out", "w_kv", "b_w_q", "b_w_o", "ffn_w_gate_up", "ffn_w_down"]
    big_wmv = [(a_w_in, m_a_w_in, v_a_w_in), (a_w_out, m_a_w_out, v_a_w_out),
               (w_kv[None], m_w_kv[None], v_w_kv[None]), (b_w_q, m_b_w_q, v_b_w_q), (b_w_o, m_b_w_o, v_b_w_o),
               tuple(jnp.swapaxes(a, 1, 2) for a in (ffn_w_gate_up, m_ffn_w_gate_up, v_ffn_w_gate_up)),
               (ffn_w_down, m_ffn_w_down, v_ffn_w_down)]
    me_arr = jnp.reshape(me, (1,)).astype(jnp.int32)
    for name, (w, m, v) in zip(big_names, big_wmv):
        outs = None
        for layer in range(w.shape[0]):
            got, own = arrived[name, layer]
            outs = _adamw_layer(got, own, w, m, v, layer, outs, me_arr, f"adamw_{name}_{layer}")
        if name == "w_kv":
            outs = [o[0] for o in outs]
        if name == "ffn_w_gate_up":
            outs = [jnp.swapaxes(o, 1, 2) for o in outs]
        results[name] = outs

    n_cols = a_norm.shape[1]
    s_cols = a_sgu_norm.shape[1]
    small_g_list = [lax.dynamic_slice(ga_norm, (0, me * n_cols), (n_a, n_cols)),
                    lax.dynamic_slice(ga_sgu, (0, me * s_cols), (n_a, s_cols)),
                    gw_sp, gb_sp, gkv_norm, gb_norm, g_relb, gffn_norm, gfinal]
    small_names = ["a_norm", "a_sgu_norm", "a_w_spatial", "a_b_spatial", "kv_norm", "b_norm", "b_rel_bias",
                   "ffn_norm", "final_norm"]
    small_w = [a_norm, a_sgu_norm, a_w_spatial, a_b_spatial, kv_norm, b_norm, b_rel_bias, ffn_norm, final_norm]
    small_m = [m_a_norm, m_a_sgu_norm, m_a_w_spatial, m_a_b_spatial, m_kv_norm, m_b_norm, m_b_rel_bias,
               m_ffn_norm, m_final_norm]
    small_v = [v_a_norm, v_a_sgu_norm, v_a_w_spatial, v_a_b_spatial, v_kv_norm, v_b_norm, v_b_rel_bias,
               v_ffn_norm, v_final_norm]
    flat_g = _pack(small_g_list, 8)
    flat_out = _adamw(flat_g[None, None], _pack(small_w, 8)[None], _pack(small_m, 8)[None],
                      _pack(small_v, 8)[None], "adamw_small")
    unpacked = [_unpack(o[0], small_w) for o in flat_out]
    for idx, name in enumerate(small_names):
        results[name] = [unpacked[kind][idx] for kind in range(4)]

    order = ["a_norm", "a_w_in", "a_sgu_norm", "a_w_spatial", "a_b_spatial", "a_w_out", "kv_norm", "w_kv",
             "b_norm", "b_w_q", "b_rel_bias", "b_w_o", "ffn_norm", "ffn_w_gate_up", "ffn_w_down", "final_norm"]
    outputs = [loss, grad_x]
    for kind in range(4):
        outputs += [results[name][kind] for name in order]
    return tuple(outputs)
```

```python
import math

import jax
import jax.numpy as jnp
from jax import lax
from jax.experimental import pallas as pl
from jax.experimental.pallas import tpu as pltpu

F32 = jnp.float32
BF16 = jnp.bfloat16
MESH = pl.DeviceIdType.MESH
HBM_SPEC = pl.BlockSpec(memory_space=pltpu.HBM)
SEM_SPEC = pl.BlockSpec(memory_space=pltpu.SEMAPHORE)

N_DEV = 8
CHUNK = 64
A_CHUNK = 128
A_GROUPS = 8
N_LEFT_CHUNKS = 8
LEFT = N_LEFT_CHUNKS * CHUNK
PAIR_ROWS = 2 * CHUNK
PAIR_BAND = PAIR_ROWS + LEFT
DIAGONALS = PAIR_BAND + PAIR_ROWS
PAIRS_PER_BLOCK = 2
Q_BLOCK = PAIRS_PER_BLOCK * PAIR_ROWS
K_BLOCK = Q_BLOCK + LEFT
ATTN_UNROLL = 2
MAX_REL = 256
N_REL = 2 * MAX_REL + 1
REL_PAD = 640
HEAD_DIM = 64
HEAD_PAIR = 2 * HEAD_DIM
ATTN_SCALE = HEAD_DIM ** -0.5
EPS = 1e-6
NEG_INF = -1e30
ADAM_LR = 0.001
ADAM_B1 = 0.9
ADAM_B2 = 0.999
ADAM_EPS = 1e-08
ADAM_WD = 0.01
ADAM_STEP = 10
FLAT_LANES = 1024
V7X_VMEM_BYTES = 64 * 1024 * 1024
VMEM_FLOOR_BYTES = 32 * 1024 * 1024
VMEM_CEIL_BYTES = V7X_VMEM_BYTES - 8 * 1024 * 1024

NN = (((1,), (0,)), ((), ()))
NT = (((1,), (1,)), ((), ()))
TN = (((0,), (0,)), ((), ()))


def _tile(n, pref):
    return pref if n % pref == 0 else n


def _row_tile(n, pref, mult):
    best = None
    for t in range(mult, min(n, pref) + 1, mult):
        if n % t == 0:
            best = t
    return best if best is not None else n


def _nbytes(shape, dtype):
    n = 1
    for s in shape:
        if s is not None:
            n *= s
    return n * jnp.dtype(dtype).itemsize


def _call(body, name, grid, in_specs, out_specs, out_shape, scratch=(), vmem_bytes=0, aliases=None):
    limit = int(min(max(VMEM_FLOOR_BYTES, vmem_bytes * 5 // 4), VMEM_CEIL_BYTES))
    return pl.pallas_call(
        body,
        name=name,
        grid=grid,
        in_specs=in_specs,
        out_specs=out_specs,
        out_shape=out_shape,
        scratch_shapes=list(scratch),
        input_output_aliases=aliases or {},
        compiler_params=pltpu.CompilerParams(
            dimension_semantics=("arbitrary",) * len(grid), vmem_limit_bytes=limit),
    )


ERFC_P = 0.3275911 / math.sqrt(2.0)
ERFC_HALF_COEFFS = tuple(0.5 * a for a in (1.061405429, -1.453152027, 1.421413741, -0.284496736, 0.254829592))


def _gelu_and_grad(x):
    d = 1.0 + ERFC_P * jnp.abs(x)
    r = pl.reciprocal(d, approx=True)
    t = r * (2.0 - d * r)
    a5, a4, a3, a2, a1 = ERFC_HALF_COEFFS
    ex = jnp.exp(-0.5 * (x * x))
    tail = ((((a5 * t + a4) * t + a3) * t + a2) * t + a1) * t * ex
    cdf = jnp.where(x < 0, tail, 1.0 - tail)
    return x * cdf, cdf + x * ex * (1.0 / math.sqrt(2.0 * math.pi))


def _sigmoid(x):
    return 1.0 / (1.0 + jnp.exp(-x))


def _split3(x):
    hi = x.astype(BF16)
    r1 = x - hi.astype(F32)
    mid = r1.astype(BF16)
    lo = (r1 - mid.astype(F32)).astype(BF16)
    return hi, mid, lo


def _rms_fwd(x, g, name):
    t, d = x.shape
    tm = _tile(t, 512)

    def body(x_ref, g_ref, o_ref):
        xf = x_ref[...]
        r = lax.rsqrt(jnp.mean(xf * xf, axis=-1, keepdims=True) + EPS)
        o_ref[...] = (xf * r * g_ref[...]).astype(o_ref.dtype)

    return _call(
        body, name, (t // tm,),
        [pl.BlockSpec((tm, d), lambda i: (i, 0)), pl.BlockSpec((1, d), lambda i: (0, 0))],
        pl.BlockSpec((tm, d), lambda i: (i, 0)),
        jax.ShapeDtypeStruct((t, d), BF16),
        vmem_bytes=2 * (_nbytes((tm, d), F32) + _nbytes((tm, d), BF16)) + 4 * _nbytes((tm, d), F32),
    )(x, g.reshape(1, d))


def _rms_bwd(x, g, dh, dx_up, name):
    t, d = x.shape
    tm = _tile(t, 512)

    def body(x_ref, g_ref, dh_ref, up_ref, dx_ref, dg_ref):
        @pl.when(pl.program_id(0) == 0)
        def _():
            dg_ref[...] = jnp.zeros_like(dg_ref)

        xf = x_ref[...]
        r = lax.rsqrt(jnp.mean(xf * xf, axis=-1, keepdims=True) + EPS)
        xhat = xf * r
        dy = dh_ref[...].astype(F32)
        dxhat = dy * g_ref[...]
        dg_ref[...] += jnp.sum(dy * xhat, axis=0, keepdims=True)
        dx = r * (dxhat - xhat * jnp.mean(dxhat * xhat, axis=-1, keepdims=True))
        dx_ref[...] = up_ref[...] + dx

    row = pl.BlockSpec((tm, d), lambda i: (i, 0))
    vec = pl.BlockSpec((1, d), lambda i: (0, 0))
    dx, dg = _call(
        body, name, (t // tm,),
        [row, vec, row, row],
        [row, vec],
        [jax.ShapeDtypeStruct((t, d), F32), jax.ShapeDtypeStruct((1, d), F32)],
        vmem_bytes=10 * _nbytes((tm, d), F32),
    )(x, g.reshape(1, d), dh, dx_up)
    return dx, dg.reshape(d)


def _mm(name, dims, a, b, *, grid, a_spec, b_spec, out_shape, out_spec, acc_shape,
        res=None, res_spec=None, scale=None):
    nk = grid[2]
    has_res = res is not None

    def body(*refs):
        refs = list(refs)
        a_ref = refs.pop(0)
        b_ref = refs.pop(0)
        r_ref = refs.pop(0) if has_res else None
        o_ref = refs.pop(0)
        part = lax.dot_general(a_ref[...].astype(BF16), b_ref[...].astype(BF16), dims,
                               preferred_element_type=F32)

        def finish(acc):
            if scale is not None:
                acc = acc * scale
            if has_res:
                acc = acc + r_ref[...]
            o_ref[...] = acc.astype(o_ref.dtype)

        if nk == 1:
            finish(part)
        else:
            acc_ref = refs.pop(0)
            k = pl.program_id(2)

            @pl.when(k == 0)
            def _():
                acc_ref[...] = part

            @pl.when(k > 0)
            def _():
                acc_ref[...] += part

            @pl.when(k == nk - 1)
            def _():
                finish(acc_ref[...])

    operands = [a, b]
    in_specs = [a_spec, b_spec]
    vmem = 2 * (_nbytes(a_spec.block_shape, a.dtype) + _nbytes(b_spec.block_shape, b.dtype)
                + _nbytes(out_spec.block_shape, out_shape.dtype))
    vmem += 3 * _nbytes(acc_shape, F32)
    if has_res:
        operands.append(res)
        in_specs.append(res_spec)
        vmem += 2 * _nbytes(res_spec.block_shape, res.dtype)
    scratch = [pltpu.VMEM(acc_shape, F32)] if nk > 1 else []
    return _call(body, name, grid, in_specs, out_spec, out_shape, scratch=scratch, vmem_bytes=vmem)(*operands)


def _norm_mm_colblock(name, x, g, w_g, layer):
    t, k = x.shape
    nb = w_g.shape[3]
    tm = _tile(t, 1024)

    def body(x_ref, g_ref, w_ref, h_ref, o_ref, h_scr):
        @pl.when(pl.program_id(1) == 0)
        def _():
            xf = x_ref[...]
            r = lax.rsqrt(jnp.mean(xf * xf, axis=-1, keepdims=True) + EPS)
            hb = (xf * r * g_ref[...]).astype(BF16)
            h_scr[...] = hb
            h_ref[...] = hb

        o_ref[...] = jnp.dot(h_scr[...], w_ref[...], preferred_element_type=F32).astype(BF16)

    return _call(
        body, name, (t // tm, N_DEV),
        [pl.BlockSpec((tm, k), lambda i, j: (i, 0)),
         pl.BlockSpec((1, k), lambda i, j: (0, 0)),
         pl.BlockSpec((None, None, k, nb), lambda i, j: (layer, j, 0, 0))],
        [pl.BlockSpec((tm, k), lambda i, j: (i, 0)), pl.BlockSpec((tm, nb), lambda i, j: (i, j))],
        [jax.ShapeDtypeStruct((t, k), BF16), jax.ShapeDtypeStruct((t, N_DEV * nb), BF16)],
        scratch=[pltpu.VMEM((tm, k), BF16)],
        vmem_bytes=2 * (_nbytes((tm, k), F32) + _nbytes((tm, k), BF16) + _nbytes((k, nb), BF16)
                        + _nbytes((tm, nb), BF16)) + _nbytes((tm, k), BF16) + 4 * _nbytes((tm, nb), F32),
    )(x, g.reshape(1, k), w_g)


def _mm_natural(name, a, w, layer, *, res=None, out_dtype=F32, scale=None):
    t, k = a.shape
    n = w.shape[2]
    tm = _tile(t, 1024)
    tn = _tile(n, 512)
    res_spec = None if res is None else pl.BlockSpec((tm, tn), lambda i, j, kk: (i, j))
    return _mm(
        name, NN, a, w, grid=(t // tm, n // tn, 1),
        a_spec=pl.BlockSpec((tm, k), lambda i, j, kk: (i, 0)),
        b_spec=pl.BlockSpec((None, k, tn), lambda i, j, kk: (layer, 0, j)),
        out_shape=jax.ShapeDtypeStruct((t, n), out_dtype),
        out_spec=pl.BlockSpec((tm, tn), lambda i, j, kk: (i, j)),
        acc_shape=(tm, tn), res=res, res_spec=res_spec, scale=scale)


def _mm_down(name, act, w4, layer, res):
    nblk, t, kb = act.shape
    n = w4.shape[3]
    tm = _tile(t, 1024)

    def body(a_ref, b_ref, r_ref, o_ref):
        acc = r_ref[...]
        for u in range(nblk):
            acc = acc + jnp.dot(a_ref[u], b_ref[u], preferred_element_type=F32)
        o_ref[...] = acc

    row = pl.BlockSpec((tm, n), lambda i: (i, 0))
    return _call(
        body, name, (t // tm,),
        [pl.BlockSpec((nblk, tm, kb), lambda i: (0, i, 0)),
         pl.BlockSpec((None, nblk, kb, n), lambda i: (layer, 0, 0, 0)),
         row],
        row,
        jax.ShapeDtypeStruct((t, n), F32),
        vmem_bytes=2 * (_nbytes((nblk, tm, kb), BF16) + _nbytes((nblk, kb, n), BF16)) + 6 * _nbytes((tm, n), F32),
    )(act, w4, res)


def _mm_t_colblock_norm_bwd(name, dz, w_g, layer, x, g, dx_up, blocked_in=False):
    k = w_g.shape[2]
    nb = w_g.shape[3]
    t = x.shape[0]
    tm = _tile(t, 1024)
    per_step = 2
    n_steps = N_DEV // per_step
    if blocked_in:
        a_spec = pl.BlockSpec((per_step, tm, nb), lambda i, kk: (kk, i, 0))
    else:
        a_spec = pl.BlockSpec((tm, per_step * nb), lambda i, kk: (i, kk))

    def body(a_ref, b_ref, x_ref, g_ref, up_ref, dx_ref, dg_ref, acc_ref):
        i = pl.program_id(0)
        kk = pl.program_id(1)
        part = None
        for u in range(per_step):
            a = a_ref[u] if blocked_in else a_ref[:, u * nb:(u + 1) * nb]
            term = lax.dot_general(a.astype(BF16), b_ref[u].astype(BF16), NT, preferred_element_type=F32)
            part = term if part is None else part + term

        @pl.when(kk == 0)
        def _():
            acc_ref[...] = part

        @pl.when(kk > 0)
        def _():
            acc_ref[...] += part

        @pl.when((i == 0) & (kk == 0))
        def _():
            dg_ref[...] = jnp.zeros_like(dg_ref)

        @pl.when(kk == n_steps - 1)
        def _():
            dy = acc_ref[...]
            xf = x_ref[...]
            r = lax.rsqrt(jnp.mean(xf * xf, axis=-1, keepdims=True) + EPS)
            xhat = xf * r
            dxhat = dy * g_ref[...]
            dg_ref[...] += jnp.sum(dy * xhat, axis=0, keepdims=True)
            dx_ref[...] = up_ref[...] + r * (dxhat - xhat * jnp.mean(dxhat * xhat, axis=-1, keepdims=True))

    row = pl.BlockSpec((tm, k), lambda i, kk: (i, 0))
    vec = pl.BlockSpec((1, k), lambda i, kk: (0, 0))
    dx, dg = _call(
        body, name, (t // tm, n_steps),
        [a_spec, pl.BlockSpec((None, per_step, k, nb), lambda i, kk: (layer, kk, 0, 0)), row, vec, row],
        [row, vec],
        [jax.ShapeDtypeStruct((t, k), F32), jax.ShapeDtypeStruct((1, k), F32)],
        scratch=[pltpu.VMEM((tm, k), F32)],
        vmem_bytes=2 * per_step * (_nbytes((tm, nb), BF16) + _nbytes((k, nb), BF16)) + 10 * _nbytes((tm, k), F32),
    )(dz, w_g, x, g.reshape(1, k), dx_up)
    return dx, dg.reshape(k)


def _ffn_gate_up(name, x, g, w_g, layer):
    t, k = x.shape
    nb = w_g.shape[3]
    half = N_DEV // 2
    tm = _tile(t, 1024)

    def body(x_ref, g_ref, wg_ref, wu_ref, h_ref, dact_ref, act_ref, h_scr):
        @pl.when(pl.program_id(1) == 0)
        def _():
            xf = x_ref[...]
            r = lax.rsqrt(jnp.mean(xf * xf, axis=-1, keepdims=True) + EPS)
            hb = (xf * r * g_ref[...]).astype(BF16)
            h_scr[...] = hb
            h_ref[...] = hb

        hb = h_scr[...]
        gate = jnp.dot(hb, wg_ref[...], preferred_element_type=F32)
        up = jnp.dot(hb, wu_ref[...], preferred_element_type=F32)
        sig = _sigmoid(gate)
        silu = gate * sig
        dact_ref[0] = (up * (sig * (1.0 + gate * (1.0 - sig)))).astype(BF16)
        dact_ref[1] = silu.astype(BF16)
        act_ref[...] = (silu * up).astype(BF16)

    return _call(
        body, name, (t // tm, half),
        [pl.BlockSpec((tm, k), lambda i, j: (i, 0)),
         pl.BlockSpec((1, k), lambda i, j: (0, 0)),
         pl.BlockSpec((None, None, k, nb), lambda i, j: (layer, j, 0, 0)),
         pl.BlockSpec((None, None, k, nb), lambda i, j: (layer, half + j, 0, 0))],
        [pl.BlockSpec((tm, k), lambda i, j: (i, 0)),
         pl.BlockSpec((2, None, tm, nb), lambda i, j: (0, j, i, 0)),
         pl.BlockSpec((None, tm, nb), lambda i, j: (j, i, 0))],
        [jax.ShapeDtypeStruct((t, k), BF16), jax.ShapeDtypeStruct((2, half, t, nb), BF16),
         jax.ShapeDtypeStruct((half, t, nb), BF16)],
        scratch=[pltpu.VMEM((tm, k), BF16)],
        vmem_bytes=2 * (_nbytes((tm, k), F32) + _nbytes((tm, k), BF16) + 2 * _nbytes((k, nb), BF16)
                        + 3 * _nbytes((tm, nb), BF16)) + _nbytes((tm, k), BF16) + 8 * _nbytes((tm, nb), F32),
    )(x, g.reshape(1, k), w_g, w_g)


def _ffn_down_dx(name, dy, w4, layer, dact):
    t, n = dy.shape
    nblk, kb = w4.shape[1], w4.shape[2]
    tm = _tile(t, 1024)

    def body(dy_ref, w_ref, dact_ref, dgu_ref):
        da = lax.dot_general(dy_ref[...].astype(BF16), w_ref[...], NT, preferred_element_type=F32)
        dgu_ref[0] = (da * dact_ref[0].astype(F32)).astype(BF16)
        dgu_ref[1] = (da * dact_ref[1].astype(F32)).astype(BF16)

    blk = pl.BlockSpec((2, None, tm, kb), lambda i, j: (0, j, i, 0))
    return _call(
        body, name, (t // tm, nblk),
        [pl.BlockSpec((tm, n), lambda i, j: (i, 0)),
         pl.BlockSpec((None, None, kb, n), lambda i, j: (layer, j, 0, 0)),
         blk],
        blk,
        jax.ShapeDtypeStruct((2, nblk, t, kb), BF16),
        vmem_bytes=2 * (_nbytes((tm, n), F32) + _nbytes((kb, n), BF16) + 4 * _nbytes((tm, kb), BF16))
        + 8 * _nbytes((tm, kb), F32),
    )(dy, w4, dact)


def _mm_t_natural(name, dy, w, layer):
    t, n = dy.shape
    k = w.shape[1]
    tm = _tile(t, 1024)
    tk = _tile(k, 512)
    return _mm(
        name, NT, dy, w, grid=(t // tm, k // tk, 1),
        a_spec=pl.BlockSpec((tm, n), lambda i, j, kk: (i, 0)),
        b_spec=pl.BlockSpec((None, tk, n), lambda i, j, kk: (layer, j, 0)),
        out_shape=jax.ShapeDtypeStruct((t, k), BF16),
        out_spec=pl.BlockSpec((tm, tk), lambda i, j, kk: (i, j)),
        acc_shape=(tm, tk))


def _mm_dw_colblock(name, h, dz, blocked_in=False, transposed=False):
    t, k = h.shape
    nb = dz.shape[2] if blocked_in else dz.shape[1] // N_DEV
    tk = _tile(t, 2048)
    h_spec = pl.BlockSpec((tk, k), lambda i, j, kk: (kk, 0))
    if blocked_in:
        dz_spec = pl.BlockSpec((None, tk, nb), lambda i, j, kk: (j, kk, 0))
    else:
        dz_spec = pl.BlockSpec((tk, nb), lambda i, j, kk: (kk, j))
    rows, cols = (nb, k) if transposed else (k, nb)
    return _mm(
        name, TN, *((dz, h) if transposed else (h, dz)), grid=(1, N_DEV, t // tk),
        a_spec=dz_spec if transposed else h_spec,
        b_spec=h_spec if transposed else dz_spec,
        out_shape=jax.ShapeDtypeStruct((N_DEV, rows, cols), BF16),
        out_spec=pl.BlockSpec((None, rows, cols), lambda i, j, kk: (j, 0, 0)),
        acc_shape=(rows, cols))


def _mm_dw_natural(name, a, dy):
    t, k = a.shape
    n = dy.shape[1]
    tko = _tile(k, 1024)
    tt = _tile(t, 2048)
    out = _mm(
        name, TN, a, dy, grid=(k // tko, 1, t // tt),
        a_spec=pl.BlockSpec((tt, tko), lambda i, j, kk: (kk, i)),
        b_spec=pl.BlockSpec((tt, n), lambda i, j, kk: (kk, 0)),
        out_shape=jax.ShapeDtypeStruct((k, n), BF16),
        out_spec=pl.BlockSpec((tko, n), lambda i, j, kk: (i, 0)),
        acc_shape=(tko, n))
    return out.reshape(N_DEV, k // N_DEV, n)


def _mm_dw_down(name, act, dy):
    nblk, t, kb = act.shape
    n = dy.shape[1]
    tt = _tile(t, 2048)
    out = _mm(
        name, TN, act, dy, grid=(nblk, 1, t // tt),
        a_spec=pl.BlockSpec((None, tt, kb), lambda i, j, kk: (i, kk, 0)),
        b_spec=pl.BlockSpec((tt, n), lambda i, j, kk: (kk, 0)),
        out_shape=jax.ShapeDtypeStruct((nblk, kb, n), BF16),
        out_spec=pl.BlockSpec((None, kb, n), lambda i, j, kk: (i, 0, 0)),
        acc_shape=(kb, n))
    return out.reshape(N_DEV, (nblk * kb) // N_DEV, n)


def _spatial_mask(transposed=False):
    r = lax.broadcasted_iota(jnp.int32, (A_CHUNK, A_CHUNK), 0) // CHUNK
    c = lax.broadcasted_iota(jnp.int32, (A_CHUNK, A_CHUNK), 1) // CHUNK
    return c >= r if transposed else r >= c


def _sgu_tile(t):
    return _tile(t, 2 * A_CHUNK)


def _sgu_fwd(zpre, g_sgu, w_sp, b_full, name):
    t, f2 = zpre.shape
    f = f2 // 2
    gd = f // A_GROUPS
    tm = _sgu_tile(t)

    def body(z_ref, g_ref, w_ref, b_ref, p_ref, zs_ref, dg_ref):
        mask = _spatial_mask()
        wm = [jnp.where(mask, w_ref[g], 0.0).astype(BF16) for g in range(A_GROUPS)]
        for c in range(tm // A_CHUNK):
            rows = pl.ds(c * A_CHUNK, A_CHUNK)
            z, dgelu = _gelu_and_grad(z_ref[rows, :].astype(F32))
            zs_ref[rows, :] = z.astype(BF16)
            dg_ref[rows, :] = dgelu.astype(BF16)
            u = z[:, :f]
            v0 = z[:, f:]
            r = lax.rsqrt(jnp.mean(v0 * v0, axis=-1, keepdims=True) + EPS)
            v1 = (v0 * r * g_ref[...]).astype(BF16)
            for g in range(A_GROUPS):
                cols = slice(g * gd, (g + 1) * gd)
                v2 = jnp.dot(wm[g], v1[:, cols], preferred_element_type=F32) + b_ref[:, cols]
                p_ref[rows, cols] = (u[:, cols] * v2).astype(BF16)

    return _call(
        body, name, (t // tm,),
        [pl.BlockSpec((tm, f2), lambda i: (i, 0)),
         pl.BlockSpec((1, f), lambda i: (0, 0)),
         pl.BlockSpec((A_GROUPS, A_CHUNK, A_CHUNK), lambda i: (0, 0, 0)),
         pl.BlockSpec((A_CHUNK, f), lambda i: (0, 0))],
        [pl.BlockSpec((tm, f), lambda i: (i, 0)), pl.BlockSpec((tm, f2), lambda i: (i, 0)),
         pl.BlockSpec((tm, f2), lambda i: (i, 0))],
        [jax.ShapeDtypeStruct((t, f), BF16), jax.ShapeDtypeStruct((t, f2), BF16), jax.ShapeDtypeStruct((t, f2), BF16)],
        vmem_bytes=6 * _nbytes((tm, f2), BF16) + 2 * _nbytes((tm, f), BF16) + 8 * _nbytes((A_CHUNK, f2), F32),
    )(zpre, g_sgu.reshape(1, f), w_sp, b_full)


def _sgu_bwd(zs, dgs, dp, g_sgu, w_sp, w_sp_t, b_full, name):
    t, f2 = zs.shape
    f = f2 // 2
    gd = f // A_GROUPS
    tm = _sgu_tile(t)
    n_steps = t // tm

    def body(z_ref, dgelu_ref, dp_ref, g_ref, w_ref, wt_ref, b_ref, dz_ref, dw_ref, db_ref, dg_ref, dv1_ref, dbf_ref):
        step = pl.program_id(0)

        @pl.when(step == 0)
        def _():
            dw_ref[...] = jnp.zeros_like(dw_ref)
            dg_ref[...] = jnp.zeros_like(dg_ref)
            dbf_ref[...] = jnp.zeros_like(dbf_ref)

        mask = _spatial_mask()
        mask_t = _spatial_mask(transposed=True)
        wm = [jnp.where(mask, w_ref[g], 0.0).astype(BF16) for g in range(A_GROUPS)]
        wmt = [jnp.where(mask_t, wt_ref[g], 0.0).astype(BF16) for g in range(A_GROUPS)]
        gain = g_ref[...]
        for c in range(tm // A_CHUNK):
            rows = pl.ds(c * A_CHUNK, A_CHUNK)
            z = z_ref[rows, :].astype(F32)
            dgelu = dgelu_ref[rows, :].astype(F32)
            u = z[:, :f]
            v0 = z[:, f:]
            r = lax.rsqrt(jnp.mean(v0 * v0, axis=-1, keepdims=True) + EPS)
            xhat = v0 * r
            v1 = (xhat * gain).astype(BF16)
            dpf = dp_ref[rows, :].astype(F32)
            for g in range(A_GROUPS):
                cols = slice(g * gd, (g + 1) * gd)
                v1g = v1[:, cols]
                v2 = jnp.dot(wm[g], v1g, preferred_element_type=F32) + b_ref[:, cols]
                dpg = dpf[:, cols]
                dz_ref[rows, cols] = (dpg * v2 * dgelu[:, cols]).astype(BF16)
                dv2 = dpg * u[:, cols]
                dbf_ref[:, cols] += dv2
                dv2b = dv2.astype(BF16)
                dwg = lax.dot_general(dv2b, v1g, NT, preferred_element_type=F32)
                dw_ref[g] += jnp.where(mask, dwg, 0.0)
                dv1_ref[:, cols] = jnp.dot(wmt[g], dv2b, preferred_element_type=F32)
            dv1 = dv1_ref[...]
            dxhat = dv1 * gain
            dg_ref[...] += jnp.sum(dv1 * xhat, axis=0, keepdims=True)
            dv0 = r * (dxhat - xhat * jnp.mean(dxhat * xhat, axis=-1, keepdims=True))
            dz_ref[rows, pl.ds(f, f)] = (dv0 * dgelu[:, f:]).astype(BF16)

        @pl.when(step == n_steps - 1)
        def _():
            for g in range(A_GROUPS):
                db_ref[g] = jnp.sum(dbf_ref[:, g * gd:(g + 1) * gd], axis=1, keepdims=True)

    wspec = pl.BlockSpec((A_GROUPS, A_CHUNK, A_CHUNK), lambda i: (0, 0, 0))
    dz, dw, db, dg = _call(
        body, name, (n_steps,),
        [pl.BlockSpec((tm, f2), lambda i: (i, 0)),
         pl.BlockSpec((tm, f2), lambda i: (i, 0)),
         pl.BlockSpec((tm, f), lambda i: (i, 0)),
         pl.BlockSpec((1, f), lambda i: (0, 0)),
         wspec, wspec,
         pl.BlockSpec((A_CHUNK, f), lambda i: (0, 0))],
        [pl.BlockSpec((tm, f2), lambda i: (i, 0)),
         wspec,
         pl.BlockSpec((A_GROUPS, A_CHUNK, 1), lambda i: (0, 0, 0)),
         pl.BlockSpec((1, f), lambda i: (0, 0))],
        [jax.ShapeDtypeStruct((t, f2), BF16),
         jax.ShapeDtypeStruct((A_GROUPS, A_CHUNK, A_CHUNK), F32),
         jax.ShapeDtypeStruct((A_GROUPS, A_CHUNK, 1), F32),
         jax.ShapeDtypeStruct((1, f), F32)],
        scratch=[pltpu.VMEM((A_CHUNK, f), F32), pltpu.VMEM((A_CHUNK, f), F32)],
        vmem_bytes=6 * _nbytes((tm, f2), BF16) + 2 * _nbytes((tm, f), BF16) + 12 * _nbytes((A_CHUNK, f2), F32),
    )(zs, dgs, dp, g_sgu.reshape(1, f), w_sp, w_sp_t, b_full)
    return dz, dw, db.reshape(A_GROUPS, A_CHUNK), dg.reshape(f)


def _pair_valid(qi, col):
    qc = qi // CHUNK
    kc = col // CHUNK
    return (kc >= qc) & (kc <= qc + N_LEFT_CHUNKS)


def _diagonal_onehot():
    e = lax.broadcasted_iota(jnp.int32, (REL_PAD, DIAGONALS), 1)
    idx = jnp.clip(PAIR_BAND - 1 - e, -MAX_REL, MAX_REL) + MAX_REL
    r = lax.broadcasted_iota(jnp.int32, (REL_PAD, DIAGONALS), 0)
    return jnp.where(r == idx, 1.0, 0.0).astype(BF16)


def _bias_build(table, name):
    h = table.shape[0]
    tab = jnp.pad(table, ((0, 0), (0, REL_PAD - N_REL)))

    def body(t_ref, o_ref):
        oh = _diagonal_onehot()
        diag = jnp.zeros((h, DIAGONALS), F32)
        for piece in _split3(t_ref[...]):
            diag += jnp.dot(piece, oh, preferred_element_type=F32)
        col = lax.broadcasted_iota(jnp.int32, (h, PAIR_BAND), 1)
        for qi in range(PAIR_ROWS):
            row = pltpu.roll(diag, (qi - (PAIR_ROWS - 1)) % DIAGONALS, 1)[:, :PAIR_BAND]
            o_ref[qi] = jnp.where(_pair_valid(qi, col), row, NEG_INF)

    out = _call(
        body, name, (1,),
        [pl.BlockSpec((h, REL_PAD), lambda i: (0, 0))],
        pl.BlockSpec((PAIR_ROWS, h, PAIR_BAND), lambda i: (0, 0, 0)),
        jax.ShapeDtypeStruct((PAIR_ROWS, h, PAIR_BAND), F32),
        vmem_bytes=4 * _nbytes((PAIR_ROWS, h, PAIR_BAND), F32),
    )(tab)
    return jnp.transpose(out, (1, 0, 2))


def _bias_block(pair_bias):
    rest = K_BLOCK - PAIR_BAND
    return jnp.concatenate(
        [jnp.pad(pair_bias, ((0, 0), (0, 0), (p * PAIR_ROWS, rest - p * PAIR_ROWS)), constant_values=NEG_INF)
         for p in range(PAIRS_PER_BLOCK)], axis=1)


def _bias_grad(dbias, name):
    h = dbias.shape[0]
    db_t = jnp.transpose(dbias, (1, 0, 2))

    def body(d_ref, o_ref):
        diag = jnp.zeros((h, DIAGONALS), F32)
        for qi in range(PAIR_ROWS):
            diag += pltpu.roll(d_ref[qi], PAIR_ROWS - 1 - qi, 1)
        oh = _diagonal_onehot()
        acc = jnp.zeros((h, REL_PAD), F32)
        for piece in _split3(diag):
            acc += lax.dot_general(piece, oh, NT, preferred_element_type=F32)
        o_ref[...] = acc

    out = _call(
        body, name, (1,),
        [pl.BlockSpec((PAIR_ROWS, h, DIAGONALS), lambda i: (0, 0, 0))],
        pl.BlockSpec((h, REL_PAD), lambda i: (0, 0)),
        jax.ShapeDtypeStruct((h, REL_PAD), F32),
        vmem_bytes=4 * _nbytes((PAIR_ROWS, h, DIAGONALS), F32),
    )(db_t)
    return out[:, :N_REL]


def _head_masks():
    lane = lax.broadcasted_iota(jnp.int32, (Q_BLOCK, HEAD_PAIR), 1)
    return lane < HEAD_DIM, lane >= HEAD_DIM


def _block_scores(qm, kb, bias, valid):
    s = lax.dot_general(qm, kb, NT, preferred_element_type=F32) + bias
    return s if valid is None else jnp.where(valid, s, NEG_INF)


def _softmax_rows(s):
    e = jnp.exp(s - jnp.max(s, axis=-1, keepdims=True))
    return e * (1.0 / jnp.sum(e, axis=-1, keepdims=True))


def _block_probs(qm, kb, bias, valid):
    return _softmax_rows(_block_scores(qm, kb, bias, valid))


def _padded_then_plain(step, n_blocks):
    n_padded = min(LEFT // Q_BLOCK, n_blocks)
    lax.fori_loop(0, n_padded, lambda j, c: step(j, c, True), 0)
    lax.fori_loop(n_padded, n_blocks, lambda j, c: step(j, c, False), 0, unroll=ATTN_UNROLL)


def _attn_fwd(q, kvpad, bias, name):
    t, d = q.shape
    n_pairs = d // HEAD_PAIR
    n_blocks = t // Q_BLOCK

    def body(q_ref, k_ref, v_ref, b_ref, o_ref):
        masks = _head_masks()
        key = lax.broadcasted_iota(jnp.int32, (Q_BLOCK, K_BLOCK), 1)

        def step(j, carry, padded):
            r0 = pl.multiple_of(j * Q_BLOCK, Q_BLOCK)
            q2 = q_ref[pl.ds(r0, Q_BLOCK), :].astype(F32)
            kb = k_ref[pl.ds(r0, K_BLOCK), :]
            vb = v_ref[pl.ds(r0, K_BLOCK), :]
            valid = key >= LEFT - j * Q_BLOCK if padded else None
            scores = [_block_scores(jnp.where(masks[a], q2, 0.0).astype(BF16), kb, b_ref[a], valid) for a in range(2)]
            probs = [_softmax_rows(s).astype(BF16) for s in scores]
            outs = [jnp.dot(p, vb, preferred_element_type=F32) for p in probs]
            o_ref[pl.ds(r0, Q_BLOCK), :] = jnp.where(masks[0], outs[0], outs[1]).astype(BF16)
            return carry

        _padded_then_plain(step, n_blocks)

    return _call(
        body, name, (n_pairs,),
        [pl.BlockSpec((t, HEAD_PAIR), lambda p: (0, p)),
         pl.BlockSpec((LEFT + t, HEAD_PAIR), lambda p: (0, p)),
         pl.BlockSpec((LEFT + t, HEAD_PAIR), lambda p: (0, n_pairs + p)),
         pl.BlockSpec((2, Q_BLOCK, K_BLOCK), lambda p: (p, 0, 0))],
        pl.BlockSpec((t, HEAD_PAIR), lambda p: (0, p)),
        jax.ShapeDtypeStruct((t, d), BF16),
        vmem_bytes=8 * _nbytes((LEFT + t, HEAD_PAIR), BF16) + 12 * _nbytes((2, Q_BLOCK, K_BLOCK), F32),
    )(q, kvpad, kvpad, bias)


def _attn_bwd(q, kvpad, bias, do, dk_in, dv_in, name):
    t, d = q.shape
    n_pairs = d // HEAD_PAIR
    n_blocks = t // Q_BLOCK
    has_in = dk_in is not None

    def body(*refs):
        refs = list(refs)
        q_ref, k_ref, v_ref, b_ref, do_ref = refs[:5]
        refs = refs[5:]
        if has_in:
            dki_ref, dvi_ref = refs[:2]
            refs = refs[2:]
        dq_ref, dk_ref, dv_ref, db_ref = refs
        masks = _head_masks()
        key = lax.broadcasted_iota(jnp.int32, (Q_BLOCK, K_BLOCK), 1)
        if has_in:
            dk_ref[...] = dki_ref[...]
            dv_ref[...] = dvi_ref[...]
        else:
            dk_ref[...] = jnp.zeros_like(dk_ref)
            dv_ref[...] = jnp.zeros_like(dv_ref)
        db_ref[...] = jnp.zeros_like(db_ref)

        def step(j, carry, padded):
            r0 = pl.multiple_of(j * Q_BLOCK, Q_BLOCK)
            q2 = q_ref[pl.ds(r0, Q_BLOCK), :].astype(F32)
            do2 = do_ref[pl.ds(r0, Q_BLOCK), :].astype(F32)
            kb = k_ref[pl.ds(r0, K_BLOCK), :]
            vb = v_ref[pl.ds(r0, K_BLOCK), :]
            valid = key >= LEFT - j * Q_BLOCK if padded else None
            heads = range(2)
            qms = [jnp.where(masks[a], q2, 0.0).astype(BF16) for a in heads]
            doms = [jnp.where(masks[a], do2, 0.0).astype(BF16) for a in heads]
            scores = [_block_scores(qms[a], kb, b_ref[a], valid) for a in heads]
            dps = [lax.dot_general(doms[a], vb, NT, preferred_element_type=F32) for a in heads]
            ps = [_softmax_rows(s) for s in scores]
            dss = [ps[a] * (dps[a] - jnp.sum(dps[a] * ps[a], axis=-1, keepdims=True)) for a in heads]
            for a in heads:
                for pair in range(PAIRS_PER_BLOCK):
                    lo = pair * PAIR_ROWS
                    db_ref[a, :, pl.ds(0, PAIR_BAND)] += dss[a][lo:lo + PAIR_ROWS, lo:lo + PAIR_BAND]
            dsbs = [ds.astype(BF16) for ds in dss]
            pbs = [p.astype(BF16) for p in ps]
            dqs = [jnp.dot(dsbs[a], kb, preferred_element_type=F32) for a in heads]
            dk_acc = sum(lax.dot_general(dsbs[a], qms[a], TN, preferred_element_type=F32) for a in heads)
            dv_acc = sum(lax.dot_general(pbs[a], doms[a], TN, preferred_element_type=F32) for a in heads)
            dq = jnp.where(masks[0], dqs[0], dqs[1]) * ATTN_SCALE
            dq_ref[pl.ds(r0, Q_BLOCK), :] = dq.astype(BF16)
            dk_ref[pl.ds(r0, K_BLOCK), :] += dk_acc
            dv_ref[pl.ds(r0, K_BLOCK), :] += dv_acc
            return carry

        _padded_then_plain(step, n_blocks)

    q_spec = pl.BlockSpec((t, HEAD_PAIR), lambda p: (0, p))
    kv_spec = pl.BlockSpec((LEFT + t, HEAD_PAIR), lambda p: (0, p))
    operands = [q, kvpad, kvpad, bias, do]
    in_specs = [q_spec, kv_spec, pl.BlockSpec((LEFT + t, HEAD_PAIR), lambda p: (0, n_pairs + p)),
                pl.BlockSpec((2, Q_BLOCK, K_BLOCK), lambda p: (p, 0, 0)), q_spec]
    aliases = None
    if has_in:
        operands += [dk_in, dv_in]
        in_specs += [kv_spec, kv_spec]
        aliases = {5: 1, 6: 2}
    return _call(
        body, name, (n_pairs,),
        in_specs,
        [q_spec, kv_spec, kv_spec, pl.BlockSpec((2, PAIR_ROWS, DIAGONALS), lambda p: (p, 0, 0))],
        [jax.ShapeDtypeStruct((t, d), BF16),
         jax.ShapeDtypeStruct((LEFT + t, d), F32),
         jax.ShapeDtypeStruct((LEFT + t, d), F32),
         jax.ShapeDtypeStruct((d // HEAD_DIM, PAIR_ROWS, DIAGONALS), F32)],
        vmem_bytes=10 * _nbytes((LEFT + t, HEAD_PAIR), BF16) + 8 * _nbytes((LEFT + t, HEAD_PAIR), F32)
        + 16 * _nbytes((2, Q_BLOCK, K_BLOCK), F32),
        aliases=aliases,
    )(*operands)


def _loss_head(x, g, target, name):
    t, d = x.shape
    tm = _tile(t, 512)

    def body(x_ref, g_ref, t_ref, dx_ref, loss_ref, dg_ref):
        @pl.when(pl.program_id(0) == 0)
        def _():
            loss_ref[...] = jnp.zeros_like(loss_ref)
            dg_ref[...] = jnp.zeros_like(dg_ref)

        xf = x_ref[...]
        r = lax.rsqrt(jnp.mean(xf * xf, axis=-1, keepdims=True) + EPS)
        xhat = xf * r
        diff = xhat * g_ref[...] - t_ref[...]
        row_loss = jnp.mean(diff * diff, axis=-1, keepdims=True)
        loss_ref[...] += 0.5 * jnp.sum(row_loss, axis=0, keepdims=True)
        dy = diff * (1.0 / d)
        dg_ref[...] += jnp.sum(dy * xhat, axis=0, keepdims=True)
        dxhat = dy * g_ref[...]
        dx_ref[...] = r * (dxhat - xhat * jnp.mean(dxhat * xhat, axis=-1, keepdims=True))

    row = pl.BlockSpec((tm, d), lambda i: (i, 0))
    vec = pl.BlockSpec((1, d), lambda i: (0, 0))
    dx, loss, dg = _call(
        body, name, (t // tm,),
        [row, vec, row],
        [row, pl.BlockSpec((1, 1), lambda i: (0, 0)), vec],
        [jax.ShapeDtypeStruct((t, d), F32), jax.ShapeDtypeStruct((1, 1), F32), jax.ShapeDtypeStruct((1, d), F32)],
        vmem_bytes=10 * _nbytes((tm, d), F32),
    )(x, g.reshape(1, d), target)
    return dx, loss[0, 0], dg.reshape(d)


def _adamw_store(g, w_ref, m_ref, v_ref, g_ref, d_ref, nm_ref, nv_ref):
    c1 = 1.0 / (1.0 - ADAM_B1 ** ADAM_STEP)
    c2 = 1.0 / (1.0 - ADAM_B2 ** ADAM_STEP)
    nm = ADAM_B1 * m_ref[...] + (1.0 - ADAM_B1) * g
    nv = ADAM_B2 * v_ref[...] + (1.0 - ADAM_B2) * (g * g)
    g_ref[...] = g
    nm_ref[...] = nm
    nv_ref[...] = nv
    d_ref[...] = -ADAM_LR * ((nm * c1) / (jnp.sqrt(nv * c2) + ADAM_EPS) + ADAM_WD * w_ref[...])


def _adamw_layer(recv, own, w, m, v, layer, prev, me, name):
    n_src, r, c = recv.shape
    tr = _row_tile(r, max(16, (256 * 1024) // c), 16)
    first = prev is None

    def body(me_ref, recv_ref, own_ref, w_ref, m_ref, v_ref, *rest):
        mine = me_ref[0]
        own_part = own_ref[...].astype(F32)
        g = None
        for s in range(n_src):
            part = jnp.where(mine == s, own_part, recv_ref[s].astype(F32))
            g = part if g is None else g + part
        _adamw_store(g, w_ref, m_ref, v_ref, *rest[-4:])

    blk = pl.BlockSpec((None, tr, c), lambda i, me_ref: (layer, i, 0))
    any_spec = pl.BlockSpec(memory_space=pl.ANY)
    out = jax.ShapeDtypeStruct(w.shape, F32)
    operands = [me, recv, own, w, m, v] + ([] if first else list(prev))
    vmem = 2 * _nbytes((n_src + 1, tr, c), BF16) + 18 * _nbytes((tr, c), F32)
    return pl.pallas_call(
        body,
        name=name,
        grid_spec=pltpu.PrefetchScalarGridSpec(
            num_scalar_prefetch=1,
            grid=(r // tr,),
            in_specs=[pl.BlockSpec((n_src, tr, c), lambda i, me_ref: (0, i, 0)),
                      pl.BlockSpec((None, tr, c), lambda i, me_ref: (me_ref[0], i, 0)),
                      blk, blk, blk] + ([] if first else [any_spec] * 4),
            out_specs=[blk, blk, blk, blk],
        ),
        out_shape=[out, out, out, out],
        input_output_aliases={} if first else {6 + j: j for j in range(4)},
        compiler_params=pltpu.CompilerParams(
            dimension_semantics=("arbitrary",),
            vmem_limit_bytes=int(min(max(VMEM_FLOOR_BYTES, vmem * 5 // 4), VMEM_CEIL_BYTES))),
    )(*operands)


def _adamw(parts, w, m, v, name):
    n_layers, n_src, r, c = parts.shape
    mult = 16 if parts.dtype == BF16 else 8
    tr = _row_tile(r, max(mult, (256 * 1024) // c), mult)

    def body(p_ref, w_ref, m_ref, v_ref, g_ref, d_ref, nm_ref, nv_ref):
        g = p_ref[0].astype(F32)
        for s in range(1, n_src):
            g = g + p_ref[s].astype(F32)
        _adamw_store(g, w_ref, m_ref, v_ref, g_ref, d_ref, nm_ref, nv_ref)

    blk = pl.BlockSpec((None, tr, c), lambda l, i: (l, i, 0))
    out = jax.ShapeDtypeStruct((n_layers, r, c), F32)
    return _call(
        body, name, (n_layers, r // tr),
        [pl.BlockSpec((None, n_src, tr, c), lambda l, i: (l, 0, i, 0)), blk, blk, blk],
        [blk, blk, blk, blk],
        [out, out, out, out],
        vmem_bytes=2 * _nbytes((n_src, tr, c), parts.dtype) + 18 * _nbytes((tr, c), F32),
    )(parts, w, m, v)


def _ordered_sum(parts, name):
    n_src, r, c = parts.shape

    def body(p_ref, o_ref):
        acc = p_ref[0]
        for s in range(1, n_src):
            acc = acc + p_ref[s]
        o_ref[...] = acc

    return _call(
        body, name, (1,),
        [pl.BlockSpec((n_src, r, c), lambda i: (0, 0, 0))],
        pl.BlockSpec((r, c), lambda i: (0, 0)),
        jax.ShapeDtypeStruct((r, c), F32),
        vmem_bytes=4 * _nbytes((n_src, r, c), F32),
    )(parts)


def _position():
    return lax.axis_index("x"), lax.axis_index("y"), lax.axis_index("c")


def _linear(p):
    return 4 * p[0] + 2 * p[1] + p[2]


def _all_gather(shards, name):
    n = len(shards)

    def body(*refs):
        ins, outs = refs[:n], refs[n:2 * n]
        send_sems, recv_sems, local_sems = refs[2 * n:]
        x, y, c = _position()
        me, sibling = (x, y, c), (x, y, 1 - c)
        chips = [(1 - x, y), (x, 1 - y), (1 - x, 1 - y)]

        def slab(t, p):
            return outs[t].at[:, _linear(p)]

        def copy(t, k, block, to, src=None):
            return pltpu.make_async_remote_copy(
                src_ref=slab(t, block) if src is None else src,
                dst_ref=slab(t, block),
                send_sem=send_sems.at[t, k],
                recv_sem=recv_sems.at[t, k],
                device_id=to,
                device_id_type=MESH,
            )

        started = []
        for t in range(n):
            mine = pltpu.make_async_copy(ins[t], slab(t, me), local_sems.at[t])
            mine.start()
            started.append(mine)
        sends = []
        for t in range(n):
            first = [copy(t, 0, me, sibling, src=ins[t])]
            first += [copy(t, 1 + j, me, (*chip, c), src=ins[t]) for j, chip in enumerate(chips)]
            for cp in first:
                cp.start()
            sends += first
        for t in range(n):
            for j, chip in enumerate(chips):
                copy(t, 1 + j, (*chip, c), me).wait_recv()
                passed = copy(t, 4 + j, (*chip, c), sibling)
                passed.start()
                sends.append(passed)
        for t in range(n):
            copy(t, 0, sibling, me).wait_recv()
            for j, chip in enumerate(chips):
                copy(t, 4 + j, (*chip, 1 - c), me).wait_recv()
        for cp in sends:
            cp.wait_send()
        for mine in started:
            mine.wait()

    out_shape = [jax.ShapeDtypeStruct((s.shape[0], N_DEV) + s.shape[1:], s.dtype) for s in shards]
    return pl.pallas_call(
        body,
        name=name,
        in_specs=[HBM_SPEC] * n,
        out_specs=[HBM_SPEC] * n,
        out_shape=out_shape,
        scratch_shapes=[
            pltpu.SemaphoreType.DMA((n, N_DEV - 1)),
            pltpu.SemaphoreType.DMA((n, N_DEV - 1)),
            pltpu.SemaphoreType.DMA((n,)),
        ],
    )(*shards)


def _exchange(blocks, name):
    n = len(blocks)

    def body(*refs):
        ins, outs = refs[:n], refs[n:2 * n]
        send_sems, recv_sems, local_sems = refs[2 * n:]
        x, y, c = _position()
        me = _linear((x, y, c))
        flips = [(fx, fy, fc) for fx in (0, 1) for fy in (0, 1) for fc in (0, 1)][1:]

        def peer_of(flip):
            fx, fy, fc = flip
            return (1 - x if fx else x, 1 - y if fy else y, 1 - c if fc else c)

        def copy(t, k, peer):
            return pltpu.make_async_remote_copy(
                src_ref=ins[t].at[:, _linear(peer)],
                dst_ref=outs[t].at[:, me],
                send_sem=send_sems.at[t, k],
                recv_sem=recv_sems.at[t, k],
                device_id=peer,
                device_id_type=MESH,
            )

        def arrival(t, k, peer):
            return pltpu.make_async_remote_copy(
                src_ref=ins[t].at[:, _linear(peer)],
                dst_ref=outs[t].at[:, _linear(peer)],
                send_sem=send_sems.at[t, k],
                recv_sem=recv_sems.at[t, k],
                device_id=peer,
                device_id_type=MESH,
            )

        own = []
        for t in range(n):
            cp = pltpu.make_async_copy(ins[t].at[:, me], outs[t].at[:, me], local_sems.at[t])
            cp.start()
            own.append(cp)
        sends = []
        for t in range(n):
            for k, flip in enumerate(flips):
                cp = copy(t, k, peer_of(flip))
                cp.start()
                sends.append(cp)
        for t in range(n):
            for k, flip in enumerate(flips):
                arrival(t, k, peer_of(flip)).wait_recv()
        for cp in sends:
            cp.wait_send()
        for cp in own:
            cp.wait()

    out_shape = [jax.ShapeDtypeStruct(b.shape, b.dtype) for b in blocks]
    return pl.pallas_call(
        body,
        name=name,
        in_specs=[HBM_SPEC] * n,
        out_specs=[HBM_SPEC] * n,
        out_shape=out_shape,
        scratch_shapes=[
            pltpu.SemaphoreType.DMA((n, N_DEV - 1)),
            pltpu.SemaphoreType.DMA((n, N_DEV - 1)),
            pltpu.SemaphoreType.DMA((n,)),
        ],
    )(*blocks)


def _peers():
    x, y, c = _position()
    flips = [(fx, fy, fc) for fx in (0, 1) for fy in (0, 1) for fc in (0, 1)][1:]
    return [(1 - x if fx else x, 1 - y if fy else y, 1 - c if fc else c) for fx, fy, fc in flips]


def _split_start(srcs, lands, carry, name, exchange=False):
    n = len(srcs)

    def body(*refs):
        src_refs, land_refs = refs[:n], refs[n:2 * n]
        send_sems, recv_sems = refs[2 * n + 1], refs[2 * n + 2]
        me = _linear(_position())
        for t in range(n):
            for k, peer in enumerate(_peers()):
                pltpu.make_async_remote_copy(
                    src_ref=src_refs[t].at[_linear(peer)] if exchange else src_refs[t],
                    dst_ref=land_refs[t].at[me],
                    send_sem=send_sems.at[t * (N_DEV - 1) + k],
                    recv_sem=recv_sems.at[t * (N_DEV - 1) + k],
                    device_id=peer,
                    device_id_type=MESH,
                ).start()

    operands = list(srcs) + list(lands) + [carry]
    sems = pltpu.SemaphoreType.DMA((n * (N_DEV - 1),))
    out = pl.pallas_call(
        body,
        name=name,
        in_specs=[HBM_SPEC] * len(operands),
        out_specs=[SEM_SPEC, SEM_SPEC] + [HBM_SPEC] * len(operands),
        out_shape=[sems, sems] + [pltpu.HBM(a.shape, a.dtype) for a in operands],
        input_output_aliases={i: 2 + i for i in range(len(operands))},
        compiler_params=pltpu.CompilerParams(has_side_effects=pltpu.SideEffectType.DATAFLOW_SIDE_EFFECTING),
    )(*[pltpu.with_memory_space_constraint(a, pltpu.HBM) for a in operands])
    return out[0], out[1], out[2:2 + n], out[2 + n:2 + 2 * n], out[2 + 2 * n]


def _split_wait(send_sems, recv_sems, srcs, lands, after, name, exchange=False):
    n = len(srcs)

    def body(*refs):
        src_refs, land_refs = refs[:n], refs[n:2 * n]
        send_ref, recv_ref = refs[2 * n], refs[2 * n + 1]
        for t in range(n):
            for k, peer in enumerate(_peers()):
                copy = pltpu.make_async_remote_copy(
                    src_ref=src_refs[t].at[0] if exchange else src_refs[t],
                    dst_ref=land_refs[t].at[0],
                    send_sem=send_ref.at[t * (N_DEV - 1) + k],
                    recv_sem=recv_ref.at[t * (N_DEV - 1) + k],
                    device_id=peer,
                    device_id_type=MESH,
                )
                copy.wait_send()
                copy.wait_recv()

    arrays = list(srcs) + list(lands)
    out = pl.pallas_call(
        body,
        name=name,
        in_specs=[HBM_SPEC] * len(arrays) + [SEM_SPEC, SEM_SPEC, pl.BlockSpec(memory_space=pl.ANY)],
        out_specs=[HBM_SPEC] * len(arrays),
        out_shape=[pltpu.HBM(a.shape, a.dtype) for a in arrays],
        input_output_aliases={i: i for i in range(len(arrays))},
        compiler_params=pltpu.CompilerParams(has_side_effects=pltpu.SideEffectType.DATAFLOW_SIDE_EFFECTING),
    )(*arrays, send_sems, recv_sems, after)
    return out[:n], out[n:]


def _pack(arrays, row_multiple):
    flat = jnp.concatenate([a.reshape(-1) for a in arrays])
    quantum = row_multiple * FLAT_LANES
    padded = -(-flat.shape[0] // quantum) * quantum
    return jnp.pad(flat, (0, padded - flat.shape[0])).reshape(-1, FLAT_LANES)


def _unpack(flat, like):
    flat = flat.reshape(-1)
    out, at = [], 0
    for a in like:
        size = math.prod(a.shape)
        out.append(flat[at:at + size].reshape(a.shape))
        at += size
    return out


def kernel(x, a_norm, a_w_in, a_sgu_norm, a_w_spatial, a_b_spatial, a_w_out, kv_norm, w_kv, b_norm, b_w_q, b_rel_bias, b_w_o, ffn_norm, ffn_w_gate_up, ffn_w_down, final_norm, loss_target, m_a_norm, m_a_w_in, m_a_sgu_norm, m_a_w_spatial, m_a_b_spatial, m_a_w_out, m_kv_norm, m_w_kv, m_b_norm, m_b_w_q, m_b_rel_bias, m_b_w_o, m_ffn_norm, m_ffn_w_gate_up, m_ffn_w_down, m_final_norm, v_a_norm, v_a_w_in, v_a_sgu_norm, v_a_w_spatial, v_a_b_spatial, v_a_w_out, v_kv_norm, v_w_kv, v_b_norm, v_b_w_q, v_b_rel_bias, v_b_w_o, v_ffn_norm, v_ffn_w_gate_up, v_ffn_w_down, v_final_norm):
    xs = x[0]
    target = loss_target[0]
    t, d = xs.shape
    n_a = a_w_in.shape[0]
    n_b = b_w_q.shape[0]
    depth = ffn_w_gate_up.shape[0]
    f_a = a_w_out.shape[1] * N_DEV
    gd = f_a // A_GROUPS
    nb_ffn = ffn_w_gate_up.shape[2]
    me = _linear(_position())

    small_rows = -(-(a_norm.size + a_sgu_norm.size) // (8 * 128)) * 8
    small = jnp.pad(jnp.concatenate([a_norm.reshape(-1), a_sgu_norm.reshape(-1)]),
                    (0, small_rows * 128 - a_norm.size - a_sgu_norm.size)).reshape(1, small_rows, 128)

    def shard(w, layer=None):
        return (w if layer is None else w[layer]).astype(BF16)

    stages = []
    for layer in range(depth):
        if layer == 0:
            stages += [("a0", [shard(a_w_in, 0)]), ("a0_out", [shard(a_w_out, 0)])]
        elif layer < n_a:
            stages.append((f"a{layer}", [shard(a_w_in, layer), shard(a_w_out, layer)]))
        else:
            i = layer - n_a
            shared = [shard(w_kv)] if i == 0 else []
            stages.append((f"b{i}", shared + [shard(b_w_q, i), shard(b_w_o, i)]))
        stages.append((f"f{layer}", [shard(ffn_w_gate_up, layer), shard(ffn_w_down, layer)]))
    first = _all_gather([s[None] for s in stages[0][1]] + [small], "gather_first")
    gathered = {stages[0][0]: [g[0] for g in first[:-1]]}
    small_g = first[-1].reshape(N_DEV, -1)
    a_norm_full = small_g[:, :a_norm.size].reshape(N_DEV, n_a, -1).transpose(1, 0, 2).reshape(n_a, d)
    a_sgu_full = small_g[:, a_norm.size:a_norm.size + a_sgu_norm.size].reshape(
        N_DEV, n_a, -1).transpose(1, 0, 2).reshape(n_a, f_a)
    in_flight = {}
    for key, shards in stages[1:]:
        lands = [lax.dynamic_update_slice(lax.empty((N_DEV,) + s.shape, BF16), s[None], (me, 0, 0)) for s in shards]
        send, recv, srcs, lands, a_norm_full = _split_start(shards, lands, a_norm_full, f"gather_start_{key}")
        in_flight[key] = (send, recv, srcs, lands)

    def weights(key, after):
        if key not in gathered:
            _, gathered[key] = _split_wait(*in_flight.pop(key), after, f"gather_wait_{key}")
        return gathered[key]

    rows_down = ffn_w_down.shape[1]

    def mixer_a_weights(i, after):
        if i == 0:
            (w_in,), (w_out,) = weights("a0", after[0]), weights("a0_out", after[1])
        else:
            w_in, w_out = weights(f"a{i}", after[0])
        return w_in[None], w_out.reshape(1, f_a, d)

    def mixer_b_weights(i, after):
        ws = weights(f"b{i}", after)
        return ws[-2].reshape(1, d, d), ws[-1].reshape(1, d, d)

    def ffn_weights(layer, after):
        w_gu, w_dn = weights(f"f{layer}", after)
        return w_gu[None], w_dn.reshape(1, N_DEV // 2, 2 * rows_down, d)

    w_sp_t = jnp.swapaxes(a_w_spatial, -1, -2)
    b_full = jnp.repeat(jnp.swapaxes(a_b_spatial, -1, -2), gd, axis=-1)

    saved = []

    def ffn_fwd(xin, layer):
        w_gu, w_dn = ffn_weights(layer, xin)
        hf, dact, act = _ffn_gate_up(f"ffn_gate_up_{layer}", xin, ffn_norm[layer], w_gu, 0)
        xout = _mm_down(f"ffn_down_{layer}", act, w_dn, 0, xin)
        return xout, (xin, hf, dact, act)

    for i in range(n_a):
        h, zpre = _norm_mm_colblock(f"a_in_{i}", xs, a_norm_full[i], weights(f"a{i}", xs)[0][None], 0)
        p, zs, dgs = _sgu_fwd(zpre, a_sgu_full[i], a_w_spatial[i], b_full[i], f"a_sgu_fwd_{i}")
        w_in, w_out = mixer_a_weights(i, (xs, p))
        x_mid = _mm_natural(f"a_out_{i}", p, w_out, 0, res=xs)
        x_out, ffn_saved = ffn_fwd(x_mid, i)
        saved.append((xs, h, zs, dgs, p, ffn_saved))
        xs = x_out

    x_kv = xs
    w_kv_g = weights("b0", x_kv)[0][None]
    h_kv, kv = _norm_mm_colblock("kv_proj", x_kv, kv_norm, w_kv_g, 0)
    kvpad = jnp.pad(kv, ((LEFT, 0), (0, 0)))

    biases = [_bias_block(_bias_build(b_rel_bias[i], f"rel_bias_{i}")) for i in range(n_b)]
    for i in range(n_b):
        layer = n_a + i
        w_q, w_o = mixer_b_weights(i, xs)
        hb = _rms_fwd(xs, b_norm[i], f"b_norm_fwd_{i}")
        q = _mm_natural(f"b_q_{i}", hb, w_q, 0, out_dtype=BF16, scale=ATTN_SCALE)
        o = _attn_fwd(q, kvpad, biases[i], f"b_attn_fwd_{i}")
        x_mid = _mm_natural(f"b_o_{i}", o, w_o, 0, res=xs)
        x_out, ffn_saved = ffn_fwd(x_mid, layer)
        saved.append((xs, hb, q, o, ffn_saved))
        xs = x_out

    dx, loss_local, g_final = _loss_head(xs, final_norm, target, "loss_head")
    loss = lax.psum(loss_local, ("x", "y", "c"))

    big_grads = {}
    pending = []
    in_flight_grads = []

    def start_exchange(dx, tag):
        srcs = [big_grads[key] for key in pending]
        lands = [lax.empty(s.shape, BF16) for s in srcs]
        send, recv, srcs, lands, dx = _split_start(srcs, lands, dx, f"exchange_start_{tag}", exchange=True)
        in_flight_grads.append((list(pending), send, recv, srcs, lands, tag))
        pending.clear()
        return dx

    g_ffn_norm = [None] * depth
    g_a_norm = [None] * n_a
    g_a_sgu = [None] * n_a
    g_w_sp = [None] * n_a
    g_b_sp = [None] * n_a
    g_b_norm = [None] * n_b
    g_rel = [None] * n_b

    def ffn_bwd(dx, layer, ffn_saved):
        eager = layer < n_a
        xin, hf, dact, act = ffn_saved
        big_grads["ffn_w_down", layer] = _mm_dw_down(f"ffn_down_dw_{layer}", act, dx)
        pending.append(("ffn_w_down", layer))
        if eager:
            dx = start_exchange(dx, f"f{layer}_down")
        w_gu, w_dn = ffn_weights(layer, xin)
        dgu = _ffn_down_dx(f"ffn_down_dx_{layer}", dx, w_dn, 0, dact).reshape(N_DEV, t, nb_ffn)
        big_grads["ffn_w_gate_up", layer] = _mm_dw_colblock(
            f"ffn_gate_up_dw_{layer}", hf, dgu, blocked_in=True, transposed=True)
        pending.append(("ffn_w_gate_up", layer))
        if eager:
            dx = start_exchange(dx, f"f{layer}_gate_up")
        dx, g_ffn_norm[layer] = _mm_t_colblock_norm_bwd(
            f"ffn_gate_up_dx_{layer}", dgu, w_gu, 0, xin, ffn_norm[layer], dx, blocked_in=True)
        return dx

    dk = dv = None
    for i in reversed(range(n_b)):
        layer = n_a + i
        x_in, hb, q, o, ffn_saved = saved[layer]
        dx = ffn_bwd(dx, layer, ffn_saved)
        big_grads["b_w_o", i] = _mm_dw_natural(f"b_o_dw_{i}", o, dx)
        w_q, w_o = mixer_b_weights(i, x_in)
        do = _mm_t_natural(f"b_o_dx_{i}", dx, w_o, 0)
        dq, dk, dv, dbias = _attn_bwd(q, kvpad, biases[i], do, dk, dv, f"b_attn_bwd_{i}")
        g_rel[i] = _bias_grad(dbias, f"rel_bias_grad_{i}")
        big_grads["b_w_q", i] = _mm_dw_natural(f"b_q_dw_{i}", hb, dq)
        pending.extend([("b_w_o", i), ("b_w_q", i)])
        dh = _mm_t_natural(f"b_q_dx_{i}", dq, w_q, 0)
        dx, g_b_norm[i] = _rms_bwd(x_in, b_norm[i], dh, dx, f"b_norm_bwd_{i}")
        if i > 0:
            dx = start_exchange(dx, f"b{i}")

    dkv = jnp.concatenate([dk[LEFT:], dv[LEFT:]], axis=1).astype(BF16)
    big_grads["w_kv", 0] = _mm_dw_colblock("kv_proj_dw", h_kv, dkv)
    pending.append(("w_kv", 0))
    dx, g_kv_norm = _mm_t_colblock_norm_bwd("kv_proj_dx", dkv, w_kv_g, 0, x_kv, kv_norm, dx)
    dx = start_exchange(dx, "kv")

    for i in reversed(range(n_a)):
        x_in, h, zs, dgs, p, ffn_saved = saved[i]
        dx = ffn_bwd(dx, i, ffn_saved)
        big_grads["a_w_out", i] = _mm_dw_natural(f"a_out_dw_{i}", p, dx)
        pending.append(("a_w_out", i))
        dx = start_exchange(dx, f"a{i}_out")
        w_in, w_out = mixer_a_weights(i, (x_in, p))
        dp = _mm_t_natural(f"a_out_dx_{i}", dx, w_out, 0)
        dz, g_w_sp[i], g_b_sp[i], g_a_sgu[i] = _sgu_bwd(
            zs, dgs, dp, a_sgu_full[i], a_w_spatial[i], w_sp_t[i], b_full[i], f"a_sgu_bwd_{i}")
        big_grads["a_w_in", i] = _mm_dw_colblock(f"a_in_dw_{i}", h, dz)
        pending.append(("a_w_in", i))
        dx = start_exchange(dx, f"a{i}_in")
        dx, g_a_norm[i] = _mm_t_colblock_norm_bwd(f"a_in_dx_{i}", dz, w_in, 0, x_in, a_norm_full[i], dx)
    grad_x = dx[None]

    small_like = [jax.ShapeDtypeStruct((n_a, d), F32), jax.ShapeDtypeStruct((n_a, f_a), F32),
                  a_w_spatial, a_b_spatial, kv_norm, b_norm, b_rel_bias, ffn_norm, final_norm]
    small_partial = _pack(
        [jnp.stack(g_a_norm), jnp.stack(g_a_sgu), jnp.stack(g_w_sp), jnp.stack(g_b_sp), g_kv_norm,
         jnp.stack(g_b_norm), jnp.stack(g_rel), jnp.stack(g_ffn_norm), g_final], N_DEV * 8)
    chunk_rows = small_partial.shape[0] // N_DEV
    arrived = {}
    for keys, send, recv, srcs, lands, tag in in_flight_grads:
        srcs, lands = _split_wait(send, recv, srcs, lands, dx, f"exchange_wait_{tag}", exchange=True)
        for key, src, land in zip(keys, srcs, lands):
            arrived[key] = (land, src)
    small_got = _exchange([small_partial.reshape(1, N_DEV, chunk_rows, FLAT_LANES)], "exchange_small")[0]
    small_sum = _ordered_sum(small_got[0], "small_grad_sum")
    small_all = _all_gather([small_sum[None]], "gather_small_grads")[0]
    (ga_norm, ga_sgu, gw_sp, gb_sp, gkv_norm, gb_norm, g_relb, gffn_norm, gfinal) = _unpack(small_all, small_like)

    results = {}
    big_names = ["a_w_in", "a_w_out", "w_kv", "b_w_q", "b_w_o", "ffn_w_gate_up", "ffn_w_down"]
    big_wmv = [(a_w_in, m_a_w_in, v_a_w_in), (a_w_out, m_a_w_out, v_a_w_out),
               (w_kv[None], m_w_kv[None], v_w_kv[None]), (b_w_q, m_b_w_q, v_b_w_q), (b_w_o, m_b_w_o, v_b_w_o),
               tuple(jnp.swapaxes(a, 1, 2) for a in (ffn_w_gate_up, m_ffn_w_gate_up, v_ffn_w_gate_up)),
               (ffn_w_down, m_ffn_w_down, v_ffn_w_down)]
    me_arr = jnp.reshape(me, (1,)).astype(jnp.int32)
    for name, (w, m, v) in zip(big_names, big_wmv):
        outs = None
        for layer in range(w.shape[0]):
            got, own = arrived[name, layer]
            outs = _adamw_layer(got, own, w, m, v, layer, outs, me_arr, f"adamw_{name}_{layer}")
        if name == "w_kv":
            outs = [o[0] for o in outs]
        if name == "ffn_w_gate_up":
            outs = [jnp.swapaxes(o, 1, 2) for o in outs]
        results[name] = outs

    n_cols = a_norm.shape[1]
    s_cols = a_sgu_norm.shape[1]
    small_g_list = [lax.dynamic_slice(ga_norm, (0, me * n_cols), (n_a, n_cols)),
                    lax.dynamic_slice(ga_sgu, (0, me * s_cols), (n_a, s_cols)),
                    gw_sp, gb_sp, gkv_norm, gb_norm, g_relb, gffn_norm, gfinal]
    small_names = ["a_norm", "a_sgu_norm", "a_w_spatial", "a_b_spatial", "kv_norm", "b_norm", "b_rel_bias",
                   "ffn_norm", "final_norm"]
    small_w = [a_norm, a_sgu_norm, a_w_spatial, a_b_spatial, kv_norm, b_norm, b_rel_bias, ffn_norm, final_norm]
    small_m = [m_a_norm, m_a_sgu_norm, m_a_w_spatial, m_a_b_spatial, m_kv_norm, m_b_norm, m_b_rel_bias,
               m_ffn_norm, m_final_norm]
    small_v = [v_a_norm, v_a_sgu_norm, v_a_w_spatial, v_a_b_spatial, v_kv_norm, v_b_norm, v_b_rel_bias,
               v_ffn_norm, v_final_norm]
    flat_g = _pack(small_g_list, 8)
    flat_out = _adamw(flat_g[None, None], _pack(small_w, 8)[None], _pack(small_m, 8)[None],
                      _pack(small_v, 8)[None], "adamw_small")
    unpacked = [_unpack(o[0], small_w) for o in flat_out]
    for idx, name in enumerate(small_names):
        results[name] = [unpacked[kind][idx] for kind in range(4)]

    order = ["a_norm", "a_w_in", "a_sgu_norm", "a_w_spatial", "a_b_spatial", "a_w_out", "kv_norm", "w_kv",
             "b_norm", "b_w_q", "b_rel_bias", "b_w_o", "ffn_norm", "ffn_w_gate_up", "ffn_w_down", "final_norm"]
    outputs = [loss, grad_x]
    for kind in range(4):
        outputs += [results[name][kind] for name in order]
    return tuple(outputs)
```

```python
import math

import jax
import jax.numpy as jnp
from jax import lax
from jax.experimental import pallas as pl
from jax.experimental.pallas import tpu as pltpu

F32 = jnp.float32
BF16 = jnp.bfloat16
MESH = pl.DeviceIdType.MESH
HBM_SPEC = pl.BlockSpec(memory_space=pltpu.HBM)
SEM_SPEC = pl.BlockSpec(memory_space=pltpu.SEMAPHORE)

N_DEV = 8
CHUNK = 64
A_CHUNK = 128
A_GROUPS = 8
N_LEFT_CHUNKS = 8
LEFT = N_LEFT_CHUNKS * CHUNK
PAIR_ROWS = 2 * CHUNK
PAIR_BAND = PAIR_ROWS + LEFT
DIAGONALS = PAIR_BAND + PAIR_ROWS
PAIRS_PER_BLOCK = 2
Q_BLOCK = PAIRS_PER_BLOCK * PAIR_ROWS
K_BLOCK = Q_BLOCK + LEFT
ATTN_UNROLL = 2
MAX_REL = 256
N_REL = 2 * MAX_REL + 1
REL_PAD = 640
HEAD_DIM = 64
HEAD_PAIR = 2 * HEAD_DIM
ATTN_SCALE = HEAD_DIM ** -0.5
EPS = 1e-6
NEG_INF = -1e30
ADAM_LR = 0.001
ADAM_B1 = 0.9
ADAM_B2 = 0.999
ADAM_EPS = 1e-08
ADAM_WD = 0.01
ADAM_STEP = 10
FLAT_LANES = 1024
V7X_VMEM_BYTES = 64 * 1024 * 1024
VMEM_FLOOR_BYTES = 32 * 1024 * 1024
VMEM_CEIL_BYTES = V7X_VMEM_BYTES - 8 * 1024 * 1024

NN = (((1,), (0,)), ((), ()))
NT = (((1,), (1,)), ((), ()))
TN = (((0,), (0,)), ((), ()))


def _tile(n, pref):
    return pref if n % pref == 0 else n


def _row_tile(n, pref, mult):
    best = None
    for t in range(mult, min(n, pref) + 1, mult):
        if n % t == 0:
            best = t
    return best if best is not None else n


def _nbytes(shape, dtype):
    n = 1
    for s in shape:
        if s is not None:
            n *= s
    return n * jnp.dtype(dtype).itemsize


def _call(body, name, grid, in_specs, out_specs, out_shape, scratch=(), vmem_bytes=0, aliases=None):
    limit = int(min(max(VMEM_FLOOR_BYTES, vmem_bytes * 5 // 4), VMEM_CEIL_BYTES))
    return pl.pallas_call(
        body,
        name=name,
        grid=grid,
        in_specs=in_specs,
        out_specs=out_specs,
        out_shape=out_shape,
        scratch_shapes=list(scratch),
        input_output_aliases=aliases or {},
        compiler_params=pltpu.CompilerParams(
            dimension_semantics=("arbitrary",) * len(grid), vmem_limit_bytes=limit),
    )


ERFC_P = 0.3275911 / math.sqrt(2.0)
ERFC_HALF_COEFFS = tuple(0.5 * a for a in (1.061405429, -1.453152027, 1.421413741, -0.284496736, 0.254829592))


def _gelu_and_grad(x):
    d = 1.0 + ERFC_P * jnp.abs(x)
    r = pl.reciprocal(d, approx=True)
    t = r * (2.0 - d * r)
    a5, a4, a3, a2, a1 = ERFC_HALF_COEFFS
    ex = jnp.exp(-0.5 * (x * x))
    tail = ((((a5 * t + a4) * t + a3) * t + a2) * t + a1) * t * ex
    cdf = jnp.where(x < 0, tail, 1.0 - tail)
    return x * cdf, cdf + x * ex * (1.0 / math.sqrt(2.0 * math.pi))


def _sigmoid(x):
    return 1.0 / (1.0 + jnp.exp(-x))


def _split3(x):
    hi = x.astype(BF16)
    r1 = x - hi.astype(F32)
    mid = r1.astype(BF16)
    lo = (r1 - mid.astype(F32)).astype(BF16)
    return hi, mid, lo


def _rms_fwd(x, g, name):
    t, d = x.shape
    tm = _tile(t, 512)

    def body(x_ref, g_ref, o_ref):
        xf = x_ref[...]
        r = lax.rsqrt(jnp.mean(xf * xf, axis=-1, keepdims=True) + EPS)
        o_ref[...] = (xf * r * g_ref[...]).astype(o_ref.dtype)

    return _call(
        body, name, (t // tm,),
        [pl.BlockSpec((tm, d), lambda i: (i, 0)), pl.BlockSpec((1, d), lambda i: (0, 0))],
        pl.BlockSpec((tm, d), lambda i: (i, 0)),
        jax.ShapeDtypeStruct((t, d), BF16),
        vmem_bytes=2 * (_nbytes((tm, d), F32) + _nbytes((tm, d), BF16)) + 4 * _nbytes((tm, d), F32),
    )(x, g.reshape(1, d))


def _rms_bwd(x, g, dh, dx_up, name):
    t, d = x.shape
    tm = _tile(t, 512)

    def body(x_ref, g_ref, dh_ref, up_ref, dx_ref, dg_ref):
        @pl.when(pl.program_id(0) == 0)
        def _():
            dg_ref[...] = jnp.zeros_like(dg_ref)

        xf = x_ref[...]
        r = lax.rsqrt(jnp.mean(xf * xf, axis=-1, keepdims=True) + EPS)
        xhat = xf * r
        dy = dh_ref[...].astype(F32)
        dxhat = dy * g_ref[...]
        dg_ref[...] += jnp.sum(dy * xhat, axis=0, keepdims=True)
        dx = r * (dxhat - xhat * jnp.mean(dxhat * xhat, axis=-1, keepdims=True))
        dx_ref[...] = up_ref[...] + dx

    row = pl.BlockSpec((tm, d), lambda i: (i, 0))
    vec = pl.BlockSpec((1, d), lambda i: (0, 0))
    dx, dg = _call(
        body, name, (t // tm,),
        [row, vec, row, row],
        [row, vec],
        [jax.ShapeDtypeStruct((t, d), F32), jax.ShapeDtypeStruct((1, d), F32)],
        vmem_bytes=10 * _nbytes((tm, d), F32),
    )(x, g.reshape(1, d), dh, dx_up)
    return dx, dg.reshape(d)


def _mm(name, dims, a, b, *, grid, a_spec, b_spec, out_shape, out_spec, acc_shape,
        res=None, res_spec=None, scale=None):
    nk = grid[2]
    has_res = res is not None

    def body(*refs):
        refs = list(refs)
        a_ref = refs.pop(0)
        b_ref = refs.pop(0)
        r_ref = refs.pop(0) if has_res else None
        o_ref = refs.pop(0)
        part = lax.dot_general(a_ref[...].astype(BF16), b_ref[...].astype(BF16), dims,
                               preferred_element_type=F32)

        def finish(acc):
            if scale is not None:
                acc = acc * scale
            if has_res:
                acc = acc + r_ref[...]
            o_ref[...] = acc.astype(o_ref.dtype)

        if nk == 1:
            finish(part)
        else:
            acc_ref = refs.pop(0)
            k = pl.program_id(2)

            @pl.when(k == 0)
            def _():
                acc_ref[...] = part

            @pl.when(k > 0)
            def _():
                acc_ref[...] += part

            @pl.when(k == nk - 1)
            def _():
                finish(acc_ref[...])

    operands = [a, b]
    in_specs = [a_spec, b_spec]
    vmem = 2 * (_nbytes(a_spec.block_shape, a.dtype) + _nbytes(b_spec.block_shape, b.dtype)
                + _nbytes(out_spec.block_shape, out_shape.dtype))
    vmem += 3 * _nbytes(acc_shape, F32)
    if has_res:
        operands.append(res)
        in_specs.append(res_spec)
        vmem += 2 * _nbytes(res_spec.block_shape, res.dtype)
    scratch = [pltpu.VMEM(acc_shape, F32)] if nk > 1 else []
    return _call(body, name, grid, in_specs, out_spec, out_shape, scratch=scratch, vmem_bytes=vmem)(*operands)


def _mm_colblock(name, h, w_g, layer):
    t, k = h.shape
    nb = w_g.shape[3]
    tm = _tile(t, 2048)
    return _mm(
        name, NN, h, w_g, grid=(t // tm, N_DEV, 1),
        a_spec=pl.BlockSpec((tm, k), lambda i, j, kk: (i, 0)),
        b_spec=pl.BlockSpec((None, None, k, nb), lambda i, j, kk: (layer, j, 0, 0)),
        out_shape=jax.ShapeDtypeStruct((t, N_DEV * nb), BF16),
        out_spec=pl.BlockSpec((tm, nb), lambda i, j, kk: (i, j)), acc_shape=(tm, nb))


def _mm_natural(name, a, w, layer, *, res=None, out_dtype=F32, scale=None):
    t, k = a.shape
    n = w.shape[2]
    tm = _tile(t, 1024)
    tn = _tile(n, 512)
    res_spec = None if res is None else pl.BlockSpec((tm, tn), lambda i, j, kk: (i, j))
    return _mm(
        name, NN, a, w, grid=(t // tm, n // tn, 1),
        a_spec=pl.BlockSpec((tm, k), lambda i, j, kk: (i, 0)),
        b_spec=pl.BlockSpec((None, k, tn), lambda i, j, kk: (layer, 0, j)),
        out_shape=jax.ShapeDtypeStruct((t, n), out_dtype),
        out_spec=pl.BlockSpec((tm, tn), lambda i, j, kk: (i, j)),
        acc_shape=(tm, tn), res=res, res_spec=res_spec, scale=scale)


def _mm_down(name, act, w4, layer, res):
    nblk, t, kb = act.shape
    n = w4.shape[3]
    tm = _tile(t, 1024)

    def body(a_ref, b_ref, r_ref, o_ref):
        acc = r_ref[...]
        for u in range(nblk):
            acc = acc + jnp.dot(a_ref[u], b_ref[u], preferred_element_type=F32)
        o_ref[...] = acc

    row = pl.BlockSpec((tm, n), lambda i: (i, 0))
    return _call(
        body, name, (t // tm,),
        [pl.BlockSpec((nblk, tm, kb), lambda i: (0, i, 0)),
         pl.BlockSpec((None, nblk, kb, n), lambda i: (layer, 0, 0, 0)),
         row],
        row,
        jax.ShapeDtypeStruct((t, n), F32),
        vmem_bytes=2 * (_nbytes((nblk, tm, kb), BF16) + _nbytes((nblk, kb, n), BF16)) + 6 * _nbytes((tm, n), F32),
    )(act, w4, res)


def _mm_t_colblock_norm_bwd(name, dz, w_g, layer, x, g, dx_up, blocked_in=False):
    k = w_g.shape[2]
    nb = w_g.shape[3]
    t = x.shape[0]
    tm = _tile(t, 1024)
    per_step = 2
    n_steps = N_DEV // per_step
    if blocked_in:
        a_spec = pl.BlockSpec((per_step, tm, nb), lambda i, kk: (kk, i, 0))
    else:
        a_spec = pl.BlockSpec((tm, per_step * nb), lambda i, kk: (i, kk))

    def body(a_ref, b_ref, x_ref, g_ref, up_ref, dx_ref, dg_ref, acc_ref):
        i = pl.program_id(0)
        kk = pl.program_id(1)
        part = None
        for u in range(per_step):
            a = a_ref[u] if blocked_in else a_ref[:, u * nb:(u + 1) * nb]
            term = lax.dot_general(a.astype(BF16), b_ref[u].astype(BF16), NT, preferred_element_type=F32)
            part = term if part is None else part + term

        @pl.when(kk == 0)
        def _():
            acc_ref[...] = part

        @pl.when(kk > 0)
        def _():
            acc_ref[...] += part

        @pl.when((i == 0) & (kk == 0))
        def _():
            dg_ref[...] = jnp.zeros_like(dg_ref)

        @pl.when(kk == n_steps - 1)
        def _():
            dy = acc_ref[...]
            xf = x_ref[...]
            r = lax.rsqrt(jnp.mean(xf * xf, axis=-1, keepdims=True) + EPS)
            xhat = xf * r
            dxhat = dy * g_ref[...]
            dg_ref[...] += jnp.sum(dy * xhat, axis=0, keepdims=True)
            dx_ref[...] = up_ref[...] + r * (dxhat - xhat * jnp.mean(dxhat * xhat, axis=-1, keepdims=True))

    row = pl.BlockSpec((tm, k), lambda i, kk: (i, 0))
    vec = pl.BlockSpec((1, k), lambda i, kk: (0, 0))
    dx, dg = _call(
        body, name, (t // tm, n_steps),
        [a_spec, pl.BlockSpec((None, per_step, k, nb), lambda i, kk: (layer, kk, 0, 0)), row, vec, row],
        [row, vec],
        [jax.ShapeDtypeStruct((t, k), F32), jax.ShapeDtypeStruct((1, k), F32)],
        scratch=[pltpu.VMEM((tm, k), F32)],
        vmem_bytes=2 * per_step * (_nbytes((tm, nb), BF16) + _nbytes((k, nb), BF16)) + 10 * _nbytes((tm, k), F32),
    )(dz, w_g, x, g.reshape(1, k), dx_up)
    return dx, dg.reshape(k)


def _ffn_gate_up(name, h, w_g, layer):
    t, k = h.shape
    nb = w_g.shape[3]
    half = N_DEV // 2
    tm = _tile(t, 1024)

    def body(h_ref, wg_ref, wu_ref, dact_ref, act_ref):
        hb = h_ref[...]
        gate = jnp.dot(hb, wg_ref[...], preferred_element_type=F32)
        up = jnp.dot(hb, wu_ref[...], preferred_element_type=F32)
        sig = _sigmoid(gate)
        silu = gate * sig
        dact_ref[0] = (up * (sig * (1.0 + gate * (1.0 - sig)))).astype(BF16)
        dact_ref[1] = silu.astype(BF16)
        act_ref[...] = (silu * up).astype(BF16)

    return _call(
        body, name, (t // tm, half),
        [pl.BlockSpec((tm, k), lambda i, j: (i, 0)),
         pl.BlockSpec((None, None, k, nb), lambda i, j: (layer, j, 0, 0)),
         pl.BlockSpec((None, None, k, nb), lambda i, j: (layer, half + j, 0, 0))],
        [pl.BlockSpec((2, None, tm, nb), lambda i, j: (0, j, i, 0)),
         pl.BlockSpec((None, tm, nb), lambda i, j: (j, i, 0))],
        [jax.ShapeDtypeStruct((2, half, t, nb), BF16), jax.ShapeDtypeStruct((half, t, nb), BF16)],
        vmem_bytes=2 * (_nbytes((tm, k), BF16) + 2 * _nbytes((k, nb), BF16) + 3 * _nbytes((tm, nb), BF16))
        + 8 * _nbytes((tm, nb), F32),
    )(h, w_g, w_g)


def _ffn_down_dx(name, dy, w4, layer, dact):
    t, n = dy.shape
    nblk, kb = w4.shape[1], w4.shape[2]
    tm = _tile(t, 1024)

    def body(dy_ref, w_ref, dact_ref, dgu_ref):
        da = lax.dot_general(dy_ref[...].astype(BF16), w_ref[...], NT, preferred_element_type=F32)
        dgu_ref[0] = (da * dact_ref[0].astype(F32)).astype(BF16)
        dgu_ref[1] = (da * dact_ref[1].astype(F32)).astype(BF16)

    blk = pl.BlockSpec((2, None, tm, kb), lambda i, j: (0, j, i, 0))
    return _call(
        body, name, (t // tm, nblk),
        [pl.BlockSpec((tm, n), lambda i, j: (i, 0)),
         pl.BlockSpec((None, None, kb, n), lambda i, j: (layer, j, 0, 0)),
         blk],
        blk,
        jax.ShapeDtypeStruct((2, nblk, t, kb), BF16),
        vmem_bytes=2 * (_nbytes((tm, n), F32) + _nbytes((kb, n), BF16) + 4 * _nbytes((tm, kb), BF16))
        + 8 * _nbytes((tm, kb), F32),
    )(dy, w4, dact)


def _mm_t_natural(name, dy, w, layer):
    t, n = dy.shape
    k = w.shape[1]
    tm = _tile(t, 1024)
    tk = _tile(k, 512)
    return _mm(
        name, NT, dy, w, grid=(t // tm, k // tk, 1),
        a_spec=pl.BlockSpec((tm, n), lambda i, j, kk: (i, 0)),
        b_spec=pl.BlockSpec((None, tk, n), lambda i, j, kk: (layer, j, 0)),
        out_shape=jax.ShapeDtypeStruct((t, k), BF16),
        out_spec=pl.BlockSpec((tm, tk), lambda i, j, kk: (i, j)),
        acc_shape=(tm, tk))


def _mm_dw_colblock(name, h, dz, blocked_in=False, transposed=False):
    t, k = h.shape
    nb = dz.shape[2] if blocked_in else dz.shape[1] // N_DEV
    tk = _tile(t, 2048)
    h_spec = pl.BlockSpec((tk, k), lambda i, j, kk: (kk, 0))
    if blocked_in:
        dz_spec = pl.BlockSpec((None, tk, nb), lambda i, j, kk: (j, kk, 0))
    else:
        dz_spec = pl.BlockSpec((tk, nb), lambda i, j, kk: (kk, j))
    rows, cols = (nb, k) if transposed else (k, nb)
    return _mm(
        name, TN, *((dz, h) if transposed else (h, dz)), grid=(1, N_DEV, t // tk),
        a_spec=dz_spec if transposed else h_spec,
        b_spec=h_spec if transposed else dz_spec,
        out_shape=jax.ShapeDtypeStruct((N_DEV, rows, cols), BF16),
        out_spec=pl.BlockSpec((None, rows, cols), lambda i, j, kk: (j, 0, 0)),
        acc_shape=(rows, cols))


def _mm_dw_natural(name, a, dy):
    t, k = a.shape
    n = dy.shape[1]
    tko = _tile(k, 1024)
    tt = _tile(t, 2048)
    out = _mm(
        name, TN, a, dy, grid=(k // tko, 1, t // tt),
        a_spec=pl.BlockSpec((tt, tko), lambda i, j, kk: (kk, i)),
        b_spec=pl.BlockSpec((tt, n), lambda i, j, kk: (kk, 0)),
        out_shape=jax.ShapeDtypeStruct((k, n), BF16),
        out_spec=pl.BlockSpec((tko, n), lambda i, j, kk: (i, 0)),
        acc_shape=(tko, n))
    return out.reshape(N_DEV, k // N_DEV, n)


def _mm_dw_down(name, act, dy):
    nblk, t, kb = act.shape
    n = dy.shape[1]
    tt = _tile(t, 2048)
    out = _mm(
        name, TN, act, dy, grid=(nblk, 1, t // tt),
        a_spec=pl.BlockSpec((None, tt, kb), lambda i, j, kk: (i, kk, 0)),
        b_spec=pl.BlockSpec((tt, n), lambda i, j, kk: (kk, 0)),
        out_shape=jax.ShapeDtypeStruct((nblk, kb, n), BF16),
        out_spec=pl.BlockSpec((None, kb, n), lambda i, j, kk: (i, 0, 0)),
        acc_shape=(kb, n))
    return out.reshape(N_DEV, (nblk * kb) // N_DEV, n)


def _spatial_mask(transposed=False):
    r = lax.broadcasted_iota(jnp.int32, (A_CHUNK, A_CHUNK), 0) // CHUNK
    c = lax.broadcasted_iota(jnp.int32, (A_CHUNK, A_CHUNK), 1) // CHUNK
    return c >= r if transposed else r >= c


def _sgu_tile(t):
    return _tile(t, 2 * A_CHUNK)


def _sgu_fwd(zpre, g_sgu, w_sp, b_full, name):
    t, f2 = zpre.shape
    f = f2 // 2
    gd = f // A_GROUPS
    tm = _sgu_tile(t)

    def body(z_ref, g_ref, w_ref, b_ref, p_ref, zs_ref, dg_ref):
        mask = _spatial_mask()
        wm = [jnp.where(mask, w_ref[g], 0.0).astype(BF16) for g in range(A_GROUPS)]
        for c in range(tm // A_CHUNK):
            rows = pl.ds(c * A_CHUNK, A_CHUNK)
            z, dgelu = _gelu_and_grad(z_ref[rows, :].astype(F32))
            zs_ref[rows, :] = z.astype(BF16)
            dg_ref[rows, :] = dgelu.astype(BF16)
            u = z[:, :f]
            v0 = z[:, f:]
            r = lax.rsqrt(jnp.mean(v0 * v0, axis=-1, keepdims=True) + EPS)
            v1 = (v0 * r * g_ref[...]).astype(BF16)
            for g in range(A_GROUPS):
                cols = slice(g * gd, (g + 1) * gd)
                v2 = jnp.dot(wm[g], v1[:, cols], preferred_element_type=F32) + b_ref[:, cols]
                p_ref[rows, cols] = (u[:, cols] * v2).astype(BF16)

    return _call(
        body, name, (t // tm,),
        [pl.BlockSpec((tm, f2), lambda i: (i, 0)),
         pl.BlockSpec((1, f), lambda i: (0, 0)),
         pl.BlockSpec((A_GROUPS, A_CHUNK, A_CHUNK), lambda i: (0, 0, 0)),
         pl.BlockSpec((A_CHUNK, f), lambda i: (0, 0))],
        [pl.BlockSpec((tm, f), lambda i: (i, 0)), pl.BlockSpec((tm, f2), lambda i: (i, 0)),
         pl.BlockSpec((tm, f2), lambda i: (i, 0))],
        [jax.ShapeDtypeStruct((t, f), BF16), jax.ShapeDtypeStruct((t, f2), BF16), jax.ShapeDtypeStruct((t, f2), BF16)],
        vmem_bytes=6 * _nbytes((tm, f2), BF16) + 2 * _nbytes((tm, f), BF16) + 8 * _nbytes((A_CHUNK, f2), F32),
    )(zpre, g_sgu.reshape(1, f), w_sp, b_full)


def _sgu_bwd(zs, dgs, dp, g_sgu, w_sp, w_sp_t, b_full, name):
    t, f2 = zs.shape
    f = f2 // 2
    gd = f // A_GROUPS
    tm = _sgu_tile(t)
    n_steps = t // tm

    def body(z_ref, dgelu_ref, dp_ref, g_ref, w_ref, wt_ref, b_ref, dz_ref, dw_ref, db_ref, dg_ref, dv1_ref, dbf_ref):
        step = pl.program_id(0)

        @pl.when(step == 0)
        def _():
            dw_ref[...] = jnp.zeros_like(dw_ref)
            dg_ref[...] = jnp.zeros_like(dg_ref)
            dbf_ref[...] = jnp.zeros_like(dbf_ref)

        mask = _spatial_mask()
        mask_t = _spatial_mask(transposed=True)
        wm = [jnp.where(mask, w_ref[g], 0.0).astype(BF16) for g in range(A_GROUPS)]
        wmt = [jnp.where(mask_t, wt_ref[g], 0.0).astype(BF16) for g in range(A_GROUPS)]
        gain = g_ref[...]
        for c in range(tm // A_CHUNK):
            rows = pl.ds(c * A_CHUNK, A_CHUNK)
            z = z_ref[rows, :].astype(F32)
            dgelu = dgelu_ref[rows, :].astype(F32)
            u = z[:, :f]
            v0 = z[:, f:]
            r = lax.rsqrt(jnp.mean(v0 * v0, axis=-1, keepdims=True) + EPS)
            xhat = v0 * r
            v1 = (xhat * gain).astype(BF16)
            dpf = dp_ref[rows, :].astype(F32)
            for g in range(A_GROUPS):
                cols = slice(g * gd, (g + 1) * gd)
                v1g = v1[:, cols]
                v2 = jnp.dot(wm[g], v1g, preferred_element_type=F32) + b_ref[:, cols]
                dpg = dpf[:, cols]
                dz_ref[rows, cols] = (dpg * v2 * dgelu[:, cols]).astype(BF16)
                dv2 = dpg * u[:, cols]
                dbf_ref[:, cols] += dv2
                dv2b = dv2.astype(BF16)
                dwg = lax.dot_general(dv2b, v1g, NT, preferred_element_type=F32)
                dw_ref[g] += jnp.where(mask, dwg, 0.0)
                dv1_ref[:, cols] = jnp.dot(wmt[g], dv2b, preferred_element_type=F32)
            dv1 = dv1_ref[...]
            dxhat = dv1 * gain
            dg_ref[...] += jnp.sum(dv1 * xhat, axis=0, keepdims=True)
            dv0 = r * (dxhat - xhat * jnp.mean(dxhat * xhat, axis=-1, keepdims=True))
            dz_ref[rows, pl.ds(f, f)] = (dv0 * dgelu[:, f:]).astype(BF16)

        @pl.when(step == n_steps - 1)
        def _():
            for g in range(A_GROUPS):
                db_ref[g] = jnp.sum(dbf_ref[:, g * gd:(g + 1) * gd], axis=1, keepdims=True)

    wspec = pl.BlockSpec((A_GROUPS, A_CHUNK, A_CHUNK), lambda i: (0, 0, 0))
    dz, dw, db, dg = _call(
        body, name, (n_steps,),
        [pl.BlockSpec((tm, f2), lambda i: (i, 0)),
         pl.BlockSpec((tm, f2), lambda i: (i, 0)),
         pl.BlockSpec((tm, f), lambda i: (i, 0)),
         pl.BlockSpec((1, f), lambda i: (0, 0)),
         wspec, wspec,
         pl.BlockSpec((A_CHUNK, f), lambda i: (0, 0))],
        [pl.BlockSpec((tm, f2), lambda i: (i, 0)),
         wspec,
         pl.BlockSpec((A_GROUPS, A_CHUNK, 1), lambda i: (0, 0, 0)),
         pl.BlockSpec((1, f), lambda i: (0, 0))],
        [jax.ShapeDtypeStruct((t, f2), BF16),
         jax.ShapeDtypeStruct((A_GROUPS, A_CHUNK, A_CHUNK), F32),
         jax.ShapeDtypeStruct((A_GROUPS, A_CHUNK, 1), F32),
         jax.ShapeDtypeStruct((1, f), F32)],
        scratch=[pltpu.VMEM((A_CHUNK, f), F32), pltpu.VMEM((A_CHUNK, f), F32)],
        vmem_bytes=6 * _nbytes((tm, f2), BF16) + 2 * _nbytes((tm, f), BF16) + 12 * _nbytes((A_CHUNK, f2), F32),
    )(zs, dgs, dp, g_sgu.reshape(1, f), w_sp, w_sp_t, b_full)
    return dz, dw, db.reshape(A_GROUPS, A_CHUNK), dg.reshape(f)


def _pair_valid(qi, col):
    qc = qi // CHUNK
    kc = col // CHUNK
    return (kc >= qc) & (kc <= qc + N_LEFT_CHUNKS)


def _diagonal_onehot():
    e = lax.broadcasted_iota(jnp.int32, (REL_PAD, DIAGONALS), 1)
    idx = jnp.clip(PAIR_BAND - 1 - e, -MAX_REL, MAX_REL) + MAX_REL
    r = lax.broadcasted_iota(jnp.int32, (REL_PAD, DIAGONALS), 0)
    return jnp.where(r == idx, 1.0, 0.0).astype(BF16)


def _bias_build(table, name):
    h = table.shape[0]
    tab = jnp.pad(table, ((0, 0), (0, REL_PAD - N_REL)))

    def body(t_ref, o_ref):
        oh = _diagonal_onehot()
        diag = jnp.zeros((h, DIAGONALS), F32)
        for piece in _split3(t_ref[...]):
            diag += jnp.dot(piece, oh, preferred_element_type=F32)
        col = lax.broadcasted_iota(jnp.int32, (h, PAIR_BAND), 1)
        for qi in range(PAIR_ROWS):
            row = pltpu.roll(diag, (qi - (PAIR_ROWS - 1)) % DIAGONALS, 1)[:, :PAIR_BAND]
            o_ref[qi] = jnp.where(_pair_valid(qi, col), row, NEG_INF)

    out = _call(
        body, name, (1,),
        [pl.BlockSpec((h, REL_PAD), lambda i: (0, 0))],
        pl.BlockSpec((PAIR_ROWS, h, PAIR_BAND), lambda i: (0, 0, 0)),
        jax.ShapeDtypeStruct((PAIR_ROWS, h, PAIR_BAND), F32),
        vmem_bytes=4 * _nbytes((PAIR_ROWS, h, PAIR_BAND), F32),
    )(tab)
    return jnp.transpose(out, (1, 0, 2))


def _bias_block(pair_bias):
    rest = K_BLOCK - PAIR_BAND
    return jnp.concatenate(
        [jnp.pad(pair_bias, ((0, 0), (0, 0), (p * PAIR_ROWS, rest - p * PAIR_ROWS)), constant_values=NEG_INF)
         for p in range(PAIRS_PER_BLOCK)], axis=1)


def _bias_grad(dbias, name):
    h = dbias.shape[0]
    db_t = jnp.transpose(dbias, (1, 0, 2))

    def body(d_ref, o_ref):
        diag = jnp.zeros((h, DIAGONALS), F32)
        for qi in range(PAIR_ROWS):
            diag += pltpu.roll(d_ref[qi], PAIR_ROWS - 1 - qi, 1)
        oh = _diagonal_onehot()
        acc = jnp.zeros((h, REL_PAD), F32)
        for piece in _split3(diag):
            acc += lax.dot_general(piece, oh, NT, preferred_element_type=F32)
        o_ref[...] = acc

    out = _call(
        body, name, (1,),
        [pl.BlockSpec((PAIR_ROWS, h, DIAGONALS), lambda i: (0, 0, 0))],
        pl.BlockSpec((h, REL_PAD), lambda i: (0, 0)),
        jax.ShapeDtypeStruct((h, REL_PAD), F32),
        vmem_bytes=4 * _nbytes((PAIR_ROWS, h, DIAGONALS), F32),
    )(db_t)
    return out[:, :N_REL]


def _head_masks():
    lane = lax.broadcasted_iota(jnp.int32, (Q_BLOCK, HEAD_PAIR), 1)
    return lane < HEAD_DIM, lane >= HEAD_DIM


def _block_scores(qm, kb, bias, valid):
    s = lax.dot_general(qm, kb, NT, preferred_element_type=F32) + bias
    return s if valid is None else jnp.where(valid, s, NEG_INF)


def _softmax_rows(s):
    e = jnp.exp(s - jnp.max(s, axis=-1, keepdims=True))
    return e * (1.0 / jnp.sum(e, axis=-1, keepdims=True))


def _block_probs(qm, kb, bias, valid):
    return _softmax_rows(_block_scores(qm, kb, bias, valid))


def _padded_then_plain(step, n_blocks):
    n_padded = min(LEFT // Q_BLOCK, n_blocks)
    lax.fori_loop(0, n_padded, lambda j, c: step(j, c, True), 0)
    lax.fori_loop(n_padded, n_blocks, lambda j, c: step(j, c, False), 0, unroll=ATTN_UNROLL)


def _attn_fwd(q, kvpad, bias, name):
    t, d = q.shape
    n_pairs = d // HEAD_PAIR
    n_blocks = t // Q_BLOCK

    def body(q_ref, k_ref, v_ref, b_ref, o_ref):
        masks = _head_masks()
        key = lax.broadcasted_iota(jnp.int32, (Q_BLOCK, K_BLOCK), 1)

        def step(j, carry, padded):
            r0 = pl.multiple_of(j * Q_BLOCK, Q_BLOCK)
            q2 = q_ref[pl.ds(r0, Q_BLOCK), :].astype(F32)
            kb = k_ref[pl.ds(r0, K_BLOCK), :]
            vb = v_ref[pl.ds(r0, K_BLOCK), :]
            valid = key >= LEFT - j * Q_BLOCK if padded else None
            scores = [_block_scores(jnp.where(masks[a], q2, 0.0).astype(BF16), kb, b_ref[a], valid) for a in range(2)]
            probs = [_softmax_rows(s).astype(BF16) for s in scores]
            outs = [jnp.dot(p, vb, preferred_element_type=F32) for p in probs]
            o_ref[pl.ds(r0, Q_BLOCK), :] = jnp.where(masks[0], outs[0], outs[1]).astype(BF16)
            return carry

        _padded_then_plain(step, n_blocks)

    return _call(
        body, name, (n_pairs,),
        [pl.BlockSpec((t, HEAD_PAIR), lambda p: (0, p)),
         pl.BlockSpec((LEFT + t, HEAD_PAIR), lambda p: (0, p)),
         pl.BlockSpec((LEFT + t, HEAD_PAIR), lambda p: (0, n_pairs + p)),
         pl.BlockSpec((2, Q_BLOCK, K_BLOCK), lambda p: (p, 0, 0))],
        pl.BlockSpec((t, HEAD_PAIR), lambda p: (0, p)),
        jax.ShapeDtypeStruct((t, d), BF16),
        vmem_bytes=8 * _nbytes((LEFT + t, HEAD_PAIR), BF16) + 12 * _nbytes((2, Q_BLOCK, K_BLOCK), F32),
    )(q, kvpad, kvpad, bias)


def _attn_bwd(q, kvpad, bias, do, dk_in, dv_in, name):
    t, d = q.shape
    n_pairs = d // HEAD_PAIR
    n_blocks = t // Q_BLOCK
    has_in = dk_in is not None

    def body(*refs):
        refs = list(refs)
        q_ref, k_ref, v_ref, b_ref, do_ref = refs[:5]
        refs = refs[5:]
        if has_in:
            dki_ref, dvi_ref = refs[:2]
            refs = refs[2:]
        dq_ref, dk_ref, dv_ref, db_ref = refs
        masks = _head_masks()
        key = lax.broadcasted_iota(jnp.int32, (Q_BLOCK, K_BLOCK), 1)
        if has_in:
            dk_ref[...] = dki_ref[...]
            dv_ref[...] = dvi_ref[...]
        else:
            dk_ref[...] = jnp.zeros_like(dk_ref)
            dv_ref[...] = jnp.zeros_like(dv_ref)
        db_ref[...] = jnp.zeros_like(db_ref)

        def step(j, carry, padded):
            r0 = pl.multiple_of(j * Q_BLOCK, Q_BLOCK)
            q2 = q_ref[pl.ds(r0, Q_BLOCK), :].astype(F32)
            do2 = do_ref[pl.ds(r0, Q_BLOCK), :].astype(F32)
            kb = k_ref[pl.ds(r0, K_BLOCK), :]
            vb = v_ref[pl.ds(r0, K_BLOCK), :]
            valid = key >= LEFT - j * Q_BLOCK if padded else None
            heads = range(2)
            qms = [jnp.where(masks[a], q2, 0.0).astype(BF16) for a in heads]
            doms = [jnp.where(masks[a], do2, 0.0).astype(BF16) for a in heads]
            scores = [_block_scores(qms[a], kb, b_ref[a], valid) for a in heads]
            dps = [lax.dot_general(doms[a], vb, NT, preferred_element_type=F32) for a in heads]
            ps = [_softmax_rows(s) for s in scores]
            dss = [ps[a] * (dps[a] - jnp.sum(dps[a] * ps[a], axis=-1, keepdims=True)) for a in heads]
            for a in heads:
                for pair in range(PAIRS_PER_BLOCK):
                    lo = pair * PAIR_ROWS
                    db_ref[a, :, pl.ds(0, PAIR_BAND)] += dss[a][lo:lo + PAIR_ROWS, lo:lo + PAIR_BAND]
            dsbs = [ds.astype(BF16) for ds in dss]
            pbs = [p.astype(BF16) for p in ps]
            dqs = [jnp.dot(dsbs[a], kb, preferred_element_type=F32) for a in heads]
            dk_acc = sum(lax.dot_general(dsbs[a], qms[a], TN, preferred_element_type=F32) for a in heads)
            dv_acc = sum(lax.dot_general(pbs[a], doms[a], TN, preferred_element_type=F32) for a in heads)
            dq = jnp.where(masks[0], dqs[0], dqs[1]) * ATTN_SCALE
            dq_ref[pl.ds(r0, Q_BLOCK), :] = dq.astype(BF16)
            dk_ref[pl.ds(r0, K_BLOCK), :] += dk_acc
            dv_ref[pl.ds(r0, K_BLOCK), :] += dv_acc
            return carry

        _padded_then_plain(step, n_blocks)

    q_spec = pl.BlockSpec((t, HEAD_PAIR), lambda p: (0, p))
    kv_spec = pl.BlockSpec((LEFT + t, HEAD_PAIR), lambda p: (0, p))
    operands = [q, kvpad, kvpad, bias, do]
    in_specs = [q_spec, kv_spec, pl.BlockSpec((LEFT + t, HEAD_PAIR), lambda p: (0, n_pairs + p)),
                pl.BlockSpec((2, Q_BLOCK, K_BLOCK), lambda p: (p, 0, 0)), q_spec]
    aliases = None
    if has_in:
        operands += [dk_in, dv_in]
        in_specs += [kv_spec, kv_spec]
        aliases = {5: 1, 6: 2}
    return _call(
        body, name, (n_pairs,),
        in_specs,
        [q_spec, kv_spec, kv_spec, pl.BlockSpec((2, PAIR_ROWS, DIAGONALS), lambda p: (p, 0, 0))],
        [jax.ShapeDtypeStruct((t, d), BF16),
         jax.ShapeDtypeStruct((LEFT + t, d), F32),
         jax.ShapeDtypeStruct((LEFT + t, d), F32),
         jax.ShapeDtypeStruct((d // HEAD_DIM, PAIR_ROWS, DIAGONALS), F32)],
        vmem_bytes=10 * _nbytes((LEFT + t, HEAD_PAIR), BF16) + 8 * _nbytes((LEFT + t, HEAD_PAIR), F32)
        + 16 * _nbytes((2, Q_BLOCK, K_BLOCK), F32),
        aliases=aliases,
    )(*operands)


def _loss_head(x, g, target, name):
    t, d = x.shape
    tm = _tile(t, 512)

    def body(x_ref, g_ref, t_ref, dx_ref, loss_ref, dg_ref):
        @pl.when(pl.program_id(0) == 0)
        def _():
            loss_ref[...] = jnp.zeros_like(loss_ref)
            dg_ref[...] = jnp.zeros_like(dg_ref)

        xf = x_ref[...]
        r = lax.rsqrt(jnp.mean(xf * xf, axis=-1, keepdims=True) + EPS)
        xhat = xf * r
        diff = xhat * g_ref[...] - t_ref[...]
        row_loss = jnp.mean(diff * diff, axis=-1, keepdims=True)
        loss_ref[...] += 0.5 * jnp.sum(row_loss, axis=0, keepdims=True)
        dy = diff * (1.0 / d)
        dg_ref[...] += jnp.sum(dy * xhat, axis=0, keepdims=True)
        dxhat = dy * g_ref[...]
        dx_ref[...] = r * (dxhat - xhat * jnp.mean(dxhat * xhat, axis=-1, keepdims=True))

    row = pl.BlockSpec((tm, d), lambda i: (i, 0))
    vec = pl.BlockSpec((1, d), lambda i: (0, 0))
    dx, loss, dg = _call(
        body, name, (t // tm,),
        [row, vec, row],
        [row, pl.BlockSpec((1, 1), lambda i: (0, 0)), vec],
        [jax.ShapeDtypeStruct((t, d), F32), jax.ShapeDtypeStruct((1, 1), F32), jax.ShapeDtypeStruct((1, d), F32)],
        vmem_bytes=10 * _nbytes((tm, d), F32),
    )(x, g.reshape(1, d), target)
    return dx, loss[0, 0], dg.reshape(d)


def _adamw_store(g, w_ref, m_ref, v_ref, g_ref, d_ref, nm_ref, nv_ref):
    c1 = 1.0 / (1.0 - ADAM_B1 ** ADAM_STEP)
    c2 = 1.0 / (1.0 - ADAM_B2 ** ADAM_STEP)
    nm = ADAM_B1 * m_ref[...] + (1.0 - ADAM_B1) * g
    nv = ADAM_B2 * v_ref[...] + (1.0 - ADAM_B2) * (g * g)
    g_ref[...] = g
    nm_ref[...] = nm
    nv_ref[...] = nv
    d_ref[...] = -ADAM_LR * ((nm * c1) / (jnp.sqrt(nv * c2) + ADAM_EPS) + ADAM_WD * w_ref[...])


def _adamw_layer(recv, own, w, m, v, layer, prev, me, name):
    n_src, r, c = recv.shape
    tr = _row_tile(r, max(16, (256 * 1024) // c), 16)
    first = prev is None

    def body(me_ref, recv_ref, own_ref, w_ref, m_ref, v_ref, *rest):
        mine = me_ref[0]
        own_part = own_ref[...].astype(F32)
        g = None
        for s in range(n_src):
            part = jnp.where(mine == s, own_part, recv_ref[s].astype(F32))
            g = part if g is None else g + part
        _adamw_store(g, w_ref, m_ref, v_ref, *rest[-4:])

    blk = pl.BlockSpec((None, tr, c), lambda i, me_ref: (layer, i, 0))
    any_spec = pl.BlockSpec(memory_space=pl.ANY)
    out = jax.ShapeDtypeStruct(w.shape, F32)
    operands = [me, recv, own, w, m, v] + ([] if first else list(prev))
    vmem = 2 * _nbytes((n_src + 1, tr, c), BF16) + 18 * _nbytes((tr, c), F32)
    return pl.pallas_call(
        body,
        name=name,
        grid_spec=pltpu.PrefetchScalarGridSpec(
            num_scalar_prefetch=1,
            grid=(r // tr,),
            in_specs=[pl.BlockSpec((n_src, tr, c), lambda i, me_ref: (0, i, 0)),
                      pl.BlockSpec((None, tr, c), lambda i, me_ref: (me_ref[0], i, 0)),
                      blk, blk, blk] + ([] if first else [any_spec] * 4),
            out_specs=[blk, blk, blk, blk],
        ),
        out_shape=[out, out, out, out],
        input_output_aliases={} if first else {6 + j: j for j in range(4)},
        compiler_params=pltpu.CompilerParams(
            dimension_semantics=("arbitrary",),
            vmem_limit_bytes=int(min(max(VMEM_FLOOR_BYTES, vmem * 5 // 4), VMEM_CEIL_BYTES))),
    )(*operands)


def _adamw(parts, w, m, v, name):
    n_layers, n_src, r, c = parts.shape
    mult = 16 if parts.dtype == BF16 else 8
    tr = _row_tile(r, max(mult, (256 * 1024) // c), mult)

    def body(p_ref, w_ref, m_ref, v_ref, g_ref, d_ref, nm_ref, nv_ref):
        g = p_ref[0].astype(F32)
        for s in range(1, n_src):
            g = g + p_ref[s].astype(F32)
        _adamw_store(g, w_ref, m_ref, v_ref, g_ref, d_ref, nm_ref, nv_ref)

    blk = pl.BlockSpec((None, tr, c), lambda l, i: (l, i, 0))
    out = jax.ShapeDtypeStruct((n_layers, r, c), F32)
    return _call(
        body, name, (n_layers, r // tr),
        [pl.BlockSpec((None, n_src, tr, c), lambda l, i: (l, 0, i, 0)), blk, blk, blk],
        [blk, blk, blk, blk],
        [out, out, out, out],
        vmem_bytes=2 * _nbytes((n_src, tr, c), parts.dtype) + 18 * _nbytes((tr, c), F32),
    )(parts, w, m, v)


def _ordered_sum(parts, name):
    n_src, r, c = parts.shape

    def body(p_ref, o_ref):
        acc = p_ref[0]
        for s in range(1, n_src):
            acc = acc + p_ref[s]
        o_ref[...] = acc

    return _call(
        body, name, (1,),
        [pl.BlockSpec((n_src, r, c), lambda i: (0, 0, 0))],
        pl.BlockSpec((r, c), lambda i: (0, 0)),
        jax.ShapeDtypeStruct((r, c), F32),
        vmem_bytes=4 * _nbytes((n_src, r, c), F32),
    )(parts)


def _position():
    return lax.axis_index("x"), lax.axis_index("y"), lax.axis_index("c")


def _linear(p):
    return 4 * p[0] + 2 * p[1] + p[2]


def _all_gather(shards, name):
    n = len(shards)

    def body(*refs):
        ins, outs = refs[:n], refs[n:2 * n]
        send_sems, recv_sems, local_sems = refs[2 * n:]
        x, y, c = _position()
        me, sibling = (x, y, c), (x, y, 1 - c)
        chips = [(1 - x, y), (x, 1 - y), (1 - x, 1 - y)]

        def slab(t, p):
            return outs[t].at[:, _linear(p)]

        def copy(t, k, block, to, src=None):
            return pltpu.make_async_remote_copy(
                src_ref=slab(t, block) if src is None else src,
                dst_ref=slab(t, block),
                send_sem=send_sems.at[t, k],
                recv_sem=recv_sems.at[t, k],
                device_id=to,
                device_id_type=MESH,
            )

        started = []
        for t in range(n):
            mine = pltpu.make_async_copy(ins[t], slab(t, me), local_sems.at[t])
            mine.start()
            started.append(mine)
        sends = []
        for t in range(n):
            first = [copy(t, 0, me, sibling, src=ins[t])]
            first += [copy(t, 1 + j, me, (*chip, c), src=ins[t]) for j, chip in enumerate(chips)]
            for cp in first:
                cp.start()
            sends += first
        for t in range(n):
            for j, chip in enumerate(chips):
                copy(t, 1 + j, (*chip, c), me).wait_recv()
                passed = copy(t, 4 + j, (*chip, c), sibling)
                passed.start()
                sends.append(passed)
        for t in range(n):
            copy(t, 0, sibling, me).wait_recv()
            for j, chip in enumerate(chips):
                copy(t, 4 + j, (*chip, 1 - c), me).wait_recv()
        for cp in sends:
            cp.wait_send()
        for mine in started:
            mine.wait()

    out_shape = [jax.ShapeDtypeStruct((s.shape[0], N_DEV) + s.shape[1:], s.dtype) for s in shards]
    return pl.pallas_call(
        body,
        name=name,
        in_specs=[HBM_SPEC] * n,
        out_specs=[HBM_SPEC] * n,
        out_shape=out_shape,
        scratch_shapes=[
            pltpu.SemaphoreType.DMA((n, N_DEV - 1)),
            pltpu.SemaphoreType.DMA((n, N_DEV - 1)),
            pltpu.SemaphoreType.DMA((n,)),
        ],
    )(*shards)


def _exchange(blocks, name):
    n = len(blocks)

    def body(*refs):
        ins, outs = refs[:n], refs[n:2 * n]
        send_sems, recv_sems, local_sems = refs[2 * n:]
        x, y, c = _position()
        me = _linear((x, y, c))
        flips = [(fx, fy, fc) for fx in (0, 1) for fy in (0, 1) for fc in (0, 1)][1:]

        def peer_of(flip):
            fx, fy, fc = flip
            return (1 - x if fx else x, 1 - y if fy else y, 1 - c if fc else c)

        def copy(t, k, peer):
            return pltpu.make_async_remote_copy(
                src_ref=ins[t].at[:, _linear(peer)],
                dst_ref=outs[t].at[:, me],
                send_sem=send_sems.at[t, k],
                recv_sem=recv_sems.at[t, k],
                device_id=peer,
                device_id_type=MESH,
            )

        def arrival(t, k, peer):
            return pltpu.make_async_remote_copy(
                src_ref=ins[t].at[:, _linear(peer)],
                dst_ref=outs[t].at[:, _linear(peer)],
                send_sem=send_sems.at[t, k],
                recv_sem=recv_sems.at[t, k],
                device_id=peer,
                device_id_type=MESH,
            )

        own = []
        for t in range(n):
            cp = pltpu.make_async_copy(ins[t].at[:, me], outs[t].at[:, me], local_sems.at[t])
            cp.start()
            own.append(cp)
        sends = []
        for t in range(n):
            for k, flip in enumerate(flips):
                cp = copy(t, k, peer_of(flip))
                cp.start()
                sends.append(cp)
        for t in range(n):
            for k, flip in enumerate(flips):
                arrival(t, k, peer_of(flip)).wait_recv()
        for cp in sends:
            cp.wait_send()
        for cp in own:
            cp.wait()

    out_shape = [jax.ShapeDtypeStruct(b.shape, b.dtype) for b in blocks]
    return pl.pallas_call(
        body,
        name=name,
        in_specs=[HBM_SPEC] * n,
        out_specs=[HBM_SPEC] * n,
        out_shape=out_shape,
        scratch_shapes=[
            pltpu.SemaphoreType.DMA((n, N_DEV - 1)),
            pltpu.SemaphoreType.DMA((n, N_DEV - 1)),
            pltpu.SemaphoreType.DMA((n,)),
        ],
    )(*blocks)


def _peers():
    x, y, c = _position()
    flips = [(fx, fy, fc) for fx in (0, 1) for fy in (0, 1) for fc in (0, 1)][1:]
    return [(1 - x if fx else x, 1 - y if fy else y, 1 - c if fc else c) for fx, fy, fc in flips]


def _split_start(srcs, lands, carry, name, exchange=False):
    n = len(srcs)

    def body(*refs):
        src_refs, land_refs = refs[:n], refs[n:2 * n]
        send_sems, recv_sems = refs[2 * n + 1], refs[2 * n + 2]
        me = _linear(_position())
        for t in range(n):
            for k, peer in enumerate(_peers()):
                pltpu.make_async_remote_copy(
                    src_ref=src_refs[t].at[_linear(peer)] if exchange else src_refs[t],
                    dst_ref=land_refs[t].at[me],
                    send_sem=send_sems.at[t * (N_DEV - 1) + k],
                    recv_sem=recv_sems.at[t * (N_DEV - 1) + k],
                    device_id=peer,
                    device_id_type=MESH,
                ).start()

    operands = list(srcs) + list(lands) + [carry]
    sems = pltpu.SemaphoreType.DMA((n * (N_DEV - 1),))
    out = pl.pallas_call(
        body,
        name=name,
        in_specs=[HBM_SPEC] * len(operands),
        out_specs=[SEM_SPEC, SEM_SPEC] + [HBM_SPEC] * len(operands),
        out_shape=[sems, sems] + [pltpu.HBM(a.shape, a.dtype) for a in operands],
        input_output_aliases={i: 2 + i for i in range(len(operands))},
        compiler_params=pltpu.CompilerParams(has_side_effects=pltpu.SideEffectType.DATAFLOW_SIDE_EFFECTING),
    )(*[pltpu.with_memory_space_constraint(a, pltpu.HBM) for a in operands])
    return out[0], out[1], out[2:2 + n], out[2 + n:2 + 2 * n], out[2 + 2 * n]


def _split_wait(send_sems, recv_sems, srcs, lands, after, name, exchange=False):
    n = len(srcs)

    def body(*refs):
        src_refs, land_refs = refs[:n], refs[n:2 * n]
        send_ref, recv_ref = refs[2 * n], refs[2 * n + 1]
        for t in range(n):
            for k, peer in enumerate(_peers()):
                copy = pltpu.make_async_remote_copy(
                    src_ref=src_refs[t].at[0] if exchange else src_refs[t],
                    dst_ref=land_refs[t].at[0],
                    send_sem=send_ref.at[t * (N_DEV - 1) + k],
                    recv_sem=recv_ref.at[t * (N_DEV - 1) + k],
                    device_id=peer,
                    device_id_type=MESH,
                )
                copy.wait_send()
                copy.wait_recv()

    arrays = list(srcs) + list(lands)
    out = pl.pallas_call(
        body,
        name=name,
        in_specs=[HBM_SPEC] * len(arrays) + [SEM_SPEC, SEM_SPEC, pl.BlockSpec(memory_space=pl.ANY)],
        out_specs=[HBM_SPEC] * len(arrays),
        out_shape=[pltpu.HBM(a.shape, a.dtype) for a in arrays],
        input_output_aliases={i: i for i in range(len(arrays))},
        compiler_params=pltpu.CompilerParams(has_side_effects=pltpu.SideEffectType.DATAFLOW_SIDE_EFFECTING),
    )(*arrays, send_sems, recv_sems, after)
    return out[:n], out[n:]


def _pack(arrays, row_multiple):
    flat = jnp.concatenate([a.reshape(-1) for a in arrays])
    quantum = row_multiple * FLAT_LANES
    padded = -(-flat.shape[0] // quantum) * quantum
    return jnp.pad(flat, (0, padded - flat.shape[0])).reshape(-1, FLAT_LANES)


def _unpack(flat, like):
    flat = flat.reshape(-1)
    out, at = [], 0
    for a in like:
        size = math.prod(a.shape)
        out.append(flat[at:at + size].reshape(a.shape))
        at += size
    return out


def kernel(x, a_norm, a_w_in, a_sgu_norm, a_w_spatial, a_b_spatial, a_w_out, kv_norm, w_kv, b_norm, b_w_q, b_rel_bias, b_w_o, ffn_norm, ffn_w_gate_up, ffn_w_down, final_norm, loss_target, m_a_norm, m_a_w_in, m_a_sgu_norm, m_a_w_spatial, m_a_b_spatial, m_a_w_out, m_kv_norm, m_w_kv, m_b_norm, m_b_w_q, m_b_rel_bias, m_b_w_o, m_ffn_norm, m_ffn_w_gate_up, m_ffn_w_down, m_final_norm, v_a_norm, v_a_w_in, v_a_sgu_norm, v_a_w_spatial, v_a_b_spatial, v_a_w_out, v_kv_norm, v_w_kv, v_b_norm, v_b_w_q, v_b_rel_bias, v_b_w_o, v_ffn_norm, v_ffn_w_gate_up, v_ffn_w_down, v_final_norm):
    xs = x[0]
    target = loss_target[0]
    t, d = xs.shape
    n_a = a_w_in.shape[0]
    n_b = b_w_q.shape[0]
    depth = ffn_w_gate_up.shape[0]
    f_a = a_w_out.shape[1] * N_DEV
    gd = f_a // A_GROUPS
    nb_ffn = ffn_w_gate_up.shape[2]
    me = _linear(_position())

    small_rows = -(-(a_norm.size + a_sgu_norm.size) // (8 * 128)) * 8
    small = jnp.pad(jnp.concatenate([a_norm.reshape(-1), a_sgu_norm.reshape(-1)]),
                    (0, small_rows * 128 - a_norm.size - a_sgu_norm.size)).reshape(1, small_rows, 128)

    def shard(w, layer=None):
        return (w if layer is None else w[layer]).astype(BF16)

    stages = []
    for layer in range(depth):
        if layer == 0:
            stages += [("a0", [shard(a_w_in, 0)]), ("a0_out", [shard(a_w_out, 0)])]
        elif layer < n_a:
            stages.append((f"a{layer}", [shard(a_w_in, layer), shard(a_w_out, layer)]))
        else:
            i = layer - n_a
            shared = [shard(w_kv)] if i == 0 else []
            stages.append((f"b{i}", shared + [shard(b_w_q, i), shard(b_w_o, i)]))
        stages.append((f"f{layer}", [shard(ffn_w_gate_up, layer), shard(ffn_w_down, layer)]))
    first = _all_gather([s[None] for s in stages[0][1]] + [small], "gather_first")
    gathered = {stages[0][0]: [g[0] for g in first[:-1]]}
    small_g = first[-1].reshape(N_DEV, -1)
    a_norm_full = small_g[:, :a_norm.size].reshape(N_DEV, n_a, -1).transpose(1, 0, 2).reshape(n_a, d)
    a_sgu_full = small_g[:, a_norm.size:a_norm.size + a_sgu_norm.size].reshape(
        N_DEV, n_a, -1).transpose(1, 0, 2).reshape(n_a, f_a)
    in_flight = {}
    for key, shards in stages[1:]:
        lands = [lax.dynamic_update_slice(lax.empty((N_DEV,) + s.shape, BF16), s[None], (me, 0, 0)) for s in shards]
        send, recv, srcs, lands, a_norm_full = _split_start(shards, lands, a_norm_full, f"gather_start_{key}")
        in_flight[key] = (send, recv, srcs, lands)

    def weights(key, after):
        if key not in gathered:
            _, gathered[key] = _split_wait(*in_flight.pop(key), after, f"gather_wait_{key}")
        return gathered[key]

    rows_down = ffn_w_down.shape[1]

    def mixer_a_weights(i, after):
        if i == 0:
            (w_in,), (w_out,) = weights("a0", after[0]), weights("a0_out", after[1])
        else:
            w_in, w_out = weights(f"a{i}", after[0])
        return w_in[None], w_out.reshape(1, f_a, d)

    def mixer_b_weights(i, after):
        ws = weights(f"b{i}", after)
        return ws[-2].reshape(1, d, d), ws[-1].reshape(1, d, d)

    def ffn_weights(layer, after):
        w_gu, w_dn = weights(f"f{layer}", after)
        return w_gu[None], w_dn.reshape(1, N_DEV // 2, 2 * rows_down, d)

    w_sp_t = jnp.swapaxes(a_w_spatial, -1, -2)
    b_full = jnp.repeat(jnp.swapaxes(a_b_spatial, -1, -2), gd, axis=-1)

    saved = []

    def ffn_fwd(xin, layer):
        hf = _rms_fwd(xin, ffn_norm[layer], f"ffn_norm_fwd_{layer}")
        w_gu, w_dn = ffn_weights(layer, xin)
        dact, act = _ffn_gate_up(f"ffn_gate_up_{layer}", hf, w_gu, 0)
        xout = _mm_down(f"ffn_down_{layer}", act, w_dn, 0, xin)
        return xout, (xin, hf, dact, act)

    for i in range(n_a):
        h = _rms_fwd(xs, a_norm_full[i], f"a_norm_fwd_{i}")
        zpre = _mm_colblock(f"a_in_{i}", h, weights(f"a{i}", xs)[0][None], 0)
        p, zs, dgs = _sgu_fwd(zpre, a_sgu_full[i], a_w_spatial[i], b_full[i], f"a_sgu_fwd_{i}")
        w_in, w_out = mixer_a_weights(i, (xs, p))
        x_mid = _mm_natural(f"a_out_{i}", p, w_out, 0, res=xs)
        x_out, ffn_saved = ffn_fwd(x_mid, i)
        saved.append((xs, h, zs, dgs, p, ffn_saved))
        xs = x_out

    x_kv = xs
    w_kv_g = weights("b0", x_kv)[0][None]
    h_kv = _rms_fwd(x_kv, kv_norm, "kv_norm_fwd")
    kv = _mm_colblock("kv_proj", h_kv, w_kv_g, 0)
    kvpad = jnp.pad(kv, ((LEFT, 0), (0, 0)))

    biases = [_bias_block(_bias_build(b_rel_bias[i], f"rel_bias_{i}")) for i in range(n_b)]
    for i in range(n_b):
        layer = n_a + i
        w_q, w_o = mixer_b_weights(i, xs)
        hb = _rms_fwd(xs, b_norm[i], f"b_norm_fwd_{i}")
        q = _mm_natural(f"b_q_{i}", hb, w_q, 0, out_dtype=BF16, scale=ATTN_SCALE)
        o = _attn_fwd(q, kvpad, biases[i], f"b_attn_fwd_{i}")
        x_mid = _mm_natural(f"b_o_{i}", o, w_o, 0, res=xs)
        x_out, ffn_saved = ffn_fwd(x_mid, layer)
        saved.append((xs, hb, q, o, ffn_saved))
        xs = x_out

    dx, loss_local, g_final = _loss_head(xs, final_norm, target, "loss_head")
    loss = lax.psum(loss_local, ("x", "y", "c"))

    big_grads = {}
    pending = []
    in_flight_grads = []

    def start_exchange(dx, tag):
        srcs = [big_grads[key] for key in pending]
        lands = [lax.empty(s.shape, BF16) for s in srcs]
        send, recv, srcs, lands, dx = _split_start(srcs, lands, dx, f"exchange_start_{tag}", exchange=True)
        in_flight_grads.append((list(pending), send, recv, srcs, lands, tag))
        pending.clear()
        return dx

    g_ffn_norm = [None] * depth
    g_a_norm = [None] * n_a
    g_a_sgu = [None] * n_a
    g_w_sp = [None] * n_a
    g_b_sp = [None] * n_a
    g_b_norm = [None] * n_b
    g_rel = [None] * n_b

    def ffn_bwd(dx, layer, ffn_saved):
        eager = layer < n_a
        xin, hf, dact, act = ffn_saved
        big_grads["ffn_w_down", layer] = _mm_dw_down(f"ffn_down_dw_{layer}", act, dx)
        pending.append(("ffn_w_down", layer))
        if eager:
            dx = start_exchange(dx, f"f{layer}_down")
        w_gu, w_dn = ffn_weights(layer, xin)
        dgu = _ffn_down_dx(f"ffn_down_dx_{layer}", dx, w_dn, 0, dact).reshape(N_DEV, t, nb_ffn)
        big_grads["ffn_w_gate_up", layer] = _mm_dw_colblock(
            f"ffn_gate_up_dw_{layer}", hf, dgu, blocked_in=True, transposed=True)
        pending.append(("ffn_w_gate_up", layer))
        if eager:
            dx = start_exchange(dx, f"f{layer}_gate_up")
        dx, g_ffn_norm[layer] = _mm_t_colblock_norm_bwd(
            f"ffn_gate_up_dx_{layer}", dgu, w_gu, 0, xin, ffn_norm[layer], dx, blocked_in=True)
        return dx

    dk = dv = None
    for i in reversed(range(n_b)):
        layer = n_a + i
        x_in, hb, q, o, ffn_saved = saved[layer]
        dx = ffn_bwd(dx, layer, ffn_saved)
        big_grads["b_w_o", i] = _mm_dw_natural(f"b_o_dw_{i}", o, dx)
        w_q, w_o = mixer_b_weights(i, x_in)
        do = _mm_t_natural(f"b_o_dx_{i}", dx, w_o, 0)
        dq, dk, dv, dbias = _attn_bwd(q, kvpad, biases[i], do, dk, dv, f"b_attn_bwd_{i}")
        g_rel[i] = _bias_grad(dbias, f"rel_bias_grad_{i}")
        big_grads["b_w_q", i] = _mm_dw_natural(f"b_q_dw_{i}", hb, dq)
        pending.extend([("b_w_o", i), ("b_w_q", i)])
        dh = _mm_t_natural(f"b_q_dx_{i}", dq, w_q, 0)
        dx, g_b_norm[i] = _rms_bwd(x_in, b_norm[i], dh, dx, f"b_norm_bwd_{i}")
        if i > 0:
            dx = start_exchange(dx, f"b{i}")

    dkv = jnp.concatenate([dk[LEFT:], dv[LEFT:]], axis=1).astype(BF16)
    big_grads["w_kv", 0] = _mm_dw_colblock("kv_proj_dw", h_kv, dkv)
    pending.append(("w_kv", 0))
    dx, g_kv_norm = _mm_t_colblock_norm_bwd("kv_proj_dx", dkv, w_kv_g, 0, x_kv, kv_norm, dx)
    dx = start_exchange(dx, "kv")

    for i in reversed(range(n_a)):
        x_in, h, zs, dgs, p, ffn_saved = saved[i]
        dx = ffn_bwd(dx, i, ffn_saved)
        big_grads["a_w_out", i] = _mm_dw_natural(f"a_out_dw_{i}", p, dx)
        pending.append(("a_w_out", i))
        dx = start_exchange(dx, f"a{i}_out")
        w_in, w_out = mixer_a_weights(i, (x_in, p))
        dp = _mm_t_natural(f"a_out_dx_{i}", dx, w_out, 0)
        dz, g_w_sp[i], g_b_sp[i], g_a_sgu[i] = _sgu_bwd(
            zs, dgs, dp, a_sgu_full[i], a_w_spatial[i], w_sp_t[i], b_full[i], f"a_sgu_bwd_{i}")
        big_grads["a_w_in", i] = _mm_dw_colblock(f"a_in_dw_{i}", h, dz)
        pending.append(("a_w_in", i))
        dx = start_exchange(dx, f"a{i}_in")
        dx, g_a_norm[i] = _mm_t_colblock_norm_bwd(f"a_in_dx_{i}", dz, w_in, 0, x_in, a_norm_full[i], dx)
    grad_x = dx[None]

    small_like = [jax.ShapeDtypeStruct((n_a, d), F32), jax.ShapeDtypeStruct((n_a, f_a), F32),
                  a_w_spatial, a_b_spatial, kv_norm, b_norm, b_rel_bias, ffn_norm, final_norm]
    small_partial = _pack(
        [jnp.stack(g_a_norm), jnp.stack(g_a_sgu), jnp.stack(g_w_sp), jnp.stack(g_b_sp), g_kv_norm,
         jnp.stack(g_b_norm), jnp.stack(g_rel), jnp.stack(g_ffn_norm), g_final], N_DEV * 8)
    chunk_rows = small_partial.shape[0] // N_DEV
    arrived = {}
    for keys, send, recv, srcs, lands, tag in in_flight_grads:
        srcs, lands = _split_wait(send, recv, srcs, lands, dx, f"exchange_wait_{tag}", exchange=True)
        for key, src, land in zip(keys, srcs, lands):
            arrived[key] = (land, src)
    small_got = _exchange([small_partial.reshape(1, N_DEV, chunk_rows, FLAT_LANES)], "exchange_small")[0]
    small_sum = _ordered_sum(small_got[0], "small_grad_sum")
    small_all = _all_gather([small_sum[None]], "gather_small_grads")[0]
    (ga_norm, ga_sgu, gw_sp, gb_sp, gkv_norm, gb_norm, g_relb, gffn_norm, gfinal) = _unpack(small_all, small_like)

    results = {}
    big_names = ["a_w_in", "a_w_out", "w_kv", "b_w_q", "b_w_o", "ffn_w_gate_up", "ffn_w_down"]
    big_wmv = [(a_w_in, m_a_w_in, v_a_w_in), (a_w_out, m_a_w_out, v_a_w_out),
               (w_kv[None], m_w_kv[None], v_w_kv[None]), (b_w_q, m_b_w_q, v_b_w_q), (b_w_o, m_b_w_o, v_b_w_o),
               tuple(jnp.swapaxes(a, 1, 2) for a in (ffn_w_gate_up, m_ffn_w_gate_up, v_ffn_w_gate_up)),
               (ffn_w_down, m_ffn_w_down, v_ffn_w_down)]
    me_arr = jnp.reshape(me, (1,)).astype(jnp.int32)
    for name, (w, m, v) in zip(big_names, big_wmv):
        outs = None
        for layer in range(w.shape[0]):
            got, own = arrived[name, layer]
            outs = _adamw_layer(got, own, w, m, v, layer, outs, me_arr, f"adamw_{name}_{layer}")
        if name == "w_kv":
            outs = [o[0] for o in outs]
        if name == "ffn_w_gate_up":
            outs = [jnp.swapaxes(o, 1, 2) for o in outs]
        results[name] = outs

    n_cols = a_norm.shape[1]
    s_cols = a_sgu_norm.shape[1]
    small_g_list = [lax.dynamic_slice(ga_norm, (0, me * n_cols), (n_a, n_cols)),
                    lax.dynamic_slice(ga_sgu, (0, me * s_cols), (n_a, s_cols)),
                    gw_sp, gb_sp, gkv_norm, gb_norm, g_relb, gffn_norm, gfinal]
    small_names = ["a_norm", "a_sgu_norm", "a_w_spatial", "a_b_spatial", "kv_norm", "b_norm", "b_rel_bias",
                   "ffn_norm", "final_norm"]
    small_w = [a_norm, a_sgu_norm, a_w_spatial, a_b_spatial, kv_norm, b_norm, b_rel_bias, ffn_norm, final_norm]
    small_m = [m_a_norm, m_a_sgu_norm, m_a_w_spatial, m_a_b_spatial, m_kv_norm, m_b_norm, m_b_rel_bias,
               m_ffn_norm, m_final_norm]
    small_v = [v_a_norm, v_a_sgu_norm, v_a_w_spatial, v_a_b_spatial, v_kv_norm, v_b_norm, v_b_rel_bias,
               v_ffn_norm, v_final_norm]
    flat_g = _pack(small_g_list, 8)
    flat_out = _adamw(flat_g[None, None], _pack(small_w, 8)[None], _pack(small_m, 8)[None],
                      _pack(small_v, 8)[None], "adamw_small")
    unpacked = [_unpack(o[0], small_w) for o in flat_out]
    for idx, name in enumerate(small_names):
        results[name] = [unpacked[kind][idx] for kind in range(4)]

    order = ["a_norm", "a_w_in", "a_sgu_norm", "a_w_spatial", "a_b_spatial", "a_w_out", "kv_norm", "w_kv",
             "b_norm", "b_w_q", "b_rel_bias", "b_w_o", "ffn_norm", "ffn_w_gate_up", "ffn_w_down", "final_norm"]
    outputs = [loss, grad_x]
    for kind in range(4):
        outputs += [results[name][kind] for name in order]
    return tuple(outputs)
```

```python
import math

import jax
import jax.numpy as jnp
from jax import lax
from jax.experimental import pallas as pl
from jax.experimental.pallas import tpu as pltpu

F32 = jnp.float32
BF16 = jnp.bfloat16
MESH = pl.DeviceIdType.MESH
HBM_SPEC = pl.BlockSpec(memory_space=pltpu.HBM)
SEM_SPEC = pl.BlockSpec(memory_space=pltpu.SEMAPHORE)

N_DEV = 8
CHUNK = 64
A_CHUNK = 128
A_GROUPS = 8
N_LEFT_CHUNKS = 8
LEFT = N_LEFT_CHUNKS * CHUNK
PAIR_ROWS = 2 * CHUNK
PAIR_BAND = PAIR_ROWS + LEFT
DIAGONALS = PAIR_BAND + PAIR_ROWS
PAIRS_PER_BLOCK = 2
Q_BLOCK = PAIRS_PER_BLOCK * PAIR_ROWS
K_BLOCK = Q_BLOCK + LEFT
ATTN_UNROLL = 2
MAX_REL = 256
N_REL = 2 * MAX_REL + 1
REL_PAD = 640
HEAD_DIM = 64
HEAD_PAIR = 2 * HEAD_DIM
ATTN_SCALE = HEAD_DIM ** -0.5
EPS = 1e-6
NEG_INF = -1e30
ADAM_LR = 0.001
ADAM_B1 = 0.9
ADAM_B2 = 0.999
ADAM_EPS = 1e-08
ADAM_WD = 0.01
ADAM_STEP = 10
FLAT_LANES = 1024
F32_SUBLANES = 8
BF16_SUBLANES = 16
ADAMW_BLOCK_ELEMS = 256 * 1024
V7X_VMEM_BYTES = 64 * 1024 * 1024
VMEM_FLOOR_BYTES = 32 * 1024 * 1024
VMEM_CEIL_BYTES = V7X_VMEM_BYTES - 8 * 1024 * 1024

NN = (((1,), (0,)), ((), ()))
NT = (((1,), (1,)), ((), ()))
TN = (((0,), (0,)), ((), ()))


def _tile(n, pref):
    return pref if n % pref == 0 else n


def _row_tile(n, pref, mult):
    best = None
    for t in range(mult, min(n, pref) + 1, mult):
        if n % t == 0:
            best = t
    return best if best is not None else n


def _nbytes(shape, dtype):
    n = 1
    for s in shape:
        if s is not None:
            n *= s
    return n * jnp.dtype(dtype).itemsize


def _call(body, name, grid, in_specs, out_specs, out_shape, scratch=(), vmem_bytes=0, aliases=None):
    limit = int(min(max(VMEM_FLOOR_BYTES, vmem_bytes * 5 // 4), VMEM_CEIL_BYTES))
    return pl.pallas_call(
        body,
        name=name,
        grid=grid,
        in_specs=in_specs,
        out_specs=out_specs,
        out_shape=out_shape,
        scratch_shapes=list(scratch),
        input_output_aliases=aliases or {},
        compiler_params=pltpu.CompilerParams(
            dimension_semantics=("arbitrary",) * len(grid), vmem_limit_bytes=limit),
    )


ERFC_P = 0.3275911 / math.sqrt(2.0)
ERFC_HALF_COEFFS = tuple(0.5 * a for a in (1.061405429, -1.453152027, 1.421413741, -0.284496736, 0.254829592))


def _gelu_and_grad(x):
    d = 1.0 + ERFC_P * jnp.abs(x)
    r = pl.reciprocal(d, approx=True)
    t = r * (2.0 - d * r)
    a5, a4, a3, a2, a1 = ERFC_HALF_COEFFS
    ex = jnp.exp(-0.5 * (x * x))
    tail = ((((a5 * t + a4) * t + a3) * t + a2) * t + a1) * t * ex
    cdf = jnp.where(x < 0, tail, 1.0 - tail)
    return x * cdf, cdf + x * ex * (1.0 / math.sqrt(2.0 * math.pi))


def _sigmoid(x):
    return 1.0 / (1.0 + jnp.exp(-x))


def _split3(x):
    hi = x.astype(BF16)
    r1 = x - hi.astype(F32)
    mid = r1.astype(BF16)
    lo = (r1 - mid.astype(F32)).astype(BF16)
    return hi, mid, lo


def _rms_fwd(x, g, name):
    t, d = x.shape
    tm = _tile(t, 512)

    def body(x_ref, g_ref, o_ref):
        xf = x_ref[...]
        r = lax.rsqrt(jnp.mean(xf * xf, axis=-1, keepdims=True) + EPS)
        o_ref[...] = (xf * r * g_ref[...]).astype(o_ref.dtype)

    return _call(
        body, name, (t // tm,),
        [pl.BlockSpec((tm, d), lambda i: (i, 0)), pl.BlockSpec((1, d), lambda i: (0, 0))],
        pl.BlockSpec((tm, d), lambda i: (i, 0)),
        jax.ShapeDtypeStruct((t, d), BF16),
        vmem_bytes=2 * (_nbytes((tm, d), F32) + _nbytes((tm, d), BF16)) + 4 * _nbytes((tm, d), F32),
    )(x, g.reshape(1, d))


def _rms_bwd(x, g, dh, dx_up, name):
    t, d = x.shape
    tm = _tile(t, 512)

    def body(x_ref, g_ref, dh_ref, up_ref, dx_ref, dg_ref):
        @pl.when(pl.program_id(0) == 0)
        def _():
            dg_ref[...] = jnp.zeros_like(dg_ref)

        xf = x_ref[...]
        r = lax.rsqrt(jnp.mean(xf * xf, axis=-1, keepdims=True) + EPS)
        xhat = xf * r
        dy = dh_ref[...].astype(F32)
        dxhat = dy * g_ref[...]
        dg_ref[...] += jnp.sum(dy * xhat, axis=0, keepdims=True)
        dx = r * (dxhat - xhat * jnp.mean(dxhat * xhat, axis=-1, keepdims=True))
        dx_ref[...] = up_ref[...] + dx

    row = pl.BlockSpec((tm, d), lambda i: (i, 0))
    vec = pl.BlockSpec((1, d), lambda i: (0, 0))
    dx, dg = _call(
        body, name, (t // tm,),
        [row, vec, row, row],
        [row, vec],
        [jax.ShapeDtypeStruct((t, d), F32), jax.ShapeDtypeStruct((1, d), F32)],
        vmem_bytes=10 * _nbytes((tm, d), F32),
    )(x, g.reshape(1, d), dh, dx_up)
    return dx, dg.reshape(d)


def _mm(name, dims, a, b, *, grid, a_spec, b_spec, out_shape, out_spec, acc_shape,
        res=None, res_spec=None, scale=None):
    nk = grid[2]
    has_res = res is not None

    def body(*refs):
        refs = list(refs)
        a_ref = refs.pop(0)
        b_ref = refs.pop(0)
        r_ref = refs.pop(0) if has_res else None
        o_ref = refs.pop(0)
        part = lax.dot_general(a_ref[...].astype(BF16), b_ref[...].astype(BF16), dims,
                               preferred_element_type=F32)

        def finish(acc):
            if scale is not None:
                acc = acc * scale
            if has_res:
                acc = acc + r_ref[...]
            o_ref[...] = acc.astype(o_ref.dtype)

        if nk == 1:
            finish(part)
        else:
            acc_ref = refs.pop(0)
            k = pl.program_id(2)

            @pl.when(k == 0)
            def _():
                acc_ref[...] = part

            @pl.when(k > 0)
            def _():
                acc_ref[...] += part

            @pl.when(k == nk - 1)
            def _():
                finish(acc_ref[...])

    operands = [a, b]
    in_specs = [a_spec, b_spec]
    vmem = 2 * (_nbytes(a_spec.block_shape, a.dtype) + _nbytes(b_spec.block_shape, b.dtype)
                + _nbytes(out_spec.block_shape, out_shape.dtype))
    vmem += 3 * _nbytes(acc_shape, F32)
    if has_res:
        operands.append(res)
        in_specs.append(res_spec)
        vmem += 2 * _nbytes(res_spec.block_shape, res.dtype)
    scratch = [pltpu.VMEM(acc_shape, F32)] if nk > 1 else []
    return _call(body, name, grid, in_specs, out_spec, out_shape, scratch=scratch, vmem_bytes=vmem)(*operands)


def _mm_colblock(name, h, w_g, layer):
    t, k = h.shape
    nb = w_g.shape[3]
    tm = _tile(t, 2048)
    return _mm(
        name, NN, h, w_g, grid=(t // tm, N_DEV, 1),
        a_spec=pl.BlockSpec((tm, k), lambda i, j, kk: (i, 0)),
        b_spec=pl.BlockSpec((None, None, k, nb), lambda i, j, kk: (layer, j, 0, 0)),
        out_shape=jax.ShapeDtypeStruct((t, N_DEV * nb), BF16),
        out_spec=pl.BlockSpec((tm, nb), lambda i, j, kk: (i, j)), acc_shape=(tm, nb))


def _mm_natural(name, a, w, layer, *, res=None, out_dtype=F32, scale=None):
    t, k = a.shape
    n = w.shape[2]
    tm = _tile(t, 1024)
    tn = _tile(n, 512)
    res_spec = None if res is None else pl.BlockSpec((tm, tn), lambda i, j, kk: (i, j))
    return _mm(
        name, NN, a, w, grid=(t // tm, n // tn, 1),
        a_spec=pl.BlockSpec((tm, k), lambda i, j, kk: (i, 0)),
        b_spec=pl.BlockSpec((None, k, tn), lambda i, j, kk: (layer, 0, j)),
        out_shape=jax.ShapeDtypeStruct((t, n), out_dtype),
        out_spec=pl.BlockSpec((tm, tn), lambda i, j, kk: (i, j)),
        acc_shape=(tm, tn), res=res, res_spec=res_spec, scale=scale)


def _mm_down(name, act, w4, layer, res):
    nblk, t, kb = act.shape
    n = w4.shape[3]
    tm = _tile(t, 1024)

    def body(a_ref, b_ref, r_ref, o_ref):
        acc = r_ref[...]
        for u in range(nblk):
            acc = acc + jnp.dot(a_ref[u], b_ref[u], preferred_element_type=F32)
        o_ref[...] = acc

    row = pl.BlockSpec((tm, n), lambda i: (i, 0))
    return _call(
        body, name, (t // tm,),
        [pl.BlockSpec((nblk, tm, kb), lambda i: (0, i, 0)),
         pl.BlockSpec((None, nblk, kb, n), lambda i: (layer, 0, 0, 0)),
         row],
        row,
        jax.ShapeDtypeStruct((t, n), F32),
        vmem_bytes=2 * (_nbytes((nblk, tm, kb), BF16) + _nbytes((nblk, kb, n), BF16)) + 6 * _nbytes((tm, n), F32),
    )(act, w4, res)


def _mm_t_colblock_norm_bwd(name, dz, w_g, layer, x, g, dx_up, blocked_in=False):
    k = w_g.shape[2]
    nb = w_g.shape[3]
    t = x.shape[0]
    tm = _tile(t, 1024)
    per_step = 2
    n_steps = N_DEV // per_step
    if blocked_in:
        a_spec = pl.BlockSpec((per_step, tm, nb), lambda i, kk: (kk, i, 0))
    else:
        a_spec = pl.BlockSpec((tm, per_step * nb), lambda i, kk: (i, kk))

    def body(a_ref, b_ref, x_ref, g_ref, up_ref, dx_ref, dg_ref, acc_ref):
        i = pl.program_id(0)
        kk = pl.program_id(1)
        part = None
        for u in range(per_step):
            a = a_ref[u] if blocked_in else a_ref[:, u * nb:(u + 1) * nb]
            term = lax.dot_general(a.astype(BF16), b_ref[u].astype(BF16), NT, preferred_element_type=F32)
            part = term if part is None else part + term

        @pl.when(kk == 0)
        def _():
            acc_ref[...] = part

        @pl.when(kk > 0)
        def _():
            acc_ref[...] += part

        @pl.when((i == 0) & (kk == 0))
        def _():
            dg_ref[...] = jnp.zeros_like(dg_ref)

        @pl.when(kk == n_steps - 1)
        def _():
            dy = acc_ref[...]
            xf = x_ref[...]
            r = lax.rsqrt(jnp.mean(xf * xf, axis=-1, keepdims=True) + EPS)
            xhat = xf * r
            dxhat = dy * g_ref[...]
            dg_ref[...] += jnp.sum(dy * xhat, axis=0, keepdims=True)
            dx_ref[...] = up_ref[...] + r * (dxhat - xhat * jnp.mean(dxhat * xhat, axis=-1, keepdims=True))

    row = pl.BlockSpec((tm, k), lambda i, kk: (i, 0))
    vec = pl.BlockSpec((1, k), lambda i, kk: (0, 0))
    dx, dg = _call(
        body, name, (t // tm, n_steps),
        [a_spec, pl.BlockSpec((None, per_step, k, nb), lambda i, kk: (layer, kk, 0, 0)), row, vec, row],
        [row, vec],
        [jax.ShapeDtypeStruct((t, k), F32), jax.ShapeDtypeStruct((1, k), F32)],
        scratch=[pltpu.VMEM((tm, k), F32)],
        vmem_bytes=2 * per_step * (_nbytes((tm, nb), BF16) + _nbytes((k, nb), BF16)) + 10 * _nbytes((tm, k), F32),
    )(dz, w_g, x, g.reshape(1, k), dx_up)
    return dx, dg.reshape(k)


def _ffn_gate_up(name, h, w_g, layer):
    t, k = h.shape
    nb = w_g.shape[3]
    half = N_DEV // 2
    tm = _tile(t, 1024)

    def body(h_ref, wg_ref, wu_ref, dact_ref, act_ref):
        hb = h_ref[...]
        gate = jnp.dot(hb, wg_ref[...], preferred_element_type=F32)
        up = jnp.dot(hb, wu_ref[...], preferred_element_type=F32)
        sig = _sigmoid(gate)
        silu = gate * sig
        dact_ref[0] = (up * (sig * (1.0 + gate * (1.0 - sig)))).astype(BF16)
        dact_ref[1] = silu.astype(BF16)
        act_ref[...] = (silu * up).astype(BF16)

    return _call(
        body, name, (t // tm, half),
        [pl.BlockSpec((tm, k), lambda i, j: (i, 0)),
         pl.BlockSpec((None, None, k, nb), lambda i, j: (layer, j, 0, 0)),
         pl.BlockSpec((None, None, k, nb), lambda i, j: (layer, half + j, 0, 0))],
        [pl.BlockSpec((2, None, tm, nb), lambda i, j: (0, j, i, 0)),
         pl.BlockSpec((None, tm, nb), lambda i, j: (j, i, 0))],
        [jax.ShapeDtypeStruct((2, half, t, nb), BF16), jax.ShapeDtypeStruct((half, t, nb), BF16)],
        vmem_bytes=2 * (_nbytes((tm, k), BF16) + 2 * _nbytes((k, nb), BF16) + 3 * _nbytes((tm, nb), BF16))
        + 8 * _nbytes((tm, nb), F32),
    )(h, w_g, w_g)


def _ffn_down_dx(name, dy, w4, layer, dact):
    t, n = dy.shape
    nblk, kb = w4.shape[1], w4.shape[2]
    tm = _tile(t, 1024)

    def body(dy_ref, w_ref, dact_ref, dgu_ref):
        da = lax.dot_general(dy_ref[...].astype(BF16), w_ref[...], NT, preferred_element_type=F32)
        dgu_ref[0] = (da * dact_ref[0].astype(F32)).astype(BF16)
        dgu_ref[1] = (da * dact_ref[1].astype(F32)).astype(BF16)

    blk = pl.BlockSpec((2, None, tm, kb), lambda i, j: (0, j, i, 0))
    return _call(
        body, name, (t // tm, nblk),
        [pl.BlockSpec((tm, n), lambda i, j: (i, 0)),
         pl.BlockSpec((None, None, kb, n), lambda i, j: (layer, j, 0, 0)),
         blk],
        blk,
        jax.ShapeDtypeStruct((2, nblk, t, kb), BF16),
        vmem_bytes=2 * (_nbytes((tm, n), F32) + _nbytes((kb, n), BF16) + 4 * _nbytes((tm, kb), BF16))
        + 8 * _nbytes((tm, kb), F32),
    )(dy, w4, dact)


def _mm_t_natural(name, dy, w, layer):
    t, n = dy.shape
    k = w.shape[1]
    tm = _tile(t, 1024)
    tk = _tile(k, 512)
    return _mm(
        name, NT, dy, w, grid=(t // tm, k // tk, 1),
        a_spec=pl.BlockSpec((tm, n), lambda i, j, kk: (i, 0)),
        b_spec=pl.BlockSpec((None, tk, n), lambda i, j, kk: (layer, j, 0)),
        out_shape=jax.ShapeDtypeStruct((t, k), BF16),
        out_spec=pl.BlockSpec((tm, tk), lambda i, j, kk: (i, j)),
        acc_shape=(tm, tk))


def _mm_dw_colblock(name, h, dz, blocked_in=False, transposed=False):
    t, k = h.shape
    nb = dz.shape[2] if blocked_in else dz.shape[1] // N_DEV
    tk = _tile(t, 2048)
    h_spec = pl.BlockSpec((tk, k), lambda i, j, kk: (kk, 0))
    if blocked_in:
        dz_spec = pl.BlockSpec((None, tk, nb), lambda i, j, kk: (j, kk, 0))
    else:
        dz_spec = pl.BlockSpec((tk, nb), lambda i, j, kk: (kk, j))
    rows, cols = (nb, k) if transposed else (k, nb)
    return _mm(
        name, TN, *((dz, h) if transposed else (h, dz)), grid=(1, N_DEV, t // tk),
        a_spec=dz_spec if transposed else h_spec,
        b_spec=h_spec if transposed else dz_spec,
        out_shape=jax.ShapeDtypeStruct((N_DEV, rows, cols), BF16),
        out_spec=pl.BlockSpec((None, rows, cols), lambda i, j, kk: (j, 0, 0)),
        acc_shape=(rows, cols))


def _mm_dw_natural(name, a, dy):
    t, k = a.shape
    n = dy.shape[1]
    tko = _tile(k, 1024)
    tt = _tile(t, 2048)
    out = _mm(
        name, TN, a, dy, grid=(k // tko, 1, t // tt),
        a_spec=pl.BlockSpec((tt, tko), lambda i, j, kk: (kk, i)),
        b_spec=pl.BlockSpec((tt, n), lambda i, j, kk: (kk, 0)),
        out_shape=jax.ShapeDtypeStruct((k, n), BF16),
        out_spec=pl.BlockSpec((tko, n), lambda i, j, kk: (i, 0)),
        acc_shape=(tko, n))
    return out.reshape(N_DEV, k // N_DEV, n)


def _mm_dw_down(name, act, dy):
    nblk, t, kb = act.shape
    n = dy.shape[1]
    tt = _tile(t, 2048)
    out = _mm(
        name, TN, act, dy, grid=(nblk, 1, t // tt),
        a_spec=pl.BlockSpec((None, tt, kb), lambda i, j, kk: (i, kk, 0)),
        b_spec=pl.BlockSpec((tt, n), lambda i, j, kk: (kk, 0)),
        out_shape=jax.ShapeDtypeStruct((nblk, kb, n), BF16),
        out_spec=pl.BlockSpec((None, kb, n), lambda i, j, kk: (i, 0, 0)),
        acc_shape=(kb, n))
    return out.reshape(N_DEV, (nblk * kb) // N_DEV, n)


def _spatial_mask(transposed=False):
    r = lax.broadcasted_iota(jnp.int32, (A_CHUNK, A_CHUNK), 0) // CHUNK
    c = lax.broadcasted_iota(jnp.int32, (A_CHUNK, A_CHUNK), 1) // CHUNK
    return c >= r if transposed else r >= c


def _sgu_tile(t):
    return _tile(t, 2 * A_CHUNK)


def _sgu_fwd(zpre, g_sgu, w_sp, b_full, name):
    t, f2 = zpre.shape
    f = f2 // 2
    gd = f // A_GROUPS
    tm = _sgu_tile(t)

    def body(z_ref, g_ref, w_ref, b_ref, p_ref, zs_ref, dg_ref):
        mask = _spatial_mask()
        wm = [jnp.where(mask, w_ref[g], 0.0).astype(BF16) for g in range(A_GROUPS)]
        for c in range(tm // A_CHUNK):
            rows = pl.ds(c * A_CHUNK, A_CHUNK)
            z, dgelu = _gelu_and_grad(z_ref[rows, :].astype(F32))
            zs_ref[rows, :] = z.astype(BF16)
            dg_ref[rows, :] = dgelu.astype(BF16)
            u = z[:, :f]
            v0 = z[:, f:]
            r = lax.rsqrt(jnp.mean(v0 * v0, axis=-1, keepdims=True) + EPS)
            v1 = (v0 * r * g_ref[...]).astype(BF16)
            for g in range(A_GROUPS):
                cols = slice(g * gd, (g + 1) * gd)
                v2 = jnp.dot(wm[g], v1[:, cols], preferred_element_type=F32) + b_ref[:, cols]
                p_ref[rows, cols] = (u[:, cols] * v2).astype(BF16)

    return _call(
        body, name, (t // tm,),
        [pl.BlockSpec((tm, f2), lambda i: (i, 0)),
         pl.BlockSpec((1, f), lambda i: (0, 0)),
         pl.BlockSpec((A_GROUPS, A_CHUNK, A_CHUNK), lambda i: (0, 0, 0)),
         pl.BlockSpec((A_CHUNK, f), lambda i: (0, 0))],
        [pl.BlockSpec((tm, f), lambda i: (i, 0)), pl.BlockSpec((tm, f2), lambda i: (i, 0)),
         pl.BlockSpec((tm, f2), lambda i: (i, 0))],
        [jax.ShapeDtypeStruct((t, f), BF16), jax.ShapeDtypeStruct((t, f2), BF16), jax.ShapeDtypeStruct((t, f2), BF16)],
        vmem_bytes=6 * _nbytes((tm, f2), BF16) + 2 * _nbytes((tm, f), BF16) + 8 * _nbytes((A_CHUNK, f2), F32),
    )(zpre, g_sgu.reshape(1, f), w_sp, b_full)


def _sgu_bwd(zs, dgs, dp, g_sgu, w_sp, w_sp_t, b_full, name):
    t, f2 = zs.shape
    f = f2 // 2
    gd = f // A_GROUPS
    tm = _sgu_tile(t)
    n_steps = t // tm

    def body(z_ref, dgelu_ref, dp_ref, g_ref, w_ref, wt_ref, b_ref, dz_ref, dw_ref, db_ref, dg_ref, dv1_ref, dbf_ref):
        step = pl.program_id(0)

        @pl.when(step == 0)
        def _():
            dw_ref[...] = jnp.zeros_like(dw_ref)
            dg_ref[...] = jnp.zeros_like(dg_ref)
            dbf_ref[...] = jnp.zeros_like(dbf_ref)

        mask = _spatial_mask()
        mask_t = _spatial_mask(transposed=True)
        wm = [jnp.where(mask, w_ref[g], 0.0).astype(BF16) for g in range(A_GROUPS)]
        wmt = [jnp.where(mask_t, wt_ref[g], 0.0).astype(BF16) for g in range(A_GROUPS)]
        gain = g_ref[...]
        for c in range(tm // A_CHUNK):
            rows = pl.ds(c * A_CHUNK, A_CHUNK)
            z = z_ref[rows, :].astype(F32)
            dgelu = dgelu_ref[rows, :].astype(F32)
            u = z[:, :f]
            v0 = z[:, f:]
            r = lax.rsqrt(jnp.mean(v0 * v0, axis=-1, keepdims=True) + EPS)
            xhat = v0 * r
            v1 = (xhat * gain).astype(BF16)
            dpf = dp_ref[rows, :].astype(F32)
            for g in range(A_GROUPS):
                cols = slice(g * gd, (g + 1) * gd)
                v1g = v1[:, cols]
                v2 = jnp.dot(wm[g], v1g, preferred_element_type=F32) + b_ref[:, cols]
                dpg = dpf[:, cols]
                dz_ref[rows, cols] = (dpg * v2 * dgelu[:, cols]).astype(BF16)
                dv2 = dpg * u[:, cols]
                dbf_ref[:, cols] += dv2
                dv2b = dv2.astype(BF16)
                dwg = lax.dot_general(dv2b, v1g, NT, preferred_element_type=F32)
                dw_ref[g] += jnp.where(mask, dwg, 0.0)
                dv1_ref[:, cols] = jnp.dot(wmt[g], dv2b, preferred_element_type=F32)
            dv1 = dv1_ref[...]
            dxhat = dv1 * gain
            dg_ref[...] += jnp.sum(dv1 * xhat, axis=0, keepdims=True)
            dv0 = r * (dxhat - xhat * jnp.mean(dxhat * xhat, axis=-1, keepdims=True))
            dz_ref[rows, pl.ds(f, f)] = (dv0 * dgelu[:, f:]).astype(BF16)

        @pl.when(step == n_steps - 1)
        def _():
            for g in range(A_GROUPS):
                db_ref[g] = jnp.sum(dbf_ref[:, g * gd:(g + 1) * gd], axis=1, keepdims=True)

    wspec = pl.BlockSpec((A_GROUPS, A_CHUNK, A_CHUNK), lambda i: (0, 0, 0))
    dz, dw, db, dg = _call(
        body, name, (n_steps,),
        [pl.BlockSpec((tm, f2), lambda i: (i, 0)),
         pl.BlockSpec((tm, f2), lambda i: (i, 0)),
         pl.BlockSpec((tm, f), lambda i: (i, 0)),
         pl.BlockSpec((1, f), lambda i: (0, 0)),
         wspec, wspec,
         pl.BlockSpec((A_CHUNK, f), lambda i: (0, 0))],
        [pl.BlockSpec((tm, f2), lambda i: (i, 0)),
         wspec,
         pl.BlockSpec((A_GROUPS, A_CHUNK, 1), lambda i: (0, 0, 0)),
         pl.BlockSpec((1, f), lambda i: (0, 0))],
        [jax.ShapeDtypeStruct((t, f2), BF16),
         jax.ShapeDtypeStruct((A_GROUPS, A_CHUNK, A_CHUNK), F32),
         jax.ShapeDtypeStruct((A_GROUPS, A_CHUNK, 1), F32),
         jax.ShapeDtypeStruct((1, f), F32)],
        scratch=[pltpu.VMEM((A_CHUNK, f), F32), pltpu.VMEM((A_CHUNK, f), F32)],
        vmem_bytes=6 * _nbytes((tm, f2), BF16) + 2 * _nbytes((tm, f), BF16) + 12 * _nbytes((A_CHUNK, f2), F32),
    )(zs, dgs, dp, g_sgu.reshape(1, f), w_sp, w_sp_t, b_full)
    return dz, dw, db.reshape(A_GROUPS, A_CHUNK), dg.reshape(f)


def _pair_valid(qi, col):
    qc = qi // CHUNK
    kc = col // CHUNK
    return (kc >= qc) & (kc <= qc + N_LEFT_CHUNKS)


def _diagonal_onehot():
    e = lax.broadcasted_iota(jnp.int32, (REL_PAD, DIAGONALS), 1)
    idx = jnp.clip(PAIR_BAND - 1 - e, -MAX_REL, MAX_REL) + MAX_REL
    r = lax.broadcasted_iota(jnp.int32, (REL_PAD, DIAGONALS), 0)
    return jnp.where(r == idx, 1.0, 0.0).astype(BF16)


def _bias_build(table, name):
    h = table.shape[0]
    tab = jnp.pad(table, ((0, 0), (0, REL_PAD - N_REL)))

    def body(t_ref, o_ref):
        oh = _diagonal_onehot()
        diag = jnp.zeros((h, DIAGONALS), F32)
        for piece in _split3(t_ref[...]):
            diag += jnp.dot(piece, oh, preferred_element_type=F32)
        col = lax.broadcasted_iota(jnp.int32, (h, PAIR_BAND), 1)
        for qi in range(PAIR_ROWS):
            row = pltpu.roll(diag, (qi - (PAIR_ROWS - 1)) % DIAGONALS, 1)[:, :PAIR_BAND]
            o_ref[qi] = jnp.where(_pair_valid(qi, col), row, NEG_INF)

    out = _call(
        body, name, (1,),
        [pl.BlockSpec((h, REL_PAD), lambda i: (0, 0))],
        pl.BlockSpec((PAIR_ROWS, h, PAIR_BAND), lambda i: (0, 0, 0)),
        jax.ShapeDtypeStruct((PAIR_ROWS, h, PAIR_BAND), F32),
        vmem_bytes=4 * _nbytes((PAIR_ROWS, h, PAIR_BAND), F32),
    )(tab)
    return jnp.transpose(out, (1, 0, 2))


def _bias_block(pair_bias):
    rest = K_BLOCK - PAIR_BAND
    return jnp.concatenate(
        [jnp.pad(pair_bias, ((0, 0), (0, 0), (p * PAIR_ROWS, rest - p * PAIR_ROWS)), constant_values=NEG_INF)
         for p in range(PAIRS_PER_BLOCK)], axis=1)


def _bias_grad(dbias, name):
    h = dbias.shape[0]
    db_t = jnp.transpose(dbias, (1, 0, 2))

    def body(d_ref, o_ref):
        diag = jnp.zeros((h, DIAGONALS), F32)
        for qi in range(PAIR_ROWS):
            diag += pltpu.roll(d_ref[qi], PAIR_ROWS - 1 - qi, 1)
        oh = _diagonal_onehot()
        acc = jnp.zeros((h, REL_PAD), F32)
        for piece in _split3(diag):
            acc += lax.dot_general(piece, oh, NT, preferred_element_type=F32)
        o_ref[...] = acc

    out = _call(
        body, name, (1,),
        [pl.BlockSpec((PAIR_ROWS, h, DIAGONALS), lambda i: (0, 0, 0))],
        pl.BlockSpec((h, REL_PAD), lambda i: (0, 0)),
        jax.ShapeDtypeStruct((h, REL_PAD), F32),
        vmem_bytes=4 * _nbytes((PAIR_ROWS, h, DIAGONALS), F32),
    )(db_t)
    return out[:, :N_REL]


def _head_masks():
    lane = lax.broadcasted_iota(jnp.int32, (Q_BLOCK, HEAD_PAIR), 1)
    return lane < HEAD_DIM, lane >= HEAD_DIM


def _block_scores(qm, kb, bias, valid):
    s = lax.dot_general(qm, kb, NT, preferred_element_type=F32) + bias
    return s if valid is None else jnp.where(valid, s, NEG_INF)


def _softmax_rows(s):
    e = jnp.exp(s - jnp.max(s, axis=-1, keepdims=True))
    return e * (1.0 / jnp.sum(e, axis=-1, keepdims=True))


def _padded_then_plain(step, n_blocks):
    n_padded = min(LEFT // Q_BLOCK, n_blocks)
    lax.fori_loop(0, n_padded, lambda j, c: step(j, c, True), 0)
    lax.fori_loop(n_padded, n_blocks, lambda j, c: step(j, c, False), 0, unroll=ATTN_UNROLL)


def _attn_fwd(q, kvpad, bias, name):
    t, d = q.shape
    n_pairs = d // HEAD_PAIR
    n_blocks = t // Q_BLOCK

    def body(q_ref, k_ref, v_ref, b_ref, o_ref):
        masks = _head_masks()
        key = lax.broadcasted_iota(jnp.int32, (Q_BLOCK, K_BLOCK), 1)

        def step(j, carry, padded):
            r0 = pl.multiple_of(j * Q_BLOCK, Q_BLOCK)
            q2 = q_ref[pl.ds(r0, Q_BLOCK), :].astype(F32)
            kb = k_ref[pl.ds(r0, K_BLOCK), :]
            vb = v_ref[pl.ds(r0, K_BLOCK), :]
            valid = key >= LEFT - j * Q_BLOCK if padded else None
            scores = [_block_scores(jnp.where(masks[a], q2, 0.0).astype(BF16), kb, b_ref[a], valid) for a in range(2)]
            probs = [_softmax_rows(s).astype(BF16) for s in scores]
            outs = [jnp.dot(p, vb, preferred_element_type=F32) for p in probs]
            o_ref[pl.ds(r0, Q_BLOCK), :] = jnp.where(masks[0], outs[0], outs[1]).astype(BF16)
            return carry

        _padded_then_plain(step, n_blocks)

    return _call(
        body, name, (n_pairs,),
        [pl.BlockSpec((t, HEAD_PAIR), lambda p: (0, p)),
         pl.BlockSpec((LEFT + t, HEAD_PAIR), lambda p: (0, p)),
         pl.BlockSpec((LEFT + t, HEAD_PAIR), lambda p: (0, n_pairs + p)),
         pl.BlockSpec((2, Q_BLOCK, K_BLOCK), lambda p: (p, 0, 0))],
        pl.BlockSpec((t, HEAD_PAIR), lambda p: (0, p)),
        jax.ShapeDtypeStruct((t, d), BF16),
        vmem_bytes=8 * _nbytes((LEFT + t, HEAD_PAIR), BF16) + 12 * _nbytes((2, Q_BLOCK, K_BLOCK), F32),
    )(q, kvpad, kvpad, bias)


def _attn_bwd(q, kvpad, bias, do, dk_in, dv_in, name):
    t, d = q.shape
    n_pairs = d // HEAD_PAIR
    n_blocks = t // Q_BLOCK
    has_in = dk_in is not None

    def body(*refs):
        refs = list(refs)
        q_ref, k_ref, v_ref, b_ref, do_ref = refs[:5]
        refs = refs[5:]
        if has_in:
            dki_ref, dvi_ref = refs[:2]
            refs = refs[2:]
        dq_ref, dk_ref, dv_ref, db_ref = refs
        masks = _head_masks()
        key = lax.broadcasted_iota(jnp.int32, (Q_BLOCK, K_BLOCK), 1)
        if has_in:
            dk_ref[...] = dki_ref[...]
            dv_ref[...] = dvi_ref[...]
        else:
            dk_ref[...] = jnp.zeros_like(dk_ref)
            dv_ref[...] = jnp.zeros_like(dv_ref)
        db_ref[...] = jnp.zeros_like(db_ref)

        def step(j, carry, padded):
            r0 = pl.multiple_of(j * Q_BLOCK, Q_BLOCK)
            q2 = q_ref[pl.ds(r0, Q_BLOCK), :].astype(F32)
            do2 = do_ref[pl.ds(r0, Q_BLOCK), :].astype(F32)
            kb = k_ref[pl.ds(r0, K_BLOCK), :]
            vb = v_ref[pl.ds(r0, K_BLOCK), :]
            valid = key >= LEFT - j * Q_BLOCK if padded else None
            heads = range(2)
            qms = [jnp.where(masks[a], q2, 0.0).astype(BF16) for a in heads]
            doms = [jnp.where(masks[a], do2, 0.0).astype(BF16) for a in heads]
            scores = [_block_scores(qms[a], kb, b_ref[a], valid) for a in heads]
            dps = [lax.dot_general(doms[a], vb, NT, preferred_element_type=F32) for a in heads]
            ps = [_softmax_rows(s) for s in scores]
            dss = [ps[a] * (dps[a] - jnp.sum(dps[a] * ps[a], axis=-1, keepdims=True)) for a in heads]
            for a in heads:
                for pair in range(PAIRS_PER_BLOCK):
                    lo = pair * PAIR_ROWS
                    db_ref[a, :, pl.ds(0, PAIR_BAND)] += dss[a][lo:lo + PAIR_ROWS, lo:lo + PAIR_BAND]
            dsbs = [ds.astype(BF16) for ds in dss]
            pbs = [p.astype(BF16) for p in ps]
            dqs = [jnp.dot(dsbs[a], kb, preferred_element_type=F32) for a in heads]
            dk_acc = sum(lax.dot_general(dsbs[a], qms[a], TN, preferred_element_type=F32) for a in heads)
            dv_acc = sum(lax.dot_general(pbs[a], doms[a], TN, preferred_element_type=F32) for a in heads)
            dq = jnp.where(masks[0], dqs[0], dqs[1]) * ATTN_SCALE
            dq_ref[pl.ds(r0, Q_BLOCK), :] = dq.astype(BF16)
            dk_ref[pl.ds(r0, K_BLOCK), :] += dk_acc
            dv_ref[pl.ds(r0, K_BLOCK), :] += dv_acc
            return carry

        _padded_then_plain(step, n_blocks)

    q_spec = pl.BlockSpec((t, HEAD_PAIR), lambda p: (0, p))
    kv_spec = pl.BlockSpec((LEFT + t, HEAD_PAIR), lambda p: (0, p))
    operands = [q, kvpad, kvpad, bias, do]
    in_specs = [q_spec, kv_spec, pl.BlockSpec((LEFT + t, HEAD_PAIR), lambda p: (0, n_pairs + p)),
                pl.BlockSpec((2, Q_BLOCK, K_BLOCK), lambda p: (p, 0, 0)), q_spec]
    aliases = None
    if has_in:
        operands += [dk_in, dv_in]
        in_specs += [kv_spec, kv_spec]
        aliases = {5: 1, 6: 2}
    return _call(
        body, name, (n_pairs,),
        in_specs,
        [q_spec, kv_spec, kv_spec, pl.BlockSpec((2, PAIR_ROWS, DIAGONALS), lambda p: (p, 0, 0))],
        [jax.ShapeDtypeStruct((t, d), BF16),
         jax.ShapeDtypeStruct((LEFT + t, d), F32),
         jax.ShapeDtypeStruct((LEFT + t, d), F32),
         jax.ShapeDtypeStruct((d // HEAD_DIM, PAIR_ROWS, DIAGONALS), F32)],
        vmem_bytes=10 * _nbytes((LEFT + t, HEAD_PAIR), BF16) + 8 * _nbytes((LEFT + t, HEAD_PAIR), F32)
        + 16 * _nbytes((2, Q_BLOCK, K_BLOCK), F32),
        aliases=aliases,
    )(*operands)


def _loss_head(x, g, target, name):
    t, d = x.shape
    tm = _tile(t, 512)

    def body(x_ref, g_ref, t_ref, dx_ref, loss_ref, dg_ref):
        @pl.when(pl.program_id(0) == 0)
        def _():
            loss_ref[...] = jnp.zeros_like(loss_ref)
            dg_ref[...] = jnp.zeros_like(dg_ref)

        xf = x_ref[...]
        r = lax.rsqrt(jnp.mean(xf * xf, axis=-1, keepdims=True) + EPS)
        xhat = xf * r
        diff = xhat * g_ref[...] - t_ref[...]
        row_loss = jnp.mean(diff * diff, axis=-1, keepdims=True)
        loss_ref[...] += 0.5 * jnp.sum(row_loss, axis=0, keepdims=True)
        dy = diff * (1.0 / d)
        dg_ref[...] += jnp.sum(dy * xhat, axis=0, keepdims=True)
        dxhat = dy * g_ref[...]
        dx_ref[...] = r * (dxhat - xhat * jnp.mean(dxhat * xhat, axis=-1, keepdims=True))

    row = pl.BlockSpec((tm, d), lambda i: (i, 0))
    vec = pl.BlockSpec((1, d), lambda i: (0, 0))
    dx, loss, dg = _call(
        body, name, (t // tm,),
        [row, vec, row],
        [row, pl.BlockSpec((1, 1), lambda i: (0, 0)), vec],
        [jax.ShapeDtypeStruct((t, d), F32), jax.ShapeDtypeStruct((1, 1), F32), jax.ShapeDtypeStruct((1, d), F32)],
        vmem_bytes=10 * _nbytes((tm, d), F32),
    )(x, g.reshape(1, d), target)
    return dx, loss[0, 0], dg.reshape(d)


def _adamw_store(g, w_ref, m_ref, v_ref, g_ref, d_ref, nm_ref, nv_ref):
    c1 = 1.0 / (1.0 - ADAM_B1 ** ADAM_STEP)
    c2 = 1.0 / (1.0 - ADAM_B2 ** ADAM_STEP)
    nm = ADAM_B1 * m_ref[...] + (1.0 - ADAM_B1) * g
    nv = ADAM_B2 * v_ref[...] + (1.0 - ADAM_B2) * (g * g)
    g_ref[...] = g
    nm_ref[...] = nm
    nv_ref[...] = nv
    d_ref[...] = -ADAM_LR * ((nm * c1) / (jnp.sqrt(nv * c2) + ADAM_EPS) + ADAM_WD * w_ref[...])


def _adamw_layer(recv, own, w, m, v, layer, prev, me, name):
    n_src, r, c = recv.shape
    tr = _row_tile(r, max(BF16_SUBLANES, ADAMW_BLOCK_ELEMS // c), BF16_SUBLANES)
    first = prev is None

    def body(me_ref, recv_ref, own_ref, w_ref, m_ref, v_ref, *rest):
        mine = me_ref[0]
        own_part = own_ref[...].astype(F32)
        g = None
        for s in range(n_src):
            part = jnp.where(mine == s, own_part, recv_ref[s].astype(F32))
            g = part if g is None else g + part
        _adamw_store(g, w_ref, m_ref, v_ref, *rest[-4:])

    blk = pl.BlockSpec((None, tr, c), lambda i, me_ref: (layer, i, 0))
    any_spec = pl.BlockSpec(memory_space=pl.ANY)
    out = jax.ShapeDtypeStruct(w.shape, F32)
    operands = [me, recv, own, w, m, v] + ([] if first else list(prev))
    vmem = 2 * _nbytes((n_src + 1, tr, c), BF16) + 18 * _nbytes((tr, c), F32)
    return pl.pallas_call(
        body,
        name=name,
        grid_spec=pltpu.PrefetchScalarGridSpec(
            num_scalar_prefetch=1,
            grid=(r // tr,),
            in_specs=[pl.BlockSpec((n_src, tr, c), lambda i, me_ref: (0, i, 0)),
                      pl.BlockSpec((None, tr, c), lambda i, me_ref: (me_ref[0], i, 0)),
                      blk, blk, blk] + ([] if first else [any_spec] * 4),
            out_specs=[blk, blk, blk, blk],
        ),
        out_shape=[out, out, out, out],
        input_output_aliases={} if first else {6 + j: j for j in range(4)},
        compiler_params=pltpu.CompilerParams(
            dimension_semantics=("arbitrary",),
            vmem_limit_bytes=int(min(max(VMEM_FLOOR_BYTES, vmem * 5 // 4), VMEM_CEIL_BYTES))),
    )(*operands)


def _adamw(parts, w, m, v, name):
    n_layers, n_src, r, c = parts.shape
    mult = BF16_SUBLANES if parts.dtype == BF16 else F32_SUBLANES
    tr = _row_tile(r, max(mult, ADAMW_BLOCK_ELEMS // c), mult)

    def body(p_ref, w_ref, m_ref, v_ref, g_ref, d_ref, nm_ref, nv_ref):
        g = p_ref[0].astype(F32)
        for s in range(1, n_src):
            g = g + p_ref[s].astype(F32)
        _adamw_store(g, w_ref, m_ref, v_ref, g_ref, d_ref, nm_ref, nv_ref)

    blk = pl.BlockSpec((None, tr, c), lambda l, i: (l, i, 0))
    out = jax.ShapeDtypeStruct((n_layers, r, c), F32)
    return _call(
        body, name, (n_layers, r // tr),
        [pl.BlockSpec((None, n_src, tr, c), lambda l, i: (l, 0, i, 0)), blk, blk, blk],
        [blk, blk, blk, blk],
        [out, out, out, out],
        vmem_bytes=2 * _nbytes((n_src, tr, c), parts.dtype) + 18 * _nbytes((tr, c), F32),
    )(parts, w, m, v)


def _ordered_sum(parts, name):
    n_src, r, c = parts.shape

    def body(p_ref, o_ref):
        acc = p_ref[0]
        for s in range(1, n_src):
            acc = acc + p_ref[s]
        o_ref[...] = acc

    return _call(
        body, name, (1,),
        [pl.BlockSpec((n_src, r, c), lambda i: (0, 0, 0))],
        pl.BlockSpec((r, c), lambda i: (0, 0)),
        jax.ShapeDtypeStruct((r, c), F32),
        vmem_bytes=4 * _nbytes((n_src, r, c), F32),
    )(parts)


def _position():
    return lax.axis_index("x"), lax.axis_index("y"), lax.axis_index("c")


def _linear(p):
    return 4 * p[0] + 2 * p[1] + p[2]


def _all_gather(shards, name):
    n = len(shards)

    def body(*refs):
        ins, outs = refs[:n], refs[n:2 * n]
        send_sems, recv_sems, local_sems = refs[2 * n:]
        x, y, c = _position()
        me, sibling = (x, y, c), (x, y, 1 - c)
        chips = [(1 - x, y), (x, 1 - y), (1 - x, 1 - y)]

        def slab(t, p):
            return outs[t].at[:, _linear(p)]

        def copy(t, k, block, to, src=None):
            return pltpu.make_async_remote_copy(
                src_ref=slab(t, block) if src is None else src,
                dst_ref=slab(t, block),
                send_sem=send_sems.at[t, k],
                recv_sem=recv_sems.at[t, k],
                device_id=to,
                device_id_type=MESH,
            )

        started = []
        for t in range(n):
            mine = pltpu.make_async_copy(ins[t], slab(t, me), local_sems.at[t])
            mine.start()
            started.append(mine)
        sends = []
        for t in range(n):
            first = [copy(t, 0, me, sibling, src=ins[t])]
            first += [copy(t, 1 + j, me, (*chip, c), src=ins[t]) for j, chip in enumerate(chips)]
            for cp in first:
                cp.start()
            sends += first
        for t in range(n):
            for j, chip in enumerate(chips):
                copy(t, 1 + j, (*chip, c), me).wait_recv()
                passed = copy(t, 4 + j, (*chip, c), sibling)
                passed.start()
                sends.append(passed)
        for t in range(n):
            copy(t, 0, sibling, me).wait_recv()
            for j, chip in enumerate(chips):
                copy(t, 4 + j, (*chip, 1 - c), me).wait_recv()
        for cp in sends:
            cp.wait_send()
        for mine in started:
            mine.wait()

    out_shape = [jax.ShapeDtypeStruct((s.shape[0], N_DEV) + s.shape[1:], s.dtype) for s in shards]
    return pl.pallas_call(
        body,
        name=name,
        in_specs=[HBM_SPEC] * n,
        out_specs=[HBM_SPEC] * n,
        out_shape=out_shape,
        scratch_shapes=[
            pltpu.SemaphoreType.DMA((n, N_DEV - 1)),
            pltpu.SemaphoreType.DMA((n, N_DEV - 1)),
            pltpu.SemaphoreType.DMA((n,)),
        ],
    )(*shards)


def _exchange(blocks, name):
    n = len(blocks)

    def body(*refs):
        ins, outs = refs[:n], refs[n:2 * n]
        send_sems, recv_sems, local_sems = refs[2 * n:]
        x, y, c = _position()
        me = _linear((x, y, c))
        flips = [(fx, fy, fc) for fx in (0, 1) for fy in (0, 1) for fc in (0, 1)][1:]

        def peer_of(flip):
            fx, fy, fc = flip
            return (1 - x if fx else x, 1 - y if fy else y, 1 - c if fc else c)

        def copy(t, k, peer):
            return pltpu.make_async_remote_copy(
                src_ref=ins[t].at[:, _linear(peer)],
                dst_ref=outs[t].at[:, me],
                send_sem=send_sems.at[t, k],
                recv_sem=recv_sems.at[t, k],
                device_id=peer,
                device_id_type=MESH,
            )

        def arrival(t, k, peer):
            return pltpu.make_async_remote_copy(
                src_ref=ins[t].at[:, _linear(peer)],
                dst_ref=outs[t].at[:, _linear(peer)],
                send_sem=send_sems.at[t, k],
                recv_sem=recv_sems.at[t, k],
                device_id=peer,
                device_id_type=MESH,
            )

        own = []
        for t in range(n):
            cp = pltpu.make_async_copy(ins[t].at[:, me], outs[t].at[:, me], local_sems.at[t])
            cp.start()
            own.append(cp)
        sends = []
        for t in range(n):
            for k, flip in enumerate(flips):
                cp = copy(t, k, peer_of(flip))
                cp.start()
                sends.append(cp)
        for t in range(n):
            for k, flip in enumerate(flips):
                arrival(t, k, peer_of(flip)).wait_recv()
        for cp in sends:
            cp.wait_send()
        for cp in own:
            cp.wait()

    out_shape = [jax.ShapeDtypeStruct(b.shape, b.dtype) for b in blocks]
    return pl.pallas_call(
        body,
        name=name,
        in_specs=[HBM_SPEC] * n,
        out_specs=[HBM_SPEC] * n,
        out_shape=out_shape,
        scratch_shapes=[
            pltpu.SemaphoreType.DMA((n, N_DEV - 1)),
            pltpu.SemaphoreType.DMA((n, N_DEV - 1)),
            pltpu.SemaphoreType.DMA((n,)),
        ],
    )(*blocks)


def _peers():
    x, y, c = _position()
    flips = [(fx, fy, fc) for fx in (0, 1) for fy in (0, 1) for fc in (0, 1)][1:]
    return [(1 - x if fx else x, 1 - y if fy else y, 1 - c if fc else c) for fx, fy, fc in flips]


def _split_start(groups, carry, name, exchange=False):
    sizes = [len(srcs) for srcs, _ in groups]
    arrays = [a for srcs, lands in groups for a in list(srcs) + list(lands)] + [carry]

    def body(*refs):
        ins, sems = refs[:len(arrays)], refs[len(arrays):len(arrays) + 2 * len(groups)]
        me = _linear(_position())
        at = 0
        for g, n in enumerate(sizes):
            src_refs, land_refs = ins[at:at + n], ins[at + n:at + 2 * n]
            at += 2 * n
            for t in range(n):
                for k, peer in enumerate(_peers()):
                    pltpu.make_async_remote_copy(
                        src_ref=src_refs[t].at[_linear(peer)] if exchange else src_refs[t],
                        dst_ref=land_refs[t].at[me],
                        send_sem=sems[2 * g].at[t * (N_DEV - 1) + k],
                        recv_sem=sems[2 * g + 1].at[t * (N_DEV - 1) + k],
                        device_id=peer,
                        device_id_type=MESH,
                    ).start()

    sem_shapes = [pltpu.SemaphoreType.DMA((n * (N_DEV - 1),)) for n in sizes for _ in range(2)]
    out = pl.pallas_call(
        body,
        name=name,
        in_specs=[HBM_SPEC] * len(arrays),
        out_specs=[SEM_SPEC] * len(sem_shapes) + [HBM_SPEC] * len(arrays),
        out_shape=sem_shapes + [pltpu.HBM(a.shape, a.dtype) for a in arrays],
        input_output_aliases={i: len(sem_shapes) + i for i in range(len(arrays))},
        compiler_params=pltpu.CompilerParams(has_side_effects=pltpu.SideEffectType.DATAFLOW_SIDE_EFFECTING),
    )(*[pltpu.with_memory_space_constraint(a, pltpu.HBM) for a in arrays])
    sems, thru = out[:len(sem_shapes)], out[len(sem_shapes):]
    started, at = [], 0
    for g, n in enumerate(sizes):
        started.append((sems[2 * g], sems[2 * g + 1], thru[at:at + n], thru[at + n:at + 2 * n]))
        at += 2 * n
    return started, thru[-1]


def _split_wait(send_sems, recv_sems, srcs, lands, after, name, exchange=False):
    n = len(srcs)

    def body(*refs):
        src_refs, land_refs = refs[:n], refs[n:2 * n]
        send_ref, recv_ref = refs[2 * n], refs[2 * n + 1]
        for t in range(n):
            for k, peer in enumerate(_peers()):
                copy = pltpu.make_async_remote_copy(
                    src_ref=src_refs[t].at[0] if exchange else src_refs[t],
                    dst_ref=land_refs[t].at[0],
                    send_sem=send_ref.at[t * (N_DEV - 1) + k],
                    recv_sem=recv_ref.at[t * (N_DEV - 1) + k],
                    device_id=peer,
                    device_id_type=MESH,
                )
                copy.wait_send()
                copy.wait_recv()

    arrays = list(srcs) + list(lands)
    out = pl.pallas_call(
        body,
        name=name,
        in_specs=[HBM_SPEC] * len(arrays) + [SEM_SPEC, SEM_SPEC, pl.BlockSpec(memory_space=pl.ANY)],
        out_specs=[HBM_SPEC] * len(arrays),
        out_shape=[pltpu.HBM(a.shape, a.dtype) for a in arrays],
        input_output_aliases={i: i for i in range(len(arrays))},
        compiler_params=pltpu.CompilerParams(has_side_effects=pltpu.SideEffectType.DATAFLOW_SIDE_EFFECTING),
    )(*arrays, send_sems, recv_sems, after)
    return out[:n], out[n:]


def _pack(arrays, row_multiple):
    flat = jnp.concatenate([a.reshape(-1) for a in arrays])
    quantum = row_multiple * FLAT_LANES
    padded = -(-flat.shape[0] // quantum) * quantum
    return jnp.pad(flat, (0, padded - flat.shape[0])).reshape(-1, FLAT_LANES)


def _unpack(flat, like):
    flat = flat.reshape(-1)
    out, at = [], 0
    for a in like:
        size = math.prod(a.shape)
        out.append(flat[at:at + size].reshape(a.shape))
        at += size
    return out


def kernel(x, a_norm, a_w_in, a_sgu_norm, a_w_spatial, a_b_spatial, a_w_out, kv_norm, w_kv, b_norm, b_w_q, b_rel_bias, b_w_o, ffn_norm, ffn_w_gate_up, ffn_w_down, final_norm, loss_target, m_a_norm, m_a_w_in, m_a_sgu_norm, m_a_w_spatial, m_a_b_spatial, m_a_w_out, m_kv_norm, m_w_kv, m_b_norm, m_b_w_q, m_b_rel_bias, m_b_w_o, m_ffn_norm, m_ffn_w_gate_up, m_ffn_w_down, m_final_norm, v_a_norm, v_a_w_in, v_a_sgu_norm, v_a_w_spatial, v_a_b_spatial, v_a_w_out, v_kv_norm, v_w_kv, v_b_norm, v_b_w_q, v_b_rel_bias, v_b_w_o, v_ffn_norm, v_ffn_w_gate_up, v_ffn_w_down, v_final_norm):
    xs = x[0]
    target = loss_target[0]
    t, d = xs.shape
    n_a = a_w_in.shape[0]
    n_b = b_w_q.shape[0]
    depth = ffn_w_gate_up.shape[0]
    f_a = a_w_out.shape[1] * N_DEV
    gd = f_a // A_GROUPS
    nb_ffn = ffn_w_gate_up.shape[2]
    me = _linear(_position())

    small_rows = -(-(a_norm.size + a_sgu_norm.size) // (8 * 128)) * 8
    small = jnp.pad(jnp.concatenate([a_norm.reshape(-1), a_sgu_norm.reshape(-1)]),
                    (0, small_rows * 128 - a_norm.size - a_sgu_norm.size)).reshape(1, small_rows, 128)

    def shard(w, layer=None):
        return (w if layer is None else w[layer]).astype(BF16)

    stages = []
    for layer in range(depth):
        if layer == 0:
            stages += [("a0", [shard(a_w_in, 0)]), ("a0_out", [shard(a_w_out, 0)])]
        elif layer < n_a:
            stages.append((f"a{layer}", [shard(a_w_in, layer), shard(a_w_out, layer)]))
        else:
            i = layer - n_a
            shared = [shard(w_kv)] if i == 0 else []
            stages.append((f"b{i}", shared + [shard(b_w_q, i), shard(b_w_o, i)]))
        stages.append((f"f{layer}", [shard(ffn_w_gate_up, layer), shard(ffn_w_down, layer)]))
    first = _all_gather([s[None] for s in stages[0][1]] + [small], "gather_first")
    gathered = {stages[0][0]: [g[0] for g in first[:-1]]}
    small_g = first[-1].reshape(N_DEV, -1)
    a_norm_full = small_g[:, :a_norm.size].reshape(N_DEV, n_a, -1).transpose(1, 0, 2).reshape(n_a, d)
    a_sgu_full = small_g[:, a_norm.size:a_norm.size + a_sgu_norm.size].reshape(
        N_DEV, n_a, -1).transpose(1, 0, 2).reshape(n_a, f_a)
    later = [(shards, [lax.dynamic_update_slice(lax.empty((N_DEV,) + s.shape, BF16), s[None], (me, 0, 0))
                       for s in shards]) for _, shards in stages[1:]]
    started, a_norm_full = _split_start(later, a_norm_full, "gather_start")
    in_flight = {key: group for (key, _), group in zip(stages[1:], started)}

    def weights(key, after):
        if key not in gathered:
            _, gathered[key] = _split_wait(*in_flight.pop(key), after, f"gather_wait_{key}")
        return gathered[key]

    rows_down = ffn_w_down.shape[1]

    def mixer_a_weights(i, after):
        if i == 0:
            (w_in,), (w_out,) = weights("a0", after[0]), weights("a0_out", after[1])
        else:
            w_in, w_out = weights(f"a{i}", after[0])
        return w_in[None], w_out.reshape(1, f_a, d)

    def mixer_b_weights(i, after):
        ws = weights(f"b{i}", after)
        return ws[-2].reshape(1, d, d), ws[-1].reshape(1, d, d)

    def ffn_weights(layer, after):
        w_gu, w_dn = weights(f"f{layer}", after)
        return w_gu[None], w_dn.reshape(1, N_DEV // 2, 2 * rows_down, d)

    w_sp_t = jnp.swapaxes(a_w_spatial, -1, -2)
    b_full = jnp.repeat(jnp.swapaxes(a_b_spatial, -1, -2), gd, axis=-1)

    saved = []

    def ffn_fwd(xin, layer):
        hf = _rms_fwd(xin, ffn_norm[layer], f"ffn_norm_fwd_{layer}")
        w_gu, w_dn = ffn_weights(layer, xin)
        dact, act = _ffn_gate_up(f"ffn_gate_up_{layer}", hf, w_gu, 0)
        xout = _mm_down(f"ffn_down_{layer}", act, w_dn, 0, xin)
        return xout, (xin, hf, dact, act)

    for i in range(n_a):
        h = _rms_fwd(xs, a_norm_full[i], f"a_norm_fwd_{i}")
        zpre = _mm_colblock(f"a_in_{i}", h, weights(f"a{i}", xs)[0][None], 0)
        p, zs, dgs = _sgu_fwd(zpre, a_sgu_full[i], a_w_spatial[i], b_full[i], f"a_sgu_fwd_{i}")
        w_in, w_out = mixer_a_weights(i, (xs, p))
        x_mid = _mm_natural(f"a_out_{i}", p, w_out, 0, res=xs)
        x_out, ffn_saved = ffn_fwd(x_mid, i)
        saved.append((xs, h, zs, dgs, p, ffn_saved))
        xs = x_out

    x_kv = xs
    w_kv_g = weights("b0", x_kv)[0][None]
    h_kv = _rms_fwd(x_kv, kv_norm, "kv_norm_fwd")
    kv = _mm_colblock("kv_proj", h_kv, w_kv_g, 0)
    kvpad = jnp.pad(kv, ((LEFT, 0), (0, 0)))

    biases = [_bias_block(_bias_build(b_rel_bias[i], f"rel_bias_{i}")) for i in range(n_b)]
    for i in range(n_b):
        layer = n_a + i
        w_q, w_o = mixer_b_weights(i, xs)
        hb = _rms_fwd(xs, b_norm[i], f"b_norm_fwd_{i}")
        q = _mm_natural(f"b_q_{i}", hb, w_q, 0, out_dtype=BF16, scale=ATTN_SCALE)
        o = _attn_fwd(q, kvpad, biases[i], f"b_attn_fwd_{i}")
        x_mid = _mm_natural(f"b_o_{i}", o, w_o, 0, res=xs)
        x_out, ffn_saved = ffn_fwd(x_mid, layer)
        saved.append((xs, hb, q, o, ffn_saved))
        xs = x_out

    dx, loss_local, g_final = _loss_head(xs, final_norm, target, "loss_head")
    loss = lax.psum(loss_local, ("x", "y", "c"))

    big_grads = {}
    pending = []
    in_flight_grads = []

    def start_exchange(dx, tag):
        srcs = [big_grads[key] for key in pending]
        lands = [lax.empty(s.shape, BF16) for s in srcs]
        ((send, recv, srcs, lands),), dx = _split_start([(srcs, lands)], dx, f"exchange_start_{tag}", exchange=True)
        in_flight_grads.append((list(pending), send, recv, srcs, lands, tag))
        pending.clear()
        return dx

    g_ffn_norm = [None] * depth
    g_a_norm = [None] * n_a
    g_a_sgu = [None] * n_a
    g_w_sp = [None] * n_a
    g_b_sp = [None] * n_a
    g_b_norm = [None] * n_b
    g_rel = [None] * n_b

    def ffn_bwd(dx, layer, ffn_saved):
        eager = layer < n_a
        xin, hf, dact, act = ffn_saved
        big_grads["ffn_w_down", layer] = _mm_dw_down(f"ffn_down_dw_{layer}", act, dx)
        pending.append(("ffn_w_down", layer))
        if eager:
            dx = start_exchange(dx, f"f{layer}_down")
        w_gu, w_dn = ffn_weights(layer, xin)
        dgu = _ffn_down_dx(f"ffn_down_dx_{layer}", dx, w_dn, 0, dact).reshape(N_DEV, t, nb_ffn)
        big_grads["ffn_w_gate_up", layer] = _mm_dw_colblock(
            f"ffn_gate_up_dw_{layer}", hf, dgu, blocked_in=True, transposed=True)
        pending.append(("ffn_w_gate_up", layer))
        if eager:
            dx = start_exchange(dx, f"f{layer}_gate_up")
        dx, g_ffn_norm[layer] = _mm_t_colblock_norm_bwd(
            f"ffn_gate_up_dx_{layer}", dgu, w_gu, 0, xin, ffn_norm[layer], dx, blocked_in=True)
        return dx

    dk = dv = None
    for i in reversed(range(n_b)):
        layer = n_a + i
        x_in, hb, q, o, ffn_saved = saved[layer]
        dx = ffn_bwd(dx, layer, ffn_saved)
        big_grads["b_w_o", i] = _mm_dw_natural(f"b_o_dw_{i}", o, dx)
        w_q, w_o = mixer_b_weights(i, x_in)
        do = _mm_t_natural(f"b_o_dx_{i}", dx, w_o, 0)
        dq, dk, dv, dbias = _attn_bwd(q, kvpad, biases[i], do, dk, dv, f"b_attn_bwd_{i}")
        g_rel[i] = _bias_grad(dbias, f"rel_bias_grad_{i}")
        big_grads["b_w_q", i] = _mm_dw_natural(f"b_q_dw_{i}", hb, dq)
        pending.extend([("b_w_o", i), ("b_w_q", i)])
        dh = _mm_t_natural(f"b_q_dx_{i}", dq, w_q, 0)
        dx, g_b_norm[i] = _rms_bwd(x_in, b_norm[i], dh, dx, f"b_norm_bwd_{i}")
        if i > 0:
            dx = start_exchange(dx, f"b{i}")

    dkv = jnp.concatenate([dk[LEFT:], dv[LEFT:]], axis=1).astype(BF16)
    big_grads["w_kv", 0] = _mm_dw_colblock("kv_proj_dw", h_kv, dkv)
    pending.append(("w_kv", 0))
    dx, g_kv_norm = _mm_t_colblock_norm_bwd("kv_proj_dx", dkv, w_kv_g, 0, x_kv, kv_norm, dx)
    dx = start_exchange(dx, "kv")

    for i in reversed(range(n_a)):
        x_in, h, zs, dgs, p, ffn_saved = saved[i]
        dx = ffn_bwd(dx, i, ffn_saved)
        big_grads["a_w_out", i] = _mm_dw_natural(f"a_out_dw_{i}", p, dx)
        pending.append(("a_w_out", i))
        dx = start_exchange(dx, f"a{i}_out")
        w_in, w_out = mixer_a_weights(i, (x_in, p))
        dp = _mm_t_natural(f"a_out_dx_{i}", dx, w_out, 0)
        dz, g_w_sp[i], g_b_sp[i], g_a_sgu[i] = _sgu_bwd(
            zs, dgs, dp, a_sgu_full[i], a_w_spatial[i], w_sp_t[i], b_full[i], f"a_sgu_bwd_{i}")
        big_grads["a_w_in", i] = _mm_dw_colblock(f"a_in_dw_{i}", h, dz)
        pending.append(("a_w_in", i))
        dx = start_exchange(dx, f"a{i}_in")
        dx, g_a_norm[i] = _mm_t_colblock_norm_bwd(f"a_in_dx_{i}", dz, w_in, 0, x_in, a_norm_full[i], dx)
    grad_x = dx[None]

    small_like = [jax.ShapeDtypeStruct((n_a, d), F32), jax.ShapeDtypeStruct((n_a, f_a), F32),
                  a_w_spatial, a_b_spatial, kv_norm, b_norm, b_rel_bias, ffn_norm, final_norm]
    small_partial = _pack(
        [jnp.stack(g_a_norm), jnp.stack(g_a_sgu), jnp.stack(g_w_sp), jnp.stack(g_b_sp), g_kv_norm,
         jnp.stack(g_b_norm), jnp.stack(g_rel), jnp.stack(g_ffn_norm), g_final], N_DEV * 8)
    chunk_rows = small_partial.shape[0] // N_DEV
    arrived = {}
    for keys, send, recv, srcs, lands, tag in in_flight_grads:
        srcs, lands = _split_wait(send, recv, srcs, lands, dx, f"exchange_wait_{tag}", exchange=True)
        for key, src, land in zip(keys, srcs, lands):
            arrived[key] = (land, src)
    small_got = _exchange([small_partial.reshape(1, N_DEV, chunk_rows, FLAT_LANES)], "exchange_small")[0]
    small_sum = _ordered_sum(small_got[0], "small_grad_sum")
    small_all = _all_gather([small_sum[None]], "gather_small_grads")[0]
    (ga_norm, ga_sgu, gw_sp, gb_sp, gkv_norm, gb_norm, g_relb, gffn_norm, gfinal) = _unpack(small_all, small_like)

    results = {}
    big_names = ["a_w_in", "a_w_out", "w_kv", "b_w_q", "b_w_o", "ffn_w_gate_up", "ffn_w_down"]
    big_wmv = [(a_w_in, m_a_w_in, v_a_w_in), (a_w_out, m_a_w_out, v_a_w_out),
               (w_kv[None], m_w_kv[None], v_w_kv[None]), (b_w_q, m_b_w_q, v_b_w_q), (b_w_o, m_b_w_o, v_b_w_o),
               tuple(jnp.swapaxes(a, 1, 2) for a in (ffn_w_gate_up, m_ffn_w_gate_up, v_ffn_w_gate_up)),
               (ffn_w_down, m_ffn_w_down, v_ffn_w_down)]
    me_arr = jnp.reshape(me, (1,)).astype(jnp.int32)
    for name, (w, m, v) in zip(big_names, big_wmv):
        outs = None
        for layer in range(w.shape[0]):
            got, own = arrived[name, layer]
            outs = _adamw_layer(got, own, w, m, v, layer, outs, me_arr, f"adamw_{name}_{layer}")
        if name == "w_kv":
            outs = [o[0] for o in outs]
        if name == "ffn_w_gate_up":
            outs = [jnp.swapaxes(o, 1, 2) for o in outs]
        results[name] = outs

    n_cols = a_norm.shape[1]
    s_cols = a_sgu_norm.shape[1]
    small_g_list = [lax.dynamic_slice(ga_norm, (0, me * n_cols), (n_a, n_cols)),
                    lax.dynamic_slice(ga_sgu, (0, me * s_cols), (n_a, s_cols)),
                    gw_sp, gb_sp, gkv_norm, gb_norm, g_relb, gffn_norm, gfinal]
    small_names = ["a_norm", "a_sgu_norm", "a_w_spatial", "a_b_spatial", "kv_norm", "b_norm", "b_rel_bias",
                   "ffn_norm", "final_norm"]
    small_w = [a_norm, a_sgu_norm, a_w_spatial, a_b_spatial, kv_norm, b_norm, b_rel_bias, ffn_norm, final_norm]
    small_m = [m_a_norm, m_a_sgu_norm, m_a_w_spatial, m_a_b_spatial, m_kv_norm, m_b_norm, m_b_rel_bias,
               m_ffn_norm, m_final_norm]
    small_v = [v_a_norm, v_a_sgu_norm, v_a_w_spatial, v_a_b_spatial, v_kv_norm, v_b_norm, v_b_rel_bias,
               v_ffn_norm, v_final_norm]
    flat_g = _pack(small_g_list, 8)
    flat_out = _adamw(flat_g[None, None], _pack(small_w, 8)[None], _pack(small_m, 8)[None],
                      _pack(small_v, 8)[None], "adamw_small")
    unpacked = [_unpack(o[0], small_w) for o in flat_out]
    for idx, name in enumerate(small_names):
        results[name] = [unpacked[kind][idx] for kind in range(4)]

    order = ["a_norm", "a_w_in", "a_sgu_norm", "a_w_spatial", "a_b_spatial", "a_w_out", "kv_norm", "w_kv",
             "b_norm", "b_w_q", "b_rel_bias", "b_w_o", "ffn_norm", "ffn_w_gate_up", "ffn_w_down", "final_norm"]
    outputs = [loss, grad_x]
    for kind in range(4):
        outputs += [results[name][kind] for name in order]
    return tuple(outputs)
```

```python
import math

import jax
import jax.numpy as jnp
from jax import lax
from jax.experimental import pallas as pl
from jax.experimental.pallas import tpu as pltpu

F32 = jnp.float32
BF16 = jnp.bfloat16
MESH = pl.DeviceIdType.MESH
HBM_SPEC = pl.BlockSpec(memory_space=pltpu.HBM)
SEM_SPEC = pl.BlockSpec(memory_space=pltpu.SEMAPHORE)

N_DEV = 8
CHUNK = 64
A_CHUNK = 128
A_GROUPS = 8
N_LEFT_CHUNKS = 8
LEFT = N_LEFT_CHUNKS * CHUNK
PAIR_ROWS = 2 * CHUNK
PAIR_BAND = PAIR_ROWS + LEFT
DIAGONALS = PAIR_BAND + PAIR_ROWS
PAIRS_PER_BLOCK = 2
Q_BLOCK = PAIRS_PER_BLOCK * PAIR_ROWS
K_BLOCK = Q_BLOCK + LEFT
ATTN_UNROLL = 2
MAX_REL = 256
N_REL = 2 * MAX_REL + 1
REL_PAD = 640
HEAD_DIM = 64
HEAD_PAIR = 2 * HEAD_DIM
ATTN_SCALE = HEAD_DIM ** -0.5
EPS = 1e-6
NEG_INF = -1e30
ADAM_LR = 0.001
ADAM_B1 = 0.9
ADAM_B2 = 0.999
ADAM_EPS = 1e-08
ADAM_WD = 0.01
ADAM_STEP = 10
FLAT_LANES = 1024
F32_SUBLANES = 8
BF16_SUBLANES = 16
ADAMW_BLOCK_ELEMS = 256 * 1024
V7X_VMEM_BYTES = 64 * 1024 * 1024
VMEM_FLOOR_BYTES = 32 * 1024 * 1024
VMEM_CEIL_BYTES = V7X_VMEM_BYTES - 8 * 1024 * 1024

NN = (((1,), (0,)), ((), ()))
NT = (((1,), (1,)), ((), ()))
TN = (((0,), (0,)), ((), ()))


def _tile(n, pref):
    return pref if n % pref == 0 else n


def _row_tile(n, pref, mult):
    best = None
    for t in range(mult, min(n, pref) + 1, mult):
        if n % t == 0:
            best = t
    return best if best is not None else n


def _nbytes(shape, dtype):
    n = 1
    for s in shape:
        if s is not None:
            n *= s
    return n * jnp.dtype(dtype).itemsize


def _call(body, name, grid, in_specs, out_specs, out_shape, scratch=(), vmem_bytes=0, aliases=None):
    limit = int(min(max(VMEM_FLOOR_BYTES, vmem_bytes * 5 // 4), VMEM_CEIL_BYTES))
    return pl.pallas_call(
        body,
        name=name,
        grid=grid,
        in_specs=in_specs,
        out_specs=out_specs,
        out_shape=out_shape,
        scratch_shapes=list(scratch),
        input_output_aliases=aliases or {},
        compiler_params=pltpu.CompilerParams(
            dimension_semantics=("arbitrary",) * len(grid), vmem_limit_bytes=limit),
    )


ERFC_P = 0.3275911 / math.sqrt(2.0)
ERFC_HALF_COEFFS = tuple(0.5 * a for a in (1.061405429, -1.453152027, 1.421413741, -0.284496736, 0.254829592))


def _gelu_and_grad(x):
    d = 1.0 + ERFC_P * jnp.abs(x)
    r = pl.reciprocal(d, approx=True)
    t = r * (2.0 - d * r)
    a5, a4, a3, a2, a1 = ERFC_HALF_COEFFS
    ex = jnp.exp(-0.5 * (x * x))
    tail = ((((a5 * t + a4) * t + a3) * t + a2) * t + a1) * t * ex
    cdf = jnp.where(x < 0, tail, 1.0 - tail)
    return x * cdf, cdf + x * ex * (1.0 / math.sqrt(2.0 * math.pi))


def _sigmoid(x):
    return 1.0 / (1.0 + jnp.exp(-x))


def _split3(x):
    hi = x.astype(BF16)
    r1 = x - hi.astype(F32)
    mid = r1.astype(BF16)
    lo = (r1 - mid.astype(F32)).astype(BF16)
    return hi, mid, lo


def _rms_fwd(x, g, name):
    t, d = x.shape
    tm = _tile(t, 512)

    def body(x_ref, g_ref, o_ref):
        xf = x_ref[...]
        r = lax.rsqrt(jnp.mean(xf * xf, axis=-1, keepdims=True) + EPS)
        o_ref[...] = (xf * r * g_ref[...]).astype(o_ref.dtype)

    return _call(
        body, name, (t // tm,),
        [pl.BlockSpec((tm, d), lambda i: (i, 0)), pl.BlockSpec((1, d), lambda i: (0, 0))],
        pl.BlockSpec((tm, d), lambda i: (i, 0)),
        jax.ShapeDtypeStruct((t, d), BF16),
        vmem_bytes=2 * (_nbytes((tm, d), F32) + _nbytes((tm, d), BF16)) + 4 * _nbytes((tm, d), F32),
    )(x, g.reshape(1, d))


def _mm(name, dims, a, b, *, grid, a_spec, b_spec, out_shape, out_spec, acc_shape,
        res=None, res_spec=None, scale=None):
    nk = grid[2]
    has_res = res is not None

    def body(*refs):
        refs = list(refs)
        a_ref = refs.pop(0)
        b_ref = refs.pop(0)
        r_ref = refs.pop(0) if has_res else None
        o_ref = refs.pop(0)
        part = lax.dot_general(a_ref[...].astype(BF16), b_ref[...].astype(BF16), dims,
                               preferred_element_type=F32)

        def finish(acc):
            if scale is not None:
                acc = acc * scale
            if has_res:
                acc = acc + r_ref[...]
            o_ref[...] = acc.astype(o_ref.dtype)

        if nk == 1:
            finish(part)
        else:
            acc_ref = refs.pop(0)
            k = pl.program_id(2)

            @pl.when(k == 0)
            def _():
                acc_ref[...] = part

            @pl.when(k > 0)
            def _():
                acc_ref[...] += part

            @pl.when(k == nk - 1)
            def _():
                finish(acc_ref[...])

    operands = [a, b]
    in_specs = [a_spec, b_spec]
    vmem = 2 * (_nbytes(a_spec.block_shape, a.dtype) + _nbytes(b_spec.block_shape, b.dtype)
                + _nbytes(out_spec.block_shape, out_shape.dtype))
    vmem += 3 * _nbytes(acc_shape, F32)
    if has_res:
        operands.append(res)
        in_specs.append(res_spec)
        vmem += 2 * _nbytes(res_spec.block_shape, res.dtype)
    scratch = [pltpu.VMEM(acc_shape, F32)] if nk > 1 else []
    return _call(body, name, grid, in_specs, out_spec, out_shape, scratch=scratch, vmem_bytes=vmem)(*operands)


def _mm_colblock(name, h, w_g, layer):
    t, k = h.shape
    nb = w_g.shape[3]
    tm = _tile(t, 2048)
    return _mm(
        name, NN, h, w_g, grid=(t // tm, N_DEV, 1),
        a_spec=pl.BlockSpec((tm, k), lambda i, j, kk: (i, 0)),
        b_spec=pl.BlockSpec((None, None, k, nb), lambda i, j, kk: (layer, j, 0, 0)),
        out_shape=jax.ShapeDtypeStruct((t, N_DEV * nb), BF16),
        out_spec=pl.BlockSpec((tm, nb), lambda i, j, kk: (i, j)), acc_shape=(tm, nb))


def _mm_natural(name, a, w, layer, *, res=None, out_dtype=F32, scale=None):
    t, k = a.shape
    n = w.shape[2]
    tm = _tile(t, 1024)
    tn = _tile(n, 512)
    res_spec = None if res is None else pl.BlockSpec((tm, tn), lambda i, j, kk: (i, j))
    return _mm(
        name, NN, a, w, grid=(t // tm, n // tn, 1),
        a_spec=pl.BlockSpec((tm, k), lambda i, j, kk: (i, 0)),
        b_spec=pl.BlockSpec((None, k, tn), lambda i, j, kk: (layer, 0, j)),
        out_shape=jax.ShapeDtypeStruct((t, n), out_dtype),
        out_spec=pl.BlockSpec((tm, tn), lambda i, j, kk: (i, j)),
        acc_shape=(tm, tn), res=res, res_spec=res_spec, scale=scale)


def _mm_down(name, act, w4, layer, res):
    nblk, t, kb = act.shape
    n = w4.shape[3]
    tm = _tile(t, 1024)

    def body(a_ref, b_ref, r_ref, o_ref):
        acc = r_ref[...]
        for u in range(nblk):
            acc = acc + jnp.dot(a_ref[u], b_ref[u], preferred_element_type=F32)
        o_ref[...] = acc

    row = pl.BlockSpec((tm, n), lambda i: (i, 0))
    return _call(
        body, name, (t // tm,),
        [pl.BlockSpec((nblk, tm, kb), lambda i: (0, i, 0)),
         pl.BlockSpec((None, nblk, kb, n), lambda i: (layer, 0, 0, 0)),
         row],
        row,
        jax.ShapeDtypeStruct((t, n), F32),
        vmem_bytes=2 * (_nbytes((nblk, tm, kb), BF16) + _nbytes((nblk, kb, n), BF16)) + 6 * _nbytes((tm, n), F32),
    )(act, w4, res)


def _mm_t_colblock_norm_bwd(name, dz, w_g, layer, x, g, dx_up, blocked_in=False):
    k = w_g.shape[2]
    nb = w_g.shape[3]
    t = x.shape[0]
    tm = _tile(t, 1024)
    per_step = 2
    n_steps = N_DEV // per_step
    if blocked_in:
        a_spec = pl.BlockSpec((per_step, tm, nb), lambda i, kk: (kk, i, 0))
    else:
        a_spec = pl.BlockSpec((tm, per_step * nb), lambda i, kk: (i, kk))

    def body(a_ref, b_ref, x_ref, g_ref, up_ref, dx_ref, dg_ref, acc_ref):
        i = pl.program_id(0)
        kk = pl.program_id(1)
        part = None
        for u in range(per_step):
            a = a_ref[u] if blocked_in else a_ref[:, u * nb:(u + 1) * nb]
            term = lax.dot_general(a.astype(BF16), b_ref[u].astype(BF16), NT, preferred_element_type=F32)
            part = term if part is None else part + term

        @pl.when(kk == 0)
        def _():
            acc_ref[...] = part

        @pl.when(kk > 0)
        def _():
            acc_ref[...] += part

        @pl.when((i == 0) & (kk == 0))
        def _():
            dg_ref[...] = jnp.zeros_like(dg_ref)

        @pl.when(kk == n_steps - 1)
        def _():
            dy = acc_ref[...]
            xf = x_ref[...]
            r = lax.rsqrt(jnp.mean(xf * xf, axis=-1, keepdims=True) + EPS)
            xhat = xf * r
            dxhat = dy * g_ref[...]
            dg_ref[...] += jnp.sum(dy * xhat, axis=0, keepdims=True)
            dx_ref[...] = up_ref[...] + r * (dxhat - xhat * jnp.mean(dxhat * xhat, axis=-1, keepdims=True))

    row = pl.BlockSpec((tm, k), lambda i, kk: (i, 0))
    vec = pl.BlockSpec((1, k), lambda i, kk: (0, 0))
    dx, dg = _call(
        body, name, (t // tm, n_steps),
        [a_spec, pl.BlockSpec((None, per_step, k, nb), lambda i, kk: (layer, kk, 0, 0)), row, vec, row],
        [row, vec],
        [jax.ShapeDtypeStruct((t, k), F32), jax.ShapeDtypeStruct((1, k), F32)],
        scratch=[pltpu.VMEM((tm, k), F32)],
        vmem_bytes=2 * per_step * (_nbytes((tm, nb), BF16) + _nbytes((k, nb), BF16)) + 10 * _nbytes((tm, k), F32),
    )(dz, w_g, x, g.reshape(1, k), dx_up)
    return dx, dg.reshape(k)


def _ffn_gate_up(name, h, w_g, layer):
    t, k = h.shape
    nb = w_g.shape[3]
    half = N_DEV // 2
    tm = _tile(t, 1024)

    def body(h_ref, wg_ref, wu_ref, dact_ref, act_ref):
        hb = h_ref[...]
        gate = jnp.dot(hb, wg_ref[...], preferred_element_type=F32)
        up = jnp.dot(hb, wu_ref[...], preferred_element_type=F32)
        sig = _sigmoid(gate)
        silu = gate * sig
        dact_ref[0] = (up * (sig * (1.0 + gate * (1.0 - sig)))).astype(BF16)
        dact_ref[1] = silu.astype(BF16)
        act_ref[...] = (silu * up).astype(BF16)

    return _call(
        body, name, (t // tm, half),
        [pl.BlockSpec((tm, k), lambda i, j: (i, 0)),
         pl.BlockSpec((None, None, k, nb), lambda i, j: (layer, j, 0, 0)),
         pl.BlockSpec((None, None, k, nb), lambda i, j: (layer, half + j, 0, 0))],
        [pl.BlockSpec((2, None, tm, nb), lambda i, j: (0, j, i, 0)),
         pl.BlockSpec((None, tm, nb), lambda i, j: (j, i, 0))],
        [jax.ShapeDtypeStruct((2, half, t, nb), BF16), jax.ShapeDtypeStruct((half, t, nb), BF16)],
        vmem_bytes=2 * (_nbytes((tm, k), BF16) + 2 * _nbytes((k, nb), BF16) + 3 * _nbytes((tm, nb), BF16))
        + 8 * _nbytes((tm, nb), F32),
    )(h, w_g, w_g)


def _ffn_down_dx(name, dy, w4, layer, dact):
    t, n = dy.shape
    nblk, kb = w4.shape[1], w4.shape[2]
    tm = _tile(t, 1024)

    def body(dy_ref, w_ref, dact_ref, dgu_ref):
        da = lax.dot_general(dy_ref[...].astype(BF16), w_ref[...], NT, preferred_element_type=F32)
        dgu_ref[0] = (da * dact_ref[0].astype(F32)).astype(BF16)
        dgu_ref[1] = (da * dact_ref[1].astype(F32)).astype(BF16)

    blk = pl.BlockSpec((2, None, tm, kb), lambda i, j: (0, j, i, 0))
    return _call(
        body, name, (t // tm, nblk),
        [pl.BlockSpec((tm, n), lambda i, j: (i, 0)),
         pl.BlockSpec((None, None, kb, n), lambda i, j: (layer, j, 0, 0)),
         blk],
        blk,
        jax.ShapeDtypeStruct((2, nblk, t, kb), BF16),
        vmem_bytes=2 * (_nbytes((tm, n), F32) + _nbytes((kb, n), BF16) + 4 * _nbytes((tm, kb), BF16))
        + 8 * _nbytes((tm, kb), F32),
    )(dy, w4, dact)


def _mm_t_natural(name, dy, w, layer):
    t, n = dy.shape
    k = w.shape[1]
    tm = _tile(t, 1024)
    tk = _tile(k, 512)
    return _mm(
        name, NT, dy, w, grid=(t // tm, k // tk, 1),
        a_spec=pl.BlockSpec((tm, n), lambda i, j, kk: (i, 0)),
        b_spec=pl.BlockSpec((None, tk, n), lambda i, j, kk: (layer, j, 0)),
        out_shape=jax.ShapeDtypeStruct((t, k), BF16),
        out_spec=pl.BlockSpec((tm, tk), lambda i, j, kk: (i, j)),
        acc_shape=(tm, tk))


def _mm_t_natural_norm_bwd(name, dy, w, layer, x, g, dx_up):
    t, n = dy.shape
    k = w.shape[1]
    tm = _tile(t, 1024)

    def body(a_ref, b_ref, x_ref, g_ref, up_ref, dx_ref, dg_ref):
        @pl.when(pl.program_id(0) == 0)
        def _():
            dg_ref[...] = jnp.zeros_like(dg_ref)

        dh = lax.dot_general(a_ref[...].astype(BF16), b_ref[...], NT, preferred_element_type=F32)
        xf = x_ref[...]
        r = lax.rsqrt(jnp.mean(xf * xf, axis=-1, keepdims=True) + EPS)
        xhat = xf * r
        dxhat = dh * g_ref[...]
        dg_ref[...] += jnp.sum(dh * xhat, axis=0, keepdims=True)
        dx_ref[...] = up_ref[...] + r * (dxhat - xhat * jnp.mean(dxhat * xhat, axis=-1, keepdims=True))

    row = pl.BlockSpec((tm, k), lambda i: (i, 0))
    vec = pl.BlockSpec((1, k), lambda i: (0, 0))
    dx, dg = _call(
        body, name, (t // tm,),
        [pl.BlockSpec((tm, n), lambda i: (i, 0)), pl.BlockSpec((None, k, n), lambda i: (layer, 0, 0)), row, vec, row],
        [row, vec],
        [jax.ShapeDtypeStruct((t, k), F32), jax.ShapeDtypeStruct((1, k), F32)],
        vmem_bytes=2 * (_nbytes((tm, n), dy.dtype) + _nbytes((k, n), BF16)) + 10 * _nbytes((tm, k), F32),
    )(dy, w, x, g.reshape(1, k), dx_up)
    return dx, dg.reshape(k)


def _mm_dw_colblock(name, h, dz, blocked_in=False, transposed=False):
    t, k = h.shape
    nb = dz.shape[2] if blocked_in else dz.shape[1] // N_DEV
    tk = _tile(t, 2048)
    h_spec = pl.BlockSpec((tk, k), lambda i, j, kk: (kk, 0))
    if blocked_in:
        dz_spec = pl.BlockSpec((None, tk, nb), lambda i, j, kk: (j, kk, 0))
    else:
        dz_spec = pl.BlockSpec((tk, nb), lambda i, j, kk: (kk, j))
    rows, cols = (nb, k) if transposed else (k, nb)
    return _mm(
        name, TN, *((dz, h) if transposed else (h, dz)), grid=(1, N_DEV, t // tk),
        a_spec=dz_spec if transposed else h_spec,
        b_spec=h_spec if transposed else dz_spec,
        out_shape=jax.ShapeDtypeStruct((N_DEV, rows, cols), BF16),
        out_spec=pl.BlockSpec((None, rows, cols), lambda i, j, kk: (j, 0, 0)),
        acc_shape=(rows, cols))


def _mm_dw_natural(name, a, dy):
    t, k = a.shape
    n = dy.shape[1]
    tko = _tile(k, 1024)
    tt = _tile(t, 2048)
    out = _mm(
        name, TN, a, dy, grid=(k // tko, 1, t // tt),
        a_spec=pl.BlockSpec((tt, tko), lambda i, j, kk: (kk, i)),
        b_spec=pl.BlockSpec((tt, n), lambda i, j, kk: (kk, 0)),
        out_shape=jax.ShapeDtypeStruct((k, n), BF16),
        out_spec=pl.BlockSpec((tko, n), lambda i, j, kk: (i, 0)),
        acc_shape=(tko, n))
    return out.reshape(N_DEV, k // N_DEV, n)


def _mm_dw_down(name, act, dy):
    nblk, t, kb = act.shape
    n = dy.shape[1]
    tt = _tile(t, 2048)
    out = _mm(
        name, TN, act, dy, grid=(nblk, 1, t // tt),
        a_spec=pl.BlockSpec((None, tt, kb), lambda i, j, kk: (i, kk, 0)),
        b_spec=pl.BlockSpec((tt, n), lambda i, j, kk: (kk, 0)),
        out_shape=jax.ShapeDtypeStruct((nblk, kb, n), BF16),
        out_spec=pl.BlockSpec((None, kb, n), lambda i, j, kk: (i, 0, 0)),
        acc_shape=(kb, n))
    return out.reshape(N_DEV, (nblk * kb) // N_DEV, n)


def _spatial_mask(transposed=False):
    r = lax.broadcasted_iota(jnp.int32, (A_CHUNK, A_CHUNK), 0) // CHUNK
    c = lax.broadcasted_iota(jnp.int32, (A_CHUNK, A_CHUNK), 1) // CHUNK
    return c >= r if transposed else r >= c


def _sgu_tile(t):
    return _tile(t, 2 * A_CHUNK)


def _sgu_fwd(zpre, g_sgu, w_sp, b_full, name):
    t, f2 = zpre.shape
    f = f2 // 2
    gd = f // A_GROUPS
    tm = _sgu_tile(t)

    def body(z_ref, g_ref, w_ref, b_ref, p_ref, zs_ref, dg_ref):
        mask = _spatial_mask()
        wm = [jnp.where(mask, w_ref[g], 0.0).astype(BF16) for g in range(A_GROUPS)]
        for c in range(tm // A_CHUNK):
            rows = pl.ds(c * A_CHUNK, A_CHUNK)
            z, dgelu = _gelu_and_grad(z_ref[rows, :].astype(F32))
            zs_ref[rows, :] = z.astype(BF16)
            dg_ref[rows, :] = dgelu.astype(BF16)
            u = z[:, :f]
            v0 = z[:, f:]
            r = lax.rsqrt(jnp.mean(v0 * v0, axis=-1, keepdims=True) + EPS)
            v1 = (v0 * r * g_ref[...]).astype(BF16)
            for g in range(A_GROUPS):
                cols = slice(g * gd, (g + 1) * gd)
                v2 = jnp.dot(wm[g], v1[:, cols], preferred_element_type=F32) + b_ref[:, cols]
                p_ref[rows, cols] = (u[:, cols] * v2).astype(BF16)

    return _call(
        body, name, (t // tm,),
        [pl.BlockSpec((tm, f2), lambda i: (i, 0)),
         pl.BlockSpec((1, f), lambda i: (0, 0)),
         pl.BlockSpec((A_GROUPS, A_CHUNK, A_CHUNK), lambda i: (0, 0, 0)),
         pl.BlockSpec((A_CHUNK, f), lambda i: (0, 0))],
        [pl.BlockSpec((tm, f), lambda i: (i, 0)), pl.BlockSpec((tm, f2), lambda i: (i, 0)),
         pl.BlockSpec((tm, f2), lambda i: (i, 0))],
        [jax.ShapeDtypeStruct((t, f), BF16), jax.ShapeDtypeStruct((t, f2), BF16), jax.ShapeDtypeStruct((t, f2), BF16)],
        vmem_bytes=6 * _nbytes((tm, f2), BF16) + 2 * _nbytes((tm, f), BF16) + 8 * _nbytes((A_CHUNK, f2), F32),
    )(zpre, g_sgu.reshape(1, f), w_sp, b_full)


def _sgu_bwd(zs, dgs, dp, g_sgu, w_sp, w_sp_t, b_full, name):
    t, f2 = zs.shape
    f = f2 // 2
    gd = f // A_GROUPS
    tm = _sgu_tile(t)
    n_steps = t // tm

    def body(z_ref, dgelu_ref, dp_ref, g_ref, w_ref, wt_ref, b_ref, dz_ref, dw_ref, db_ref, dg_ref, dv1_ref, dbf_ref):
        step = pl.program_id(0)

        @pl.when(step == 0)
        def _():
            dw_ref[...] = jnp.zeros_like(dw_ref)
            dg_ref[...] = jnp.zeros_like(dg_ref)
            dbf_ref[...] = jnp.zeros_like(dbf_ref)

        mask = _spatial_mask()
        mask_t = _spatial_mask(transposed=True)
        wm = [jnp.where(mask, w_ref[g], 0.0).astype(BF16) for g in range(A_GROUPS)]
        wmt = [jnp.where(mask_t, wt_ref[g], 0.0).astype(BF16) for g in range(A_GROUPS)]
        gain = g_ref[...]
        for c in range(tm // A_CHUNK):
            rows = pl.ds(c * A_CHUNK, A_CHUNK)
            z = z_ref[rows, :].astype(F32)
            dgelu = dgelu_ref[rows, :].astype(F32)
            u = z[:, :f]
            v0 = z[:, f:]
            r = lax.rsqrt(jnp.mean(v0 * v0, axis=-1, keepdims=True) + EPS)
            xhat = v0 * r
            v1 = (xhat * gain).astype(BF16)
            dpf = dp_ref[rows, :].astype(F32)
            for g in range(A_GROUPS):
                cols = slice(g * gd, (g + 1) * gd)
                v1g = v1[:, cols]
                v2 = jnp.dot(wm[g], v1g, preferred_element_type=F32) + b_ref[:, cols]
                dpg = dpf[:, cols]
                dz_ref[rows, cols] = (dpg * v2 * dgelu[:, cols]).astype(BF16)
                dv2 = dpg * u[:, cols]
                dbf_ref[:, cols] += dv2
                dv2b = dv2.astype(BF16)
                dwg = lax.dot_general(dv2b, v1g, NT, preferred_element_type=F32)
                dw_ref[g] += jnp.where(mask, dwg, 0.0)
                dv1_ref[:, cols] = jnp.dot(wmt[g], dv2b, preferred_element_type=F32)
            dv1 = dv1_ref[...]
            dxhat = dv1 * gain
            dg_ref[...] += jnp.sum(dv1 * xhat, axis=0, keepdims=True)
            dv0 = r * (dxhat - xhat * jnp.mean(dxhat * xhat, axis=-1, keepdims=True))
            dz_ref[rows, pl.ds(f, f)] = (dv0 * dgelu[:, f:]).astype(BF16)

        @pl.when(step == n_steps - 1)
        def _():
            for g in range(A_GROUPS):
                db_ref[g] = jnp.sum(dbf_ref[:, g * gd:(g + 1) * gd], axis=1, keepdims=True)

    wspec = pl.BlockSpec((A_GROUPS, A_CHUNK, A_CHUNK), lambda i: (0, 0, 0))
    dz, dw, db, dg = _call(
        body, name, (n_steps,),
        [pl.BlockSpec((tm, f2), lambda i: (i, 0)),
         pl.BlockSpec((tm, f2), lambda i: (i, 0)),
         pl.BlockSpec((tm, f), lambda i: (i, 0)),
         pl.BlockSpec((1, f), lambda i: (0, 0)),
         wspec, wspec,
         pl.BlockSpec((A_CHUNK, f), lambda i: (0, 0))],
        [pl.BlockSpec((tm, f2), lambda i: (i, 0)),
         wspec,
         pl.BlockSpec((A_GROUPS, A_CHUNK, 1), lambda i: (0, 0, 0)),
         pl.BlockSpec((1, f), lambda i: (0, 0))],
        [jax.ShapeDtypeStruct((t, f2), BF16),
         jax.ShapeDtypeStruct((A_GROUPS, A_CHUNK, A_CHUNK), F32),
         jax.ShapeDtypeStruct((A_GROUPS, A_CHUNK, 1), F32),
         jax.ShapeDtypeStruct((1, f), F32)],
        scratch=[pltpu.VMEM((A_CHUNK, f), F32), pltpu.VMEM((A_CHUNK, f), F32)],
        vmem_bytes=6 * _nbytes((tm, f2), BF16) + 2 * _nbytes((tm, f), BF16) + 12 * _nbytes((A_CHUNK, f2), F32),
    )(zs, dgs, dp, g_sgu.reshape(1, f), w_sp, w_sp_t, b_full)
    return dz, dw, db.reshape(A_GROUPS, A_CHUNK), dg.reshape(f)


def _pair_valid(qi, col):
    qc = qi // CHUNK
    kc = col // CHUNK
    return (kc >= qc) & (kc <= qc + N_LEFT_CHUNKS)


def _diagonal_onehot():
    e = lax.broadcasted_iota(jnp.int32, (REL_PAD, DIAGONALS), 1)
    idx = jnp.clip(PAIR_BAND - 1 - e, -MAX_REL, MAX_REL) + MAX_REL
    r = lax.broadcasted_iota(jnp.int32, (REL_PAD, DIAGONALS), 0)
    return jnp.where(r == idx, 1.0, 0.0).astype(BF16)


def _bias_build(table, name):
    h = table.shape[0]
    tab = jnp.pad(table, ((0, 0), (0, REL_PAD - N_REL)))

    def body(t_ref, o_ref):
        oh = _diagonal_onehot()
        diag = jnp.zeros((h, DIAGONALS), F32)
        for piece in _split3(t_ref[...]):
            diag += jnp.dot(piece, oh, preferred_element_type=F32)
        col = lax.broadcasted_iota(jnp.int32, (h, PAIR_BAND), 1)
        for qi in range(PAIR_ROWS):
            row = pltpu.roll(diag, (qi - (PAIR_ROWS - 1)) % DIAGONALS, 1)[:, :PAIR_BAND]
            o_ref[qi] = jnp.where(_pair_valid(qi, col), row, NEG_INF)

    out = _call(
        body, name, (1,),
        [pl.BlockSpec((h, REL_PAD), lambda i: (0, 0))],
        pl.BlockSpec((PAIR_ROWS, h, PAIR_BAND), lambda i: (0, 0, 0)),
        jax.ShapeDtypeStruct((PAIR_ROWS, h, PAIR_BAND), F32),
        vmem_bytes=4 * _nbytes((PAIR_ROWS, h, PAIR_BAND), F32),
    )(tab)
    return jnp.transpose(out, (1, 0, 2))


def _bias_block(pair_bias):
    rest = K_BLOCK - PAIR_BAND
    return jnp.concatenate(
        [jnp.pad(pair_bias, ((0, 0), (0, 0), (p * PAIR_ROWS, rest - p * PAIR_ROWS)), constant_values=NEG_INF)
         for p in range(PAIRS_PER_BLOCK)], axis=1)


def _bias_grad(dbias, name):
    h = dbias.shape[0]
    db_t = jnp.transpose(dbias, (1, 0, 2))

    def body(d_ref, o_ref):
        diag = jnp.zeros((h, DIAGONALS), F32)
        for qi in range(PAIR_ROWS):
            diag += pltpu.roll(d_ref[qi], PAIR_ROWS - 1 - qi, 1)
        oh = _diagonal_onehot()
        acc = jnp.zeros((h, REL_PAD), F32)
        for piece in _split3(diag):
            acc += lax.dot_general(piece, oh, NT, preferred_element_type=F32)
        o_ref[...] = acc

    out = _call(
        body, name, (1,),
        [pl.BlockSpec((PAIR_ROWS, h, DIAGONALS), lambda i: (0, 0, 0))],
        pl.BlockSpec((h, REL_PAD), lambda i: (0, 0)),
        jax.ShapeDtypeStruct((h, REL_PAD), F32),
        vmem_bytes=4 * _nbytes((PAIR_ROWS, h, DIAGONALS), F32),
    )(db_t)
    return out[:, :N_REL]


def _head_masks():
    lane = lax.broadcasted_iota(jnp.int32, (Q_BLOCK, HEAD_PAIR), 1)
    return lane < HEAD_DIM, lane >= HEAD_DIM


def _block_scores(qm, kb, bias, valid):
    s = lax.dot_general(qm, kb, NT, preferred_element_type=F32) + bias
    return s if valid is None else jnp.where(valid, s, NEG_INF)


def _softmax_rows(s):
    e = jnp.exp(s - jnp.max(s, axis=-1, keepdims=True))
    return e * (1.0 / jnp.sum(e, axis=-1, keepdims=True))


def _padded_then_plain(step, n_blocks):
    n_padded = min(LEFT // Q_BLOCK, n_blocks)
    lax.fori_loop(0, n_padded, lambda j, c: step(j, c, True), 0)
    lax.fori_loop(n_padded, n_blocks, lambda j, c: step(j, c, False), 0, unroll=ATTN_UNROLL)


def _attn_fwd(q, kvpad, bias, name):
    t, d = q.shape
    n_pairs = d // HEAD_PAIR
    n_blocks = t // Q_BLOCK

    def body(q_ref, k_ref, v_ref, b_ref, o_ref):
        masks = _head_masks()
        key = lax.broadcasted_iota(jnp.int32, (Q_BLOCK, K_BLOCK), 1)

        def step(j, carry, padded):
            r0 = pl.multiple_of(j * Q_BLOCK, Q_BLOCK)
            q2 = q_ref[pl.ds(r0, Q_BLOCK), :].astype(F32)
            kb = k_ref[pl.ds(r0, K_BLOCK), :]
            vb = v_ref[pl.ds(r0, K_BLOCK), :]
            valid = key >= LEFT - j * Q_BLOCK if padded else None
            scores = [_block_scores(jnp.where(masks[a], q2, 0.0).astype(BF16), kb, b_ref[a], valid) for a in range(2)]
            probs = [_softmax_rows(s).astype(BF16) for s in scores]
            outs = [jnp.dot(p, vb, preferred_element_type=F32) for p in probs]
            o_ref[pl.ds(r0, Q_BLOCK), :] = jnp.where(masks[0], outs[0], outs[1]).astype(BF16)
            return carry

        _padded_then_plain(step, n_blocks)

    return _call(
        body, name, (n_pairs,),
        [pl.BlockSpec((t, HEAD_PAIR), lambda p: (0, p)),
         pl.BlockSpec((LEFT + t, HEAD_PAIR), lambda p: (0, p)),
         pl.BlockSpec((LEFT + t, HEAD_PAIR), lambda p: (0, n_pairs + p)),
         pl.BlockSpec((2, Q_BLOCK, K_BLOCK), lambda p: (p, 0, 0))],
        pl.BlockSpec((t, HEAD_PAIR), lambda p: (0, p)),
        jax.ShapeDtypeStruct((t, d), BF16),
        vmem_bytes=8 * _nbytes((LEFT + t, HEAD_PAIR), BF16) + 12 * _nbytes((2, Q_BLOCK, K_BLOCK), F32),
    )(q, kvpad, kvpad, bias)


def _attn_bwd(q, kvpad, bias, do, dk_in, dv_in, name):
    t, d = q.shape
    n_pairs = d // HEAD_PAIR
    n_blocks = t // Q_BLOCK
    has_in = dk_in is not None

    def body(*refs):
        refs = list(refs)
        q_ref, k_ref, v_ref, b_ref, do_ref = refs[:5]
        refs = refs[5:]
        if has_in:
            dki_ref, dvi_ref = refs[:2]
            refs = refs[2:]
        dq_ref, dk_ref, dv_ref, db_ref = refs
        masks = _head_masks()
        key = lax.broadcasted_iota(jnp.int32, (Q_BLOCK, K_BLOCK), 1)
        if has_in:
            dk_ref[...] = dki_ref[...]
            dv_ref[...] = dvi_ref[...]
        else:
            dk_ref[...] = jnp.zeros_like(dk_ref)
            dv_ref[...] = jnp.zeros_like(dv_ref)
        db_ref[...] = jnp.zeros_like(db_ref)

        def step(j, carry, padded):
            r0 = pl.multiple_of(j * Q_BLOCK, Q_BLOCK)
            q2 = q_ref[pl.ds(r0, Q_BLOCK), :].astype(F32)
            do2 = do_ref[pl.ds(r0, Q_BLOCK), :].astype(F32)
            kb = k_ref[pl.ds(r0, K_BLOCK), :]
            vb = v_ref[pl.ds(r0, K_BLOCK), :]
            valid = key >= LEFT - j * Q_BLOCK if padded else None
            heads = range(2)
            qms = [jnp.where(masks[a], q2, 0.0).astype(BF16) for a in heads]
            doms = [jnp.where(masks[a], do2, 0.0).astype(BF16) for a in heads]
            scores = [_block_scores(qms[a], kb, b_ref[a], valid) for a in heads]
            dps = [lax.dot_general(doms[a], vb, NT, preferred_element_type=F32) for a in heads]
            ps = [_softmax_rows(s) for s in scores]
            dss = [ps[a] * (dps[a] - jnp.sum(dps[a] * ps[a], axis=-1, keepdims=True)) for a in heads]
            for a in heads:
                for pair in range(PAIRS_PER_BLOCK):
                    lo = pair * PAIR_ROWS
                    db_ref[a, :, pl.ds(0, PAIR_BAND)] += dss[a][lo:lo + PAIR_ROWS, lo:lo + PAIR_BAND]
            dsbs = [ds.astype(BF16) for ds in dss]
            pbs = [p.astype(BF16) for p in ps]
            dqs = [jnp.dot(dsbs[a], kb, preferred_element_type=F32) for a in heads]
            dk_acc = sum(lax.dot_general(dsbs[a], qms[a], TN, preferred_element_type=F32) for a in heads)
            dv_acc = sum(lax.dot_general(pbs[a], doms[a], TN, preferred_element_type=F32) for a in heads)
            dq = jnp.where(masks[0], dqs[0], dqs[1]) * ATTN_SCALE
            dq_ref[pl.ds(r0, Q_BLOCK), :] = dq.astype(BF16)
            dk_ref[pl.ds(r0, K_BLOCK), :] += dk_acc
            dv_ref[pl.ds(r0, K_BLOCK), :] += dv_acc
            return carry

        _padded_then_plain(step, n_blocks)

    q_spec = pl.BlockSpec((t, HEAD_PAIR), lambda p: (0, p))
    kv_spec = pl.BlockSpec((LEFT + t, HEAD_PAIR), lambda p: (0, p))
    operands = [q, kvpad, kvpad, bias, do]
    in_specs = [q_spec, kv_spec, pl.BlockSpec((LEFT + t, HEAD_PAIR), lambda p: (0, n_pairs + p)),
                pl.BlockSpec((2, Q_BLOCK, K_BLOCK), lambda p: (p, 0, 0)), q_spec]
    aliases = None
    if has_in:
        operands += [dk_in, dv_in]
        in_specs += [kv_spec, kv_spec]
        aliases = {5: 1, 6: 2}
    return _call(
        body, name, (n_pairs,),
        in_specs,
        [q_spec, kv_spec, kv_spec, pl.BlockSpec((2, PAIR_ROWS, DIAGONALS), lambda p: (p, 0, 0))],
        [jax.ShapeDtypeStruct((t, d), BF16),
         jax.ShapeDtypeStruct((LEFT + t, d), F32),
         jax.ShapeDtypeStruct((LEFT + t, d), F32),
         jax.ShapeDtypeStruct((d // HEAD_DIM, PAIR_ROWS, DIAGONALS), F32)],
        vmem_bytes=10 * _nbytes((LEFT + t, HEAD_PAIR), BF16) + 8 * _nbytes((LEFT + t, HEAD_PAIR), F32)
        + 16 * _nbytes((2, Q_BLOCK, K_BLOCK), F32),
        aliases=aliases,
    )(*operands)


def _loss_head(x, g, target, name):
    t, d = x.shape
    tm = _tile(t, 512)

    def body(x_ref, g_ref, t_ref, dx_ref, loss_ref, dg_ref):
        @pl.when(pl.program_id(0) == 0)
        def _():
            loss_ref[...] = jnp.zeros_like(loss_ref)
            dg_ref[...] = jnp.zeros_like(dg_ref)

        xf = x_ref[...]
        r = lax.rsqrt(jnp.mean(xf * xf, axis=-1, keepdims=True) + EPS)
        xhat = xf * r
        diff = xhat * g_ref[...] - t_ref[...]
        row_loss = jnp.mean(diff * diff, axis=-1, keepdims=True)
        loss_ref[...] += 0.5 * jnp.sum(row_loss, axis=0, keepdims=True)
        dy = diff * (1.0 / d)
        dg_ref[...] += jnp.sum(dy * xhat, axis=0, keepdims=True)
        dxhat = dy * g_ref[...]
        dx_ref[...] = r * (dxhat - xhat * jnp.mean(dxhat * xhat, axis=-1, keepdims=True))

    row = pl.BlockSpec((tm, d), lambda i: (i, 0))
    vec = pl.BlockSpec((1, d), lambda i: (0, 0))
    dx, loss, dg = _call(
        body, name, (t // tm,),
        [row, vec, row],
        [row, pl.BlockSpec((1, 1), lambda i: (0, 0)), vec],
        [jax.ShapeDtypeStruct((t, d), F32), jax.ShapeDtypeStruct((1, 1), F32), jax.ShapeDtypeStruct((1, d), F32)],
        vmem_bytes=10 * _nbytes((tm, d), F32),
    )(x, g.reshape(1, d), target)
    return dx, loss[0, 0], dg.reshape(d)


def _adamw_store(g, w_ref, m_ref, v_ref, g_ref, d_ref, nm_ref, nv_ref):
    c1 = 1.0 / (1.0 - ADAM_B1 ** ADAM_STEP)
    c2 = 1.0 / (1.0 - ADAM_B2 ** ADAM_STEP)
    nm = ADAM_B1 * m_ref[...] + (1.0 - ADAM_B1) * g
    nv = ADAM_B2 * v_ref[...] + (1.0 - ADAM_B2) * (g * g)
    g_ref[...] = g
    nm_ref[...] = nm
    nv_ref[...] = nv
    d_ref[...] = -ADAM_LR * ((nm * c1) / (jnp.sqrt(nv * c2) + ADAM_EPS) + ADAM_WD * w_ref[...])


def _adamw_layer(recv, own, w, m, v, layer, prev, me, name):
    n_src, r, c = recv.shape
    tr = _row_tile(r, max(BF16_SUBLANES, ADAMW_BLOCK_ELEMS // c), BF16_SUBLANES)
    first = prev is None

    def body(me_ref, recv_ref, own_ref, w_ref, m_ref, v_ref, *rest):
        mine = me_ref[0]
        own_part = own_ref[...].astype(F32)
        g = None
        for s in range(n_src):
            part = jnp.where(mine == s, own_part, recv_ref[s].astype(F32))
            g = part if g is None else g + part
        _adamw_store(g, w_ref, m_ref, v_ref, *rest[-4:])

    blk = pl.BlockSpec((None, tr, c), lambda i, me_ref: (layer, i, 0))
    any_spec = pl.BlockSpec(memory_space=pl.ANY)
    out = jax.ShapeDtypeStruct(w.shape, F32)
    operands = [me, recv, own, w, m, v] + ([] if first else list(prev))
    vmem = 2 * _nbytes((n_src + 1, tr, c), BF16) + 18 * _nbytes((tr, c), F32)
    return pl.pallas_call(
        body,
        name=name,
        grid_spec=pltpu.PrefetchScalarGridSpec(
            num_scalar_prefetch=1,
            grid=(r // tr,),
            in_specs=[pl.BlockSpec((n_src, tr, c), lambda i, me_ref: (0, i, 0)),
                      pl.BlockSpec((None, tr, c), lambda i, me_ref: (me_ref[0], i, 0)),
                      blk, blk, blk] + ([] if first else [any_spec] * 4),
            out_specs=[blk, blk, blk, blk],
        ),
        out_shape=[out, out, out, out],
        input_output_aliases={} if first else {6 + j: j for j in range(4)},
        compiler_params=pltpu.CompilerParams(
            dimension_semantics=("arbitrary",),
            vmem_limit_bytes=int(min(max(VMEM_FLOOR_BYTES, vmem * 5 // 4), VMEM_CEIL_BYTES))),
    )(*operands)


def _adamw(parts, w, m, v, name):
    n_layers, n_src, r, c = parts.shape
    mult = BF16_SUBLANES if parts.dtype == BF16 else F32_SUBLANES
    tr = _row_tile(r, max(mult, ADAMW_BLOCK_ELEMS // c), mult)

    def body(p_ref, w_ref, m_ref, v_ref, g_ref, d_ref, nm_ref, nv_ref):
        g = p_ref[0].astype(F32)
        for s in range(1, n_src):
            g = g + p_ref[s].astype(F32)
        _adamw_store(g, w_ref, m_ref, v_ref, g_ref, d_ref, nm_ref, nv_ref)

    blk = pl.BlockSpec((None, tr, c), lambda l, i: (l, i, 0))
    out = jax.ShapeDtypeStruct((n_layers, r, c), F32)
    return _call(
        body, name, (n_layers, r // tr),
        [pl.BlockSpec((None, n_src, tr, c), lambda l, i: (l, 0, i, 0)), blk, blk, blk],
        [blk, blk, blk, blk],
        [out, out, out, out],
        vmem_bytes=2 * _nbytes((n_src, tr, c), parts.dtype) + 18 * _nbytes((tr, c), F32),
    )(parts, w, m, v)


def _ordered_sum(parts, name):
    n_src, r, c = parts.shape

    def body(p_ref, o_ref):
        acc = p_ref[0]
        for s in range(1, n_src):
            acc = acc + p_ref[s]
        o_ref[...] = acc

    return _call(
        body, name, (1,),
        [pl.BlockSpec((n_src, r, c), lambda i: (0, 0, 0))],
        pl.BlockSpec((r, c), lambda i: (0, 0)),
        jax.ShapeDtypeStruct((r, c), F32),
        vmem_bytes=4 * _nbytes((n_src, r, c), F32),
    )(parts)


def _position():
    return lax.axis_index("x"), lax.axis_index("y"), lax.axis_index("c")


def _linear(p):
    return 4 * p[0] + 2 * p[1] + p[2]


def _all_gather(shards, name):
    n = len(shards)

    def body(*refs):
        ins, outs = refs[:n], refs[n:2 * n]
        send_sems, recv_sems, local_sems = refs[2 * n:]
        x, y, c = _position()
        me, sibling = (x, y, c), (x, y, 1 - c)
        chips = [(1 - x, y), (x, 1 - y), (1 - x, 1 - y)]

        def slab(t, p):
            return outs[t].at[:, _linear(p)]

        def copy(t, k, block, to, src=None):
            return pltpu.make_async_remote_copy(
                src_ref=slab(t, block) if src is None else src,
                dst_ref=slab(t, block),
                send_sem=send_sems.at[t, k],
                recv_sem=recv_sems.at[t, k],
                device_id=to,
                device_id_type=MESH,
            )

        started = []
        for t in range(n):
            mine = pltpu.make_async_copy(ins[t], slab(t, me), local_sems.at[t])
            mine.start()
            started.append(mine)
        sends = []
        for t in range(n):
            first = [copy(t, 0, me, sibling, src=ins[t])]
            first += [copy(t, 1 + j, me, (*chip, c), src=ins[t]) for j, chip in enumerate(chips)]
            for cp in first:
                cp.start()
            sends += first
        for t in range(n):
            for j, chip in enumerate(chips):
                copy(t, 1 + j, (*chip, c), me).wait_recv()
                passed = copy(t, 4 + j, (*chip, c), sibling)
                passed.start()
                sends.append(passed)
        for t in range(n):
            copy(t, 0, sibling, me).wait_recv()
            for j, chip in enumerate(chips):
                copy(t, 4 + j, (*chip, 1 - c), me).wait_recv()
        for cp in sends:
            cp.wait_send()
        for mine in started:
            mine.wait()

    out_shape = [jax.ShapeDtypeStruct((s.shape[0], N_DEV) + s.shape[1:], s.dtype) for s in shards]
    return pl.pallas_call(
        body,
        name=name,
        in_specs=[HBM_SPEC] * n,
        out_specs=[HBM_SPEC] * n,
        out_shape=out_shape,
        scratch_shapes=[
            pltpu.SemaphoreType.DMA((n, N_DEV - 1)),
            pltpu.SemaphoreType.DMA((n, N_DEV - 1)),
            pltpu.SemaphoreType.DMA((n,)),
        ],
    )(*shards)


def _exchange(blocks, name):
    n = len(blocks)

    def body(*refs):
        ins, outs = refs[:n], refs[n:2 * n]
        send_sems, recv_sems, local_sems = refs[2 * n:]
        x, y, c = _position()
        me = _linear((x, y, c))
        flips = [(fx, fy, fc) for fx in (0, 1) for fy in (0, 1) for fc in (0, 1)][1:]

        def peer_of(flip):
            fx, fy, fc = flip
            return (1 - x if fx else x, 1 - y if fy else y, 1 - c if fc else c)

        def copy(t, k, peer):
            return pltpu.make_async_remote_copy(
                src_ref=ins[t].at[:, _linear(peer)],
                dst_ref=outs[t].at[:, me],
                send_sem=send_sems.at[t, k],
                recv_sem=recv_sems.at[t, k],
                device_id=peer,
                device_id_type=MESH,
            )

        def arrival(t, k, peer):
            return pltpu.make_async_remote_copy(
                src_ref=ins[t].at[:, _linear(peer)],
                dst_ref=outs[t].at[:, _linear(peer)],
                send_sem=send_sems.at[t, k],
                recv_sem=recv_sems.at[t, k],
                device_id=peer,
                device_id_type=MESH,
            )

        own = []
        for t in range(n):
            cp = pltpu.make_async_copy(ins[t].at[:, me], outs[t].at[:, me], local_sems.at[t])
            cp.start()
            own.append(cp)
        sends = []
        for t in range(n):
            for k, flip in enumerate(flips):
                cp = copy(t, k, peer_of(flip))
                cp.start()
                sends.append(cp)
        for t in range(n):
            for k, flip in enumerate(flips):
                arrival(t, k, peer_of(flip)).wait_recv()
        for cp in sends:
            cp.wait_send()
        for cp in own:
            cp.wait()

    out_shape = [jax.ShapeDtypeStruct(b.shape, b.dtype) for b in blocks]
    return pl.pallas_call(
        body,
        name=name,
        in_specs=[HBM_SPEC] * n,
        out_specs=[HBM_SPEC] * n,
        out_shape=out_shape,
        scratch_shapes=[
            pltpu.SemaphoreType.DMA((n, N_DEV - 1)),
            pltpu.SemaphoreType.DMA((n, N_DEV - 1)),
            pltpu.SemaphoreType.DMA((n,)),
        ],
    )(*blocks)


def _peers():
    x, y, c = _position()
    flips = [(fx, fy, fc) for fx in (0, 1) for fy in (0, 1) for fc in (0, 1)][1:]
    return [(1 - x if fx else x, 1 - y if fy else y, 1 - c if fc else c) for fx, fy, fc in flips]


SIBLING, OTHER_CHIPS = (0,), (1, 3, 5)
COPY_PEERS = {"gather": tuple(range(N_DEV - 1)), "exchange": tuple(range(N_DEV - 1)),
              "chips": SIBLING + OTHER_CHIPS, "forward": OTHER_CHIPS}


def _split_copy(kind, src_ref, land_ref, k, send_sem, recv_sem, starting):
    peers = _peers()
    peer = peers[SIBLING[0]] if kind == "forward" else peers[k]
    me = _linear(_position())
    if kind == "forward":
        slab = _linear(peers[k]) if starting else 0
        src, dst = land_ref.at[slab], land_ref.at[slab]
    elif kind == "exchange":
        src, dst = src_ref.at[_linear(peer) if starting else 0], land_ref.at[me if starting else 0]
    else:
        src, dst = src_ref, land_ref.at[me if starting else 0]
    return pltpu.make_async_remote_copy(src_ref=src, dst_ref=dst, send_sem=send_sem, recv_sem=recv_sem,
                                        device_id=peer, device_id_type=MESH)


def _split_start(groups, carry, name):
    arrays = [a for _, srcs, lands in groups for a in list(srcs) + list(lands)] + [carry]

    def body(*refs):
        ins, sems = refs[:len(arrays)], refs[len(arrays):len(arrays) + 2 * len(groups)]
        at = 0
        for g, (kind, srcs, lands) in enumerate(groups):
            src_refs, land_refs = ins[at:at + len(srcs)], ins[at + len(srcs):at + len(srcs) + len(lands)]
            at += len(srcs) + len(lands)
            peers = COPY_PEERS[kind]
            for t in range(len(lands)):
                for slot, k in enumerate(peers):
                    sem = t * len(peers) + slot
                    _split_copy(kind, src_refs[t] if srcs else None, land_refs[t], k,
                                sems[2 * g].at[sem], sems[2 * g + 1].at[sem], True).start()

    sem_shapes = [pltpu.SemaphoreType.DMA((len(lands) * len(COPY_PEERS[kind]),))
                  for kind, _, lands in groups for _ in range(2)]
    out = pl.pallas_call(
        body,
        name=name,
        in_specs=[HBM_SPEC] * len(arrays),
        out_specs=[SEM_SPEC] * len(sem_shapes) + [HBM_SPEC] * len(arrays),
        out_shape=sem_shapes + [pltpu.HBM(a.shape, a.dtype) for a in arrays],
        input_output_aliases={i: len(sem_shapes) + i for i in range(len(arrays))},
        compiler_params=pltpu.CompilerParams(has_side_effects=pltpu.SideEffectType.DATAFLOW_SIDE_EFFECTING),
    )(*[pltpu.with_memory_space_constraint(a, pltpu.HBM) for a in arrays])
    sems, thru = out[:len(sem_shapes)], out[len(sem_shapes):]
    started, at = [], 0
    for g, (kind, srcs, lands) in enumerate(groups):
        n_s, n_l = len(srcs), len(lands)
        started.append((kind, sems[2 * g], sems[2 * g + 1], thru[at:at + n_s], thru[at + n_s:at + n_s + n_l]))
        at += n_s + n_l
    return started, thru[-1]


def _split_wait(started, after, name):
    kind, send_sems, recv_sems, srcs, lands = started
    n_s, n_l = len(srcs), len(lands)
    peers = COPY_PEERS[kind]

    def body(*refs):
        src_refs, land_refs = refs[:n_s], refs[n_s:n_s + n_l]
        send_ref, recv_ref = refs[n_s + n_l], refs[n_s + n_l + 1]
        for t in range(n_l):
            for slot, k in enumerate(peers):
                sem = t * len(peers) + slot
                copy = _split_copy(kind, src_refs[t] if n_s else None, land_refs[t], k,
                                   send_ref.at[sem], recv_ref.at[sem], False)
                copy.wait_send()
                copy.wait_recv()

    arrays = list(srcs) + list(lands)
    out = pl.pallas_call(
        body,
        name=name,
        in_specs=[HBM_SPEC] * len(arrays) + [SEM_SPEC, SEM_SPEC, pl.BlockSpec(memory_space=pl.ANY)],
        out_specs=[HBM_SPEC] * len(arrays),
        out_shape=[pltpu.HBM(a.shape, a.dtype) for a in arrays],
        input_output_aliases={i: i for i in range(len(arrays))},
        compiler_params=pltpu.CompilerParams(has_side_effects=pltpu.SideEffectType.DATAFLOW_SIDE_EFFECTING),
    )(*arrays, send_sems, recv_sems, after)
    return out[:n_s], out[n_s:]


def _pack(arrays, row_multiple):
    flat = jnp.concatenate([a.reshape(-1) for a in arrays])
    quantum = row_multiple * FLAT_LANES
    padded = -(-flat.shape[0] // quantum) * quantum
    return jnp.pad(flat, (0, padded - flat.shape[0])).reshape(-1, FLAT_LANES)


def _unpack(flat, like):
    flat = flat.reshape(-1)
    out, at = [], 0
    for a in like:
        size = math.prod(a.shape)
        out.append(flat[at:at + size].reshape(a.shape))
        at += size
    return out


def kernel(x, a_norm, a_w_in, a_sgu_norm, a_w_spatial, a_b_spatial, a_w_out, kv_norm, w_kv, b_norm, b_w_q, b_rel_bias, b_w_o, ffn_norm, ffn_w_gate_up, ffn_w_down, final_norm, loss_target, m_a_norm, m_a_w_in, m_a_sgu_norm, m_a_w_spatial, m_a_b_spatial, m_a_w_out, m_kv_norm, m_w_kv, m_b_norm, m_b_w_q, m_b_rel_bias, m_b_w_o, m_ffn_norm, m_ffn_w_gate_up, m_ffn_w_down, m_final_norm, v_a_norm, v_a_w_in, v_a_sgu_norm, v_a_w_spatial, v_a_b_spatial, v_a_w_out, v_kv_norm, v_w_kv, v_b_norm, v_b_w_q, v_b_rel_bias, v_b_w_o, v_ffn_norm, v_ffn_w_gate_up, v_ffn_w_down, v_final_norm):
    xs = x[0]
    target = loss_target[0]
    t, d = xs.shape
    n_a = a_w_in.shape[0]
    n_b = b_w_q.shape[0]
    depth = ffn_w_gate_up.shape[0]
    f_a = a_w_out.shape[1] * N_DEV
    gd = f_a // A_GROUPS
    nb_ffn = ffn_w_gate_up.shape[2]
    me = _linear(_position())

    small_rows = -(-(a_norm.size + a_sgu_norm.size) // (8 * 128)) * 8
    small = jnp.pad(jnp.concatenate([a_norm.reshape(-1), a_sgu_norm.reshape(-1)]),
                    (0, small_rows * 128 - a_norm.size - a_sgu_norm.size)).reshape(1, small_rows, 128)

    def shard(w, layer=None):
        return (w if layer is None else w[layer]).astype(BF16)

    stages = []
    for layer in range(depth):
        if layer == 0:
            stages += [("a0", [shard(a_w_in, 0)]), ("a0_out", [shard(a_w_out, 0)])]
        elif layer < n_a:
            stages.append((f"a{layer}", [shard(a_w_in, layer), shard(a_w_out, layer)]))
        else:
            i = layer - n_a
            shared = [shard(w_kv)] if i == 0 else []
            stages.append((f"b{i}", shared + [shard(b_w_q, i), shard(b_w_o, i)]))
        stages.append((f"f{layer}", [shard(ffn_w_gate_up, layer), shard(ffn_w_down, layer)]))
    first = _all_gather([s[None] for s in stages[0][1]] + [small], "gather_first")
    gathered = {stages[0][0]: [g[0] for g in first[:-1]]}
    small_g = first[-1].reshape(N_DEV, -1)
    a_norm_full = small_g[:, :a_norm.size].reshape(N_DEV, n_a, -1).transpose(1, 0, 2).reshape(n_a, d)
    a_sgu_full = small_g[:, a_norm.size:a_norm.size + a_sgu_norm.size].reshape(
        N_DEV, n_a, -1).transpose(1, 0, 2).reshape(n_a, f_a)
    two_level = ("f0", "a1", "f1")
    later = [("chips" if key in two_level else "gather", shards,
              [lax.dynamic_update_slice(lax.empty((N_DEV,) + s.shape, BF16), s[None], (me, 0, 0)) for s in shards])
             for key, shards in stages[1:]]
    started, a_norm_full = _split_start(later, a_norm_full, "gather_start")
    in_flight = {key: group for (key, _), group in zip(stages[1:], started)}

    def pass_on(key, carry):
        if key in two_level and key in in_flight and in_flight[key][0] == "chips":
            _, lands = _split_wait(in_flight.pop(key), carry, f"gather_wait_{key}_chips")
            (in_flight[key],), carry = _split_start([("forward", [], lands)], carry, f"gather_pass_on_{key}")
        return carry

    def weights(key, after):
        if key not in gathered:
            _, gathered[key] = _split_wait(in_flight.pop(key), after, f"gather_wait_{key}")
        return gathered[key]

    rows_down = ffn_w_down.shape[1]

    def mixer_a_weights(i, after):
        if i == 0:
            (w_in,), (w_out,) = weights("a0", after[0]), weights("a0_out", after[1])
        else:
            w_in, w_out = weights(f"a{i}", after[0])
        return w_in[None], w_out.reshape(1, f_a, d)

    def mixer_b_weights(i, after):
        ws = weights(f"b{i}", after)
        return ws[-2].reshape(1, d, d), ws[-1].reshape(1, d, d)

    def ffn_weights(layer, after):
        w_gu, w_dn = weights(f"f{layer}", after)
        return w_gu[None], w_dn.reshape(1, N_DEV // 2, 2 * rows_down, d)

    w_sp_t = jnp.swapaxes(a_w_spatial, -1, -2)
    b_full = jnp.repeat(jnp.swapaxes(a_b_spatial, -1, -2), gd, axis=-1)

    saved = []

    def ffn_fwd(xin, layer):
        hf = _rms_fwd(xin, ffn_norm[layer], f"ffn_norm_fwd_{layer}")
        w_gu, w_dn = ffn_weights(layer, xin)
        dact, act = _ffn_gate_up(f"ffn_gate_up_{layer}", hf, w_gu, 0)
        act = pass_on(f"a{layer + 1}", act)
        xout = _mm_down(f"ffn_down_{layer}", act, w_dn, 0, xin)
        return xout, (xin, hf, dact, act)

    for i in range(n_a):
        h = _rms_fwd(xs, a_norm_full[i], f"a_norm_fwd_{i}")
        zpre = _mm_colblock(f"a_in_{i}", h, weights(f"a{i}", xs)[0][None], 0)
        p, zs, dgs = _sgu_fwd(zpre, a_sgu_full[i], a_w_spatial[i], b_full[i], f"a_sgu_fwd_{i}")
        p = pass_on(f"f{i}", p)
        w_in, w_out = mixer_a_weights(i, (xs, p))
        x_mid = _mm_natural(f"a_out_{i}", p, w_out, 0, res=xs)
        x_out, ffn_saved = ffn_fwd(x_mid, i)
        saved.append((xs, h, zs, dgs, p, ffn_saved))
        xs = x_out

    x_kv = xs
    w_kv_g = weights("b0", x_kv)[0][None]
    h_kv = _rms_fwd(x_kv, kv_norm, "kv_norm_fwd")
    kv = _mm_colblock("kv_proj", h_kv, w_kv_g, 0)
    kvpad = jnp.pad(kv, ((LEFT, 0), (0, 0)))

    biases = [_bias_block(_bias_build(b_rel_bias[i], f"rel_bias_{i}")) for i in range(n_b)]
    for i in range(n_b):
        layer = n_a + i
        w_q, w_o = mixer_b_weights(i, xs)
        hb = _rms_fwd(xs, b_norm[i], f"b_norm_fwd_{i}")
        q = _mm_natural(f"b_q_{i}", hb, w_q, 0, out_dtype=BF16, scale=ATTN_SCALE)
        o = _attn_fwd(q, kvpad, biases[i], f"b_attn_fwd_{i}")
        x_mid = _mm_natural(f"b_o_{i}", o, w_o, 0, res=xs)
        x_out, ffn_saved = ffn_fwd(x_mid, layer)
        saved.append((xs, hb, q, o, ffn_saved))
        xs = x_out

    dx, loss_local, g_final = _loss_head(xs, final_norm, target, "loss_head")
    loss = lax.psum(loss_local, ("x", "y", "c"))

    big_grads = {}
    pending = []
    in_flight_grads = []

    def start_exchange(dx, tag):
        srcs = [big_grads[key] for key in pending]
        lands = [lax.empty(s.shape, BF16) for s in srcs]
        (group,), dx = _split_start([("exchange", srcs, lands)], dx, f"exchange_start_{tag}")
        in_flight_grads.append((list(pending), group, tag))
        pending.clear()
        return dx

    g_ffn_norm = [None] * depth
    g_a_norm = [None] * n_a
    g_a_sgu = [None] * n_a
    g_w_sp = [None] * n_a
    g_b_sp = [None] * n_a
    g_b_norm = [None] * n_b
    g_rel = [None] * n_b

    def ffn_bwd(dx, layer, ffn_saved):
        eager = layer < n_a
        xin, hf, dact, act = ffn_saved
        big_grads["ffn_w_down", layer] = _mm_dw_down(f"ffn_down_dw_{layer}", act, dx)
        pending.append(("ffn_w_down", layer))
        if eager:
            dx = start_exchange(dx, f"f{layer}_down")
        w_gu, w_dn = ffn_weights(layer, xin)
        dgu = _ffn_down_dx(f"ffn_down_dx_{layer}", dx, w_dn, 0, dact).reshape(N_DEV, t, nb_ffn)
        big_grads["ffn_w_gate_up", layer] = _mm_dw_colblock(
            f"ffn_gate_up_dw_{layer}", hf, dgu, blocked_in=True, transposed=True)
        pending.append(("ffn_w_gate_up", layer))
        if eager:
            dx = start_exchange(dx, f"f{layer}_gate_up")
        dx, g_ffn_norm[layer] = _mm_t_colblock_norm_bwd(
            f"ffn_gate_up_dx_{layer}", dgu, w_gu, 0, xin, ffn_norm[layer], dx, blocked_in=True)
        return dx

    dk = dv = None
    for i in reversed(range(n_b)):
        layer = n_a + i
        x_in, hb, q, o, ffn_saved = saved[layer]
        dx = ffn_bwd(dx, layer, ffn_saved)
        big_grads["b_w_o", i] = _mm_dw_natural(f"b_o_dw_{i}", o, dx)
        w_q, w_o = mixer_b_weights(i, x_in)
        do = _mm_t_natural(f"b_o_dx_{i}", dx, w_o, 0)
        dq, dk, dv, dbias = _attn_bwd(q, kvpad, biases[i], do, dk, dv, f"b_attn_bwd_{i}")
        g_rel[i] = _bias_grad(dbias, f"rel_bias_grad_{i}")
        big_grads["b_w_q", i] = _mm_dw_natural(f"b_q_dw_{i}", hb, dq)
        pending.extend([("b_w_o", i), ("b_w_q", i)])
        dx, g_b_norm[i] = _mm_t_natural_norm_bwd(f"b_q_dx_{i}", dq, w_q, 0, x_in, b_norm[i], dx)
        if i > 0:
            dx = start_exchange(dx, f"b{i}")

    dkv = jnp.concatenate([dk[LEFT:], dv[LEFT:]], axis=1).astype(BF16)
    big_grads["w_kv", 0] = _mm_dw_colblock("kv_proj_dw", h_kv, dkv)
    pending.append(("w_kv", 0))
    dx, g_kv_norm = _mm_t_colblock_norm_bwd("kv_proj_dx", dkv, w_kv_g, 0, x_kv, kv_norm, dx)
    dx = start_exchange(dx, "kv")

    for i in reversed(range(n_a)):
        x_in, h, zs, dgs, p, ffn_saved = saved[i]
        dx = ffn_bwd(dx, i, ffn_saved)
        big_grads["a_w_out", i] = _mm_dw_natural(f"a_out_dw_{i}", p, dx)
        pending.append(("a_w_out", i))
        dx = start_exchange(dx, f"a{i}_out")
        w_in, w_out = mixer_a_weights(i, (x_in, p))
        dp = _mm_t_natural(f"a_out_dx_{i}", dx, w_out, 0)
        dz, g_w_sp[i], g_b_sp[i], g_a_sgu[i] = _sgu_bwd(
            zs, dgs, dp, a_sgu_full[i], a_w_spatial[i], w_sp_t[i], b_full[i], f"a_sgu_bwd_{i}")
        big_grads["a_w_in", i] = _mm_dw_colblock(f"a_in_dw_{i}", h, dz)
        pending.append(("a_w_in", i))
        dx = start_exchange(dx, f"a{i}_in")
        dx, g_a_norm[i] = _mm_t_colblock_norm_bwd(f"a_in_dx_{i}", dz, w_in, 0, x_in, a_norm_full[i], dx)
    grad_x = dx[None]

    small_like = [jax.ShapeDtypeStruct((n_a, d), F32), jax.ShapeDtypeStruct((n_a, f_a), F32),
                  a_w_spatial, a_b_spatial, kv_norm, b_norm, b_rel_bias, ffn_norm, final_norm]
    small_partial = _pack(
        [jnp.stack(g_a_norm), jnp.stack(g_a_sgu), jnp.stack(g_w_sp), jnp.stack(g_b_sp), g_kv_norm,
         jnp.stack(g_b_norm), jnp.stack(g_rel), jnp.stack(g_ffn_norm), g_final], N_DEV * 8)
    chunk_rows = small_partial.shape[0] // N_DEV
    arrived = {}
    for keys, group, tag in in_flight_grads:
        srcs, lands = _split_wait(group, dx, f"exchange_wait_{tag}")
        for key, src, land in zip(keys, srcs, lands):
            arrived[key] = (land, src)
    small_got = _exchange([small_partial.reshape(1, N_DEV, chunk_rows, FLAT_LANES)], "exchange_small")[0]
    small_sum = _ordered_sum(small_got[0], "small_grad_sum")
    small_all = _all_gather([small_sum[None]], "gather_small_grads")[0]
    (ga_norm, ga_sgu, gw_sp, gb_sp, gkv_norm, gb_norm, g_relb, gffn_norm, gfinal) = _unpack(small_all, small_like)

    results = {}
    big_names = ["a_w_in", "a_w_out", "w_kv", "b_w_q", "b_w_o", "ffn_w_gate_up", "ffn_w_down"]
    big_wmv = [(a_w_in, m_a_w_in, v_a_w_in), (a_w_out, m_a_w_out, v_a_w_out),
               (w_kv[None], m_w_kv[None], v_w_kv[None]), (b_w_q, m_b_w_q, v_b_w_q), (b_w_o, m_b_w_o, v_b_w_o),
               tuple(jnp.swapaxes(a, 1, 2) for a in (ffn_w_gate_up, m_ffn_w_gate_up, v_ffn_w_gate_up)),
               (ffn_w_down, m_ffn_w_down, v_ffn_w_down)]
    me_arr = jnp.reshape(me, (1,)).astype(jnp.int32)
    for name, (w, m, v) in zip(big_names, big_wmv):
        outs = None
        for layer in range(w.shape[0]):
            got, own = arrived[name, layer]
            outs = _adamw_layer(got, own, w, m, v, layer, outs, me_arr, f"adamw_{name}_{layer}")
        if name == "w_kv":
            outs = [o[0] for o in outs]
        if name == "ffn_w_gate_up":
            outs = [jnp.swapaxes(o, 1, 2) for o in outs]
        results[name] = outs

    n_cols = a_norm.shape[1]
    s_cols = a_sgu_norm.shape[1]
    small_g_list = [lax.dynamic_slice(ga_norm, (0, me * n_cols), (n_a, n_cols)),
                    lax.dynamic_slice(ga_sgu, (0, me * s_cols), (n_a, s_cols)),
                    gw_sp, gb_sp, gkv_norm, gb_norm, g_relb, gffn_norm, gfinal]
    small_names = ["a_norm", "a_sgu_norm", "a_w_spatial", "a_b_spatial", "kv_norm", "b_norm", "b_rel_bias",
                   "ffn_norm", "final_norm"]
    small_w = [a_norm, a_sgu_norm, a_w_spatial, a_b_spatial, kv_norm, b_norm, b_rel_bias, ffn_norm, final_norm]
    small_m = [m_a_norm, m_a_sgu_norm, m_a_w_spatial, m_a_b_spatial, m_kv_norm, m_b_norm, m_b_rel_bias,
               m_ffn_norm, m_final_norm]
    small_v = [v_a_norm, v_a_sgu_norm, v_a_w_spatial, v_a_b_spatial, v_kv_norm, v_b_norm, v_b_rel_bias,
               v_ffn_norm, v_final_norm]
    flat_g = _pack(small_g_list, 8)
    flat_out = _adamw(flat_g[None, None], _pack(small_w, 8)[None], _pack(small_m, 8)[None],
                      _pack(small_v, 8)[None], "adamw_small")
    unpacked = [_unpack(o[0], small_w) for o in flat_out]
    for idx, name in enumerate(small_names):
        results[name] = [unpacked[kind][idx] for kind in range(4)]

    order = ["a_norm", "a_w_in", "a_sgu_norm", "a_w_spatial", "a_b_spatial", "a_w_out", "kv_norm", "w_kv",
             "b_norm", "b_w_q", "b_rel_bias", "b_w_o", "ffn_norm", "ffn_w_gate_up", "ffn_w_down", "final_norm"]
    outputs = [loss, grad_x]
    for kind in range(4):
        outputs += [results[name][kind] for name in order]
    return tuple(outputs)
```

```python
import math

import jax
import jax.numpy as jnp
from jax import lax
from jax.experimental import pallas as pl
from jax.experimental.pallas import tpu as pltpu

F32 = jnp.float32
BF16 = jnp.bfloat16
MESH = pl.DeviceIdType.MESH
HBM_SPEC = pl.BlockSpec(memory_space=pltpu.HBM)
SEM_SPEC = pl.BlockSpec(memory_space=pltpu.SEMAPHORE)

N_DEV = 8
CHUNK = 64
A_CHUNK = 128
A_GROUPS = 8
N_LEFT_CHUNKS = 8
LEFT = N_LEFT_CHUNKS * CHUNK
PAIR_ROWS = 2 * CHUNK
PAIR_BAND = PAIR_ROWS + LEFT
DIAGONALS = PAIR_BAND + PAIR_ROWS
PAIRS_PER_BLOCK = 2
Q_BLOCK = PAIRS_PER_BLOCK * PAIR_ROWS
K_BLOCK = Q_BLOCK + LEFT
ATTN_UNROLL = 2
MAX_REL = 256
N_REL = 2 * MAX_REL + 1
REL_PAD = 640
HEAD_DIM = 64
HEAD_PAIR = 2 * HEAD_DIM
ATTN_SCALE = HEAD_DIM ** -0.5
EPS = 1e-6
NEG_INF = -1e30
ADAM_LR = 0.001
ADAM_B1 = 0.9
ADAM_B2 = 0.999
ADAM_EPS = 1e-08
ADAM_WD = 0.01
ADAM_STEP = 10
FLAT_LANES = 1024
F32_SUBLANES = 8
BF16_SUBLANES = 16
ADAMW_BLOCK_ELEMS = 256 * 1024
V7X_VMEM_BYTES = 64 * 1024 * 1024
VMEM_FLOOR_BYTES = 32 * 1024 * 1024
VMEM_CEIL_BYTES = V7X_VMEM_BYTES - 8 * 1024 * 1024

NN = (((1,), (0,)), ((), ()))
NT = (((1,), (1,)), ((), ()))
TN = (((0,), (0,)), ((), ()))


def _tile(n, pref):
    return pref if n % pref == 0 else n


def _row_tile(n, pref, mult):
    best = None
    for t in range(mult, min(n, pref) + 1, mult):
        if n % t == 0:
            best = t
    return best if best is not None else n


def _nbytes(shape, dtype):
    n = 1
    for s in shape:
        if s is not None:
            n *= s
    return n * jnp.dtype(dtype).itemsize


def _call(body, name, grid, in_specs, out_specs, out_shape, scratch=(), vmem_bytes=0, aliases=None):
    limit = int(min(max(VMEM_FLOOR_BYTES, vmem_bytes * 5 // 4), VMEM_CEIL_BYTES))
    return pl.pallas_call(
        body,
        name=name,
        grid=grid,
        in_specs=in_specs,
        out_specs=out_specs,
        out_shape=out_shape,
        scratch_shapes=list(scratch),
        input_output_aliases=aliases or {},
        compiler_params=pltpu.CompilerParams(
            dimension_semantics=("arbitrary",) * len(grid), vmem_limit_bytes=limit),
    )


ERFC_P = 0.3275911 / math.sqrt(2.0)
ERFC_HALF_COEFFS = tuple(0.5 * a for a in (1.061405429, -1.453152027, 1.421413741, -0.284496736, 0.254829592))


def _gelu_and_grad(x):
    d = 1.0 + ERFC_P * jnp.abs(x)
    r = pl.reciprocal(d, approx=True)
    t = r * (2.0 - d * r)
    a5, a4, a3, a2, a1 = ERFC_HALF_COEFFS
    ex = jnp.exp(-0.5 * (x * x))
    tail = ((((a5 * t + a4) * t + a3) * t + a2) * t + a1) * t * ex
    cdf = jnp.where(x < 0, tail, 1.0 - tail)
    return x * cdf, cdf + x * ex * (1.0 / math.sqrt(2.0 * math.pi))


def _sigmoid(x):
    return 1.0 / (1.0 + jnp.exp(-x))


def _split3(x):
    hi = x.astype(BF16)
    r1 = x - hi.astype(F32)
    mid = r1.astype(BF16)
    lo = (r1 - mid.astype(F32)).astype(BF16)
    return hi, mid, lo


def _rms_fwd(x, g, name):
    t, d = x.shape
    tm = _tile(t, 512)

    def body(x_ref, g_ref, o_ref):
        xf = x_ref[...]
        r = lax.rsqrt(jnp.mean(xf * xf, axis=-1, keepdims=True) + EPS)
        o_ref[...] = (xf * r * g_ref[...]).astype(o_ref.dtype)

    return _call(
        body, name, (t // tm,),
        [pl.BlockSpec((tm, d), lambda i: (i, 0)), pl.BlockSpec((1, d), lambda i: (0, 0))],
        pl.BlockSpec((tm, d), lambda i: (i, 0)),
        jax.ShapeDtypeStruct((t, d), BF16),
        vmem_bytes=2 * (_nbytes((tm, d), F32) + _nbytes((tm, d), BF16)) + 4 * _nbytes((tm, d), F32),
    )(x, g.reshape(1, d))


def _mm(name, dims, a, b, *, grid, a_spec, b_spec, out_shape, out_spec, acc_shape,
        res=None, res_spec=None, scale=None):
    nk = grid[2]
    has_res = res is not None

    def body(*refs):
        refs = list(refs)
        a_ref = refs.pop(0)
        b_ref = refs.pop(0)
        r_ref = refs.pop(0) if has_res else None
        o_ref = refs.pop(0)
        part = lax.dot_general(a_ref[...].astype(BF16), b_ref[...].astype(BF16), dims,
                               preferred_element_type=F32)

        def finish(acc):
            if scale is not None:
                acc = acc * scale
            if has_res:
                acc = acc + r_ref[...]
            o_ref[...] = acc.astype(o_ref.dtype)

        if nk == 1:
            finish(part)
        else:
            acc_ref = refs.pop(0)
            k = pl.program_id(2)

            @pl.when(k == 0)
            def _():
                acc_ref[...] = part

            @pl.when(k > 0)
            def _():
                acc_ref[...] += part

            @pl.when(k == nk - 1)
            def _():
                finish(acc_ref[...])

    operands = [a, b]
    in_specs = [a_spec, b_spec]
    vmem = 2 * (_nbytes(a_spec.block_shape, a.dtype) + _nbytes(b_spec.block_shape, b.dtype)
                + _nbytes(out_spec.block_shape, out_shape.dtype))
    vmem += 3 * _nbytes(acc_shape, F32)
    if has_res:
        operands.append(res)
        in_specs.append(res_spec)
        vmem += 2 * _nbytes(res_spec.block_shape, res.dtype)
    scratch = [pltpu.VMEM(acc_shape, F32)] if nk > 1 else []
    return _call(body, name, grid, in_specs, out_spec, out_shape, scratch=scratch, vmem_bytes=vmem)(*operands)


def _mm_colblock(name, h, w_g, layer):
    t, k = h.shape
    nb = w_g.shape[3]
    tm = _tile(t, 2048)
    return _mm(
        name, NN, h, w_g, grid=(t // tm, N_DEV, 1),
        a_spec=pl.BlockSpec((tm, k), lambda i, j, kk: (i, 0)),
        b_spec=pl.BlockSpec((None, None, k, nb), lambda i, j, kk: (layer, j, 0, 0)),
        out_shape=jax.ShapeDtypeStruct((t, N_DEV * nb), BF16),
        out_spec=pl.BlockSpec((tm, nb), lambda i, j, kk: (i, j)), acc_shape=(tm, nb))


def _mm_natural(name, a, w, layer, *, res=None, out_dtype=F32, scale=None):
    t, k = a.shape
    n = w.shape[2]
    tm = _tile(t, 1024)
    tn = _tile(n, 1024)
    res_spec = None if res is None else pl.BlockSpec((tm, tn), lambda i, j, kk: (i, j))
    return _mm(
        name, NN, a, w, grid=(t // tm, n // tn, 1),
        a_spec=pl.BlockSpec((tm, k), lambda i, j, kk: (i, 0)),
        b_spec=pl.BlockSpec((None, k, tn), lambda i, j, kk: (layer, 0, j)),
        out_shape=jax.ShapeDtypeStruct((t, n), out_dtype),
        out_spec=pl.BlockSpec((tm, tn), lambda i, j, kk: (i, j)),
        acc_shape=(tm, tn), res=res, res_spec=res_spec, scale=scale)


def _mm_down(name, act, w4, layer, res):
    nblk, t, kb = act.shape
    n = w4.shape[3]
    tm = _tile(t, 1024)

    def body(a_ref, b_ref, r_ref, o_ref):
        acc = r_ref[...]
        for u in range(nblk):
            acc = acc + jnp.dot(a_ref[u], b_ref[u], preferred_element_type=F32)
        o_ref[...] = acc

    row = pl.BlockSpec((tm, n), lambda i: (i, 0))
    return _call(
        body, name, (t // tm,),
        [pl.BlockSpec((nblk, tm, kb), lambda i: (0, i, 0)),
         pl.BlockSpec((None, nblk, kb, n), lambda i: (layer, 0, 0, 0)),
         row],
        row,
        jax.ShapeDtypeStruct((t, n), F32),
        vmem_bytes=2 * (_nbytes((nblk, tm, kb), BF16) + _nbytes((nblk, kb, n), BF16)) + 6 * _nbytes((tm, n), F32),
    )(act, w4, res)


def _mm_t_colblock_norm_bwd(name, dz, w_g, layer, x, g, dx_up, blocked_in=False):
    k = w_g.shape[2]
    nb = w_g.shape[3]
    t = x.shape[0]
    tm = _tile(t, 1024)
    per_step = 2
    n_steps = N_DEV // per_step
    if blocked_in:
        a_spec = pl.BlockSpec((per_step, tm, nb), lambda i, kk: (kk, i, 0))
    else:
        a_spec = pl.BlockSpec((tm, per_step * nb), lambda i, kk: (i, kk))

    def body(a_ref, b_ref, x_ref, g_ref, up_ref, dx_ref, dg_ref, acc_ref):
        i = pl.program_id(0)
        kk = pl.program_id(1)
        part = None
        for u in range(per_step):
            a = a_ref[u] if blocked_in else a_ref[:, u * nb:(u + 1) * nb]
            term = lax.dot_general(a.astype(BF16), b_ref[u].astype(BF16), NT, preferred_element_type=F32)
            part = term if part is None else part + term

        @pl.when(kk == 0)
        def _():
            acc_ref[...] = part

        @pl.when(kk > 0)
        def _():
            acc_ref[...] += part

        @pl.when((i == 0) & (kk == 0))
        def _():
            dg_ref[...] = jnp.zeros_like(dg_ref)

        @pl.when(kk == n_steps - 1)
        def _():
            dy = acc_ref[...]
            xf = x_ref[...]
            r = lax.rsqrt(jnp.mean(xf * xf, axis=-1, keepdims=True) + EPS)
            xhat = xf * r
            dxhat = dy * g_ref[...]
            dg_ref[...] += jnp.sum(dy * xhat, axis=0, keepdims=True)
            dx_ref[...] = up_ref[...] + r * (dxhat - xhat * jnp.mean(dxhat * xhat, axis=-1, keepdims=True))

    row = pl.BlockSpec((tm, k), lambda i, kk: (i, 0))
    vec = pl.BlockSpec((1, k), lambda i, kk: (0, 0))
    dx, dg = _call(
        body, name, (t // tm, n_steps),
        [a_spec, pl.BlockSpec((None, per_step, k, nb), lambda i, kk: (layer, kk, 0, 0)), row, vec, row],
        [row, vec],
        [jax.ShapeDtypeStruct((t, k), F32), jax.ShapeDtypeStruct((1, k), F32)],
        scratch=[pltpu.VMEM((tm, k), F32)],
        vmem_bytes=2 * per_step * (_nbytes((tm, nb), BF16) + _nbytes((k, nb), BF16)) + 10 * _nbytes((tm, k), F32),
    )(dz, w_g, x, g.reshape(1, k), dx_up)
    return dx, dg.reshape(k)


def _ffn_gate_up(name, h, w_g, layer):
    t, k = h.shape
    nb = w_g.shape[3]
    half = N_DEV // 2
    tm = _tile(t, 1024)

    def body(h_ref, wg_ref, wu_ref, dact_ref, act_ref):
        hb = h_ref[...]
        gate = jnp.dot(hb, wg_ref[...], preferred_element_type=F32)
        up = jnp.dot(hb, wu_ref[...], preferred_element_type=F32)
        sig = _sigmoid(gate)
        silu = gate * sig
        dact_ref[0] = (up * (sig * (1.0 + gate * (1.0 - sig)))).astype(BF16)
        dact_ref[1] = silu.astype(BF16)
        act_ref[...] = (silu * up).astype(BF16)

    return _call(
        body, name, (t // tm, half),
        [pl.BlockSpec((tm, k), lambda i, j: (i, 0)),
         pl.BlockSpec((None, None, k, nb), lambda i, j: (layer, j, 0, 0)),
         pl.BlockSpec((None, None, k, nb), lambda i, j: (layer, half + j, 0, 0))],
        [pl.BlockSpec((2, None, tm, nb), lambda i, j: (0, j, i, 0)),
         pl.BlockSpec((None, tm, nb), lambda i, j: (j, i, 0))],
        [jax.ShapeDtypeStruct((2, half, t, nb), BF16), jax.ShapeDtypeStruct((half, t, nb), BF16)],
        vmem_bytes=2 * (_nbytes((tm, k), BF16) + 2 * _nbytes((k, nb), BF16) + 3 * _nbytes((tm, nb), BF16))
        + 8 * _nbytes((tm, nb), F32),
    )(h, w_g, w_g)


def _ffn_down_dx(name, dy, w4, layer, dact):
    t, n = dy.shape
    nblk, kb = w4.shape[1], w4.shape[2]
    tm = _tile(t, 1024)

    def body(dy_ref, w_ref, dact_ref, dgu_ref):
        da = lax.dot_general(dy_ref[...].astype(BF16), w_ref[...], NT, preferred_element_type=F32)
        dgu_ref[0] = (da * dact_ref[0].astype(F32)).astype(BF16)
        dgu_ref[1] = (da * dact_ref[1].astype(F32)).astype(BF16)

    blk = pl.BlockSpec((2, None, tm, kb), lambda i, j: (0, j, i, 0))
    return _call(
        body, name, (t // tm, nblk),
        [pl.BlockSpec((tm, n), lambda i, j: (i, 0)),
         pl.BlockSpec((None, None, kb, n), lambda i, j: (layer, j, 0, 0)),
         blk],
        blk,
        jax.ShapeDtypeStruct((2, nblk, t, kb), BF16),
        vmem_bytes=2 * (_nbytes((tm, n), F32) + _nbytes((kb, n), BF16) + 4 * _nbytes((tm, kb), BF16))
        + 8 * _nbytes((tm, kb), F32),
    )(dy, w4, dact)


def _mm_t_natural(name, dy, w, layer):
    t, n = dy.shape
    k = w.shape[1]
    tm = _tile(t, 1024)
    tk = _tile(k, 1024)
    return _mm(
        name, NT, dy, w, grid=(t // tm, k // tk, 1),
        a_spec=pl.BlockSpec((tm, n), lambda i, j, kk: (i, 0)),
        b_spec=pl.BlockSpec((None, tk, n), lambda i, j, kk: (layer, j, 0)),
        out_shape=jax.ShapeDtypeStruct((t, k), BF16),
        out_spec=pl.BlockSpec((tm, tk), lambda i, j, kk: (i, j)),
        acc_shape=(tm, tk))


def _mm_t_natural_norm_bwd(name, dy, w, layer, x, g, dx_up):
    t, n = dy.shape
    k = w.shape[1]
    tm = _tile(t, 1024)

    def body(a_ref, b_ref, x_ref, g_ref, up_ref, dx_ref, dg_ref):
        @pl.when(pl.program_id(0) == 0)
        def _():
            dg_ref[...] = jnp.zeros_like(dg_ref)

        dh = lax.dot_general(a_ref[...].astype(BF16), b_ref[...], NT, preferred_element_type=F32)
        xf = x_ref[...]
        r = lax.rsqrt(jnp.mean(xf * xf, axis=-1, keepdims=True) + EPS)
        xhat = xf * r
        dxhat = dh * g_ref[...]
        dg_ref[...] += jnp.sum(dh * xhat, axis=0, keepdims=True)
        dx_ref[...] = up_ref[...] + r * (dxhat - xhat * jnp.mean(dxhat * xhat, axis=-1, keepdims=True))

    row = pl.BlockSpec((tm, k), lambda i: (i, 0))
    vec = pl.BlockSpec((1, k), lambda i: (0, 0))
    dx, dg = _call(
        body, name, (t // tm,),
        [pl.BlockSpec((tm, n), lambda i: (i, 0)), pl.BlockSpec((None, k, n), lambda i: (layer, 0, 0)), row, vec, row],
        [row, vec],
        [jax.ShapeDtypeStruct((t, k), F32), jax.ShapeDtypeStruct((1, k), F32)],
        vmem_bytes=2 * (_nbytes((tm, n), dy.dtype) + _nbytes((k, n), BF16)) + 10 * _nbytes((tm, k), F32),
    )(dy, w, x, g.reshape(1, k), dx_up)
    return dx, dg.reshape(k)


def _mm_dw_colblock(name, h, dz, blocked_in=False, transposed=False):
    t, k = h.shape
    nb = dz.shape[2] if blocked_in else dz.shape[1] // N_DEV
    tk = _tile(t, 2048)
    h_spec = pl.BlockSpec((tk, k), lambda i, j, kk: (kk, 0))
    if blocked_in:
        dz_spec = pl.BlockSpec((None, tk, nb), lambda i, j, kk: (j, kk, 0))
    else:
        dz_spec = pl.BlockSpec((tk, nb), lambda i, j, kk: (kk, j))
    rows, cols = (nb, k) if transposed else (k, nb)
    return _mm(
        name, TN, *((dz, h) if transposed else (h, dz)), grid=(1, N_DEV, t // tk),
        a_spec=dz_spec if transposed else h_spec,
        b_spec=h_spec if transposed else dz_spec,
        out_shape=jax.ShapeDtypeStruct((N_DEV, rows, cols), BF16),
        out_spec=pl.BlockSpec((None, rows, cols), lambda i, j, kk: (j, 0, 0)),
        acc_shape=(rows, cols))


def _mm_dw_natural(name, a, dy):
    t, k = a.shape
    n = dy.shape[1]
    tko = _tile(k, 1024)
    tt = _tile(t, 2048)
    out = _mm(
        name, TN, a, dy, grid=(k // tko, 1, t // tt),
        a_spec=pl.BlockSpec((tt, tko), lambda i, j, kk: (kk, i)),
        b_spec=pl.BlockSpec((tt, n), lambda i, j, kk: (kk, 0)),
        out_shape=jax.ShapeDtypeStruct((k, n), BF16),
        out_spec=pl.BlockSpec((tko, n), lambda i, j, kk: (i, 0)),
        acc_shape=(tko, n))
    return out.reshape(N_DEV, k // N_DEV, n)


def _mm_dw_down(name, act, dy):
    nblk, t, kb = act.shape
    n = dy.shape[1]
    tt = _tile(t, 2048)
    out = _mm(
        name, TN, act, dy, grid=(nblk, 1, t // tt),
        a_spec=pl.BlockSpec((None, tt, kb), lambda i, j, kk: (i, kk, 0)),
        b_spec=pl.BlockSpec((tt, n), lambda i, j, kk: (kk, 0)),
        out_shape=jax.ShapeDtypeStruct((nblk, kb, n), BF16),
        out_spec=pl.BlockSpec((None, kb, n), lambda i, j, kk: (i, 0, 0)),
        acc_shape=(kb, n))
    return out.reshape(N_DEV, (nblk * kb) // N_DEV, n)


def _spatial_mask(transposed=False):
    r = lax.broadcasted_iota(jnp.int32, (A_CHUNK, A_CHUNK), 0) // CHUNK
    c = lax.broadcasted_iota(jnp.int32, (A_CHUNK, A_CHUNK), 1) // CHUNK
    return c >= r if transposed else r >= c


def _sgu_tile(t):
    return _tile(t, 2 * A_CHUNK)


def _sgu_fwd(zpre, g_sgu, w_sp, b_full, name):
    t, f2 = zpre.shape
    f = f2 // 2
    gd = f // A_GROUPS
    tm = _sgu_tile(t)

    def body(z_ref, g_ref, w_ref, b_ref, p_ref, zs_ref, dg_ref):
        mask = _spatial_mask()
        wm = [jnp.where(mask, w_ref[g], 0.0).astype(BF16) for g in range(A_GROUPS)]
        for c in range(tm // A_CHUNK):
            rows = pl.ds(c * A_CHUNK, A_CHUNK)
            z, dgelu = _gelu_and_grad(z_ref[rows, :].astype(F32))
            zs_ref[rows, :] = z.astype(BF16)
            dg_ref[rows, :] = dgelu.astype(BF16)
            u = z[:, :f]
            v0 = z[:, f:]
            r = lax.rsqrt(jnp.mean(v0 * v0, axis=-1, keepdims=True) + EPS)
            v1 = (v0 * r * g_ref[...]).astype(BF16)
            for g in range(A_GROUPS):
                cols = slice(g * gd, (g + 1) * gd)
                v2 = jnp.dot(wm[g], v1[:, cols], preferred_element_type=F32) + b_ref[:, cols]
                p_ref[rows, cols] = (u[:, cols] * v2).astype(BF16)

    return _call(
        body, name, (t // tm,),
        [pl.BlockSpec((tm, f2), lambda i: (i, 0)),
         pl.BlockSpec((1, f), lambda i: (0, 0)),
         pl.BlockSpec((A_GROUPS, A_CHUNK, A_CHUNK), lambda i: (0, 0, 0)),
         pl.BlockSpec((A_CHUNK, f), lambda i: (0, 0))],
        [pl.BlockSpec((tm, f), lambda i: (i, 0)), pl.BlockSpec((tm, f2), lambda i: (i, 0)),
         pl.BlockSpec((tm, f2), lambda i: (i, 0))],
        [jax.ShapeDtypeStruct((t, f), BF16), jax.ShapeDtypeStruct((t, f2), BF16), jax.ShapeDtypeStruct((t, f2), BF16)],
        vmem_bytes=6 * _nbytes((tm, f2), BF16) + 2 * _nbytes((tm, f), BF16) + 8 * _nbytes((A_CHUNK, f2), F32),
    )(zpre, g_sgu.reshape(1, f), w_sp, b_full)


def _sgu_bwd(zs, dgs, dp, g_sgu, w_sp, w_sp_t, b_full, name):
    t, f2 = zs.shape
    f = f2 // 2
    gd = f // A_GROUPS
    tm = _sgu_tile(t)
    n_steps = t // tm

    def body(z_ref, dgelu_ref, dp_ref, g_ref, w_ref, wt_ref, b_ref, dz_ref, dw_ref, db_ref, dg_ref, dv1_ref, dbf_ref):
        step = pl.program_id(0)

        @pl.when(step == 0)
        def _():
            dw_ref[...] = jnp.zeros_like(dw_ref)
            dg_ref[...] = jnp.zeros_like(dg_ref)
            dbf_ref[...] = jnp.zeros_like(dbf_ref)

        mask = _spatial_mask()
        mask_t = _spatial_mask(transposed=True)
        wm = [jnp.where(mask, w_ref[g], 0.0).astype(BF16) for g in range(A_GROUPS)]
        wmt = [jnp.where(mask_t, wt_ref[g], 0.0).astype(BF16) for g in range(A_GROUPS)]
        gain = g_ref[...]
        for c in range(tm // A_CHUNK):
            rows = pl.ds(c * A_CHUNK, A_CHUNK)
            z = z_ref[rows, :].astype(F32)
            dgelu = dgelu_ref[rows, :].astype(F32)
            u = z[:, :f]
            v0 = z[:, f:]
            r = lax.rsqrt(jnp.mean(v0 * v0, axis=-1, keepdims=True) + EPS)
            xhat = v0 * r
            v1 = (xhat * gain).astype(BF16)
            dpf = dp_ref[rows, :].astype(F32)
            for g in range(A_GROUPS):
                cols = slice(g * gd, (g + 1) * gd)
                v1g = v1[:, cols]
                v2 = jnp.dot(wm[g], v1g, preferred_element_type=F32) + b_ref[:, cols]
                dpg = dpf[:, cols]
                dz_ref[rows, cols] = (dpg * v2 * dgelu[:, cols]).astype(BF16)
                dv2 = dpg * u[:, cols]
                dbf_ref[:, cols] += dv2
                dv2b = dv2.astype(BF16)
                dwg = lax.dot_general(dv2b, v1g, NT, preferred_element_type=F32)
                dw_ref[g] += jnp.where(mask, dwg, 0.0)
                dv1_ref[:, cols] = jnp.dot(wmt[g], dv2b, preferred_element_type=F32)
            dv1 = dv1_ref[...]
            dxhat = dv1 * gain
            dg_ref[...] += jnp.sum(dv1 * xhat, axis=0, keepdims=True)
            dv0 = r * (dxhat - xhat * jnp.mean(dxhat * xhat, axis=-1, keepdims=True))
            dz_ref[rows, pl.ds(f, f)] = (dv0 * dgelu[:, f:]).astype(BF16)

        @pl.when(step == n_steps - 1)
        def _():
            for g in range(A_GROUPS):
                db_ref[g] = jnp.sum(dbf_ref[:, g * gd:(g + 1) * gd], axis=1, keepdims=True)

    wspec = pl.BlockSpec((A_GROUPS, A_CHUNK, A_CHUNK), lambda i: (0, 0, 0))
    dz, dw, db, dg = _call(
        body, name, (n_steps,),
        [pl.BlockSpec((tm, f2), lambda i: (i, 0)),
         pl.BlockSpec((tm, f2), lambda i: (i, 0)),
         pl.BlockSpec((tm, f), lambda i: (i, 0)),
         pl.BlockSpec((1, f), lambda i: (0, 0)),
         wspec, wspec,
         pl.BlockSpec((A_CHUNK, f), lambda i: (0, 0))],
        [pl.BlockSpec((tm, f2), lambda i: (i, 0)),
         wspec,
         pl.BlockSpec((A_GROUPS, A_CHUNK, 1), lambda i: (0, 0, 0)),
         pl.BlockSpec((1, f), lambda i: (0, 0))],
        [jax.ShapeDtypeStruct((t, f2), BF16),
         jax.ShapeDtypeStruct((A_GROUPS, A_CHUNK, A_CHUNK), F32),
         jax.ShapeDtypeStruct((A_GROUPS, A_CHUNK, 1), F32),
         jax.ShapeDtypeStruct((1, f), F32)],
        scratch=[pltpu.VMEM((A_CHUNK, f), F32), pltpu.VMEM((A_CHUNK, f), F32)],
        vmem_bytes=6 * _nbytes((tm, f2), BF16) + 2 * _nbytes((tm, f), BF16) + 12 * _nbytes((A_CHUNK, f2), F32),
    )(zs, dgs, dp, g_sgu.reshape(1, f), w_sp, w_sp_t, b_full)
    return dz, dw, db.reshape(A_GROUPS, A_CHUNK), dg.reshape(f)


def _pair_valid(qi, col):
    qc = qi // CHUNK
    kc = col // CHUNK
    return (kc >= qc) & (kc <= qc + N_LEFT_CHUNKS)


def _diagonal_onehot():
    e = lax.broadcasted_iota(jnp.int32, (REL_PAD, DIAGONALS), 1)
    idx = jnp.clip(PAIR_BAND - 1 - e, -MAX_REL, MAX_REL) + MAX_REL
    r = lax.broadcasted_iota(jnp.int32, (REL_PAD, DIAGONALS), 0)
    return jnp.where(r == idx, 1.0, 0.0).astype(BF16)


def _bias_build(table, name):
    h = table.shape[0]
    tab = jnp.pad(table, ((0, 0), (0, REL_PAD - N_REL)))

    def body(t_ref, o_ref):
        oh = _diagonal_onehot()
        diag = jnp.zeros((h, DIAGONALS), F32)
        for piece in _split3(t_ref[...]):
            diag += jnp.dot(piece, oh, preferred_element_type=F32)
        col = lax.broadcasted_iota(jnp.int32, (h, PAIR_BAND), 1)
        for qi in range(PAIR_ROWS):
            row = pltpu.roll(diag, (qi - (PAIR_ROWS - 1)) % DIAGONALS, 1)[:, :PAIR_BAND]
            o_ref[qi] = jnp.where(_pair_valid(qi, col), row, NEG_INF)

    out = _call(
        body, name, (1,),
        [pl.BlockSpec((h, REL_PAD), lambda i: (0, 0))],
        pl.BlockSpec((PAIR_ROWS, h, PAIR_BAND), lambda i: (0, 0, 0)),
        jax.ShapeDtypeStruct((PAIR_ROWS, h, PAIR_BAND), F32),
        vmem_bytes=4 * _nbytes((PAIR_ROWS, h, PAIR_BAND), F32),
    )(tab)
    return jnp.transpose(out, (1, 0, 2))


def _bias_block(pair_bias):
    rest = K_BLOCK - PAIR_BAND
    return jnp.concatenate(
        [jnp.pad(pair_bias, ((0, 0), (0, 0), (p * PAIR_ROWS, rest - p * PAIR_ROWS)), constant_values=NEG_INF)
         for p in range(PAIRS_PER_BLOCK)], axis=1)


def _bias_grad(dbias, name):
    h = dbias.shape[0]
    db_t = jnp.transpose(dbias, (1, 0, 2))

    def body(d_ref, o_ref):
        diag = jnp.zeros((h, DIAGONALS), F32)
        for qi in range(PAIR_ROWS):
            diag += pltpu.roll(d_ref[qi], PAIR_ROWS - 1 - qi, 1)
        oh = _diagonal_onehot()
        acc = jnp.zeros((h, REL_PAD), F32)
        for piece in _split3(diag):
            acc += lax.dot_general(piece, oh, NT, preferred_element_type=F32)
        o_ref[...] = acc

    out = _call(
        body, name, (1,),
        [pl.BlockSpec((PAIR_ROWS, h, DIAGONALS), lambda i: (0, 0, 0))],
        pl.BlockSpec((h, REL_PAD), lambda i: (0, 0)),
        jax.ShapeDtypeStruct((h, REL_PAD), F32),
        vmem_bytes=4 * _nbytes((PAIR_ROWS, h, DIAGONALS), F32),
    )(db_t)
    return out[:, :N_REL]


def _head_masks():
    lane = lax.broadcasted_iota(jnp.int32, (Q_BLOCK, HEAD_PAIR), 1)
    return lane < HEAD_DIM, lane >= HEAD_DIM


def _block_scores(qm, kb, bias, valid):
    s = lax.dot_general(qm, kb, NT, preferred_element_type=F32) + bias
    return s if valid is None else jnp.where(valid, s, NEG_INF)


def _softmax_rows(s):
    e = jnp.exp(s - jnp.max(s, axis=-1, keepdims=True))
    return e * (1.0 / jnp.sum(e, axis=-1, keepdims=True))


def _padded_then_plain(step, n_blocks):
    n_padded = min(LEFT // Q_BLOCK, n_blocks)
    lax.fori_loop(0, n_padded, lambda j, c: step(j, c, True), 0)
    lax.fori_loop(n_padded, n_blocks, lambda j, c: step(j, c, False), 0, unroll=ATTN_UNROLL)


def _attn_fwd(q, kvpad, bias, name):
    t, d = q.shape
    n_pairs = d // HEAD_PAIR
    n_blocks = t // Q_BLOCK

    def body(q_ref, k_ref, v_ref, b_ref, o_ref):
        masks = _head_masks()
        key = lax.broadcasted_iota(jnp.int32, (Q_BLOCK, K_BLOCK), 1)

        def step(j, carry, padded):
            r0 = pl.multiple_of(j * Q_BLOCK, Q_BLOCK)
            q2 = q_ref[pl.ds(r0, Q_BLOCK), :].astype(F32)
            kb = k_ref[pl.ds(r0, K_BLOCK), :]
            vb = v_ref[pl.ds(r0, K_BLOCK), :]
            valid = key >= LEFT - j * Q_BLOCK if padded else None
            scores = [_block_scores(jnp.where(masks[a], q2, 0.0).astype(BF16), kb, b_ref[a], valid) for a in range(2)]
            probs = [_softmax_rows(s).astype(BF16) for s in scores]
            outs = [jnp.dot(p, vb, preferred_element_type=F32) for p in probs]
            o_ref[pl.ds(r0, Q_BLOCK), :] = jnp.where(masks[0], outs[0], outs[1]).astype(BF16)
            return carry

        _padded_then_plain(step, n_blocks)

    return _call(
        body, name, (n_pairs,),
        [pl.BlockSpec((t, HEAD_PAIR), lambda p: (0, p)),
         pl.BlockSpec((LEFT + t, HEAD_PAIR), lambda p: (0, p)),
         pl.BlockSpec((LEFT + t, HEAD_PAIR), lambda p: (0, n_pairs + p)),
         pl.BlockSpec((2, Q_BLOCK, K_BLOCK), lambda p: (p, 0, 0))],
        pl.BlockSpec((t, HEAD_PAIR), lambda p: (0, p)),
        jax.ShapeDtypeStruct((t, d), BF16),
        vmem_bytes=8 * _nbytes((LEFT + t, HEAD_PAIR), BF16) + 12 * _nbytes((2, Q_BLOCK, K_BLOCK), F32),
    )(q, kvpad, kvpad, bias)


def _attn_bwd(q, kvpad, bias, do, dk_in, dv_in, name):
    t, d = q.shape
    n_pairs = d // HEAD_PAIR
    n_blocks = t // Q_BLOCK
    has_in = dk_in is not None

    def body(*refs):
        refs = list(refs)
        q_ref, k_ref, v_ref, b_ref, do_ref = refs[:5]
        refs = refs[5:]
        if has_in:
            dki_ref, dvi_ref = refs[:2]
            refs = refs[2:]
        dq_ref, dk_ref, dv_ref, db_ref = refs
        masks = _head_masks()
        key = lax.broadcasted_iota(jnp.int32, (Q_BLOCK, K_BLOCK), 1)
        if has_in:
            dk_ref[...] = dki_ref[...]
            dv_ref[...] = dvi_ref[...]
        else:
            dk_ref[...] = jnp.zeros_like(dk_ref)
            dv_ref[...] = jnp.zeros_like(dv_ref)
        db_ref[...] = jnp.zeros_like(db_ref)

        def step(j, carry, padded):
            r0 = pl.multiple_of(j * Q_BLOCK, Q_BLOCK)
            q2 = q_ref[pl.ds(r0, Q_BLOCK), :].astype(F32)
            do2 = do_ref[pl.ds(r0, Q_BLOCK), :].astype(F32)
            kb = k_ref[pl.ds(r0, K_BLOCK), :]
            vb = v_ref[pl.ds(r0, K_BLOCK), :]
            valid = key >= LEFT - j * Q_BLOCK if padded else None
            heads = range(2)
            qms = [jnp.where(masks[a], q2, 0.0).astype(BF16) for a in heads]
            doms = [jnp.where(masks[a], do2, 0.0).astype(BF16) for a in heads]
            scores = [_block_scores(qms[a], kb, b_ref[a], valid) for a in heads]
            dps = [lax.dot_general(doms[a], vb, NT, preferred_element_type=F32) for a in heads]
            ps = [_softmax_rows(s) for s in scores]
            dss = [ps[a] * (dps[a] - jnp.sum(dps[a] * ps[a], axis=-1, keepdims=True)) for a in heads]
            for a in heads:
                for pair in range(PAIRS_PER_BLOCK):
                    lo = pair * PAIR_ROWS
                    db_ref[a, :, pl.ds(0, PAIR_BAND)] += dss[a][lo:lo + PAIR_ROWS, lo:lo + PAIR_BAND]
            dsbs = [ds.astype(BF16) for ds in dss]
            pbs = [p.astype(BF16) for p in ps]
            dqs = [jnp.dot(dsbs[a], kb, preferred_element_type=F32) for a in heads]
            dk_acc = sum(lax.dot_general(dsbs[a], qms[a], TN, preferred_element_type=F32) for a in heads)
            dv_acc = sum(lax.dot_general(pbs[a], doms[a], TN, preferred_element_type=F32) for a in heads)
            dq = jnp.where(masks[0], dqs[0], dqs[1]) * ATTN_SCALE
            dq_ref[pl.ds(r0, Q_BLOCK), :] = dq.astype(BF16)
            dk_ref[pl.ds(r0, K_BLOCK), :] += dk_acc
            dv_ref[pl.ds(r0, K_BLOCK), :] += dv_acc
            return carry

        _padded_then_plain(step, n_blocks)

    q_spec = pl.BlockSpec((t, HEAD_PAIR), lambda p: (0, p))
    kv_spec = pl.BlockSpec((LEFT + t, HEAD_PAIR), lambda p: (0, p))
    operands = [q, kvpad, kvpad, bias, do]
    in_specs = [q_spec, kv_spec, pl.BlockSpec((LEFT + t, HEAD_PAIR), lambda p: (0, n_pairs + p)),
                pl.BlockSpec((2, Q_BLOCK, K_BLOCK), lambda p: (p, 0, 0)), q_spec]
    aliases = None
    if has_in:
        operands += [dk_in, dv_in]
        in_specs += [kv_spec, kv_spec]
        aliases = {5: 1, 6: 2}
    return _call(
        body, name, (n_pairs,),
        in_specs,
        [q_spec, kv_spec, kv_spec, pl.BlockSpec((2, PAIR_ROWS, DIAGONALS), lambda p: (p, 0, 0))],
        [jax.ShapeDtypeStruct((t, d), BF16),
         jax.ShapeDtypeStruct((LEFT + t, d), F32),
         jax.ShapeDtypeStruct((LEFT + t, d), F32),
         jax.ShapeDtypeStruct((d // HEAD_DIM, PAIR_ROWS, DIAGONALS), F32)],
        vmem_bytes=10 * _nbytes((LEFT + t, HEAD_PAIR), BF16) + 8 * _nbytes((LEFT + t, HEAD_PAIR), F32)
        + 16 * _nbytes((2, Q_BLOCK, K_BLOCK), F32),
        aliases=aliases,
    )(*operands)


def _loss_head(x, g, target, name):
    t, d = x.shape
    tm = _tile(t, 512)

    def body(x_ref, g_ref, t_ref, dx_ref, loss_ref, dg_ref):
        @pl.when(pl.program_id(0) == 0)
        def _():
            loss_ref[...] = jnp.zeros_like(loss_ref)
            dg_ref[...] = jnp.zeros_like(dg_ref)

        xf = x_ref[...]
        r = lax.rsqrt(jnp.mean(xf * xf, axis=-1, keepdims=True) + EPS)
        xhat = xf * r
        diff = xhat * g_ref[...] - t_ref[...]
        row_loss = jnp.mean(diff * diff, axis=-1, keepdims=True)
        loss_ref[...] += 0.5 * jnp.sum(row_loss, axis=0, keepdims=True)
        dy = diff * (1.0 / d)
        dg_ref[...] += jnp.sum(dy * xhat, axis=0, keepdims=True)
        dxhat = dy * g_ref[...]
        dx_ref[...] = r * (dxhat - xhat * jnp.mean(dxhat * xhat, axis=-1, keepdims=True))

    row = pl.BlockSpec((tm, d), lambda i: (i, 0))
    vec = pl.BlockSpec((1, d), lambda i: (0, 0))
    dx, loss, dg = _call(
        body, name, (t // tm,),
        [row, vec, row],
        [row, pl.BlockSpec((1, 1), lambda i: (0, 0)), vec],
        [jax.ShapeDtypeStruct((t, d), F32), jax.ShapeDtypeStruct((1, 1), F32), jax.ShapeDtypeStruct((1, d), F32)],
        vmem_bytes=10 * _nbytes((tm, d), F32),
    )(x, g.reshape(1, d), target)
    return dx, loss[0, 0], dg.reshape(d)


def _adamw_store(g, w_ref, m_ref, v_ref, g_ref, d_ref, nm_ref, nv_ref):
    c1 = 1.0 / (1.0 - ADAM_B1 ** ADAM_STEP)
    c2 = 1.0 / (1.0 - ADAM_B2 ** ADAM_STEP)
    nm = ADAM_B1 * m_ref[...] + (1.0 - ADAM_B1) * g
    nv = ADAM_B2 * v_ref[...] + (1.0 - ADAM_B2) * (g * g)
    g_ref[...] = g
    nm_ref[...] = nm
    nv_ref[...] = nv
    d_ref[...] = -ADAM_LR * ((nm * c1) / (jnp.sqrt(nv * c2) + ADAM_EPS) + ADAM_WD * w_ref[...])


def _adamw_layer(recv, own, w, m, v, layer, prev, me, name):
    n_src, r, c = recv.shape
    tr = _row_tile(r, max(BF16_SUBLANES, ADAMW_BLOCK_ELEMS // c), BF16_SUBLANES)
    first = prev is None

    def body(me_ref, recv_ref, own_ref, w_ref, m_ref, v_ref, *rest):
        mine = me_ref[0]
        own_part = own_ref[...].astype(F32)
        g = None
        for s in range(n_src):
            part = jnp.where(mine == s, own_part, recv_ref[s].astype(F32))
            g = part if g is None else g + part
        _adamw_store(g, w_ref, m_ref, v_ref, *rest[-4:])

    blk = pl.BlockSpec((None, tr, c), lambda i, me_ref: (layer, i, 0))
    any_spec = pl.BlockSpec(memory_space=pl.ANY)
    out = jax.ShapeDtypeStruct(w.shape, F32)
    operands = [me, recv, own, w, m, v] + ([] if first else list(prev))
    vmem = 2 * _nbytes((n_src + 1, tr, c), BF16) + 18 * _nbytes((tr, c), F32)
    return pl.pallas_call(
        body,
        name=name,
        grid_spec=pltpu.PrefetchScalarGridSpec(
            num_scalar_prefetch=1,
            grid=(r // tr,),
            in_specs=[pl.BlockSpec((n_src, tr, c), lambda i, me_ref: (0, i, 0)),
                      pl.BlockSpec((None, tr, c), lambda i, me_ref: (me_ref[0], i, 0)),
                      blk, blk, blk] + ([] if first else [any_spec] * 4),
            out_specs=[blk, blk, blk, blk],
        ),
        out_shape=[out, out, out, out],
        input_output_aliases={} if first else {6 + j: j for j in range(4)},
        compiler_params=pltpu.CompilerParams(
            dimension_semantics=("arbitrary",),
            vmem_limit_bytes=int(min(max(VMEM_FLOOR_BYTES, vmem * 5 // 4), VMEM_CEIL_BYTES))),
    )(*operands)


def _adamw(parts, w, m, v, name):
    n_layers, n_src, r, c = parts.shape
    mult = BF16_SUBLANES if parts.dtype == BF16 else F32_SUBLANES
    tr = _row_tile(r, max(mult, ADAMW_BLOCK_ELEMS // c), mult)

    def body(p_ref, w_ref, m_ref, v_ref, g_ref, d_ref, nm_ref, nv_ref):
        g = p_ref[0].astype(F32)
        for s in range(1, n_src):
            g = g + p_ref[s].astype(F32)
        _adamw_store(g, w_ref, m_ref, v_ref, g_ref, d_ref, nm_ref, nv_ref)

    blk = pl.BlockSpec((None, tr, c), lambda l, i: (l, i, 0))
    out = jax.ShapeDtypeStruct((n_layers, r, c), F32)
    return _call(
        body, name, (n_layers, r // tr),
        [pl.BlockSpec((None, n_src, tr, c), lambda l, i: (l, 0, i, 0)), blk, blk, blk],
        [blk, blk, blk, blk],
        [out, out, out, out],
        vmem_bytes=2 * _nbytes((n_src, tr, c), parts.dtype) + 18 * _nbytes((tr, c), F32),
    )(parts, w, m, v)


def _ordered_sum(parts, name):
    n_src, r, c = parts.shape

    def body(p_ref, o_ref):
        acc = p_ref[0]
        for s in range(1, n_src):
            acc = acc + p_ref[s]
        o_ref[...] = acc

    return _call(
        body, name, (1,),
        [pl.BlockSpec((n_src, r, c), lambda i: (0, 0, 0))],
        pl.BlockSpec((r, c), lambda i: (0, 0)),
        jax.ShapeDtypeStruct((r, c), F32),
        vmem_bytes=4 * _nbytes((n_src, r, c), F32),
    )(parts)


def _position():
    return lax.axis_index("x"), lax.axis_index("y"), lax.axis_index("c")


def _linear(p):
    return 4 * p[0] + 2 * p[1] + p[2]


def _all_gather(shards, name):
    n = len(shards)

    def body(*refs):
        ins, outs = refs[:n], refs[n:2 * n]
        send_sems, recv_sems, local_sems = refs[2 * n:]
        x, y, c = _position()
        me, sibling = (x, y, c), (x, y, 1 - c)
        chips = [(1 - x, y), (x, 1 - y), (1 - x, 1 - y)]

        def slab(t, p):
            return outs[t].at[:, _linear(p)]

        def copy(t, k, block, to, src=None):
            return pltpu.make_async_remote_copy(
                src_ref=slab(t, block) if src is None else src,
                dst_ref=slab(t, block),
                send_sem=send_sems.at[t, k],
                recv_sem=recv_sems.at[t, k],
                device_id=to,
                device_id_type=MESH,
            )

        started = []
        for t in range(n):
            mine = pltpu.make_async_copy(ins[t], slab(t, me), local_sems.at[t])
            mine.start()
            started.append(mine)
        sends = []
        for t in range(n):
            first = [copy(t, 0, me, sibling, src=ins[t])]
            first += [copy(t, 1 + j, me, (*chip, c), src=ins[t]) for j, chip in enumerate(chips)]
            for cp in first:
                cp.start()
            sends += first
        for t in range(n):
            for j, chip in enumerate(chips):
                copy(t, 1 + j, (*chip, c), me).wait_recv()
                passed = copy(t, 4 + j, (*chip, c), sibling)
                passed.start()
                sends.append(passed)
        for t in range(n):
            copy(t, 0, sibling, me).wait_recv()
            for j, chip in enumerate(chips):
                copy(t, 4 + j, (*chip, 1 - c), me).wait_recv()
        for cp in sends:
            cp.wait_send()
        for mine in started:
            mine.wait()

    out_shape = [jax.ShapeDtypeStruct((s.shape[0], N_DEV) + s.shape[1:], s.dtype) for s in shards]
    return pl.pallas_call(
        body,
        name=name,
        in_specs=[HBM_SPEC] * n,
        out_specs=[HBM_SPEC] * n,
        out_shape=out_shape,
        scratch_shapes=[
            pltpu.SemaphoreType.DMA((n, N_DEV - 1)),
            pltpu.SemaphoreType.DMA((n, N_DEV - 1)),
            pltpu.SemaphoreType.DMA((n,)),
        ],
    )(*shards)


def _exchange(blocks, name):
    n = len(blocks)

    def body(*refs):
        ins, outs = refs[:n], refs[n:2 * n]
        send_sems, recv_sems, local_sems = refs[2 * n:]
        x, y, c = _position()
        me = _linear((x, y, c))
        flips = [(fx, fy, fc) for fx in (0, 1) for fy in (0, 1) for fc in (0, 1)][1:]

        def peer_of(flip):
            fx, fy, fc = flip
            return (1 - x if fx else x, 1 - y if fy else y, 1 - c if fc else c)

        def copy(t, k, peer):
            return pltpu.make_async_remote_copy(
                src_ref=ins[t].at[:, _linear(peer)],
                dst_ref=outs[t].at[:, me],
                send_sem=send_sems.at[t, k],
                recv_sem=recv_sems.at[t, k],
                device_id=peer,
                device_id_type=MESH,
            )

        def arrival(t, k, peer):
            return pltpu.make_async_remote_copy(
                src_ref=ins[t].at[:, _linear(peer)],
                dst_ref=outs[t].at[:, _linear(peer)],
                send_sem=send_sems.at[t, k],
                recv_sem=recv_sems.at[t, k],
                device_id=peer,
                device_id_type=MESH,
            )

        own = []
        for t in range(n):
            cp = pltpu.make_async_copy(ins[t].at[:, me], outs[t].at[:, me], local_sems.at[t])
            cp.start()
            own.append(cp)
        sends = []
        for t in range(n):
            for k, flip in enumerate(flips):
                cp = copy(t, k, peer_of(flip))
                cp.start()
                sends.append(cp)
        for t in range(n):
            for k, flip in enumerate(flips):
                arrival(t, k, peer_of(flip)).wait_recv()
        for cp in sends:
            cp.wait_send()
        for cp in own:
            cp.wait()

    out_shape = [jax.ShapeDtypeStruct(b.shape, b.dtype) for b in blocks]
    return pl.pallas_call(
        body,
        name=name,
        in_specs=[HBM_SPEC] * n,
        out_specs=[HBM_SPEC] * n,
        out_shape=out_shape,
        scratch_shapes=[
            pltpu.SemaphoreType.DMA((n, N_DEV - 1)),
            pltpu.SemaphoreType.DMA((n, N_DEV - 1)),
            pltpu.SemaphoreType.DMA((n,)),
        ],
    )(*blocks)


def _peers():
    x, y, c = _position()
    flips = [(fx, fy, fc) for fx in (0, 1) for fy in (0, 1) for fc in (0, 1)][1:]
    return [(1 - x if fx else x, 1 - y if fy else y, 1 - c if fc else c) for fx, fy, fc in flips]


SIBLING, OTHER_CHIPS = (0,), (1, 3, 5)
COPY_PEERS = {"gather": tuple(range(N_DEV - 1)), "exchange": tuple(range(N_DEV - 1)),
              "chips": SIBLING + OTHER_CHIPS, "forward": OTHER_CHIPS}


def _split_copy(kind, src_ref, land_ref, k, send_sem, recv_sem, starting):
    peers = _peers()
    peer = peers[SIBLING[0]] if kind == "forward" else peers[k]
    me = _linear(_position())
    if kind == "forward":
        slab = _linear(peers[k]) if starting else 0
        src, dst = land_ref.at[slab], land_ref.at[slab]
    elif kind == "exchange":
        src, dst = src_ref.at[_linear(peer) if starting else 0], land_ref.at[me if starting else 0]
    else:
        src, dst = src_ref, land_ref.at[me if starting else 0]
    return pltpu.make_async_remote_copy(src_ref=src, dst_ref=dst, send_sem=send_sem, recv_sem=recv_sem,
                                        device_id=peer, device_id_type=MESH)


def _split_start(groups, carry, name):
    arrays = [a for _, srcs, lands in groups for a in list(srcs) + list(lands)] + [carry]

    def body(*refs):
        ins, sems = refs[:len(arrays)], refs[len(arrays):len(arrays) + 2 * len(groups)]
        at = 0
        for g, (kind, srcs, lands) in enumerate(groups):
            src_refs, land_refs = ins[at:at + len(srcs)], ins[at + len(srcs):at + len(srcs) + len(lands)]
            at += len(srcs) + len(lands)
            peers = COPY_PEERS[kind]
            for t in range(len(lands)):
                for slot, k in enumerate(peers):
                    sem = t * len(peers) + slot
                    _split_copy(kind, src_refs[t] if srcs else None, land_refs[t], k,
                                sems[2 * g].at[sem], sems[2 * g + 1].at[sem], True).start()

    sem_shapes = [pltpu.SemaphoreType.DMA((len(lands) * len(COPY_PEERS[kind]),))
                  for kind, _, lands in groups for _ in range(2)]
    out = pl.pallas_call(
        body,
        name=name,
        in_specs=[HBM_SPEC] * len(arrays),
        out_specs=[SEM_SPEC] * len(sem_shapes) + [HBM_SPEC] * len(arrays),
        out_shape=sem_shapes + [pltpu.HBM(a.shape, a.dtype) for a in arrays],
        input_output_aliases={i: len(sem_shapes) + i for i in range(len(arrays))},
        compiler_params=pltpu.CompilerParams(has_side_effects=pltpu.SideEffectType.DATAFLOW_SIDE_EFFECTING),
    )(*[pltpu.with_memory_space_constraint(a, pltpu.HBM) for a in arrays])
    sems, thru = out[:len(sem_shapes)], out[len(sem_shapes):]
    started, at = [], 0
    for g, (kind, srcs, lands) in enumerate(groups):
        n_s, n_l = len(srcs), len(lands)
        started.append((kind, sems[2 * g], sems[2 * g + 1], thru[at:at + n_s], thru[at + n_s:at + n_s + n_l]))
        at += n_s + n_l
    return started, thru[-1]


def _split_wait(started, after, name):
    kind, send_sems, recv_sems, srcs, lands = started
    n_s, n_l = len(srcs), len(lands)
    peers = COPY_PEERS[kind]

    def body(*refs):
        src_refs, land_refs = refs[:n_s], refs[n_s:n_s + n_l]
        send_ref, recv_ref = refs[n_s + n_l], refs[n_s + n_l + 1]
        for t in range(n_l):
            for slot, k in enumerate(peers):
                sem = t * len(peers) + slot
                copy = _split_copy(kind, src_refs[t] if n_s else None, land_refs[t], k,
                                   send_ref.at[sem], recv_ref.at[sem], False)
                copy.wait_send()
                copy.wait_recv()

    arrays = list(srcs) + list(lands)
    out = pl.pallas_call(
        body,
        name=name,
        in_specs=[HBM_SPEC] * len(arrays) + [SEM_SPEC, SEM_SPEC, pl.BlockSpec(memory_space=pl.ANY)],
        out_specs=[HBM_SPEC] * len(arrays),
        out_shape=[pltpu.HBM(a.shape, a.dtype) for a in arrays],
        input_output_aliases={i: i for i in range(len(arrays))},
        compiler_params=pltpu.CompilerParams(has_side_effects=pltpu.SideEffectType.DATAFLOW_SIDE_EFFECTING),
    )(*arrays, send_sems, recv_sems, after)
    return out[:n_s], out[n_s:]


def _pack(arrays, row_multiple):
    flat = jnp.concatenate([a.reshape(-1) for a in arrays])
    quantum = row_multiple * FLAT_LANES
    padded = -(-flat.shape[0] // quantum) * quantum
    return jnp.pad(flat, (0, padded - flat.shape[0])).reshape(-1, FLAT_LANES)


def _unpack(flat, like):
    flat = flat.reshape(-1)
    out, at = [], 0
    for a in like:
        size = math.prod(a.shape)
        out.append(flat[at:at + size].reshape(a.shape))
        at += size
    return out


def kernel(x, a_norm, a_w_in, a_sgu_norm, a_w_spatial, a_b_spatial, a_w_out, kv_norm, w_kv, b_norm, b_w_q, b_rel_bias, b_w_o, ffn_norm, ffn_w_gate_up, ffn_w_down, final_norm, loss_target, m_a_norm, m_a_w_in, m_a_sgu_norm, m_a_w_spatial, m_a_b_spatial, m_a_w_out, m_kv_norm, m_w_kv, m_b_norm, m_b_w_q, m_b_rel_bias, m_b_w_o, m_ffn_norm, m_ffn_w_gate_up, m_ffn_w_down, m_final_norm, v_a_norm, v_a_w_in, v_a_sgu_norm, v_a_w_spatial, v_a_b_spatial, v_a_w_out, v_kv_norm, v_w_kv, v_b_norm, v_b_w_q, v_b_rel_bias, v_b_w_o, v_ffn_norm, v_ffn_w_gate_up, v_ffn_w_down, v_final_norm):
    xs = x[0]
    target = loss_target[0]
    t, d = xs.shape
    n_a = a_w_in.shape[0]
    n_b = b_w_q.shape[0]
    depth = ffn_w_gate_up.shape[0]
    f_a = a_w_out.shape[1] * N_DEV
    gd = f_a // A_GROUPS
    nb_ffn = ffn_w_gate_up.shape[2]
    me = _linear(_position())

    small_rows = -(-(a_norm.size + a_sgu_norm.size) // (8 * 128)) * 8
    small = jnp.pad(jnp.concatenate([a_norm.reshape(-1), a_sgu_norm.reshape(-1)]),
                    (0, small_rows * 128 - a_norm.size - a_sgu_norm.size)).reshape(1, small_rows, 128)

    def shard(w, layer=None):
        return (w if layer is None else w[layer]).astype(BF16)

    stages = []
    for layer in range(depth):
        if layer == 0:
            stages += [("a0", [shard(a_w_in, 0)]), ("a0_out", [shard(a_w_out, 0)])]
        elif layer < n_a:
            stages.append((f"a{layer}", [shard(a_w_in, layer), shard(a_w_out, layer)]))
        else:
            i = layer - n_a
            shared = [shard(w_kv)] if i == 0 else []
            stages.append((f"b{i}", shared + [shard(b_w_q, i), shard(b_w_o, i)]))
        stages.append((f"f{layer}", [shard(ffn_w_gate_up, layer), shard(ffn_w_down, layer)]))
    first = _all_gather([s[None] for s in stages[0][1]] + [small], "gather_first")
    gathered = {stages[0][0]: [g[0] for g in first[:-1]]}
    small_g = first[-1].reshape(N_DEV, -1)
    a_norm_full = small_g[:, :a_norm.size].reshape(N_DEV, n_a, -1).transpose(1, 0, 2).reshape(n_a, d)
    a_sgu_full = small_g[:, a_norm.size:a_norm.size + a_sgu_norm.size].reshape(
        N_DEV, n_a, -1).transpose(1, 0, 2).reshape(n_a, f_a)
    two_level = ("f0", "a1", "f1")
    later = [("chips" if key in two_level else "gather", shards,
              [lax.dynamic_update_slice(lax.empty((N_DEV,) + s.shape, BF16), s[None], (me, 0, 0)) for s in shards])
             for key, shards in stages[1:]]
    started, a_norm_full = _split_start(later, a_norm_full, "gather_start")
    in_flight = {key: group for (key, _), group in zip(stages[1:], started)}

    def pass_on(key, carry):
        if key in two_level and key in in_flight and in_flight[key][0] == "chips":
            _, lands = _split_wait(in_flight.pop(key), carry, f"gather_wait_{key}_chips")
            (in_flight[key],), carry = _split_start([("forward", [], lands)], carry, f"gather_pass_on_{key}")
        return carry

    def weights(key, after):
        if key not in gathered:
            _, gathered[key] = _split_wait(in_flight.pop(key), after, f"gather_wait_{key}")
        return gathered[key]

    rows_down = ffn_w_down.shape[1]

    def mixer_a_weights(i, after):
        if i == 0:
            (w_in,), (w_out,) = weights("a0", after[0]), weights("a0_out", after[1])
        else:
            w_in, w_out = weights(f"a{i}", after[0])
        return w_in[None], w_out.reshape(1, f_a, d)

    def mixer_b_weights(i, after):
        ws = weights(f"b{i}", after)
        return ws[-2].reshape(1, d, d), ws[-1].reshape(1, d, d)

    def ffn_weights(layer, after):
        w_gu, w_dn = weights(f"f{layer}", after)
        return w_gu[None], w_dn.reshape(1, N_DEV // 2, 2 * rows_down, d)

    w_sp_t = jnp.swapaxes(a_w_spatial, -1, -2)
    b_full = jnp.repeat(jnp.swapaxes(a_b_spatial, -1, -2), gd, axis=-1)

    saved = []

    def ffn_fwd(xin, layer):
        hf = _rms_fwd(xin, ffn_norm[layer], f"ffn_norm_fwd_{layer}")
        w_gu, w_dn = ffn_weights(layer, xin)
        dact, act = _ffn_gate_up(f"ffn_gate_up_{layer}", hf, w_gu, 0)
        act = pass_on(f"a{layer + 1}", act)
        xout = _mm_down(f"ffn_down_{layer}", act, w_dn, 0, xin)
        return xout, (xin, hf, dact, act)

    for i in range(n_a):
        h = _rms_fwd(xs, a_norm_full[i], f"a_norm_fwd_{i}")
        zpre = _mm_colblock(f"a_in_{i}", h, weights(f"a{i}", xs)[0][None], 0)
        p, zs, dgs = _sgu_fwd(zpre, a_sgu_full[i], a_w_spatial[i], b_full[i], f"a_sgu_fwd_{i}")
        p = pass_on(f"f{i}", p)
        w_in, w_out = mixer_a_weights(i, (xs, p))
        x_mid = _mm_natural(f"a_out_{i}", p, w_out, 0, res=xs)
        x_out, ffn_saved = ffn_fwd(x_mid, i)
        saved.append((xs, h, zs, dgs, p, ffn_saved))
        xs = x_out

    x_kv = xs
    w_kv_g = weights("b0", x_kv)[0][None]
    h_kv = _rms_fwd(x_kv, kv_norm, "kv_norm_fwd")
    kv = _mm_colblock("kv_proj", h_kv, w_kv_g, 0)
    kvpad = jnp.pad(kv, ((LEFT, 0), (0, 0)))

    biases = [_bias_block(_bias_build(b_rel_bias[i], f"rel_bias_{i}")) for i in range(n_b)]
    for i in range(n_b):
        layer = n_a + i
        w_q, w_o = mixer_b_weights(i, xs)
        hb = _rms_fwd(xs, b_norm[i], f"b_norm_fwd_{i}")
        q = _mm_natural(f"b_q_{i}", hb, w_q, 0, out_dtype=BF16, scale=ATTN_SCALE)
        o = _attn_fwd(q, kvpad, biases[i], f"b_attn_fwd_{i}")
        x_mid = _mm_natural(f"b_o_{i}", o, w_o, 0, res=xs)
        x_out, ffn_saved = ffn_fwd(x_mid, layer)
        saved.append((xs, hb, q, o, ffn_saved))
        xs = x_out

    dx, loss_local, g_final = _loss_head(xs, final_norm, target, "loss_head")
    loss = lax.psum(loss_local, ("x", "y", "c"))

    big_grads = {}
    pending = []
    in_flight_grads = []

    def start_exchange(dx, tag):
        srcs = [big_grads[key] for key in pending]
        lands = [lax.empty(s.shape, BF16) for s in srcs]
        (group,), dx = _split_start([("exchange", srcs, lands)], dx, f"exchange_start_{tag}")
        in_flight_grads.append((list(pending), group, tag))
        pending.clear()
        return dx

    g_ffn_norm = [None] * depth
    g_a_norm = [None] * n_a
    g_a_sgu = [None] * n_a
    g_w_sp = [None] * n_a
    g_b_sp = [None] * n_a
    g_b_norm = [None] * n_b
    g_rel = [None] * n_b

    def ffn_bwd(dx, layer, ffn_saved):
        eager = layer < n_a
        xin, hf, dact, act = ffn_saved
        big_grads["ffn_w_down", layer] = _mm_dw_down(f"ffn_down_dw_{layer}", act, dx)
        pending.append(("ffn_w_down", layer))
        if eager:
            dx = start_exchange(dx, f"f{layer}_down")
        w_gu, w_dn = ffn_weights(layer, xin)
        dgu = _ffn_down_dx(f"ffn_down_dx_{layer}", dx, w_dn, 0, dact).reshape(N_DEV, t, nb_ffn)
        big_grads["ffn_w_gate_up", layer] = _mm_dw_colblock(
            f"ffn_gate_up_dw_{layer}", hf, dgu, blocked_in=True, transposed=True)
        pending.append(("ffn_w_gate_up", layer))
        if eager:
            dx = start_exchange(dx, f"f{layer}_gate_up")
        dx, g_ffn_norm[layer] = _mm_t_colblock_norm_bwd(
            f"ffn_gate_up_dx_{layer}", dgu, w_gu, 0, xin, ffn_norm[layer], dx, blocked_in=True)
        return dx

    dk = dv = None
    for i in reversed(range(n_b)):
        layer = n_a + i
        x_in, hb, q, o, ffn_saved = saved[layer]
        dx = ffn_bwd(dx, layer, ffn_saved)
        big_grads["b_w_o", i] = _mm_dw_natural(f"b_o_dw_{i}", o, dx)
        w_q, w_o = mixer_b_weights(i, x_in)
        do = _mm_t_natural(f"b_o_dx_{i}", dx, w_o, 0)
        dq, dk, dv, dbias = _attn_bwd(q, kvpad, biases[i], do, dk, dv, f"b_attn_bwd_{i}")
        g_rel[i] = _bias_grad(dbias, f"rel_bias_grad_{i}")
        big_grads["b_w_q", i] = _mm_dw_natural(f"b_q_dw_{i}", hb, dq)
        pending.extend([("b_w_o", i), ("b_w_q", i)])
        dx, g_b_norm[i] = _mm_t_natural_norm_bwd(f"b_q_dx_{i}", dq, w_q, 0, x_in, b_norm[i], dx)
        if i > 0:
            dx = start_exchange(dx, f"b{i}")

    dkv = jnp.concatenate([dk[LEFT:], dv[LEFT:]], axis=1).astype(BF16)
    big_grads["w_kv", 0] = _mm_dw_colblock("kv_proj_dw", h_kv, dkv)
    pending.append(("w_kv", 0))
    dx, g_kv_norm = _mm_t_colblock_norm_bwd("kv_proj_dx", dkv, w_kv_g, 0, x_kv, kv_norm, dx)
    dx = start_exchange(dx, "kv")

    for i in reversed(range(n_a)):
        x_in, h, zs, dgs, p, ffn_saved = saved[i]
        dx = ffn_bwd(dx, i, ffn_saved)
        big_grads["a_w_out", i] = _mm_dw_natural(f"a_out_dw_{i}", p, dx)
        pending.append(("a_w_out", i))
        dx = start_exchange(dx, f"a{i}_out")
        w_in, w_out = mixer_a_weights(i, (x_in, p))
        dp = _mm_t_natural(f"a_out_dx_{i}", dx, w_out, 0)
        dz, g_w_sp[i], g_b_sp[i], g_a_sgu[i] = _sgu_bwd(
            zs, dgs, dp, a_sgu_full[i], a_w_spatial[i], w_sp_t[i], b_full[i], f"a_sgu_bwd_{i}")
        big_grads["a_w_in", i] = _mm_dw_colblock(f"a_in_dw_{i}", h, dz)
        pending.append(("a_w_in", i))
        dx = start_exchange(dx, f"a{i}_in")
        dx, g_a_norm[i] = _mm_t_colblock_norm_bwd(f"a_in_dx_{i}", dz, w_in, 0, x_in, a_norm_full[i], dx)
    grad_x = dx[None]

    small_like = [jax.ShapeDtypeStruct((n_a, d), F32), jax.ShapeDtypeStruct((n_a, f_a), F32),
                  a_w_spatial, a_b_spatial, kv_norm, b_norm, b_rel_bias, ffn_norm, final_norm]
    small_partial = _pack(
        [jnp.stack(g_a_norm), jnp.stack(g_a_sgu), jnp.stack(g_w_sp), jnp.stack(g_b_sp), g_kv_norm,
         jnp.stack(g_b_norm), jnp.stack(g_rel), jnp.stack(g_ffn_norm), g_final], N_DEV * 8)
    chunk_rows = small_partial.shape[0] // N_DEV
    arrived = {}
    for keys, group, tag in in_flight_grads:
        srcs, lands = _split_wait(group, dx, f"exchange_wait_{tag}")
        for key, src, land in zip(keys, srcs, lands):
            arrived[key] = (land, src)
    small_got = _exchange([small_partial.reshape(1, N_DEV, chunk_rows, FLAT_LANES)], "exchange_small")[0]
    small_sum = _ordered_sum(small_got[0], "small_grad_sum")
    small_all = _all_gather([small_sum[None]], "gather_small_grads")[0]
    (ga_norm, ga_sgu, gw_sp, gb_sp, gkv_norm, gb_norm, g_relb, gffn_norm, gfinal) = _unpack(small_all, small_like)

    results = {}
    big_names = ["a_w_in", "a_w_out", "w_kv", "b_w_q", "b_w_o", "ffn_w_gate_up", "ffn_w_down"]
    big_wmv = [(a_w_in, m_a_w_in, v_a_w_in), (a_w_out, m_a_w_out, v_a_w_out),
               (w_kv[None], m_w_kv[None], v_w_kv[None]), (b_w_q, m_b_w_q, v_b_w_q), (b_w_o, m_b_w_o, v_b_w_o),
               tuple(jnp.swapaxes(a, 1, 2) for a in (ffn_w_gate_up, m_ffn_w_gate_up, v_ffn_w_gate_up)),
               (ffn_w_down, m_ffn_w_down, v_ffn_w_down)]
    me_arr = jnp.reshape(me, (1,)).astype(jnp.int32)
    for name, (w, m, v) in zip(big_names, big_wmv):
        outs = None
        for layer in range(w.shape[0]):
            got, own = arrived[name, layer]
            outs = _adamw_layer(got, own, w, m, v, layer, outs, me_arr, f"adamw_{name}_{layer}")
        if name == "w_kv":
            outs = [o[0] for o in outs]
        if name == "ffn_w_gate_up":
            outs = [jnp.swapaxes(o, 1, 2) for o in outs]
        results[name] = outs

    n_cols = a_norm.shape[1]
    s_cols = a_sgu_norm.shape[1]
    small_g_list = [lax.dynamic_slice(ga_norm, (0, me * n_cols), (n_a, n_cols)),
                    lax.dynamic_slice(ga_sgu, (0, me * s_cols), (n_a, s_cols)),
                    gw_sp, gb_sp, gkv_norm, gb_norm, g_relb, gffn_norm, gfinal]
    small_names = ["a_norm", "a_sgu_norm", "a_w_spatial", "a_b_spatial", "kv_norm", "b_norm", "b_rel_bias",
                   "ffn_norm", "final_norm"]
    small_w = [a_norm, a_sgu_norm, a_w_spatial, a_b_spatial, kv_norm, b_norm, b_rel_bias, ffn_norm, final_norm]
    small_m = [m_a_norm, m_a_sgu_norm, m_a_w_spatial, m_a_b_spatial, m_kv_norm, m_b_norm, m_b_rel_bias,
               m_ffn_norm, m_final_norm]
    small_v = [v_a_norm, v_a_sgu_norm, v_a_w_spatial, v_a_b_spatial, v_kv_norm, v_b_norm, v_b_rel_bias,
               v_ffn_norm, v_final_norm]
    flat_g = _pack(small_g_list, 8)
    flat_out = _adamw(flat_g[None, None], _pack(small_w, 8)[None], _pack(small_m, 8)[None],
                      _pack(small_v, 8)[None], "adamw_small")
    unpacked = [_unpack(o[0], small_w) for o in flat_out]
    for idx, name in enumerate(small_names):
        results[name] = [unpacked[kind][idx] for kind in range(4)]

    order = ["a_norm", "a_w_in", "a_sgu_norm", "a_w_spatial", "a_b_spatial", "a_w_out", "kv_norm", "w_kv",
             "b_norm", "b_w_q", "b_rel_bias", "b_w_o", "ffn_norm", "ffn_w_gate_up", "ffn_w_down", "final_norm"]
    outputs = [loss, grad_x]
    for kind in range(4):
        outputs += [results[name][kind] for name in order]
    return tuple(outputs)
```

```python
import math

import jax
import jax.numpy as jnp
from jax import lax
from jax.experimental import pallas as pl
from jax.experimental.pallas import tpu as pltpu

F32 = jnp.float32
BF16 = jnp.bfloat16
MESH = pl.DeviceIdType.MESH
HBM_SPEC = pl.BlockSpec(memory_space=pltpu.HBM)
SEM_SPEC = pl.BlockSpec(memory_space=pltpu.SEMAPHORE)

N_DEV = 8
CHUNK = 64
A_CHUNK = 128
A_GROUPS = 8
N_LEFT_CHUNKS = 8
LEFT = N_LEFT_CHUNKS * CHUNK
PAIR_ROWS = 2 * CHUNK
PAIR_BAND = PAIR_ROWS + LEFT
DIAGONALS = PAIR_BAND + PAIR_ROWS
PAIRS_PER_BLOCK = 2
Q_BLOCK = PAIRS_PER_BLOCK * PAIR_ROWS
K_BLOCK = Q_BLOCK + LEFT
ATTN_UNROLL = 2
MAX_REL = 256
N_REL = 2 * MAX_REL + 1
REL_PAD = 640
HEAD_DIM = 64
HEAD_PAIR = 2 * HEAD_DIM
ATTN_SCALE = HEAD_DIM ** -0.5
EPS = 1e-6
NEG_INF = -1e30
ADAM_LR = 0.001
ADAM_B1 = 0.9
ADAM_B2 = 0.999
ADAM_EPS = 1e-08
ADAM_WD = 0.01
ADAM_STEP = 10
FLAT_LANES = 1024
F32_SUBLANES = 8
BF16_SUBLANES = 16
ADAMW_BLOCK_ELEMS = 256 * 1024
V7X_VMEM_BYTES = 64 * 1024 * 1024
VMEM_FLOOR_BYTES = 32 * 1024 * 1024
VMEM_CEIL_BYTES = V7X_VMEM_BYTES - 8 * 1024 * 1024

NN = (((1,), (0,)), ((), ()))
NT = (((1,), (1,)), ((), ()))
TN = (((0,), (0,)), ((), ()))


def _tile(n, pref):
    return pref if n % pref == 0 else n


def _row_tile(n, pref, mult):
    best = None
    for t in range(mult, min(n, pref) + 1, mult):
        if n % t == 0:
            best = t
    return best if best is not None else n


def _nbytes(shape, dtype):
    n = 1
    for s in shape:
        if s is not None:
            n *= s
    return n * jnp.dtype(dtype).itemsize


def _call(body, name, grid, in_specs, out_specs, out_shape, scratch=(), vmem_bytes=0, aliases=None):
    limit = int(min(max(VMEM_FLOOR_BYTES, vmem_bytes * 5 // 4), VMEM_CEIL_BYTES))
    return pl.pallas_call(
        body,
        name=name,
        grid=grid,
        in_specs=in_specs,
        out_specs=out_specs,
        out_shape=out_shape,
        scratch_shapes=list(scratch),
        input_output_aliases=aliases or {},
        compiler_params=pltpu.CompilerParams(
            dimension_semantics=("arbitrary",) * len(grid), vmem_limit_bytes=limit),
    )


ERFC_P = 0.3275911 / math.sqrt(2.0)
ERFC_HALF_COEFFS = tuple(0.5 * a for a in (1.061405429, -1.453152027, 1.421413741, -0.284496736, 0.254829592))


def _gelu_and_grad(x):
    d = 1.0 + ERFC_P * jnp.abs(x)
    r = pl.reciprocal(d, approx=True)
    t = r * (2.0 - d * r)
    a5, a4, a3, a2, a1 = ERFC_HALF_COEFFS
    ex = jnp.exp(-0.5 * (x * x))
    tail = ((((a5 * t + a4) * t + a3) * t + a2) * t + a1) * t * ex
    cdf = jnp.where(x < 0, tail, 1.0 - tail)
    return x * cdf, cdf + x * ex * (1.0 / math.sqrt(2.0 * math.pi))


def _sigmoid(x):
    return 1.0 / (1.0 + jnp.exp(-x))


def _split3(x):
    hi = x.astype(BF16)
    r1 = x - hi.astype(F32)
    mid = r1.astype(BF16)
    lo = (r1 - mid.astype(F32)).astype(BF16)
    return hi, mid, lo


def _rms_fwd(x, g, name):
    t, d = x.shape
    tm = _tile(t, 512)

    def body(x_ref, g_ref, o_ref):
        xf = x_ref[...]
        r = lax.rsqrt(jnp.mean(xf * xf, axis=-1, keepdims=True) + EPS)
        o_ref[...] = (xf * r * g_ref[...]).astype(o_ref.dtype)

    return _call(
        body, name, (t // tm,),
        [pl.BlockSpec((tm, d), lambda i: (i, 0)), pl.BlockSpec((1, d), lambda i: (0, 0))],
        pl.BlockSpec((tm, d), lambda i: (i, 0)),
        jax.ShapeDtypeStruct((t, d), BF16),
        vmem_bytes=2 * (_nbytes((tm, d), F32) + _nbytes((tm, d), BF16)) + 4 * _nbytes((tm, d), F32),
    )(x, g.reshape(1, d))


def _mm(name, dims, a, b, *, grid, a_spec, b_spec, out_shape, out_spec, acc_shape,
        res=None, res_spec=None, scale=None):
    nk = grid[2]
    has_res = res is not None

    def body(*refs):
        refs = list(refs)
        a_ref = refs.pop(0)
        b_ref = refs.pop(0)
        r_ref = refs.pop(0) if has_res else None
        o_ref = refs.pop(0)
        part = lax.dot_general(a_ref[...].astype(BF16), b_ref[...].astype(BF16), dims,
                               preferred_element_type=F32)

        def finish(acc):
            if scale is not None:
                acc = acc * scale
            if has_res:
                acc = acc + r_ref[...]
            o_ref[...] = acc.astype(o_ref.dtype)

        if nk == 1:
            finish(part)
        else:
            acc_ref = refs.pop(0)
            k = pl.program_id(2)

            @pl.when(k == 0)
            def _():
                acc_ref[...] = part

            @pl.when(k > 0)
            def _():
                acc_ref[...] += part

            @pl.when(k == nk - 1)
            def _():
                finish(acc_ref[...])

    operands = [a, b]
    in_specs = [a_spec, b_spec]
    vmem = 2 * (_nbytes(a_spec.block_shape, a.dtype) + _nbytes(b_spec.block_shape, b.dtype)
                + _nbytes(out_spec.block_shape, out_shape.dtype))
    vmem += 3 * _nbytes(acc_shape, F32)
    if has_res:
        operands.append(res)
        in_specs.append(res_spec)
        vmem += 2 * _nbytes(res_spec.block_shape, res.dtype)
    scratch = [pltpu.VMEM(acc_shape, F32)] if nk > 1 else []
    return _call(body, name, grid, in_specs, out_spec, out_shape, scratch=scratch, vmem_bytes=vmem)(*operands)


def _mm_colblock(name, h, w_g, layer):
    t, k = h.shape
    nb = w_g.shape[3]
    tm = _tile(t, 2048)
    return _mm(
        name, NN, h, w_g, grid=(t // tm, N_DEV, 1),
        a_spec=pl.BlockSpec((tm, k), lambda i, j, kk: (i, 0)),
        b_spec=pl.BlockSpec((None, None, k, nb), lambda i, j, kk: (layer, j, 0, 0)),
        out_shape=jax.ShapeDtypeStruct((t, N_DEV * nb), BF16),
        out_spec=pl.BlockSpec((tm, nb), lambda i, j, kk: (i, j)), acc_shape=(tm, nb))


def _mm_natural(name, a, w, layer, *, res=None, out_dtype=F32, scale=None):
    t, k = a.shape
    n = w.shape[2]
    tm = _tile(t, 1024)
    tn = _tile(n, 1024 if k <= 1024 else 512)
    res_spec = None if res is None else pl.BlockSpec((tm, tn), lambda i, j, kk: (i, j))
    return _mm(
        name, NN, a, w, grid=(t // tm, n // tn, 1),
        a_spec=pl.BlockSpec((tm, k), lambda i, j, kk: (i, 0)),
        b_spec=pl.BlockSpec((None, k, tn), lambda i, j, kk: (layer, 0, j)),
        out_shape=jax.ShapeDtypeStruct((t, n), out_dtype),
        out_spec=pl.BlockSpec((tm, tn), lambda i, j, kk: (i, j)),
        acc_shape=(tm, tn), res=res, res_spec=res_spec, scale=scale)


def _mm_down(name, act, w4, layer, res):
    nblk, t, kb = act.shape
    n = w4.shape[3]
    tm = _tile(t, 1024)

    def body(a_ref, b_ref, r_ref, o_ref):
        acc = r_ref[...]
        for u in range(nblk):
            acc = acc + jnp.dot(a_ref[u], b_ref[u], preferred_element_type=F32)
        o_ref[...] = acc

    row = pl.BlockSpec((tm, n), lambda i: (i, 0))
    return _call(
        body, name, (t // tm,),
        [pl.BlockSpec((nblk, tm, kb), lambda i: (0, i, 0)),
         pl.BlockSpec((None, nblk, kb, n), lambda i: (layer, 0, 0, 0)),
         row],
        row,
        jax.ShapeDtypeStruct((t, n), F32),
        vmem_bytes=2 * (_nbytes((nblk, tm, kb), BF16) + _nbytes((nblk, kb, n), BF16)) + 6 * _nbytes((tm, n), F32),
    )(act, w4, res)


def _mm_t_colblock_norm_bwd(name, dz, w_g, layer, x, g, dx_up, blocked_in=False):
    k = w_g.shape[2]
    nb = w_g.shape[3]
    t = x.shape[0]
    tm = _tile(t, 1024)
    per_step = 4 if nb * k <= 512 * 1024 else 2
    n_steps = N_DEV // per_step
    if blocked_in:
        a_spec = pl.BlockSpec((per_step, tm, nb), lambda i, kk: (kk, i, 0))
    else:
        a_spec = pl.BlockSpec((tm, per_step * nb), lambda i, kk: (i, kk))

    def body(a_ref, b_ref, x_ref, g_ref, up_ref, dx_ref, dg_ref, acc_ref):
        i = pl.program_id(0)
        kk = pl.program_id(1)
        part = None
        for u in range(per_step):
            a = a_ref[u] if blocked_in else a_ref[:, u * nb:(u + 1) * nb]
            term = lax.dot_general(a.astype(BF16), b_ref[u].astype(BF16), NT, preferred_element_type=F32)
            part = term if part is None else part + term

        @pl.when(kk == 0)
        def _():
            acc_ref[...] = part

        @pl.when(kk > 0)
        def _():
            acc_ref[...] += part

        @pl.when((i == 0) & (kk == 0))
        def _():
            dg_ref[...] = jnp.zeros_like(dg_ref)

        @pl.when(kk == n_steps - 1)
        def _():
            dy = acc_ref[...]
            xf = x_ref[...]
            r = lax.rsqrt(jnp.mean(xf * xf, axis=-1, keepdims=True) + EPS)
            xhat = xf * r
            dxhat = dy * g_ref[...]
            dg_ref[...] += jnp.sum(dy * xhat, axis=0, keepdims=True)
            dx_ref[...] = up_ref[...] + r * (dxhat - xhat * jnp.mean(dxhat * xhat, axis=-1, keepdims=True))

    row = pl.BlockSpec((tm, k), lambda i, kk: (i, 0))
    vec = pl.BlockSpec((1, k), lambda i, kk: (0, 0))
    dx, dg = _call(
        body, name, (t // tm, n_steps),
        [a_spec, pl.BlockSpec((None, per_step, k, nb), lambda i, kk: (layer, kk, 0, 0)), row, vec, row],
        [row, vec],
        [jax.ShapeDtypeStruct((t, k), F32), jax.ShapeDtypeStruct((1, k), F32)],
        scratch=[pltpu.VMEM((tm, k), F32)],
        vmem_bytes=2 * per_step * (_nbytes((tm, nb), BF16) + _nbytes((k, nb), BF16)) + 10 * _nbytes((tm, k), F32),
    )(dz, w_g, x, g.reshape(1, k), dx_up)
    return dx, dg.reshape(k)


def _ffn_gate_up(name, h, w_g, layer):
    t, k = h.shape
    nb = w_g.shape[3]
    half = N_DEV // 2
    tm = _tile(t, 1024)

    def body(h_ref, wg_ref, wu_ref, dact_ref, act_ref):
        hb = h_ref[...]
        gate = jnp.dot(hb, wg_ref[...], preferred_element_type=F32)
        up = jnp.dot(hb, wu_ref[...], preferred_element_type=F32)
        sig = _sigmoid(gate)
        silu = gate * sig
        dact_ref[0] = (up * (sig * (1.0 + gate * (1.0 - sig)))).astype(BF16)
        dact_ref[1] = silu.astype(BF16)
        act_ref[...] = (silu * up).astype(BF16)

    return _call(
        body, name, (t // tm, half),
        [pl.BlockSpec((tm, k), lambda i, j: (i, 0)),
         pl.BlockSpec((None, None, k, nb), lambda i, j: (layer, j, 0, 0)),
         pl.BlockSpec((None, None, k, nb), lambda i, j: (layer, half + j, 0, 0))],
        [pl.BlockSpec((2, None, tm, nb), lambda i, j: (0, j, i, 0)),
         pl.BlockSpec((None, tm, nb), lambda i, j: (j, i, 0))],
        [jax.ShapeDtypeStruct((2, half, t, nb), BF16), jax.ShapeDtypeStruct((half, t, nb), BF16)],
        vmem_bytes=2 * (_nbytes((tm, k), BF16) + 2 * _nbytes((k, nb), BF16) + 3 * _nbytes((tm, nb), BF16))
        + 8 * _nbytes((tm, nb), F32),
    )(h, w_g, w_g)


def _ffn_down_dx(name, dy, w4, layer, dact):
    t, n = dy.shape
    nblk, kb = w4.shape[1], w4.shape[2]
    tm = _tile(t, 1024)

    def body(dy_ref, w_ref, dact_ref, dgu_ref):
        da = lax.dot_general(dy_ref[...].astype(BF16), w_ref[...], NT, preferred_element_type=F32)
        dgu_ref[0] = (da * dact_ref[0].astype(F32)).astype(BF16)
        dgu_ref[1] = (da * dact_ref[1].astype(F32)).astype(BF16)

    blk = pl.BlockSpec((2, None, tm, kb), lambda i, j: (0, j, i, 0))
    return _call(
        body, name, (t // tm, nblk),
        [pl.BlockSpec((tm, n), lambda i, j: (i, 0)),
         pl.BlockSpec((None, None, kb, n), lambda i, j: (layer, j, 0, 0)),
         blk],
        blk,
        jax.ShapeDtypeStruct((2, nblk, t, kb), BF16),
        vmem_bytes=2 * (_nbytes((tm, n), F32) + _nbytes((kb, n), BF16) + 4 * _nbytes((tm, kb), BF16))
        + 8 * _nbytes((tm, kb), F32),
    )(dy, w4, dact)


def _mm_t_natural(name, dy, w, layer):
    t, n = dy.shape
    k = w.shape[1]
    tm = _tile(t, 1024)
    tk = _tile(k, 1024)
    return _mm(
        name, NT, dy, w, grid=(t // tm, k // tk, 1),
        a_spec=pl.BlockSpec((tm, n), lambda i, j, kk: (i, 0)),
        b_spec=pl.BlockSpec((None, tk, n), lambda i, j, kk: (layer, j, 0)),
        out_shape=jax.ShapeDtypeStruct((t, k), BF16),
        out_spec=pl.BlockSpec((tm, tk), lambda i, j, kk: (i, j)),
        acc_shape=(tm, tk))


def _mm_t_natural_norm_bwd(name, dy, w, layer, x, g, dx_up):
    t, n = dy.shape
    k = w.shape[1]
    tm = _tile(t, 1024)

    def body(a_ref, b_ref, x_ref, g_ref, up_ref, dx_ref, dg_ref):
        @pl.when(pl.program_id(0) == 0)
        def _():
            dg_ref[...] = jnp.zeros_like(dg_ref)

        dh = lax.dot_general(a_ref[...].astype(BF16), b_ref[...], NT, preferred_element_type=F32)
        xf = x_ref[...]
        r = lax.rsqrt(jnp.mean(xf * xf, axis=-1, keepdims=True) + EPS)
        xhat = xf * r
        dxhat = dh * g_ref[...]
        dg_ref[...] += jnp.sum(dh * xhat, axis=0, keepdims=True)
        dx_ref[...] = up_ref[...] + r * (dxhat - xhat * jnp.mean(dxhat * xhat, axis=-1, keepdims=True))

    row = pl.BlockSpec((tm, k), lambda i: (i, 0))
    vec = pl.BlockSpec((1, k), lambda i: (0, 0))
    dx, dg = _call(
        body, name, (t // tm,),
        [pl.BlockSpec((tm, n), lambda i: (i, 0)), pl.BlockSpec((None, k, n), lambda i: (layer, 0, 0)), row, vec, row],
        [row, vec],
        [jax.ShapeDtypeStruct((t, k), F32), jax.ShapeDtypeStruct((1, k), F32)],
        vmem_bytes=2 * (_nbytes((tm, n), dy.dtype) + _nbytes((k, n), BF16)) + 10 * _nbytes((tm, k), F32),
    )(dy, w, x, g.reshape(1, k), dx_up)
    return dx, dg.reshape(k)


def _mm_dw_colblock(name, h, dz, blocked_in=False, transposed=False):
    t, k = h.shape
    nb = dz.shape[2] if blocked_in else dz.shape[1] // N_DEV
    tk = _tile(t, 2048)
    h_spec = pl.BlockSpec((tk, k), lambda i, j, kk: (kk, 0))
    if blocked_in:
        dz_spec = pl.BlockSpec((None, tk, nb), lambda i, j, kk: (j, kk, 0))
    else:
        dz_spec = pl.BlockSpec((tk, nb), lambda i, j, kk: (kk, j))
    rows, cols = (nb, k) if transposed else (k, nb)
    return _mm(
        name, TN, *((dz, h) if transposed else (h, dz)), grid=(1, N_DEV, t // tk),
        a_spec=dz_spec if transposed else h_spec,
        b_spec=h_spec if transposed else dz_spec,
        out_shape=jax.ShapeDtypeStruct((N_DEV, rows, cols), BF16),
        out_spec=pl.BlockSpec((None, rows, cols), lambda i, j, kk: (j, 0, 0)),
        acc_shape=(rows, cols))


def _mm_dw_natural(name, a, dy):
    t, k = a.shape
    n = dy.shape[1]
    tko = _tile(k, 1024)
    tt = _tile(t, 2048)
    out = _mm(
        name, TN, a, dy, grid=(k // tko, 1, t // tt),
        a_spec=pl.BlockSpec((tt, tko), lambda i, j, kk: (kk, i)),
        b_spec=pl.BlockSpec((tt, n), lambda i, j, kk: (kk, 0)),
        out_shape=jax.ShapeDtypeStruct((k, n), BF16),
        out_spec=pl.BlockSpec((tko, n), lambda i, j, kk: (i, 0)),
        acc_shape=(tko, n))
    return out.reshape(N_DEV, k // N_DEV, n)


def _mm_dw_down(name, act, dy):
    nblk, t, kb = act.shape
    n = dy.shape[1]
    tt = _tile(t, 2048)
    out = _mm(
        name, TN, act, dy, grid=(nblk, 1, t // tt),
        a_spec=pl.BlockSpec((None, tt, kb), lambda i, j, kk: (i, kk, 0)),
        b_spec=pl.BlockSpec((tt, n), lambda i, j, kk: (kk, 0)),
        out_shape=jax.ShapeDtypeStruct((nblk, kb, n), BF16),
        out_spec=pl.BlockSpec((None, kb, n), lambda i, j, kk: (i, 0, 0)),
        acc_shape=(kb, n))
    return out.reshape(N_DEV, (nblk * kb) // N_DEV, n)


def _spatial_mask(transposed=False):
    r = lax.broadcasted_iota(jnp.int32, (A_CHUNK, A_CHUNK), 0) // CHUNK
    c = lax.broadcasted_iota(jnp.int32, (A_CHUNK, A_CHUNK), 1) // CHUNK
    return c >= r if transposed else r >= c


def _sgu_tile(t):
    return _tile(t, 2 * A_CHUNK)


def _sgu_fwd(zpre, g_sgu, w_sp, b_full, name):
    t, f2 = zpre.shape
    f = f2 // 2
    gd = f // A_GROUPS
    tm = _sgu_tile(t)

    def body(z_ref, g_ref, w_ref, b_ref, p_ref, zs_ref, dg_ref):
        mask = _spatial_mask()
        wm = [jnp.where(mask, w_ref[g], 0.0).astype(BF16) for g in range(A_GROUPS)]
        for c in range(tm // A_CHUNK):
            rows = pl.ds(c * A_CHUNK, A_CHUNK)
            z, dgelu = _gelu_and_grad(z_ref[rows, :].astype(F32))
            zs_ref[rows, :] = z.astype(BF16)
            dg_ref[rows, :] = dgelu.astype(BF16)
            u = z[:, :f]
            v0 = z[:, f:]
            r = lax.rsqrt(jnp.mean(v0 * v0, axis=-1, keepdims=True) + EPS)
            v1 = (v0 * r * g_ref[...]).astype(BF16)
            for g in range(A_GROUPS):
                cols = slice(g * gd, (g + 1) * gd)
                v2 = jnp.dot(wm[g], v1[:, cols], preferred_element_type=F32) + b_ref[:, cols]
                p_ref[rows, cols] = (u[:, cols] * v2).astype(BF16)

    return _call(
        body, name, (t // tm,),
        [pl.BlockSpec((tm, f2), lambda i: (i, 0)),
         pl.BlockSpec((1, f), lambda i: (0, 0)),
         pl.BlockSpec((A_GROUPS, A_CHUNK, A_CHUNK), lambda i: (0, 0, 0)),
         pl.BlockSpec((A_CHUNK, f), lambda i: (0, 0))],
        [pl.BlockSpec((tm, f), lambda i: (i, 0)), pl.BlockSpec((tm, f2), lambda i: (i, 0)),
         pl.BlockSpec((tm, f2), lambda i: (i, 0))],
        [jax.ShapeDtypeStruct((t, f), BF16), jax.ShapeDtypeStruct((t, f2), BF16), jax.ShapeDtypeStruct((t, f2), BF16)],
        vmem_bytes=6 * _nbytes((tm, f2), BF16) + 2 * _nbytes((tm, f), BF16) + 8 * _nbytes((A_CHUNK, f2), F32),
    )(zpre, g_sgu.reshape(1, f), w_sp, b_full)


def _sgu_bwd(zs, dgs, dp, g_sgu, w_sp, w_sp_t, b_full, name):
    t, f2 = zs.shape
    f = f2 // 2
    gd = f // A_GROUPS
    tm = _sgu_tile(t)
    n_steps = t // tm

    def body(z_ref, dgelu_ref, dp_ref, g_ref, w_ref, wt_ref, b_ref, dz_ref, dw_ref, db_ref, dg_ref, dv1_ref, dbf_ref):
        step = pl.program_id(0)

        @pl.when(step == 0)
        def _():
            dw_ref[...] = jnp.zeros_like(dw_ref)
            dg_ref[...] = jnp.zeros_like(dg_ref)
            dbf_ref[...] = jnp.zeros_like(dbf_ref)

        mask = _spatial_mask()
        mask_t = _spatial_mask(transposed=True)
        wm = [jnp.where(mask, w_ref[g], 0.0).astype(BF16) for g in range(A_GROUPS)]
        wmt = [jnp.where(mask_t, wt_ref[g], 0.0).astype(BF16) for g in range(A_GROUPS)]
        gain = g_ref[...]
        for c in range(tm // A_CHUNK):
            rows = pl.ds(c * A_CHUNK, A_CHUNK)
            z = z_ref[rows, :].astype(F32)
            dgelu = dgelu_ref[rows, :].astype(F32)
            u = z[:, :f]
            v0 = z[:, f:]
            r = lax.rsqrt(jnp.mean(v0 * v0, axis=-1, keepdims=True) + EPS)
            xhat = v0 * r
            v1 = (xhat * gain).astype(BF16)
            dpf = dp_ref[rows, :].astype(F32)
            for g in range(A_GROUPS):
                cols = slice(g * gd, (g + 1) * gd)
                v1g = v1[:, cols]
                v2 = jnp.dot(wm[g], v1g, preferred_element_type=F32) + b_ref[:, cols]
                dpg = dpf[:, cols]
                dz_ref[rows, cols] = (dpg * v2 * dgelu[:, cols]).astype(BF16)
                dv2 = dpg * u[:, cols]
                dbf_ref[:, cols] += dv2
                dv2b = dv2.astype(BF16)
                dwg = lax.dot_general(dv2b, v1g, NT, preferred_element_type=F32)
                dw_ref[g] += jnp.where(mask, dwg, 0.0)
                dv1_ref[:, cols] = jnp.dot(wmt[g], dv2b, preferred_element_type=F32)
            dv1 = dv1_ref[...]
            dxhat = dv1 * gain
            dg_ref[...] += jnp.sum(dv1 * xhat, axis=0, keepdims=True)
            dv0 = r * (dxhat - xhat * jnp.mean(dxhat * xhat, axis=-1, keepdims=True))
            dz_ref[rows, pl.ds(f, f)] = (dv0 * dgelu[:, f:]).astype(BF16)

        @pl.when(step == n_steps - 1)
        def _():
            for g in range(A_GROUPS):
                db_ref[g] = jnp.sum(dbf_ref[:, g * gd:(g + 1) * gd], axis=1, keepdims=True)

    wspec = pl.BlockSpec((A_GROUPS, A_CHUNK, A_CHUNK), lambda i: (0, 0, 0))
    dz, dw, db, dg = _call(
        body, name, (n_steps,),
        [pl.BlockSpec((tm, f2), lambda i: (i, 0)),
         pl.BlockSpec((tm, f2), lambda i: (i, 0)),
         pl.BlockSpec((tm, f), lambda i: (i, 0)),
         pl.BlockSpec((1, f), lambda i: (0, 0)),
         wspec, wspec,
         pl.BlockSpec((A_CHUNK, f), lambda i: (0, 0))],
        [pl.BlockSpec((tm, f2), lambda i: (i, 0)),
         wspec,
         pl.BlockSpec((A_GROUPS, A_CHUNK, 1), lambda i: (0, 0, 0)),
         pl.BlockSpec((1, f), lambda i: (0, 0))],
        [jax.ShapeDtypeStruct((t, f2), BF16),
         jax.ShapeDtypeStruct((A_GROUPS, A_CHUNK, A_CHUNK), F32),
         jax.ShapeDtypeStruct((A_GROUPS, A_CHUNK, 1), F32),
         jax.ShapeDtypeStruct((1, f), F32)],
        scratch=[pltpu.VMEM((A_CHUNK, f), F32), pltpu.VMEM((A_CHUNK, f), F32)],
        vmem_bytes=6 * _nbytes((tm, f2), BF16) + 2 * _nbytes((tm, f), BF16) + 12 * _nbytes((A_CHUNK, f2), F32),
    )(zs, dgs, dp, g_sgu.reshape(1, f), w_sp, w_sp_t, b_full)
    return dz, dw, db.reshape(A_GROUPS, A_CHUNK), dg.reshape(f)


def _pair_valid(qi, col):
    qc = qi // CHUNK
    kc = col // CHUNK
    return (kc >= qc) & (kc <= qc + N_LEFT_CHUNKS)


def _diagonal_onehot():
    e = lax.broadcasted_iota(jnp.int32, (REL_PAD, DIAGONALS), 1)
    idx = jnp.clip(PAIR_BAND - 1 - e, -MAX_REL, MAX_REL) + MAX_REL
    r = lax.broadcasted_iota(jnp.int32, (REL_PAD, DIAGONALS), 0)
    return jnp.where(r == idx, 1.0, 0.0).astype(BF16)


def _bias_build(table, name):
    h = table.shape[0]
    tab = jnp.pad(table, ((0, 0), (0, REL_PAD - N_REL)))

    def body(t_ref, o_ref):
        oh = _diagonal_onehot()
        diag = jnp.zeros((h, DIAGONALS), F32)
        for piece in _split3(t_ref[...]):
            diag += jnp.dot(piece, oh, preferred_element_type=F32)
        col = lax.broadcasted_iota(jnp.int32, (h, PAIR_BAND), 1)
        for qi in range(PAIR_ROWS):
            row = pltpu.roll(diag, (qi - (PAIR_ROWS - 1)) % DIAGONALS, 1)[:, :PAIR_BAND]
            o_ref[qi] = jnp.where(_pair_valid(qi, col), row, NEG_INF)

    out = _call(
        body, name, (1,),
        [pl.BlockSpec((h, REL_PAD), lambda i: (0, 0))],
        pl.BlockSpec((PAIR_ROWS, h, PAIR_BAND), lambda i: (0, 0, 0)),
        jax.ShapeDtypeStruct((PAIR_ROWS, h, PAIR_BAND), F32),
        vmem_bytes=4 * _nbytes((PAIR_ROWS, h, PAIR_BAND), F32),
    )(tab)
    return jnp.transpose(out, (1, 0, 2))


def _bias_block(pair_bias):
    rest = K_BLOCK - PAIR_BAND
    return jnp.concatenate(
        [jnp.pad(pair_bias, ((0, 0), (0, 0), (p * PAIR_ROWS, rest - p * PAIR_ROWS)), constant_values=NEG_INF)
         for p in range(PAIRS_PER_BLOCK)], axis=1)


def _bias_grad(dbias, name):
    h = dbias.shape[0]
    db_t = jnp.transpose(dbias, (1, 0, 2))

    def body(d_ref, o_ref):
        diag = jnp.zeros((h, DIAGONALS), F32)
        for qi in range(PAIR_ROWS):
            diag += pltpu.roll(d_ref[qi], PAIR_ROWS - 1 - qi, 1)
        oh = _diagonal_onehot()
        acc = jnp.zeros((h, REL_PAD), F32)
        for piece in _split3(diag):
            acc += lax.dot_general(piece, oh, NT, preferred_element_type=F32)
        o_ref[...] = acc

    out = _call(
        body, name, (1,),
        [pl.BlockSpec((PAIR_ROWS, h, DIAGONALS), lambda i: (0, 0, 0))],
        pl.BlockSpec((h, REL_PAD), lambda i: (0, 0)),
        jax.ShapeDtypeStruct((h, REL_PAD), F32),
        vmem_bytes=4 * _nbytes((PAIR_ROWS, h, DIAGONALS), F32),
    )(db_t)
    return out[:, :N_REL]


def _head_masks():
    lane = lax.broadcasted_iota(jnp.int32, (Q_BLOCK, HEAD_PAIR), 1)
    return lane < HEAD_DIM, lane >= HEAD_DIM


def _block_scores(qm, kb, bias, valid):
    s = lax.dot_general(qm, kb, NT, preferred_element_type=F32) + bias
    return s if valid is None else jnp.where(valid, s, NEG_INF)


def _softmax_rows(s):
    e = jnp.exp(s - jnp.max(s, axis=-1, keepdims=True))
    return e * (1.0 / jnp.sum(e, axis=-1, keepdims=True))


def _padded_then_plain(step, n_blocks):
    n_padded = min(LEFT // Q_BLOCK, n_blocks)
    lax.fori_loop(0, n_padded, lambda j, c: step(j, c, True), 0)
    lax.fori_loop(n_padded, n_blocks, lambda j, c: step(j, c, False), 0, unroll=ATTN_UNROLL)


def _attn_fwd(q, kvpad, bias, name):
    t, d = q.shape
    n_pairs = d // HEAD_PAIR
    n_blocks = t // Q_BLOCK

    def body(q_ref, k_ref, v_ref, b_ref, o_ref):
        masks = _head_masks()
        key = lax.broadcasted_iota(jnp.int32, (Q_BLOCK, K_BLOCK), 1)

        def step(j, carry, padded):
            r0 = pl.multiple_of(j * Q_BLOCK, Q_BLOCK)
            q2 = q_ref[pl.ds(r0, Q_BLOCK), :].astype(F32)
            kb = k_ref[pl.ds(r0, K_BLOCK), :]
            vb = v_ref[pl.ds(r0, K_BLOCK), :]
            valid = key >= LEFT - j * Q_BLOCK if padded else None
            scores = [_block_scores(jnp.where(masks[a], q2, 0.0).astype(BF16), kb, b_ref[a], valid) for a in range(2)]
            probs = [_softmax_rows(s).astype(BF16) for s in scores]
            outs = [jnp.dot(p, vb, preferred_element_type=F32) for p in probs]
            o_ref[pl.ds(r0, Q_BLOCK), :] = jnp.where(masks[0], outs[0], outs[1]).astype(BF16)
            return carry

        _padded_then_plain(step, n_blocks)

    return _call(
        body, name, (n_pairs,),
        [pl.BlockSpec((t, HEAD_PAIR), lambda p: (0, p)),
         pl.BlockSpec((LEFT + t, HEAD_PAIR), lambda p: (0, p)),
         pl.BlockSpec((LEFT + t, HEAD_PAIR), lambda p: (0, n_pairs + p)),
         pl.BlockSpec((2, Q_BLOCK, K_BLOCK), lambda p: (p, 0, 0))],
        pl.BlockSpec((t, HEAD_PAIR), lambda p: (0, p)),
        jax.ShapeDtypeStruct((t, d), BF16),
        vmem_bytes=8 * _nbytes((LEFT + t, HEAD_PAIR), BF16) + 12 * _nbytes((2, Q_BLOCK, K_BLOCK), F32),
    )(q, kvpad, kvpad, bias)


def _attn_bwd(q, kvpad, bias, do, dk_in, dv_in, name):
    t, d = q.shape
    n_pairs = d // HEAD_PAIR
    n_blocks = t // Q_BLOCK
    has_in = dk_in is not None

    def body(*refs):
        refs = list(refs)
        q_ref, k_ref, v_ref, b_ref, do_ref = refs[:5]
        refs = refs[5:]
        if has_in:
            dki_ref, dvi_ref = refs[:2]
            refs = refs[2:]
        dq_ref, dk_ref, dv_ref, db_ref = refs
        masks = _head_masks()
        key = lax.broadcasted_iota(jnp.int32, (Q_BLOCK, K_BLOCK), 1)
        if has_in:
            dk_ref[...] = dki_ref[...]
            dv_ref[...] = dvi_ref[...]
        else:
            dk_ref[...] = jnp.zeros_like(dk_ref)
            dv_ref[...] = jnp.zeros_like(dv_ref)
        db_ref[...] = jnp.zeros_like(db_ref)

        def step(j, carry, padded):
            r0 = pl.multiple_of(j * Q_BLOCK, Q_BLOCK)
            q2 = q_ref[pl.ds(r0, Q_BLOCK), :].astype(F32)
            do2 = do_ref[pl.ds(r0, Q_BLOCK), :].astype(F32)
            kb = k_ref[pl.ds(r0, K_BLOCK), :]
            vb = v_ref[pl.ds(r0, K_BLOCK), :]
            valid = key >= LEFT - j * Q_BLOCK if padded else None
            heads = range(2)
            qms = [jnp.where(masks[a], q2, 0.0).astype(BF16) for a in heads]
            doms = [jnp.where(masks[a], do2, 0.0).astype(BF16) for a in heads]
            scores = [_block_scores(qms[a], kb, b_ref[a], valid) for a in heads]
            dps = [lax.dot_general(doms[a], vb, NT, preferred_element_type=F32) for a in heads]
            ps = [_softmax_rows(s) for s in scores]
            dss = [ps[a] * (dps[a] - jnp.sum(dps[a] * ps[a], axis=-1, keepdims=True)) for a in heads]
            for a in heads:
                for pair in range(PAIRS_PER_BLOCK):
                    lo = pair * PAIR_ROWS
                    db_ref[a, :, pl.ds(0, PAIR_BAND)] += dss[a][lo:lo + PAIR_ROWS, lo:lo + PAIR_BAND]
            dsbs = [ds.astype(BF16) for ds in dss]
            pbs = [p.astype(BF16) for p in ps]
            dqs = [jnp.dot(dsbs[a], kb, preferred_element_type=F32) for a in heads]
            dk_acc = sum(lax.dot_general(dsbs[a], qms[a], TN, preferred_element_type=F32) for a in heads)
            dv_acc = sum(lax.dot_general(pbs[a], doms[a], TN, preferred_element_type=F32) for a in heads)
            dq = jnp.where(masks[0], dqs[0], dqs[1]) * ATTN_SCALE
            dq_ref[pl.ds(r0, Q_BLOCK), :] = dq.astype(BF16)
            dk_ref[pl.ds(r0, K_BLOCK), :] += dk_acc
            dv_ref[pl.ds(r0, K_BLOCK), :] += dv_acc
            return carry

        _padded_then_plain(step, n_blocks)

    q_spec = pl.BlockSpec((t, HEAD_PAIR), lambda p: (0, p))
    kv_spec = pl.BlockSpec((LEFT + t, HEAD_PAIR), lambda p: (0, p))
    operands = [q, kvpad, kvpad, bias, do]
    in_specs = [q_spec, kv_spec, pl.BlockSpec((LEFT + t, HEAD_PAIR), lambda p: (0, n_pairs + p)),
                pl.BlockSpec((2, Q_BLOCK, K_BLOCK), lambda p: (p, 0, 0)), q_spec]
    aliases = None
    if has_in:
        operands += [dk_in, dv_in]
        in_specs += [kv_spec, kv_spec]
        aliases = {5: 1, 6: 2}
    return _call(
        body, name, (n_pairs,),
        in_specs,
        [q_spec, kv_spec, kv_spec, pl.BlockSpec((2, PAIR_ROWS, DIAGONALS), lambda p: (p, 0, 0))],
        [jax.ShapeDtypeStruct((t, d), BF16),
         jax.ShapeDtypeStruct((LEFT + t, d), F32),
         jax.ShapeDtypeStruct((LEFT + t, d), F32),
         jax.ShapeDtypeStruct((d // HEAD_DIM, PAIR_ROWS, DIAGONALS), F32)],
        vmem_bytes=10 * _nbytes((LEFT + t, HEAD_PAIR), BF16) + 8 * _nbytes((LEFT + t, HEAD_PAIR), F32)
        + 16 * _nbytes((2, Q_BLOCK, K_BLOCK), F32),
        aliases=aliases,
    )(*operands)


def _loss_head(x, g, target, name):
    t, d = x.shape
    tm = _tile(t, 512)

    def body(x_ref, g_ref, t_ref, dx_ref, loss_ref, dg_ref):
        @pl.when(pl.program_id(0) == 0)
        def _():
            loss_ref[...] = jnp.zeros_like(loss_ref)
            dg_ref[...] = jnp.zeros_like(dg_ref)

        xf = x_ref[...]
        r = lax.rsqrt(jnp.mean(xf * xf, axis=-1, keepdims=True) + EPS)
        xhat = xf * r
        diff = xhat * g_ref[...] - t_ref[...]
        row_loss = jnp.mean(diff * diff, axis=-1, keepdims=True)
        loss_ref[...] += 0.5 * jnp.sum(row_loss, axis=0, keepdims=True)
        dy = diff * (1.0 / d)
        dg_ref[...] += jnp.sum(dy * xhat, axis=0, keepdims=True)
        dxhat = dy * g_ref[...]
        dx_ref[...] = r * (dxhat - xhat * jnp.mean(dxhat * xhat, axis=-1, keepdims=True))

    row = pl.BlockSpec((tm, d), lambda i: (i, 0))
    vec = pl.BlockSpec((1, d), lambda i: (0, 0))
    dx, loss, dg = _call(
        body, name, (t // tm,),
        [row, vec, row],
        [row, pl.BlockSpec((1, 1), lambda i: (0, 0)), vec],
        [jax.ShapeDtypeStruct((t, d), F32), jax.ShapeDtypeStruct((1, 1), F32), jax.ShapeDtypeStruct((1, d), F32)],
        vmem_bytes=10 * _nbytes((tm, d), F32),
    )(x, g.reshape(1, d), target)
    return dx, loss[0, 0], dg.reshape(d)


def _adamw_store(g, w_ref, m_ref, v_ref, g_ref, d_ref, nm_ref, nv_ref):
    c1 = 1.0 / (1.0 - ADAM_B1 ** ADAM_STEP)
    c2 = 1.0 / (1.0 - ADAM_B2 ** ADAM_STEP)
    nm = ADAM_B1 * m_ref[...] + (1.0 - ADAM_B1) * g
    nv = ADAM_B2 * v_ref[...] + (1.0 - ADAM_B2) * (g * g)
    g_ref[...] = g
    nm_ref[...] = nm
    nv_ref[...] = nv
    d_ref[...] = -ADAM_LR * ((nm * c1) / (jnp.sqrt(nv * c2) + ADAM_EPS) + ADAM_WD * w_ref[...])


def _adamw_layer(recv, own, w, m, v, layer, prev, me, name):
    n_src, r, c = recv.shape
    tr = _row_tile(r, max(BF16_SUBLANES, ADAMW_BLOCK_ELEMS // c), BF16_SUBLANES)
    first = prev is None

    def body(me_ref, recv_ref, own_ref, w_ref, m_ref, v_ref, *rest):
        mine = me_ref[0]
        own_part = own_ref[...].astype(F32)
        g = None
        for s in range(n_src):
            part = jnp.where(mine == s, own_part, recv_ref[s].astype(F32))
            g = part if g is None else g + part
        _adamw_store(g, w_ref, m_ref, v_ref, *rest[-4:])

    blk = pl.BlockSpec((None, tr, c), lambda i, me_ref: (layer, i, 0))
    any_spec = pl.BlockSpec(memory_space=pl.ANY)
    out = jax.ShapeDtypeStruct(w.shape, F32)
    operands = [me, recv, own, w, m, v] + ([] if first else list(prev))
    vmem = 2 * _nbytes((n_src + 1, tr, c), BF16) + 18 * _nbytes((tr, c), F32)
    return pl.pallas_call(
        body,
        name=name,
        grid_spec=pltpu.PrefetchScalarGridSpec(
            num_scalar_prefetch=1,
            grid=(r // tr,),
            in_specs=[pl.BlockSpec((n_src, tr, c), lambda i, me_ref: (0, i, 0)),
                      pl.BlockSpec((None, tr, c), lambda i, me_ref: (me_ref[0], i, 0)),
                      blk, blk, blk] + ([] if first else [any_spec] * 4),
            out_specs=[blk, blk, blk, blk],
        ),
        out_shape=[out, out, out, out],
        input_output_aliases={} if first else {6 + j: j for j in range(4)},
        compiler_params=pltpu.CompilerParams(
            dimension_semantics=("arbitrary",),
            vmem_limit_bytes=int(min(max(VMEM_FLOOR_BYTES, vmem * 5 // 4), VMEM_CEIL_BYTES))),
    )(*operands)


def _adamw(parts, w, m, v, name):
    n_layers, n_src, r, c = parts.shape
    mult = BF16_SUBLANES if parts.dtype == BF16 else F32_SUBLANES
    tr = _row_tile(r, max(mult, ADAMW_BLOCK_ELEMS // c), mult)

    def body(p_ref, w_ref, m_ref, v_ref, g_ref, d_ref, nm_ref, nv_ref):
        g = p_ref[0].astype(F32)
        for s in range(1, n_src):
            g = g + p_ref[s].astype(F32)
        _adamw_store(g, w_ref, m_ref, v_ref, g_ref, d_ref, nm_ref, nv_ref)

    blk = pl.BlockSpec((None, tr, c), lambda l, i: (l, i, 0))
    out = jax.ShapeDtypeStruct((n_layers, r, c), F32)
    return _call(
        body, name, (n_layers, r // tr),
        [pl.BlockSpec((None, n_src, tr, c), lambda l, i: (l, 0, i, 0)), blk, blk, blk],
        [blk, blk, blk, blk],
        [out, out, out, out],
        vmem_bytes=2 * _nbytes((n_src, tr, c), parts.dtype) + 18 * _nbytes((tr, c), F32),
    )(parts, w, m, v)


def _ordered_sum(parts, name):
    n_src, r, c = parts.shape

    def body(p_ref, o_ref):
        acc = p_ref[0]
        for s in range(1, n_src):
            acc = acc + p_ref[s]
        o_ref[...] = acc

    return _call(
        body, name, (1,),
        [pl.BlockSpec((n_src, r, c), lambda i: (0, 0, 0))],
        pl.BlockSpec((r, c), lambda i: (0, 0)),
        jax.ShapeDtypeStruct((r, c), F32),
        vmem_bytes=4 * _nbytes((n_src, r, c), F32),
    )(parts)


def _position():
    return lax.axis_index("x"), lax.axis_index("y"), lax.axis_index("c")


def _linear(p):
    return 4 * p[0] + 2 * p[1] + p[2]


def _all_gather(shards, name):
    n = len(shards)

    def body(*refs):
        ins, outs = refs[:n], refs[n:2 * n]
        send_sems, recv_sems, local_sems = refs[2 * n:]
        x, y, c = _position()
        me, sibling = (x, y, c), (x, y, 1 - c)
        chips = [(1 - x, y), (x, 1 - y), (1 - x, 1 - y)]

        def slab(t, p):
            return outs[t].at[:, _linear(p)]

        def copy(t, k, block, to, src=None):
            return pltpu.make_async_remote_copy(
                src_ref=slab(t, block) if src is None else src,
                dst_ref=slab(t, block),
                send_sem=send_sems.at[t, k],
                recv_sem=recv_sems.at[t, k],
                device_id=to,
                device_id_type=MESH,
            )

        started = []
        for t in range(n):
            mine = pltpu.make_async_copy(ins[t], slab(t, me), local_sems.at[t])
            mine.start()
            started.append(mine)
        sends = []
        for t in range(n):
            first = [copy(t, 0, me, sibling, src=ins[t])]
            first += [copy(t, 1 + j, me, (*chip, c), src=ins[t]) for j, chip in enumerate(chips)]
            for cp in first:
                cp.start()
            sends += first
        for t in range(n):
            for j, chip in enumerate(chips):
                copy(t, 1 + j, (*chip, c), me).wait_recv()
                passed = copy(t, 4 + j, (*chip, c), sibling)
                passed.start()
                sends.append(passed)
        for t in range(n):
            copy(t, 0, sibling, me).wait_recv()
            for j, chip in enumerate(chips):
                copy(t, 4 + j, (*chip, 1 - c), me).wait_recv()
        for cp in sends:
            cp.wait_send()
        for mine in started:
            mine.wait()

    out_shape = [jax.ShapeDtypeStruct((s.shape[0], N_DEV) + s.shape[1:], s.dtype) for s in shards]
    return pl.pallas_call(
        body,
        name=name,
        in_specs=[HBM_SPEC] * n,
        out_specs=[HBM_SPEC] * n,
        out_shape=out_shape,
        scratch_shapes=[
            pltpu.SemaphoreType.DMA((n, N_DEV - 1)),
            pltpu.SemaphoreType.DMA((n, N_DEV - 1)),
            pltpu.SemaphoreType.DMA((n,)),
        ],
    )(*shards)


def _exchange(blocks, name):
    n = len(blocks)

    def body(*refs):
        ins, outs = refs[:n], refs[n:2 * n]
        send_sems, recv_sems, local_sems = refs[2 * n:]
        x, y, c = _position()
        me = _linear((x, y, c))
        flips = [(fx, fy, fc) for fx in (0, 1) for fy in (0, 1) for fc in (0, 1)][1:]

        def peer_of(flip):
            fx, fy, fc = flip
            return (1 - x if fx else x, 1 - y if fy else y, 1 - c if fc else c)

        def copy(t, k, peer):
            return pltpu.make_async_remote_copy(
                src_ref=ins[t].at[:, _linear(peer)],
                dst_ref=outs[t].at[:, me],
                send_sem=send_sems.at[t, k],
                recv_sem=recv_sems.at[t, k],
                device_id=peer,
                device_id_type=MESH,
            )

        def arrival(t, k, peer):
            return pltpu.make_async_remote_copy(
                src_ref=ins[t].at[:, _linear(peer)],
                dst_ref=outs[t].at[:, _linear(peer)],
                send_sem=send_sems.at[t, k],
                recv_sem=recv_sems.at[t, k],
                device_id=peer,
                device_id_type=MESH,
            )

        own = []
        for t in range(n):
            cp = pltpu.make_async_copy(ins[t].at[:, me], outs[t].at[:, me], local_sems.at[t])
            cp.start()
            own.append(cp)
        sends = []
        for t in range(n):
            for k, flip in enumerate(flips):
                cp = copy(t, k, peer_of(flip))
                cp.start()
                sends.append(cp)
        for t in range(n):
            for k, flip in enumerate(flips):
                arrival(t, k, peer_of(flip)).wait_recv()
        for cp in sends:
            cp.wait_send()
        for cp in own:
            cp.wait()

    out_shape = [jax.ShapeDtypeStruct(b.shape, b.dtype) for b in blocks]
    return pl.pallas_call(
        body,
        name=name,
        in_specs=[HBM_SPEC] * n,
        out_specs=[HBM_SPEC] * n,
        out_shape=out_shape,
        scratch_shapes=[
            pltpu.SemaphoreType.DMA((n, N_DEV - 1)),
            pltpu.SemaphoreType.DMA((n, N_DEV - 1)),
            pltpu.SemaphoreType.DMA((n,)),
        ],
    )(*blocks)


def _peers():
    x, y, c = _position()
    flips = [(fx, fy, fc) for fx in (0, 1) for fy in (0, 1) for fc in (0, 1)][1:]
    return [(1 - x if fx else x, 1 - y if fy else y, 1 - c if fc else c) for fx, fy, fc in flips]


SIBLING, OTHER_CHIPS = (0,), (1, 3, 5)
COPY_PEERS = {"gather": tuple(range(N_DEV - 1)), "exchange": tuple(range(N_DEV - 1)),
              "chips": SIBLING + OTHER_CHIPS, "forward": OTHER_CHIPS}


def _split_copy(kind, src_ref, land_ref, k, send_sem, recv_sem, starting):
    peers = _peers()
    peer = peers[SIBLING[0]] if kind == "forward" else peers[k]
    me = _linear(_position())
    if kind == "forward":
        slab = _linear(peers[k]) if starting else 0
        src, dst = land_ref.at[slab], land_ref.at[slab]
    elif kind == "exchange":
        src, dst = src_ref.at[_linear(peer) if starting else 0], land_ref.at[me if starting else 0]
    else:
        src, dst = src_ref, land_ref.at[me if starting else 0]
    return pltpu.make_async_remote_copy(src_ref=src, dst_ref=dst, send_sem=send_sem, recv_sem=recv_sem,
                                        device_id=peer, device_id_type=MESH)


def _split_start(groups, carry, name):
    arrays = [a for _, srcs, lands in groups for a in list(srcs) + list(lands)] + [carry]

    def body(*refs):
        ins, sems = refs[:len(arrays)], refs[len(arrays):len(arrays) + 2 * len(groups)]
        at = 0
        for g, (kind, srcs, lands) in enumerate(groups):
            src_refs, land_refs = ins[at:at + len(srcs)], ins[at + len(srcs):at + len(srcs) + len(lands)]
            at += len(srcs) + len(lands)
            peers = COPY_PEERS[kind]
            for t in range(len(lands)):
                for slot, k in enumerate(peers):
                    sem = t * len(peers) + slot
                    _split_copy(kind, src_refs[t] if srcs else None, land_refs[t], k,
                                sems[2 * g].at[sem], sems[2 * g + 1].at[sem], True).start()

    sem_shapes = [pltpu.SemaphoreType.DMA((len(lands) * len(COPY_PEERS[kind]),))
                  for kind, _, lands in groups for _ in range(2)]
    out = pl.pallas_call(
        body,
        name=name,
        in_specs=[HBM_SPEC] * len(arrays),
        out_specs=[SEM_SPEC] * len(sem_shapes) + [HBM_SPEC] * len(arrays),
        out_shape=sem_shapes + [pltpu.HBM(a.shape, a.dtype) for a in arrays],
        input_output_aliases={i: len(sem_shapes) + i for i in range(len(arrays))},
        compiler_params=pltpu.CompilerParams(has_side_effects=pltpu.SideEffectType.DATAFLOW_SIDE_EFFECTING),
    )(*[pltpu.with_memory_space_constraint(a, pltpu.HBM) for a in arrays])
    sems, thru = out[:len(sem_shapes)], out[len(sem_shapes):]
    started, at = [], 0
    for g, (kind, srcs, lands) in enumerate(groups):
        n_s, n_l = len(srcs), len(lands)
        started.append((kind, sems[2 * g], sems[2 * g + 1], thru[at:at + n_s], thru[at + n_s:at + n_s + n_l]))
        at += n_s + n_l
    return started, thru[-1]


def _split_wait(started, after, name):
    kind, send_sems, recv_sems, srcs, lands = started
    n_s, n_l = len(srcs), len(lands)
    peers = COPY_PEERS[kind]

    def body(*refs):
        src_refs, land_refs = refs[:n_s], refs[n_s:n_s + n_l]
        send_ref, recv_ref = refs[n_s + n_l], refs[n_s + n_l + 1]
        for t in range(n_l):
            for slot, k in enumerate(peers):
                sem = t * len(peers) + slot
                copy = _split_copy(kind, src_refs[t] if n_s else None, land_refs[t], k,
                                   send_ref.at[sem], recv_ref.at[sem], False)
                copy.wait_send()
                copy.wait_recv()

    arrays = list(srcs) + list(lands)
    out = pl.pallas_call(
        body,
        name=name,
        in_specs=[HBM_SPEC] * len(arrays) + [SEM_SPEC, SEM_SPEC, pl.BlockSpec(memory_space=pl.ANY)],
        out_specs=[HBM_SPEC] * len(arrays),
        out_shape=[pltpu.HBM(a.shape, a.dtype) for a in arrays],
        input_output_aliases={i: i for i in range(len(arrays))},
        compiler_params=pltpu.CompilerParams(has_side_effects=pltpu.SideEffectType.DATAFLOW_SIDE_EFFECTING),
    )(*arrays, send_sems, recv_sems, after)
    return out[:n_s], out[n_s:]


def _pack(arrays, row_multiple):
    flat = jnp.concatenate([a.reshape(-1) for a in arrays])
    quantum = row_multiple * FLAT_LANES
    padded = -(-flat.shape[0] // quantum) * quantum
    return jnp.pad(flat, (0, padded - flat.shape[0])).reshape(-1, FLAT_LANES)


def _unpack(flat, like):
    flat = flat.reshape(-1)
    out, at = [], 0
    for a in like:
        size = math.prod(a.shape)
        out.append(flat[at:at + size].reshape(a.shape))
        at += size
    return out


def kernel(x, a_norm, a_w_in, a_sgu_norm, a_w_spatial, a_b_spatial, a_w_out, kv_norm, w_kv, b_norm, b_w_q, b_rel_bias, b_w_o, ffn_norm, ffn_w_gate_up, ffn_w_down, final_norm, loss_target, m_a_norm, m_a_w_in, m_a_sgu_norm, m_a_w_spatial, m_a_b_spatial, m_a_w_out, m_kv_norm, m_w_kv, m_b_norm, m_b_w_q, m_b_rel_bias, m_b_w_o, m_ffn_norm, m_ffn_w_gate_up, m_ffn_w_down, m_final_norm, v_a_norm, v_a_w_in, v_a_sgu_norm, v_a_w_spatial, v_a_b_spatial, v_a_w_out, v_kv_norm, v_w_kv, v_b_norm, v_b_w_q, v_b_rel_bias, v_b_w_o, v_ffn_norm, v_ffn_w_gate_up, v_ffn_w_down, v_final_norm):
    xs = x[0]
    target = loss_target[0]
    t, d = xs.shape
    n_a = a_w_in.shape[0]
    n_b = b_w_q.shape[0]
    depth = ffn_w_gate_up.shape[0]
    f_a = a_w_out.shape[1] * N_DEV
    gd = f_a // A_GROUPS
    nb_ffn = ffn_w_gate_up.shape[2]
    me = _linear(_position())

    small_rows = -(-(a_norm.size + a_sgu_norm.size) // (8 * 128)) * 8
    small = jnp.pad(jnp.concatenate([a_norm.reshape(-1), a_sgu_norm.reshape(-1)]),
                    (0, small_rows * 128 - a_norm.size - a_sgu_norm.size)).reshape(1, small_rows, 128)

    def shard(w, layer=None):
        return (w if layer is None else w[layer]).astype(BF16)

    stages = []
    for layer in range(depth):
        if layer == 0:
            stages += [("a0", [shard(a_w_in, 0)]), ("a0_out", [shard(a_w_out, 0)])]
        elif layer < n_a:
            stages.append((f"a{layer}", [shard(a_w_in, layer), shard(a_w_out, layer)]))
        else:
            i = layer - n_a
            shared = [shard(w_kv)] if i == 0 else []
            stages.append((f"b{i}", shared + [shard(b_w_q, i), shard(b_w_o, i)]))
        stages.append((f"f{layer}", [shard(ffn_w_gate_up, layer), shard(ffn_w_down, layer)]))
    first = _all_gather([s[None] for s in stages[0][1]] + [small], "gather_first")
    gathered = {stages[0][0]: [g[0] for g in first[:-1]]}
    small_g = first[-1].reshape(N_DEV, -1)
    a_norm_full = small_g[:, :a_norm.size].reshape(N_DEV, n_a, -1).transpose(1, 0, 2).reshape(n_a, d)
    a_sgu_full = small_g[:, a_norm.size:a_norm.size + a_sgu_norm.size].reshape(
        N_DEV, n_a, -1).transpose(1, 0, 2).reshape(n_a, f_a)
    two_level = ("f0", "a1", "f1")
    later = [("chips" if key in two_level else "gather", shards,
              [lax.dynamic_update_slice(lax.empty((N_DEV,) + s.shape, BF16), s[None], (me, 0, 0)) for s in shards])
             for key, shards in stages[1:]]
    started, a_norm_full = _split_start(later, a_norm_full, "gather_start")
    in_flight = {key: group for (key, _), group in zip(stages[1:], started)}

    def pass_on(key, carry):
        if key in two_level and key in in_flight and in_flight[key][0] == "chips":
            _, lands = _split_wait(in_flight.pop(key), carry, f"gather_wait_{key}_chips")
            (in_flight[key],), carry = _split_start([("forward", [], lands)], carry, f"gather_pass_on_{key}")
        return carry

    def weights(key, after):
        if key not in gathered:
            _, gathered[key] = _split_wait(in_flight.pop(key), after, f"gather_wait_{key}")
        return gathered[key]

    rows_down = ffn_w_down.shape[1]

    def mixer_a_weights(i, after):
        if i == 0:
            (w_in,), (w_out,) = weights("a0", after[0]), weights("a0_out", after[1])
        else:
            w_in, w_out = weights(f"a{i}", after[0])
        return w_in[None], w_out.reshape(1, f_a, d)

    def mixer_b_weights(i, after):
        ws = weights(f"b{i}", after)
        return ws[-2].reshape(1, d, d), ws[-1].reshape(1, d, d)

    def ffn_weights(layer, after):
        w_gu, w_dn = weights(f"f{layer}", after)
        return w_gu[None], w_dn.reshape(1, N_DEV // 2, 2 * rows_down, d)

    w_sp_t = jnp.swapaxes(a_w_spatial, -1, -2)
    b_full = jnp.repeat(jnp.swapaxes(a_b_spatial, -1, -2), gd, axis=-1)

    saved = []

    def ffn_fwd(xin, layer):
        hf = _rms_fwd(xin, ffn_norm[layer], f"ffn_norm_fwd_{layer}")
        w_gu, w_dn = ffn_weights(layer, xin)
        dact, act = _ffn_gate_up(f"ffn_gate_up_{layer}", hf, w_gu, 0)
        act = pass_on(f"a{layer + 1}", act)
        xout = _mm_down(f"ffn_down_{layer}", act, w_dn, 0, xin)
        return xout, (xin, hf, dact, act)

    for i in range(n_a):
        h = _rms_fwd(xs, a_norm_full[i], f"a_norm_fwd_{i}")
        zpre = _mm_colblock(f"a_in_{i}", h, weights(f"a{i}", xs)[0][None], 0)
        p, zs, dgs = _sgu_fwd(zpre, a_sgu_full[i], a_w_spatial[i], b_full[i], f"a_sgu_fwd_{i}")
        p = pass_on(f"f{i}", p)
        w_in, w_out = mixer_a_weights(i, (xs, p))
        x_mid = _mm_natural(f"a_out_{i}", p, w_out, 0, res=xs)
        x_out, ffn_saved = ffn_fwd(x_mid, i)
        saved.append((xs, h, zs, dgs, p, ffn_saved))
        xs = x_out

    x_kv = xs
    w_kv_g = weights("b0", x_kv)[0][None]
    h_kv = _rms_fwd(x_kv, kv_norm, "kv_norm_fwd")
    kv = _mm_colblock("kv_proj", h_kv, w_kv_g, 0)
    kvpad = jnp.pad(kv, ((LEFT, 0), (0, 0)))

    biases = [_bias_block(_bias_build(b_rel_bias[i], f"rel_bias_{i}")) for i in range(n_b)]
    for i in range(n_b):
        layer = n_a + i
        w_q, w_o = mixer_b_weights(i, xs)
        hb = _rms_fwd(xs, b_norm[i], f"b_norm_fwd_{i}")
        q = _mm_natural(f"b_q_{i}", hb, w_q, 0, out_dtype=BF16, scale=ATTN_SCALE)
        o = _attn_fwd(q, kvpad, biases[i], f"b_attn_fwd_{i}")
        x_mid = _mm_natural(f"b_o_{i}", o, w_o, 0, res=xs)
        x_out, ffn_saved = ffn_fwd(x_mid, layer)
        saved.append((xs, hb, q, o, ffn_saved))
        xs = x_out

    dx, loss_local, g_final = _loss_head(xs, final_norm, target, "loss_head")
    loss = lax.psum(loss_local, ("x", "y", "c"))

    big_grads = {}
    pending = []
    in_flight_grads = []

    def start_exchange(dx, tag):
        srcs = [big_grads[key] for key in pending]
        lands = [lax.empty(s.shape, BF16) for s in srcs]
        (group,), dx = _split_start([("exchange", srcs, lands)], dx, f"exchange_start_{tag}")
        in_flight_grads.append((list(pending), group, tag))
        pending.clear()
        return dx

    g_ffn_norm = [None] * depth
    g_a_norm = [None] * n_a
    g_a_sgu = [None] * n_a
    g_w_sp = [None] * n_a
    g_b_sp = [None] * n_a
    g_b_norm = [None] * n_b
    g_rel = [None] * n_b

    def ffn_bwd(dx, layer, ffn_saved):
        eager = layer < n_a
        xin, hf, dact, act = ffn_saved
        big_grads["ffn_w_down", layer] = _mm_dw_down(f"ffn_down_dw_{layer}", act, dx)
        pending.append(("ffn_w_down", layer))
        if eager:
            dx = start_exchange(dx, f"f{layer}_down")
        w_gu, w_dn = ffn_weights(layer, xin)
        dgu = _ffn_down_dx(f"ffn_down_dx_{layer}", dx, w_dn, 0, dact).reshape(N_DEV, t, nb_ffn)
        big_grads["ffn_w_gate_up", layer] = _mm_dw_colblock(
            f"ffn_gate_up_dw_{layer}", hf, dgu, blocked_in=True, transposed=True)
        pending.append(("ffn_w_gate_up", layer))
        if eager:
            dx = start_exchange(dx, f"f{layer}_gate_up")
        dx, g_ffn_norm[layer] = _mm_t_colblock_norm_bwd(
            f"ffn_gate_up_dx_{layer}", dgu, w_gu, 0, xin, ffn_norm[layer], dx, blocked_in=True)
        return dx

    dk = dv = None
    for i in reversed(range(n_b)):
        layer = n_a + i
        x_in, hb, q, o, ffn_saved = saved[layer]
        dx = ffn_bwd(dx, layer, ffn_saved)
        big_grads["b_w_o", i] = _mm_dw_natural(f"b_o_dw_{i}", o, dx)
        w_q, w_o = mixer_b_weights(i, x_in)
        do = _mm_t_natural(f"b_o_dx_{i}", dx, w_o, 0)
        dq, dk, dv, dbias = _attn_bwd(q, kvpad, biases[i], do, dk, dv, f"b_attn_bwd_{i}")
        g_rel[i] = _bias_grad(dbias, f"rel_bias_grad_{i}")
        big_grads["b_w_q", i] = _mm_dw_natural(f"b_q_dw_{i}", hb, dq)
        pending.extend([("b_w_o", i), ("b_w_q", i)])
        dx, g_b_norm[i] = _mm_t_natural_norm_bwd(f"b_q_dx_{i}", dq, w_q, 0, x_in, b_norm[i], dx)
        if i > 0:
            dx = start_exchange(dx, f"b{i}")

    dkv = jnp.concatenate([dk[LEFT:], dv[LEFT:]], axis=1).astype(BF16)
    big_grads["w_kv", 0] = _mm_dw_colblock("kv_proj_dw", h_kv, dkv)
    pending.append(("w_kv", 0))
    dx, g_kv_norm = _mm_t_colblock_norm_bwd("kv_proj_dx", dkv, w_kv_g, 0, x_kv, kv_norm, dx)
    dx = start_exchange(dx, "kv")

    for i in reversed(range(n_a)):
        x_in, h, zs, dgs, p, ffn_saved = saved[i]
        dx = ffn_bwd(dx, i, ffn_saved)
        big_grads["a_w_out", i] = _mm_dw_natural(f"a_out_dw_{i}", p, dx)
        pending.append(("a_w_out", i))
        dx = start_exchange(dx, f"a{i}_out")
        w_in, w_out = mixer_a_weights(i, (x_in, p))
        dp = _mm_t_natural(f"a_out_dx_{i}", dx, w_out, 0)
        dz, g_w_sp[i], g_b_sp[i], g_a_sgu[i] = _sgu_bwd(
            zs, dgs, dp, a_sgu_full[i], a_w_spatial[i], w_sp_t[i], b_full[i], f"a_sgu_bwd_{i}")
        big_grads["a_w_in", i] = _mm_dw_colblock(f"a_in_dw_{i}", h, dz)
        pending.append(("a_w_in", i))
        dx = start_exchange(dx, f"a{i}_in")
        dx, g_a_norm[i] = _mm_t_colblock_norm_bwd(f"a_in_dx_{i}", dz, w_in, 0, x_in, a_norm_full[i], dx)
    grad_x = dx[None]

    small_like = [jax.ShapeDtypeStruct((n_a, d), F32), jax.ShapeDtypeStruct((n_a, f_a), F32),
                  a_w_spatial, a_b_spatial, kv_norm, b_norm, b_rel_bias, ffn_norm, final_norm]
    small_partial = _pack(
        [jnp.stack(g_a_norm), jnp.stack(g_a_sgu), jnp.stack(g_w_sp), jnp.stack(g_b_sp), g_kv_norm,
         jnp.stack(g_b_norm), jnp.stack(g_rel), jnp.stack(g_ffn_norm), g_final], N_DEV * 8)
    chunk_rows = small_partial.shape[0] // N_DEV
    arrived = {}
    for keys, group, tag in in_flight_grads:
        srcs, lands = _split_wait(group, dx, f"exchange_wait_{tag}")
        for key, src, land in zip(keys, srcs, lands):
            arrived[key] = (land, src)
    small_got = _exchange([small_partial.reshape(1, N_DEV, chunk_rows, FLAT_LANES)], "exchange_small")[0]
    small_sum = _ordered_sum(small_got[0], "small_grad_sum")
    small_all = _all_gather([small_sum[None]], "gather_small_grads")[0]
    (ga_norm, ga_sgu, gw_sp, gb_sp, gkv_norm, gb_norm, g_relb, gffn_norm, gfinal) = _unpack(small_all, small_like)

    results = {}
    big_names = ["a_w_in", "a_w_out", "w_kv", "b_w_q", "b_w_o", "ffn_w_gate_up", "ffn_w_down"]
    big_wmv = [(a_w_in, m_a_w_in, v_a_w_in), (a_w_out, m_a_w_out, v_a_w_out),
               (w_kv[None], m_w_kv[None], v_w_kv[None]), (b_w_q, m_b_w_q, v_b_w_q), (b_w_o, m_b_w_o, v_b_w_o),
               tuple(jnp.swapaxes(a, 1, 2) for a in (ffn_w_gate_up, m_ffn_w_gate_up, v_ffn_w_gate_up)),
               (ffn_w_down, m_ffn_w_down, v_ffn_w_down)]
    me_arr = jnp.reshape(me, (1,)).astype(jnp.int32)
    for name, (w, m, v) in zip(big_names, big_wmv):
        outs = None
        for layer in range(w.shape[0]):
            got, own = arrived[name, layer]
            outs = _adamw_layer(got, own, w, m, v, layer, outs, me_arr, f"adamw_{name}_{layer}")
        if name == "w_kv":
            outs = [o[0] for o in outs]
        if name == "ffn_w_gate_up":
            outs = [jnp.swapaxes(o, 1, 2) for o in outs]
        results[name] = outs

    n_cols = a_norm.shape[1]
    s_cols = a_sgu_norm.shape[1]
    small_g_list = [lax.dynamic_slice(ga_norm, (0, me * n_cols), (n_a, n_cols)),
                    lax.dynamic_slice(ga_sgu, (0, me * s_cols), (n_a, s_cols)),
                    gw_sp, gb_sp, gkv_norm, gb_norm, g_relb, gffn_norm, gfinal]
    small_names = ["a_norm", "a_sgu_norm", "a_w_spatial", "a_b_spatial", "kv_norm", "b_norm", "b_rel_bias",
                   "ffn_norm", "final_norm"]
    small_w = [a_norm, a_sgu_norm, a_w_spatial, a_b_spatial, kv_norm, b_norm, b_rel_bias, ffn_norm, final_norm]
    small_m = [m_a_norm, m_a_sgu_norm, m_a_w_spatial, m_a_b_spatial, m_kv_norm, m_b_norm, m_b_rel_bias,
               m_ffn_norm, m_final_norm]
    small_v = [v_a_norm, v_a_sgu_norm, v_a_w_spatial, v_a_b_spatial, v_kv_norm, v_b_norm, v_b_rel_bias,
               v_ffn_norm, v_final_norm]
    flat_g = _pack(small_g_list, 8)
    flat_out = _adamw(flat_g[None, None], _pack(small_w, 8)[None], _pack(small_m, 8)[None],
                      _pack(small_v, 8)[None], "adamw_small")
    unpacked = [_unpack(o[0], small_w) for o in flat_out]
    for idx, name in enumerate(small_names):
        results[name] = [unpacked[kind][idx] for kind in range(4)]

    order = ["a_norm", "a_w_in", "a_sgu_norm", "a_w_spatial", "a_b_spatial", "a_w_out", "kv_norm", "w_kv",
             "b_norm", "b_w_q", "b_rel_bias", "b_w_o", "ffn_norm", "ffn_w_gate_up", "ffn_w_down", "final_norm"]
    outputs = [loss, grad_x]
    for kind in range(4):
        outputs += [results[name][kind] for name in order]
    return tuple(outputs)
```

```python
import math

import jax
import jax.numpy as jnp
from jax import lax
from jax.experimental import pallas as pl
from jax.experimental.pallas import tpu as pltpu

F32 = jnp.float32
BF16 = jnp.bfloat16
MESH = pl.DeviceIdType.MESH
HBM_SPEC = pl.BlockSpec(memory_space=pltpu.HBM)
SEM_SPEC = pl.BlockSpec(memory_space=pltpu.SEMAPHORE)

N_DEV = 8
CHUNK = 64
A_CHUNK = 128
A_GROUPS = 8
N_LEFT_CHUNKS = 8
LEFT = N_LEFT_CHUNKS * CHUNK
PAIR_ROWS = 2 * CHUNK
PAIR_BAND = PAIR_ROWS + LEFT
DIAGONALS = PAIR_BAND + PAIR_ROWS
PAIRS_PER_BLOCK = 2
Q_BLOCK = PAIRS_PER_BLOCK * PAIR_ROWS
K_BLOCK = Q_BLOCK + LEFT
ATTN_UNROLL = 7
MAX_REL = 256
N_REL = 2 * MAX_REL + 1
REL_PAD = 640
HEAD_DIM = 64
HEAD_PAIR = 2 * HEAD_DIM
ATTN_SCALE = HEAD_DIM ** -0.5
EPS = 1e-6
NEG_INF = -1e30
ADAM_LR = 0.001
ADAM_B1 = 0.9
ADAM_B2 = 0.999
ADAM_EPS = 1e-08
ADAM_WD = 0.01
ADAM_STEP = 10
FLAT_LANES = 1024
F32_SUBLANES = 8
BF16_SUBLANES = 16
ADAMW_BLOCK_ELEMS = 256 * 1024
V7X_VMEM_BYTES = 64 * 1024 * 1024
VMEM_FLOOR_BYTES = 32 * 1024 * 1024
VMEM_CEIL_BYTES = V7X_VMEM_BYTES - 8 * 1024 * 1024

NN = (((1,), (0,)), ((), ()))
NT = (((1,), (1,)), ((), ()))
TN = (((0,), (0,)), ((), ()))


def _tile(n, pref):
    return pref if n % pref == 0 else n


def _row_tile(n, pref, mult):
    best = None
    for t in range(mult, min(n, pref) + 1, mult):
        if n % t == 0:
            best = t
    return best if best is not None else n


def _nbytes(shape, dtype):
    n = 1
    for s in shape:
        if s is not None:
            n *= s
    return n * jnp.dtype(dtype).itemsize


def _call(body, name, grid, in_specs, out_specs, out_shape, scratch=(), vmem_bytes=0, aliases=None):
    limit = int(min(max(VMEM_FLOOR_BYTES, vmem_bytes * 5 // 4), VMEM_CEIL_BYTES))
    return pl.pallas_call(
        body,
        name=name,
        grid=grid,
        in_specs=in_specs,
        out_specs=out_specs,
        out_shape=out_shape,
        scratch_shapes=list(scratch),
        input_output_aliases=aliases or {},
        compiler_params=pltpu.CompilerParams(
            dimension_semantics=("arbitrary",) * len(grid), vmem_limit_bytes=limit),
    )


ERFC_P = 0.3275911 / math.sqrt(2.0)
ERFC_HALF_COEFFS = tuple(0.5 * a for a in (1.061405429, -1.453152027, 1.421413741, -0.284496736, 0.254829592))


def _gelu_and_grad(x):
    d = 1.0 + ERFC_P * jnp.abs(x)
    r = pl.reciprocal(d, approx=True)
    t = r * (2.0 - d * r)
    a5, a4, a3, a2, a1 = ERFC_HALF_COEFFS
    ex = jnp.exp(-0.5 * (x * x))
    tail = ((((a5 * t + a4) * t + a3) * t + a2) * t + a1) * t * ex
    cdf = jnp.where(x < 0, tail, 1.0 - tail)
    return x * cdf, cdf + x * ex * (1.0 / math.sqrt(2.0 * math.pi))


def _sigmoid(x):
    return 1.0 / (1.0 + jnp.exp(-x))


def _split3(x):
    hi = x.astype(BF16)
    r1 = x - hi.astype(F32)
    mid = r1.astype(BF16)
    lo = (r1 - mid.astype(F32)).astype(BF16)
    return hi, mid, lo


def _rms_fwd(x, g, name):
    t, d = x.shape
    tm = _tile(t, 512)

    def body(x_ref, g_ref, o_ref):
        xf = x_ref[...]
        r = lax.rsqrt(jnp.mean(xf * xf, axis=-1, keepdims=True) + EPS)
        o_ref[...] = (xf * r * g_ref[...]).astype(o_ref.dtype)

    return _call(
        body, name, (t // tm,),
        [pl.BlockSpec((tm, d), lambda i: (i, 0)), pl.BlockSpec((1, d), lambda i: (0, 0))],
        pl.BlockSpec((tm, d), lambda i: (i, 0)),
        jax.ShapeDtypeStruct((t, d), BF16),
        vmem_bytes=2 * (_nbytes((tm, d), F32) + _nbytes((tm, d), BF16)) + 4 * _nbytes((tm, d), F32),
    )(x, g.reshape(1, d))


def _mm(name, dims, a, b, *, grid, a_spec, b_spec, out_shape, out_spec, acc_shape,
        res=None, res_spec=None, scale=None):
    nk = grid[2]
    has_res = res is not None

    def body(*refs):
        refs = list(refs)
        a_ref = refs.pop(0)
        b_ref = refs.pop(0)
        r_ref = refs.pop(0) if has_res else None
        o_ref = refs.pop(0)
        part = lax.dot_general(a_ref[...].astype(BF16), b_ref[...].astype(BF16), dims,
                               preferred_element_type=F32)

        def finish(acc):
            if scale is not None:
                acc = acc * scale
            if has_res:
                acc = acc + r_ref[...]
            o_ref[...] = acc.astype(o_ref.dtype)

        if nk == 1:
            finish(part)
        else:
            acc_ref = refs.pop(0)
            k = pl.program_id(2)

            @pl.when(k == 0)
            def _():
                acc_ref[...] = part

            @pl.when(k > 0)
            def _():
                acc_ref[...] += part

            @pl.when(k == nk - 1)
            def _():
                finish(acc_ref[...])

    operands = [a, b]
    in_specs = [a_spec, b_spec]
    vmem = 2 * (_nbytes(a_spec.block_shape, a.dtype) + _nbytes(b_spec.block_shape, b.dtype)
                + _nbytes(out_spec.block_shape, out_shape.dtype))
    vmem += 3 * _nbytes(acc_shape, F32)
    if has_res:
        operands.append(res)
        in_specs.append(res_spec)
        vmem += 2 * _nbytes(res_spec.block_shape, res.dtype)
    scratch = [pltpu.VMEM(acc_shape, F32)] if nk > 1 else []
    return _call(body, name, grid, in_specs, out_spec, out_shape, scratch=scratch, vmem_bytes=vmem)(*operands)


def _mm_colblock(name, h, w_g, layer):
    t, k = h.shape
    nb = w_g.shape[3]
    tm = _tile(t, 2048)
    return _mm(
        name, NN, h, w_g, grid=(t // tm, N_DEV, 1),
        a_spec=pl.BlockSpec((tm, k), lambda i, j, kk: (i, 0)),
        b_spec=pl.BlockSpec((None, None, k, nb), lambda i, j, kk: (layer, j, 0, 0)),
        out_shape=jax.ShapeDtypeStruct((t, N_DEV * nb), BF16),
        out_spec=pl.BlockSpec((tm, nb), lambda i, j, kk: (i, j)), acc_shape=(tm, nb))


def _mm_natural(name, a, w, layer, *, res=None, out_dtype=F32, scale=None):
    t, k = a.shape
    n = w.shape[2]
    tm = _tile(t, 1024)
    tn = _tile(n, 1024 if k <= 1024 else 512)
    res_spec = None if res is None else pl.BlockSpec((tm, tn), lambda i, j, kk: (i, j))
    return _mm(
        name, NN, a, w, grid=(t // tm, n // tn, 1),
        a_spec=pl.BlockSpec((tm, k), lambda i, j, kk: (i, 0)),
        b_spec=pl.BlockSpec((None, k, tn), lambda i, j, kk: (layer, 0, j)),
        out_shape=jax.ShapeDtypeStruct((t, n), out_dtype),
        out_spec=pl.BlockSpec((tm, tn), lambda i, j, kk: (i, j)),
        acc_shape=(tm, tn), res=res, res_spec=res_spec, scale=scale)


def _mm_down(name, act, w4, layer, res):
    nblk, t, kb = act.shape
    n = w4.shape[3]
    tm = _tile(t, 1024)

    def body(a_ref, b_ref, r_ref, o_ref):
        acc = r_ref[...]
        for u in range(nblk):
            acc = acc + jnp.dot(a_ref[u], b_ref[u], preferred_element_type=F32)
        o_ref[...] = acc

    row = pl.BlockSpec((tm, n), lambda i: (i, 0))
    return _call(
        body, name, (t // tm,),
        [pl.BlockSpec((nblk, tm, kb), lambda i: (0, i, 0)),
         pl.BlockSpec((None, nblk, kb, n), lambda i: (layer, 0, 0, 0)),
         row],
        row,
        jax.ShapeDtypeStruct((t, n), F32),
        vmem_bytes=2 * (_nbytes((nblk, tm, kb), BF16) + _nbytes((nblk, kb, n), BF16)) + 6 * _nbytes((tm, n), F32),
    )(act, w4, res)


def _mm_t_colblock_norm_bwd(name, dz, w_g, layer, x, g, dx_up, blocked_in=False):
    k = w_g.shape[2]
    nb = w_g.shape[3]
    t = x.shape[0]
    tm = _tile(t, 1024)
    per_step = 4 if nb * k <= 512 * 1024 else 2
    n_steps = N_DEV // per_step
    if blocked_in:
        a_spec = pl.BlockSpec((per_step, tm, nb), lambda i, kk: (kk, i, 0))
    else:
        a_spec = pl.BlockSpec((tm, per_step * nb), lambda i, kk: (i, kk))

    def body(a_ref, b_ref, x_ref, g_ref, up_ref, dx_ref, dg_ref, acc_ref):
        i = pl.program_id(0)
        kk = pl.program_id(1)
        part = None
        for u in range(per_step):
            a = a_ref[u] if blocked_in else a_ref[:, u * nb:(u + 1) * nb]
            term = lax.dot_general(a.astype(BF16), b_ref[u].astype(BF16), NT, preferred_element_type=F32)
            part = term if part is None else part + term

        @pl.when(kk == 0)
        def _():
            acc_ref[...] = part

        @pl.when(kk > 0)
        def _():
            acc_ref[...] += part

        @pl.when((i == 0) & (kk == 0))
        def _():
            dg_ref[...] = jnp.zeros_like(dg_ref)

        @pl.when(kk == n_steps - 1)
        def _():
            dy = acc_ref[...]
            xf = x_ref[...]
            r = lax.rsqrt(jnp.mean(xf * xf, axis=-1, keepdims=True) + EPS)
            xhat = xf * r
            dxhat = dy * g_ref[...]
            dg_ref[...] += jnp.sum(dy * xhat, axis=0, keepdims=True)
            dx_ref[...] = up_ref[...] + r * (dxhat - xhat * jnp.mean(dxhat * xhat, axis=-1, keepdims=True))

    row = pl.BlockSpec((tm, k), lambda i, kk: (i, 0))
    vec = pl.BlockSpec((1, k), lambda i, kk: (0, 0))
    dx, dg = _call(
        body, name, (t // tm, n_steps),
        [a_spec, pl.BlockSpec((None, per_step, k, nb), lambda i, kk: (layer, kk, 0, 0)), row, vec, row],
        [row, vec],
        [jax.ShapeDtypeStruct((t, k), F32), jax.ShapeDtypeStruct((1, k), F32)],
        scratch=[pltpu.VMEM((tm, k), F32)],
        vmem_bytes=2 * per_step * (_nbytes((tm, nb), BF16) + _nbytes((k, nb), BF16)) + 10 * _nbytes((tm, k), F32),
    )(dz, w_g, x, g.reshape(1, k), dx_up)
    return dx, dg.reshape(k)


def _ffn_gate_up(name, h, w_g, layer):
    t, k = h.shape
    nb = w_g.shape[3]
    half = N_DEV // 2
    tm = _tile(t, 1024)

    def body(h_ref, wg_ref, wu_ref, dact_ref, act_ref):
        hb = h_ref[...]
        gate = jnp.dot(hb, wg_ref[...], preferred_element_type=F32)
        up = jnp.dot(hb, wu_ref[...], preferred_element_type=F32)
        sig = _sigmoid(gate)
        silu = gate * sig
        dact_ref[0] = (up * (sig * (1.0 + gate * (1.0 - sig)))).astype(BF16)
        dact_ref[1] = silu.astype(BF16)
        act_ref[...] = (silu * up).astype(BF16)

    return _call(
        body, name, (t // tm, half),
        [pl.BlockSpec((tm, k), lambda i, j: (i, 0)),
         pl.BlockSpec((None, None, k, nb), lambda i, j: (layer, j, 0, 0)),
         pl.BlockSpec((None, None, k, nb), lambda i, j: (layer, half + j, 0, 0))],
        [pl.BlockSpec((2, None, tm, nb), lambda i, j: (0, j, i, 0)),
         pl.BlockSpec((None, tm, nb), lambda i, j: (j, i, 0))],
        [jax.ShapeDtypeStruct((2, half, t, nb), BF16), jax.ShapeDtypeStruct((half, t, nb), BF16)],
        vmem_bytes=2 * (_nbytes((tm, k), BF16) + 2 * _nbytes((k, nb), BF16) + 3 * _nbytes((tm, nb), BF16))
        + 8 * _nbytes((tm, nb), F32),
    )(h, w_g, w_g)


def _ffn_down_dx(name, dy, w4, layer, dact):
    t, n = dy.shape
    nblk, kb = w4.shape[1], w4.shape[2]
    tm = _tile(t, 1024)

    def body(dy_ref, w_ref, dact_ref, dgu_ref):
        da = lax.dot_general(dy_ref[...].astype(BF16), w_ref[...], NT, preferred_element_type=F32)
        dgu_ref[0] = (da * dact_ref[0].astype(F32)).astype(BF16)
        dgu_ref[1] = (da * dact_ref[1].astype(F32)).astype(BF16)

    blk = pl.BlockSpec((2, None, tm, kb), lambda i, j: (0, j, i, 0))
    return _call(
        body, name, (t // tm, nblk),
        [pl.BlockSpec((tm, n), lambda i, j: (i, 0)),
         pl.BlockSpec((None, None, kb, n), lambda i, j: (layer, j, 0, 0)),
         blk],
        blk,
        jax.ShapeDtypeStruct((2, nblk, t, kb), BF16),
        vmem_bytes=2 * (_nbytes((tm, n), F32) + _nbytes((kb, n), BF16) + 4 * _nbytes((tm, kb), BF16))
        + 8 * _nbytes((tm, kb), F32),
    )(dy, w4, dact)


def _mm_t_natural(name, dy, w, layer):
    t, n = dy.shape
    k = w.shape[1]
    tm = _tile(t, 1024)
    tk = _tile(k, 1024)
    return _mm(
        name, NT, dy, w, grid=(t // tm, k // tk, 1),
        a_spec=pl.BlockSpec((tm, n), lambda i, j, kk: (i, 0)),
        b_spec=pl.BlockSpec((None, tk, n), lambda i, j, kk: (layer, j, 0)),
        out_shape=jax.ShapeDtypeStruct((t, k), BF16),
        out_spec=pl.BlockSpec((tm, tk), lambda i, j, kk: (i, j)),
        acc_shape=(tm, tk))


def _mm_t_natural_norm_bwd(name, dy, w, layer, x, g, dx_up):
    t, n = dy.shape
    k = w.shape[1]
    tm = _tile(t, 1024)

    def body(a_ref, b_ref, x_ref, g_ref, up_ref, dx_ref, dg_ref):
        @pl.when(pl.program_id(0) == 0)
        def _():
            dg_ref[...] = jnp.zeros_like(dg_ref)

        dh = lax.dot_general(a_ref[...].astype(BF16), b_ref[...], NT, preferred_element_type=F32)
        xf = x_ref[...]
        r = lax.rsqrt(jnp.mean(xf * xf, axis=-1, keepdims=True) + EPS)
        xhat = xf * r
        dxhat = dh * g_ref[...]
        dg_ref[...] += jnp.sum(dh * xhat, axis=0, keepdims=True)
        dx_ref[...] = up_ref[...] + r * (dxhat - xhat * jnp.mean(dxhat * xhat, axis=-1, keepdims=True))

    row = pl.BlockSpec((tm, k), lambda i: (i, 0))
    vec = pl.BlockSpec((1, k), lambda i: (0, 0))
    dx, dg = _call(
        body, name, (t // tm,),
        [pl.BlockSpec((tm, n), lambda i: (i, 0)), pl.BlockSpec((None, k, n), lambda i: (layer, 0, 0)), row, vec, row],
        [row, vec],
        [jax.ShapeDtypeStruct((t, k), F32), jax.ShapeDtypeStruct((1, k), F32)],
        vmem_bytes=2 * (_nbytes((tm, n), dy.dtype) + _nbytes((k, n), BF16)) + 10 * _nbytes((tm, k), F32),
    )(dy, w, x, g.reshape(1, k), dx_up)
    return dx, dg.reshape(k)


def _mm_dw_colblock(name, h, dz, blocked_in=False, transposed=False):
    t, k = h.shape
    nb = dz.shape[2] if blocked_in else dz.shape[1] // N_DEV
    tk = _tile(t, 2048)
    h_spec = pl.BlockSpec((tk, k), lambda i, j, kk: (kk, 0))
    if blocked_in:
        dz_spec = pl.BlockSpec((None, tk, nb), lambda i, j, kk: (j, kk, 0))
    else:
        dz_spec = pl.BlockSpec((tk, nb), lambda i, j, kk: (kk, j))
    rows, cols = (nb, k) if transposed else (k, nb)
    return _mm(
        name, TN, *((dz, h) if transposed else (h, dz)), grid=(1, N_DEV, t // tk),
        a_spec=dz_spec if transposed else h_spec,
        b_spec=h_spec if transposed else dz_spec,
        out_shape=jax.ShapeDtypeStruct((N_DEV, rows, cols), BF16),
        out_spec=pl.BlockSpec((None, rows, cols), lambda i, j, kk: (j, 0, 0)),
        acc_shape=(rows, cols))


def _mm_dw_natural(name, a, dy):
    t, k = a.shape
    n = dy.shape[1]
    tko = _tile(k, 1024)
    tt = _tile(t, 2048)
    out = _mm(
        name, TN, a, dy, grid=(k // tko, 1, t // tt),
        a_spec=pl.BlockSpec((tt, tko), lambda i, j, kk: (kk, i)),
        b_spec=pl.BlockSpec((tt, n), lambda i, j, kk: (kk, 0)),
        out_shape=jax.ShapeDtypeStruct((k, n), BF16),
        out_spec=pl.BlockSpec((tko, n), lambda i, j, kk: (i, 0)),
        acc_shape=(tko, n))
    return out.reshape(N_DEV, k // N_DEV, n)


def _mm_dw_down(name, act, dy):
    nblk, t, kb = act.shape
    n = dy.shape[1]
    tt = _tile(t, 2048)
    out = _mm(
        name, TN, act, dy, grid=(nblk, 1, t // tt),
        a_spec=pl.BlockSpec((None, tt, kb), lambda i, j, kk: (i, kk, 0)),
        b_spec=pl.BlockSpec((tt, n), lambda i, j, kk: (kk, 0)),
        out_shape=jax.ShapeDtypeStruct((nblk, kb, n), BF16),
        out_spec=pl.BlockSpec((None, kb, n), lambda i, j, kk: (i, 0, 0)),
        acc_shape=(kb, n))
    return out.reshape(N_DEV, (nblk * kb) // N_DEV, n)


def _spatial_mask(transposed=False):
    r = lax.broadcasted_iota(jnp.int32, (A_CHUNK, A_CHUNK), 0) // CHUNK
    c = lax.broadcasted_iota(jnp.int32, (A_CHUNK, A_CHUNK), 1) // CHUNK
    return c >= r if transposed else r >= c


def _sgu_tile(t):
    return _tile(t, 2 * A_CHUNK)


def _sgu_fwd(zpre, g_sgu, w_sp, b_full, name):
    t, f2 = zpre.shape
    f = f2 // 2
    gd = f // A_GROUPS
    tm = _sgu_tile(t)

    def body(z_ref, g_ref, w_ref, b_ref, p_ref, zs_ref, dg_ref):
        mask = _spatial_mask()
        wm = [jnp.where(mask, w_ref[g], 0.0).astype(BF16) for g in range(A_GROUPS)]
        for c in range(tm // A_CHUNK):
            rows = pl.ds(c * A_CHUNK, A_CHUNK)
            z, dgelu = _gelu_and_grad(z_ref[rows, :].astype(F32))
            zs_ref[rows, :] = z.astype(BF16)
            dg_ref[rows, :] = dgelu.astype(BF16)
            u = z[:, :f]
            v0 = z[:, f:]
            r = lax.rsqrt(jnp.mean(v0 * v0, axis=-1, keepdims=True) + EPS)
            v1 = (v0 * r * g_ref[...]).astype(BF16)
            for g in range(A_GROUPS):
                cols = slice(g * gd, (g + 1) * gd)
                v2 = jnp.dot(wm[g], v1[:, cols], preferred_element_type=F32) + b_ref[:, cols]
                p_ref[rows, cols] = (u[:, cols] * v2).astype(BF16)

    return _call(
        body, name, (t // tm,),
        [pl.BlockSpec((tm, f2), lambda i: (i, 0)),
         pl.BlockSpec((1, f), lambda i: (0, 0)),
         pl.BlockSpec((A_GROUPS, A_CHUNK, A_CHUNK), lambda i: (0, 0, 0)),
         pl.BlockSpec((A_CHUNK, f), lambda i: (0, 0))],
        [pl.BlockSpec((tm, f), lambda i: (i, 0)), pl.BlockSpec((tm, f2), lambda i: (i, 0)),
         pl.BlockSpec((tm, f2), lambda i: (i, 0))],
        [jax.ShapeDtypeStruct((t, f), BF16), jax.ShapeDtypeStruct((t, f2), BF16), jax.ShapeDtypeStruct((t, f2), BF16)],
        vmem_bytes=6 * _nbytes((tm, f2), BF16) + 2 * _nbytes((tm, f), BF16) + 8 * _nbytes((A_CHUNK, f2), F32),
    )(zpre, g_sgu.reshape(1, f), w_sp, b_full)


def _sgu_bwd(zs, dgs, dp, g_sgu, w_sp, w_sp_t, b_full, name):
    t, f2 = zs.shape
    f = f2 // 2
    gd = f // A_GROUPS
    tm = _sgu_tile(t)
    n_steps = t // tm

    def body(z_ref, dgelu_ref, dp_ref, g_ref, w_ref, wt_ref, b_ref, dz_ref, dw_ref, db_ref, dg_ref, dv1_ref, dbf_ref):
        step = pl.program_id(0)

        @pl.when(step == 0)
        def _():
            dw_ref[...] = jnp.zeros_like(dw_ref)
            dg_ref[...] = jnp.zeros_like(dg_ref)
            dbf_ref[...] = jnp.zeros_like(dbf_ref)

        mask = _spatial_mask()
        mask_t = _spatial_mask(transposed=True)
        wm = [jnp.where(mask, w_ref[g], 0.0).astype(BF16) for g in range(A_GROUPS)]
        wmt = [jnp.where(mask_t, wt_ref[g], 0.0).astype(BF16) for g in range(A_GROUPS)]
        gain = g_ref[...]
        for c in range(tm // A_CHUNK):
            rows = pl.ds(c * A_CHUNK, A_CHUNK)
            z = z_ref[rows, :].astype(F32)
            dgelu = dgelu_ref[rows, :].astype(F32)
            u = z[:, :f]
            v0 = z[:, f:]
            r = lax.rsqrt(jnp.mean(v0 * v0, axis=-1, keepdims=True) + EPS)
            xhat = v0 * r
            v1 = (xhat * gain).astype(BF16)
            dpf = dp_ref[rows, :].astype(F32)
            for g in range(A_GROUPS):
                cols = slice(g * gd, (g + 1) * gd)
                v1g = v1[:, cols]
                v2 = jnp.dot(wm[g], v1g, preferred_element_type=F32) + b_ref[:, cols]
                dpg = dpf[:, cols]
                dz_ref[rows, cols] = (dpg * v2 * dgelu[:, cols]).astype(BF16)
                dv2 = dpg * u[:, cols]
                dbf_ref[:, cols] += dv2
                dv2b = dv2.astype(BF16)
                dwg = lax.dot_general(dv2b, v1g, NT, preferred_element_type=F32)
                dw_ref[g] += jnp.where(mask, dwg, 0.0)
                dv1_ref[:, cols] = jnp.dot(wmt[g], dv2b, preferred_element_type=F32)
            dv1 = dv1_ref[...]
            dxhat = dv1 * gain
            dg_ref[...] += jnp.sum(dv1 * xhat, axis=0, keepdims=True)
            dv0 = r * (dxhat - xhat * jnp.mean(dxhat * xhat, axis=-1, keepdims=True))
            dz_ref[rows, pl.ds(f, f)] = (dv0 * dgelu[:, f:]).astype(BF16)

        @pl.when(step == n_steps - 1)
        def _():
            for g in range(A_GROUPS):
                db_ref[g] = jnp.sum(dbf_ref[:, g * gd:(g + 1) * gd], axis=1, keepdims=True)

    wspec = pl.BlockSpec((A_GROUPS, A_CHUNK, A_CHUNK), lambda i: (0, 0, 0))
    dz, dw, db, dg = _call(
        body, name, (n_steps,),
        [pl.BlockSpec((tm, f2), lambda i: (i, 0)),
         pl.BlockSpec((tm, f2), lambda i: (i, 0)),
         pl.BlockSpec((tm, f), lambda i: (i, 0)),
         pl.BlockSpec((1, f), lambda i: (0, 0)),
         wspec, wspec,
         pl.BlockSpec((A_CHUNK, f), lambda i: (0, 0))],
        [pl.BlockSpec((tm, f2), lambda i: (i, 0)),
         wspec,
         pl.BlockSpec((A_GROUPS, A_CHUNK, 1), lambda i: (0, 0, 0)),
         pl.BlockSpec((1, f), lambda i: (0, 0))],
        [jax.ShapeDtypeStruct((t, f2), BF16),
         jax.ShapeDtypeStruct((A_GROUPS, A_CHUNK, A_CHUNK), F32),
         jax.ShapeDtypeStruct((A_GROUPS, A_CHUNK, 1), F32),
         jax.ShapeDtypeStruct((1, f), F32)],
        scratch=[pltpu.VMEM((A_CHUNK, f), F32), pltpu.VMEM((A_CHUNK, f), F32)],
        vmem_bytes=6 * _nbytes((tm, f2), BF16) + 2 * _nbytes((tm, f), BF16) + 12 * _nbytes((A_CHUNK, f2), F32),
    )(zs, dgs, dp, g_sgu.reshape(1, f), w_sp, w_sp_t, b_full)
    return dz, dw, db.reshape(A_GROUPS, A_CHUNK), dg.reshape(f)


def _pair_valid(qi, col):
    qc = qi // CHUNK
    kc = col // CHUNK
    return (kc >= qc) & (kc <= qc + N_LEFT_CHUNKS)


def _diagonal_onehot():
    e = lax.broadcasted_iota(jnp.int32, (REL_PAD, DIAGONALS), 1)
    idx = jnp.clip(PAIR_BAND - 1 - e, -MAX_REL, MAX_REL) + MAX_REL
    r = lax.broadcasted_iota(jnp.int32, (REL_PAD, DIAGONALS), 0)
    return jnp.where(r == idx, 1.0, 0.0).astype(BF16)


def _bias_build(table, name):
    h = table.shape[0]
    tab = jnp.pad(table, ((0, 0), (0, REL_PAD - N_REL)))

    def body(t_ref, o_ref):
        oh = _diagonal_onehot()
        diag = jnp.zeros((h, DIAGONALS), F32)
        for piece in _split3(t_ref[...]):
            diag += jnp.dot(piece, oh, preferred_element_type=F32)
        col = lax.broadcasted_iota(jnp.int32, (h, PAIR_BAND), 1)
        for qi in range(PAIR_ROWS):
            row = pltpu.roll(diag, (qi - (PAIR_ROWS - 1)) % DIAGONALS, 1)[:, :PAIR_BAND]
            o_ref[qi] = jnp.where(_pair_valid(qi, col), row, NEG_INF)

    out = _call(
        body, name, (1,),
        [pl.BlockSpec((h, REL_PAD), lambda i: (0, 0))],
        pl.BlockSpec((PAIR_ROWS, h, PAIR_BAND), lambda i: (0, 0, 0)),
        jax.ShapeDtypeStruct((PAIR_ROWS, h, PAIR_BAND), F32),
        vmem_bytes=4 * _nbytes((PAIR_ROWS, h, PAIR_BAND), F32),
    )(tab)
    return jnp.transpose(out, (1, 0, 2))


def _bias_block(pair_bias):
    rest = K_BLOCK - PAIR_BAND
    return jnp.concatenate(
        [jnp.pad(pair_bias, ((0, 0), (0, 0), (p * PAIR_ROWS, rest - p * PAIR_ROWS)), constant_values=NEG_INF)
         for p in range(PAIRS_PER_BLOCK)], axis=1)


def _bias_grad(dbias, name):
    h = dbias.shape[0]
    db_t = jnp.transpose(dbias, (1, 0, 2))

    def body(d_ref, o_ref):
        diag = jnp.zeros((h, DIAGONALS), F32)
        for qi in range(PAIR_ROWS):
            diag += pltpu.roll(d_ref[qi], PAIR_ROWS - 1 - qi, 1)
        oh = _diagonal_onehot()
        acc = jnp.zeros((h, REL_PAD), F32)
        for piece in _split3(diag):
            acc += lax.dot_general(piece, oh, NT, preferred_element_type=F32)
        o_ref[...] = acc

    out = _call(
        body, name, (1,),
        [pl.BlockSpec((PAIR_ROWS, h, DIAGONALS), lambda i: (0, 0, 0))],
        pl.BlockSpec((h, REL_PAD), lambda i: (0, 0)),
        jax.ShapeDtypeStruct((h, REL_PAD), F32),
        vmem_bytes=4 * _nbytes((PAIR_ROWS, h, DIAGONALS), F32),
    )(db_t)
    return out[:, :N_REL]


def _head_masks():
    lane = lax.broadcasted_iota(jnp.int32, (Q_BLOCK, HEAD_PAIR), 1)
    return lane < HEAD_DIM, lane >= HEAD_DIM


def _block_scores(qm, kb, bias, valid):
    s = lax.dot_general(qm, kb, NT, preferred_element_type=F32) + bias
    return s if valid is None else jnp.where(valid, s, NEG_INF)


def _softmax_rows(s):
    e = jnp.exp(s - jnp.max(s, axis=-1, keepdims=True))
    return e * (1.0 / jnp.sum(e, axis=-1, keepdims=True))


def _padded_then_plain(step, n_blocks):
    n_padded = min(LEFT // Q_BLOCK, n_blocks)
    lax.fori_loop(0, n_padded, lambda j, c: step(j, c, True), 0, unroll=True)
    lax.fori_loop(n_padded, n_blocks, lambda j, c: step(j, c, False), 0, unroll=ATTN_UNROLL)


def _attn_fwd(q, kvpad, bias, name):
    t, d = q.shape
    n_pairs = d // HEAD_PAIR
    n_blocks = t // Q_BLOCK

    def body(q_ref, k_ref, v_ref, b_ref, o_ref):
        masks = _head_masks()
        key = lax.broadcasted_iota(jnp.int32, (Q_BLOCK, K_BLOCK), 1)

        def step(j, carry, padded):
            r0 = pl.multiple_of(j * Q_BLOCK, Q_BLOCK)
            q2 = q_ref[pl.ds(r0, Q_BLOCK), :].astype(F32)
            kb = k_ref[pl.ds(r0, K_BLOCK), :]
            vb = v_ref[pl.ds(r0, K_BLOCK), :]
            valid = key >= LEFT - j * Q_BLOCK if padded else None
            scores = [_block_scores(jnp.where(masks[a], q2, 0.0).astype(BF16), kb, b_ref[a], valid) for a in range(2)]
            probs = [_softmax_rows(s).astype(BF16) for s in scores]
            outs = [jnp.dot(p, vb, preferred_element_type=F32) for p in probs]
            o_ref[pl.ds(r0, Q_BLOCK), :] = jnp.where(masks[0], outs[0], outs[1]).astype(BF16)
            return carry

        _padded_then_plain(step, n_blocks)

    return _call(
        body, name, (n_pairs,),
        [pl.BlockSpec((t, HEAD_PAIR), lambda p: (0, p)),
         pl.BlockSpec((LEFT + t, HEAD_PAIR), lambda p: (0, p)),
         pl.BlockSpec((LEFT + t, HEAD_PAIR), lambda p: (0, n_pairs + p)),
         pl.BlockSpec((2, Q_BLOCK, K_BLOCK), lambda p: (p, 0, 0))],
        pl.BlockSpec((t, HEAD_PAIR), lambda p: (0, p)),
        jax.ShapeDtypeStruct((t, d), BF16),
        vmem_bytes=8 * _nbytes((LEFT + t, HEAD_PAIR), BF16) + 12 * _nbytes((2, Q_BLOCK, K_BLOCK), F32),
    )(q, kvpad, kvpad, bias)


def _attn_bwd(q, kvpad, bias, do, dk_in, dv_in, name):
    t, d = q.shape
    n_pairs = d // HEAD_PAIR
    n_blocks = t // Q_BLOCK
    has_in = dk_in is not None

    def body(*refs):
        refs = list(refs)
        q_ref, k_ref, v_ref, b_ref, do_ref = refs[:5]
        refs = refs[5:]
        if has_in:
            dki_ref, dvi_ref = refs[:2]
            refs = refs[2:]
        dq_ref, dk_ref, dv_ref, db_ref = refs
        masks = _head_masks()
        key = lax.broadcasted_iota(jnp.int32, (Q_BLOCK, K_BLOCK), 1)
        if has_in:
            dk_ref[...] = dki_ref[...]
            dv_ref[...] = dvi_ref[...]
        else:
            dk_ref[...] = jnp.zeros_like(dk_ref)
            dv_ref[...] = jnp.zeros_like(dv_ref)
        db_ref[...] = jnp.zeros_like(db_ref)

        def step(j, carry, padded):
            r0 = pl.multiple_of(j * Q_BLOCK, Q_BLOCK)
            q2 = q_ref[pl.ds(r0, Q_BLOCK), :].astype(F32)
            do2 = do_ref[pl.ds(r0, Q_BLOCK), :].astype(F32)
            kb = k_ref[pl.ds(r0, K_BLOCK), :]
            vb = v_ref[pl.ds(r0, K_BLOCK), :]
            valid = key >= LEFT - j * Q_BLOCK if padded else None
            heads = range(2)
            qms = [jnp.where(masks[a], q2, 0.0).astype(BF16) for a in heads]
            doms = [jnp.where(masks[a], do2, 0.0).astype(BF16) for a in heads]
            scores = [_block_scores(qms[a], kb, b_ref[a], valid) for a in heads]
            dps = [lax.dot_general(doms[a], vb, NT, preferred_element_type=F32) for a in heads]
            ps = [_softmax_rows(s) for s in scores]
            dss = [ps[a] * (dps[a] - jnp.sum(dps[a] * ps[a], axis=-1, keepdims=True)) for a in heads]
            for a in heads:
                for pair in range(PAIRS_PER_BLOCK):
                    lo = pair * PAIR_ROWS
                    db_ref[a, :, pl.ds(0, PAIR_BAND)] += dss[a][lo:lo + PAIR_ROWS, lo:lo + PAIR_BAND]
            dsbs = [ds.astype(BF16) for ds in dss]
            pbs = [p.astype(BF16) for p in ps]
            dqs = [jnp.dot(dsbs[a], kb, preferred_element_type=F32) for a in heads]
            dk_acc = sum(lax.dot_general(dsbs[a], qms[a], TN, preferred_element_type=F32) for a in heads)
            dv_acc = sum(lax.dot_general(pbs[a], doms[a], TN, preferred_element_type=F32) for a in heads)
            dq = jnp.where(masks[0], dqs[0], dqs[1]) * ATTN_SCALE
            dq_ref[pl.ds(r0, Q_BLOCK), :] = dq.astype(BF16)
            dk_ref[pl.ds(r0, K_BLOCK), :] += dk_acc
            dv_ref[pl.ds(r0, K_BLOCK), :] += dv_acc
            return carry

        _padded_then_plain(step, n_blocks)

    q_spec = pl.BlockSpec((t, HEAD_PAIR), lambda p: (0, p))
    kv_spec = pl.BlockSpec((LEFT + t, HEAD_PAIR), lambda p: (0, p))
    operands = [q, kvpad, kvpad, bias, do]
    in_specs = [q_spec, kv_spec, pl.BlockSpec((LEFT + t, HEAD_PAIR), lambda p: (0, n_pairs + p)),
                pl.BlockSpec((2, Q_BLOCK, K_BLOCK), lambda p: (p, 0, 0)), q_spec]
    aliases = None
    if has_in:
        operands += [dk_in, dv_in]
        in_specs += [kv_spec, kv_spec]
        aliases = {5: 1, 6: 2}
    return _call(
        body, name, (n_pairs,),
        in_specs,
        [q_spec, kv_spec, kv_spec, pl.BlockSpec((2, PAIR_ROWS, DIAGONALS), lambda p: (p, 0, 0))],
        [jax.ShapeDtypeStruct((t, d), BF16),
         jax.ShapeDtypeStruct((LEFT + t, d), F32),
         jax.ShapeDtypeStruct((LEFT + t, d), F32),
         jax.ShapeDtypeStruct((d // HEAD_DIM, PAIR_ROWS, DIAGONALS), F32)],
        vmem_bytes=10 * _nbytes((LEFT + t, HEAD_PAIR), BF16) + 8 * _nbytes((LEFT + t, HEAD_PAIR), F32)
        + 16 * _nbytes((2, Q_BLOCK, K_BLOCK), F32),
        aliases=aliases,
    )(*operands)


def _loss_head(x, g, target, name):
    t, d = x.shape
    tm = _tile(t, 512)

    def body(x_ref, g_ref, t_ref, dx_ref, loss_ref, dg_ref):
        @pl.when(pl.program_id(0) == 0)
        def _():
            loss_ref[...] = jnp.zeros_like(loss_ref)
            dg_ref[...] = jnp.zeros_like(dg_ref)

        xf = x_ref[...]
        r = lax.rsqrt(jnp.mean(xf * xf, axis=-1, keepdims=True) + EPS)
        xhat = xf * r
        diff = xhat * g_ref[...] - t_ref[...]
        row_loss = jnp.mean(diff * diff, axis=-1, keepdims=True)
        loss_ref[...] += 0.5 * jnp.sum(row_loss, axis=0, keepdims=True)
        dy = diff * (1.0 / d)
        dg_ref[...] += jnp.sum(dy * xhat, axis=0, keepdims=True)
        dxhat = dy * g_ref[...]
        dx_ref[...] = r * (dxhat - xhat * jnp.mean(dxhat * xhat, axis=-1, keepdims=True))

    row = pl.BlockSpec((tm, d), lambda i: (i, 0))
    vec = pl.BlockSpec((1, d), lambda i: (0, 0))
    dx, loss, dg = _call(
        body, name, (t // tm,),
        [row, vec, row],
        [row, pl.BlockSpec((1, 1), lambda i: (0, 0)), vec],
        [jax.ShapeDtypeStruct((t, d), F32), jax.ShapeDtypeStruct((1, 1), F32), jax.ShapeDtypeStruct((1, d), F32)],
        vmem_bytes=10 * _nbytes((tm, d), F32),
    )(x, g.reshape(1, d), target)
    return dx, loss[0, 0], dg.reshape(d)


def _adamw_store(g, w_ref, m_ref, v_ref, g_ref, d_ref, nm_ref, nv_ref):
    c1 = 1.0 / (1.0 - ADAM_B1 ** ADAM_STEP)
    c2 = 1.0 / (1.0 - ADAM_B2 ** ADAM_STEP)
    nm = ADAM_B1 * m_ref[...] + (1.0 - ADAM_B1) * g
    nv = ADAM_B2 * v_ref[...] + (1.0 - ADAM_B2) * (g * g)
    g_ref[...] = g
    nm_ref[...] = nm
    nv_ref[...] = nv
    d_ref[...] = -ADAM_LR * ((nm * c1) / (jnp.sqrt(nv * c2) + ADAM_EPS) + ADAM_WD * w_ref[...])


def _adamw_layer(recv, own, w, m, v, layer, prev, me, name):
    n_src, r, c = recv.shape
    tr = _row_tile(r, max(BF16_SUBLANES, ADAMW_BLOCK_ELEMS // c), BF16_SUBLANES)
    first = prev is None

    def body(me_ref, recv_ref, own_ref, w_ref, m_ref, v_ref, *rest):
        mine = me_ref[0]
        own_part = own_ref[...].astype(F32)
        g = None
        for s in range(n_src):
            part = jnp.where(mine == s, own_part, recv_ref[s].astype(F32))
            g = part if g is None else g + part
        _adamw_store(g, w_ref, m_ref, v_ref, *rest[-4:])

    blk = pl.BlockSpec((None, tr, c), lambda i, me_ref: (layer, i, 0))
    any_spec = pl.BlockSpec(memory_space=pl.ANY)
    out = jax.ShapeDtypeStruct(w.shape, F32)
    operands = [me, recv, own, w, m, v] + ([] if first else list(prev))
    vmem = 2 * _nbytes((n_src + 1, tr, c), BF16) + 18 * _nbytes((tr, c), F32)
    return pl.pallas_call(
        body,
        name=name,
        grid_spec=pltpu.PrefetchScalarGridSpec(
            num_scalar_prefetch=1,
            grid=(r // tr,),
            in_specs=[pl.BlockSpec((n_src, tr, c), lambda i, me_ref: (0, i, 0)),
                      pl.BlockSpec((None, tr, c), lambda i, me_ref: (me_ref[0], i, 0)),
                      blk, blk, blk] + ([] if first else [any_spec] * 4),
            out_specs=[blk, blk, blk, blk],
        ),
        out_shape=[out, out, out, out],
        input_output_aliases={} if first else {6 + j: j for j in range(4)},
        compiler_params=pltpu.CompilerParams(
            dimension_semantics=("arbitrary",),
            vmem_limit_bytes=int(min(max(VMEM_FLOOR_BYTES, vmem * 5 // 4), VMEM_CEIL_BYTES))),
    )(*operands)


def _adamw(parts, w, m, v, name):
    n_layers, n_src, r, c = parts.shape
    mult = BF16_SUBLANES if parts.dtype == BF16 else F32_SUBLANES
    tr = _row_tile(r, max(mult, ADAMW_BLOCK_ELEMS // c), mult)

    def body(p_ref, w_ref, m_ref, v_ref, g_ref, d_ref, nm_ref, nv_ref):
        g = p_ref[0].astype(F32)
        for s in range(1, n_src):
            g = g + p_ref[s].astype(F32)
        _adamw_store(g, w_ref, m_ref, v_ref, g_ref, d_ref, nm_ref, nv_ref)

    blk = pl.BlockSpec((None, tr, c), lambda l, i: (l, i, 0))
    out = jax.ShapeDtypeStruct((n_layers, r, c), F32)
    return _call(
        body, name, (n_layers, r // tr),
        [pl.BlockSpec((None, n_src, tr, c), lambda l, i: (l, 0, i, 0)), blk, blk, blk],
        [blk, blk, blk, blk],
        [out, out, out, out],
        vmem_bytes=2 * _nbytes((n_src, tr, c), parts.dtype) + 18 * _nbytes((tr, c), F32),
    )(parts, w, m, v)


def _ordered_sum(parts, name):
    n_src, r, c = parts.shape

    def body(p_ref, o_ref):
        acc = p_ref[0]
        for s in range(1, n_src):
            acc = acc + p_ref[s]
        o_ref[...] = acc

    return _call(
        body, name, (1,),
        [pl.BlockSpec((n_src, r, c), lambda i: (0, 0, 0))],
        pl.BlockSpec((r, c), lambda i: (0, 0)),
        jax.ShapeDtypeStruct((r, c), F32),
        vmem_bytes=4 * _nbytes((n_src, r, c), F32),
    )(parts)


def _position():
    return lax.axis_index("x"), lax.axis_index("y"), lax.axis_index("c")


def _linear(p):
    return 4 * p[0] + 2 * p[1] + p[2]


def _all_gather(shards, name):
    n = len(shards)

    def body(*refs):
        ins, outs = refs[:n], refs[n:2 * n]
        send_sems, recv_sems, local_sems = refs[2 * n:]
        x, y, c = _position()
        me, sibling = (x, y, c), (x, y, 1 - c)
        chips = [(1 - x, y), (x, 1 - y), (1 - x, 1 - y)]

        def slab(t, p):
            return outs[t].at[:, _linear(p)]

        def copy(t, k, block, to, src=None):
            return pltpu.make_async_remote_copy(
                src_ref=slab(t, block) if src is None else src,
                dst_ref=slab(t, block),
                send_sem=send_sems.at[t, k],
                recv_sem=recv_sems.at[t, k],
                device_id=to,
                device_id_type=MESH,
            )

        started = []
        for t in range(n):
            mine = pltpu.make_async_copy(ins[t], slab(t, me), local_sems.at[t])
            mine.start()
            started.append(mine)
        sends = []
        for t in range(n):
            first = [copy(t, 0, me, sibling, src=ins[t])]
            first += [copy(t, 1 + j, me, (*chip, c), src=ins[t]) for j, chip in enumerate(chips)]
            for cp in first:
                cp.start()
            sends += first
        for t in range(n):
            for j, chip in enumerate(chips):
                copy(t, 1 + j, (*chip, c), me).wait_recv()
                passed = copy(t, 4 + j, (*chip, c), sibling)
                passed.start()
                sends.append(passed)
        for t in range(n):
            copy(t, 0, sibling, me).wait_recv()
            for j, chip in enumerate(chips):
                copy(t, 4 + j, (*chip, 1 - c), me).wait_recv()
        for cp in sends:
            cp.wait_send()
        for mine in started:
            mine.wait()

    out_shape = [jax.ShapeDtypeStruct((s.shape[0], N_DEV) + s.shape[1:], s.dtype) for s in shards]
    return pl.pallas_call(
        body,
        name=name,
        in_specs=[HBM_SPEC] * n,
        out_specs=[HBM_SPEC] * n,
        out_shape=out_shape,
        scratch_shapes=[
            pltpu.SemaphoreType.DMA((n, N_DEV - 1)),
            pltpu.SemaphoreType.DMA((n, N_DEV - 1)),
            pltpu.SemaphoreType.DMA((n,)),
        ],
    )(*shards)


def _exchange(blocks, name):
    n = len(blocks)

    def body(*refs):
        ins, outs = refs[:n], refs[n:2 * n]
        send_sems, recv_sems, local_sems = refs[2 * n:]
        x, y, c = _position()
        me = _linear((x, y, c))
        flips = [(fx, fy, fc) for fx in (0, 1) for fy in (0, 1) for fc in (0, 1)][1:]

        def peer_of(flip):
            fx, fy, fc = flip
            return (1 - x if fx else x, 1 - y if fy else y, 1 - c if fc else c)

        def copy(t, k, peer):
            return pltpu.make_async_remote_copy(
                src_ref=ins[t].at[:, _linear(peer)],
                dst_ref=outs[t].at[:, me],
                send_sem=send_sems.at[t, k],
                recv_sem=recv_sems.at[t, k],
                device_id=peer,
                device_id_type=MESH,
            )

        def arrival(t, k, peer):
            return pltpu.make_async_remote_copy(
                src_ref=ins[t].at[:, _linear(peer)],
                dst_ref=outs[t].at[:, _linear(peer)],
                send_sem=send_sems.at[t, k],
                recv_sem=recv_sems.at[t, k],
                device_id=peer,
                device_id_type=MESH,
            )

        own = []
        for t in range(n):
            cp = pltpu.make_async_copy(ins[t].at[:, me], outs[t].at[:, me], local_sems.at[t])
            cp.start()
            own.append(cp)
        sends = []
        for t in range(n):
            for k, flip in enumerate(flips):
                cp = copy(t, k, peer_of(flip))
                cp.start()
                sends.append(cp)
        for t in range(n):
            for k, flip in enumerate(flips):
                arrival(t, k, peer_of(flip)).wait_recv()
        for cp in sends:
            cp.wait_send()
        for cp in own:
            cp.wait()

    out_shape = [jax.ShapeDtypeStruct(b.shape, b.dtype) for b in blocks]
    return pl.pallas_call(
        body,
        name=name,
        in_specs=[HBM_SPEC] * n,
        out_specs=[HBM_SPEC] * n,
        out_shape=out_shape,
        scratch_shapes=[
            pltpu.SemaphoreType.DMA((n, N_DEV - 1)),
            pltpu.SemaphoreType.DMA((n, N_DEV - 1)),
            pltpu.SemaphoreType.DMA((n,)),
        ],
    )(*blocks)


def _peers():
    x, y, c = _position()
    flips = [(fx, fy, fc) for fx in (0, 1) for fy in (0, 1) for fc in (0, 1)][1:]
    return [(1 - x if fx else x, 1 - y if fy else y, 1 - c if fc else c) for fx, fy, fc in flips]


SIBLING, OTHER_CHIPS = (0,), (1, 3, 5)
COPY_PEERS = {"gather": tuple(range(N_DEV - 1)), "exchange": tuple(range(N_DEV - 1)),
              "chips": SIBLING + OTHER_CHIPS, "forward": OTHER_CHIPS}


def _split_copy(kind, src_ref, land_ref, k, send_sem, recv_sem, starting):
    peers = _peers()
    peer = peers[SIBLING[0]] if kind == "forward" else peers[k]
    me = _linear(_position())
    if kind == "forward":
        slab = _linear(peers[k]) if starting else 0
        src, dst = land_ref.at[slab], land_ref.at[slab]
    elif kind == "exchange":
        src, dst = src_ref.at[_linear(peer) if starting else 0], land_ref.at[me if starting else 0]
    else:
        src, dst = src_ref, land_ref.at[me if starting else 0]
    return pltpu.make_async_remote_copy(src_ref=src, dst_ref=dst, send_sem=send_sem, recv_sem=recv_sem,
                                        device_id=peer, device_id_type=MESH)


def _split_start(groups, carry, name):
    arrays = [a for _, srcs, lands in groups for a in list(srcs) + list(lands)] + [carry]

    def body(*refs):
        ins, sems = refs[:len(arrays)], refs[len(arrays):len(arrays) + 2 * len(groups)]
        at = 0
        for g, (kind, srcs, lands) in enumerate(groups):
            src_refs, land_refs = ins[at:at + len(srcs)], ins[at + len(srcs):at + len(srcs) + len(lands)]
            at += len(srcs) + len(lands)
            peers = COPY_PEERS[kind]
            for t in range(len(lands)):
                for slot, k in enumerate(peers):
                    sem = t * len(peers) + slot
                    _split_copy(kind, src_refs[t] if srcs else None, land_refs[t], k,
                                sems[2 * g].at[sem], sems[2 * g + 1].at[sem], True).start()

    sem_shapes = [pltpu.SemaphoreType.DMA((len(lands) * len(COPY_PEERS[kind]),))
                  for kind, _, lands in groups for _ in range(2)]
    out = pl.pallas_call(
        body,
        name=name,
        in_specs=[HBM_SPEC] * len(arrays),
        out_specs=[SEM_SPEC] * len(sem_shapes) + [HBM_SPEC] * len(arrays),
        out_shape=sem_shapes + [pltpu.HBM(a.shape, a.dtype) for a in arrays],
        input_output_aliases={i: len(sem_shapes) + i for i in range(len(arrays))},
        compiler_params=pltpu.CompilerParams(has_side_effects=pltpu.SideEffectType.DATAFLOW_SIDE_EFFECTING),
    )(*[pltpu.with_memory_space_constraint(a, pltpu.HBM) for a in arrays])
    sems, thru = out[:len(sem_shapes)], out[len(sem_shapes):]
    started, at = [], 0
    for g, (kind, srcs, lands) in enumerate(groups):
        n_s, n_l = len(srcs), len(lands)
        started.append((kind, sems[2 * g], sems[2 * g + 1], thru[at:at + n_s], thru[at + n_s:at + n_s + n_l]))
        at += n_s + n_l
    return started, thru[-1]


def _split_wait(started, after, name):
    kind, send_sems, recv_sems, srcs, lands = started
    n_s, n_l = len(srcs), len(lands)
    peers = COPY_PEERS[kind]

    def body(*refs):
        src_refs, land_refs = refs[:n_s], refs[n_s:n_s + n_l]
        send_ref, recv_ref = refs[n_s + n_l], refs[n_s + n_l + 1]
        for t in range(n_l):
            for slot, k in enumerate(peers):
                sem = t * len(peers) + slot
                copy = _split_copy(kind, src_refs[t] if n_s else None, land_refs[t], k,
                                   send_ref.at[sem], recv_ref.at[sem], False)
                copy.wait_send()
                copy.wait_recv()

    arrays = list(srcs) + list(lands)
    out = pl.pallas_call(
        body,
        name=name,
        in_specs=[HBM_SPEC] * len(arrays) + [SEM_SPEC, SEM_SPEC, pl.BlockSpec(memory_space=pl.ANY)],
        out_specs=[HBM_SPEC] * len(arrays),
        out_shape=[pltpu.HBM(a.shape, a.dtype) for a in arrays],
        input_output_aliases={i: i for i in range(len(arrays))},
        compiler_params=pltpu.CompilerParams(has_side_effects=pltpu.SideEffectType.DATAFLOW_SIDE_EFFECTING),
    )(*arrays, send_sems, recv_sems, after)
    return out[:n_s], out[n_s:]


def _pack(arrays, row_multiple):
    flat = jnp.concatenate([a.reshape(-1) for a in arrays])
    quantum = row_multiple * FLAT_LANES
    padded = -(-flat.shape[0] // quantum) * quantum
    return jnp.pad(flat, (0, padded - flat.shape[0])).reshape(-1, FLAT_LANES)


def _unpack(flat, like):
    flat = flat.reshape(-1)
    out, at = [], 0
    for a in like:
        size = math.prod(a.shape)
        out.append(flat[at:at + size].reshape(a.shape))
        at += size
    return out


def kernel(x, a_norm, a_w_in, a_sgu_norm, a_w_spatial, a_b_spatial, a_w_out, kv_norm, w_kv, b_norm, b_w_q, b_rel_bias, b_w_o, ffn_norm, ffn_w_gate_up, ffn_w_down, final_norm, loss_target, m_a_norm, m_a_w_in, m_a_sgu_norm, m_a_w_spatial, m_a_b_spatial, m_a_w_out, m_kv_norm, m_w_kv, m_b_norm, m_b_w_q, m_b_rel_bias, m_b_w_o, m_ffn_norm, m_ffn_w_gate_up, m_ffn_w_down, m_final_norm, v_a_norm, v_a_w_in, v_a_sgu_norm, v_a_w_spatial, v_a_b_spatial, v_a_w_out, v_kv_norm, v_w_kv, v_b_norm, v_b_w_q, v_b_rel_bias, v_b_w_o, v_ffn_norm, v_ffn_w_gate_up, v_ffn_w_down, v_final_norm):
    xs = x[0]
    target = loss_target[0]
    t, d = xs.shape
    n_a = a_w_in.shape[0]
    n_b = b_w_q.shape[0]
    depth = ffn_w_gate_up.shape[0]
    f_a = a_w_out.shape[1] * N_DEV
    gd = f_a // A_GROUPS
    nb_ffn = ffn_w_gate_up.shape[2]
    me = _linear(_position())

    small_rows = -(-(a_norm.size + a_sgu_norm.size) // (8 * 128)) * 8
    small = jnp.pad(jnp.concatenate([a_norm.reshape(-1), a_sgu_norm.reshape(-1)]),
                    (0, small_rows * 128 - a_norm.size - a_sgu_norm.size)).reshape(1, small_rows, 128)

    def shard(w, layer=None):
        return (w if layer is None else w[layer]).astype(BF16)

    stages = []
    for layer in range(depth):
        if layer == 0:
            stages += [("a0", [shard(a_w_in, 0)]), ("a0_out", [shard(a_w_out, 0)])]
        elif layer < n_a:
            stages.append((f"a{layer}", [shard(a_w_in, layer), shard(a_w_out, layer)]))
        else:
            i = layer - n_a
            shared = [shard(w_kv)] if i == 0 else []
            stages.append((f"b{i}", shared + [shard(b_w_q, i), shard(b_w_o, i)]))
        stages.append((f"f{layer}", [shard(ffn_w_gate_up, layer), shard(ffn_w_down, layer)]))
    first = _all_gather([s[None] for s in stages[0][1]] + [small], "gather_first")
    gathered = {stages[0][0]: [g[0] for g in first[:-1]]}
    small_g = first[-1].reshape(N_DEV, -1)
    a_norm_full = small_g[:, :a_norm.size].reshape(N_DEV, n_a, -1).transpose(1, 0, 2).reshape(n_a, d)
    a_sgu_full = small_g[:, a_norm.size:a_norm.size + a_sgu_norm.size].reshape(
        N_DEV, n_a, -1).transpose(1, 0, 2).reshape(n_a, f_a)
    two_level = ("f0", "a1", "f1")
    later = [("chips" if key in two_level else "gather", shards,
              [lax.dynamic_update_slice(lax.empty((N_DEV,) + s.shape, BF16), s[None], (me, 0, 0)) for s in shards])
             for key, shards in stages[1:]]
    started, a_norm_full = _split_start(later, a_norm_full, "gather_start")
    in_flight = {key: group for (key, _), group in zip(stages[1:], started)}

    def pass_on(key, carry):
        if key in two_level and key in in_flight and in_flight[key][0] == "chips":
            _, lands = _split_wait(in_flight.pop(key), carry, f"gather_wait_{key}_chips")
            (in_flight[key],), carry = _split_start([("forward", [], lands)], carry, f"gather_pass_on_{key}")
        return carry

    def weights(key, after):
        if key not in gathered:
            _, gathered[key] = _split_wait(in_flight.pop(key), after, f"gather_wait_{key}")
        return gathered[key]

    rows_down = ffn_w_down.shape[1]

    def mixer_a_weights(i, after):
        if i == 0:
            (w_in,), (w_out,) = weights("a0", after[0]), weights("a0_out", after[1])
        else:
            w_in, w_out = weights(f"a{i}", after[0])
        return w_in[None], w_out.reshape(1, f_a, d)

    def mixer_b_weights(i, after):
        ws = weights(f"b{i}", after)
        return ws[-2].reshape(1, d, d), ws[-1].reshape(1, d, d)

    def ffn_weights(layer, after):
        w_gu, w_dn = weights(f"f{layer}", after)
        return w_gu[None], w_dn.reshape(1, N_DEV // 2, 2 * rows_down, d)

    w_sp_t = jnp.swapaxes(a_w_spatial, -1, -2)
    b_full = jnp.repeat(jnp.swapaxes(a_b_spatial, -1, -2), gd, axis=-1)

    saved = []

    def ffn_fwd(xin, layer):
        hf = _rms_fwd(xin, ffn_norm[layer], f"ffn_norm_fwd_{layer}")
        w_gu, w_dn = ffn_weights(layer, xin)
        dact, act = _ffn_gate_up(f"ffn_gate_up_{layer}", hf, w_gu, 0)
        act = pass_on(f"a{layer + 1}", act)
        xout = _mm_down(f"ffn_down_{layer}", act, w_dn, 0, xin)
        return xout, (xin, hf, dact, act)

    for i in range(n_a):
        h = _rms_fwd(xs, a_norm_full[i], f"a_norm_fwd_{i}")
        zpre = _mm_colblock(f"a_in_{i}", h, weights(f"a{i}", xs)[0][None], 0)
        p, zs, dgs = _sgu_fwd(zpre, a_sgu_full[i], a_w_spatial[i], b_full[i], f"a_sgu_fwd_{i}")
        p = pass_on(f"f{i}", p)
        w_in, w_out = mixer_a_weights(i, (xs, p))
        x_mid = _mm_natural(f"a_out_{i}", p, w_out, 0, res=xs)
        x_out, ffn_saved = ffn_fwd(x_mid, i)
        saved.append((xs, h, zs, dgs, p, ffn_saved))
        xs = x_out

    x_kv = xs
    w_kv_g = weights("b0", x_kv)[0][None]
    h_kv = _rms_fwd(x_kv, kv_norm, "kv_norm_fwd")
    kv = _mm_colblock("kv_proj", h_kv, w_kv_g, 0)
    kvpad = jnp.pad(kv, ((LEFT, 0), (0, 0)))

    biases = [_bias_block(_bias_build(b_rel_bias[i], f"rel_bias_{i}")) for i in range(n_b)]
    for i in range(n_b):
        layer = n_a + i
        w_q, w_o = mixer_b_weights(i, xs)
        hb = _rms_fwd(xs, b_norm[i], f"b_norm_fwd_{i}")
        q = _mm_natural(f"b_q_{i}", hb, w_q, 0, out_dtype=BF16, scale=ATTN_SCALE)
        o = _attn_fwd(q, kvpad, biases[i], f"b_attn_fwd_{i}")
        x_mid = _mm_natural(f"b_o_{i}", o, w_o, 0, res=xs)
        x_out, ffn_saved = ffn_fwd(x_mid, layer)
        saved.append((xs, hb, q, o, ffn_saved))
        xs = x_out

    dx, loss_local, g_final = _loss_head(xs, final_norm, target, "loss_head")
    loss = lax.psum(loss_local, ("x", "y", "c"))

    big_grads = {}
    pending = []
    in_flight_grads = []

    def start_exchange(dx, tag):
        srcs = [big_grads[key] for key in pending]
        lands = [lax.empty(s.shape, BF16) for s in srcs]
        (group,), dx = _split_start([("exchange", srcs, lands)], dx, f"exchange_start_{tag}")
        in_flight_grads.append((list(pending), group, tag))
        pending.clear()
        return dx

    g_ffn_norm = [None] * depth
    g_a_norm = [None] * n_a
    g_a_sgu = [None] * n_a
    g_w_sp = [None] * n_a
    g_b_sp = [None] * n_a
    g_b_norm = [None] * n_b
    g_rel = [None] * n_b

    def ffn_bwd(dx, layer, ffn_saved):
        eager = layer < n_a
        xin, hf, dact, act = ffn_saved
        big_grads["ffn_w_down", layer] = _mm_dw_down(f"ffn_down_dw_{layer}", act, dx)
        pending.append(("ffn_w_down", layer))
        if eager:
            dx = start_exchange(dx, f"f{layer}_down")
        w_gu, w_dn = ffn_weights(layer, xin)
        dgu = _ffn_down_dx(f"ffn_down_dx_{layer}", dx, w_dn, 0, dact).reshape(N_DEV, t, nb_ffn)
        big_grads["ffn_w_gate_up", layer] = _mm_dw_colblock(
            f"ffn_gate_up_dw_{layer}", hf, dgu, blocked_in=True, transposed=True)
        pending.append(("ffn_w_gate_up", layer))
        if eager:
            dx = start_exchange(dx, f"f{layer}_gate_up")
        dx, g_ffn_norm[layer] = _mm_t_colblock_norm_bwd(
            f"ffn_gate_up_dx_{layer}", dgu, w_gu, 0, xin, ffn_norm[layer], dx, blocked_in=True)
        return dx

    dk = dv = None
    for i in reversed(range(n_b)):
        layer = n_a + i
        x_in, hb, q, o, ffn_saved = saved[layer]
        dx = ffn_bwd(dx, layer, ffn_saved)
        big_grads["b_w_o", i] = _mm_dw_natural(f"b_o_dw_{i}", o, dx)
        w_q, w_o = mixer_b_weights(i, x_in)
        do = _mm_t_natural(f"b_o_dx_{i}", dx, w_o, 0)
        dq, dk, dv, dbias = _attn_bwd(q, kvpad, biases[i], do, dk, dv, f"b_attn_bwd_{i}")
        g_rel[i] = _bias_grad(dbias, f"rel_bias_grad_{i}")
        big_grads["b_w_q", i] = _mm_dw_natural(f"b_q_dw_{i}", hb, dq)
        pending.extend([("b_w_o", i), ("b_w_q", i)])
        dx, g_b_norm[i] = _mm_t_natural_norm_bwd(f"b_q_dx_{i}", dq, w_q, 0, x_in, b_norm[i], dx)
        if i > 0:
            dx = start_exchange(dx, f"b{i}")

    dkv = jnp.concatenate([dk[LEFT:], dv[LEFT:]], axis=1).astype(BF16)
    big_grads["w_kv", 0] = _mm_dw_colblock("kv_proj_dw", h_kv, dkv)
    pending.append(("w_kv", 0))
    dx, g_kv_norm = _mm_t_colblock_norm_bwd("kv_proj_dx", dkv, w_kv_g, 0, x_kv, kv_norm, dx)
    dx = start_exchange(dx, "kv")

    for i in reversed(range(n_a)):
        x_in, h, zs, dgs, p, ffn_saved = saved[i]
        dx = ffn_bwd(dx, i, ffn_saved)
        big_grads["a_w_out", i] = _mm_dw_natural(f"a_out_dw_{i}", p, dx)
        pending.append(("a_w_out", i))
        dx = start_exchange(dx, f"a{i}_out")
        w_in, w_out = mixer_a_weights(i, (x_in, p))
        dp = _mm_t_natural(f"a_out_dx_{i}", dx, w_out, 0)
        dz, g_w_sp[i], g_b_sp[i], g_a_sgu[i] = _sgu_bwd(
            zs, dgs, dp, a_sgu_full[i], a_w_spatial[i], w_sp_t[i], b_full[i], f"a_sgu_bwd_{i}")
        big_grads["a_w_in", i] = _mm_dw_colblock(f"a_in_dw_{i}", h, dz)
        pending.append(("a_w_in", i))
        dx = start_exchange(dx, f"a{i}_in")
        dx, g_a_norm[i] = _mm_t_colblock_norm_bwd(f"a_in_dx_{i}", dz, w_in, 0, x_in, a_norm_full[i], dx)
    grad_x = dx[None]

    small_like = [jax.ShapeDtypeStruct((n_a, d), F32), jax.ShapeDtypeStruct((n_a, f_a), F32),
                  a_w_spatial, a_b_spatial, kv_norm, b_norm, b_rel_bias, ffn_norm, final_norm]
    small_partial = _pack(
        [jnp.stack(g_a_norm), jnp.stack(g_a_sgu), jnp.stack(g_w_sp), jnp.stack(g_b_sp), g_kv_norm,
         jnp.stack(g_b_norm), jnp.stack(g_rel), jnp.stack(g_ffn_norm), g_final], N_DEV * 8)
    chunk_rows = small_partial.shape[0] // N_DEV
    arrived = {}
    for keys, group, tag in in_flight_grads:
        srcs, lands = _split_wait(group, dx, f"exchange_wait_{tag}")
        for key, src, land in zip(keys, srcs, lands):
            arrived[key] = (land, src)
    small_got = _exchange([small_partial.reshape(1, N_DEV, chunk_rows, FLAT_LANES)], "exchange_small")[0]
    small_sum = _ordered_sum(small_got[0], "small_grad_sum")
    small_all = _all_gather([small_sum[None]], "gather_small_grads")[0]
    (ga_norm, ga_sgu, gw_sp, gb_sp, gkv_norm, gb_norm, g_relb, gffn_norm, gfinal) = _unpack(small_all, small_like)

    results = {}
    big_names = ["a_w_in", "a_w_out", "w_kv", "b_w_q", "b_w_o", "ffn_w_gate_up", "ffn_w_down"]
    big_wmv = [(a_w_in, m_a_w_in, v_a_w_in), (a_w_out, m_a_w_out, v_a_w_out),
               (w_kv[None], m_w_kv[None], v_w_kv[None]), (b_w_q, m_b_w_q, v_b_w_q), (b_w_o, m_b_w_o, v_b_w_o),
               tuple(jnp.swapaxes(a, 1, 2) for a in (ffn_w_gate_up, m_ffn_w_gate_up, v_ffn_w_gate_up)),
               (ffn_w_down, m_ffn_w_down, v_ffn_w_down)]
    me_arr = jnp.reshape(me, (1,)).astype(jnp.int32)
    for name, (w, m, v) in zip(big_names, big_wmv):
        outs = None
        for layer in range(w.shape[0]):
            got, own = arrived[name, layer]
            outs = _adamw_layer(got, own, w, m, v, layer, outs, me_arr, f"adamw_{name}_{layer}")
        if name == "w_kv":
            outs = [o[0] for o in outs]
        if name == "ffn_w_gate_up":
            outs = [jnp.swapaxes(o, 1, 2) for o in outs]
        results[name] = outs

    n_cols = a_norm.shape[1]
    s_cols = a_sgu_norm.shape[1]
    small_g_list = [lax.dynamic_slice(ga_norm, (0, me * n_cols), (n_a, n_cols)),
                    lax.dynamic_slice(ga_sgu, (0, me * s_cols), (n_a, s_cols)),
                    gw_sp, gb_sp, gkv_norm, gb_norm, g_relb, gffn_norm, gfinal]
    small_names = ["a_norm", "a_sgu_norm", "a_w_spatial", "a_b_spatial", "kv_norm", "b_norm", "b_rel_bias",
                   "ffn_norm", "final_norm"]
    small_w = [a_norm, a_sgu_norm, a_w_spatial, a_b_spatial, kv_norm, b_norm, b_rel_bias, ffn_norm, final_norm]
    small_m = [m_a_norm, m_a_sgu_norm, m_a_w_spatial, m_a_b_spatial, m_kv_norm, m_b_norm, m_b_rel_bias,
               m_ffn_norm, m_final_norm]
    small_v = [v_a_norm, v_a_sgu_norm, v_a_w_spatial, v_a_b_spatial, v_kv_norm, v_b_norm, v_b_rel_bias,
               v_ffn_norm, v_final_norm]
    flat_g = _pack(small_g_list, 8)
    flat_out = _adamw(flat_g[None, None], _pack(small_w, 8)[None], _pack(small_m, 8)[None],
                      _pack(small_v, 8)[None], "adamw_small")
    unpacked = [_unpack(o[0], small_w) for o in flat_out]
    for idx, name in enumerate(small_names):
        results[name] = [unpacked[kind][idx] for kind in range(4)]

    order = ["a_norm", "a_w_in", "a_sgu_norm", "a_w_spatial", "a_b_spatial", "a_w_out", "kv_norm", "w_kv",
             "b_norm", "b_w_q", "b_rel_bias", "b_w_o", "ffn_norm", "ffn_w_gate_up", "ffn_w_down", "final_norm"]
    outputs = [loss, grad_x]
    for kind in range(4):
        outputs += [results[name][kind] for name in order]
    return tuple(outputs)
```

```python
import math

import jax
import jax.numpy as jnp
from jax import lax
from jax.experimental import pallas as pl
from jax.experimental.pallas import tpu as pltpu

F32 = jnp.float32
BF16 = jnp.bfloat16
MESH = pl.DeviceIdType.MESH
HBM_SPEC = pl.BlockSpec(memory_space=pltpu.HBM)
SEM_SPEC = pl.BlockSpec(memory_space=pltpu.SEMAPHORE)

N_DEV = 8
CHUNK = 64
A_CHUNK = 128
A_GROUPS = 8
N_LEFT_CHUNKS = 8
LEFT = N_LEFT_CHUNKS * CHUNK
PAIR_ROWS = 2 * CHUNK
PAIR_BAND = PAIR_ROWS + LEFT
DIAGONALS = PAIR_BAND + PAIR_ROWS
PAIRS_PER_BLOCK = 2
Q_BLOCK = PAIRS_PER_BLOCK * PAIR_ROWS
K_BLOCK = Q_BLOCK + LEFT
ATTN_UNROLL = 7
MAX_REL = 256
N_REL = 2 * MAX_REL + 1
REL_PAD = 640
HEAD_DIM = 64
HEAD_PAIR = 2 * HEAD_DIM
ATTN_SCALE = HEAD_DIM ** -0.5
EPS = 1e-6
NEG_INF = -1e30
ADAM_LR = 0.001
ADAM_B1 = 0.9
ADAM_B2 = 0.999
ADAM_EPS = 1e-08
ADAM_WD = 0.01
ADAM_STEP = 10
FLAT_LANES = 1024
F32_SUBLANES = 8
BF16_SUBLANES = 16
ADAMW_BLOCK_ELEMS = 256 * 1024
V7X_VMEM_BYTES = 64 * 1024 * 1024
VMEM_FLOOR_BYTES = 32 * 1024 * 1024
VMEM_CEIL_BYTES = V7X_VMEM_BYTES - 8 * 1024 * 1024

NN = (((1,), (0,)), ((), ()))
NT = (((1,), (1,)), ((), ()))
TN = (((0,), (0,)), ((), ()))


def _tile(n, pref):
    return pref if n % pref == 0 else n


def _row_tile(n, pref, mult):
    best = None
    for t in range(mult, min(n, pref) + 1, mult):
        if n % t == 0:
            best = t
    return best if best is not None else n


def _nbytes(shape, dtype):
    n = 1
    for s in shape:
        if s is not None:
            n *= s
    return n * jnp.dtype(dtype).itemsize


def _call(body, name, grid, in_specs, out_specs, out_shape, scratch=(), vmem_bytes=0, aliases=None):
    limit = int(min(max(VMEM_FLOOR_BYTES, vmem_bytes * 5 // 4), VMEM_CEIL_BYTES))
    return pl.pallas_call(
        body,
        name=name,
        grid=grid,
        in_specs=in_specs,
        out_specs=out_specs,
        out_shape=out_shape,
        scratch_shapes=list(scratch),
        input_output_aliases=aliases or {},
        compiler_params=pltpu.CompilerParams(
            dimension_semantics=("arbitrary",) * len(grid), vmem_limit_bytes=limit),
    )


ERFC_P = 0.3275911 / math.sqrt(2.0)
ERFC_HALF_COEFFS = tuple(0.5 * a for a in (1.061405429, -1.453152027, 1.421413741, -0.284496736, 0.254829592))


def _gelu_and_grad(x):
    d = 1.0 + ERFC_P * jnp.abs(x)
    r = pl.reciprocal(d, approx=True)
    t = r * (2.0 - d * r)
    a5, a4, a3, a2, a1 = ERFC_HALF_COEFFS
    ex = jnp.exp(-0.5 * (x * x))
    tail = ((((a5 * t + a4) * t + a3) * t + a2) * t + a1) * t * ex
    cdf = jnp.where(x < 0, tail, 1.0 - tail)
    return x * cdf, cdf + x * ex * (1.0 / math.sqrt(2.0 * math.pi))


def _sigmoid(x):
    return 1.0 / (1.0 + jnp.exp(-x))


def _split3(x):
    hi = x.astype(BF16)
    r1 = x - hi.astype(F32)
    mid = r1.astype(BF16)
    lo = (r1 - mid.astype(F32)).astype(BF16)
    return hi, mid, lo


def _rms_fwd(x, g, name):
    t, d = x.shape
    tm = _tile(t, 512)

    def body(x_ref, g_ref, o_ref):
        xf = x_ref[...]
        r = lax.rsqrt(jnp.mean(xf * xf, axis=-1, keepdims=True) + EPS)
        o_ref[...] = (xf * r * g_ref[...]).astype(o_ref.dtype)

    return _call(
        body, name, (t // tm,),
        [pl.BlockSpec((tm, d), lambda i: (i, 0)), pl.BlockSpec((1, d), lambda i: (0, 0))],
        pl.BlockSpec((tm, d), lambda i: (i, 0)),
        jax.ShapeDtypeStruct((t, d), BF16),
        vmem_bytes=2 * (_nbytes((tm, d), F32) + _nbytes((tm, d), BF16)) + 4 * _nbytes((tm, d), F32),
    )(x, g.reshape(1, d))


def _mm(name, dims, a, b, *, grid, a_spec, b_spec, out_shape, out_spec, acc_shape,
        res=None, res_spec=None, scale=None):
    nk = grid[2]
    has_res = res is not None

    def body(*refs):
        refs = list(refs)
        a_ref = refs.pop(0)
        b_ref = refs.pop(0)
        r_ref = refs.pop(0) if has_res else None
        o_ref = refs.pop(0)
        part = lax.dot_general(a_ref[...].astype(BF16), b_ref[...].astype(BF16), dims,
                               preferred_element_type=F32)

        def finish(acc):
            if scale is not None:
                acc = acc * scale
            if has_res:
                acc = acc + r_ref[...]
            o_ref[...] = acc.astype(o_ref.dtype)

        if nk == 1:
            finish(part)
        else:
            acc_ref = refs.pop(0)
            k = pl.program_id(2)

            @pl.when(k == 0)
            def _():
                acc_ref[...] = part

            @pl.when(k > 0)
            def _():
                acc_ref[...] += part

            @pl.when(k == nk - 1)
            def _():
                finish(acc_ref[...])

    operands = [a, b]
    in_specs = [a_spec, b_spec]
    vmem = 2 * (_nbytes(a_spec.block_shape, a.dtype) + _nbytes(b_spec.block_shape, b.dtype)
                + _nbytes(out_spec.block_shape, out_shape.dtype))
    vmem += 3 * _nbytes(acc_shape, F32)
    if has_res:
        operands.append(res)
        in_specs.append(res_spec)
        vmem += 2 * _nbytes(res_spec.block_shape, res.dtype)
    scratch = [pltpu.VMEM(acc_shape, F32)] if nk > 1 else []
    return _call(body, name, grid, in_specs, out_spec, out_shape, scratch=scratch, vmem_bytes=vmem)(*operands)


def _mm_colblock(name, h, w_g, layer):
    t, k = h.shape
    nb = w_g.shape[3]
    tm = _tile(t, 2048)
    return _mm(
        name, NN, h, w_g, grid=(t // tm, N_DEV, 1),
        a_spec=pl.BlockSpec((tm, k), lambda i, j, kk: (i, 0)),
        b_spec=pl.BlockSpec((None, None, k, nb), lambda i, j, kk: (layer, j, 0, 0)),
        out_shape=jax.ShapeDtypeStruct((t, N_DEV * nb), BF16),
        out_spec=pl.BlockSpec((tm, nb), lambda i, j, kk: (i, j)), acc_shape=(tm, nb))


def _mm_natural(name, a, w, layer, *, res=None, out_dtype=F32, scale=None):
    t, k = a.shape
    n = w.shape[2]
    tm = _tile(t, 1024)
    tn = _tile(n, 1024 if k <= 1024 else 512)
    res_spec = None if res is None else pl.BlockSpec((tm, tn), lambda i, j, kk: (i, j))
    return _mm(
        name, NN, a, w, grid=(t // tm, n // tn, 1),
        a_spec=pl.BlockSpec((tm, k), lambda i, j, kk: (i, 0)),
        b_spec=pl.BlockSpec((None, k, tn), lambda i, j, kk: (layer, 0, j)),
        out_shape=jax.ShapeDtypeStruct((t, n), out_dtype),
        out_spec=pl.BlockSpec((tm, tn), lambda i, j, kk: (i, j)),
        acc_shape=(tm, tn), res=res, res_spec=res_spec, scale=scale)


def _mm_down(name, act, w4, layer, res):
    nblk, t, kb = act.shape
    n = w4.shape[3]
    tm = _tile(t, 1024)

    def body(a_ref, b_ref, r_ref, o_ref):
        acc = r_ref[...]
        for u in range(nblk):
            acc = acc + jnp.dot(a_ref[u], b_ref[u], preferred_element_type=F32)
        o_ref[...] = acc

    row = pl.BlockSpec((tm, n), lambda i: (i, 0))
    return _call(
        body, name, (t // tm,),
        [pl.BlockSpec((nblk, tm, kb), lambda i: (0, i, 0)),
         pl.BlockSpec((None, nblk, kb, n), lambda i: (layer, 0, 0, 0)),
         row],
        row,
        jax.ShapeDtypeStruct((t, n), F32),
        vmem_bytes=2 * (_nbytes((nblk, tm, kb), BF16) + _nbytes((nblk, kb, n), BF16)) + 6 * _nbytes((tm, n), F32),
    )(act, w4, res)


def _mm_t_colblock_norm_bwd(name, dz, w_g, layer, x, g, dx_up, blocked_in=False):
    k = w_g.shape[2]
    nb = w_g.shape[3]
    t = x.shape[0]
    tm = _tile(t, 1024)
    per_step = 4 if nb * k <= 512 * 1024 else 2
    n_steps = N_DEV // per_step
    if blocked_in:
        a_spec = pl.BlockSpec((per_step, tm, nb), lambda i, kk: (kk, i, 0))
    else:
        a_spec = pl.BlockSpec((tm, per_step * nb), lambda i, kk: (i, kk))

    def body(a_ref, b_ref, x_ref, g_ref, up_ref, dx_ref, dg_ref, acc_ref):
        i = pl.program_id(0)
        kk = pl.program_id(1)
        part = None
        for u in range(per_step):
            a = a_ref[u] if blocked_in else a_ref[:, u * nb:(u + 1) * nb]
            term = lax.dot_general(a.astype(BF16), b_ref[u].astype(BF16), NT, preferred_element_type=F32)
            part = term if part is None else part + term

        @pl.when(kk == 0)
        def _():
            acc_ref[...] = part

        @pl.when(kk > 0)
        def _():
            acc_ref[...] += part

        @pl.when((i == 0) & (kk == 0))
        def _():
            dg_ref[...] = jnp.zeros_like(dg_ref)

        @pl.when(kk == n_steps - 1)
        def _():
            dy = acc_ref[...]
            xf = x_ref[...]
            r = lax.rsqrt(jnp.mean(xf * xf, axis=-1, keepdims=True) + EPS)
            xhat = xf * r
            dxhat = dy * g_ref[...]
            dg_ref[...] += jnp.sum(dy * xhat, axis=0, keepdims=True)
            dx_ref[...] = up_ref[...] + r * (dxhat - xhat * jnp.mean(dxhat * xhat, axis=-1, keepdims=True))

    row = pl.BlockSpec((tm, k), lambda i, kk: (i, 0))
    vec = pl.BlockSpec((1, k), lambda i, kk: (0, 0))
    dx, dg = _call(
        body, name, (t // tm, n_steps),
        [a_spec, pl.BlockSpec((None, per_step, k, nb), lambda i, kk: (layer, kk, 0, 0)), row, vec, row],
        [row, vec],
        [jax.ShapeDtypeStruct((t, k), F32), jax.ShapeDtypeStruct((1, k), F32)],
        scratch=[pltpu.VMEM((tm, k), F32)],
        vmem_bytes=2 * per_step * (_nbytes((tm, nb), BF16) + _nbytes((k, nb), BF16)) + 10 * _nbytes((tm, k), F32),
    )(dz, w_g, x, g.reshape(1, k), dx_up)
    return dx, dg.reshape(k)


def _ffn_gate_up(name, h, w_g, layer):
    t, k = h.shape
    nb = w_g.shape[3]
    half = N_DEV // 2
    tm = _tile(t, 1024)

    def body(h_ref, wg_ref, wu_ref, dact_ref, act_ref):
        hb = h_ref[...]
        gate = jnp.dot(hb, wg_ref[...], preferred_element_type=F32)
        up = jnp.dot(hb, wu_ref[...], preferred_element_type=F32)
        sig = _sigmoid(gate)
        silu = gate * sig
        dact_ref[0] = (up * (sig * (1.0 + gate * (1.0 - sig)))).astype(BF16)
        dact_ref[1] = silu.astype(BF16)
        act_ref[...] = (silu * up).astype(BF16)

    return _call(
        body, name, (t // tm, half),
        [pl.BlockSpec((tm, k), lambda i, j: (i, 0)),
         pl.BlockSpec((None, None, k, nb), lambda i, j: (layer, j, 0, 0)),
         pl.BlockSpec((None, None, k, nb), lambda i, j: (layer, half + j, 0, 0))],
        [pl.BlockSpec((2, None, tm, nb), lambda i, j: (0, j, i, 0)),
         pl.BlockSpec((None, tm, nb), lambda i, j: (j, i, 0))],
        [jax.ShapeDtypeStruct((2, half, t, nb), BF16), jax.ShapeDtypeStruct((half, t, nb), BF16)],
        vmem_bytes=2 * (_nbytes((tm, k), BF16) + 2 * _nbytes((k, nb), BF16) + 3 * _nbytes((tm, nb), BF16))
        + 8 * _nbytes((tm, nb), F32),
    )(h, w_g, w_g)


def _ffn_down_dx(name, dy, w4, layer, dact):
    t, n = dy.shape
    nblk, kb = w4.shape[1], w4.shape[2]
    tm = _tile(t, 1024)

    def body(dy_ref, w_ref, dact_ref, dgu_ref):
        da = lax.dot_general(dy_ref[...].astype(BF16), w_ref[...], NT, preferred_element_type=F32)
        dgu_ref[0] = (da * dact_ref[0].astype(F32)).astype(BF16)
        dgu_ref[1] = (da * dact_ref[1].astype(F32)).astype(BF16)

    blk = pl.BlockSpec((2, None, tm, kb), lambda i, j: (0, j, i, 0))
    return _call(
        body, name, (t // tm, nblk),
        [pl.BlockSpec((tm, n), lambda i, j: (i, 0)),
         pl.BlockSpec((None, None, kb, n), lambda i, j: (layer, j, 0, 0)),
         blk],
        blk,
        jax.ShapeDtypeStruct((2, nblk, t, kb), BF16),
        vmem_bytes=2 * (_nbytes((tm, n), F32) + _nbytes((kb, n), BF16) + 4 * _nbytes((tm, kb), BF16))
        + 8 * _nbytes((tm, kb), F32),
    )(dy, w4, dact)


def _mm_t_natural(name, dy, w, layer):
    t, n = dy.shape
    k = w.shape[1]
    tm = _tile(t, 1024)
    tk = _tile(k, 1024)
    return _mm(
        name, NT, dy, w, grid=(t // tm, k // tk, 1),
        a_spec=pl.BlockSpec((tm, n), lambda i, j, kk: (i, 0)),
        b_spec=pl.BlockSpec((None, tk, n), lambda i, j, kk: (layer, j, 0)),
        out_shape=jax.ShapeDtypeStruct((t, k), BF16),
        out_spec=pl.BlockSpec((tm, tk), lambda i, j, kk: (i, j)),
        acc_shape=(tm, tk))


def _mm_t_natural_norm_bwd(name, dy, w, layer, x, g, dx_up):
    t, n = dy.shape
    k = w.shape[1]
    tm = _tile(t, 1024)

    def body(a_ref, b_ref, x_ref, g_ref, up_ref, dx_ref, dg_ref):
        @pl.when(pl.program_id(0) == 0)
        def _():
            dg_ref[...] = jnp.zeros_like(dg_ref)

        dh = lax.dot_general(a_ref[...].astype(BF16), b_ref[...], NT, preferred_element_type=F32)
        xf = x_ref[...]
        r = lax.rsqrt(jnp.mean(xf * xf, axis=-1, keepdims=True) + EPS)
        xhat = xf * r
        dxhat = dh * g_ref[...]
        dg_ref[...] += jnp.sum(dh * xhat, axis=0, keepdims=True)
        dx_ref[...] = up_ref[...] + r * (dxhat - xhat * jnp.mean(dxhat * xhat, axis=-1, keepdims=True))

    row = pl.BlockSpec((tm, k), lambda i: (i, 0))
    vec = pl.BlockSpec((1, k), lambda i: (0, 0))
    dx, dg = _call(
        body, name, (t // tm,),
        [pl.BlockSpec((tm, n), lambda i: (i, 0)), pl.BlockSpec((None, k, n), lambda i: (layer, 0, 0)), row, vec, row],
        [row, vec],
        [jax.ShapeDtypeStruct((t, k), F32), jax.ShapeDtypeStruct((1, k), F32)],
        vmem_bytes=2 * (_nbytes((tm, n), dy.dtype) + _nbytes((k, n), BF16)) + 10 * _nbytes((tm, k), F32),
    )(dy, w, x, g.reshape(1, k), dx_up)
    return dx, dg.reshape(k)


def _mm_dw_colblock(name, h, dz, blocked_in=False, transposed=False):
    t, k = h.shape
    nb = dz.shape[2] if blocked_in else dz.shape[1] // N_DEV
    tk = _tile(t, 4096)
    h_spec = pl.BlockSpec((tk, k), lambda i, j, kk: (kk, 0))
    if blocked_in:
        dz_spec = pl.BlockSpec((None, tk, nb), lambda i, j, kk: (j, kk, 0))
    else:
        dz_spec = pl.BlockSpec((tk, nb), lambda i, j, kk: (kk, j))
    rows, cols = (nb, k) if transposed else (k, nb)
    return _mm(
        name, TN, *((dz, h) if transposed else (h, dz)), grid=(1, N_DEV, t // tk),
        a_spec=dz_spec if transposed else h_spec,
        b_spec=h_spec if transposed else dz_spec,
        out_shape=jax.ShapeDtypeStruct((N_DEV, rows, cols), BF16),
        out_spec=pl.BlockSpec((None, rows, cols), lambda i, j, kk: (j, 0, 0)),
        acc_shape=(rows, cols))


def _mm_dw_natural(name, a, dy):
    t, k = a.shape
    n = dy.shape[1]
    tko = _tile(k, 1024)
    tt = _tile(t, 2048)
    out = _mm(
        name, TN, a, dy, grid=(k // tko, 1, t // tt),
        a_spec=pl.BlockSpec((tt, tko), lambda i, j, kk: (kk, i)),
        b_spec=pl.BlockSpec((tt, n), lambda i, j, kk: (kk, 0)),
        out_shape=jax.ShapeDtypeStruct((k, n), BF16),
        out_spec=pl.BlockSpec((tko, n), lambda i, j, kk: (i, 0)),
        acc_shape=(tko, n))
    return out.reshape(N_DEV, k // N_DEV, n)


def _mm_dw_down(name, act, dy):
    nblk, t, kb = act.shape
    n = dy.shape[1]
    tt = _tile(t, 2048)
    out = _mm(
        name, TN, act, dy, grid=(nblk, 1, t // tt),
        a_spec=pl.BlockSpec((None, tt, kb), lambda i, j, kk: (i, kk, 0)),
        b_spec=pl.BlockSpec((tt, n), lambda i, j, kk: (kk, 0)),
        out_shape=jax.ShapeDtypeStruct((nblk, kb, n), BF16),
        out_spec=pl.BlockSpec((None, kb, n), lambda i, j, kk: (i, 0, 0)),
        acc_shape=(kb, n))
    return out.reshape(N_DEV, (nblk * kb) // N_DEV, n)


def _spatial_mask(transposed=False):
    r = lax.broadcasted_iota(jnp.int32, (A_CHUNK, A_CHUNK), 0) // CHUNK
    c = lax.broadcasted_iota(jnp.int32, (A_CHUNK, A_CHUNK), 1) // CHUNK
    return c >= r if transposed else r >= c


def _sgu_tile(t):
    return _tile(t, 2 * A_CHUNK)


def _sgu_fwd(zpre, g_sgu, w_sp, b_full, name):
    t, f2 = zpre.shape
    f = f2 // 2
    gd = f // A_GROUPS
    tm = _sgu_tile(t)

    def body(z_ref, g_ref, w_ref, b_ref, p_ref, zs_ref, dg_ref):
        mask = _spatial_mask()
        wm = [jnp.where(mask, w_ref[g], 0.0).astype(BF16) for g in range(A_GROUPS)]
        for c in range(tm // A_CHUNK):
            rows = pl.ds(c * A_CHUNK, A_CHUNK)
            z, dgelu = _gelu_and_grad(z_ref[rows, :].astype(F32))
            zs_ref[rows, :] = z.astype(BF16)
            dg_ref[rows, :] = dgelu.astype(BF16)
            u = z[:, :f]
            v0 = z[:, f:]
            r = lax.rsqrt(jnp.mean(v0 * v0, axis=-1, keepdims=True) + EPS)
            v1 = (v0 * r * g_ref[...]).astype(BF16)
            for g in range(A_GROUPS):
                cols = slice(g * gd, (g + 1) * gd)
                v2 = jnp.dot(wm[g], v1[:, cols], preferred_element_type=F32) + b_ref[:, cols]
                p_ref[rows, cols] = (u[:, cols] * v2).astype(BF16)

    return _call(
        body, name, (t // tm,),
        [pl.BlockSpec((tm, f2), lambda i: (i, 0)),
         pl.BlockSpec((1, f), lambda i: (0, 0)),
         pl.BlockSpec((A_GROUPS, A_CHUNK, A_CHUNK), lambda i: (0, 0, 0)),
         pl.BlockSpec((A_CHUNK, f), lambda i: (0, 0))],
        [pl.BlockSpec((tm, f), lambda i: (i, 0)), pl.BlockSpec((tm, f2), lambda i: (i, 0)),
         pl.BlockSpec((tm, f2), lambda i: (i, 0))],
        [jax.ShapeDtypeStruct((t, f), BF16), jax.ShapeDtypeStruct((t, f2), BF16), jax.ShapeDtypeStruct((t, f2), BF16)],
        vmem_bytes=6 * _nbytes((tm, f2), BF16) + 2 * _nbytes((tm, f), BF16) + 8 * _nbytes((A_CHUNK, f2), F32),
    )(zpre, g_sgu.reshape(1, f), w_sp, b_full)


def _sgu_bwd(zs, dgs, dp, g_sgu, w_sp, w_sp_t, b_full, name):
    t, f2 = zs.shape
    f = f2 // 2
    gd = f // A_GROUPS
    tm = _sgu_tile(t)
    n_steps = t // tm

    def body(z_ref, dgelu_ref, dp_ref, g_ref, w_ref, wt_ref, b_ref, dz_ref, dw_ref, db_ref, dg_ref, dv1_ref, dbf_ref):
        step = pl.program_id(0)

        @pl.when(step == 0)
        def _():
            dw_ref[...] = jnp.zeros_like(dw_ref)
            dg_ref[...] = jnp.zeros_like(dg_ref)
            dbf_ref[...] = jnp.zeros_like(dbf_ref)

        mask = _spatial_mask()
        mask_t = _spatial_mask(transposed=True)
        wm = [jnp.where(mask, w_ref[g], 0.0).astype(BF16) for g in range(A_GROUPS)]
        wmt = [jnp.where(mask_t, wt_ref[g], 0.0).astype(BF16) for g in range(A_GROUPS)]
        gain = g_ref[...]
        for c in range(tm // A_CHUNK):
            rows = pl.ds(c * A_CHUNK, A_CHUNK)
            z = z_ref[rows, :].astype(F32)
            dgelu = dgelu_ref[rows, :].astype(F32)
            u = z[:, :f]
            v0 = z[:, f:]
            r = lax.rsqrt(jnp.mean(v0 * v0, axis=-1, keepdims=True) + EPS)
            xhat = v0 * r
            v1 = (xhat * gain).astype(BF16)
            dpf = dp_ref[rows, :].astype(F32)
            for g in range(A_GROUPS):
                cols = slice(g * gd, (g + 1) * gd)
                v1g = v1[:, cols]
                v2 = jnp.dot(wm[g], v1g, preferred_element_type=F32) + b_ref[:, cols]
                dpg = dpf[:, cols]
                dz_ref[rows, cols] = (dpg * v2 * dgelu[:, cols]).astype(BF16)
                dv2 = dpg * u[:, cols]
                dbf_ref[:, cols] += dv2
                dv2b = dv2.astype(BF16)
                dwg = lax.dot_general(dv2b, v1g, NT, preferred_element_type=F32)
                dw_ref[g] += jnp.where(mask, dwg, 0.0)
                dv1_ref[:, cols] = jnp.dot(wmt[g], dv2b, preferred_element_type=F32)
            dv1 = dv1_ref[...]
            dxhat = dv1 * gain
            dg_ref[...] += jnp.sum(dv1 * xhat, axis=0, keepdims=True)
            dv0 = r * (dxhat - xhat * jnp.mean(dxhat * xhat, axis=-1, keepdims=True))
            dz_ref[rows, pl.ds(f, f)] = (dv0 * dgelu[:, f:]).astype(BF16)

        @pl.when(step == n_steps - 1)
        def _():
            for g in range(A_GROUPS):
                db_ref[g] = jnp.sum(dbf_ref[:, g * gd:(g + 1) * gd], axis=1, keepdims=True)

    wspec = pl.BlockSpec((A_GROUPS, A_CHUNK, A_CHUNK), lambda i: (0, 0, 0))
    dz, dw, db, dg = _call(
        body, name, (n_steps,),
        [pl.BlockSpec((tm, f2), lambda i: (i, 0)),
         pl.BlockSpec((tm, f2), lambda i: (i, 0)),
         pl.BlockSpec((tm, f), lambda i: (i, 0)),
         pl.BlockSpec((1, f), lambda i: (0, 0)),
         wspec, wspec,
         pl.BlockSpec((A_CHUNK, f), lambda i: (0, 0))],
        [pl.BlockSpec((tm, f2), lambda i: (i, 0)),
         wspec,
         pl.BlockSpec((A_GROUPS, A_CHUNK, 1), lambda i: (0, 0, 0)),
         pl.BlockSpec((1, f), lambda i: (0, 0))],
        [jax.ShapeDtypeStruct((t, f2), BF16),
         jax.ShapeDtypeStruct((A_GROUPS, A_CHUNK, A_CHUNK), F32),
         jax.ShapeDtypeStruct((A_GROUPS, A_CHUNK, 1), F32),
         jax.ShapeDtypeStruct((1, f), F32)],
        scratch=[pltpu.VMEM((A_CHUNK, f), F32), pltpu.VMEM((A_CHUNK, f), F32)],
        vmem_bytes=6 * _nbytes((tm, f2), BF16) + 2 * _nbytes((tm, f), BF16) + 12 * _nbytes((A_CHUNK, f2), F32),
    )(zs, dgs, dp, g_sgu.reshape(1, f), w_sp, w_sp_t, b_full)
    return dz, dw, db.reshape(A_GROUPS, A_CHUNK), dg.reshape(f)


def _pair_valid(qi, col):
    qc = qi // CHUNK
    kc = col // CHUNK
    return (kc >= qc) & (kc <= qc + N_LEFT_CHUNKS)


def _diagonal_onehot():
    e = lax.broadcasted_iota(jnp.int32, (REL_PAD, DIAGONALS), 1)
    idx = jnp.clip(PAIR_BAND - 1 - e, -MAX_REL, MAX_REL) + MAX_REL
    r = lax.broadcasted_iota(jnp.int32, (REL_PAD, DIAGONALS), 0)
    return jnp.where(r == idx, 1.0, 0.0).astype(BF16)


def _bias_build(table, name):
    h = table.shape[0]
    tab = jnp.pad(table, ((0, 0), (0, REL_PAD - N_REL)))

    def body(t_ref, o_ref):
        oh = _diagonal_onehot()
        diag = jnp.zeros((h, DIAGONALS), F32)
        for piece in _split3(t_ref[...]):
            diag += jnp.dot(piece, oh, preferred_element_type=F32)
        col = lax.broadcasted_iota(jnp.int32, (h, PAIR_BAND), 1)
        for qi in range(PAIR_ROWS):
            row = pltpu.roll(diag, (qi - (PAIR_ROWS - 1)) % DIAGONALS, 1)[:, :PAIR_BAND]
            o_ref[qi] = jnp.where(_pair_valid(qi, col), row, NEG_INF)

    out = _call(
        body, name, (1,),
        [pl.BlockSpec((h, REL_PAD), lambda i: (0, 0))],
        pl.BlockSpec((PAIR_ROWS, h, PAIR_BAND), lambda i: (0, 0, 0)),
        jax.ShapeDtypeStruct((PAIR_ROWS, h, PAIR_BAND), F32),
        vmem_bytes=4 * _nbytes((PAIR_ROWS, h, PAIR_BAND), F32),
    )(tab)
    return jnp.transpose(out, (1, 0, 2))


def _bias_block(pair_bias):
    rest = K_BLOCK - PAIR_BAND
    return jnp.concatenate(
        [jnp.pad(pair_bias, ((0, 0), (0, 0), (p * PAIR_ROWS, rest - p * PAIR_ROWS)), constant_values=NEG_INF)
         for p in range(PAIRS_PER_BLOCK)], axis=1)


def _bias_grad(dbias, name):
    h = dbias.shape[0]
    db_t = jnp.transpose(dbias, (1, 0, 2))

    def body(d_ref, o_ref):
        diag = jnp.zeros((h, DIAGONALS), F32)
        for qi in range(PAIR_ROWS):
            diag += pltpu.roll(d_ref[qi], PAIR_ROWS - 1 - qi, 1)
        oh = _diagonal_onehot()
        acc = jnp.zeros((h, REL_PAD), F32)
        for piece in _split3(diag):
            acc += lax.dot_general(piece, oh, NT, preferred_element_type=F32)
        o_ref[...] = acc

    out = _call(
        body, name, (1,),
        [pl.BlockSpec((PAIR_ROWS, h, DIAGONALS), lambda i: (0, 0, 0))],
        pl.BlockSpec((h, REL_PAD), lambda i: (0, 0)),
        jax.ShapeDtypeStruct((h, REL_PAD), F32),
        vmem_bytes=4 * _nbytes((PAIR_ROWS, h, DIAGONALS), F32),
    )(db_t)
    return out[:, :N_REL]


def _head_masks():
    lane = lax.broadcasted_iota(jnp.int32, (Q_BLOCK, HEAD_PAIR), 1)
    return lane < HEAD_DIM, lane >= HEAD_DIM


def _block_scores(qm, kb, bias, valid):
    s = lax.dot_general(qm, kb, NT, preferred_element_type=F32) + bias
    return s if valid is None else jnp.where(valid, s, NEG_INF)


def _softmax_rows(s):
    e = jnp.exp(s - jnp.max(s, axis=-1, keepdims=True))
    return e * (1.0 / jnp.sum(e, axis=-1, keepdims=True))


def _padded_then_plain(step, n_blocks):
    n_padded = min(LEFT // Q_BLOCK, n_blocks)
    lax.fori_loop(0, n_padded, lambda j, c: step(j, c, True), 0, unroll=True)
    lax.fori_loop(n_padded, n_blocks, lambda j, c: step(j, c, False), 0, unroll=ATTN_UNROLL)


def _attn_fwd(q, kvpad, bias, name):
    t, d = q.shape
    n_pairs = d // HEAD_PAIR
    n_blocks = t // Q_BLOCK

    def body(q_ref, k_ref, v_ref, b_ref, o_ref):
        masks = _head_masks()
        key = lax.broadcasted_iota(jnp.int32, (Q_BLOCK, K_BLOCK), 1)

        def step(j, carry, padded):
            r0 = pl.multiple_of(j * Q_BLOCK, Q_BLOCK)
            q2 = q_ref[pl.ds(r0, Q_BLOCK), :].astype(F32)
            kb = k_ref[pl.ds(r0, K_BLOCK), :]
            vb = v_ref[pl.ds(r0, K_BLOCK), :]
            valid = key >= LEFT - j * Q_BLOCK if padded else None
            scores = [_block_scores(jnp.where(masks[a], q2, 0.0).astype(BF16), kb, b_ref[a], valid) for a in range(2)]
            probs = [_softmax_rows(s).astype(BF16) for s in scores]
            outs = [jnp.dot(p, vb, preferred_element_type=F32) for p in probs]
            o_ref[pl.ds(r0, Q_BLOCK), :] = jnp.where(masks[0], outs[0], outs[1]).astype(BF16)
            return carry

        _padded_then_plain(step, n_blocks)

    return _call(
        body, name, (n_pairs,),
        [pl.BlockSpec((t, HEAD_PAIR), lambda p: (0, p)),
         pl.BlockSpec((LEFT + t, HEAD_PAIR), lambda p: (0, p)),
         pl.BlockSpec((LEFT + t, HEAD_PAIR), lambda p: (0, n_pairs + p)),
         pl.BlockSpec((2, Q_BLOCK, K_BLOCK), lambda p: (p, 0, 0))],
        pl.BlockSpec((t, HEAD_PAIR), lambda p: (0, p)),
        jax.ShapeDtypeStruct((t, d), BF16),
        vmem_bytes=8 * _nbytes((LEFT + t, HEAD_PAIR), BF16) + 12 * _nbytes((2, Q_BLOCK, K_BLOCK), F32),
    )(q, kvpad, kvpad, bias)


def _attn_bwd(q, kvpad, bias, do, dk_in, dv_in, name):
    t, d = q.shape
    n_pairs = d // HEAD_PAIR
    n_blocks = t // Q_BLOCK
    has_in = dk_in is not None

    def body(*refs):
        refs = list(refs)
        q_ref, k_ref, v_ref, b_ref, do_ref = refs[:5]
        refs = refs[5:]
        if has_in:
            dki_ref, dvi_ref = refs[:2]
            refs = refs[2:]
        dq_ref, dk_ref, dv_ref, db_ref = refs
        masks = _head_masks()
        key = lax.broadcasted_iota(jnp.int32, (Q_BLOCK, K_BLOCK), 1)
        if has_in:
            dk_ref[...] = dki_ref[...]
            dv_ref[...] = dvi_ref[...]
        else:
            dk_ref[...] = jnp.zeros_like(dk_ref)
            dv_ref[...] = jnp.zeros_like(dv_ref)
        db_ref[...] = jnp.zeros_like(db_ref)

        def step(j, carry, padded):
            r0 = pl.multiple_of(j * Q_BLOCK, Q_BLOCK)
            q2 = q_ref[pl.ds(r0, Q_BLOCK), :].astype(F32)
            do2 = do_ref[pl.ds(r0, Q_BLOCK), :].astype(F32)
            kb = k_ref[pl.ds(r0, K_BLOCK), :]
            vb = v_ref[pl.ds(r0, K_BLOCK), :]
            valid = key >= LEFT - j * Q_BLOCK if padded else None
            heads = range(2)
            qms = [jnp.where(masks[a], q2, 0.0).astype(BF16) for a in heads]
            doms = [jnp.where(masks[a], do2, 0.0).astype(BF16) for a in heads]
            scores = [_block_scores(qms[a], kb, b_ref[a], valid) for a in heads]
            dps = [lax.dot_general(doms[a], vb, NT, preferred_element_type=F32) for a in heads]
            ps = [_softmax_rows(s) for s in scores]
            dss = [ps[a] * (dps[a] - jnp.sum(dps[a] * ps[a], axis=-1, keepdims=True)) for a in heads]
            for a in heads:
                for pair in range(PAIRS_PER_BLOCK):
                    lo = pair * PAIR_ROWS
                    db_ref[a, :, pl.ds(0, PAIR_BAND)] += dss[a][lo:lo + PAIR_ROWS, lo:lo + PAIR_BAND]
            dsbs = [ds.astype(BF16) for ds in dss]
            pbs = [p.astype(BF16) for p in ps]
            dqs = [jnp.dot(dsbs[a], kb, preferred_element_type=F32) for a in heads]
            dk_acc = sum(lax.dot_general(dsbs[a], qms[a], TN, preferred_element_type=F32) for a in heads)
            dv_acc = sum(lax.dot_general(pbs[a], doms[a], TN, preferred_element_type=F32) for a in heads)
            dq = jnp.where(masks[0], dqs[0], dqs[1]) * ATTN_SCALE
            dq_ref[pl.ds(r0, Q_BLOCK), :] = dq.astype(BF16)
            dk_ref[pl.ds(r0, K_BLOCK), :] += dk_acc
            dv_ref[pl.ds(r0, K_BLOCK), :] += dv_acc
            return carry

        _padded_then_plain(step, n_blocks)

    q_spec = pl.BlockSpec((t, HEAD_PAIR), lambda p: (0, p))
    kv_spec = pl.BlockSpec((LEFT + t, HEAD_PAIR), lambda p: (0, p))
    operands = [q, kvpad, kvpad, bias, do]
    in_specs = [q_spec, kv_spec, pl.BlockSpec((LEFT + t, HEAD_PAIR), lambda p: (0, n_pairs + p)),
                pl.BlockSpec((2, Q_BLOCK, K_BLOCK), lambda p: (p, 0, 0)), q_spec]
    aliases = None
    if has_in:
        operands += [dk_in, dv_in]
        in_specs += [kv_spec, kv_spec]
        aliases = {5: 1, 6: 2}
    return _call(
        body, name, (n_pairs,),
        in_specs,
        [q_spec, kv_spec, kv_spec, pl.BlockSpec((2, PAIR_ROWS, DIAGONALS), lambda p: (p, 0, 0))],
        [jax.ShapeDtypeStruct((t, d), BF16),
         jax.ShapeDtypeStruct((LEFT + t, d), F32),
         jax.ShapeDtypeStruct((LEFT + t, d), F32),
         jax.ShapeDtypeStruct((d // HEAD_DIM, PAIR_ROWS, DIAGONALS), F32)],
        vmem_bytes=10 * _nbytes((LEFT + t, HEAD_PAIR), BF16) + 8 * _nbytes((LEFT + t, HEAD_PAIR), F32)
        + 16 * _nbytes((2, Q_BLOCK, K_BLOCK), F32),
        aliases=aliases,
    )(*operands)


def _loss_head(x, g, target, name):
    t, d = x.shape
    tm = _tile(t, 512)

    def body(x_ref, g_ref, t_ref, dx_ref, loss_ref, dg_ref):
        @pl.when(pl.program_id(0) == 0)
        def _():
            loss_ref[...] = jnp.zeros_like(loss_ref)
            dg_ref[...] = jnp.zeros_like(dg_ref)

        xf = x_ref[...]
        r = lax.rsqrt(jnp.mean(xf * xf, axis=-1, keepdims=True) + EPS)
        xhat = xf * r
        diff = xhat * g_ref[...] - t_ref[...]
        row_loss = jnp.mean(diff * diff, axis=-1, keepdims=True)
        loss_ref[...] += 0.5 * jnp.sum(row_loss, axis=0, keepdims=True)
        dy = diff * (1.0 / d)
        dg_ref[...] += jnp.sum(dy * xhat, axis=0, keepdims=True)
        dxhat = dy * g_ref[...]
        dx_ref[...] = r * (dxhat - xhat * jnp.mean(dxhat * xhat, axis=-1, keepdims=True))

    row = pl.BlockSpec((tm, d), lambda i: (i, 0))
    vec = pl.BlockSpec((1, d), lambda i: (0, 0))
    dx, loss, dg = _call(
        body, name, (t // tm,),
        [row, vec, row],
        [row, pl.BlockSpec((1, 1), lambda i: (0, 0)), vec],
        [jax.ShapeDtypeStruct((t, d), F32), jax.ShapeDtypeStruct((1, 1), F32), jax.ShapeDtypeStruct((1, d), F32)],
        vmem_bytes=10 * _nbytes((tm, d), F32),
    )(x, g.reshape(1, d), target)
    return dx, loss[0, 0], dg.reshape(d)


def _adamw_store(g, w_ref, m_ref, v_ref, g_ref, d_ref, nm_ref, nv_ref):
    c1 = 1.0 / (1.0 - ADAM_B1 ** ADAM_STEP)
    c2 = 1.0 / (1.0 - ADAM_B2 ** ADAM_STEP)
    nm = ADAM_B1 * m_ref[...] + (1.0 - ADAM_B1) * g
    nv = ADAM_B2 * v_ref[...] + (1.0 - ADAM_B2) * (g * g)
    g_ref[...] = g
    nm_ref[...] = nm
    nv_ref[...] = nv
    d_ref[...] = -ADAM_LR * ((nm * c1) / (jnp.sqrt(nv * c2) + ADAM_EPS) + ADAM_WD * w_ref[...])


def _adamw_layer(recv, own, w, m, v, layer, prev, me, name):
    n_src, r, c = recv.shape
    tr = _row_tile(r, max(BF16_SUBLANES, ADAMW_BLOCK_ELEMS // c), BF16_SUBLANES)
    first = prev is None

    def body(me_ref, recv_ref, own_ref, w_ref, m_ref, v_ref, *rest):
        mine = me_ref[0]
        own_part = own_ref[...].astype(F32)
        g = None
        for s in range(n_src):
            part = jnp.where(mine == s, own_part, recv_ref[s].astype(F32))
            g = part if g is None else g + part
        _adamw_store(g, w_ref, m_ref, v_ref, *rest[-4:])

    blk = pl.BlockSpec((None, tr, c), lambda i, me_ref: (layer, i, 0))
    any_spec = pl.BlockSpec(memory_space=pl.ANY)
    out = jax.ShapeDtypeStruct(w.shape, F32)
    operands = [me, recv, own, w, m, v] + ([] if first else list(prev))
    vmem = 2 * _nbytes((n_src + 1, tr, c), BF16) + 18 * _nbytes((tr, c), F32)
    return pl.pallas_call(
        body,
        name=name,
        grid_spec=pltpu.PrefetchScalarGridSpec(
            num_scalar_prefetch=1,
            grid=(r // tr,),
            in_specs=[pl.BlockSpec((n_src, tr, c), lambda i, me_ref: (0, i, 0)),
                      pl.BlockSpec((None, tr, c), lambda i, me_ref: (me_ref[0], i, 0)),
                      blk, blk, blk] + ([] if first else [any_spec] * 4),
            out_specs=[blk, blk, blk, blk],
        ),
        out_shape=[out, out, out, out],
        input_output_aliases={} if first else {6 + j: j for j in range(4)},
        compiler_params=pltpu.CompilerParams(
            dimension_semantics=("arbitrary",),
            vmem_limit_bytes=int(min(max(VMEM_FLOOR_BYTES, vmem * 5 // 4), VMEM_CEIL_BYTES))),
    )(*operands)


def _adamw(parts, w, m, v, name):
    n_layers, n_src, r, c = parts.shape
    mult = BF16_SUBLANES if parts.dtype == BF16 else F32_SUBLANES
    tr = _row_tile(r, max(mult, ADAMW_BLOCK_ELEMS // c), mult)

    def body(p_ref, w_ref, m_ref, v_ref, g_ref, d_ref, nm_ref, nv_ref):
        g = p_ref[0].astype(F32)
        for s in range(1, n_src):
            g = g + p_ref[s].astype(F32)
        _adamw_store(g, w_ref, m_ref, v_ref, g_ref, d_ref, nm_ref, nv_ref)

    blk = pl.BlockSpec((None, tr, c), lambda l, i: (l, i, 0))
    out = jax.ShapeDtypeStruct((n_layers, r, c), F32)
    return _call(
        body, name, (n_layers, r // tr),
        [pl.BlockSpec((None, n_src, tr, c), lambda l, i: (l, 0, i, 0)), blk, blk, blk],
        [blk, blk, blk, blk],
        [out, out, out, out],
        vmem_bytes=2 * _nbytes((n_src, tr, c), parts.dtype) + 18 * _nbytes((tr, c), F32),
    )(parts, w, m, v)


def _ordered_sum(parts, name):
    n_src, r, c = parts.shape

    def body(p_ref, o_ref):
        acc = p_ref[0]
        for s in range(1, n_src):
            acc = acc + p_ref[s]
        o_ref[...] = acc

    return _call(
        body, name, (1,),
        [pl.BlockSpec((n_src, r, c), lambda i: (0, 0, 0))],
        pl.BlockSpec((r, c), lambda i: (0, 0)),
        jax.ShapeDtypeStruct((r, c), F32),
        vmem_bytes=4 * _nbytes((n_src, r, c), F32),
    )(parts)


def _position():
    return lax.axis_index("x"), lax.axis_index("y"), lax.axis_index("c")


def _linear(p):
    return 4 * p[0] + 2 * p[1] + p[2]


def _all_gather(shards, name):
    n = len(shards)

    def body(*refs):
        ins, outs = refs[:n], refs[n:2 * n]
        send_sems, recv_sems, local_sems = refs[2 * n:]
        x, y, c = _position()
        me, sibling = (x, y, c), (x, y, 1 - c)
        chips = [(1 - x, y), (x, 1 - y), (1 - x, 1 - y)]

        def slab(t, p):
            return outs[t].at[:, _linear(p)]

        def copy(t, k, block, to, src=None):
            return pltpu.make_async_remote_copy(
                src_ref=slab(t, block) if src is None else src,
                dst_ref=slab(t, block),
                send_sem=send_sems.at[t, k],
                recv_sem=recv_sems.at[t, k],
                device_id=to,
                device_id_type=MESH,
            )

        started = []
        for t in range(n):
            mine = pltpu.make_async_copy(ins[t], slab(t, me), local_sems.at[t])
            mine.start()
            started.append(mine)
        sends = []
        for t in range(n):
            first = [copy(t, 0, me, sibling, src=ins[t])]
            first += [copy(t, 1 + j, me, (*chip, c), src=ins[t]) for j, chip in enumerate(chips)]
            for cp in first:
                cp.start()
            sends += first
        for t in range(n):
            for j, chip in enumerate(chips):
                copy(t, 1 + j, (*chip, c), me).wait_recv()
                passed = copy(t, 4 + j, (*chip, c), sibling)
                passed.start()
                sends.append(passed)
        for t in range(n):
            copy(t, 0, sibling, me).wait_recv()
            for j, chip in enumerate(chips):
                copy(t, 4 + j, (*chip, 1 - c), me).wait_recv()
        for cp in sends:
            cp.wait_send()
        for mine in started:
            mine.wait()

    out_shape = [jax.ShapeDtypeStruct((s.shape[0], N_DEV) + s.shape[1:], s.dtype) for s in shards]
    return pl.pallas_call(
        body,
        name=name,
        in_specs=[HBM_SPEC] * n,
        out_specs=[HBM_SPEC] * n,
        out_shape=out_shape,
        scratch_shapes=[
            pltpu.SemaphoreType.DMA((n, N_DEV - 1)),
            pltpu.SemaphoreType.DMA((n, N_DEV - 1)),
            pltpu.SemaphoreType.DMA((n,)),
        ],
    )(*shards)


def _exchange(blocks, name):
    n = len(blocks)

    def body(*refs):
        ins, outs = refs[:n], refs[n:2 * n]
        send_sems, recv_sems, local_sems = refs[2 * n:]
        x, y, c = _position()
        me = _linear((x, y, c))
        flips = [(fx, fy, fc) for fx in (0, 1) for fy in (0, 1) for fc in (0, 1)][1:]

        def peer_of(flip):
            fx, fy, fc = flip
            return (1 - x if fx else x, 1 - y if fy else y, 1 - c if fc else c)

        def copy(t, k, peer):
            return pltpu.make_async_remote_copy(
                src_ref=ins[t].at[:, _linear(peer)],
                dst_ref=outs[t].at[:, me],
                send_sem=send_sems.at[t, k],
                recv_sem=recv_sems.at[t, k],
                device_id=peer,
                device_id_type=MESH,
            )

        def arrival(t, k, peer):
            return pltpu.make_async_remote_copy(
                src_ref=ins[t].at[:, _linear(peer)],
                dst_ref=outs[t].at[:, _linear(peer)],
                send_sem=send_sems.at[t, k],
                recv_sem=recv_sems.at[t, k],
                device_id=peer,
                device_id_type=MESH,
            )

        own = []
        for t in range(n):
            cp = pltpu.make_async_copy(ins[t].at[:, me], outs[t].at[:, me], local_sems.at[t])
            cp.start()
            own.append(cp)
        sends = []
        for t in range(n):
            for k, flip in enumerate(flips):
                cp = copy(t, k, peer_of(flip))
                cp.start()
                sends.append(cp)
        for t in range(n):
            for k, flip in enumerate(flips):
                arrival(t, k, peer_of(flip)).wait_recv()
        for cp in sends:
            cp.wait_send()
        for cp in own:
            cp.wait()

    out_shape = [jax.ShapeDtypeStruct(b.shape, b.dtype) for b in blocks]
    return pl.pallas_call(
        body,
        name=name,
        in_specs=[HBM_SPEC] * n,
        out_specs=[HBM_SPEC] * n,
        out_shape=out_shape,
        scratch_shapes=[
            pltpu.SemaphoreType.DMA((n, N_DEV - 1)),
            pltpu.SemaphoreType.DMA((n, N_DEV - 1)),
            pltpu.SemaphoreType.DMA((n,)),
        ],
    )(*blocks)


def _peers():
    x, y, c = _position()
    flips = [(fx, fy, fc) for fx in (0, 1) for fy in (0, 1) for fc in (0, 1)][1:]
    return [(1 - x if fx else x, 1 - y if fy else y, 1 - c if fc else c) for fx, fy, fc in flips]


SIBLING, OTHER_CHIPS = (0,), (1, 3, 5)
COPY_PEERS = {"gather": tuple(range(N_DEV - 1)), "exchange": tuple(range(N_DEV - 1)),
              "chips": SIBLING + OTHER_CHIPS, "forward": OTHER_CHIPS}


def _split_copy(kind, src_ref, land_ref, k, send_sem, recv_sem, starting):
    peers = _peers()
    peer = peers[SIBLING[0]] if kind == "forward" else peers[k]
    me = _linear(_position())
    if kind == "forward":
        slab = _linear(peers[k]) if starting else 0
        src, dst = land_ref.at[slab], land_ref.at[slab]
    elif kind == "exchange":
        src, dst = src_ref.at[_linear(peer) if starting else 0], land_ref.at[me if starting else 0]
    else:
        src, dst = src_ref, land_ref.at[me if starting else 0]
    return pltpu.make_async_remote_copy(src_ref=src, dst_ref=dst, send_sem=send_sem, recv_sem=recv_sem,
                                        device_id=peer, device_id_type=MESH)


def _split_start(groups, carry, name):
    arrays = [a for _, srcs, lands in groups for a in list(srcs) + list(lands)] + [carry]

    def body(*refs):
        ins, sems = refs[:len(arrays)], refs[len(arrays):len(arrays) + 2 * len(groups)]
        at = 0
        for g, (kind, srcs, lands) in enumerate(groups):
            src_refs, land_refs = ins[at:at + len(srcs)], ins[at + len(srcs):at + len(srcs) + len(lands)]
            at += len(srcs) + len(lands)
            peers = COPY_PEERS[kind]
            for t in range(len(lands)):
                for slot, k in enumerate(peers):
                    sem = t * len(peers) + slot
                    _split_copy(kind, src_refs[t] if srcs else None, land_refs[t], k,
                                sems[2 * g].at[sem], sems[2 * g + 1].at[sem], True).start()

    sem_shapes = [pltpu.SemaphoreType.DMA((len(lands) * len(COPY_PEERS[kind]),))
                  for kind, _, lands in groups for _ in range(2)]
    out = pl.pallas_call(
        body,
        name=name,
        in_specs=[HBM_SPEC] * len(arrays),
        out_specs=[SEM_SPEC] * len(sem_shapes) + [HBM_SPEC] * len(arrays),
        out_shape=sem_shapes + [pltpu.HBM(a.shape, a.dtype) for a in arrays],
        input_output_aliases={i: len(sem_shapes) + i for i in range(len(arrays))},
        compiler_params=pltpu.CompilerParams(has_side_effects=pltpu.SideEffectType.DATAFLOW_SIDE_EFFECTING),
    )(*[pltpu.with_memory_space_constraint(a, pltpu.HBM) for a in arrays])
    sems, thru = out[:len(sem_shapes)], out[len(sem_shapes):]
    started, at = [], 0
    for g, (kind, srcs, lands) in enumerate(groups):
        n_s, n_l = len(srcs), len(lands)
        started.append((kind, sems[2 * g], sems[2 * g + 1], thru[at:at + n_s], thru[at + n_s:at + n_s + n_l]))
        at += n_s + n_l
    return started, thru[-1]


def _split_wait(started, after, name):
    kind, send_sems, recv_sems, srcs, lands = started
    n_s, n_l = len(srcs), len(lands)
    peers = COPY_PEERS[kind]

    def body(*refs):
        src_refs, land_refs = refs[:n_s], refs[n_s:n_s + n_l]
        send_ref, recv_ref = refs[n_s + n_l], refs[n_s + n_l + 1]
        for t in range(n_l):
            for slot, k in enumerate(peers):
                sem = t * len(peers) + slot
                copy = _split_copy(kind, src_refs[t] if n_s else None, land_refs[t], k,
                                   send_ref.at[sem], recv_ref.at[sem], False)
                copy.wait_send()
                copy.wait_recv()

    arrays = list(srcs) + list(lands)
    out = pl.pallas_call(
        body,
        name=name,
        in_specs=[HBM_SPEC] * len(arrays) + [SEM_SPEC, SEM_SPEC, pl.BlockSpec(memory_space=pl.ANY)],
        out_specs=[HBM_SPEC] * len(arrays),
        out_shape=[pltpu.HBM(a.shape, a.dtype) for a in arrays],
        input_output_aliases={i: i for i in range(len(arrays))},
        compiler_params=pltpu.CompilerParams(has_side_effects=pltpu.SideEffectType.DATAFLOW_SIDE_EFFECTING),
    )(*arrays, send_sems, recv_sems, after)
    return out[:n_s], out[n_s:]


def _pack(arrays, row_multiple):
    flat = jnp.concatenate([a.reshape(-1) for a in arrays])
    quantum = row_multiple * FLAT_LANES
    padded = -(-flat.shape[0] // quantum) * quantum
    return jnp.pad(flat, (0, padded - flat.shape[0])).reshape(-1, FLAT_LANES)


def _unpack(flat, like):
    flat = flat.reshape(-1)
    out, at = [], 0
    for a in like:
        size = math.prod(a.shape)
        out.append(flat[at:at + size].reshape(a.shape))
        at += size
    return out


def kernel(x, a_norm, a_w_in, a_sgu_norm, a_w_spatial, a_b_spatial, a_w_out, kv_norm, w_kv, b_norm, b_w_q, b_rel_bias, b_w_o, ffn_norm, ffn_w_gate_up, ffn_w_down, final_norm, loss_target, m_a_norm, m_a_w_in, m_a_sgu_norm, m_a_w_spatial, m_a_b_spatial, m_a_w_out, m_kv_norm, m_w_kv, m_b_norm, m_b_w_q, m_b_rel_bias, m_b_w_o, m_ffn_norm, m_ffn_w_gate_up, m_ffn_w_down, m_final_norm, v_a_norm, v_a_w_in, v_a_sgu_norm, v_a_w_spatial, v_a_b_spatial, v_a_w_out, v_kv_norm, v_w_kv, v_b_norm, v_b_w_q, v_b_rel_bias, v_b_w_o, v_ffn_norm, v_ffn_w_gate_up, v_ffn_w_down, v_final_norm):
    xs = x[0]
    target = loss_target[0]
    t, d = xs.shape
    n_a = a_w_in.shape[0]
    n_b = b_w_q.shape[0]
    depth = ffn_w_gate_up.shape[0]
    f_a = a_w_out.shape[1] * N_DEV
    gd = f_a // A_GROUPS
    nb_ffn = ffn_w_gate_up.shape[2]
    me = _linear(_position())

    small_rows = -(-(a_norm.size + a_sgu_norm.size) // (8 * 128)) * 8
    small = jnp.pad(jnp.concatenate([a_norm.reshape(-1), a_sgu_norm.reshape(-1)]),
                    (0, small_rows * 128 - a_norm.size - a_sgu_norm.size)).reshape(1, small_rows, 128)

    def shard(w, layer=None):
        return (w if layer is None else w[layer]).astype(BF16)

    stages = []
    for layer in range(depth):
        if layer == 0:
            stages += [("a0", [shard(a_w_in, 0)]), ("a0_out", [shard(a_w_out, 0)])]
        elif layer < n_a:
            stages.append((f"a{layer}", [shard(a_w_in, layer), shard(a_w_out, layer)]))
        else:
            i = layer - n_a
            shared = [shard(w_kv)] if i == 0 else []
            stages.append((f"b{i}", shared + [shard(b_w_q, i), shard(b_w_o, i)]))
        stages.append((f"f{layer}", [shard(ffn_w_gate_up, layer), shard(ffn_w_down, layer)]))
    first = _all_gather([s[None] for s in stages[0][1]] + [small], "gather_first")
    gathered = {stages[0][0]: [g[0] for g in first[:-1]]}
    small_g = first[-1].reshape(N_DEV, -1)
    a_norm_full = small_g[:, :a_norm.size].reshape(N_DEV, n_a, -1).transpose(1, 0, 2).reshape(n_a, d)
    a_sgu_full = small_g[:, a_norm.size:a_norm.size + a_sgu_norm.size].reshape(
        N_DEV, n_a, -1).transpose(1, 0, 2).reshape(n_a, f_a)
    two_level = ("f0", "a1", "f1")
    later = [("chips" if key in two_level else "gather", shards,
              [lax.dynamic_update_slice(lax.empty((N_DEV,) + s.shape, BF16), s[None], (me, 0, 0)) for s in shards])
             for key, shards in stages[1:]]
    started, a_norm_full = _split_start(later, a_norm_full, "gather_start")
    in_flight = {key: group for (key, _), group in zip(stages[1:], started)}

    def pass_on(key, carry):
        if key in two_level and key in in_flight and in_flight[key][0] == "chips":
            _, lands = _split_wait(in_flight.pop(key), carry, f"gather_wait_{key}_chips")
            (in_flight[key],), carry = _split_start([("forward", [], lands)], carry, f"gather_pass_on_{key}")
        return carry

    def weights(key, after):
        if key not in gathered:
            _, gathered[key] = _split_wait(in_flight.pop(key), after, f"gather_wait_{key}")
        return gathered[key]

    rows_down = ffn_w_down.shape[1]

    def mixer_a_weights(i, after):
        if i == 0:
            (w_in,), (w_out,) = weights("a0", after[0]), weights("a0_out", after[1])
        else:
            w_in, w_out = weights(f"a{i}", after[0])
        return w_in[None], w_out.reshape(1, f_a, d)

    def mixer_b_weights(i, after):
        ws = weights(f"b{i}", after)
        return ws[-2].reshape(1, d, d), ws[-1].reshape(1, d, d)

    def ffn_weights(layer, after):
        w_gu, w_dn = weights(f"f{layer}", after)
        return w_gu[None], w_dn.reshape(1, N_DEV // 2, 2 * rows_down, d)

    w_sp_t = jnp.swapaxes(a_w_spatial, -1, -2)
    b_full = jnp.repeat(jnp.swapaxes(a_b_spatial, -1, -2), gd, axis=-1)

    saved = []

    def ffn_fwd(xin, layer):
        hf = _rms_fwd(xin, ffn_norm[layer], f"ffn_norm_fwd_{layer}")
        w_gu, w_dn = ffn_weights(layer, xin)
        dact, act = _ffn_gate_up(f"ffn_gate_up_{layer}", hf, w_gu, 0)
        act = pass_on(f"a{layer + 1}", act)
        xout = _mm_down(f"ffn_down_{layer}", act, w_dn, 0, xin)
        return xout, (xin, hf, dact, act)

    for i in range(n_a):
        h = _rms_fwd(xs, a_norm_full[i], f"a_norm_fwd_{i}")
        zpre = _mm_colblock(f"a_in_{i}", h, weights(f"a{i}", xs)[0][None], 0)
        p, zs, dgs = _sgu_fwd(zpre, a_sgu_full[i], a_w_spatial[i], b_full[i], f"a_sgu_fwd_{i}")
        p = pass_on(f"f{i}", p)
        w_in, w_out = mixer_a_weights(i, (xs, p))
        x_mid = _mm_natural(f"a_out_{i}", p, w_out, 0, res=xs)
        x_out, ffn_saved = ffn_fwd(x_mid, i)
        saved.append((xs, h, zs, dgs, p, ffn_saved))
        xs = x_out

    x_kv = xs
    w_kv_g = weights("b0", x_kv)[0][None]
    h_kv = _rms_fwd(x_kv, kv_norm, "kv_norm_fwd")
    kv = _mm_colblock("kv_proj", h_kv, w_kv_g, 0)
    kvpad = jnp.pad(kv, ((LEFT, 0), (0, 0)))

    biases = [_bias_block(_bias_build(b_rel_bias[i], f"rel_bias_{i}")) for i in range(n_b)]
    for i in range(n_b):
        layer = n_a + i
        w_q, w_o = mixer_b_weights(i, xs)
        hb = _rms_fwd(xs, b_norm[i], f"b_norm_fwd_{i}")
        q = _mm_natural(f"b_q_{i}", hb, w_q, 0, out_dtype=BF16, scale=ATTN_SCALE)
        o = _attn_fwd(q, kvpad, biases[i], f"b_attn_fwd_{i}")
        x_mid = _mm_natural(f"b_o_{i}", o, w_o, 0, res=xs)
        x_out, ffn_saved = ffn_fwd(x_mid, layer)
        saved.append((xs, hb, q, o, ffn_saved))
        xs = x_out

    dx, loss_local, g_final = _loss_head(xs, final_norm, target, "loss_head")
    loss = lax.psum(loss_local, ("x", "y", "c"))

    big_grads = {}
    pending = []
    in_flight_grads = []

    def start_exchange(dx, tag):
        srcs = [big_grads[key] for key in pending]
        lands = [lax.empty(s.shape, BF16) for s in srcs]
        (group,), dx = _split_start([("exchange", srcs, lands)], dx, f"exchange_start_{tag}")
        in_flight_grads.append((list(pending), group, tag))
        pending.clear()
        return dx

    g_ffn_norm = [None] * depth
    g_a_norm = [None] * n_a
    g_a_sgu = [None] * n_a
    g_w_sp = [None] * n_a
    g_b_sp = [None] * n_a
    g_b_norm = [None] * n_b
    g_rel = [None] * n_b

    def ffn_bwd(dx, layer, ffn_saved):
        eager = layer < n_a
        xin, hf, dact, act = ffn_saved
        big_grads["ffn_w_down", layer] = _mm_dw_down(f"ffn_down_dw_{layer}", act, dx)
        pending.append(("ffn_w_down", layer))
        if eager:
            dx = start_exchange(dx, f"f{layer}_down")
        w_gu, w_dn = ffn_weights(layer, xin)
        dgu = _ffn_down_dx(f"ffn_down_dx_{layer}", dx, w_dn, 0, dact).reshape(N_DEV, t, nb_ffn)
        big_grads["ffn_w_gate_up", layer] = _mm_dw_colblock(
            f"ffn_gate_up_dw_{layer}", hf, dgu, blocked_in=True, transposed=True)
        pending.append(("ffn_w_gate_up", layer))
        if eager:
            dx = start_exchange(dx, f"f{layer}_gate_up")
        dx, g_ffn_norm[layer] = _mm_t_colblock_norm_bwd(
            f"ffn_gate_up_dx_{layer}", dgu, w_gu, 0, xin, ffn_norm[layer], dx, blocked_in=True)
        return dx

    dk = dv = None
    for i in reversed(range(n_b)):
        layer = n_a + i
        x_in, hb, q, o, ffn_saved = saved[layer]
        dx = ffn_bwd(dx, layer, ffn_saved)
        big_grads["b_w_o", i] = _mm_dw_natural(f"b_o_dw_{i}", o, dx)
        w_q, w_o = mixer_b_weights(i, x_in)
        do = _mm_t_natural(f"b_o_dx_{i}", dx, w_o, 0)
        dq, dk, dv, dbias = _attn_bwd(q, kvpad, biases[i], do, dk, dv, f"b_attn_bwd_{i}")
        g_rel[i] = _bias_grad(dbias, f"rel_bias_grad_{i}")
        big_grads["b_w_q", i] = _mm_dw_natural(f"b_q_dw_{i}", hb, dq)
        pending.extend([("b_w_o", i), ("b_w_q", i)])
        dx, g_b_norm[i] = _mm_t_natural_norm_bwd(f"b_q_dx_{i}", dq, w_q, 0, x_in, b_norm[i], dx)
        if i > 0:
            dx = start_exchange(dx, f"b{i}")

    dkv = jnp.concatenate([dk[LEFT:], dv[LEFT:]], axis=1).astype(BF16)
    big_grads["w_kv", 0] = _mm_dw_colblock("kv_proj_dw", h_kv, dkv)
    pending.append(("w_kv", 0))
    dx, g_kv_norm = _mm_t_colblock_norm_bwd("kv_proj_dx", dkv, w_kv_g, 0, x_kv, kv_norm, dx)
    dx = start_exchange(dx, "kv")

    for i in reversed(range(n_a)):
        x_in, h, zs, dgs, p, ffn_saved = saved[i]
        dx = ffn_bwd(dx, i, ffn_saved)
        big_grads["a_w_out", i] = _mm_dw_natural(f"a_out_dw_{i}", p, dx)
        pending.append(("a_w_out", i))
        dx = start_exchange(dx, f"a{i}_out")
        w_in, w_out = mixer_a_weights(i, (x_in, p))
        dp = _mm_t_natural(f"a_out_dx_{i}", dx, w_out, 0)
        dz, g_w_sp[i], g_b_sp[i], g_a_sgu[i] = _sgu_bwd(
            zs, dgs, dp, a_sgu_full[i], a_w_spatial[i], w_sp_t[i], b_full[i], f"a_sgu_bwd_{i}")
        big_grads["a_w_in", i] = _mm_dw_colblock(f"a_in_dw_{i}", h, dz)
        pending.append(("a_w_in", i))
        dx = start_exchange(dx, f"a{i}_in")
        dx, g_a_norm[i] = _mm_t_colblock_norm_bwd(f"a_in_dx_{i}", dz, w_in, 0, x_in, a_norm_full[i], dx)
    grad_x = dx[None]

    small_like = [jax.ShapeDtypeStruct((n_a, d), F32), jax.ShapeDtypeStruct((n_a, f_a), F32),
                  a_w_spatial, a_b_spatial, kv_norm, b_norm, b_rel_bias, ffn_norm, final_norm]
    small_partial = _pack(
        [jnp.stack(g_a_norm), jnp.stack(g_a_sgu), jnp.stack(g_w_sp), jnp.stack(g_b_sp), g_kv_norm,
         jnp.stack(g_b_norm), jnp.stack(g_rel), jnp.stack(g_ffn_norm), g_final], N_DEV * 8)
    chunk_rows = small_partial.shape[0] // N_DEV
    arrived = {}
    for keys, group, tag in in_flight_grads:
        srcs, lands = _split_wait(group, dx, f"exchange_wait_{tag}")
        for key, src, land in zip(keys, srcs, lands):
            arrived[key] = (land, src)
    small_got = _exchange([small_partial.reshape(1, N_DEV, chunk_rows, FLAT_LANES)], "exchange_small")[0]
    small_sum = _ordered_sum(small_got[0], "small_grad_sum")
    small_all = _all_gather([small_sum[None]], "gather_small_grads")[0]
    (ga_norm, ga_sgu, gw_sp, gb_sp, gkv_norm, gb_norm, g_relb, gffn_norm, gfinal) = _unpack(small_all, small_like)

    results = {}
    big_names = ["a_w_in", "a_w_out", "w_kv", "b_w_q", "b_w_o", "ffn_w_gate_up", "ffn_w_down"]
    big_wmv = [(a_w_in, m_a_w_in, v_a_w_in), (a_w_out, m_a_w_out, v_a_w_out),
               (w_kv[None], m_w_kv[None], v_w_kv[None]), (b_w_q, m_b_w_q, v_b_w_q), (b_w_o, m_b_w_o, v_b_w_o),
               tuple(jnp.swapaxes(a, 1, 2) for a in (ffn_w_gate_up, m_ffn_w_gate_up, v_ffn_w_gate_up)),
               (ffn_w_down, m_ffn_w_down, v_ffn_w_down)]
    me_arr = jnp.reshape(me, (1,)).astype(jnp.int32)
    for name, (w, m, v) in zip(big_names, big_wmv):
        outs = None
        for layer in range(w.shape[0]):
            got, own = arrived[name, layer]
            outs = _adamw_layer(got, own, w, m, v, layer, outs, me_arr, f"adamw_{name}_{layer}")
        if name == "w_kv":
            outs = [o[0] for o in outs]
        if name == "ffn_w_gate_up":
            outs = [jnp.swapaxes(o, 1, 2) for o in outs]
        results[name] = outs

    n_cols = a_norm.shape[1]
    s_cols = a_sgu_norm.shape[1]
    small_g_list = [lax.dynamic_slice(ga_norm, (0, me * n_cols), (n_a, n_cols)),
                    lax.dynamic_slice(ga_sgu, (0, me * s_cols), (n_a, s_cols)),
                    gw_sp, gb_sp, gkv_norm, gb_norm, g_relb, gffn_norm, gfinal]
    small_names = ["a_norm", "a_sgu_norm", "a_w_spatial", "a_b_spatial", "kv_norm", "b_norm", "b_rel_bias",
                   "ffn_norm", "final_norm"]
    small_w = [a_norm, a_sgu_norm, a_w_spatial, a_b_spatial, kv_norm, b_norm, b_rel_bias, ffn_norm, final_norm]
    small_m = [m_a_norm, m_a_sgu_norm, m_a_w_spatial, m_a_b_spatial, m_kv_norm, m_b_norm, m_b_rel_bias,
               m_ffn_norm, m_final_norm]
    small_v = [v_a_norm, v_a_sgu_norm, v_a_w_spatial, v_a_b_spatial, v_kv_norm, v_b_norm, v_b_rel_bias,
               v_ffn_norm, v_final_norm]
    flat_g = _pack(small_g_list, 8)
    flat_out = _adamw(flat_g[None, None], _pack(small_w, 8)[None], _pack(small_m, 8)[None],
                      _pack(small_v, 8)[None], "adamw_small")
    unpacked = [_unpack(o[0], small_w) for o in flat_out]
    for idx, name in enumerate(small_names):
        results[name] = [unpacked[kind][idx] for kind in range(4)]

    order = ["a_norm", "a_w_in", "a_sgu_norm", "a_w_spatial", "a_b_spatial", "a_w_out", "kv_norm", "w_kv",
             "b_norm", "b_w_q", "b_rel_bias", "b_w_o", "ffn_norm", "ffn_w_gate_up", "ffn_w_down", "final_norm"]
    outputs = [loss, grad_x]
    for kind in range(4):
        outputs += [results[name][kind] for name in order]
    return tuple(outputs)
```

```python
import math

import jax
import jax.numpy as jnp
from jax import lax
from jax.experimental import pallas as pl
from jax.experimental.pallas import tpu as pltpu

F32 = jnp.float32
BF16 = jnp.bfloat16
MESH = pl.DeviceIdType.MESH
HBM_SPEC = pl.BlockSpec(memory_space=pltpu.HBM)
SEM_SPEC = pl.BlockSpec(memory_space=pltpu.SEMAPHORE)

N_DEV = 8
CHUNK = 64
A_CHUNK = 128
A_GROUPS = 8
N_LEFT_CHUNKS = 8
LEFT = N_LEFT_CHUNKS * CHUNK
PAIR_ROWS = 2 * CHUNK
PAIR_BAND = PAIR_ROWS + LEFT
DIAGONALS = PAIR_BAND + PAIR_ROWS
PAIRS_PER_BLOCK = 2
Q_BLOCK = PAIRS_PER_BLOCK * PAIR_ROWS
K_BLOCK = Q_BLOCK + LEFT
ATTN_UNROLL = 7
MAX_REL = 256
N_REL = 2 * MAX_REL + 1
REL_PAD = 640
HEAD_DIM = 64
HEAD_PAIR = 2 * HEAD_DIM
ATTN_SCALE = HEAD_DIM ** -0.5
EPS = 1e-6
NEG_INF = -1e30
ADAM_LR = 0.001
ADAM_B1 = 0.9
ADAM_B2 = 0.999
ADAM_EPS = 1e-08
ADAM_WD = 0.01
ADAM_STEP = 10
FLAT_LANES = 1024
F32_SUBLANES = 8
BF16_SUBLANES = 16
ADAMW_BLOCK_ELEMS = 256 * 1024
V7X_VMEM_BYTES = 64 * 1024 * 1024
VMEM_FLOOR_BYTES = 32 * 1024 * 1024
VMEM_CEIL_BYTES = V7X_VMEM_BYTES - 8 * 1024 * 1024

NN = (((1,), (0,)), ((), ()))
NT = (((1,), (1,)), ((), ()))
TN = (((0,), (0,)), ((), ()))


def _tile(n, pref):
    return pref if n % pref == 0 else n


def _row_tile(n, pref, mult):
    best = None
    for t in range(mult, min(n, pref) + 1, mult):
        if n % t == 0:
            best = t
    return best if best is not None else n


def _nbytes(shape, dtype):
    n = 1
    for s in shape:
        if s is not None:
            n *= s
    return n * jnp.dtype(dtype).itemsize


def _call(body, name, grid, in_specs, out_specs, out_shape, scratch=(), vmem_bytes=0, aliases=None):
    limit = int(min(max(VMEM_FLOOR_BYTES, vmem_bytes * 5 // 4), VMEM_CEIL_BYTES))
    return pl.pallas_call(
        body,
        name=name,
        grid=grid,
        in_specs=in_specs,
        out_specs=out_specs,
        out_shape=out_shape,
        scratch_shapes=list(scratch),
        input_output_aliases=aliases or {},
        compiler_params=pltpu.CompilerParams(
            dimension_semantics=("arbitrary",) * len(grid), vmem_limit_bytes=limit),
    )


ERFC_P = 0.3275911 / math.sqrt(2.0)
ERFC_HALF_COEFFS = tuple(0.5 * a for a in (1.061405429, -1.453152027, 1.421413741, -0.284496736, 0.254829592))


def _gelu_and_grad(x):
    d = 1.0 + ERFC_P * jnp.abs(x)
    r = pl.reciprocal(d, approx=True)
    t = r * (2.0 - d * r)
    a5, a4, a3, a2, a1 = ERFC_HALF_COEFFS
    ex = jnp.exp(-0.5 * (x * x))
    tail = ((((a5 * t + a4) * t + a3) * t + a2) * t + a1) * t * ex
    cdf = jnp.where(x < 0, tail, 1.0 - tail)
    return x * cdf, cdf + x * ex * (1.0 / math.sqrt(2.0 * math.pi))


def _sigmoid(x):
    return 1.0 / (1.0 + jnp.exp(-x))


def _split3(x):
    hi = x.astype(BF16)
    r1 = x - hi.astype(F32)
    mid = r1.astype(BF16)
    lo = (r1 - mid.astype(F32)).astype(BF16)
    return hi, mid, lo


def _rms_fwd(x, g, name):
    t, d = x.shape
    tm = _tile(t, 512)

    def body(x_ref, g_ref, o_ref):
        xf = x_ref[...]
        r = lax.rsqrt(jnp.mean(xf * xf, axis=-1, keepdims=True) + EPS)
        o_ref[...] = (xf * r * g_ref[...]).astype(o_ref.dtype)

    return _call(
        body, name, (t // tm,),
        [pl.BlockSpec((tm, d), lambda i: (i, 0)), pl.BlockSpec((1, d), lambda i: (0, 0))],
        pl.BlockSpec((tm, d), lambda i: (i, 0)),
        jax.ShapeDtypeStruct((t, d), BF16),
        vmem_bytes=2 * (_nbytes((tm, d), F32) + _nbytes((tm, d), BF16)) + 4 * _nbytes((tm, d), F32),
    )(x, g.reshape(1, d))


def _mm(name, dims, a, b, *, grid, a_spec, b_spec, out_shape, out_spec, acc_shape,
        res=None, res_spec=None, scale=None):
    nk = grid[2]
    has_res = res is not None

    def body(*refs):
        refs = list(refs)
        a_ref = refs.pop(0)
        b_ref = refs.pop(0)
        r_ref = refs.pop(0) if has_res else None
        o_ref = refs.pop(0)
        part = lax.dot_general(a_ref[...].astype(BF16), b_ref[...].astype(BF16), dims,
                               preferred_element_type=F32)

        def finish(acc):
            if scale is not None:
                acc = acc * scale
            if has_res:
                acc = acc + r_ref[...]
            o_ref[...] = acc.astype(o_ref.dtype)

        if nk == 1:
            finish(part)
        else:
            acc_ref = refs.pop(0)
            k = pl.program_id(2)

            @pl.when(k == 0)
            def _():
                acc_ref[...] = part

            @pl.when(k > 0)
            def _():
                acc_ref[...] += part

            @pl.when(k == nk - 1)
            def _():
                finish(acc_ref[...])

    operands = [a, b]
    in_specs = [a_spec, b_spec]
    vmem = 2 * (_nbytes(a_spec.block_shape, a.dtype) + _nbytes(b_spec.block_shape, b.dtype)
                + _nbytes(out_spec.block_shape, out_shape.dtype))
    vmem += 3 * _nbytes(acc_shape, F32)
    if has_res:
        operands.append(res)
        in_specs.append(res_spec)
        vmem += 2 * _nbytes(res_spec.block_shape, res.dtype)
    scratch = [pltpu.VMEM(acc_shape, F32)] if nk > 1 else []
    return _call(body, name, grid, in_specs, out_spec, out_shape, scratch=scratch, vmem_bytes=vmem)(*operands)


def _mm_colblock(name, h, w_g, layer):
    t, k = h.shape
    nb = w_g.shape[3]
    tm = _tile(t, 2048)
    return _mm(
        name, NN, h, w_g, grid=(t // tm, N_DEV, 1),
        a_spec=pl.BlockSpec((tm, k), lambda i, j, kk: (i, 0)),
        b_spec=pl.BlockSpec((None, None, k, nb), lambda i, j, kk: (layer, j, 0, 0)),
        out_shape=jax.ShapeDtypeStruct((t, N_DEV * nb), BF16),
        out_spec=pl.BlockSpec((tm, nb), lambda i, j, kk: (i, j)), acc_shape=(tm, nb))


def _mm_natural(name, a, w, layer, *, res=None, out_dtype=F32, scale=None):
    t, k = a.shape
    n = w.shape[2]
    tm = _tile(t, 1024)
    tn = _tile(n, 1024 if k <= 1024 else 512)
    res_spec = None if res is None else pl.BlockSpec((tm, tn), lambda i, j, kk: (i, j))
    return _mm(
        name, NN, a, w, grid=(t // tm, n // tn, 1),
        a_spec=pl.BlockSpec((tm, k), lambda i, j, kk: (i, 0)),
        b_spec=pl.BlockSpec((None, k, tn), lambda i, j, kk: (layer, 0, j)),
        out_shape=jax.ShapeDtypeStruct((t, n), out_dtype),
        out_spec=pl.BlockSpec((tm, tn), lambda i, j, kk: (i, j)),
        acc_shape=(tm, tn), res=res, res_spec=res_spec, scale=scale)


def _mm_down(name, act, w4, layer, res):
    nblk, t, kb = act.shape
    n = w4.shape[3]
    tm = _tile(t, 1024)

    def body(a_ref, b_ref, r_ref, o_ref):
        acc = r_ref[...]
        for u in range(nblk):
            acc = acc + jnp.dot(a_ref[u], b_ref[u], preferred_element_type=F32)
        o_ref[...] = acc

    row = pl.BlockSpec((tm, n), lambda i: (i, 0))
    return _call(
        body, name, (t // tm,),
        [pl.BlockSpec((nblk, tm, kb), lambda i: (0, i, 0)),
         pl.BlockSpec((None, nblk, kb, n), lambda i: (layer, 0, 0, 0)),
         row],
        row,
        jax.ShapeDtypeStruct((t, n), F32),
        vmem_bytes=2 * (_nbytes((nblk, tm, kb), BF16) + _nbytes((nblk, kb, n), BF16)) + 6 * _nbytes((tm, n), F32),
    )(act, w4, res)


def _mm_t_colblock_norm_bwd(name, dz, w_g, layer, x, g, dx_up, blocked_in=False):
    k = w_g.shape[2]
    nb = w_g.shape[3]
    t = x.shape[0]
    tm = _tile(t, 1024)
    per_step = 4 if nb * k <= 512 * 1024 else 2
    n_steps = N_DEV // per_step
    if blocked_in:
        a_spec = pl.BlockSpec((per_step, tm, nb), lambda i, kk: (kk, i, 0))
    else:
        a_spec = pl.BlockSpec((tm, per_step * nb), lambda i, kk: (i, kk))

    def body(a_ref, b_ref, x_ref, g_ref, up_ref, dx_ref, dg_ref, acc_ref):
        i = pl.program_id(0)
        kk = pl.program_id(1)
        part = None
        for u in range(per_step):
            a = a_ref[u] if blocked_in else a_ref[:, u * nb:(u + 1) * nb]
            term = lax.dot_general(a.astype(BF16), b_ref[u].astype(BF16), NT, preferred_element_type=F32)
            part = term if part is None else part + term

        @pl.when(kk == 0)
        def _():
            acc_ref[...] = part

        @pl.when(kk > 0)
        def _():
            acc_ref[...] += part

        @pl.when((i == 0) & (kk == 0))
        def _():
            dg_ref[...] = jnp.zeros_like(dg_ref)

        @pl.when(kk == n_steps - 1)
        def _():
            dy = acc_ref[...]
            xf = x_ref[...]
            r = lax.rsqrt(jnp.mean(xf * xf, axis=-1, keepdims=True) + EPS)
            xhat = xf * r
            dxhat = dy * g_ref[...]
            dg_ref[...] += jnp.sum(dy * xhat, axis=0, keepdims=True)
            dx_ref[...] = up_ref[...] + r * (dxhat - xhat * jnp.mean(dxhat * xhat, axis=-1, keepdims=True))

    row = pl.BlockSpec((tm, k), lambda i, kk: (i, 0))
    vec = pl.BlockSpec((1, k), lambda i, kk: (0, 0))
    dx, dg = _call(
        body, name, (t // tm, n_steps),
        [a_spec, pl.BlockSpec((None, per_step, k, nb), lambda i, kk: (layer, kk, 0, 0)), row, vec, row],
        [row, vec],
        [jax.ShapeDtypeStruct((t, k), F32), jax.ShapeDtypeStruct((1, k), F32)],
        scratch=[pltpu.VMEM((tm, k), F32)],
        vmem_bytes=2 * per_step * (_nbytes((tm, nb), BF16) + _nbytes((k, nb), BF16)) + 10 * _nbytes((tm, k), F32),
    )(dz, w_g, x, g.reshape(1, k), dx_up)
    return dx, dg.reshape(k)


def _ffn_gate_up(name, h, w_g, layer):
    t, k = h.shape
    nb = w_g.shape[3]
    half = N_DEV // 2
    tm = _tile(t, 1024)

    def body(h_ref, wg_ref, wu_ref, dact_ref, act_ref):
        hb = h_ref[...]
        gate = jnp.dot(hb, wg_ref[...], preferred_element_type=F32)
        up = jnp.dot(hb, wu_ref[...], preferred_element_type=F32)
        sig = _sigmoid(gate)
        silu = gate * sig
        dact_ref[0] = (up * (sig * (1.0 + gate * (1.0 - sig)))).astype(BF16)
        dact_ref[1] = silu.astype(BF16)
        act_ref[...] = (silu * up).astype(BF16)

    return _call(
        body, name, (t // tm, half),
        [pl.BlockSpec((tm, k), lambda i, j: (i, 0)),
         pl.BlockSpec((None, None, k, nb), lambda i, j: (layer, j, 0, 0)),
         pl.BlockSpec((None, None, k, nb), lambda i, j: (layer, half + j, 0, 0))],
        [pl.BlockSpec((2, None, tm, nb), lambda i, j: (0, j, i, 0)),
         pl.BlockSpec((None, tm, nb), lambda i, j: (j, i, 0))],
        [jax.ShapeDtypeStruct((2, half, t, nb), BF16), jax.ShapeDtypeStruct((half, t, nb), BF16)],
        vmem_bytes=2 * (_nbytes((tm, k), BF16) + 2 * _nbytes((k, nb), BF16) + 3 * _nbytes((tm, nb), BF16))
        + 8 * _nbytes((tm, nb), F32),
    )(h, w_g, w_g)


def _ffn_down_dx(name, dy, w4, layer, dact):
    t, n = dy.shape
    nblk, kb = w4.shape[1], w4.shape[2]
    tm = _tile(t, 1024)

    def body(dy_ref, w_ref, dact_ref, dgu_ref):
        da = lax.dot_general(dy_ref[...].astype(BF16), w_ref[...], NT, preferred_element_type=F32)
        dgu_ref[0] = (da * dact_ref[0].astype(F32)).astype(BF16)
        dgu_ref[1] = (da * dact_ref[1].astype(F32)).astype(BF16)

    blk = pl.BlockSpec((2, None, tm, kb), lambda i, j: (0, j, i, 0))
    return _call(
        body, name, (t // tm, nblk),
        [pl.BlockSpec((tm, n), lambda i, j: (i, 0)),
         pl.BlockSpec((None, None, kb, n), lambda i, j: (layer, j, 0, 0)),
         blk],
        blk,
        jax.ShapeDtypeStruct((2, nblk, t, kb), BF16),
        vmem_bytes=2 * (_nbytes((tm, n), F32) + _nbytes((kb, n), BF16) + 4 * _nbytes((tm, kb), BF16))
        + 8 * _nbytes((tm, kb), F32),
    )(dy, w4, dact)


def _mm_t_natural(name, dy, w, layer):
    t, n = dy.shape
    k = w.shape[1]
    tm = _tile(t, 1024)
    tk = _tile(k, 1024)
    return _mm(
        name, NT, dy, w, grid=(t // tm, k // tk, 1),
        a_spec=pl.BlockSpec((tm, n), lambda i, j, kk: (i, 0)),
        b_spec=pl.BlockSpec((None, tk, n), lambda i, j, kk: (layer, j, 0)),
        out_shape=jax.ShapeDtypeStruct((t, k), BF16),
        out_spec=pl.BlockSpec((tm, tk), lambda i, j, kk: (i, j)),
        acc_shape=(tm, tk))


def _mm_t_natural_norm_bwd(name, dy, w, layer, x, g, dx_up):
    t, n = dy.shape
    k = w.shape[1]
    tm = _tile(t, 1024)

    def body(a_ref, b_ref, x_ref, g_ref, up_ref, dx_ref, dg_ref):
        @pl.when(pl.program_id(0) == 0)
        def _():
            dg_ref[...] = jnp.zeros_like(dg_ref)

        dh = lax.dot_general(a_ref[...].astype(BF16), b_ref[...], NT, preferred_element_type=F32)
        xf = x_ref[...]
        r = lax.rsqrt(jnp.mean(xf * xf, axis=-1, keepdims=True) + EPS)
        xhat = xf * r
        dxhat = dh * g_ref[...]
        dg_ref[...] += jnp.sum(dh * xhat, axis=0, keepdims=True)
        dx_ref[...] = up_ref[...] + r * (dxhat - xhat * jnp.mean(dxhat * xhat, axis=-1, keepdims=True))

    row = pl.BlockSpec((tm, k), lambda i: (i, 0))
    vec = pl.BlockSpec((1, k), lambda i: (0, 0))
    dx, dg = _call(
        body, name, (t // tm,),
        [pl.BlockSpec((tm, n), lambda i: (i, 0)), pl.BlockSpec((None, k, n), lambda i: (layer, 0, 0)), row, vec, row],
        [row, vec],
        [jax.ShapeDtypeStruct((t, k), F32), jax.ShapeDtypeStruct((1, k), F32)],
        vmem_bytes=2 * (_nbytes((tm, n), dy.dtype) + _nbytes((k, n), BF16)) + 10 * _nbytes((tm, k), F32),
    )(dy, w, x, g.reshape(1, k), dx_up)
    return dx, dg.reshape(k)


def _mm_dw_colblock(name, h, dz, blocked_in=False, transposed=False):
    t, k = h.shape
    nb = dz.shape[2] if blocked_in else dz.shape[1] // N_DEV
    tk = _tile(t, 4096)
    h_spec = pl.BlockSpec((tk, k), lambda i, j, kk: (kk, 0))
    if blocked_in:
        dz_spec = pl.BlockSpec((None, tk, nb), lambda i, j, kk: (j, kk, 0))
    else:
        dz_spec = pl.BlockSpec((tk, nb), lambda i, j, kk: (kk, j))
    rows, cols = (nb, k) if transposed else (k, nb)
    return _mm(
        name, TN, *((dz, h) if transposed else (h, dz)), grid=(1, N_DEV, t // tk),
        a_spec=dz_spec if transposed else h_spec,
        b_spec=h_spec if transposed else dz_spec,
        out_shape=jax.ShapeDtypeStruct((N_DEV, rows, cols), BF16),
        out_spec=pl.BlockSpec((None, rows, cols), lambda i, j, kk: (j, 0, 0)),
        acc_shape=(rows, cols))


def _mm_dw_natural(name, a, dy):
    t, k = a.shape
    n = dy.shape[1]
    tko = _tile(k, 1024)
    tt = _tile(t, 2048)
    out = _mm(
        name, TN, a, dy, grid=(k // tko, 1, t // tt),
        a_spec=pl.BlockSpec((tt, tko), lambda i, j, kk: (kk, i)),
        b_spec=pl.BlockSpec((tt, n), lambda i, j, kk: (kk, 0)),
        out_shape=jax.ShapeDtypeStruct((k, n), BF16),
        out_spec=pl.BlockSpec((tko, n), lambda i, j, kk: (i, 0)),
        acc_shape=(tko, n))
    return out.reshape(N_DEV, k // N_DEV, n)


def _mm_dw_down(name, act, dy):
    nblk, t, kb = act.shape
    n = dy.shape[1]
    tt = _tile(t, 2048)
    out = _mm(
        name, TN, act, dy, grid=(nblk, 1, t // tt),
        a_spec=pl.BlockSpec((None, tt, kb), lambda i, j, kk: (i, kk, 0)),
        b_spec=pl.BlockSpec((tt, n), lambda i, j, kk: (kk, 0)),
        out_shape=jax.ShapeDtypeStruct((nblk, kb, n), BF16),
        out_spec=pl.BlockSpec((None, kb, n), lambda i, j, kk: (i, 0, 0)),
        acc_shape=(kb, n))
    return out.reshape(N_DEV, (nblk * kb) // N_DEV, n)


def _spatial_mask(transposed=False):
    r = lax.broadcasted_iota(jnp.int32, (A_CHUNK, A_CHUNK), 0) // CHUNK
    c = lax.broadcasted_iota(jnp.int32, (A_CHUNK, A_CHUNK), 1) // CHUNK
    return c >= r if transposed else r >= c


def _sgu_tile(t):
    return _tile(t, 2 * A_CHUNK)


def _sgu_fwd(zpre, g_sgu, w_sp, b_full, name):
    t, f2 = zpre.shape
    f = f2 // 2
    gd = f // A_GROUPS
    tm = _sgu_tile(t)

    def body(z_ref, g_ref, w_ref, b_ref, p_ref, zs_ref, dg_ref):
        mask = _spatial_mask()
        wm = [jnp.where(mask, w_ref[g], 0.0).astype(BF16) for g in range(A_GROUPS)]
        for c in range(tm // A_CHUNK):
            rows = pl.ds(c * A_CHUNK, A_CHUNK)
            z, dgelu = _gelu_and_grad(z_ref[rows, :].astype(F32))
            zs_ref[rows, :] = z.astype(BF16)
            dg_ref[rows, :] = dgelu.astype(BF16)
            u = z[:, :f]
            v0 = z[:, f:]
            r = lax.rsqrt(jnp.mean(v0 * v0, axis=-1, keepdims=True) + EPS)
            v1 = (v0 * r * g_ref[...]).astype(BF16)
            for g in range(A_GROUPS):
                cols = slice(g * gd, (g + 1) * gd)
                v2 = jnp.dot(wm[g], v1[:, cols], preferred_element_type=F32) + b_ref[:, cols]
                p_ref[rows, cols] = (u[:, cols] * v2).astype(BF16)

    return _call(
        body, name, (t // tm,),
        [pl.BlockSpec((tm, f2), lambda i: (i, 0)),
         pl.BlockSpec((1, f), lambda i: (0, 0)),
         pl.BlockSpec((A_GROUPS, A_CHUNK, A_CHUNK), lambda i: (0, 0, 0)),
         pl.BlockSpec((A_CHUNK, f), lambda i: (0, 0))],
        [pl.BlockSpec((tm, f), lambda i: (i, 0)), pl.BlockSpec((tm, f2), lambda i: (i, 0)),
         pl.BlockSpec((tm, f2), lambda i: (i, 0))],
        [jax.ShapeDtypeStruct((t, f), BF16), jax.ShapeDtypeStruct((t, f2), BF16), jax.ShapeDtypeStruct((t, f2), BF16)],
        vmem_bytes=6 * _nbytes((tm, f2), BF16) + 2 * _nbytes((tm, f), BF16) + 8 * _nbytes((A_CHUNK, f2), F32),
    )(zpre, g_sgu.reshape(1, f), w_sp, b_full)


def _sgu_bwd(zs, dgs, dp, g_sgu, w_sp, w_sp_t, b_full, name):
    t, f2 = zs.shape
    f = f2 // 2
    gd = f // A_GROUPS
    tm = _sgu_tile(t)
    n_steps = t // tm

    def body(z_ref, dgelu_ref, dp_ref, g_ref, w_ref, wt_ref, b_ref, dz_ref, dw_ref, db_ref, dg_ref, dv1_ref, dbf_ref):
        step = pl.program_id(0)

        @pl.when(step == 0)
        def _():
            dw_ref[...] = jnp.zeros_like(dw_ref)
            dg_ref[...] = jnp.zeros_like(dg_ref)
            dbf_ref[...] = jnp.zeros_like(dbf_ref)

        mask = _spatial_mask()
        mask_t = _spatial_mask(transposed=True)
        wm = [jnp.where(mask, w_ref[g], 0.0).astype(BF16) for g in range(A_GROUPS)]
        wmt = [jnp.where(mask_t, wt_ref[g], 0.0).astype(BF16) for g in range(A_GROUPS)]
        gain = g_ref[...]
        for c in range(tm // A_CHUNK):
            rows = pl.ds(c * A_CHUNK, A_CHUNK)
            z = z_ref[rows, :].astype(F32)
            dgelu = dgelu_ref[rows, :].astype(F32)
            u = z[:, :f]
            v0 = z[:, f:]
            r = lax.rsqrt(jnp.mean(v0 * v0, axis=-1, keepdims=True) + EPS)
            xhat = v0 * r
            v1 = (xhat * gain).astype(BF16)
            dpf = dp_ref[rows, :].astype(F32)
            for g in range(A_GROUPS):
                cols = slice(g * gd, (g + 1) * gd)
                v1g = v1[:, cols]
                v2 = jnp.dot(wm[g], v1g, preferred_element_type=F32) + b_ref[:, cols]
                dpg = dpf[:, cols]
                dz_ref[rows, cols] = (dpg * v2 * dgelu[:, cols]).astype(BF16)
                dv2 = dpg * u[:, cols]
                dbf_ref[:, cols] += dv2
                dv2b = dv2.astype(BF16)
                dwg = lax.dot_general(dv2b, v1g, NT, preferred_element_type=F32)
                dw_ref[g] += jnp.where(mask, dwg, 0.0)
                dv1_ref[:, cols] = jnp.dot(wmt[g], dv2b, preferred_element_type=F32)
            dv1 = dv1_ref[...]
            dxhat = dv1 * gain
            dg_ref[...] += jnp.sum(dv1 * xhat, axis=0, keepdims=True)
            dv0 = r * (dxhat - xhat * jnp.mean(dxhat * xhat, axis=-1, keepdims=True))
            dz_ref[rows, pl.ds(f, f)] = (dv0 * dgelu[:, f:]).astype(BF16)

        @pl.when(step == n_steps - 1)
        def _():
            for g in range(A_GROUPS):
                db_ref[g] = jnp.sum(dbf_ref[:, g * gd:(g + 1) * gd], axis=1, keepdims=True)

    wspec = pl.BlockSpec((A_GROUPS, A_CHUNK, A_CHUNK), lambda i: (0, 0, 0))
    dz, dw, db, dg = _call(
        body, name, (n_steps,),
        [pl.BlockSpec((tm, f2), lambda i: (i, 0)),
         pl.BlockSpec((tm, f2), lambda i: (i, 0)),
         pl.BlockSpec((tm, f), lambda i: (i, 0)),
         pl.BlockSpec((1, f), lambda i: (0, 0)),
         wspec, wspec,
         pl.BlockSpec((A_CHUNK, f), lambda i: (0, 0))],
        [pl.BlockSpec((tm, f2), lambda i: (i, 0)),
         wspec,
         pl.BlockSpec((A_GROUPS, A_CHUNK, 1), lambda i: (0, 0, 0)),
         pl.BlockSpec((1, f), lambda i: (0, 0))],
        [jax.ShapeDtypeStruct((t, f2), BF16),
         jax.ShapeDtypeStruct((A_GROUPS, A_CHUNK, A_CHUNK), F32),
         jax.ShapeDtypeStruct((A_GROUPS, A_CHUNK, 1), F32),
         jax.ShapeDtypeStruct((1, f), F32)],
        scratch=[pltpu.VMEM((A_CHUNK, f), F32), pltpu.VMEM((A_CHUNK, f), F32)],
        vmem_bytes=6 * _nbytes((tm, f2), BF16) + 2 * _nbytes((tm, f), BF16) + 12 * _nbytes((A_CHUNK, f2), F32),
    )(zs, dgs, dp, g_sgu.reshape(1, f), w_sp, w_sp_t, b_full)
    return dz, dw, db.reshape(A_GROUPS, A_CHUNK), dg.reshape(f)


def _pair_valid(qi, col):
    qc = qi // CHUNK
    kc = col // CHUNK
    return (kc >= qc) & (kc <= qc + N_LEFT_CHUNKS)


def _diagonal_onehot():
    e = lax.broadcasted_iota(jnp.int32, (REL_PAD, DIAGONALS), 1)
    idx = jnp.clip(PAIR_BAND - 1 - e, -MAX_REL, MAX_REL) + MAX_REL
    r = lax.broadcasted_iota(jnp.int32, (REL_PAD, DIAGONALS), 0)
    return jnp.where(r == idx, 1.0, 0.0).astype(BF16)


def _bias_build(table, name):
    h = table.shape[0]
    tab = jnp.pad(table, ((0, 0), (0, REL_PAD - N_REL)))

    def body(t_ref, o_ref):
        oh = _diagonal_onehot()
        diag = jnp.zeros((h, DIAGONALS), F32)
        for piece in _split3(t_ref[...]):
            diag += jnp.dot(piece, oh, preferred_element_type=F32)
        col = lax.broadcasted_iota(jnp.int32, (h, PAIR_BAND), 1)
        for qi in range(PAIR_ROWS):
            row = pltpu.roll(diag, (qi - (PAIR_ROWS - 1)) % DIAGONALS, 1)[:, :PAIR_BAND]
            o_ref[qi] = jnp.where(_pair_valid(qi, col), row, NEG_INF)

    out = _call(
        body, name, (1,),
        [pl.BlockSpec((h, REL_PAD), lambda i: (0, 0))],
        pl.BlockSpec((PAIR_ROWS, h, PAIR_BAND), lambda i: (0, 0, 0)),
        jax.ShapeDtypeStruct((PAIR_ROWS, h, PAIR_BAND), F32),
        vmem_bytes=4 * _nbytes((PAIR_ROWS, h, PAIR_BAND), F32),
    )(tab)
    return jnp.transpose(out, (1, 0, 2))


def _bias_block(pair_bias):
    rest = K_BLOCK - PAIR_BAND
    return jnp.concatenate(
        [jnp.pad(pair_bias, ((0, 0), (0, 0), (p * PAIR_ROWS, rest - p * PAIR_ROWS)), constant_values=NEG_INF)
         for p in range(PAIRS_PER_BLOCK)], axis=1)


def _bias_grad(dbias, name):
    h = dbias.shape[0]
    db_t = jnp.transpose(dbias, (1, 0, 2))

    def body(d_ref, o_ref):
        diag = jnp.zeros((h, DIAGONALS), F32)
        for qi in range(PAIR_ROWS):
            diag += pltpu.roll(d_ref[qi], PAIR_ROWS - 1 - qi, 1)
        oh = _diagonal_onehot()
        acc = jnp.zeros((h, REL_PAD), F32)
        for piece in _split3(diag):
            acc += lax.dot_general(piece, oh, NT, preferred_element_type=F32)
        o_ref[...] = acc

    out = _call(
        body, name, (1,),
        [pl.BlockSpec((PAIR_ROWS, h, DIAGONALS), lambda i: (0, 0, 0))],
        pl.BlockSpec((h, REL_PAD), lambda i: (0, 0)),
        jax.ShapeDtypeStruct((h, REL_PAD), F32),
        vmem_bytes=4 * _nbytes((PAIR_ROWS, h, DIAGONALS), F32),
    )(db_t)
    return out[:, :N_REL]


def _head_masks():
    lane = lax.broadcasted_iota(jnp.int32, (Q_BLOCK, HEAD_PAIR), 1)
    return lane < HEAD_DIM, lane >= HEAD_DIM


def _block_scores(qm, kb, bias, valid):
    s = lax.dot_general(qm, kb, NT, preferred_element_type=F32) + bias
    return s if valid is None else jnp.where(valid, s, NEG_INF)


def _softmax_rows(s):
    e = jnp.exp(s - jnp.max(s, axis=-1, keepdims=True))
    return e * (1.0 / jnp.sum(e, axis=-1, keepdims=True))


def _padded_then_plain(step, n_blocks):
    n_padded = min(LEFT // Q_BLOCK, n_blocks)
    lax.fori_loop(0, n_padded, lambda j, c: step(j, c, True), 0, unroll=True)
    lax.fori_loop(n_padded, n_blocks, lambda j, c: step(j, c, False), 0, unroll=ATTN_UNROLL)


def _attn_fwd(q, kvpad, bias, name):
    t, d = q.shape
    n_pairs = d // HEAD_PAIR
    n_blocks = t // Q_BLOCK

    def body(q_ref, k_ref, v_ref, b_ref, o_ref):
        masks = _head_masks()
        key = lax.broadcasted_iota(jnp.int32, (Q_BLOCK, K_BLOCK), 1)

        def step(j, carry, padded):
            r0 = pl.multiple_of(j * Q_BLOCK, Q_BLOCK)
            q2 = q_ref[pl.ds(r0, Q_BLOCK), :].astype(F32)
            kb = k_ref[pl.ds(r0, K_BLOCK), :]
            vb = v_ref[pl.ds(r0, K_BLOCK), :]
            valid = key >= LEFT - j * Q_BLOCK if padded else None
            scores = [_block_scores(jnp.where(masks[a], q2, 0.0).astype(BF16), kb, b_ref[a], valid) for a in range(2)]
            probs = [_softmax_rows(s).astype(BF16) for s in scores]
            outs = [jnp.dot(p, vb, preferred_element_type=F32) for p in probs]
            o_ref[pl.ds(r0, Q_BLOCK), :] = jnp.where(masks[0], outs[0], outs[1]).astype(BF16)
            return carry

        _padded_then_plain(step, n_blocks)

    return _call(
        body, name, (n_pairs,),
        [pl.BlockSpec((t, HEAD_PAIR), lambda p: (0, p)),
         pl.BlockSpec((LEFT + t, HEAD_PAIR), lambda p: (0, p)),
         pl.BlockSpec((LEFT + t, HEAD_PAIR), lambda p: (0, n_pairs + p)),
         pl.BlockSpec((2, Q_BLOCK, K_BLOCK), lambda p: (p, 0, 0))],
        pl.BlockSpec((t, HEAD_PAIR), lambda p: (0, p)),
        jax.ShapeDtypeStruct((t, d), BF16),
        vmem_bytes=8 * _nbytes((LEFT + t, HEAD_PAIR), BF16) + 12 * _nbytes((2, Q_BLOCK, K_BLOCK), F32),
    )(q, kvpad, kvpad, bias)


def _attn_bwd(q, kvpad, bias, o, do, dk_in, dv_in, name):
    t, d = q.shape
    n_pairs = d // HEAD_PAIR
    n_blocks = t // Q_BLOCK
    has_in = dk_in is not None

    def body(*refs):
        refs = list(refs)
        q_ref, k_ref, v_ref, b_ref, o_ref, do_ref = refs[:6]
        refs = refs[6:]
        if has_in:
            dki_ref, dvi_ref = refs[:2]
            refs = refs[2:]
        dq_ref, dk_ref, dv_ref, db_ref = refs
        masks = _head_masks()
        key = lax.broadcasted_iota(jnp.int32, (Q_BLOCK, K_BLOCK), 1)
        if has_in:
            dk_ref[...] = dki_ref[...]
            dv_ref[...] = dvi_ref[...]
        else:
            dk_ref[...] = jnp.zeros_like(dk_ref)
            dv_ref[...] = jnp.zeros_like(dv_ref)
        db_ref[...] = jnp.zeros_like(db_ref)

        def step(j, carry, padded):
            r0 = pl.multiple_of(j * Q_BLOCK, Q_BLOCK)
            q2 = q_ref[pl.ds(r0, Q_BLOCK), :].astype(F32)
            do2 = do_ref[pl.ds(r0, Q_BLOCK), :].astype(F32)
            do_o = do2 * o_ref[pl.ds(r0, Q_BLOCK), :].astype(F32)
            kb = k_ref[pl.ds(r0, K_BLOCK), :]
            vb = v_ref[pl.ds(r0, K_BLOCK), :]
            valid = key >= LEFT - j * Q_BLOCK if padded else None
            heads = range(2)
            qms = [jnp.where(masks[a], q2, 0.0).astype(BF16) for a in heads]
            doms = [jnp.where(masks[a], do2, 0.0).astype(BF16) for a in heads]
            scores = [_block_scores(qms[a], kb, b_ref[a], valid) for a in heads]
            dps = [lax.dot_general(doms[a], vb, NT, preferred_element_type=F32) for a in heads]
            ps = [_softmax_rows(s) for s in scores]
            rows = [jnp.sum(jnp.where(masks[a], do_o, 0.0), axis=-1, keepdims=True) for a in heads]
            dss = [ps[a] * (dps[a] - rows[a]) for a in heads]
            for a in heads:
                for pair in range(PAIRS_PER_BLOCK):
                    lo = pair * PAIR_ROWS
                    db_ref[a, :, pl.ds(0, PAIR_BAND)] += dss[a][lo:lo + PAIR_ROWS, lo:lo + PAIR_BAND]
            dsbs = [ds.astype(BF16) for ds in dss]
            pbs = [p.astype(BF16) for p in ps]
            dqs = [jnp.dot(dsbs[a], kb, preferred_element_type=F32) for a in heads]
            dk_acc = sum(lax.dot_general(dsbs[a], qms[a], TN, preferred_element_type=F32) for a in heads)
            dv_acc = sum(lax.dot_general(pbs[a], doms[a], TN, preferred_element_type=F32) for a in heads)
            dq = jnp.where(masks[0], dqs[0], dqs[1]) * ATTN_SCALE
            dq_ref[pl.ds(r0, Q_BLOCK), :] = dq.astype(BF16)
            dk_ref[pl.ds(r0, K_BLOCK), :] += dk_acc
            dv_ref[pl.ds(r0, K_BLOCK), :] += dv_acc
            return carry

        _padded_then_plain(step, n_blocks)

    q_spec = pl.BlockSpec((t, HEAD_PAIR), lambda p: (0, p))
    kv_spec = pl.BlockSpec((LEFT + t, HEAD_PAIR), lambda p: (0, p))
    operands = [q, kvpad, kvpad, bias, o, do]
    in_specs = [q_spec, kv_spec, pl.BlockSpec((LEFT + t, HEAD_PAIR), lambda p: (0, n_pairs + p)),
                pl.BlockSpec((2, Q_BLOCK, K_BLOCK), lambda p: (p, 0, 0)), q_spec, q_spec]
    aliases = None
    if has_in:
        operands += [dk_in, dv_in]
        in_specs += [kv_spec, kv_spec]
        aliases = {6: 1, 7: 2}
    return _call(
        body, name, (n_pairs,),
        in_specs,
        [q_spec, kv_spec, kv_spec, pl.BlockSpec((2, PAIR_ROWS, DIAGONALS), lambda p: (p, 0, 0))],
        [jax.ShapeDtypeStruct((t, d), BF16),
         jax.ShapeDtypeStruct((LEFT + t, d), F32),
         jax.ShapeDtypeStruct((LEFT + t, d), F32),
         jax.ShapeDtypeStruct((d // HEAD_DIM, PAIR_ROWS, DIAGONALS), F32)],
        vmem_bytes=10 * _nbytes((LEFT + t, HEAD_PAIR), BF16) + 8 * _nbytes((LEFT + t, HEAD_PAIR), F32)
        + 16 * _nbytes((2, Q_BLOCK, K_BLOCK), F32),
        aliases=aliases,
    )(*operands)


def _loss_head(x, g, target, name):
    t, d = x.shape
    tm = _tile(t, 512)

    def body(x_ref, g_ref, t_ref, dx_ref, loss_ref, dg_ref):
        @pl.when(pl.program_id(0) == 0)
        def _():
            loss_ref[...] = jnp.zeros_like(loss_ref)
            dg_ref[...] = jnp.zeros_like(dg_ref)

        xf = x_ref[...]
        r = lax.rsqrt(jnp.mean(xf * xf, axis=-1, keepdims=True) + EPS)
        xhat = xf * r
        diff = xhat * g_ref[...] - t_ref[...]
        row_loss = jnp.mean(diff * diff, axis=-1, keepdims=True)
        loss_ref[...] += 0.5 * jnp.sum(row_loss, axis=0, keepdims=True)
        dy = diff * (1.0 / d)
        dg_ref[...] += jnp.sum(dy * xhat, axis=0, keepdims=True)
        dxhat = dy * g_ref[...]
        dx_ref[...] = r * (dxhat - xhat * jnp.mean(dxhat * xhat, axis=-1, keepdims=True))

    row = pl.BlockSpec((tm, d), lambda i: (i, 0))
    vec = pl.BlockSpec((1, d), lambda i: (0, 0))
    dx, loss, dg = _call(
        body, name, (t // tm,),
        [row, vec, row],
        [row, pl.BlockSpec((1, 1), lambda i: (0, 0)), vec],
        [jax.ShapeDtypeStruct((t, d), F32), jax.ShapeDtypeStruct((1, 1), F32), jax.ShapeDtypeStruct((1, d), F32)],
        vmem_bytes=10 * _nbytes((tm, d), F32),
    )(x, g.reshape(1, d), target)
    return dx, loss[0, 0], dg.reshape(d)


def _adamw_store(g, w_ref, m_ref, v_ref, g_ref, d_ref, nm_ref, nv_ref):
    c1 = 1.0 / (1.0 - ADAM_B1 ** ADAM_STEP)
    c2 = 1.0 / (1.0 - ADAM_B2 ** ADAM_STEP)
    nm = ADAM_B1 * m_ref[...] + (1.0 - ADAM_B1) * g
    nv = ADAM_B2 * v_ref[...] + (1.0 - ADAM_B2) * (g * g)
    g_ref[...] = g
    nm_ref[...] = nm
    nv_ref[...] = nv
    d_ref[...] = -ADAM_LR * ((nm * c1) / (jnp.sqrt(nv * c2) + ADAM_EPS) + ADAM_WD * w_ref[...])


def _adamw_layer(recv, own, w, m, v, layer, prev, me, name):
    n_src, r, c = recv.shape
    tr = _row_tile(r, max(BF16_SUBLANES, ADAMW_BLOCK_ELEMS // c), BF16_SUBLANES)
    first = prev is None

    def body(me_ref, recv_ref, own_ref, w_ref, m_ref, v_ref, *rest):
        mine = me_ref[0]
        own_part = own_ref[...].astype(F32)
        g = None
        for s in range(n_src):
            part = jnp.where(mine == s, own_part, recv_ref[s].astype(F32))
            g = part if g is None else g + part
        _adamw_store(g, w_ref, m_ref, v_ref, *rest[-4:])

    blk = pl.BlockSpec((None, tr, c), lambda i, me_ref: (layer, i, 0))
    any_spec = pl.BlockSpec(memory_space=pl.ANY)
    out = jax.ShapeDtypeStruct(w.shape, F32)
    operands = [me, recv, own, w, m, v] + ([] if first else list(prev))
    vmem = 2 * _nbytes((n_src + 1, tr, c), BF16) + 18 * _nbytes((tr, c), F32)
    return pl.pallas_call(
        body,
        name=name,
        grid_spec=pltpu.PrefetchScalarGridSpec(
            num_scalar_prefetch=1,
            grid=(r // tr,),
            in_specs=[pl.BlockSpec((n_src, tr, c), lambda i, me_ref: (0, i, 0)),
                      pl.BlockSpec((None, tr, c), lambda i, me_ref: (me_ref[0], i, 0)),
                      blk, blk, blk] + ([] if first else [any_spec] * 4),
            out_specs=[blk, blk, blk, blk],
        ),
        out_shape=[out, out, out, out],
        input_output_aliases={} if first else {6 + j: j for j in range(4)},
        compiler_params=pltpu.CompilerParams(
            dimension_semantics=("arbitrary",),
            vmem_limit_bytes=int(min(max(VMEM_FLOOR_BYTES, vmem * 5 // 4), VMEM_CEIL_BYTES))),
    )(*operands)


def _adamw(parts, w, m, v, name):
    n_layers, n_src, r, c = parts.shape
    mult = BF16_SUBLANES if parts.dtype == BF16 else F32_SUBLANES
    tr = _row_tile(r, max(mult, ADAMW_BLOCK_ELEMS // c), mult)

    def body(p_ref, w_ref, m_ref, v_ref, g_ref, d_ref, nm_ref, nv_ref):
        g = p_ref[0].astype(F32)
        for s in range(1, n_src):
            g = g + p_ref[s].astype(F32)
        _adamw_store(g, w_ref, m_ref, v_ref, g_ref, d_ref, nm_ref, nv_ref)

    blk = pl.BlockSpec((None, tr, c), lambda l, i: (l, i, 0))
    out = jax.ShapeDtypeStruct((n_layers, r, c), F32)
    return _call(
        body, name, (n_layers, r // tr),
        [pl.BlockSpec((None, n_src, tr, c), lambda l, i: (l, 0, i, 0)), blk, blk, blk],
        [blk, blk, blk, blk],
        [out, out, out, out],
        vmem_bytes=2 * _nbytes((n_src, tr, c), parts.dtype) + 18 * _nbytes((tr, c), F32),
    )(parts, w, m, v)


def _ordered_sum(parts, name):
    n_src, r, c = parts.shape

    def body(p_ref, o_ref):
        acc = p_ref[0]
        for s in range(1, n_src):
            acc = acc + p_ref[s]
        o_ref[...] = acc

    return _call(
        body, name, (1,),
        [pl.BlockSpec((n_src, r, c), lambda i: (0, 0, 0))],
        pl.BlockSpec((r, c), lambda i: (0, 0)),
        jax.ShapeDtypeStruct((r, c), F32),
        vmem_bytes=4 * _nbytes((n_src, r, c), F32),
    )(parts)


def _position():
    return lax.axis_index("x"), lax.axis_index("y"), lax.axis_index("c")


def _linear(p):
    return 4 * p[0] + 2 * p[1] + p[2]


def _all_gather(shards, name):
    n = len(shards)

    def body(*refs):
        ins, outs = refs[:n], refs[n:2 * n]
        send_sems, recv_sems, local_sems = refs[2 * n:]
        x, y, c = _position()
        me, sibling = (x, y, c), (x, y, 1 - c)
        chips = [(1 - x, y), (x, 1 - y), (1 - x, 1 - y)]

        def slab(t, p):
            return outs[t].at[:, _linear(p)]

        def copy(t, k, block, to, src=None):
            return pltpu.make_async_remote_copy(
                src_ref=slab(t, block) if src is None else src,
                dst_ref=slab(t, block),
                send_sem=send_sems.at[t, k],
                recv_sem=recv_sems.at[t, k],
                device_id=to,
                device_id_type=MESH,
            )

        started = []
        for t in range(n):
            mine = pltpu.make_async_copy(ins[t], slab(t, me), local_sems.at[t])
            mine.start()
            started.append(mine)
        sends = []
        for t in range(n):
            first = [copy(t, 0, me, sibling, src=ins[t])]
            first += [copy(t, 1 + j, me, (*chip, c), src=ins[t]) for j, chip in enumerate(chips)]
            for cp in first:
                cp.start()
            sends += first
        for t in range(n):
            for j, chip in enumerate(chips):
                copy(t, 1 + j, (*chip, c), me).wait_recv()
                passed = copy(t, 4 + j, (*chip, c), sibling)
                passed.start()
                sends.append(passed)
        for t in range(n):
            copy(t, 0, sibling, me).wait_recv()
            for j, chip in enumerate(chips):
                copy(t, 4 + j, (*chip, 1 - c), me).wait_recv()
        for cp in sends:
            cp.wait_send()
        for mine in started:
            mine.wait()

    out_shape = [jax.ShapeDtypeStruct((s.shape[0], N_DEV) + s.shape[1:], s.dtype) for s in shards]
    return pl.pallas_call(
        body,
        name=name,
        in_specs=[HBM_SPEC] * n,
        out_specs=[HBM_SPEC] * n,
        out_shape=out_shape,
        scratch_shapes=[
            pltpu.SemaphoreType.DMA((n, N_DEV - 1)),
            pltpu.SemaphoreType.DMA((n, N_DEV - 1)),
            pltpu.SemaphoreType.DMA((n,)),
        ],
    )(*shards)


def _exchange(blocks, name):
    n = len(blocks)

    def body(*refs):
        ins, outs = refs[:n], refs[n:2 * n]
        send_sems, recv_sems, local_sems = refs[2 * n:]
        x, y, c = _position()
        me = _linear((x, y, c))
        flips = [(fx, fy, fc) for fx in (0, 1) for fy in (0, 1) for fc in (0, 1)][1:]

        def peer_of(flip):
            fx, fy, fc = flip
            return (1 - x if fx else x, 1 - y if fy else y, 1 - c if fc else c)

        def copy(t, k, peer):
            return pltpu.make_async_remote_copy(
                src_ref=ins[t].at[:, _linear(peer)],
                dst_ref=outs[t].at[:, me],
                send_sem=send_sems.at[t, k],
                recv_sem=recv_sems.at[t, k],
                device_id=peer,
                device_id_type=MESH,
            )

        def arrival(t, k, peer):
            return pltpu.make_async_remote_copy(
                src_ref=ins[t].at[:, _linear(peer)],
                dst_ref=outs[t].at[:, _linear(peer)],
                send_sem=send_sems.at[t, k],
                recv_sem=recv_sems.at[t, k],
                device_id=peer,
                device_id_type=MESH,
            )

        own = []
        for t in range(n):
            cp = pltpu.make_async_copy(ins[t].at[:, me], outs[t].at[:, me], local_sems.at[t])
            cp.start()
            own.append(cp)
        sends = []
        for t in range(n):
            for k, flip in enumerate(flips):
                cp = copy(t, k, peer_of(flip))
                cp.start()
                sends.append(cp)
        for t in range(n):
            for k, flip in enumerate(flips):
                arrival(t, k, peer_of(flip)).wait_recv()
        for cp in sends:
            cp.wait_send()
        for cp in own:
            cp.wait()

    out_shape = [jax.ShapeDtypeStruct(b.shape, b.dtype) for b in blocks]
    return pl.pallas_call(
        body,
        name=name,
        in_specs=[HBM_SPEC] * n,
        out_specs=[HBM_SPEC] * n,
        out_shape=out_shape,
        scratch_shapes=[
            pltpu.SemaphoreType.DMA((n, N_DEV - 1)),
            pltpu.SemaphoreType.DMA((n, N_DEV - 1)),
            pltpu.SemaphoreType.DMA((n,)),
        ],
    )(*blocks)


def _peers():
    x, y, c = _position()
    flips = [(fx, fy, fc) for fx in (0, 1) for fy in (0, 1) for fc in (0, 1)][1:]
    return [(1 - x if fx else x, 1 - y if fy else y, 1 - c if fc else c) for fx, fy, fc in flips]


SIBLING, OTHER_CHIPS = (0,), (1, 3, 5)
COPY_PEERS = {"gather": tuple(range(N_DEV - 1)), "exchange": tuple(range(N_DEV - 1)),
              "chips": SIBLING + OTHER_CHIPS, "forward": OTHER_CHIPS}


def _split_copy(kind, src_ref, land_ref, k, send_sem, recv_sem, starting):
    peers = _peers()
    peer = peers[SIBLING[0]] if kind == "forward" else peers[k]
    me = _linear(_position())
    if kind == "forward":
        slab = _linear(peers[k]) if starting else 0
        src, dst = land_ref.at[slab], land_ref.at[slab]
    elif kind == "exchange":
        src, dst = src_ref.at[_linear(peer) if starting else 0], land_ref.at[me if starting else 0]
    else:
        src, dst = src_ref, land_ref.at[me if starting else 0]
    return pltpu.make_async_remote_copy(src_ref=src, dst_ref=dst, send_sem=send_sem, recv_sem=recv_sem,
                                        device_id=peer, device_id_type=MESH)


def _split_start(groups, carry, name):
    arrays = [a for _, srcs, lands in groups for a in list(srcs) + list(lands)] + [carry]

    def body(*refs):
        ins, sems = refs[:len(arrays)], refs[len(arrays):len(arrays) + 2 * len(groups)]
        at = 0
        for g, (kind, srcs, lands) in enumerate(groups):
            src_refs, land_refs = ins[at:at + len(srcs)], ins[at + len(srcs):at + len(srcs) + len(lands)]
            at += len(srcs) + len(lands)
            peers = COPY_PEERS[kind]
            for t in range(len(lands)):
                for slot, k in enumerate(peers):
                    sem = t * len(peers) + slot
                    _split_copy(kind, src_refs[t] if srcs else None, land_refs[t], k,
                                sems[2 * g].at[sem], sems[2 * g + 1].at[sem], True).start()

    sem_shapes = [pltpu.SemaphoreType.DMA((len(lands) * len(COPY_PEERS[kind]),))
                  for kind, _, lands in groups for _ in range(2)]
    out = pl.pallas_call(
        body,
        name=name,
        in_specs=[HBM_SPEC] * len(arrays),
        out_specs=[SEM_SPEC] * len(sem_shapes) + [HBM_SPEC] * len(arrays),
        out_shape=sem_shapes + [pltpu.HBM(a.shape, a.dtype) for a in arrays],
        input_output_aliases={i: len(sem_shapes) + i for i in range(len(arrays))},
        compiler_params=pltpu.CompilerParams(has_side_effects=pltpu.SideEffectType.DATAFLOW_SIDE_EFFECTING),
    )(*[pltpu.with_memory_space_constraint(a, pltpu.HBM) for a in arrays])
    sems, thru = out[:len(sem_shapes)], out[len(sem_shapes):]
    started, at = [], 0
    for g, (kind, srcs, lands) in enumerate(groups):
        n_s, n_l = len(srcs), len(lands)
        started.append((kind, sems[2 * g], sems[2 * g + 1], thru[at:at + n_s], thru[at + n_s:at + n_s + n_l]))
        at += n_s + n_l
    return started, thru[-1]


def _split_wait(started, after, name):
    kind, send_sems, recv_sems, srcs, lands = started
    n_s, n_l = len(srcs), len(lands)
    peers = COPY_PEERS[kind]

    def body(*refs):
        src_refs, land_refs = refs[:n_s], refs[n_s:n_s + n_l]
        send_ref, recv_ref = refs[n_s + n_l], refs[n_s + n_l + 1]
        for t in range(n_l):
            for slot, k in enumerate(peers):
                sem = t * len(peers) + slot
                copy = _split_copy(kind, src_refs[t] if n_s else None, land_refs[t], k,
                                   send_ref.at[sem], recv_ref.at[sem], False)
                copy.wait_send()
                copy.wait_recv()

    arrays = list(srcs) + list(lands)
    out = pl.pallas_call(
        body,
        name=name,
        in_specs=[HBM_SPEC] * len(arrays) + [SEM_SPEC, SEM_SPEC, pl.BlockSpec(memory_space=pl.ANY)],
        out_specs=[HBM_SPEC] * len(arrays),
        out_shape=[pltpu.HBM(a.shape, a.dtype) for a in arrays],
        input_output_aliases={i: i for i in range(len(arrays))},
        compiler_params=pltpu.CompilerParams(has_side_effects=pltpu.SideEffectType.DATAFLOW_SIDE_EFFECTING),
    )(*arrays, send_sems, recv_sems, after)
    return out[:n_s], out[n_s:]


def _pack(arrays, row_multiple):
    flat = jnp.concatenate([a.reshape(-1) for a in arrays])
    quantum = row_multiple * FLAT_LANES
    padded = -(-flat.shape[0] // quantum) * quantum
    return jnp.pad(flat, (0, padded - flat.shape[0])).reshape(-1, FLAT_LANES)


def _unpack(flat, like):
    flat = flat.reshape(-1)
    out, at = [], 0
    for a in like:
        size = math.prod(a.shape)
        out.append(flat[at:at + size].reshape(a.shape))
        at += size
    return out


def kernel(x, a_norm, a_w_in, a_sgu_norm, a_w_spatial, a_b_spatial, a_w_out, kv_norm, w_kv, b_norm, b_w_q, b_rel_bias, b_w_o, ffn_norm, ffn_w_gate_up, ffn_w_down, final_norm, loss_target, m_a_norm, m_a_w_in, m_a_sgu_norm, m_a_w_spatial, m_a_b_spatial, m_a_w_out, m_kv_norm, m_w_kv, m_b_norm, m_b_w_q, m_b_rel_bias, m_b_w_o, m_ffn_norm, m_ffn_w_gate_up, m_ffn_w_down, m_final_norm, v_a_norm, v_a_w_in, v_a_sgu_norm, v_a_w_spatial, v_a_b_spatial, v_a_w_out, v_kv_norm, v_w_kv, v_b_norm, v_b_w_q, v_b_rel_bias, v_b_w_o, v_ffn_norm, v_ffn_w_gate_up, v_ffn_w_down, v_final_norm):
    xs = x[0]
    target = loss_target[0]
    t, d = xs.shape
    n_a = a_w_in.shape[0]
    n_b = b_w_q.shape[0]
    depth = ffn_w_gate_up.shape[0]
    f_a = a_w_out.shape[1] * N_DEV
    gd = f_a // A_GROUPS
    nb_ffn = ffn_w_gate_up.shape[2]
    me = _linear(_position())

    small_rows = -(-(a_norm.size + a_sgu_norm.size) // (8 * 128)) * 8
    small = jnp.pad(jnp.concatenate([a_norm.reshape(-1), a_sgu_norm.reshape(-1)]),
                    (0, small_rows * 128 - a_norm.size - a_sgu_norm.size)).reshape(1, small_rows, 128)

    def shard(w, layer=None):
        return (w if layer is None else w[layer]).astype(BF16)

    stages = []
    for layer in range(depth):
        if layer == 0:
            stages += [("a0", [shard(a_w_in, 0)]), ("a0_out", [shard(a_w_out, 0)])]
        elif layer < n_a:
            stages.append((f"a{layer}", [shard(a_w_in, layer), shard(a_w_out, layer)]))
        else:
            i = layer - n_a
            shared = [shard(w_kv)] if i == 0 else []
            stages.append((f"b{i}", shared + [shard(b_w_q, i), shard(b_w_o, i)]))
        stages.append((f"f{layer}", [shard(ffn_w_gate_up, layer), shard(ffn_w_down, layer)]))
    first = _all_gather([s[None] for s in stages[0][1]] + [small], "gather_first")
    gathered = {stages[0][0]: [g[0] for g in first[:-1]]}
    small_g = first[-1].reshape(N_DEV, -1)
    a_norm_full = small_g[:, :a_norm.size].reshape(N_DEV, n_a, -1).transpose(1, 0, 2).reshape(n_a, d)
    a_sgu_full = small_g[:, a_norm.size:a_norm.size + a_sgu_norm.size].reshape(
        N_DEV, n_a, -1).transpose(1, 0, 2).reshape(n_a, f_a)
    two_level = ("f0", "a1", "f1")
    later = [("chips" if key in two_level else "gather", shards,
              [lax.dynamic_update_slice(lax.empty((N_DEV,) + s.shape, BF16), s[None], (me, 0, 0)) for s in shards])
             for key, shards in stages[1:]]
    started, a_norm_full = _split_start(later, a_norm_full, "gather_start")
    in_flight = {key: group for (key, _), group in zip(stages[1:], started)}

    def pass_on(key, carry):
        if key in two_level and key in in_flight and in_flight[key][0] == "chips":
            _, lands = _split_wait(in_flight.pop(key), carry, f"gather_wait_{key}_chips")
            (in_flight[key],), carry = _split_start([("forward", [], lands)], carry, f"gather_pass_on_{key}")
        return carry

    def weights(key, after):
        if key not in gathered:
            _, gathered[key] = _split_wait(in_flight.pop(key), after, f"gather_wait_{key}")
        return gathered[key]

    rows_down = ffn_w_down.shape[1]

    def mixer_a_weights(i, after):
        if i == 0:
            (w_in,), (w_out,) = weights("a0", after[0]), weights("a0_out", after[1])
        else:
            w_in, w_out = weights(f"a{i}", after[0])
        return w_in[None], w_out.reshape(1, f_a, d)

    def mixer_b_weights(i, after):
        ws = weights(f"b{i}", after)
        return ws[-2].reshape(1, d, d), ws[-1].reshape(1, d, d)

    def ffn_weights(layer, after):
        w_gu, w_dn = weights(f"f{layer}", after)
        return w_gu[None], w_dn.reshape(1, N_DEV // 2, 2 * rows_down, d)

    w_sp_t = jnp.swapaxes(a_w_spatial, -1, -2)
    b_full = jnp.repeat(jnp.swapaxes(a_b_spatial, -1, -2), gd, axis=-1)

    saved = []

    def ffn_fwd(xin, layer):
        hf = _rms_fwd(xin, ffn_norm[layer], f"ffn_norm_fwd_{layer}")
        w_gu, w_dn = ffn_weights(layer, xin)
        dact, act = _ffn_gate_up(f"ffn_gate_up_{layer}", hf, w_gu, 0)
        act = pass_on(f"a{layer + 1}", act)
        xout = _mm_down(f"ffn_down_{layer}", act, w_dn, 0, xin)
        return xout, (xin, hf, dact, act)

    for i in range(n_a):
        h = _rms_fwd(xs, a_norm_full[i], f"a_norm_fwd_{i}")
        zpre = _mm_colblock(f"a_in_{i}", h, weights(f"a{i}", xs)[0][None], 0)
        p, zs, dgs = _sgu_fwd(zpre, a_sgu_full[i], a_w_spatial[i], b_full[i], f"a_sgu_fwd_{i}")
        p = pass_on(f"f{i}", p)
        w_in, w_out = mixer_a_weights(i, (xs, p))
        x_mid = _mm_natural(f"a_out_{i}", p, w_out, 0, res=xs)
        x_out, ffn_saved = ffn_fwd(x_mid, i)
        saved.append((xs, h, zs, dgs, p, ffn_saved))
        xs = x_out

    x_kv = xs
    w_kv_g = weights("b0", x_kv)[0][None]
    h_kv = _rms_fwd(x_kv, kv_norm, "kv_norm_fwd")
    kv = _mm_colblock("kv_proj", h_kv, w_kv_g, 0)
    kvpad = jnp.pad(kv, ((LEFT, 0), (0, 0)))

    biases = [_bias_block(_bias_build(b_rel_bias[i], f"rel_bias_{i}")) for i in range(n_b)]
    for i in range(n_b):
        layer = n_a + i
        w_q, w_o = mixer_b_weights(i, xs)
        hb = _rms_fwd(xs, b_norm[i], f"b_norm_fwd_{i}")
        q = _mm_natural(f"b_q_{i}", hb, w_q, 0, out_dtype=BF16, scale=ATTN_SCALE)
        o = _attn_fwd(q, kvpad, biases[i], f"b_attn_fwd_{i}")
        x_mid = _mm_natural(f"b_o_{i}", o, w_o, 0, res=xs)
        x_out, ffn_saved = ffn_fwd(x_mid, layer)
        saved.append((xs, hb, q, o, ffn_saved))
        xs = x_out

    dx, loss_local, g_final = _loss_head(xs, final_norm, target, "loss_head")
    loss = lax.psum(loss_local, ("x", "y", "c"))

    big_grads = {}
    pending = []
    in_flight_grads = []

    def start_exchange(dx, tag):
        srcs = [big_grads[key] for key in pending]
        lands = [lax.empty(s.shape, BF16) for s in srcs]
        (group,), dx = _split_start([("exchange", srcs, lands)], dx, f"exchange_start_{tag}")
        in_flight_grads.append((list(pending), group, tag))
        pending.clear()
        return dx

    g_ffn_norm = [None] * depth
    g_a_norm = [None] * n_a
    g_a_sgu = [None] * n_a
    g_w_sp = [None] * n_a
    g_b_sp = [None] * n_a
    g_b_norm = [None] * n_b
    g_rel = [None] * n_b

    def ffn_bwd(dx, layer, ffn_saved):
        eager = layer < n_a
        xin, hf, dact, act = ffn_saved
        big_grads["ffn_w_down", layer] = _mm_dw_down(f"ffn_down_dw_{layer}", act, dx)
        pending.append(("ffn_w_down", layer))
        if eager:
            dx = start_exchange(dx, f"f{layer}_down")
        w_gu, w_dn = ffn_weights(layer, xin)
        dgu = _ffn_down_dx(f"ffn_down_dx_{layer}", dx, w_dn, 0, dact).reshape(N_DEV, t, nb_ffn)
        big_grads["ffn_w_gate_up", layer] = _mm_dw_colblock(
            f"ffn_gate_up_dw_{layer}", hf, dgu, blocked_in=True, transposed=True)
        pending.append(("ffn_w_gate_up", layer))
        if eager:
            dx = start_exchange(dx, f"f{layer}_gate_up")
        dx, g_ffn_norm[layer] = _mm_t_colblock_norm_bwd(
            f"ffn_gate_up_dx_{layer}", dgu, w_gu, 0, xin, ffn_norm[layer], dx, blocked_in=True)
        return dx

    dk = dv = None
    for i in reversed(range(n_b)):
        layer = n_a + i
        x_in, hb, q, o, ffn_saved = saved[layer]
        dx = ffn_bwd(dx, layer, ffn_saved)
        big_grads["b_w_o", i] = _mm_dw_natural(f"b_o_dw_{i}", o, dx)
        w_q, w_o = mixer_b_weights(i, x_in)
        do = _mm_t_natural(f"b_o_dx_{i}", dx, w_o, 0)
        dq, dk, dv, dbias = _attn_bwd(q, kvpad, biases[i], o, do, dk, dv, f"b_attn_bwd_{i}")
        g_rel[i] = _bias_grad(dbias, f"rel_bias_grad_{i}")
        big_grads["b_w_q", i] = _mm_dw_natural(f"b_q_dw_{i}", hb, dq)
        pending.extend([("b_w_o", i), ("b_w_q", i)])
        dx, g_b_norm[i] = _mm_t_natural_norm_bwd(f"b_q_dx_{i}", dq, w_q, 0, x_in, b_norm[i], dx)
        if i > 0:
            dx = start_exchange(dx, f"b{i}")

    dkv = jnp.concatenate([dk[LEFT:], dv[LEFT:]], axis=1).astype(BF16)
    big_grads["w_kv", 0] = _mm_dw_colblock("kv_proj_dw", h_kv, dkv)
    pending.append(("w_kv", 0))
    dx, g_kv_norm = _mm_t_colblock_norm_bwd("kv_proj_dx", dkv, w_kv_g, 0, x_kv, kv_norm, dx)
    dx = start_exchange(dx, "kv")

    for i in reversed(range(n_a)):
        x_in, h, zs, dgs, p, ffn_saved = saved[i]
        dx = ffn_bwd(dx, i, ffn_saved)
        big_grads["a_w_out", i] = _mm_dw_natural(f"a_out_dw_{i}", p, dx)
        pending.append(("a_w_out", i))
        dx = start_exchange(dx, f"a{i}_out")
        w_in, w_out = mixer_a_weights(i, (x_in, p))
        dp = _mm_t_natural(f"a_out_dx_{i}", dx, w_out, 0)
        dz, g_w_sp[i], g_b_sp[i], g_a_sgu[i] = _sgu_bwd(
            zs, dgs, dp, a_sgu_full[i], a_w_spatial[i], w_sp_t[i], b_full[i], f"a_sgu_bwd_{i}")
        big_grads["a_w_in", i] = _mm_dw_colblock(f"a_in_dw_{i}", h, dz)
        pending.append(("a_w_in", i))
        dx = start_exchange(dx, f"a{i}_in")
        dx, g_a_norm[i] = _mm_t_colblock_norm_bwd(f"a_in_dx_{i}", dz, w_in, 0, x_in, a_norm_full[i], dx)
    grad_x = dx[None]

    small_like = [jax.ShapeDtypeStruct((n_a, d), F32), jax.ShapeDtypeStruct((n_a, f_a), F32),
                  a_w_spatial, a_b_spatial, kv_norm, b_norm, b_rel_bias, ffn_norm, final_norm]
    small_partial = _pack(
        [jnp.stack(g_a_norm), jnp.stack(g_a_sgu), jnp.stack(g_w_sp), jnp.stack(g_b_sp), g_kv_norm,
         jnp.stack(g_b_norm), jnp.stack(g_rel), jnp.stack(g_ffn_norm), g_final], N_DEV * 8)
    chunk_rows = small_partial.shape[0] // N_DEV
    arrived = {}
    for keys, group, tag in in_flight_grads:
        srcs, lands = _split_wait(group, dx, f"exchange_wait_{tag}")
        for key, src, land in zip(keys, srcs, lands):
            arrived[key] = (land, src)
    small_got = _exchange([small_partial.reshape(1, N_DEV, chunk_rows, FLAT_LANES)], "exchange_small")[0]
    small_sum = _ordered_sum(small_got[0], "small_grad_sum")
    small_all = _all_gather([small_sum[None]], "gather_small_grads")[0]
    (ga_norm, ga_sgu, gw_sp, gb_sp, gkv_norm, gb_norm, g_relb, gffn_norm, gfinal) = _unpack(small_all, small_like)

    results = {}
    big_names = ["a_w_in", "a_w_out", "w_kv", "b_w_q", "b_w_o", "ffn_w_gate_up", "ffn_w_down"]
    big_wmv = [(a_w_in, m_a_w_in, v_a_w_in), (a_w_out, m_a_w_out, v_a_w_out),
               (w_kv[None], m_w_kv[None], v_w_kv[None]), (b_w_q, m_b_w_q, v_b_w_q), (b_w_o, m_b_w_o, v_b_w_o),
               tuple(jnp.swapaxes(a, 1, 2) for a in (ffn_w_gate_up, m_ffn_w_gate_up, v_ffn_w_gate_up)),
               (ffn_w_down, m_ffn_w_down, v_ffn_w_down)]
    me_arr = jnp.reshape(me, (1,)).astype(jnp.int32)
    for name, (w, m, v) in zip(big_names, big_wmv):
        outs = None
        for layer in range(w.shape[0]):
            got, own = arrived[name, layer]
            outs = _adamw_layer(got, own, w, m, v, layer, outs, me_arr, f"adamw_{name}_{layer}")
        if name == "w_kv":
            outs = [o[0] for o in outs]
        if name == "ffn_w_gate_up":
            outs = [jnp.swapaxes(o, 1, 2) for o in outs]
        results[name] = outs

    n_cols = a_norm.shape[1]
    s_cols = a_sgu_norm.shape[1]
    small_g_list = [lax.dynamic_slice(ga_norm, (0, me * n_cols), (n_a, n_cols)),
                    lax.dynamic_slice(ga_sgu, (0, me * s_cols), (n_a, s_cols)),
                    gw_sp, gb_sp, gkv_norm, gb_norm, g_relb, gffn_norm, gfinal]
    small_names = ["a_norm", "a_sgu_norm", "a_w_spatial", "a_b_spatial", "kv_norm", "b_norm", "b_rel_bias",
                   "ffn_norm", "final_norm"]
    small_w = [a_norm, a_sgu_norm, a_w_spatial, a_b_spatial, kv_norm, b_norm, b_rel_bias, ffn_norm, final_norm]
    small_m = [m_a_norm, m_a_sgu_norm, m_a_w_spatial, m_a_b_spatial, m_kv_norm, m_b_norm, m_b_rel_bias,
               m_ffn_norm, m_final_norm]
    small_v = [v_a_norm, v_a_sgu_norm, v_a_w_spatial, v_a_b_spatial, v_kv_norm, v_b_norm, v_b_rel_bias,
               v_ffn_norm, v_final_norm]
    flat_g = _pack(small_g_list, 8)
    flat_out = _adamw(flat_g[None, None], _pack(small_w, 8)[None], _pack(small_m, 8)[None],
                      _pack(small_v, 8)[None], "adamw_small")
    unpacked = [_unpack(o[0], small_w) for o in flat_out]
    for idx, name in enumerate(small_names):
        results[name] = [unpacked[kind][idx] for kind in range(4)]

    order = ["a_norm", "a_w_in", "a_sgu_norm", "a_w_spatial", "a_b_spatial", "a_w_out", "kv_norm", "w_kv",
             "b_norm", "b_w_q", "b_rel_bias", "b_w_o", "ffn_norm", "ffn_w_gate_up", "ffn_w_down", "final_norm"]
    outputs = [loss, grad_x]
    for kind in range(4):
        outputs += [results[name][kind] for name in order]
    return tuple(outputs)
```

```python
import math

import jax
import jax.numpy as jnp
from jax import lax
from jax.experimental import pallas as pl
from jax.experimental.pallas import tpu as pltpu

F32 = jnp.float32
BF16 = jnp.bfloat16
MESH = pl.DeviceIdType.MESH
HBM_SPEC = pl.BlockSpec(memory_space=pltpu.HBM)
SEM_SPEC = pl.BlockSpec(memory_space=pltpu.SEMAPHORE)

N_DEV = 8
CHUNK = 64
A_CHUNK = 128
A_GROUPS = 8
N_LEFT_CHUNKS = 8
LEFT = N_LEFT_CHUNKS * CHUNK
PAIR_ROWS = 2 * CHUNK
PAIR_BAND = PAIR_ROWS + LEFT
DIAGONALS = PAIR_BAND + PAIR_ROWS
PAIRS_PER_BLOCK = 2
Q_BLOCK = PAIRS_PER_BLOCK * PAIR_ROWS
K_BLOCK = Q_BLOCK + LEFT
ATTN_UNROLL = 7
MAX_REL = 256
N_REL = 2 * MAX_REL + 1
REL_PAD = 640
HEAD_DIM = 64
HEAD_PAIR = 2 * HEAD_DIM
ATTN_SCALE = HEAD_DIM ** -0.5
EPS = 1e-6
NEG_INF = -1e30
ADAM_LR = 0.001
ADAM_B1 = 0.9
ADAM_B2 = 0.999
ADAM_EPS = 1e-08
ADAM_WD = 0.01
ADAM_STEP = 10
FLAT_LANES = 1024
F32_SUBLANES = 8
BF16_SUBLANES = 16
ADAMW_BLOCK_ELEMS = 256 * 1024
V7X_VMEM_BYTES = 64 * 1024 * 1024
VMEM_FLOOR_BYTES = 32 * 1024 * 1024
VMEM_CEIL_BYTES = V7X_VMEM_BYTES - 8 * 1024 * 1024

NN = (((1,), (0,)), ((), ()))
NT = (((1,), (1,)), ((), ()))
TN = (((0,), (0,)), ((), ()))


def _tile(n, pref):
    return pref if n % pref == 0 else n


def _row_tile(n, pref, mult):
    best = None
    for t in range(mult, min(n, pref) + 1, mult):
        if n % t == 0:
            best = t
    return best if best is not None else n


def _nbytes(shape, dtype):
    n = 1
    for s in shape:
        if s is not None:
            n *= s
    return n * jnp.dtype(dtype).itemsize


def _call(body, name, grid, in_specs, out_specs, out_shape, scratch=(), vmem_bytes=0, aliases=None):
    limit = int(min(max(VMEM_FLOOR_BYTES, vmem_bytes * 5 // 4), VMEM_CEIL_BYTES))
    return pl.pallas_call(
        body,
        name=name,
        grid=grid,
        in_specs=in_specs,
        out_specs=out_specs,
        out_shape=out_shape,
        scratch_shapes=list(scratch),
        input_output_aliases=aliases or {},
        compiler_params=pltpu.CompilerParams(
            dimension_semantics=("arbitrary",) * len(grid), vmem_limit_bytes=limit),
    )


ERFC_P = 0.3275911 / math.sqrt(2.0)
ERFC_HALF_COEFFS = tuple(0.5 * a for a in (1.061405429, -1.453152027, 1.421413741, -0.284496736, 0.254829592))


def _gelu_and_grad(x):
    d = 1.0 + ERFC_P * jnp.abs(x)
    r = pl.reciprocal(d, approx=True)
    t = r * (2.0 - d * r)
    a5, a4, a3, a2, a1 = ERFC_HALF_COEFFS
    ex = jnp.exp(-0.5 * (x * x))
    tail = ((((a5 * t + a4) * t + a3) * t + a2) * t + a1) * t * ex
    cdf = jnp.where(x < 0, tail, 1.0 - tail)
    return x * cdf, cdf + x * ex * (1.0 / math.sqrt(2.0 * math.pi))


def _sigmoid(x):
    return 1.0 / (1.0 + jnp.exp(-x))


def _split3(x):
    hi = x.astype(BF16)
    r1 = x - hi.astype(F32)
    mid = r1.astype(BF16)
    lo = (r1 - mid.astype(F32)).astype(BF16)
    return hi, mid, lo


def _rms_fwd(x, g, name):
    t, d = x.shape
    tm = _tile(t, 512)

    def body(x_ref, g_ref, o_ref):
        xf = x_ref[...]
        r = lax.rsqrt(jnp.mean(xf * xf, axis=-1, keepdims=True) + EPS)
        o_ref[...] = (xf * r * g_ref[...]).astype(o_ref.dtype)

    return _call(
        body, name, (t // tm,),
        [pl.BlockSpec((tm, d), lambda i: (i, 0)), pl.BlockSpec((1, d), lambda i: (0, 0))],
        pl.BlockSpec((tm, d), lambda i: (i, 0)),
        jax.ShapeDtypeStruct((t, d), BF16),
        vmem_bytes=2 * (_nbytes((tm, d), F32) + _nbytes((tm, d), BF16)) + 4 * _nbytes((tm, d), F32),
    )(x, g.reshape(1, d))


def _mm(name, dims, a, b, *, grid, a_spec, b_spec, out_shape, out_spec, acc_shape,
        res=None, res_spec=None, scale=None):
    nk = grid[2]
    has_res = res is not None

    def body(*refs):
        refs = list(refs)
        a_ref = refs.pop(0)
        b_ref = refs.pop(0)
        r_ref = refs.pop(0) if has_res else None
        o_ref = refs.pop(0)
        part = lax.dot_general(a_ref[...].astype(BF16), b_ref[...].astype(BF16), dims,
                               preferred_element_type=F32)

        def finish(acc):
            if scale is not None:
                acc = acc * scale
            if has_res:
                acc = acc + r_ref[...]
            o_ref[...] = acc.astype(o_ref.dtype)

        if nk == 1:
            finish(part)
        else:
            acc_ref = refs.pop(0)
            k = pl.program_id(2)

            @pl.when(k == 0)
            def _():
                acc_ref[...] = part

            @pl.when(k > 0)
            def _():
                acc_ref[...] += part

            @pl.when(k == nk - 1)
            def _():
                finish(acc_ref[...])

    operands = [a, b]
    in_specs = [a_spec, b_spec]
    vmem = 2 * (_nbytes(a_spec.block_shape, a.dtype) + _nbytes(b_spec.block_shape, b.dtype)
                + _nbytes(out_spec.block_shape, out_shape.dtype))
    vmem += 3 * _nbytes(acc_shape, F32)
    if has_res:
        operands.append(res)
        in_specs.append(res_spec)
        vmem += 2 * _nbytes(res_spec.block_shape, res.dtype)
    scratch = [pltpu.VMEM(acc_shape, F32)] if nk > 1 else []
    return _call(body, name, grid, in_specs, out_spec, out_shape, scratch=scratch, vmem_bytes=vmem)(*operands)


def _mm_colblock(name, h, w_g, layer):
    t, k = h.shape
    nb = w_g.shape[3]
    tm = _tile(t, 2048)
    return _mm(
        name, NN, h, w_g, grid=(t // tm, N_DEV, 1),
        a_spec=pl.BlockSpec((tm, k), lambda i, j, kk: (i, 0)),
        b_spec=pl.BlockSpec((None, None, k, nb), lambda i, j, kk: (layer, j, 0, 0)),
        out_shape=jax.ShapeDtypeStruct((t, N_DEV * nb), BF16),
        out_spec=pl.BlockSpec((tm, nb), lambda i, j, kk: (i, j)), acc_shape=(tm, nb))


def _mm_natural(name, a, w, layer, *, res=None, out_dtype=F32, scale=None):
    t, k = a.shape
    n = w.shape[2]
    tm = _tile(t, 1024)
    tn = _tile(n, 1024 if k <= 1024 else 512)
    res_spec = None if res is None else pl.BlockSpec((tm, tn), lambda i, j, kk: (i, j))
    return _mm(
        name, NN, a, w, grid=(t // tm, n // tn, 1),
        a_spec=pl.BlockSpec((tm, k), lambda i, j, kk: (i, 0)),
        b_spec=pl.BlockSpec((None, k, tn), lambda i, j, kk: (layer, 0, j)),
        out_shape=jax.ShapeDtypeStruct((t, n), out_dtype),
        out_spec=pl.BlockSpec((tm, tn), lambda i, j, kk: (i, j)),
        acc_shape=(tm, tn), res=res, res_spec=res_spec, scale=scale)


def _mm_down(name, act, w4, layer, res):
    nblk, t, kb = act.shape
    n = w4.shape[3]
    tm = _tile(t, 1024)

    def body(a_ref, b_ref, r_ref, o_ref):
        acc = r_ref[...]
        for u in range(nblk):
            acc = acc + jnp.dot(a_ref[u], b_ref[u], preferred_element_type=F32)
        o_ref[...] = acc

    row = pl.BlockSpec((tm, n), lambda i: (i, 0))
    return _call(
        body, name, (t // tm,),
        [pl.BlockSpec((nblk, tm, kb), lambda i: (0, i, 0)),
         pl.BlockSpec((None, nblk, kb, n), lambda i: (layer, 0, 0, 0)),
         row],
        row,
        jax.ShapeDtypeStruct((t, n), F32),
        vmem_bytes=2 * (_nbytes((nblk, tm, kb), BF16) + _nbytes((nblk, kb, n), BF16)) + 6 * _nbytes((tm, n), F32),
    )(act, w4, res)


def _mm_t_colblock_norm_bwd(name, dz, w_g, layer, x, g, dx_up, blocked_in=False):
    k = w_g.shape[2]
    nb = w_g.shape[3]
    t = x.shape[0]
    narrow = nb * k <= 512 * 1024
    tm = _tile(t, 1024 if narrow else 512)
    per_step = 4 if narrow else N_DEV
    n_steps = N_DEV // per_step
    if blocked_in:
        a_spec = pl.BlockSpec((per_step, tm, nb), lambda i, kk: (kk, i, 0))
    else:
        a_spec = pl.BlockSpec((tm, per_step * nb), lambda i, kk: (i, kk))

    def body(a_ref, b_ref, x_ref, g_ref, up_ref, dx_ref, dg_ref, acc_ref):
        i = pl.program_id(0)
        kk = pl.program_id(1)
        part = None
        for u in range(per_step):
            a = a_ref[u] if blocked_in else a_ref[:, u * nb:(u + 1) * nb]
            term = lax.dot_general(a.astype(BF16), b_ref[u].astype(BF16), NT, preferred_element_type=F32)
            part = term if part is None else part + term

        @pl.when(kk == 0)
        def _():
            acc_ref[...] = part

        @pl.when(kk > 0)
        def _():
            acc_ref[...] += part

        @pl.when((i == 0) & (kk == 0))
        def _():
            dg_ref[...] = jnp.zeros_like(dg_ref)

        @pl.when(kk == n_steps - 1)
        def _():
            dy = acc_ref[...]
            xf = x_ref[...]
            r = lax.rsqrt(jnp.mean(xf * xf, axis=-1, keepdims=True) + EPS)
            xhat = xf * r
            dxhat = dy * g_ref[...]
            dg_ref[...] += jnp.sum(dy * xhat, axis=0, keepdims=True)
            dx_ref[...] = up_ref[...] + r * (dxhat - xhat * jnp.mean(dxhat * xhat, axis=-1, keepdims=True))

    row = pl.BlockSpec((tm, k), lambda i, kk: (i, 0))
    vec = pl.BlockSpec((1, k), lambda i, kk: (0, 0))
    dx, dg = _call(
        body, name, (t // tm, n_steps),
        [a_spec, pl.BlockSpec((None, per_step, k, nb), lambda i, kk: (layer, kk, 0, 0)), row, vec, row],
        [row, vec],
        [jax.ShapeDtypeStruct((t, k), F32), jax.ShapeDtypeStruct((1, k), F32)],
        scratch=[pltpu.VMEM((tm, k), F32)],
        vmem_bytes=2 * per_step * (_nbytes((tm, nb), BF16) + _nbytes((k, nb), BF16)) + 10 * _nbytes((tm, k), F32),
    )(dz, w_g, x, g.reshape(1, k), dx_up)
    return dx, dg.reshape(k)


def _ffn_gate_up(name, h, w_g, layer):
    t, k = h.shape
    nb = w_g.shape[3]
    half = N_DEV // 2
    tm = _tile(t, 1024)

    def body(h_ref, wg_ref, wu_ref, dact_ref, act_ref):
        hb = h_ref[...]
        gate = jnp.dot(hb, wg_ref[...], preferred_element_type=F32)
        up = jnp.dot(hb, wu_ref[...], preferred_element_type=F32)
        sig = _sigmoid(gate)
        silu = gate * sig
        dact_ref[0] = (up * (sig * (1.0 + gate * (1.0 - sig)))).astype(BF16)
        dact_ref[1] = silu.astype(BF16)
        act_ref[...] = (silu * up).astype(BF16)

    return _call(
        body, name, (t // tm, half),
        [pl.BlockSpec((tm, k), lambda i, j: (i, 0)),
         pl.BlockSpec((None, None, k, nb), lambda i, j: (layer, j, 0, 0)),
         pl.BlockSpec((None, None, k, nb), lambda i, j: (layer, half + j, 0, 0))],
        [pl.BlockSpec((2, None, tm, nb), lambda i, j: (0, j, i, 0)),
         pl.BlockSpec((None, tm, nb), lambda i, j: (j, i, 0))],
        [jax.ShapeDtypeStruct((2, half, t, nb), BF16), jax.ShapeDtypeStruct((half, t, nb), BF16)],
        vmem_bytes=2 * (_nbytes((tm, k), BF16) + 2 * _nbytes((k, nb), BF16) + 3 * _nbytes((tm, nb), BF16))
        + 8 * _nbytes((tm, nb), F32),
    )(h, w_g, w_g)


def _ffn_down_dx(name, dy, w4, layer, dact):
    t, n = dy.shape
    nblk, kb = w4.shape[1], w4.shape[2]
    tm = _tile(t, 1024)

    def body(dy_ref, w_ref, dact_ref, dgu_ref):
        da = lax.dot_general(dy_ref[...].astype(BF16), w_ref[...], NT, preferred_element_type=F32)
        dgu_ref[0] = (da * dact_ref[0].astype(F32)).astype(BF16)
        dgu_ref[1] = (da * dact_ref[1].astype(F32)).astype(BF16)

    blk = pl.BlockSpec((2, None, tm, kb), lambda i, j: (0, j, i, 0))
    return _call(
        body, name, (t // tm, nblk),
        [pl.BlockSpec((tm, n), lambda i, j: (i, 0)),
         pl.BlockSpec((None, None, kb, n), lambda i, j: (layer, j, 0, 0)),
         blk],
        blk,
        jax.ShapeDtypeStruct((2, nblk, t, kb), BF16),
        vmem_bytes=2 * (_nbytes((tm, n), F32) + _nbytes((kb, n), BF16) + 4 * _nbytes((tm, kb), BF16))
        + 8 * _nbytes((tm, kb), F32),
    )(dy, w4, dact)


def _mm_t_natural(name, dy, w, layer):
    t, n = dy.shape
    k = w.shape[1]
    tm = _tile(t, 1024)
    tk = _tile(k, 1024)
    return _mm(
        name, NT, dy, w, grid=(t // tm, k // tk, 1),
        a_spec=pl.BlockSpec((tm, n), lambda i, j, kk: (i, 0)),
        b_spec=pl.BlockSpec((None, tk, n), lambda i, j, kk: (layer, j, 0)),
        out_shape=jax.ShapeDtypeStruct((t, k), BF16),
        out_spec=pl.BlockSpec((tm, tk), lambda i, j, kk: (i, j)),
        acc_shape=(tm, tk))


def _mm_t_natural_norm_bwd(name, dy, w, layer, x, g, dx_up):
    t, n = dy.shape
    k = w.shape[1]
    tm = _tile(t, 1024)

    def body(a_ref, b_ref, x_ref, g_ref, up_ref, dx_ref, dg_ref):
        @pl.when(pl.program_id(0) == 0)
        def _():
            dg_ref[...] = jnp.zeros_like(dg_ref)

        dh = lax.dot_general(a_ref[...].astype(BF16), b_ref[...], NT, preferred_element_type=F32)
        xf = x_ref[...]
        r = lax.rsqrt(jnp.mean(xf * xf, axis=-1, keepdims=True) + EPS)
        xhat = xf * r
        dxhat = dh * g_ref[...]
        dg_ref[...] += jnp.sum(dh * xhat, axis=0, keepdims=True)
        dx_ref[...] = up_ref[...] + r * (dxhat - xhat * jnp.mean(dxhat * xhat, axis=-1, keepdims=True))

    row = pl.BlockSpec((tm, k), lambda i: (i, 0))
    vec = pl.BlockSpec((1, k), lambda i: (0, 0))
    dx, dg = _call(
        body, name, (t // tm,),
        [pl.BlockSpec((tm, n), lambda i: (i, 0)), pl.BlockSpec((None, k, n), lambda i: (layer, 0, 0)), row, vec, row],
        [row, vec],
        [jax.ShapeDtypeStruct((t, k), F32), jax.ShapeDtypeStruct((1, k), F32)],
        vmem_bytes=2 * (_nbytes((tm, n), dy.dtype) + _nbytes((k, n), BF16)) + 10 * _nbytes((tm, k), F32),
    )(dy, w, x, g.reshape(1, k), dx_up)
    return dx, dg.reshape(k)


def _mm_dw_colblock(name, h, dz, blocked_in=False, transposed=False):
    t, k = h.shape
    nb = dz.shape[2] if blocked_in else dz.shape[1] // N_DEV
    tk = _tile(t, 4096)
    h_spec = pl.BlockSpec((tk, k), lambda i, j, kk: (kk, 0))
    if blocked_in:
        dz_spec = pl.BlockSpec((None, tk, nb), lambda i, j, kk: (j, kk, 0))
    else:
        dz_spec = pl.BlockSpec((tk, nb), lambda i, j, kk: (kk, j))
    rows, cols = (nb, k) if transposed else (k, nb)
    return _mm(
        name, TN, *((dz, h) if transposed else (h, dz)), grid=(1, N_DEV, t // tk),
        a_spec=dz_spec if transposed else h_spec,
        b_spec=h_spec if transposed else dz_spec,
        out_shape=jax.ShapeDtypeStruct((N_DEV, rows, cols), BF16),
        out_spec=pl.BlockSpec((None, rows, cols), lambda i, j, kk: (j, 0, 0)),
        acc_shape=(rows, cols))


def _mm_dw_natural(name, a, dy):
    t, k = a.shape
    n = dy.shape[1]
    tko = _tile(k, 1024)
    tt = _tile(t, 2048)
    out = _mm(
        name, TN, a, dy, grid=(k // tko, 1, t // tt),
        a_spec=pl.BlockSpec((tt, tko), lambda i, j, kk: (kk, i)),
        b_spec=pl.BlockSpec((tt, n), lambda i, j, kk: (kk, 0)),
        out_shape=jax.ShapeDtypeStruct((k, n), BF16),
        out_spec=pl.BlockSpec((tko, n), lambda i, j, kk: (i, 0)),
        acc_shape=(tko, n))
    return out.reshape(N_DEV, k // N_DEV, n)


def _mm_dw_down(name, act, dy):
    nblk, t, kb = act.shape
    n = dy.shape[1]
    tt = _tile(t, 2048)
    out = _mm(
        name, TN, act, dy, grid=(nblk, 1, t // tt),
        a_spec=pl.BlockSpec((None, tt, kb), lambda i, j, kk: (i, kk, 0)),
        b_spec=pl.BlockSpec((tt, n), lambda i, j, kk: (kk, 0)),
        out_shape=jax.ShapeDtypeStruct((nblk, kb, n), BF16),
        out_spec=pl.BlockSpec((None, kb, n), lambda i, j, kk: (i, 0, 0)),
        acc_shape=(kb, n))
    return out.reshape(N_DEV, (nblk * kb) // N_DEV, n)


def _spatial_mask(transposed=False):
    r = lax.broadcasted_iota(jnp.int32, (A_CHUNK, A_CHUNK), 0) // CHUNK
    c = lax.broadcasted_iota(jnp.int32, (A_CHUNK, A_CHUNK), 1) // CHUNK
    return c >= r if transposed else r >= c


def _sgu_tile(t):
    return _tile(t, 2 * A_CHUNK)


def _sgu_fwd(zpre, g_sgu, w_sp, b_full, name):
    t, f2 = zpre.shape
    f = f2 // 2
    gd = f // A_GROUPS
    tm = _sgu_tile(t)

    def body(z_ref, g_ref, w_ref, b_ref, p_ref, zs_ref, dg_ref):
        mask = _spatial_mask()
        wm = [jnp.where(mask, w_ref[g], 0.0).astype(BF16) for g in range(A_GROUPS)]
        for c in range(tm // A_CHUNK):
            rows = pl.ds(c * A_CHUNK, A_CHUNK)
            z, dgelu = _gelu_and_grad(z_ref[rows, :].astype(F32))
            zs_ref[rows, :] = z.astype(BF16)
            dg_ref[rows, :] = dgelu.astype(BF16)
            u = z[:, :f]
            v0 = z[:, f:]
            r = lax.rsqrt(jnp.mean(v0 * v0, axis=-1, keepdims=True) + EPS)
            v1 = (v0 * r * g_ref[...]).astype(BF16)
            for g in range(A_GROUPS):
                cols = slice(g * gd, (g + 1) * gd)
                v2 = jnp.dot(wm[g], v1[:, cols], preferred_element_type=F32) + b_ref[:, cols]
                p_ref[rows, cols] = (u[:, cols] * v2).astype(BF16)

    return _call(
        body, name, (t // tm,),
        [pl.BlockSpec((tm, f2), lambda i: (i, 0)),
         pl.BlockSpec((1, f), lambda i: (0, 0)),
         pl.BlockSpec((A_GROUPS, A_CHUNK, A_CHUNK), lambda i: (0, 0, 0)),
         pl.BlockSpec((A_CHUNK, f), lambda i: (0, 0))],
        [pl.BlockSpec((tm, f), lambda i: (i, 0)), pl.BlockSpec((tm, f2), lambda i: (i, 0)),
         pl.BlockSpec((tm, f2), lambda i: (i, 0))],
        [jax.ShapeDtypeStruct((t, f), BF16), jax.ShapeDtypeStruct((t, f2), BF16), jax.ShapeDtypeStruct((t, f2), BF16)],
        vmem_bytes=6 * _nbytes((tm, f2), BF16) + 2 * _nbytes((tm, f), BF16) + 8 * _nbytes((A_CHUNK, f2), F32),
    )(zpre, g_sgu.reshape(1, f), w_sp, b_full)


def _sgu_bwd(zs, dgs, dp, g_sgu, w_sp, w_sp_t, b_full, name):
    t, f2 = zs.shape
    f = f2 // 2
    gd = f // A_GROUPS
    tm = _sgu_tile(t)
    n_steps = t // tm

    def body(z_ref, dgelu_ref, dp_ref, g_ref, w_ref, wt_ref, b_ref, dz_ref, dw_ref, db_ref, dg_ref, dv1_ref, dbf_ref):
        step = pl.program_id(0)

        @pl.when(step == 0)
        def _():
            dw_ref[...] = jnp.zeros_like(dw_ref)
            dg_ref[...] = jnp.zeros_like(dg_ref)
            dbf_ref[...] = jnp.zeros_like(dbf_ref)

        mask = _spatial_mask()
        mask_t = _spatial_mask(transposed=True)
        wm = [jnp.where(mask, w_ref[g], 0.0).astype(BF16) for g in range(A_GROUPS)]
        wmt = [jnp.where(mask_t, wt_ref[g], 0.0).astype(BF16) for g in range(A_GROUPS)]
        gain = g_ref[...]
        for c in range(tm // A_CHUNK):
            rows = pl.ds(c * A_CHUNK, A_CHUNK)
            z = z_ref[rows, :].astype(F32)
            dgelu = dgelu_ref[rows, :].astype(F32)
            u = z[:, :f]
            v0 = z[:, f:]
            r = lax.rsqrt(jnp.mean(v0 * v0, axis=-1, keepdims=True) + EPS)
            xhat = v0 * r
            v1 = (xhat * gain).astype(BF16)
            dpf = dp_ref[rows, :].astype(F32)
            for g in range(A_GROUPS):
                cols = slice(g * gd, (g + 1) * gd)
                v1g = v1[:, cols]
                v2 = jnp.dot(wm[g], v1g, preferred_element_type=F32) + b_ref[:, cols]
                dpg = dpf[:, cols]
                dz_ref[rows, cols] = (dpg * v2 * dgelu[:, cols]).astype(BF16)
                dv2 = dpg * u[:, cols]
                dbf_ref[:, cols] += dv2
                dv2b = dv2.astype(BF16)
                dwg = lax.dot_general(dv2b, v1g, NT, preferred_element_type=F32)
                dw_ref[g] += jnp.where(mask, dwg, 0.0)
                dv1_ref[:, cols] = jnp.dot(wmt[g], dv2b, preferred_element_type=F32)
            dv1 = dv1_ref[...]
            dxhat = dv1 * gain
            dg_ref[...] += jnp.sum(dv1 * xhat, axis=0, keepdims=True)
            dv0 = r * (dxhat - xhat * jnp.mean(dxhat * xhat, axis=-1, keepdims=True))
            dz_ref[rows, pl.ds(f, f)] = (dv0 * dgelu[:, f:]).astype(BF16)

        @pl.when(step == n_steps - 1)
        def _():
            for g in range(A_GROUPS):
                db_ref[g] = jnp.sum(dbf_ref[:, g * gd:(g + 1) * gd], axis=1, keepdims=True)

    wspec = pl.BlockSpec((A_GROUPS, A_CHUNK, A_CHUNK), lambda i: (0, 0, 0))
    dz, dw, db, dg = _call(
        body, name, (n_steps,),
        [pl.BlockSpec((tm, f2), lambda i: (i, 0)),
         pl.BlockSpec((tm, f2), lambda i: (i, 0)),
         pl.BlockSpec((tm, f), lambda i: (i, 0)),
         pl.BlockSpec((1, f), lambda i: (0, 0)),
         wspec, wspec,
         pl.BlockSpec((A_CHUNK, f), lambda i: (0, 0))],
        [pl.BlockSpec((tm, f2), lambda i: (i, 0)),
         wspec,
         pl.BlockSpec((A_GROUPS, A_CHUNK, 1), lambda i: (0, 0, 0)),
         pl.BlockSpec((1, f), lambda i: (0, 0))],
        [jax.ShapeDtypeStruct((t, f2), BF16),
         jax.ShapeDtypeStruct((A_GROUPS, A_CHUNK, A_CHUNK), F32),
         jax.ShapeDtypeStruct((A_GROUPS, A_CHUNK, 1), F32),
         jax.ShapeDtypeStruct((1, f), F32)],
        scratch=[pltpu.VMEM((A_CHUNK, f), F32), pltpu.VMEM((A_CHUNK, f), F32)],
        vmem_bytes=6 * _nbytes((tm, f2), BF16) + 2 * _nbytes((tm, f), BF16) + 12 * _nbytes((A_CHUNK, f2), F32),
    )(zs, dgs, dp, g_sgu.reshape(1, f), w_sp, w_sp_t, b_full)
    return dz, dw, db.reshape(A_GROUPS, A_CHUNK), dg.reshape(f)


def _pair_valid(qi, col):
    qc = qi // CHUNK
    kc = col // CHUNK
    return (kc >= qc) & (kc <= qc + N_LEFT_CHUNKS)


def _diagonal_onehot():
    e = lax.broadcasted_iota(jnp.int32, (REL_PAD, DIAGONALS), 1)
    idx = jnp.clip(PAIR_BAND - 1 - e, -MAX_REL, MAX_REL) + MAX_REL
    r = lax.broadcasted_iota(jnp.int32, (REL_PAD, DIAGONALS), 0)
    return jnp.where(r == idx, 1.0, 0.0).astype(BF16)


def _bias_build(table, name):
    h = table.shape[0]
    tab = jnp.pad(table, ((0, 0), (0, REL_PAD - N_REL)))

    def body(t_ref, o_ref):
        oh = _diagonal_onehot()
        diag = jnp.zeros((h, DIAGONALS), F32)
        for piece in _split3(t_ref[...]):
            diag += jnp.dot(piece, oh, preferred_element_type=F32)
        col = lax.broadcasted_iota(jnp.int32, (h, PAIR_BAND), 1)
        for qi in range(PAIR_ROWS):
            row = pltpu.roll(diag, (qi - (PAIR_ROWS - 1)) % DIAGONALS, 1)[:, :PAIR_BAND]
            o_ref[qi] = jnp.where(_pair_valid(qi, col), row, NEG_INF)

    out = _call(
        body, name, (1,),
        [pl.BlockSpec((h, REL_PAD), lambda i: (0, 0))],
        pl.BlockSpec((PAIR_ROWS, h, PAIR_BAND), lambda i: (0, 0, 0)),
        jax.ShapeDtypeStruct((PAIR_ROWS, h, PAIR_BAND), F32),
        vmem_bytes=4 * _nbytes((PAIR_ROWS, h, PAIR_BAND), F32),
    )(tab)
    return jnp.transpose(out, (1, 0, 2))


def _bias_block(pair_bias):
    rest = K_BLOCK - PAIR_BAND
    return jnp.concatenate(
        [jnp.pad(pair_bias, ((0, 0), (0, 0), (p * PAIR_ROWS, rest - p * PAIR_ROWS)), constant_values=NEG_INF)
         for p in range(PAIRS_PER_BLOCK)], axis=1)


def _bias_grad(dbias, name):
    h = dbias.shape[0]
    db_t = jnp.transpose(dbias, (1, 0, 2))

    def body(d_ref, o_ref):
        diag = jnp.zeros((h, DIAGONALS), F32)
        for qi in range(PAIR_ROWS):
            diag += pltpu.roll(d_ref[qi], PAIR_ROWS - 1 - qi, 1)
        oh = _diagonal_onehot()
        acc = jnp.zeros((h, REL_PAD), F32)
        for piece in _split3(diag):
            acc += lax.dot_general(piece, oh, NT, preferred_element_type=F32)
        o_ref[...] = acc

    out = _call(
        body, name, (1,),
        [pl.BlockSpec((PAIR_ROWS, h, DIAGONALS), lambda i: (0, 0, 0))],
        pl.BlockSpec((h, REL_PAD), lambda i: (0, 0)),
        jax.ShapeDtypeStruct((h, REL_PAD), F32),
        vmem_bytes=4 * _nbytes((PAIR_ROWS, h, DIAGONALS), F32),
    )(db_t)
    return out[:, :N_REL]


def _head_masks():
    lane = lax.broadcasted_iota(jnp.int32, (Q_BLOCK, HEAD_PAIR), 1)
    return lane < HEAD_DIM, lane >= HEAD_DIM


def _block_scores(qm, kb, bias, valid):
    s = lax.dot_general(qm, kb, NT, preferred_element_type=F32) + bias
    return s if valid is None else jnp.where(valid, s, NEG_INF)


def _softmax_rows(s):
    e = jnp.exp(s - jnp.max(s, axis=-1, keepdims=True))
    return e * (1.0 / jnp.sum(e, axis=-1, keepdims=True))


def _padded_then_plain(step, n_blocks):
    n_padded = min(LEFT // Q_BLOCK, n_blocks)
    lax.fori_loop(0, n_padded, lambda j, c: step(j, c, True), 0, unroll=True)
    lax.fori_loop(n_padded, n_blocks, lambda j, c: step(j, c, False), 0, unroll=ATTN_UNROLL)


def _attn_fwd(q, kvpad, bias, name):
    t, d = q.shape
    n_pairs = d // HEAD_PAIR
    n_blocks = t // Q_BLOCK

    def body(q_ref, k_ref, v_ref, b_ref, o_ref):
        masks = _head_masks()
        key = lax.broadcasted_iota(jnp.int32, (Q_BLOCK, K_BLOCK), 1)

        def step(j, carry, padded):
            r0 = pl.multiple_of(j * Q_BLOCK, Q_BLOCK)
            q2 = q_ref[pl.ds(r0, Q_BLOCK), :].astype(F32)
            kb = k_ref[pl.ds(r0, K_BLOCK), :]
            vb = v_ref[pl.ds(r0, K_BLOCK), :]
            valid = key >= LEFT - j * Q_BLOCK if padded else None
            scores = [_block_scores(jnp.where(masks[a], q2, 0.0).astype(BF16), kb, b_ref[a], valid) for a in range(2)]
            probs = [_softmax_rows(s).astype(BF16) for s in scores]
            outs = [jnp.dot(p, vb, preferred_element_type=F32) for p in probs]
            o_ref[pl.ds(r0, Q_BLOCK), :] = jnp.where(masks[0], outs[0], outs[1]).astype(BF16)
            return carry

        _padded_then_plain(step, n_blocks)

    return _call(
        body, name, (n_pairs,),
        [pl.BlockSpec((t, HEAD_PAIR), lambda p: (0, p)),
         pl.BlockSpec((LEFT + t, HEAD_PAIR), lambda p: (0, p)),
         pl.BlockSpec((LEFT + t, HEAD_PAIR), lambda p: (0, n_pairs + p)),
         pl.BlockSpec((2, Q_BLOCK, K_BLOCK), lambda p: (p, 0, 0))],
        pl.BlockSpec((t, HEAD_PAIR), lambda p: (0, p)),
        jax.ShapeDtypeStruct((t, d), BF16),
        vmem_bytes=8 * _nbytes((LEFT + t, HEAD_PAIR), BF16) + 12 * _nbytes((2, Q_BLOCK, K_BLOCK), F32),
    )(q, kvpad, kvpad, bias)


def _attn_bwd(q, kvpad, bias, o, do, dk_in, dv_in, name):
    t, d = q.shape
    n_pairs = d // HEAD_PAIR
    n_blocks = t // Q_BLOCK
    has_in = dk_in is not None

    def body(*refs):
        refs = list(refs)
        q_ref, k_ref, v_ref, b_ref, o_ref, do_ref = refs[:6]
        refs = refs[6:]
        if has_in:
            dki_ref, dvi_ref = refs[:2]
            refs = refs[2:]
        dq_ref, dk_ref, dv_ref, db_ref = refs
        masks = _head_masks()
        key = lax.broadcasted_iota(jnp.int32, (Q_BLOCK, K_BLOCK), 1)
        if has_in:
            dk_ref[...] = dki_ref[...]
            dv_ref[...] = dvi_ref[...]
        else:
            dk_ref[...] = jnp.zeros_like(dk_ref)
            dv_ref[...] = jnp.zeros_like(dv_ref)
        db_ref[...] = jnp.zeros_like(db_ref)

        def step(j, carry, padded):
            r0 = pl.multiple_of(j * Q_BLOCK, Q_BLOCK)
            q2 = q_ref[pl.ds(r0, Q_BLOCK), :].astype(F32)
            do2 = do_ref[pl.ds(r0, Q_BLOCK), :].astype(F32)
            do_o = do2 * o_ref[pl.ds(r0, Q_BLOCK), :].astype(F32)
            kb = k_ref[pl.ds(r0, K_BLOCK), :]
            vb = v_ref[pl.ds(r0, K_BLOCK), :]
            valid = key >= LEFT - j * Q_BLOCK if padded else None
            heads = range(2)
            qms = [jnp.where(masks[a], q2, 0.0).astype(BF16) for a in heads]
            doms = [jnp.where(masks[a], do2, 0.0).astype(BF16) for a in heads]
            scores = [_block_scores(qms[a], kb, b_ref[a], valid) for a in heads]
            dps = [lax.dot_general(doms[a], vb, NT, preferred_element_type=F32) for a in heads]
            ps = [_softmax_rows(s) for s in scores]
            rows = [jnp.sum(jnp.where(masks[a], do_o, 0.0), axis=-1, keepdims=True) for a in heads]
            dss = [ps[a] * (dps[a] - rows[a]) for a in heads]
            for a in heads:
                for pair in range(PAIRS_PER_BLOCK):
                    lo = pair * PAIR_ROWS
                    db_ref[a, :, pl.ds(0, PAIR_BAND)] += dss[a][lo:lo + PAIR_ROWS, lo:lo + PAIR_BAND]
            dsbs = [ds.astype(BF16) for ds in dss]
            pbs = [p.astype(BF16) for p in ps]
            dqs = [jnp.dot(dsbs[a], kb, preferred_element_type=F32) for a in heads]
            dk_acc = sum(lax.dot_general(dsbs[a], qms[a], TN, preferred_element_type=F32) for a in heads)
            dv_acc = sum(lax.dot_general(pbs[a], doms[a], TN, preferred_element_type=F32) for a in heads)
            dq = jnp.where(masks[0], dqs[0], dqs[1]) * ATTN_SCALE
            dq_ref[pl.ds(r0, Q_BLOCK), :] = dq.astype(BF16)
            dk_ref[pl.ds(r0, K_BLOCK), :] += dk_acc
            dv_ref[pl.ds(r0, K_BLOCK), :] += dv_acc
            return carry

        _padded_then_plain(step, n_blocks)

    q_spec = pl.BlockSpec((t, HEAD_PAIR), lambda p: (0, p))
    kv_spec = pl.BlockSpec((LEFT + t, HEAD_PAIR), lambda p: (0, p))
    operands = [q, kvpad, kvpad, bias, o, do]
    in_specs = [q_spec, kv_spec, pl.BlockSpec((LEFT + t, HEAD_PAIR), lambda p: (0, n_pairs + p)),
                pl.BlockSpec((2, Q_BLOCK, K_BLOCK), lambda p: (p, 0, 0)), q_spec, q_spec]
    aliases = None
    if has_in:
        operands += [dk_in, dv_in]
        in_specs += [kv_spec, kv_spec]
        aliases = {6: 1, 7: 2}
    return _call(
        body, name, (n_pairs,),
        in_specs,
        [q_spec, kv_spec, kv_spec, pl.BlockSpec((2, PAIR_ROWS, DIAGONALS), lambda p: (p, 0, 0))],
        [jax.ShapeDtypeStruct((t, d), BF16),
         jax.ShapeDtypeStruct((LEFT + t, d), F32),
         jax.ShapeDtypeStruct((LEFT + t, d), F32),
         jax.ShapeDtypeStruct((d // HEAD_DIM, PAIR_ROWS, DIAGONALS), F32)],
        vmem_bytes=10 * _nbytes((LEFT + t, HEAD_PAIR), BF16) + 8 * _nbytes((LEFT + t, HEAD_PAIR), F32)
        + 16 * _nbytes((2, Q_BLOCK, K_BLOCK), F32),
        aliases=aliases,
    )(*operands)


def _loss_head(x, g, target, name):
    t, d = x.shape
    tm = _tile(t, 512)

    def body(x_ref, g_ref, t_ref, dx_ref, loss_ref, dg_ref):
        @pl.when(pl.program_id(0) == 0)
        def _():
            loss_ref[...] = jnp.zeros_like(loss_ref)
            dg_ref[...] = jnp.zeros_like(dg_ref)

        xf = x_ref[...]
        r = lax.rsqrt(jnp.mean(xf * xf, axis=-1, keepdims=True) + EPS)
        xhat = xf * r
        diff = xhat * g_ref[...] - t_ref[...]
        row_loss = jnp.mean(diff * diff, axis=-1, keepdims=True)
        loss_ref[...] += 0.5 * jnp.sum(row_loss, axis=0, keepdims=True)
        dy = diff * (1.0 / d)
        dg_ref[...] += jnp.sum(dy * xhat, axis=0, keepdims=True)
        dxhat = dy * g_ref[...]
        dx_ref[...] = r * (dxhat - xhat * jnp.mean(dxhat * xhat, axis=-1, keepdims=True))

    row = pl.BlockSpec((tm, d), lambda i: (i, 0))
    vec = pl.BlockSpec((1, d), lambda i: (0, 0))
    dx, loss, dg = _call(
        body, name, (t // tm,),
        [row, vec, row],
        [row, pl.BlockSpec((1, 1), lambda i: (0, 0)), vec],
        [jax.ShapeDtypeStruct((t, d), F32), jax.ShapeDtypeStruct((1, 1), F32), jax.ShapeDtypeStruct((1, d), F32)],
        vmem_bytes=10 * _nbytes((tm, d), F32),
    )(x, g.reshape(1, d), target)
    return dx, loss[0, 0], dg.reshape(d)


def _adamw_store(g, w_ref, m_ref, v_ref, g_ref, d_ref, nm_ref, nv_ref):
    c1 = 1.0 / (1.0 - ADAM_B1 ** ADAM_STEP)
    c2 = 1.0 / (1.0 - ADAM_B2 ** ADAM_STEP)
    nm = ADAM_B1 * m_ref[...] + (1.0 - ADAM_B1) * g
    nv = ADAM_B2 * v_ref[...] + (1.0 - ADAM_B2) * (g * g)
    g_ref[...] = g
    nm_ref[...] = nm
    nv_ref[...] = nv
    d_ref[...] = -ADAM_LR * ((nm * c1) / (jnp.sqrt(nv * c2) + ADAM_EPS) + ADAM_WD * w_ref[...])


def _adamw_layer(recv, own, w, m, v, layer, prev, me, name):
    n_src, r, c = recv.shape
    tr = _row_tile(r, max(BF16_SUBLANES, ADAMW_BLOCK_ELEMS // c), BF16_SUBLANES)
    first = prev is None

    def body(me_ref, recv_ref, own_ref, w_ref, m_ref, v_ref, *rest):
        mine = me_ref[0]
        own_part = own_ref[...].astype(F32)
        g = None
        for s in range(n_src):
            part = jnp.where(mine == s, own_part, recv_ref[s].astype(F32))
            g = part if g is None else g + part
        _adamw_store(g, w_ref, m_ref, v_ref, *rest[-4:])

    blk = pl.BlockSpec((None, tr, c), lambda i, me_ref: (layer, i, 0))
    any_spec = pl.BlockSpec(memory_space=pl.ANY)
    out = jax.ShapeDtypeStruct(w.shape, F32)
    operands = [me, recv, own, w, m, v] + ([] if first else list(prev))
    vmem = 2 * _nbytes((n_src + 1, tr, c), BF16) + 18 * _nbytes((tr, c), F32)
    return pl.pallas_call(
        body,
        name=name,
        grid_spec=pltpu.PrefetchScalarGridSpec(
            num_scalar_prefetch=1,
            grid=(r // tr,),
            in_specs=[pl.BlockSpec((n_src, tr, c), lambda i, me_ref: (0, i, 0)),
                      pl.BlockSpec((None, tr, c), lambda i, me_ref: (me_ref[0], i, 0)),
                      blk, blk, blk] + ([] if first else [any_spec] * 4),
            out_specs=[blk, blk, blk, blk],
        ),
        out_shape=[out, out, out, out],
        input_output_aliases={} if first else {6 + j: j for j in range(4)},
        compiler_params=pltpu.CompilerParams(
            dimension_semantics=("arbitrary",),
            vmem_limit_bytes=int(min(max(VMEM_FLOOR_BYTES, vmem * 5 // 4), VMEM_CEIL_BYTES))),
    )(*operands)


def _adamw(parts, w, m, v, name):
    n_layers, n_src, r, c = parts.shape
    mult = BF16_SUBLANES if parts.dtype == BF16 else F32_SUBLANES
    tr = _row_tile(r, max(mult, ADAMW_BLOCK_ELEMS // c), mult)

    def body(p_ref, w_ref, m_ref, v_ref, g_ref, d_ref, nm_ref, nv_ref):
        g = p_ref[0].astype(F32)
        for s in range(1, n_src):
            g = g + p_ref[s].astype(F32)
        _adamw_store(g, w_ref, m_ref, v_ref, g_ref, d_ref, nm_ref, nv_ref)

    blk = pl.BlockSpec((None, tr, c), lambda l, i: (l, i, 0))
    out = jax.ShapeDtypeStruct((n_layers, r, c), F32)
    return _call(
        body, name, (n_layers, r // tr),
        [pl.BlockSpec((None, n_src, tr, c), lambda l, i: (l, 0, i, 0)), blk, blk, blk],
        [blk, blk, blk, blk],
        [out, out, out, out],
        vmem_bytes=2 * _nbytes((n_src, tr, c), parts.dtype) + 18 * _nbytes((tr, c), F32),
    )(parts, w, m, v)


def _ordered_sum(parts, name):
    n_src, r, c = parts.shape

    def body(p_ref, o_ref):
        acc = p_ref[0]
        for s in range(1, n_src):
            acc = acc + p_ref[s]
        o_ref[...] = acc

    return _call(
        body, name, (1,),
        [pl.BlockSpec((n_src, r, c), lambda i: (0, 0, 0))],
        pl.BlockSpec((r, c), lambda i: (0, 0)),
        jax.ShapeDtypeStruct((r, c), F32),
        vmem_bytes=4 * _nbytes((n_src, r, c), F32),
    )(parts)


def _position():
    return lax.axis_index("x"), lax.axis_index("y"), lax.axis_index("c")


def _linear(p):
    return 4 * p[0] + 2 * p[1] + p[2]


def _all_gather(shards, name):
    n = len(shards)

    def body(*refs):
        ins, outs = refs[:n], refs[n:2 * n]
        send_sems, recv_sems, local_sems = refs[2 * n:]
        x, y, c = _position()
        me, sibling = (x, y, c), (x, y, 1 - c)
        chips = [(1 - x, y), (x, 1 - y), (1 - x, 1 - y)]

        def slab(t, p):
            return outs[t].at[:, _linear(p)]

        def copy(t, k, block, to, src=None):
            return pltpu.make_async_remote_copy(
                src_ref=slab(t, block) if src is None else src,
                dst_ref=slab(t, block),
                send_sem=send_sems.at[t, k],
                recv_sem=recv_sems.at[t, k],
                device_id=to,
                device_id_type=MESH,
            )

        started = []
        for t in range(n):
            mine = pltpu.make_async_copy(ins[t], slab(t, me), local_sems.at[t])
            mine.start()
            started.append(mine)
        sends = []
        for t in range(n):
            first = [copy(t, 0, me, sibling, src=ins[t])]
            first += [copy(t, 1 + j, me, (*chip, c), src=ins[t]) for j, chip in enumerate(chips)]
            for cp in first:
                cp.start()
            sends += first
        for t in range(n):
            for j, chip in enumerate(chips):
                copy(t, 1 + j, (*chip, c), me).wait_recv()
                passed = copy(t, 4 + j, (*chip, c), sibling)
                passed.start()
                sends.append(passed)
        for t in range(n):
            copy(t, 0, sibling, me).wait_recv()
            for j, chip in enumerate(chips):
                copy(t, 4 + j, (*chip, 1 - c), me).wait_recv()
        for cp in sends:
            cp.wait_send()
        for mine in started:
            mine.wait()

    out_shape = [jax.ShapeDtypeStruct((s.shape[0], N_DEV) + s.shape[1:], s.dtype) for s in shards]
    return pl.pallas_call(
        body,
        name=name,
        in_specs=[HBM_SPEC] * n,
        out_specs=[HBM_SPEC] * n,
        out_shape=out_shape,
        scratch_shapes=[
            pltpu.SemaphoreType.DMA((n, N_DEV - 1)),
            pltpu.SemaphoreType.DMA((n, N_DEV - 1)),
            pltpu.SemaphoreType.DMA((n,)),
        ],
    )(*shards)


def _exchange(blocks, name):
    n = len(blocks)

    def body(*refs):
        ins, outs = refs[:n], refs[n:2 * n]
        send_sems, recv_sems, local_sems = refs[2 * n:]
        x, y, c = _position()
        me = _linear((x, y, c))
        flips = [(fx, fy, fc) for fx in (0, 1) for fy in (0, 1) for fc in (0, 1)][1:]

        def peer_of(flip):
            fx, fy, fc = flip
            return (1 - x if fx else x, 1 - y if fy else y, 1 - c if fc else c)

        def copy(t, k, peer):
            return pltpu.make_async_remote_copy(
                src_ref=ins[t].at[:, _linear(peer)],
                dst_ref=outs[t].at[:, me],
                send_sem=send_sems.at[t, k],
                recv_sem=recv_sems.at[t, k],
                device_id=peer,
                device_id_type=MESH,
            )

        def arrival(t, k, peer):
            return pltpu.make_async_remote_copy(
                src_ref=ins[t].at[:, _linear(peer)],
                dst_ref=outs[t].at[:, _linear(peer)],
                send_sem=send_sems.at[t, k],
                recv_sem=recv_sems.at[t, k],
                device_id=peer,
                device_id_type=MESH,
            )

        own = []
        for t in range(n):
            cp = pltpu.make_async_copy(ins[t].at[:, me], outs[t].at[:, me], local_sems.at[t])
            cp.start()
            own.append(cp)
        sends = []
        for t in range(n):
            for k, flip in enumerate(flips):
                cp = copy(t, k, peer_of(flip))
                cp.start()
                sends.append(cp)
        for t in range(n):
            for k, flip in enumerate(flips):
                arrival(t, k, peer_of(flip)).wait_recv()
        for cp in sends:
            cp.wait_send()
        for cp in own:
            cp.wait()

    out_shape = [jax.ShapeDtypeStruct(b.shape, b.dtype) for b in blocks]
    return pl.pallas_call(
        body,
        name=name,
        in_specs=[HBM_SPEC] * n,
        out_specs=[HBM_SPEC] * n,
        out_shape=out_shape,
        scratch_shapes=[
            pltpu.SemaphoreType.DMA((n, N_DEV - 1)),
            pltpu.SemaphoreType.DMA((n, N_DEV - 1)),
            pltpu.SemaphoreType.DMA((n,)),
        ],
    )(*blocks)


def _peers():
    x, y, c = _position()
    flips = [(fx, fy, fc) for fx in (0, 1) for fy in (0, 1) for fc in (0, 1)][1:]
    return [(1 - x if fx else x, 1 - y if fy else y, 1 - c if fc else c) for fx, fy, fc in flips]


SIBLING, OTHER_CHIPS = (0,), (1, 3, 5)
COPY_PEERS = {"gather": tuple(range(N_DEV - 1)), "exchange": tuple(range(N_DEV - 1)),
              "chips": SIBLING + OTHER_CHIPS, "forward": OTHER_CHIPS}


def _split_copy(kind, src_ref, land_ref, k, send_sem, recv_sem, starting):
    peers = _peers()
    peer = peers[SIBLING[0]] if kind == "forward" else peers[k]
    me = _linear(_position())
    if kind == "forward":
        slab = _linear(peers[k]) if starting else 0
        src, dst = land_ref.at[slab], land_ref.at[slab]
    elif kind == "exchange":
        src, dst = src_ref.at[_linear(peer) if starting else 0], land_ref.at[me if starting else 0]
    else:
        src, dst = src_ref, land_ref.at[me if starting else 0]
    return pltpu.make_async_remote_copy(src_ref=src, dst_ref=dst, send_sem=send_sem, recv_sem=recv_sem,
                                        device_id=peer, device_id_type=MESH)


def _split_start(groups, carry, name):
    arrays = [a for _, srcs, lands in groups for a in list(srcs) + list(lands)] + [carry]

    def body(*refs):
        ins, sems = refs[:len(arrays)], refs[len(arrays):len(arrays) + 2 * len(groups)]
        at = 0
        for g, (kind, srcs, lands) in enumerate(groups):
            src_refs, land_refs = ins[at:at + len(srcs)], ins[at + len(srcs):at + len(srcs) + len(lands)]
            at += len(srcs) + len(lands)
            peers = COPY_PEERS[kind]
            for t in range(len(lands)):
                for slot, k in enumerate(peers):
                    sem = t * len(peers) + slot
                    _split_copy(kind, src_refs[t] if srcs else None, land_refs[t], k,
                                sems[2 * g].at[sem], sems[2 * g + 1].at[sem], True).start()

    sem_shapes = [pltpu.SemaphoreType.DMA((len(lands) * len(COPY_PEERS[kind]),))
                  for kind, _, lands in groups for _ in range(2)]
    out = pl.pallas_call(
        body,
        name=name,
        in_specs=[HBM_SPEC] * len(arrays),
        out_specs=[SEM_SPEC] * len(sem_shapes) + [HBM_SPEC] * len(arrays),
        out_shape=sem_shapes + [pltpu.HBM(a.shape, a.dtype) for a in arrays],
        input_output_aliases={i: len(sem_shapes) + i for i in range(len(arrays))},
        compiler_params=pltpu.CompilerParams(has_side_effects=pltpu.SideEffectType.DATAFLOW_SIDE_EFFECTING),
    )(*[pltpu.with_memory_space_constraint(a, pltpu.HBM) for a in arrays])
    sems, thru = out[:len(sem_shapes)], out[len(sem_shapes):]
    started, at = [], 0
    for g, (kind, srcs, lands) in enumerate(groups):
        n_s, n_l = len(srcs), len(lands)
        started.append((kind, sems[2 * g], sems[2 * g + 1], thru[at:at + n_s], thru[at + n_s:at + n_s + n_l]))
        at += n_s + n_l
    return started, thru[-1]


def _split_wait(started, after, name):
    kind, send_sems, recv_sems, srcs, lands = started
    n_s, n_l = len(srcs), len(lands)
    peers = COPY_PEERS[kind]

    def body(*refs):
        src_refs, land_refs = refs[:n_s], refs[n_s:n_s + n_l]
        send_ref, recv_ref = refs[n_s + n_l], refs[n_s + n_l + 1]
        for t in range(n_l):
            for slot, k in enumerate(peers):
                sem = t * len(peers) + slot
                copy = _split_copy(kind, src_refs[t] if n_s else None, land_refs[t], k,
                                   send_ref.at[sem], recv_ref.at[sem], False)
                copy.wait_send()
                copy.wait_recv()

    arrays = list(srcs) + list(lands)
    out = pl.pallas_call(
        body,
        name=name,
        in_specs=[HBM_SPEC] * len(arrays) + [SEM_SPEC, SEM_SPEC, pl.BlockSpec(memory_space=pl.ANY)],
        out_specs=[HBM_SPEC] * len(arrays),
        out_shape=[pltpu.HBM(a.shape, a.dtype) for a in arrays],
        input_output_aliases={i: i for i in range(len(arrays))},
        compiler_params=pltpu.CompilerParams(has_side_effects=pltpu.SideEffectType.DATAFLOW_SIDE_EFFECTING),
    )(*arrays, send_sems, recv_sems, after)
    return out[:n_s], out[n_s:]


def _pack(arrays, row_multiple):
    flat = jnp.concatenate([a.reshape(-1) for a in arrays])
    quantum = row_multiple * FLAT_LANES
    padded = -(-flat.shape[0] // quantum) * quantum
    return jnp.pad(flat, (0, padded - flat.shape[0])).reshape(-1, FLAT_LANES)


def _unpack(flat, like):
    flat = flat.reshape(-1)
    out, at = [], 0
    for a in like:
        size = math.prod(a.shape)
        out.append(flat[at:at + size].reshape(a.shape))
        at += size
    return out


def kernel(x, a_norm, a_w_in, a_sgu_norm, a_w_spatial, a_b_spatial, a_w_out, kv_norm, w_kv, b_norm, b_w_q, b_rel_bias, b_w_o, ffn_norm, ffn_w_gate_up, ffn_w_down, final_norm, loss_target, m_a_norm, m_a_w_in, m_a_sgu_norm, m_a_w_spatial, m_a_b_spatial, m_a_w_out, m_kv_norm, m_w_kv, m_b_norm, m_b_w_q, m_b_rel_bias, m_b_w_o, m_ffn_norm, m_ffn_w_gate_up, m_ffn_w_down, m_final_norm, v_a_norm, v_a_w_in, v_a_sgu_norm, v_a_w_spatial, v_a_b_spatial, v_a_w_out, v_kv_norm, v_w_kv, v_b_norm, v_b_w_q, v_b_rel_bias, v_b_w_o, v_ffn_norm, v_ffn_w_gate_up, v_ffn_w_down, v_final_norm):
    xs = x[0]
    target = loss_target[0]
    t, d = xs.shape
    n_a = a_w_in.shape[0]
    n_b = b_w_q.shape[0]
    depth = ffn_w_gate_up.shape[0]
    f_a = a_w_out.shape[1] * N_DEV
    gd = f_a // A_GROUPS
    nb_ffn = ffn_w_gate_up.shape[2]
    me = _linear(_position())

    small_rows = -(-(a_norm.size + a_sgu_norm.size) // (8 * 128)) * 8
    small = jnp.pad(jnp.concatenate([a_norm.reshape(-1), a_sgu_norm.reshape(-1)]),
                    (0, small_rows * 128 - a_norm.size - a_sgu_norm.size)).reshape(1, small_rows, 128)

    def shard(w, layer=None):
        return (w if layer is None else w[layer]).astype(BF16)

    stages = []
    for layer in range(depth):
        if layer == 0:
            stages += [("a0", [shard(a_w_in, 0)]), ("a0_out", [shard(a_w_out, 0)])]
        elif layer < n_a:
            stages.append((f"a{layer}", [shard(a_w_in, layer), shard(a_w_out, layer)]))
        else:
            i = layer - n_a
            shared = [shard(w_kv)] if i == 0 else []
            stages.append((f"b{i}", shared + [shard(b_w_q, i), shard(b_w_o, i)]))
        stages.append((f"f{layer}", [shard(ffn_w_gate_up, layer), shard(ffn_w_down, layer)]))
    first = _all_gather([s[None] for s in stages[0][1]] + [small], "gather_first")
    gathered = {stages[0][0]: [g[0] for g in first[:-1]]}
    small_g = first[-1].reshape(N_DEV, -1)
    a_norm_full = small_g[:, :a_norm.size].reshape(N_DEV, n_a, -1).transpose(1, 0, 2).reshape(n_a, d)
    a_sgu_full = small_g[:, a_norm.size:a_norm.size + a_sgu_norm.size].reshape(
        N_DEV, n_a, -1).transpose(1, 0, 2).reshape(n_a, f_a)
    two_level = ("f0", "a1", "f1")
    later = [("chips" if key in two_level else "gather", shards,
              [lax.dynamic_update_slice(lax.empty((N_DEV,) + s.shape, BF16), s[None], (me, 0, 0)) for s in shards])
             for key, shards in stages[1:]]
    started, a_norm_full = _split_start(later, a_norm_full, "gather_start")
    in_flight = {key: group for (key, _), group in zip(stages[1:], started)}

    def pass_on(key, carry):
        if key in two_level and key in in_flight and in_flight[key][0] == "chips":
            _, lands = _split_wait(in_flight.pop(key), carry, f"gather_wait_{key}_chips")
            (in_flight[key],), carry = _split_start([("forward", [], lands)], carry, f"gather_pass_on_{key}")
        return carry

    def weights(key, after):
        if key not in gathered:
            _, gathered[key] = _split_wait(in_flight.pop(key), after, f"gather_wait_{key}")
        return gathered[key]

    rows_down = ffn_w_down.shape[1]

    def mixer_a_weights(i, after):
        if i == 0:
            (w_in,), (w_out,) = weights("a0", after[0]), weights("a0_out", after[1])
        else:
            w_in, w_out = weights(f"a{i}", after[0])
        return w_in[None], w_out.reshape(1, f_a, d)

    def mixer_b_weights(i, after):
        ws = weights(f"b{i}", after)
        return ws[-2].reshape(1, d, d), ws[-1].reshape(1, d, d)

    def ffn_weights(layer, after):
        w_gu, w_dn = weights(f"f{layer}", after)
        return w_gu[None], w_dn.reshape(1, N_DEV // 2, 2 * rows_down, d)

    w_sp_t = jnp.swapaxes(a_w_spatial, -1, -2)
    b_full = jnp.repeat(jnp.swapaxes(a_b_spatial, -1, -2), gd, axis=-1)

    saved = []

    def ffn_fwd(xin, layer):
        hf = _rms_fwd(xin, ffn_norm[layer], f"ffn_norm_fwd_{layer}")
        w_gu, w_dn = ffn_weights(layer, xin)
        dact, act = _ffn_gate_up(f"ffn_gate_up_{layer}", hf, w_gu, 0)
        act = pass_on(f"a{layer + 1}", act)
        xout = _mm_down(f"ffn_down_{layer}", act, w_dn, 0, xin)
        return xout, (xin, hf, dact, act)

    for i in range(n_a):
        h = _rms_fwd(xs, a_norm_full[i], f"a_norm_fwd_{i}")
        zpre = _mm_colblock(f"a_in_{i}", h, weights(f"a{i}", xs)[0][None], 0)
        p, zs, dgs = _sgu_fwd(zpre, a_sgu_full[i], a_w_spatial[i], b_full[i], f"a_sgu_fwd_{i}")
        p = pass_on(f"f{i}", p)
        w_in, w_out = mixer_a_weights(i, (xs, p))
        x_mid = _mm_natural(f"a_out_{i}", p, w_out, 0, res=xs)
        x_out, ffn_saved = ffn_fwd(x_mid, i)
        saved.append((xs, h, zs, dgs, p, ffn_saved))
        xs = x_out

    x_kv = xs
    w_kv_g = weights("b0", x_kv)[0][None]
    h_kv = _rms_fwd(x_kv, kv_norm, "kv_norm_fwd")
    kv = _mm_colblock("kv_proj", h_kv, w_kv_g, 0)
    kvpad = jnp.pad(kv, ((LEFT, 0), (0, 0)))

    biases = [_bias_block(_bias_build(b_rel_bias[i], f"rel_bias_{i}")) for i in range(n_b)]
    for i in range(n_b):
        layer = n_a + i
        w_q, w_o = mixer_b_weights(i, xs)
        hb = _rms_fwd(xs, b_norm[i], f"b_norm_fwd_{i}")
        q = _mm_natural(f"b_q_{i}", hb, w_q, 0, out_dtype=BF16, scale=ATTN_SCALE)
        o = _attn_fwd(q, kvpad, biases[i], f"b_attn_fwd_{i}")
        x_mid = _mm_natural(f"b_o_{i}", o, w_o, 0, res=xs)
        x_out, ffn_saved = ffn_fwd(x_mid, layer)
        saved.append((xs, hb, q, o, ffn_saved))
        xs = x_out

    dx, loss_local, g_final = _loss_head(xs, final_norm, target, "loss_head")
    loss = lax.psum(loss_local, ("x", "y", "c"))

    big_grads = {}
    pending = []
    in_flight_grads = []

    def start_exchange(dx, tag):
        srcs = [big_grads[key] for key in pending]
        lands = [lax.empty(s.shape, BF16) for s in srcs]
        (group,), dx = _split_start([("exchange", srcs, lands)], dx, f"exchange_start_{tag}")
        in_flight_grads.append((list(pending), group, tag))
        pending.clear()
        return dx

    g_ffn_norm = [None] * depth
    g_a_norm = [None] * n_a
    g_a_sgu = [None] * n_a
    g_w_sp = [None] * n_a
    g_b_sp = [None] * n_a
    g_b_norm = [None] * n_b
    g_rel = [None] * n_b

    def ffn_bwd(dx, layer, ffn_saved):
        eager = layer < n_a
        xin, hf, dact, act = ffn_saved
        big_grads["ffn_w_down", layer] = _mm_dw_down(f"ffn_down_dw_{layer}", act, dx)
        pending.append(("ffn_w_down", layer))
        if eager:
            dx = start_exchange(dx, f"f{layer}_down")
        w_gu, w_dn = ffn_weights(layer, xin)
        dgu = _ffn_down_dx(f"ffn_down_dx_{layer}", dx, w_dn, 0, dact).reshape(N_DEV, t, nb_ffn)
        big_grads["ffn_w_gate_up", layer] = _mm_dw_colblock(
            f"ffn_gate_up_dw_{layer}", hf, dgu, blocked_in=True, transposed=True)
        pending.append(("ffn_w_gate_up", layer))
        if eager:
            dx = start_exchange(dx, f"f{layer}_gate_up")
        dx, g_ffn_norm[layer] = _mm_t_colblock_norm_bwd(
            f"ffn_gate_up_dx_{layer}", dgu, w_gu, 0, xin, ffn_norm[layer], dx, blocked_in=True)
        return dx

    dk = dv = None
    for i in reversed(range(n_b)):
        layer = n_a + i
        x_in, hb, q, o, ffn_saved = saved[layer]
        dx = ffn_bwd(dx, layer, ffn_saved)
        big_grads["b_w_o", i] = _mm_dw_natural(f"b_o_dw_{i}", o, dx)
        w_q, w_o = mixer_b_weights(i, x_in)
        do = _mm_t_natural(f"b_o_dx_{i}", dx, w_o, 0)
        dq, dk, dv, dbias = _attn_bwd(q, kvpad, biases[i], o, do, dk, dv, f"b_attn_bwd_{i}")
        g_rel[i] = _bias_grad(dbias, f"rel_bias_grad_{i}")
        big_grads["b_w_q", i] = _mm_dw_natural(f"b_q_dw_{i}", hb, dq)
        pending.extend([("b_w_o", i), ("b_w_q", i)])
        dx, g_b_norm[i] = _mm_t_natural_norm_bwd(f"b_q_dx_{i}", dq, w_q, 0, x_in, b_norm[i], dx)
        if i > 0:
            dx = start_exchange(dx, f"b{i}")

    dkv = jnp.concatenate([dk[LEFT:], dv[LEFT:]], axis=1).astype(BF16)
    big_grads["w_kv", 0] = _mm_dw_colblock("kv_proj_dw", h_kv, dkv)
    pending.append(("w_kv", 0))
    dx, g_kv_norm = _mm_t_colblock_norm_bwd("kv_proj_dx", dkv, w_kv_g, 0, x_kv, kv_norm, dx)
    dx = start_exchange(dx, "kv")

    for i in reversed(range(n_a)):
        x_in, h, zs, dgs, p, ffn_saved = saved[i]
        dx = ffn_bwd(dx, i, ffn_saved)
        big_grads["a_w_out", i] = _mm_dw_natural(f"a_out_dw_{i}", p, dx)
        pending.append(("a_w_out", i))
        dx = start_exchange(dx, f"a{i}_out")
        w_in, w_out = mixer_a_weights(i, (x_in, p))
        dp = _mm_t_natural(f"a_out_dx_{i}", dx, w_out, 0)
        dz, g_w_sp[i], g_b_sp[i], g_a_sgu[i] = _sgu_bwd(
            zs, dgs, dp, a_sgu_full[i], a_w_spatial[i], w_sp_t[i], b_full[i], f"a_sgu_bwd_{i}")
        big_grads["a_w_in", i] = _mm_dw_colblock(f"a_in_dw_{i}", h, dz)
        pending.append(("a_w_in", i))
        dx = start_exchange(dx, f"a{i}_in")
        dx, g_a_norm[i] = _mm_t_colblock_norm_bwd(f"a_in_dx_{i}", dz, w_in, 0, x_in, a_norm_full[i], dx)
    grad_x = dx[None]

    small_like = [jax.ShapeDtypeStruct((n_a, d), F32), jax.ShapeDtypeStruct((n_a, f_a), F32),
                  a_w_spatial, a_b_spatial, kv_norm, b_norm, b_rel_bias, ffn_norm, final_norm]
    small_partial = _pack(
        [jnp.stack(g_a_norm), jnp.stack(g_a_sgu), jnp.stack(g_w_sp), jnp.stack(g_b_sp), g_kv_norm,
         jnp.stack(g_b_norm), jnp.stack(g_rel), jnp.stack(g_ffn_norm), g_final], N_DEV * 8)
    chunk_rows = small_partial.shape[0] // N_DEV
    arrived = {}
    for keys, group, tag in in_flight_grads:
        srcs, lands = _split_wait(group, dx, f"exchange_wait_{tag}")
        for key, src, land in zip(keys, srcs, lands):
            arrived[key] = (land, src)
    small_got = _exchange([small_partial.reshape(1, N_DEV, chunk_rows, FLAT_LANES)], "exchange_small")[0]
    small_sum = _ordered_sum(small_got[0], "small_grad_sum")
    small_all = _all_gather([small_sum[None]], "gather_small_grads")[0]
    (ga_norm, ga_sgu, gw_sp, gb_sp, gkv_norm, gb_norm, g_relb, gffn_norm, gfinal) = _unpack(small_all, small_like)

    results = {}
    big_names = ["a_w_in", "a_w_out", "w_kv", "b_w_q", "b_w_o", "ffn_w_gate_up", "ffn_w_down"]
    big_wmv = [(a_w_in, m_a_w_in, v_a_w_in), (a_w_out, m_a_w_out, v_a_w_out),
               (w_kv[None], m_w_kv[None], v_w_kv[None]), (b_w_q, m_b_w_q, v_b_w_q), (b_w_o, m_b_w_o, v_b_w_o),
               tuple(jnp.swapaxes(a, 1, 2) for a in (ffn_w_gate_up, m_ffn_w_gate_up, v_ffn_w_gate_up)),
               (ffn_w_down, m_ffn_w_down, v_ffn_w_down)]
    me_arr = jnp.reshape(me, (1,)).astype(jnp.int32)
    for name, (w, m, v) in zip(big_names, big_wmv):
        outs = None
        for layer in range(w.shape[0]):
            got, own = arrived[name, layer]
            outs = _adamw_layer(got, own, w, m, v, layer, outs, me_arr, f"adamw_{name}_{layer}")
        if name == "w_kv":
            outs = [o[0] for o in outs]
        if name == "ffn_w_gate_up":
            outs = [jnp.swapaxes(o, 1, 2) for o in outs]
        results[name] = outs

    n_cols = a_norm.shape[1]
    s_cols = a_sgu_norm.shape[1]
    small_g_list = [lax.dynamic_slice(ga_norm, (0, me * n_cols), (n_a, n_cols)),
                    lax.dynamic_slice(ga_sgu, (0, me * s_cols), (n_a, s_cols)),
                    gw_sp, gb_sp, gkv_norm, gb_norm, g_relb, gffn_norm, gfinal]
    small_names = ["a_norm", "a_sgu_norm", "a_w_spatial", "a_b_spatial", "kv_norm", "b_norm", "b_rel_bias",
                   "ffn_norm", "final_norm"]
    small_w = [a_norm, a_sgu_norm, a_w_spatial, a_b_spatial, kv_norm, b_norm, b_rel_bias, ffn_norm, final_norm]
    small_m = [m_a_norm, m_a_sgu_norm, m_a_w_spatial, m_a_b_spatial, m_kv_norm, m_b_norm, m_b_rel_bias,
               m_ffn_norm, m_final_norm]
    small_v = [v_a_norm, v_a_sgu_norm, v_a_w_spatial, v_a_b_spatial, v_kv_norm, v_b_norm, v_b_rel_bias,
               v_ffn_norm, v_final_norm]
    flat_g = _pack(small_g_list, 8)
    flat_out = _adamw(flat_g[None, None], _pack(small_w, 8)[None], _pack(small_m, 8)[None],
                      _pack(small_v, 8)[None], "adamw_small")
    unpacked = [_unpack(o[0], small_w) for o in flat_out]
    for idx, name in enumerate(small_names):
        results[name] = [unpacked[kind][idx] for kind in range(4)]

    order = ["a_norm", "a_w_in", "a_sgu_norm", "a_w_spatial", "a_b_spatial", "a_w_out", "kv_norm", "w_kv",
             "b_norm", "b_w_q", "b_rel_bias", "b_w_o", "ffn_norm", "ffn_w_gate_up", "ffn_w_down", "final_norm"]
    outputs = [loss, grad_x]
    for kind in range(4):
        outputs += [results[name][kind] for name in order]
    return tuple(outputs)
```

```python
import math

import jax
import jax.numpy as jnp
from jax import lax
from jax.experimental import pallas as pl
from jax.experimental.pallas import tpu as pltpu

F32 = jnp.float32
BF16 = jnp.bfloat16
MESH = pl.DeviceIdType.MESH
HBM_SPEC = pl.BlockSpec(memory_space=pltpu.HBM)
SEM_SPEC = pl.BlockSpec(memory_space=pltpu.SEMAPHORE)

N_DEV = 8
CHUNK = 64
A_CHUNK = 128
A_GROUPS = 8
N_LEFT_CHUNKS = 8
LEFT = N_LEFT_CHUNKS * CHUNK
PAIR_ROWS = 2 * CHUNK
PAIR_BAND = PAIR_ROWS + LEFT
DIAGONALS = PAIR_BAND + PAIR_ROWS
PAIRS_PER_BLOCK = 2
Q_BLOCK = PAIRS_PER_BLOCK * PAIR_ROWS
K_BLOCK = Q_BLOCK + LEFT
ATTN_UNROLL = 7
MAX_REL = 256
N_REL = 2 * MAX_REL + 1
REL_PAD = 640
HEAD_DIM = 64
HEAD_PAIR = 2 * HEAD_DIM
ATTN_SCALE = HEAD_DIM ** -0.5
EPS = 1e-6
NEG_INF = -1e30
ADAM_LR = 0.001
ADAM_B1 = 0.9
ADAM_B2 = 0.999
ADAM_EPS = 1e-08
ADAM_WD = 0.01
ADAM_STEP = 10
FLAT_LANES = 1024
F32_SUBLANES = 8
BF16_SUBLANES = 16
ADAMW_BLOCK_ELEMS = 256 * 1024
V7X_VMEM_BYTES = 64 * 1024 * 1024
VMEM_FLOOR_BYTES = 32 * 1024 * 1024
VMEM_CEIL_BYTES = V7X_VMEM_BYTES - 8 * 1024 * 1024

NN = (((1,), (0,)), ((), ()))
NT = (((1,), (1,)), ((), ()))
TN = (((0,), (0,)), ((), ()))


def _tile(n, pref):
    return pref if n % pref == 0 else n


def _row_tile(n, pref, mult):
    best = None
    for t in range(mult, min(n, pref) + 1, mult):
        if n % t == 0:
            best = t
    return best if best is not None else n


def _nbytes(shape, dtype):
    n = 1
    for s in shape:
        if s is not None:
            n *= s
    return n * jnp.dtype(dtype).itemsize


def _call(body, name, grid, in_specs, out_specs, out_shape, scratch=(), vmem_bytes=0, aliases=None):
    limit = int(min(max(VMEM_FLOOR_BYTES, vmem_bytes * 5 // 4), VMEM_CEIL_BYTES))
    return pl.pallas_call(
        body,
        name=name,
        grid=grid,
        in_specs=in_specs,
        out_specs=out_specs,
        out_shape=out_shape,
        scratch_shapes=list(scratch),
        input_output_aliases=aliases or {},
        compiler_params=pltpu.CompilerParams(
            dimension_semantics=("arbitrary",) * len(grid), vmem_limit_bytes=limit),
    )


ERFC_P = 0.3275911 / math.sqrt(2.0)
ERFC_HALF_COEFFS = tuple(0.5 * a for a in (1.061405429, -1.453152027, 1.421413741, -0.284496736, 0.254829592))


def _gelu_and_grad(x):
    d = 1.0 + ERFC_P * jnp.abs(x)
    r = pl.reciprocal(d, approx=True)
    t = r * (2.0 - d * r)
    a5, a4, a3, a2, a1 = ERFC_HALF_COEFFS
    ex = jnp.exp(-0.5 * (x * x))
    tail = ((((a5 * t + a4) * t + a3) * t + a2) * t + a1) * t * ex
    cdf = jnp.where(x < 0, tail, 1.0 - tail)
    return x * cdf, cdf + x * ex * (1.0 / math.sqrt(2.0 * math.pi))


def _sigmoid(x):
    return 1.0 / (1.0 + jnp.exp(-x))


def _split3(x):
    hi = x.astype(BF16)
    r1 = x - hi.astype(F32)
    mid = r1.astype(BF16)
    lo = (r1 - mid.astype(F32)).astype(BF16)
    return hi, mid, lo


def _rms_fwd(x, g, name):
    t, d = x.shape
    tm = _tile(t, 512)

    def body(x_ref, g_ref, o_ref):
        xf = x_ref[...]
        r = lax.rsqrt(jnp.mean(xf * xf, axis=-1, keepdims=True) + EPS)
        o_ref[...] = (xf * r * g_ref[...]).astype(o_ref.dtype)

    return _call(
        body, name, (t // tm,),
        [pl.BlockSpec((tm, d), lambda i: (i, 0)), pl.BlockSpec((1, d), lambda i: (0, 0))],
        pl.BlockSpec((tm, d), lambda i: (i, 0)),
        jax.ShapeDtypeStruct((t, d), BF16),
        vmem_bytes=2 * (_nbytes((tm, d), F32) + _nbytes((tm, d), BF16)) + 4 * _nbytes((tm, d), F32),
    )(x, g.reshape(1, d))


def _mm(name, dims, a, b, *, grid, a_spec, b_spec, out_shape, out_spec, acc_shape,
        res=None, res_spec=None, scale=None):
    nk = grid[2]
    has_res = res is not None

    def body(*refs):
        refs = list(refs)
        a_ref = refs.pop(0)
        b_ref = refs.pop(0)
        r_ref = refs.pop(0) if has_res else None
        o_ref = refs.pop(0)
        part = lax.dot_general(a_ref[...].astype(BF16), b_ref[...].astype(BF16), dims,
                               preferred_element_type=F32)

        def finish(acc):
            if scale is not None:
                acc = acc * scale
            if has_res:
                acc = acc + r_ref[...]
            o_ref[...] = acc.astype(o_ref.dtype)

        if nk == 1:
            finish(part)
        else:
            acc_ref = refs.pop(0)
            k = pl.program_id(2)

            @pl.when(k == 0)
            def _():
                acc_ref[...] = part

            @pl.when(k > 0)
            def _():
                acc_ref[...] += part

            @pl.when(k == nk - 1)
            def _():
                finish(acc_ref[...])

    operands = [a, b]
    in_specs = [a_spec, b_spec]
    vmem = 2 * (_nbytes(a_spec.block_shape, a.dtype) + _nbytes(b_spec.block_shape, b.dtype)
                + _nbytes(out_spec.block_shape, out_shape.dtype))
    vmem += 3 * _nbytes(acc_shape, F32)
    if has_res:
        operands.append(res)
        in_specs.append(res_spec)
        vmem += 2 * _nbytes(res_spec.block_shape, res.dtype)
    scratch = [pltpu.VMEM(acc_shape, F32)] if nk > 1 else []
    return _call(body, name, grid, in_specs, out_spec, out_shape, scratch=scratch, vmem_bytes=vmem)(*operands)


def _mm_colblock(name, h, w_g, layer):
    t, k = h.shape
    nb = w_g.shape[3]
    tm = _tile(t, 2048)
    return _mm(
        name, NN, h, w_g, grid=(t // tm, N_DEV, 1),
        a_spec=pl.BlockSpec((tm, k), lambda i, j, kk: (i, 0)),
        b_spec=pl.BlockSpec((None, None, k, nb), lambda i, j, kk: (layer, j, 0, 0)),
        out_shape=jax.ShapeDtypeStruct((t, N_DEV * nb), BF16),
        out_spec=pl.BlockSpec((tm, nb), lambda i, j, kk: (i, j)), acc_shape=(tm, nb))


def _mm_natural(name, a, w, layer, *, res=None, out_dtype=F32, scale=None):
    t, k = a.shape
    n = w.shape[2]
    tm = _tile(t, 1024)
    tn = _tile(n, 1024 if k <= 1024 else 512)
    res_spec = None if res is None else pl.BlockSpec((tm, tn), lambda i, j, kk: (i, j))
    return _mm(
        name, NN, a, w, grid=(t // tm, n // tn, 1),
        a_spec=pl.BlockSpec((tm, k), lambda i, j, kk: (i, 0)),
        b_spec=pl.BlockSpec((None, k, tn), lambda i, j, kk: (layer, 0, j)),
        out_shape=jax.ShapeDtypeStruct((t, n), out_dtype),
        out_spec=pl.BlockSpec((tm, tn), lambda i, j, kk: (i, j)),
        acc_shape=(tm, tn), res=res, res_spec=res_spec, scale=scale)


def _mm_down(name, act, w4, layer, res):
    nblk, t, kb = act.shape
    n = w4.shape[3]
    tm = _tile(t, 1024)

    def body(a_ref, b_ref, r_ref, o_ref):
        acc = r_ref[...]
        for u in range(nblk):
            acc = acc + jnp.dot(a_ref[u], b_ref[u], preferred_element_type=F32)
        o_ref[...] = acc

    row = pl.BlockSpec((tm, n), lambda i: (i, 0))
    return _call(
        body, name, (t // tm,),
        [pl.BlockSpec((nblk, tm, kb), lambda i: (0, i, 0)),
         pl.BlockSpec((None, nblk, kb, n), lambda i: (layer, 0, 0, 0)),
         row],
        row,
        jax.ShapeDtypeStruct((t, n), F32),
        vmem_bytes=2 * (_nbytes((nblk, tm, kb), BF16) + _nbytes((nblk, kb, n), BF16)) + 6 * _nbytes((tm, n), F32),
    )(act, w4, res)


def _mm_t_colblock_norm_bwd(name, dz, w_g, layer, x, g, dx_up, blocked_in=False):
    k = w_g.shape[2]
    nb = w_g.shape[3]
    t = x.shape[0]
    tm = _tile(t, 512)
    per_step = N_DEV
    n_steps = N_DEV // per_step
    if blocked_in:
        a_spec = pl.BlockSpec((per_step, tm, nb), lambda i, kk: (kk, i, 0))
    else:
        a_spec = pl.BlockSpec((tm, per_step * nb), lambda i, kk: (i, kk))

    def body(a_ref, b_ref, x_ref, g_ref, up_ref, dx_ref, dg_ref, acc_ref):
        i = pl.program_id(0)
        kk = pl.program_id(1)
        part = None
        for u in range(per_step):
            a = a_ref[u] if blocked_in else a_ref[:, u * nb:(u + 1) * nb]
            term = lax.dot_general(a.astype(BF16), b_ref[u].astype(BF16), NT, preferred_element_type=F32)
            part = term if part is None else part + term

        @pl.when(kk == 0)
        def _():
            acc_ref[...] = part

        @pl.when(kk > 0)
        def _():
            acc_ref[...] += part

        @pl.when((i == 0) & (kk == 0))
        def _():
            dg_ref[...] = jnp.zeros_like(dg_ref)

        @pl.when(kk == n_steps - 1)
        def _():
            dy = acc_ref[...]
            xf = x_ref[...]
            r = lax.rsqrt(jnp.mean(xf * xf, axis=-1, keepdims=True) + EPS)
            xhat = xf * r
            dxhat = dy * g_ref[...]
            dg_ref[...] += jnp.sum(dy * xhat, axis=0, keepdims=True)
            dx_ref[...] = up_ref[...] + r * (dxhat - xhat * jnp.mean(dxhat * xhat, axis=-1, keepdims=True))

    row = pl.BlockSpec((tm, k), lambda i, kk: (i, 0))
    vec = pl.BlockSpec((1, k), lambda i, kk: (0, 0))
    dx, dg = _call(
        body, name, (t // tm, n_steps),
        [a_spec, pl.BlockSpec((None, per_step, k, nb), lambda i, kk: (layer, kk, 0, 0)), row, vec, row],
        [row, vec],
        [jax.ShapeDtypeStruct((t, k), F32), jax.ShapeDtypeStruct((1, k), F32)],
        scratch=[pltpu.VMEM((tm, k), F32)],
        vmem_bytes=2 * per_step * (_nbytes((tm, nb), BF16) + _nbytes((k, nb), BF16)) + 10 * _nbytes((tm, k), F32),
    )(dz, w_g, x, g.reshape(1, k), dx_up)
    return dx, dg.reshape(k)


def _ffn_gate_up(name, h, w_g, layer):
    t, k = h.shape
    nb = w_g.shape[3]
    half = N_DEV // 2
    tm = _tile(t, 1024)

    def body(h_ref, wg_ref, wu_ref, dact_ref, act_ref):
        hb = h_ref[...]
        gate = jnp.dot(hb, wg_ref[...], preferred_element_type=F32)
        up = jnp.dot(hb, wu_ref[...], preferred_element_type=F32)
        sig = _sigmoid(gate)
        silu = gate * sig
        dact_ref[0] = (up * (sig * (1.0 + gate * (1.0 - sig)))).astype(BF16)
        dact_ref[1] = silu.astype(BF16)
        act_ref[...] = (silu * up).astype(BF16)

    return _call(
        body, name, (t // tm, half),
        [pl.BlockSpec((tm, k), lambda i, j: (i, 0)),
         pl.BlockSpec((None, None, k, nb), lambda i, j: (layer, j, 0, 0)),
         pl.BlockSpec((None, None, k, nb), lambda i, j: (layer, half + j, 0, 0))],
        [pl.BlockSpec((2, None, tm, nb), lambda i, j: (0, j, i, 0)),
         pl.BlockSpec((None, tm, nb), lambda i, j: (j, i, 0))],
        [jax.ShapeDtypeStruct((2, half, t, nb), BF16), jax.ShapeDtypeStruct((half, t, nb), BF16)],
        vmem_bytes=2 * (_nbytes((tm, k), BF16) + 2 * _nbytes((k, nb), BF16) + 3 * _nbytes((tm, nb), BF16))
        + 8 * _nbytes((tm, nb), F32),
    )(h, w_g, w_g)


def _ffn_down_dx(name, dy, w4, layer, dact):
    t, n = dy.shape
    nblk, kb = w4.shape[1], w4.shape[2]
    tm = _tile(t, 1024)

    def body(dy_ref, w_ref, dact_ref, dgu_ref):
        da = lax.dot_general(dy_ref[...].astype(BF16), w_ref[...], NT, preferred_element_type=F32)
        dgu_ref[0] = (da * dact_ref[0].astype(F32)).astype(BF16)
        dgu_ref[1] = (da * dact_ref[1].astype(F32)).astype(BF16)

    blk = pl.BlockSpec((2, None, tm, kb), lambda i, j: (0, j, i, 0))
    return _call(
        body, name, (t // tm, nblk),
        [pl.BlockSpec((tm, n), lambda i, j: (i, 0)),
         pl.BlockSpec((None, None, kb, n), lambda i, j: (layer, j, 0, 0)),
         blk],
        blk,
        jax.ShapeDtypeStruct((2, nblk, t, kb), BF16),
        vmem_bytes=2 * (_nbytes((tm, n), F32) + _nbytes((kb, n), BF16) + 4 * _nbytes((tm, kb), BF16))
        + 8 * _nbytes((tm, kb), F32),
    )(dy, w4, dact)


def _mm_t_natural(name, dy, w, layer):
    t, n = dy.shape
    k = w.shape[1]
    tm = _tile(t, 1024)
    tk = _tile(k, 1024)
    return _mm(
        name, NT, dy, w, grid=(t // tm, k // tk, 1),
        a_spec=pl.BlockSpec((tm, n), lambda i, j, kk: (i, 0)),
        b_spec=pl.BlockSpec((None, tk, n), lambda i, j, kk: (layer, j, 0)),
        out_shape=jax.ShapeDtypeStruct((t, k), BF16),
        out_spec=pl.BlockSpec((tm, tk), lambda i, j, kk: (i, j)),
        acc_shape=(tm, tk))


def _mm_t_natural_norm_bwd(name, dy, w, layer, x, g, dx_up):
    t, n = dy.shape
    k = w.shape[1]
    tm = _tile(t, 1024)

    def body(a_ref, b_ref, x_ref, g_ref, up_ref, dx_ref, dg_ref):
        @pl.when(pl.program_id(0) == 0)
        def _():
            dg_ref[...] = jnp.zeros_like(dg_ref)

        dh = lax.dot_general(a_ref[...].astype(BF16), b_ref[...], NT, preferred_element_type=F32)
        xf = x_ref[...]
        r = lax.rsqrt(jnp.mean(xf * xf, axis=-1, keepdims=True) + EPS)
        xhat = xf * r
        dxhat = dh * g_ref[...]
        dg_ref[...] += jnp.sum(dh * xhat, axis=0, keepdims=True)
        dx_ref[...] = up_ref[...] + r * (dxhat - xhat * jnp.mean(dxhat * xhat, axis=-1, keepdims=True))

    row = pl.BlockSpec((tm, k), lambda i: (i, 0))
    vec = pl.BlockSpec((1, k), lambda i: (0, 0))
    dx, dg = _call(
        body, name, (t // tm,),
        [pl.BlockSpec((tm, n), lambda i: (i, 0)), pl.BlockSpec((None, k, n), lambda i: (layer, 0, 0)), row, vec, row],
        [row, vec],
        [jax.ShapeDtypeStruct((t, k), F32), jax.ShapeDtypeStruct((1, k), F32)],
        vmem_bytes=2 * (_nbytes((tm, n), dy.dtype) + _nbytes((k, n), BF16)) + 10 * _nbytes((tm, k), F32),
    )(dy, w, x, g.reshape(1, k), dx_up)
    return dx, dg.reshape(k)


def _mm_dw_colblock(name, h, dz, blocked_in=False, transposed=False):
    t, k = h.shape
    nb = dz.shape[2] if blocked_in else dz.shape[1] // N_DEV
    tk = _tile(t, 4096)
    h_spec = pl.BlockSpec((tk, k), lambda i, j, kk: (kk, 0))
    if blocked_in:
        dz_spec = pl.BlockSpec((None, tk, nb), lambda i, j, kk: (j, kk, 0))
    else:
        dz_spec = pl.BlockSpec((tk, nb), lambda i, j, kk: (kk, j))
    rows, cols = (nb, k) if transposed else (k, nb)
    return _mm(
        name, TN, *((dz, h) if transposed else (h, dz)), grid=(1, N_DEV, t // tk),
        a_spec=dz_spec if transposed else h_spec,
        b_spec=h_spec if transposed else dz_spec,
        out_shape=jax.ShapeDtypeStruct((N_DEV, rows, cols), BF16),
        out_spec=pl.BlockSpec((None, rows, cols), lambda i, j, kk: (j, 0, 0)),
        acc_shape=(rows, cols))


def _mm_dw_natural(name, a, dy):
    t, k = a.shape
    n = dy.shape[1]
    tko = _tile(k, 1024)
    tt = _tile(t, 2048)
    out = _mm(
        name, TN, a, dy, grid=(k // tko, 1, t // tt),
        a_spec=pl.BlockSpec((tt, tko), lambda i, j, kk: (kk, i)),
        b_spec=pl.BlockSpec((tt, n), lambda i, j, kk: (kk, 0)),
        out_shape=jax.ShapeDtypeStruct((k, n), BF16),
        out_spec=pl.BlockSpec((tko, n), lambda i, j, kk: (i, 0)),
        acc_shape=(tko, n))
    return out.reshape(N_DEV, k // N_DEV, n)


def _mm_dw_down(name, act, dy):
    nblk, t, kb = act.shape
    n = dy.shape[1]
    tt = _tile(t, 2048)
    out = _mm(
        name, TN, act, dy, grid=(nblk, 1, t // tt),
        a_spec=pl.BlockSpec((None, tt, kb), lambda i, j, kk: (i, kk, 0)),
        b_spec=pl.BlockSpec((tt, n), lambda i, j, kk: (kk, 0)),
        out_shape=jax.ShapeDtypeStruct((nblk, kb, n), BF16),
        out_spec=pl.BlockSpec((None, kb, n), lambda i, j, kk: (i, 0, 0)),
        acc_shape=(kb, n))
    return out.reshape(N_DEV, (nblk * kb) // N_DEV, n)


def _spatial_mask(transposed=False):
    r = lax.broadcasted_iota(jnp.int32, (A_CHUNK, A_CHUNK), 0) // CHUNK
    c = lax.broadcasted_iota(jnp.int32, (A_CHUNK, A_CHUNK), 1) // CHUNK
    return c >= r if transposed else r >= c


def _sgu_tile(t):
    return _tile(t, 2 * A_CHUNK)


def _sgu_fwd(zpre, g_sgu, w_sp, b_full, name):
    t, f2 = zpre.shape
    f = f2 // 2
    gd = f // A_GROUPS
    tm = _sgu_tile(t)

    def body(z_ref, g_ref, w_ref, b_ref, p_ref, zs_ref, dg_ref):
        mask = _spatial_mask()
        wm = [jnp.where(mask, w_ref[g], 0.0).astype(BF16) for g in range(A_GROUPS)]
        for c in range(tm // A_CHUNK):
            rows = pl.ds(c * A_CHUNK, A_CHUNK)
            z, dgelu = _gelu_and_grad(z_ref[rows, :].astype(F32))
            zs_ref[rows, :] = z.astype(BF16)
            dg_ref[rows, :] = dgelu.astype(BF16)
            u = z[:, :f]
            v0 = z[:, f:]
            r = lax.rsqrt(jnp.mean(v0 * v0, axis=-1, keepdims=True) + EPS)
            v1 = (v0 * r * g_ref[...]).astype(BF16)
            for g in range(A_GROUPS):
                cols = slice(g * gd, (g + 1) * gd)
                v2 = jnp.dot(wm[g], v1[:, cols], preferred_element_type=F32) + b_ref[:, cols]
                p_ref[rows, cols] = (u[:, cols] * v2).astype(BF16)

    return _call(
        body, name, (t // tm,),
        [pl.BlockSpec((tm, f2), lambda i: (i, 0)),
         pl.BlockSpec((1, f), lambda i: (0, 0)),
         pl.BlockSpec((A_GROUPS, A_CHUNK, A_CHUNK), lambda i: (0, 0, 0)),
         pl.BlockSpec((A_CHUNK, f), lambda i: (0, 0))],
        [pl.BlockSpec((tm, f), lambda i: (i, 0)), pl.BlockSpec((tm, f2), lambda i: (i, 0)),
         pl.BlockSpec((tm, f2), lambda i: (i, 0))],
        [jax.ShapeDtypeStruct((t, f), BF16), jax.ShapeDtypeStruct((t, f2), BF16), jax.ShapeDtypeStruct((t, f2), BF16)],
        vmem_bytes=6 * _nbytes((tm, f2), BF16) + 2 * _nbytes((tm, f), BF16) + 8 * _nbytes((A_CHUNK, f2), F32),
    )(zpre, g_sgu.reshape(1, f), w_sp, b_full)


def _sgu_bwd(zs, dgs, dp, g_sgu, w_sp, w_sp_t, b_full, name):
    t, f2 = zs.shape
    f = f2 // 2
    gd = f // A_GROUPS
    tm = _sgu_tile(t)
    n_steps = t // tm

    def body(z_ref, dgelu_ref, dp_ref, g_ref, w_ref, wt_ref, b_ref, dz_ref, dw_ref, db_ref, dg_ref, dv1_ref, dbf_ref):
        step = pl.program_id(0)

        @pl.when(step == 0)
        def _():
            dw_ref[...] = jnp.zeros_like(dw_ref)
            dg_ref[...] = jnp.zeros_like(dg_ref)
            dbf_ref[...] = jnp.zeros_like(dbf_ref)

        mask = _spatial_mask()
        mask_t = _spatial_mask(transposed=True)
        wm = [jnp.where(mask, w_ref[g], 0.0).astype(BF16) for g in range(A_GROUPS)]
        wmt = [jnp.where(mask_t, wt_ref[g], 0.0).astype(BF16) for g in range(A_GROUPS)]
        gain = g_ref[...]
        for c in range(tm // A_CHUNK):
            rows = pl.ds(c * A_CHUNK, A_CHUNK)
            z = z_ref[rows, :].astype(F32)
            dgelu = dgelu_ref[rows, :].astype(F32)
            u = z[:, :f]
            v0 = z[:, f:]
            r = lax.rsqrt(jnp.mean(v0 * v0, axis=-1, keepdims=True) + EPS)
            xhat = v0 * r
            v1 = (xhat * gain).astype(BF16)
            dpf = dp_ref[rows, :].astype(F32)
            for g in range(A_GROUPS):
                cols = slice(g * gd, (g + 1) * gd)
                v1g = v1[:, cols]
                v2 = jnp.dot(wm[g], v1g, preferred_element_type=F32) + b_ref[:, cols]
                dpg = dpf[:, cols]
                dz_ref[rows, cols] = (dpg * v2 * dgelu[:, cols]).astype(BF16)
                dv2 = dpg * u[:, cols]
                dbf_ref[:, cols] += dv2
                dv2b = dv2.astype(BF16)
                dwg = lax.dot_general(dv2b, v1g, NT, preferred_element_type=F32)
                dw_ref[g] += jnp.where(mask, dwg, 0.0)
                dv1_ref[:, cols] = jnp.dot(wmt[g], dv2b, preferred_element_type=F32)
            dv1 = dv1_ref[...]
            dxhat = dv1 * gain
            dg_ref[...] += jnp.sum(dv1 * xhat, axis=0, keepdims=True)
            dv0 = r * (dxhat - xhat * jnp.mean(dxhat * xhat, axis=-1, keepdims=True))
            dz_ref[rows, pl.ds(f, f)] = (dv0 * dgelu[:, f:]).astype(BF16)

        @pl.when(step == n_steps - 1)
        def _():
            for g in range(A_GROUPS):
                db_ref[g] = jnp.sum(dbf_ref[:, g * gd:(g + 1) * gd], axis=1, keepdims=True)

    wspec = pl.BlockSpec((A_GROUPS, A_CHUNK, A_CHUNK), lambda i: (0, 0, 0))
    dz, dw, db, dg = _call(
        body, name, (n_steps,),
        [pl.BlockSpec((tm, f2), lambda i: (i, 0)),
         pl.BlockSpec((tm, f2), lambda i: (i, 0)),
         pl.BlockSpec((tm, f), lambda i: (i, 0)),
         pl.BlockSpec((1, f), lambda i: (0, 0)),
         wspec, wspec,
         pl.BlockSpec((A_CHUNK, f), lambda i: (0, 0))],
        [pl.BlockSpec((tm, f2), lambda i: (i, 0)),
         wspec,
         pl.BlockSpec((A_GROUPS, A_CHUNK, 1), lambda i: (0, 0, 0)),
         pl.BlockSpec((1, f), lambda i: (0, 0))],
        [jax.ShapeDtypeStruct((t, f2), BF16),
         jax.ShapeDtypeStruct((A_GROUPS, A_CHUNK, A_CHUNK), F32),
         jax.ShapeDtypeStruct((A_GROUPS, A_CHUNK, 1), F32),
         jax.ShapeDtypeStruct((1, f), F32)],
        scratch=[pltpu.VMEM((A_CHUNK, f), F32), pltpu.VMEM((A_CHUNK, f), F32)],
        vmem_bytes=6 * _nbytes((tm, f2), BF16) + 2 * _nbytes((tm, f), BF16) + 12 * _nbytes((A_CHUNK, f2), F32),
    )(zs, dgs, dp, g_sgu.reshape(1, f), w_sp, w_sp_t, b_full)
    return dz, dw, db.reshape(A_GROUPS, A_CHUNK), dg.reshape(f)


def _pair_valid(qi, col):
    qc = qi // CHUNK
    kc = col // CHUNK
    return (kc >= qc) & (kc <= qc + N_LEFT_CHUNKS)


def _diagonal_onehot():
    e = lax.broadcasted_iota(jnp.int32, (REL_PAD, DIAGONALS), 1)
    idx = jnp.clip(PAIR_BAND - 1 - e, -MAX_REL, MAX_REL) + MAX_REL
    r = lax.broadcasted_iota(jnp.int32, (REL_PAD, DIAGONALS), 0)
    return jnp.where(r == idx, 1.0, 0.0).astype(BF16)


def _bias_build(table, name):
    h = table.shape[0]
    tab = jnp.pad(table, ((0, 0), (0, REL_PAD - N_REL)))

    def body(t_ref, o_ref):
        oh = _diagonal_onehot()
        diag = jnp.zeros((h, DIAGONALS), F32)
        for piece in _split3(t_ref[...]):
            diag += jnp.dot(piece, oh, preferred_element_type=F32)
        col = lax.broadcasted_iota(jnp.int32, (h, PAIR_BAND), 1)
        for qi in range(PAIR_ROWS):
            row = pltpu.roll(diag, (qi - (PAIR_ROWS - 1)) % DIAGONALS, 1)[:, :PAIR_BAND]
            o_ref[qi] = jnp.where(_pair_valid(qi, col), row, NEG_INF)

    out = _call(
        body, name, (1,),
        [pl.BlockSpec((h, REL_PAD), lambda i: (0, 0))],
        pl.BlockSpec((PAIR_ROWS, h, PAIR_BAND), lambda i: (0, 0, 0)),
        jax.ShapeDtypeStruct((PAIR_ROWS, h, PAIR_BAND), F32),
        vmem_bytes=4 * _nbytes((PAIR_ROWS, h, PAIR_BAND), F32),
    )(tab)
    return jnp.transpose(out, (1, 0, 2))


def _bias_block(pair_bias):
    rest = K_BLOCK - PAIR_BAND
    return jnp.concatenate(
        [jnp.pad(pair_bias, ((0, 0), (0, 0), (p * PAIR_ROWS, rest - p * PAIR_ROWS)), constant_values=NEG_INF)
         for p in range(PAIRS_PER_BLOCK)], axis=1)


def _bias_grad(dbias, name):
    h = dbias.shape[0]
    db_t = jnp.transpose(dbias, (1, 0, 2))

    def body(d_ref, o_ref):
        diag = jnp.zeros((h, DIAGONALS), F32)
        for qi in range(PAIR_ROWS):
            diag += pltpu.roll(d_ref[qi], PAIR_ROWS - 1 - qi, 1)
        oh = _diagonal_onehot()
        acc = jnp.zeros((h, REL_PAD), F32)
        for piece in _split3(diag):
            acc += lax.dot_general(piece, oh, NT, preferred_element_type=F32)
        o_ref[...] = acc

    out = _call(
        body, name, (1,),
        [pl.BlockSpec((PAIR_ROWS, h, DIAGONALS), lambda i: (0, 0, 0))],
        pl.BlockSpec((h, REL_PAD), lambda i: (0, 0)),
        jax.ShapeDtypeStruct((h, REL_PAD), F32),
        vmem_bytes=4 * _nbytes((PAIR_ROWS, h, DIAGONALS), F32),
    )(db_t)
    return out[:, :N_REL]


def _head_masks():
    lane = lax.broadcasted_iota(jnp.int32, (Q_BLOCK, HEAD_PAIR), 1)
    return lane < HEAD_DIM, lane >= HEAD_DIM


def _block_scores(qm, kb, bias, valid):
    s = lax.dot_general(qm, kb, NT, preferred_element_type=F32) + bias
    return s if valid is None else jnp.where(valid, s, NEG_INF)


def _softmax_rows(s):
    e = jnp.exp(s - jnp.max(s, axis=-1, keepdims=True))
    return e * (1.0 / jnp.sum(e, axis=-1, keepdims=True))


def _padded_then_plain(step, n_blocks):
    n_padded = min(LEFT // Q_BLOCK, n_blocks)
    lax.fori_loop(0, n_padded, lambda j, c: step(j, c, True), 0, unroll=True)
    lax.fori_loop(n_padded, n_blocks, lambda j, c: step(j, c, False), 0, unroll=ATTN_UNROLL)


def _attn_fwd(q, kvpad, bias, name):
    t, d = q.shape
    n_pairs = d // HEAD_PAIR
    n_blocks = t // Q_BLOCK

    def body(q_ref, k_ref, v_ref, b_ref, o_ref):
        masks = _head_masks()
        key = lax.broadcasted_iota(jnp.int32, (Q_BLOCK, K_BLOCK), 1)

        def step(j, carry, padded):
            r0 = pl.multiple_of(j * Q_BLOCK, Q_BLOCK)
            q2 = q_ref[pl.ds(r0, Q_BLOCK), :].astype(F32)
            kb = k_ref[pl.ds(r0, K_BLOCK), :]
            vb = v_ref[pl.ds(r0, K_BLOCK), :]
            valid = key >= LEFT - j * Q_BLOCK if padded else None
            scores = [_block_scores(jnp.where(masks[a], q2, 0.0).astype(BF16), kb, b_ref[a], valid) for a in range(2)]
            probs = [_softmax_rows(s).astype(BF16) for s in scores]
            outs = [jnp.dot(p, vb, preferred_element_type=F32) for p in probs]
            o_ref[pl.ds(r0, Q_BLOCK), :] = jnp.where(masks[0], outs[0], outs[1]).astype(BF16)
            return carry

        _padded_then_plain(step, n_blocks)

    return _call(
        body, name, (n_pairs,),
        [pl.BlockSpec((t, HEAD_PAIR), lambda p: (0, p)),
         pl.BlockSpec((LEFT + t, HEAD_PAIR), lambda p: (0, p)),
         pl.BlockSpec((LEFT + t, HEAD_PAIR), lambda p: (0, n_pairs + p)),
         pl.BlockSpec((2, Q_BLOCK, K_BLOCK), lambda p: (p, 0, 0))],
        pl.BlockSpec((t, HEAD_PAIR), lambda p: (0, p)),
        jax.ShapeDtypeStruct((t, d), BF16),
        vmem_bytes=8 * _nbytes((LEFT + t, HEAD_PAIR), BF16) + 12 * _nbytes((2, Q_BLOCK, K_BLOCK), F32),
    )(q, kvpad, kvpad, bias)


def _attn_bwd(q, kvpad, bias, o, do, dk_in, dv_in, name):
    t, d = q.shape
    n_pairs = d // HEAD_PAIR
    n_blocks = t // Q_BLOCK
    has_in = dk_in is not None

    def body(*refs):
        refs = list(refs)
        q_ref, k_ref, v_ref, b_ref, o_ref, do_ref = refs[:6]
        refs = refs[6:]
        if has_in:
            dki_ref, dvi_ref = refs[:2]
            refs = refs[2:]
        dq_ref, dk_ref, dv_ref, db_ref = refs
        masks = _head_masks()
        key = lax.broadcasted_iota(jnp.int32, (Q_BLOCK, K_BLOCK), 1)
        if has_in:
            dk_ref[...] = dki_ref[...]
            dv_ref[...] = dvi_ref[...]
        else:
            dk_ref[...] = jnp.zeros_like(dk_ref)
            dv_ref[...] = jnp.zeros_like(dv_ref)
        db_ref[...] = jnp.zeros_like(db_ref)

        def step(j, carry, padded):
            r0 = pl.multiple_of(j * Q_BLOCK, Q_BLOCK)
            q2 = q_ref[pl.ds(r0, Q_BLOCK), :].astype(F32)
            do2 = do_ref[pl.ds(r0, Q_BLOCK), :].astype(F32)
            do_o = do2 * o_ref[pl.ds(r0, Q_BLOCK), :].astype(F32)
            kb = k_ref[pl.ds(r0, K_BLOCK), :]
            vb = v_ref[pl.ds(r0, K_BLOCK), :]
            valid = key >= LEFT - j * Q_BLOCK if padded else None
            heads = range(2)
            qms = [jnp.where(masks[a], q2, 0.0).astype(BF16) for a in heads]
            doms = [jnp.where(masks[a], do2, 0.0).astype(BF16) for a in heads]
            scores = [_block_scores(qms[a], kb, b_ref[a], valid) for a in heads]
            dps = [lax.dot_general(doms[a], vb, NT, preferred_element_type=F32) for a in heads]
            ps = [_softmax_rows(s) for s in scores]
            rows = [jnp.sum(jnp.where(masks[a], do_o, 0.0), axis=-1, keepdims=True) for a in heads]
            dss = [ps[a] * (dps[a] - rows[a]) for a in heads]
            for a in heads:
                for pair in range(PAIRS_PER_BLOCK):
                    lo = pair * PAIR_ROWS
                    db_ref[a, :, pl.ds(0, PAIR_BAND)] += dss[a][lo:lo + PAIR_ROWS, lo:lo + PAIR_BAND]
            dsbs = [ds.astype(BF16) for ds in dss]
            pbs = [p.astype(BF16) for p in ps]
            dqs = [jnp.dot(dsbs[a], kb, preferred_element_type=F32) for a in heads]
            dk_acc = sum(lax.dot_general(dsbs[a], qms[a], TN, preferred_element_type=F32) for a in heads)
            dv_acc = sum(lax.dot_general(pbs[a], doms[a], TN, preferred_element_type=F32) for a in heads)
            dq = jnp.where(masks[0], dqs[0], dqs[1]) * ATTN_SCALE
            dq_ref[pl.ds(r0, Q_BLOCK), :] = dq.astype(BF16)
            dk_ref[pl.ds(r0, K_BLOCK), :] += dk_acc
            dv_ref[pl.ds(r0, K_BLOCK), :] += dv_acc
            return carry

        _padded_then_plain(step, n_blocks)

    q_spec = pl.BlockSpec((t, HEAD_PAIR), lambda p: (0, p))
    kv_spec = pl.BlockSpec((LEFT + t, HEAD_PAIR), lambda p: (0, p))
    operands = [q, kvpad, kvpad, bias, o, do]
    in_specs = [q_spec, kv_spec, pl.BlockSpec((LEFT + t, HEAD_PAIR), lambda p: (0, n_pairs + p)),
                pl.BlockSpec((2, Q_BLOCK, K_BLOCK), lambda p: (p, 0, 0)), q_spec, q_spec]
    aliases = None
    if has_in:
        operands += [dk_in, dv_in]
        in_specs += [kv_spec, kv_spec]
        aliases = {6: 1, 7: 2}
    return _call(
        body, name, (n_pairs,),
        in_specs,
        [q_spec, kv_spec, kv_spec, pl.BlockSpec((2, PAIR_ROWS, DIAGONALS), lambda p: (p, 0, 0))],
        [jax.ShapeDtypeStruct((t, d), BF16),
         jax.ShapeDtypeStruct((LEFT + t, d), F32),
         jax.ShapeDtypeStruct((LEFT + t, d), F32),
         jax.ShapeDtypeStruct((d // HEAD_DIM, PAIR_ROWS, DIAGONALS), F32)],
        vmem_bytes=10 * _nbytes((LEFT + t, HEAD_PAIR), BF16) + 8 * _nbytes((LEFT + t, HEAD_PAIR), F32)
        + 16 * _nbytes((2, Q_BLOCK, K_BLOCK), F32),
        aliases=aliases,
    )(*operands)


def _loss_head(x, g, target, name):
    t, d = x.shape
    tm = _tile(t, 512)

    def body(x_ref, g_ref, t_ref, dx_ref, loss_ref, dg_ref):
        @pl.when(pl.program_id(0) == 0)
        def _():
            loss_ref[...] = jnp.zeros_like(loss_ref)
            dg_ref[...] = jnp.zeros_like(dg_ref)

        xf = x_ref[...]
        r = lax.rsqrt(jnp.mean(xf * xf, axis=-1, keepdims=True) + EPS)
        xhat = xf * r
        diff = xhat * g_ref[...] - t_ref[...]
        row_loss = jnp.mean(diff * diff, axis=-1, keepdims=True)
        loss_ref[...] += 0.5 * jnp.sum(row_loss, axis=0, keepdims=True)
        dy = diff * (1.0 / d)
        dg_ref[...] += jnp.sum(dy * xhat, axis=0, keepdims=True)
        dxhat = dy * g_ref[...]
        dx_ref[...] = r * (dxhat - xhat * jnp.mean(dxhat * xhat, axis=-1, keepdims=True))

    row = pl.BlockSpec((tm, d), lambda i: (i, 0))
    vec = pl.BlockSpec((1, d), lambda i: (0, 0))
    dx, loss, dg = _call(
        body, name, (t // tm,),
        [row, vec, row],
        [row, pl.BlockSpec((1, 1), lambda i: (0, 0)), vec],
        [jax.ShapeDtypeStruct((t, d), F32), jax.ShapeDtypeStruct((1, 1), F32), jax.ShapeDtypeStruct((1, d), F32)],
        vmem_bytes=10 * _nbytes((tm, d), F32),
    )(x, g.reshape(1, d), target)
    return dx, loss[0, 0], dg.reshape(d)


def _adamw_store(g, w_ref, m_ref, v_ref, g_ref, d_ref, nm_ref, nv_ref):
    c1 = 1.0 / (1.0 - ADAM_B1 ** ADAM_STEP)
    c2 = 1.0 / (1.0 - ADAM_B2 ** ADAM_STEP)
    nm = ADAM_B1 * m_ref[...] + (1.0 - ADAM_B1) * g
    nv = ADAM_B2 * v_ref[...] + (1.0 - ADAM_B2) * (g * g)
    g_ref[...] = g
    nm_ref[...] = nm
    nv_ref[...] = nv
    d_ref[...] = -ADAM_LR * ((nm * c1) / (jnp.sqrt(nv * c2) + ADAM_EPS) + ADAM_WD * w_ref[...])


def _adamw_layer(recv, own, w, m, v, layer, prev, me, name):
    n_src, r, c = recv.shape
    tr = _row_tile(r, max(BF16_SUBLANES, ADAMW_BLOCK_ELEMS // c), BF16_SUBLANES)
    first = prev is None

    def body(me_ref, recv_ref, own_ref, w_ref, m_ref, v_ref, *rest):
        mine = me_ref[0]
        own_part = own_ref[...].astype(F32)
        g = None
        for s in range(n_src):
            part = jnp.where(mine == s, own_part, recv_ref[s].astype(F32))
            g = part if g is None else g + part
        _adamw_store(g, w_ref, m_ref, v_ref, *rest[-4:])

    blk = pl.BlockSpec((None, tr, c), lambda i, me_ref: (layer, i, 0))
    any_spec = pl.BlockSpec(memory_space=pl.ANY)
    out = jax.ShapeDtypeStruct(w.shape, F32)
    operands = [me, recv, own, w, m, v] + ([] if first else list(prev))
    vmem = 2 * _nbytes((n_src + 1, tr, c), BF16) + 18 * _nbytes((tr, c), F32)
    return pl.pallas_call(
        body,
        name=name,
        grid_spec=pltpu.PrefetchScalarGridSpec(
            num_scalar_prefetch=1,
            grid=(r // tr,),
            in_specs=[pl.BlockSpec((n_src, tr, c), lambda i, me_ref: (0, i, 0)),
                      pl.BlockSpec((None, tr, c), lambda i, me_ref: (me_ref[0], i, 0)),
                      blk, blk, blk] + ([] if first else [any_spec] * 4),
            out_specs=[blk, blk, blk, blk],
        ),
        out_shape=[out, out, out, out],
        input_output_aliases={} if first else {6 + j: j for j in range(4)},
        compiler_params=pltpu.CompilerParams(
            dimension_semantics=("arbitrary",),
            vmem_limit_bytes=int(min(max(VMEM_FLOOR_BYTES, vmem * 5 // 4), VMEM_CEIL_BYTES))),
    )(*operands)


def _adamw(parts, w, m, v, name):
    n_layers, n_src, r, c = parts.shape
    mult = BF16_SUBLANES if parts.dtype == BF16 else F32_SUBLANES
    tr = _row_tile(r, max(mult, ADAMW_BLOCK_ELEMS // c), mult)

    def body(p_ref, w_ref, m_ref, v_ref, g_ref, d_ref, nm_ref, nv_ref):
        g = p_ref[0].astype(F32)
        for s in range(1, n_src):
            g = g + p_ref[s].astype(F32)
        _adamw_store(g, w_ref, m_ref, v_ref, g_ref, d_ref, nm_ref, nv_ref)

    blk = pl.BlockSpec((None, tr, c), lambda l, i: (l, i, 0))
    out = jax.ShapeDtypeStruct((n_layers, r, c), F32)
    return _call(
        body, name, (n_layers, r // tr),
        [pl.BlockSpec((None, n_src, tr, c), lambda l, i: (l, 0, i, 0)), blk, blk, blk],
        [blk, blk, blk, blk],
        [out, out, out, out],
        vmem_bytes=2 * _nbytes((n_src, tr, c), parts.dtype) + 18 * _nbytes((tr, c), F32),
    )(parts, w, m, v)


def _ordered_sum(parts, name):
    n_src, r, c = parts.shape

    def body(p_ref, o_ref):
        acc = p_ref[0]
        for s in range(1, n_src):
            acc = acc + p_ref[s]
        o_ref[...] = acc

    return _call(
        body, name, (1,),
        [pl.BlockSpec((n_src, r, c), lambda i: (0, 0, 0))],
        pl.BlockSpec((r, c), lambda i: (0, 0)),
        jax.ShapeDtypeStruct((r, c), F32),
        vmem_bytes=4 * _nbytes((n_src, r, c), F32),
    )(parts)


def _position():
    return lax.axis_index("x"), lax.axis_index("y"), lax.axis_index("c")


def _linear(p):
    return 4 * p[0] + 2 * p[1] + p[2]


def _all_gather(shards, name):
    n = len(shards)

    def body(*refs):
        ins, outs = refs[:n], refs[n:2 * n]
        send_sems, recv_sems, local_sems = refs[2 * n:]
        x, y, c = _position()
        me, sibling = (x, y, c), (x, y, 1 - c)
        chips = [(1 - x, y), (x, 1 - y), (1 - x, 1 - y)]

        def slab(t, p):
            return outs[t].at[:, _linear(p)]

        def copy(t, k, block, to, src=None):
            return pltpu.make_async_remote_copy(
                src_ref=slab(t, block) if src is None else src,
                dst_ref=slab(t, block),
                send_sem=send_sems.at[t, k],
                recv_sem=recv_sems.at[t, k],
                device_id=to,
                device_id_type=MESH,
            )

        started = []
        for t in range(n):
            mine = pltpu.make_async_copy(ins[t], slab(t, me), local_sems.at[t])
            mine.start()
            started.append(mine)
        sends = []
        for t in range(n):
            first = [copy(t, 0, me, sibling, src=ins[t])]
            first += [copy(t, 1 + j, me, (*chip, c), src=ins[t]) for j, chip in enumerate(chips)]
            for cp in first:
                cp.start()
            sends += first
        for t in range(n):
            for j, chip in enumerate(chips):
                copy(t, 1 + j, (*chip, c), me).wait_recv()
                passed = copy(t, 4 + j, (*chip, c), sibling)
                passed.start()
                sends.append(passed)
        for t in range(n):
            copy(t, 0, sibling, me).wait_recv()
            for j, chip in enumerate(chips):
                copy(t, 4 + j, (*chip, 1 - c), me).wait_recv()
        for cp in sends:
            cp.wait_send()
        for mine in started:
            mine.wait()

    out_shape = [jax.ShapeDtypeStruct((s.shape[0], N_DEV) + s.shape[1:], s.dtype) for s in shards]
    return pl.pallas_call(
        body,
        name=name,
        in_specs=[HBM_SPEC] * n,
        out_specs=[HBM_SPEC] * n,
        out_shape=out_shape,
        scratch_shapes=[
            pltpu.SemaphoreType.DMA((n, N_DEV - 1)),
            pltpu.SemaphoreType.DMA((n, N_DEV - 1)),
            pltpu.SemaphoreType.DMA((n,)),
        ],
    )(*shards)


def _exchange(blocks, name):
    n = len(blocks)

    def body(*refs):
        ins, outs = refs[:n], refs[n:2 * n]
        send_sems, recv_sems, local_sems = refs[2 * n:]
        x, y, c = _position()
        me = _linear((x, y, c))
        flips = [(fx, fy, fc) for fx in (0, 1) for fy in (0, 1) for fc in (0, 1)][1:]

        def peer_of(flip):
            fx, fy, fc = flip
            return (1 - x if fx else x, 1 - y if fy else y, 1 - c if fc else c)

        def copy(t, k, peer):
            return pltpu.make_async_remote_copy(
                src_ref=ins[t].at[:, _linear(peer)],
                dst_ref=outs[t].at[:, me],
                send_sem=send_sems.at[t, k],
                recv_sem=recv_sems.at[t, k],
                device_id=peer,
                device_id_type=MESH,
            )

        def arrival(t, k, peer):
            return pltpu.make_async_remote_copy(
                src_ref=ins[t].at[:, _linear(peer)],
                dst_ref=outs[t].at[:, _linear(peer)],
                send_sem=send_sems.at[t, k],
                recv_sem=recv_sems.at[t, k],
                device_id=peer,
                device_id_type=MESH,
            )

        own = []
        for t in range(n):
            cp = pltpu.make_async_copy(ins[t].at[:, me], outs[t].at[:, me], local_sems.at[t])
            cp.start()
            own.append(cp)
        sends = []
        for t in range(n):
            for k, flip in enumerate(flips):
                cp = copy(t, k, peer_of(flip))
                cp.start()
                sends.append(cp)
        for t in range(n):
            for k, flip in enumerate(flips):
                arrival(t, k, peer_of(flip)).wait_recv()
        for cp in sends:
            cp.wait_send()
        for cp in own:
            cp.wait()

    out_shape = [jax.ShapeDtypeStruct(b.shape, b.dtype) for b in blocks]
    return pl.pallas_call(
        body,
        name=name,
        in_specs=[HBM_SPEC] * n,
        out_specs=[HBM_SPEC] * n,
        out_shape=out_shape,
        scratch_shapes=[
            pltpu.SemaphoreType.DMA((n, N_DEV - 1)),
            pltpu.SemaphoreType.DMA((n, N_DEV - 1)),
            pltpu.SemaphoreType.DMA((n,)),
        ],
    )(*blocks)


def _peers():
    x, y, c = _position()
    flips = [(fx, fy, fc) for fx in (0, 1) for fy in (0, 1) for fc in (0, 1)][1:]
    return [(1 - x if fx else x, 1 - y if fy else y, 1 - c if fc else c) for fx, fy, fc in flips]


SIBLING, OTHER_CHIPS = (0,), (1, 3, 5)
COPY_PEERS = {"gather": tuple(range(N_DEV - 1)), "exchange": tuple(range(N_DEV - 1)),
              "chips": SIBLING + OTHER_CHIPS, "forward": OTHER_CHIPS}


def _split_copy(kind, src_ref, land_ref, k, send_sem, recv_sem, starting):
    peers = _peers()
    peer = peers[SIBLING[0]] if kind == "forward" else peers[k]
    me = _linear(_position())
    if kind == "forward":
        slab = _linear(peers[k]) if starting else 0
        src, dst = land_ref.at[slab], land_ref.at[slab]
    elif kind == "exchange":
        src, dst = src_ref.at[_linear(peer) if starting else 0], land_ref.at[me if starting else 0]
    else:
        src, dst = src_ref, land_ref.at[me if starting else 0]
    return pltpu.make_async_remote_copy(src_ref=src, dst_ref=dst, send_sem=send_sem, recv_sem=recv_sem,
                                        device_id=peer, device_id_type=MESH)


def _split_start(groups, carry, name):
    arrays = [a for _, srcs, lands in groups for a in list(srcs) + list(lands)] + [carry]

    def body(*refs):
        ins, sems = refs[:len(arrays)], refs[len(arrays):len(arrays) + 2 * len(groups)]
        at = 0
        for g, (kind, srcs, lands) in enumerate(groups):
            src_refs, land_refs = ins[at:at + len(srcs)], ins[at + len(srcs):at + len(srcs) + len(lands)]
            at += len(srcs) + len(lands)
            peers = COPY_PEERS[kind]
            for t in range(len(lands)):
                for slot, k in enumerate(peers):
                    sem = t * len(peers) + slot
                    _split_copy(kind, src_refs[t] if srcs else None, land_refs[t], k,
                                sems[2 * g].at[sem], sems[2 * g + 1].at[sem], True).start()

    sem_shapes = [pltpu.SemaphoreType.DMA((len(lands) * len(COPY_PEERS[kind]),))
                  for kind, _, lands in groups for _ in range(2)]
    out = pl.pallas_call(
        body,
        name=name,
        in_specs=[HBM_SPEC] * len(arrays),
        out_specs=[SEM_SPEC] * len(sem_shapes) + [HBM_SPEC] * len(arrays),
        out_shape=sem_shapes + [pltpu.HBM(a.shape, a.dtype) for a in arrays],
        input_output_aliases={i: len(sem_shapes) + i for i in range(len(arrays))},
        compiler_params=pltpu.CompilerParams(has_side_effects=pltpu.SideEffectType.DATAFLOW_SIDE_EFFECTING),
    )(*[pltpu.with_memory_space_constraint(a, pltpu.HBM) for a in arrays])
    sems, thru = out[:len(sem_shapes)], out[len(sem_shapes):]
    started, at = [], 0
    for g, (kind, srcs, lands) in enumerate(groups):
        n_s, n_l = len(srcs), len(lands)
        started.append((kind, sems[2 * g], sems[2 * g + 1], thru[at:at + n_s], thru[at + n_s:at + n_s + n_l]))
        at += n_s + n_l
    return started, thru[-1]


def _split_wait(started, after, name):
    kind, send_sems, recv_sems, srcs, lands = started
    n_s, n_l = len(srcs), len(lands)
    peers = COPY_PEERS[kind]

    def body(*refs):
        src_refs, land_refs = refs[:n_s], refs[n_s:n_s + n_l]
        send_ref, recv_ref = refs[n_s + n_l], refs[n_s + n_l + 1]
        for t in range(n_l):
            for slot, k in enumerate(peers):
                sem = t * len(peers) + slot
                copy = _split_copy(kind, src_refs[t] if n_s else None, land_refs[t], k,
                                   send_ref.at[sem], recv_ref.at[sem], False)
                copy.wait_send()
                copy.wait_recv()

    arrays = list(srcs) + list(lands)
    out = pl.pallas_call(
        body,
        name=name,
        in_specs=[HBM_SPEC] * len(arrays) + [SEM_SPEC, SEM_SPEC, pl.BlockSpec(memory_space=pl.ANY)],
        out_specs=[HBM_SPEC] * len(arrays),
        out_shape=[pltpu.HBM(a.shape, a.dtype) for a in arrays],
        input_output_aliases={i: i for i in range(len(arrays))},
        compiler_params=pltpu.CompilerParams(has_side_effects=pltpu.SideEffectType.DATAFLOW_SIDE_EFFECTING),
    )(*arrays, send_sems, recv_sems, after)
    return out[:n_s], out[n_s:]


def _pack(arrays, row_multiple):
    flat = jnp.concatenate([a.reshape(-1) for a in arrays])
    quantum = row_multiple * FLAT_LANES
    padded = -(-flat.shape[0] // quantum) * quantum
    return jnp.pad(flat, (0, padded - flat.shape[0])).reshape(-1, FLAT_LANES)


def _unpack(flat, like):
    flat = flat.reshape(-1)
    out, at = [], 0
    for a in like:
        size = math.prod(a.shape)
        out.append(flat[at:at + size].reshape(a.shape))
        at += size
    return out


def kernel(x, a_norm, a_w_in, a_sgu_norm, a_w_spatial, a_b_spatial, a_w_out, kv_norm, w_kv, b_norm, b_w_q, b_rel_bias, b_w_o, ffn_norm, ffn_w_gate_up, ffn_w_down, final_norm, loss_target, m_a_norm, m_a_w_in, m_a_sgu_norm, m_a_w_spatial, m_a_b_spatial, m_a_w_out, m_kv_norm, m_w_kv, m_b_norm, m_b_w_q, m_b_rel_bias, m_b_w_o, m_ffn_norm, m_ffn_w_gate_up, m_ffn_w_down, m_final_norm, v_a_norm, v_a_w_in, v_a_sgu_norm, v_a_w_spatial, v_a_b_spatial, v_a_w_out, v_kv_norm, v_w_kv, v_b_norm, v_b_w_q, v_b_rel_bias, v_b_w_o, v_ffn_norm, v_ffn_w_gate_up, v_ffn_w_down, v_final_norm):
    xs = x[0]
    target = loss_target[0]
    t, d = xs.shape
    n_a = a_w_in.shape[0]
    n_b = b_w_q.shape[0]
    depth = ffn_w_gate_up.shape[0]
    f_a = a_w_out.shape[1] * N_DEV
    gd = f_a // A_GROUPS
    nb_ffn = ffn_w_gate_up.shape[2]
    me = _linear(_position())

    small_rows = -(-(a_norm.size + a_sgu_norm.size) // (8 * 128)) * 8
    small = jnp.pad(jnp.concatenate([a_norm.reshape(-1), a_sgu_norm.reshape(-1)]),
                    (0, small_rows * 128 - a_norm.size - a_sgu_norm.size)).reshape(1, small_rows, 128)

    def shard(w, layer=None):
        return (w if layer is None else w[layer]).astype(BF16)

    stages = []
    for layer in range(depth):
        if layer == 0:
            stages += [("a0", [shard(a_w_in, 0)]), ("a0_out", [shard(a_w_out, 0)])]
        elif layer < n_a:
            stages.append((f"a{layer}", [shard(a_w_in, layer), shard(a_w_out, layer)]))
        else:
            i = layer - n_a
            shared = [shard(w_kv)] if i == 0 else []
            stages.append((f"b{i}", shared + [shard(b_w_q, i), shard(b_w_o, i)]))
        stages.append((f"f{layer}", [shard(ffn_w_gate_up, layer), shard(ffn_w_down, layer)]))
    first = _all_gather([s[None] for s in stages[0][1]] + [small], "gather_first")
    gathered = {stages[0][0]: [g[0] for g in first[:-1]]}
    small_g = first[-1].reshape(N_DEV, -1)
    a_norm_full = small_g[:, :a_norm.size].reshape(N_DEV, n_a, -1).transpose(1, 0, 2).reshape(n_a, d)
    a_sgu_full = small_g[:, a_norm.size:a_norm.size + a_sgu_norm.size].reshape(
        N_DEV, n_a, -1).transpose(1, 0, 2).reshape(n_a, f_a)
    two_level = ("f0", "a1", "f1")
    later = [("chips" if key in two_level else "gather", shards,
              [lax.dynamic_update_slice(lax.empty((N_DEV,) + s.shape, BF16), s[None], (me, 0, 0)) for s in shards])
             for key, shards in stages[1:]]
    started, a_norm_full = _split_start(later, a_norm_full, "gather_start")
    in_flight = {key: group for (key, _), group in zip(stages[1:], started)}

    def pass_on(key, carry):
        if key in two_level and key in in_flight and in_flight[key][0] == "chips":
            _, lands = _split_wait(in_flight.pop(key), carry, f"gather_wait_{key}_chips")
            (in_flight[key],), carry = _split_start([("forward", [], lands)], carry, f"gather_pass_on_{key}")
        return carry

    def weights(key, after):
        if key not in gathered:
            _, gathered[key] = _split_wait(in_flight.pop(key), after, f"gather_wait_{key}")
        return gathered[key]

    rows_down = ffn_w_down.shape[1]

    def mixer_a_weights(i, after):
        if i == 0:
            (w_in,), (w_out,) = weights("a0", after[0]), weights("a0_out", after[1])
        else:
            w_in, w_out = weights(f"a{i}", after[0])
        return w_in[None], w_out.reshape(1, f_a, d)

    def mixer_b_weights(i, after):
        ws = weights(f"b{i}", after)
        return ws[-2].reshape(1, d, d), ws[-1].reshape(1, d, d)

    def ffn_weights(layer, after):
        w_gu, w_dn = weights(f"f{layer}", after)
        return w_gu[None], w_dn.reshape(1, N_DEV // 2, 2 * rows_down, d)

    w_sp_t = jnp.swapaxes(a_w_spatial, -1, -2)
    b_full = jnp.repeat(jnp.swapaxes(a_b_spatial, -1, -2), gd, axis=-1)

    saved = []

    def ffn_fwd(xin, layer):
        hf = _rms_fwd(xin, ffn_norm[layer], f"ffn_norm_fwd_{layer}")
        w_gu, w_dn = ffn_weights(layer, xin)
        dact, act = _ffn_gate_up(f"ffn_gate_up_{layer}", hf, w_gu, 0)
        act = pass_on(f"a{layer + 1}", act)
        xout = _mm_down(f"ffn_down_{layer}", act, w_dn, 0, xin)
        return xout, (xin, hf, dact, act)

    for i in range(n_a):
        h = _rms_fwd(xs, a_norm_full[i], f"a_norm_fwd_{i}")
        zpre = _mm_colblock(f"a_in_{i}", h, weights(f"a{i}", xs)[0][None], 0)
        p, zs, dgs = _sgu_fwd(zpre, a_sgu_full[i], a_w_spatial[i], b_full[i], f"a_sgu_fwd_{i}")
        p = pass_on(f"f{i}", p)
        w_in, w_out = mixer_a_weights(i, (xs, p))
        x_mid = _mm_natural(f"a_out_{i}", p, w_out, 0, res=xs)
        x_out, ffn_saved = ffn_fwd(x_mid, i)
        saved.append((xs, h, zs, dgs, p, ffn_saved))
        xs = x_out

    x_kv = xs
    w_kv_g = weights("b0", x_kv)[0][None]
    h_kv = _rms_fwd(x_kv, kv_norm, "kv_norm_fwd")
    kv = _mm_colblock("kv_proj", h_kv, w_kv_g, 0)
    kvpad = jnp.pad(kv, ((LEFT, 0), (0, 0)))

    biases = [_bias_block(_bias_build(b_rel_bias[i], f"rel_bias_{i}")) for i in range(n_b)]
    for i in range(n_b):
        layer = n_a + i
        w_q, w_o = mixer_b_weights(i, xs)
        hb = _rms_fwd(xs, b_norm[i], f"b_norm_fwd_{i}")
        q = _mm_natural(f"b_q_{i}", hb, w_q, 0, out_dtype=BF16, scale=ATTN_SCALE)
        o = _attn_fwd(q, kvpad, biases[i], f"b_attn_fwd_{i}")
        x_mid = _mm_natural(f"b_o_{i}", o, w_o, 0, res=xs)
        x_out, ffn_saved = ffn_fwd(x_mid, layer)
        saved.append((xs, hb, q, o, ffn_saved))
        xs = x_out

    dx, loss_local, g_final = _loss_head(xs, final_norm, target, "loss_head")
    loss = lax.psum(loss_local, ("x", "y", "c"))

    big_grads = {}
    pending = []
    in_flight_grads = []

    def start_exchange(dx, tag):
        srcs = [big_grads[key] for key in pending]
        lands = [lax.empty(s.shape, BF16) for s in srcs]
        (group,), dx = _split_start([("exchange", srcs, lands)], dx, f"exchange_start_{tag}")
        in_flight_grads.append((list(pending), group, tag))
        pending.clear()
        return dx

    g_ffn_norm = [None] * depth
    g_a_norm = [None] * n_a
    g_a_sgu = [None] * n_a
    g_w_sp = [None] * n_a
    g_b_sp = [None] * n_a
    g_b_norm = [None] * n_b
    g_rel = [None] * n_b

    def ffn_bwd(dx, layer, ffn_saved):
        eager = layer < n_a
        xin, hf, dact, act = ffn_saved
        big_grads["ffn_w_down", layer] = _mm_dw_down(f"ffn_down_dw_{layer}", act, dx)
        pending.append(("ffn_w_down", layer))
        if eager:
            dx = start_exchange(dx, f"f{layer}_down")
        w_gu, w_dn = ffn_weights(layer, xin)
        dgu = _ffn_down_dx(f"ffn_down_dx_{layer}", dx, w_dn, 0, dact).reshape(N_DEV, t, nb_ffn)
        big_grads["ffn_w_gate_up", layer] = _mm_dw_colblock(
            f"ffn_gate_up_dw_{layer}", hf, dgu, blocked_in=True, transposed=True)
        pending.append(("ffn_w_gate_up", layer))
        if eager:
            dx = start_exchange(dx, f"f{layer}_gate_up")
        dx, g_ffn_norm[layer] = _mm_t_colblock_norm_bwd(
            f"ffn_gate_up_dx_{layer}", dgu, w_gu, 0, xin, ffn_norm[layer], dx, blocked_in=True)
        return dx

    dk = dv = None
    for i in reversed(range(n_b)):
        layer = n_a + i
        x_in, hb, q, o, ffn_saved = saved[layer]
        dx = ffn_bwd(dx, layer, ffn_saved)
        big_grads["b_w_o", i] = _mm_dw_natural(f"b_o_dw_{i}", o, dx)
        w_q, w_o = mixer_b_weights(i, x_in)
        do = _mm_t_natural(f"b_o_dx_{i}", dx, w_o, 0)
        dq, dk, dv, dbias = _attn_bwd(q, kvpad, biases[i], o, do, dk, dv, f"b_attn_bwd_{i}")
        g_rel[i] = _bias_grad(dbias, f"rel_bias_grad_{i}")
        big_grads["b_w_q", i] = _mm_dw_natural(f"b_q_dw_{i}", hb, dq)
        pending.extend([("b_w_o", i), ("b_w_q", i)])
        dx, g_b_norm[i] = _mm_t_natural_norm_bwd(f"b_q_dx_{i}", dq, w_q, 0, x_in, b_norm[i], dx)
        if i > 0:
            dx = start_exchange(dx, f"b{i}")

    dkv = jnp.concatenate([dk[LEFT:], dv[LEFT:]], axis=1).astype(BF16)
    big_grads["w_kv", 0] = _mm_dw_colblock("kv_proj_dw", h_kv, dkv)
    pending.append(("w_kv", 0))
    dx, g_kv_norm = _mm_t_colblock_norm_bwd("kv_proj_dx", dkv, w_kv_g, 0, x_kv, kv_norm, dx)
    dx = start_exchange(dx, "kv")

    for i in reversed(range(n_a)):
        x_in, h, zs, dgs, p, ffn_saved = saved[i]
        dx = ffn_bwd(dx, i, ffn_saved)
        big_grads["a_w_out", i] = _mm_dw_natural(f"a_out_dw_{i}", p, dx)
        pending.append(("a_w_out", i))
        dx = start_exchange(dx, f"a{i}_out")
        w_in, w_out = mixer_a_weights(i, (x_in, p))
        dp = _mm_t_natural(f"a_out_dx_{i}", dx, w_out, 0)
        dz, g_w_sp[i], g_b_sp[i], g_a_sgu[i] = _sgu_bwd(
            zs, dgs, dp, a_sgu_full[i], a_w_spatial[i], w_sp_t[i], b_full[i], f"a_sgu_bwd_{i}")
        big_grads["a_w_in", i] = _mm_dw_colblock(f"a_in_dw_{i}", h, dz)
        pending.append(("a_w_in", i))
        dx = start_exchange(dx, f"a{i}_in")
        dx, g_a_norm[i] = _mm_t_colblock_norm_bwd(f"a_in_dx_{i}", dz, w_in, 0, x_in, a_norm_full[i], dx)
    grad_x = dx[None]

    small_like = [jax.ShapeDtypeStruct((n_a, d), F32), jax.ShapeDtypeStruct((n_a, f_a), F32),
                  a_w_spatial, a_b_spatial, kv_norm, b_norm, b_rel_bias, ffn_norm, final_norm]
    small_partial = _pack(
        [jnp.stack(g_a_norm), jnp.stack(g_a_sgu), jnp.stack(g_w_sp), jnp.stack(g_b_sp), g_kv_norm,
         jnp.stack(g_b_norm), jnp.stack(g_rel), jnp.stack(g_ffn_norm), g_final], N_DEV * 8)
    chunk_rows = small_partial.shape[0] // N_DEV
    arrived = {}
    for keys, group, tag in in_flight_grads:
        srcs, lands = _split_wait(group, dx, f"exchange_wait_{tag}")
        for key, src, land in zip(keys, srcs, lands):
            arrived[key] = (land, src)
    small_got = _exchange([small_partial.reshape(1, N_DEV, chunk_rows, FLAT_LANES)], "exchange_small")[0]
    small_sum = _ordered_sum(small_got[0], "small_grad_sum")
    small_all = _all_gather([small_sum[None]], "gather_small_grads")[0]
    (ga_norm, ga_sgu, gw_sp, gb_sp, gkv_norm, gb_norm, g_relb, gffn_norm, gfinal) = _unpack(small_all, small_like)

    results = {}
    big_names = ["a_w_in", "a_w_out", "w_kv", "b_w_q", "b_w_o", "ffn_w_gate_up", "ffn_w_down"]
    big_wmv = [(a_w_in, m_a_w_in, v_a_w_in), (a_w_out, m_a_w_out, v_a_w_out),
               (w_kv[None], m_w_kv[None], v_w_kv[None]), (b_w_q, m_b_w_q, v_b_w_q), (b_w_o, m_b_w_o, v_b_w_o),
               tuple(jnp.swapaxes(a, 1, 2) for a in (ffn_w_gate_up, m_ffn_w_gate_up, v_ffn_w_gate_up)),
               (ffn_w_down, m_ffn_w_down, v_ffn_w_down)]
    me_arr = jnp.reshape(me, (1,)).astype(jnp.int32)
    for name, (w, m, v) in zip(big_names, big_wmv):
        outs = None
        for layer in range(w.shape[0]):
            got, own = arrived[name, layer]
            outs = _adamw_layer(got, own, w, m, v, layer, outs, me_arr, f"adamw_{name}_{layer}")
        if name == "w_kv":
            outs = [o[0] for o in outs]
        if name == "ffn_w_gate_up":
            outs = [jnp.swapaxes(o, 1, 2) for o in outs]
        results[name] = outs

    n_cols = a_norm.shape[1]
    s_cols = a_sgu_norm.shape[1]
    small_g_list = [lax.dynamic_slice(ga_norm, (0, me * n_cols), (n_a, n_cols)),
                    lax.dynamic_slice(ga_sgu, (0, me * s_cols), (n_a, s_cols)),
                    gw_sp, gb_sp, gkv_norm, gb_norm, g_relb, gffn_norm, gfinal]
    small_names = ["a_norm", "a_sgu_norm", "a_w_spatial", "a_b_spatial", "kv_norm", "b_norm", "b_rel_bias",
                   "ffn_norm", "final_norm"]
    small_w = [a_norm, a_sgu_norm, a_w_spatial, a_b_spatial, kv_norm, b_norm, b_rel_bias, ffn_norm, final_norm]
    small_m = [m_a_norm, m_a_sgu_norm, m_a_w_spatial, m_a_b_spatial, m_kv_norm, m_b_norm, m_b_rel_bias,
               m_ffn_norm, m_final_norm]
    small_v = [v_a_norm, v_a_sgu_norm, v_a_w_spatial, v_a_b_spatial, v_kv_norm, v_b_norm, v_b_rel_bias,
               v_ffn_norm, v_final_norm]
    flat_g = _pack(small_g_list, 8)
    flat_out = _adamw(flat_g[None, None], _pack(small_w, 8)[None], _pack(small_m, 8)[None],
                      _pack(small_v, 8)[None], "adamw_small")
    unpacked = [_unpack(o[0], small_w) for o in flat_out]
    for idx, name in enumerate(small_names):
        results[name] = [unpacked[kind][idx] for kind in range(4)]

    order = ["a_norm", "a_w_in", "a_sgu_norm", "a_w_spatial", "a_b_spatial", "a_w_out", "kv_norm", "w_kv",
             "b_norm", "b_w_q", "b_rel_bias", "b_w_o", "ffn_norm", "ffn_w_gate_up", "ffn_w_down", "final_norm"]
    outputs = [loss, grad_x]
    for kind in range(4):
        outputs += [results[name][kind] for name in order]
    return tuple(outputs)
```

```python
import math

import jax
import jax.numpy as jnp
from jax import lax
from jax.experimental import pallas as pl
from jax.experimental.pallas import tpu as pltpu

F32 = jnp.float32
BF16 = jnp.bfloat16
MESH = pl.DeviceIdType.MESH
HBM_SPEC = pl.BlockSpec(memory_space=pltpu.HBM)
SEM_SPEC = pl.BlockSpec(memory_space=pltpu.SEMAPHORE)

N_DEV = 8
CHUNK = 64
A_CHUNK = 128
A_GROUPS = 8
N_LEFT_CHUNKS = 8
LEFT = N_LEFT_CHUNKS * CHUNK
PAIR_ROWS = 2 * CHUNK
PAIR_BAND = PAIR_ROWS + LEFT
DIAGONALS = PAIR_BAND + PAIR_ROWS
PAIRS_PER_BLOCK = 2
Q_BLOCK = PAIRS_PER_BLOCK * PAIR_ROWS
K_BLOCK = Q_BLOCK + LEFT
ATTN_UNROLL = 7
MAX_REL = 256
N_REL = 2 * MAX_REL + 1
REL_PAD = 640
HEAD_DIM = 64
HEAD_PAIR = 2 * HEAD_DIM
ATTN_SCALE = HEAD_DIM ** -0.5
EPS = 1e-6
NEG_INF = -1e30
ADAM_LR = 0.001
ADAM_B1 = 0.9
ADAM_B2 = 0.999
ADAM_EPS = 1e-08
ADAM_WD = 0.01
ADAM_STEP = 10
FLAT_LANES = 1024
F32_SUBLANES = 8
BF16_SUBLANES = 16
ADAMW_BLOCK_ELEMS = 256 * 1024
V7X_VMEM_BYTES = 64 * 1024 * 1024
VMEM_FLOOR_BYTES = 32 * 1024 * 1024
VMEM_CEIL_BYTES = V7X_VMEM_BYTES - 8 * 1024 * 1024

NN = (((1,), (0,)), ((), ()))
NT = (((1,), (1,)), ((), ()))
TN = (((0,), (0,)), ((), ()))


def _tile(n, pref):
    return pref if n % pref == 0 else n


def _row_tile(n, pref, mult):
    best = None
    for t in range(mult, min(n, pref) + 1, mult):
        if n % t == 0:
            best = t
    return best if best is not None else n


def _nbytes(shape, dtype):
    n = 1
    for s in shape:
        if s is not None:
            n *= s
    return n * jnp.dtype(dtype).itemsize


def _call(body, name, grid, in_specs, out_specs, out_shape, scratch=(), vmem_bytes=0, aliases=None):
    limit = int(min(max(VMEM_FLOOR_BYTES, vmem_bytes * 5 // 4), VMEM_CEIL_BYTES))
    return pl.pallas_call(
        body,
        name=name,
        grid=grid,
        in_specs=in_specs,
        out_specs=out_specs,
        out_shape=out_shape,
        scratch_shapes=list(scratch),
        input_output_aliases=aliases or {},
        compiler_params=pltpu.CompilerParams(
            dimension_semantics=("arbitrary",) * len(grid), vmem_limit_bytes=limit),
    )


ERFC_P = 0.3275911 / math.sqrt(2.0)
ERFC_HALF_COEFFS = tuple(0.5 * a for a in (1.061405429, -1.453152027, 1.421413741, -0.284496736, 0.254829592))


def _gelu_and_grad(x):
    d = 1.0 + ERFC_P * jnp.abs(x)
    r = pl.reciprocal(d, approx=True)
    t = r * (2.0 - d * r)
    a5, a4, a3, a2, a1 = ERFC_HALF_COEFFS
    ex = jnp.exp(-0.5 * (x * x))
    tail = ((((a5 * t + a4) * t + a3) * t + a2) * t + a1) * t * ex
    cdf = jnp.where(x < 0, tail, 1.0 - tail)
    return x * cdf, cdf + x * ex * (1.0 / math.sqrt(2.0 * math.pi))


def _sigmoid(x):
    return 1.0 / (1.0 + jnp.exp(-x))


def _split3(x):
    hi = x.astype(BF16)
    r1 = x - hi.astype(F32)
    mid = r1.astype(BF16)
    lo = (r1 - mid.astype(F32)).astype(BF16)
    return hi, mid, lo


def _rms_fwd(x, g, name):
    t, d = x.shape
    tm = _tile(t, 512)

    def body(x_ref, g_ref, o_ref):
        xf = x_ref[...]
        r = lax.rsqrt(jnp.mean(xf * xf, axis=-1, keepdims=True) + EPS)
        o_ref[...] = (xf * r * g_ref[...]).astype(o_ref.dtype)

    return _call(
        body, name, (t // tm,),
        [pl.BlockSpec((tm, d), lambda i: (i, 0)), pl.BlockSpec((1, d), lambda i: (0, 0))],
        pl.BlockSpec((tm, d), lambda i: (i, 0)),
        jax.ShapeDtypeStruct((t, d), BF16),
        vmem_bytes=2 * (_nbytes((tm, d), F32) + _nbytes((tm, d), BF16)) + 4 * _nbytes((tm, d), F32),
    )(x, g.reshape(1, d))


def _mm(name, dims, a, b, *, grid, a_spec, b_spec, out_shape, out_spec, acc_shape,
        res=None, res_spec=None, scale=None):
    nk = grid[2]
    has_res = res is not None

    def body(*refs):
        refs = list(refs)
        a_ref = refs.pop(0)
        b_ref = refs.pop(0)
        r_ref = refs.pop(0) if has_res else None
        o_ref = refs.pop(0)
        part = lax.dot_general(a_ref[...].astype(BF16), b_ref[...].astype(BF16), dims,
                               preferred_element_type=F32)

        def finish(acc):
            if scale is not None:
                acc = acc * scale
            if has_res:
                acc = acc + r_ref[...]
            o_ref[...] = acc.astype(o_ref.dtype)

        if nk == 1:
            finish(part)
        else:
            acc_ref = refs.pop(0)
            k = pl.program_id(2)

            @pl.when(k == 0)
            def _():
                acc_ref[...] = part

            @pl.when(k > 0)
            def _():
                acc_ref[...] += part

            @pl.when(k == nk - 1)
            def _():
                finish(acc_ref[...])

    operands = [a, b]
    in_specs = [a_spec, b_spec]
    vmem = 2 * (_nbytes(a_spec.block_shape, a.dtype) + _nbytes(b_spec.block_shape, b.dtype)
                + _nbytes(out_spec.block_shape, out_shape.dtype))
    vmem += 3 * _nbytes(acc_shape, F32)
    if has_res:
        operands.append(res)
        in_specs.append(res_spec)
        vmem += 2 * _nbytes(res_spec.block_shape, res.dtype)
    scratch = [pltpu.VMEM(acc_shape, F32)] if nk > 1 else []
    return _call(body, name, grid, in_specs, out_spec, out_shape, scratch=scratch, vmem_bytes=vmem)(*operands)


def _mm_colblock(name, h, w_g, layer):
    t, k = h.shape
    nb = w_g.shape[3]
    tm = _tile(t, 2048)
    return _mm(
        name, NN, h, w_g, grid=(t // tm, N_DEV, 1),
        a_spec=pl.BlockSpec((tm, k), lambda i, j, kk: (i, 0)),
        b_spec=pl.BlockSpec((None, None, k, nb), lambda i, j, kk: (layer, j, 0, 0)),
        out_shape=jax.ShapeDtypeStruct((t, N_DEV * nb), BF16),
        out_spec=pl.BlockSpec((tm, nb), lambda i, j, kk: (i, j)), acc_shape=(tm, nb))


def _mm_natural(name, a, w, layer, *, res=None, out_dtype=F32, scale=None):
    t, k = a.shape
    n = w.shape[2]
    tm = _tile(t, 1024)
    tn = _tile(n, 1024 if k <= 1024 else 512)
    res_spec = None if res is None else pl.BlockSpec((tm, tn), lambda i, j, kk: (i, j))
    return _mm(
        name, NN, a, w, grid=(t // tm, n // tn, 1),
        a_spec=pl.BlockSpec((tm, k), lambda i, j, kk: (i, 0)),
        b_spec=pl.BlockSpec((None, k, tn), lambda i, j, kk: (layer, 0, j)),
        out_shape=jax.ShapeDtypeStruct((t, n), out_dtype),
        out_spec=pl.BlockSpec((tm, tn), lambda i, j, kk: (i, j)),
        acc_shape=(tm, tn), res=res, res_spec=res_spec, scale=scale)


def _mm_down(name, act, w4, layer, res):
    nblk, t, kb = act.shape
    n = w4.shape[3]
    tm = _tile(t, 1024)

    def body(a_ref, b_ref, r_ref, o_ref):
        acc = r_ref[...]
        for u in range(nblk):
            acc = acc + jnp.dot(a_ref[u], b_ref[u], preferred_element_type=F32)
        o_ref[...] = acc

    row = pl.BlockSpec((tm, n), lambda i: (i, 0))
    return _call(
        body, name, (t // tm,),
        [pl.BlockSpec((nblk, tm, kb), lambda i: (0, i, 0)),
         pl.BlockSpec((None, nblk, kb, n), lambda i: (layer, 0, 0, 0)),
         row],
        row,
        jax.ShapeDtypeStruct((t, n), F32),
        vmem_bytes=2 * (_nbytes((nblk, tm, kb), BF16) + _nbytes((nblk, kb, n), BF16)) + 6 * _nbytes((tm, n), F32),
    )(act, w4, res)


def _mm_t_colblock_norm_bwd(name, dz, w_g, layer, x, g, dx_up, blocked_in=False):
    k = w_g.shape[2]
    nb = w_g.shape[3]
    t = x.shape[0]
    tm = _tile(t, 512)
    if blocked_in:
        a_spec = pl.BlockSpec((N_DEV, tm, nb), lambda i: (0, i, 0))
    else:
        a_spec = pl.BlockSpec((tm, N_DEV * nb), lambda i: (i, 0))

    def body(a_ref, b_ref, x_ref, g_ref, up_ref, dx_ref, dg_ref):
        @pl.when(pl.program_id(0) == 0)
        def _():
            dg_ref[...] = jnp.zeros_like(dg_ref)

        dy = None
        for u in range(N_DEV):
            a = a_ref[u] if blocked_in else a_ref[:, u * nb:(u + 1) * nb]
            term = lax.dot_general(a.astype(BF16), b_ref[u].astype(BF16), NT, preferred_element_type=F32)
            dy = term if dy is None else dy + term
        xf = x_ref[...]
        r = lax.rsqrt(jnp.mean(xf * xf, axis=-1, keepdims=True) + EPS)
        xhat = xf * r
        dxhat = dy * g_ref[...]
        dg_ref[...] += jnp.sum(dy * xhat, axis=0, keepdims=True)
        dx_ref[...] = up_ref[...] + r * (dxhat - xhat * jnp.mean(dxhat * xhat, axis=-1, keepdims=True))

    row = pl.BlockSpec((tm, k), lambda i: (i, 0))
    vec = pl.BlockSpec((1, k), lambda i: (0, 0))
    dx, dg = _call(
        body, name, (t // tm,),
        [a_spec, pl.BlockSpec((None, N_DEV, k, nb), lambda i: (layer, 0, 0, 0)), row, vec, row],
        [row, vec],
        [jax.ShapeDtypeStruct((t, k), F32), jax.ShapeDtypeStruct((1, k), F32)],
        vmem_bytes=2 * N_DEV * (_nbytes((tm, nb), BF16) + _nbytes((k, nb), BF16)) + 10 * _nbytes((tm, k), F32),
    )(dz, w_g, x, g.reshape(1, k), dx_up)
    return dx, dg.reshape(k)


def _ffn_gate_up(name, h, w_g, layer):
    t, k = h.shape
    nb = w_g.shape[3]
    half = N_DEV // 2
    tm = _tile(t, 1024)

    def body(h_ref, wg_ref, wu_ref, dact_ref, act_ref):
        hb = h_ref[...]
        gate = jnp.dot(hb, wg_ref[...], preferred_element_type=F32)
        up = jnp.dot(hb, wu_ref[...], preferred_element_type=F32)
        sig = _sigmoid(gate)
        silu = gate * sig
        dact_ref[0] = (up * (sig * (1.0 + gate * (1.0 - sig)))).astype(BF16)
        dact_ref[1] = silu.astype(BF16)
        act_ref[...] = (silu * up).astype(BF16)

    return _call(
        body, name, (t // tm, half),
        [pl.BlockSpec((tm, k), lambda i, j: (i, 0)),
         pl.BlockSpec((None, None, k, nb), lambda i, j: (layer, j, 0, 0)),
         pl.BlockSpec((None, None, k, nb), lambda i, j: (layer, half + j, 0, 0))],
        [pl.BlockSpec((2, None, tm, nb), lambda i, j: (0, j, i, 0)),
         pl.BlockSpec((None, tm, nb), lambda i, j: (j, i, 0))],
        [jax.ShapeDtypeStruct((2, half, t, nb), BF16), jax.ShapeDtypeStruct((half, t, nb), BF16)],
        vmem_bytes=2 * (_nbytes((tm, k), BF16) + 2 * _nbytes((k, nb), BF16) + 3 * _nbytes((tm, nb), BF16))
        + 8 * _nbytes((tm, nb), F32),
    )(h, w_g, w_g)


def _ffn_down_dx(name, dy, w4, layer, dact):
    t, n = dy.shape
    nblk, kb = w4.shape[1], w4.shape[2]
    tm = _tile(t, 1024)

    def body(dy_ref, w_ref, dact_ref, dgu_ref):
        da = lax.dot_general(dy_ref[...].astype(BF16), w_ref[...], NT, preferred_element_type=F32)
        dgu_ref[0] = (da * dact_ref[0].astype(F32)).astype(BF16)
        dgu_ref[1] = (da * dact_ref[1].astype(F32)).astype(BF16)

    blk = pl.BlockSpec((2, None, tm, kb), lambda i, j: (0, j, i, 0))
    return _call(
        body, name, (t // tm, nblk),
        [pl.BlockSpec((tm, n), lambda i, j: (i, 0)),
         pl.BlockSpec((None, None, kb, n), lambda i, j: (layer, j, 0, 0)),
         blk],
        blk,
        jax.ShapeDtypeStruct((2, nblk, t, kb), BF16),
        vmem_bytes=2 * (_nbytes((tm, n), F32) + _nbytes((kb, n), BF16) + 4 * _nbytes((tm, kb), BF16))
        + 8 * _nbytes((tm, kb), F32),
    )(dy, w4, dact)


def _mm_t_natural(name, dy, w, layer):
    t, n = dy.shape
    k = w.shape[1]
    tm = _tile(t, 1024)
    tk = _tile(k, 1024)
    return _mm(
        name, NT, dy, w, grid=(t // tm, k // tk, 1),
        a_spec=pl.BlockSpec((tm, n), lambda i, j, kk: (i, 0)),
        b_spec=pl.BlockSpec((None, tk, n), lambda i, j, kk: (layer, j, 0)),
        out_shape=jax.ShapeDtypeStruct((t, k), BF16),
        out_spec=pl.BlockSpec((tm, tk), lambda i, j, kk: (i, j)),
        acc_shape=(tm, tk))


def _mm_t_natural_norm_bwd(name, dy, w, layer, x, g, dx_up):
    t, n = dy.shape
    k = w.shape[1]
    tm = _tile(t, 1024)

    def body(a_ref, b_ref, x_ref, g_ref, up_ref, dx_ref, dg_ref):
        @pl.when(pl.program_id(0) == 0)
        def _():
            dg_ref[...] = jnp.zeros_like(dg_ref)

        dh = lax.dot_general(a_ref[...].astype(BF16), b_ref[...], NT, preferred_element_type=F32)
        xf = x_ref[...]
        r = lax.rsqrt(jnp.mean(xf * xf, axis=-1, keepdims=True) + EPS)
        xhat = xf * r
        dxhat = dh * g_ref[...]
        dg_ref[...] += jnp.sum(dh * xhat, axis=0, keepdims=True)
        dx_ref[...] = up_ref[...] + r * (dxhat - xhat * jnp.mean(dxhat * xhat, axis=-1, keepdims=True))

    row = pl.BlockSpec((tm, k), lambda i: (i, 0))
    vec = pl.BlockSpec((1, k), lambda i: (0, 0))
    dx, dg = _call(
        body, name, (t // tm,),
        [pl.BlockSpec((tm, n), lambda i: (i, 0)), pl.BlockSpec((None, k, n), lambda i: (layer, 0, 0)), row, vec, row],
        [row, vec],
        [jax.ShapeDtypeStruct((t, k), F32), jax.ShapeDtypeStruct((1, k), F32)],
        vmem_bytes=2 * (_nbytes((tm, n), dy.dtype) + _nbytes((k, n), BF16)) + 10 * _nbytes((tm, k), F32),
    )(dy, w, x, g.reshape(1, k), dx_up)
    return dx, dg.reshape(k)


def _mm_dw_colblock(name, h, dz, blocked_in=False, transposed=False):
    t, k = h.shape
    nb = dz.shape[2] if blocked_in else dz.shape[1] // N_DEV
    tk = _tile(t, 4096)
    h_spec = pl.BlockSpec((tk, k), lambda i, j, kk: (kk, 0))
    if blocked_in:
        dz_spec = pl.BlockSpec((None, tk, nb), lambda i, j, kk: (j, kk, 0))
    else:
        dz_spec = pl.BlockSpec((tk, nb), lambda i, j, kk: (kk, j))
    rows, cols = (nb, k) if transposed else (k, nb)
    return _mm(
        name, TN, *((dz, h) if transposed else (h, dz)), grid=(1, N_DEV, t // tk),
        a_spec=dz_spec if transposed else h_spec,
        b_spec=h_spec if transposed else dz_spec,
        out_shape=jax.ShapeDtypeStruct((N_DEV, rows, cols), BF16),
        out_spec=pl.BlockSpec((None, rows, cols), lambda i, j, kk: (j, 0, 0)),
        acc_shape=(rows, cols))


def _mm_dw_natural(name, a, dy):
    t, k = a.shape
    n = dy.shape[1]
    tko = _tile(k, 1024)
    tt = _tile(t, 2048)
    out = _mm(
        name, TN, a, dy, grid=(k // tko, 1, t // tt),
        a_spec=pl.BlockSpec((tt, tko), lambda i, j, kk: (kk, i)),
        b_spec=pl.BlockSpec((tt, n), lambda i, j, kk: (kk, 0)),
        out_shape=jax.ShapeDtypeStruct((k, n), BF16),
        out_spec=pl.BlockSpec((tko, n), lambda i, j, kk: (i, 0)),
        acc_shape=(tko, n))
    return out.reshape(N_DEV, k // N_DEV, n)


def _mm_dw_down(name, act, dy):
    nblk, t, kb = act.shape
    n = dy.shape[1]
    tt = _tile(t, 2048)
    out = _mm(
        name, TN, act, dy, grid=(nblk, 1, t // tt),
        a_spec=pl.BlockSpec((None, tt, kb), lambda i, j, kk: (i, kk, 0)),
        b_spec=pl.BlockSpec((tt, n), lambda i, j, kk: (kk, 0)),
        out_shape=jax.ShapeDtypeStruct((nblk, kb, n), BF16),
        out_spec=pl.BlockSpec((None, kb, n), lambda i, j, kk: (i, 0, 0)),
        acc_shape=(kb, n))
    return out.reshape(N_DEV, (nblk * kb) // N_DEV, n)


def _spatial_mask(transposed=False):
    r = lax.broadcasted_iota(jnp.int32, (A_CHUNK, A_CHUNK), 0) // CHUNK
    c = lax.broadcasted_iota(jnp.int32, (A_CHUNK, A_CHUNK), 1) // CHUNK
    return c >= r if transposed else r >= c


def _sgu_tile(t):
    return _tile(t, 2 * A_CHUNK)


def _sgu_fwd(zpre, g_sgu, w_sp, b_full, name):
    t, f2 = zpre.shape
    f = f2 // 2
    gd = f // A_GROUPS
    tm = _sgu_tile(t)

    def body(z_ref, g_ref, w_ref, b_ref, p_ref, zs_ref, dg_ref):
        mask = _spatial_mask()
        wm = [jnp.where(mask, w_ref[g], 0.0).astype(BF16) for g in range(A_GROUPS)]
        for c in range(tm // A_CHUNK):
            rows = pl.ds(c * A_CHUNK, A_CHUNK)
            z, dgelu = _gelu_and_grad(z_ref[rows, :].astype(F32))
            zs_ref[rows, :] = z.astype(BF16)
            dg_ref[rows, :] = dgelu.astype(BF16)
            u = z[:, :f]
            v0 = z[:, f:]
            r = lax.rsqrt(jnp.mean(v0 * v0, axis=-1, keepdims=True) + EPS)
            v1 = (v0 * r * g_ref[...]).astype(BF16)
            for g in range(A_GROUPS):
                cols = slice(g * gd, (g + 1) * gd)
                v2 = jnp.dot(wm[g], v1[:, cols], preferred_element_type=F32) + b_ref[:, cols]
                p_ref[rows, cols] = (u[:, cols] * v2).astype(BF16)

    return _call(
        body, name, (t // tm,),
        [pl.BlockSpec((tm, f2), lambda i: (i, 0)),
         pl.BlockSpec((1, f), lambda i: (0, 0)),
         pl.BlockSpec((A_GROUPS, A_CHUNK, A_CHUNK), lambda i: (0, 0, 0)),
         pl.BlockSpec((A_CHUNK, f), lambda i: (0, 0))],
        [pl.BlockSpec((tm, f), lambda i: (i, 0)), pl.BlockSpec((tm, f2), lambda i: (i, 0)),
         pl.BlockSpec((tm, f2), lambda i: (i, 0))],
        [jax.ShapeDtypeStruct((t, f), BF16), jax.ShapeDtypeStruct((t, f2), BF16), jax.ShapeDtypeStruct((t, f2), BF16)],
        vmem_bytes=6 * _nbytes((tm, f2), BF16) + 2 * _nbytes((tm, f), BF16) + 8 * _nbytes((A_CHUNK, f2), F32),
    )(zpre, g_sgu.reshape(1, f), w_sp, b_full)


def _sgu_bwd(zs, dgs, dp, g_sgu, w_sp, w_sp_t, b_full, name):
    t, f2 = zs.shape
    f = f2 // 2
    gd = f // A_GROUPS
    tm = _sgu_tile(t)
    n_steps = t // tm

    def body(z_ref, dgelu_ref, dp_ref, g_ref, w_ref, wt_ref, b_ref, dz_ref, dw_ref, db_ref, dg_ref, dv1_ref, dbf_ref):
        step = pl.program_id(0)

        @pl.when(step == 0)
        def _():
            dw_ref[...] = jnp.zeros_like(dw_ref)
            dg_ref[...] = jnp.zeros_like(dg_ref)
            dbf_ref[...] = jnp.zeros_like(dbf_ref)

        mask = _spatial_mask()
        mask_t = _spatial_mask(transposed=True)
        wm = [jnp.where(mask, w_ref[g], 0.0).astype(BF16) for g in range(A_GROUPS)]
        wmt = [jnp.where(mask_t, wt_ref[g], 0.0).astype(BF16) for g in range(A_GROUPS)]
        gain = g_ref[...]
        for c in range(tm // A_CHUNK):
            rows = pl.ds(c * A_CHUNK, A_CHUNK)
            z = z_ref[rows, :].astype(F32)
            dgelu = dgelu_ref[rows, :].astype(F32)
            u = z[:, :f]
            v0 = z[:, f:]
            r = lax.rsqrt(jnp.mean(v0 * v0, axis=-1, keepdims=True) + EPS)
            xhat = v0 * r
            v1 = (xhat * gain).astype(BF16)
            dpf = dp_ref[rows, :].astype(F32)
            for g in range(A_GROUPS):
                cols = slice(g * gd, (g + 1) * gd)
                v1g = v1[:, cols]
                v2 = jnp.dot(wm[g], v1g, preferred_element_type=F32) + b_ref[:, cols]
                dpg = dpf[:, cols]
                dz_ref[rows, cols] = (dpg * v2 * dgelu[:, cols]).astype(BF16)
                dv2 = dpg * u[:, cols]
                dbf_ref[:, cols] += dv2
                dv2b = dv2.astype(BF16)
                dwg = lax.dot_general(dv2b, v1g, NT, preferred_element_type=F32)
                dw_ref[g] += jnp.where(mask, dwg, 0.0)
                dv1_ref[:, cols] = jnp.dot(wmt[g], dv2b, preferred_element_type=F32)
            dv1 = dv1_ref[...]
            dxhat = dv1 * gain
            dg_ref[...] += jnp.sum(dv1 * xhat, axis=0, keepdims=True)
            dv0 = r * (dxhat - xhat * jnp.mean(dxhat * xhat, axis=-1, keepdims=True))
            dz_ref[rows, pl.ds(f, f)] = (dv0 * dgelu[:, f:]).astype(BF16)

        @pl.when(step == n_steps - 1)
        def _():
            for g in range(A_GROUPS):
                db_ref[g] = jnp.sum(dbf_ref[:, g * gd:(g + 1) * gd], axis=1, keepdims=True)

    wspec = pl.BlockSpec((A_GROUPS, A_CHUNK, A_CHUNK), lambda i: (0, 0, 0))
    dz, dw, db, dg = _call(
        body, name, (n_steps,),
        [pl.BlockSpec((tm, f2), lambda i: (i, 0)),
         pl.BlockSpec((tm, f2), lambda i: (i, 0)),
         pl.BlockSpec((tm, f), lambda i: (i, 0)),
         pl.BlockSpec((1, f), lambda i: (0, 0)),
         wspec, wspec,
         pl.BlockSpec((A_CHUNK, f), lambda i: (0, 0))],
        [pl.BlockSpec((tm, f2), lambda i: (i, 0)),
         wspec,
         pl.BlockSpec((A_GROUPS, A_CHUNK, 1), lambda i: (0, 0, 0)),
         pl.BlockSpec((1, f), lambda i: (0, 0))],
        [jax.ShapeDtypeStruct((t, f2), BF16),
         jax.ShapeDtypeStruct((A_GROUPS, A_CHUNK, A_CHUNK), F32),
         jax.ShapeDtypeStruct((A_GROUPS, A_CHUNK, 1), F32),
         jax.ShapeDtypeStruct((1, f), F32)],
        scratch=[pltpu.VMEM((A_CHUNK, f), F32), pltpu.VMEM((A_CHUNK, f), F32)],
        vmem_bytes=6 * _nbytes((tm, f2), BF16) + 2 * _nbytes((tm, f), BF16) + 12 * _nbytes((A_CHUNK, f2), F32),
    )(zs, dgs, dp, g_sgu.reshape(1, f), w_sp, w_sp_t, b_full)
    return dz, dw, db.reshape(A_GROUPS, A_CHUNK), dg.reshape(f)


def _pair_valid(qi, col):
    qc = qi // CHUNK
    kc = col // CHUNK
    return (kc >= qc) & (kc <= qc + N_LEFT_CHUNKS)


def _diagonal_onehot():
    e = lax.broadcasted_iota(jnp.int32, (REL_PAD, DIAGONALS), 1)
    idx = jnp.clip(PAIR_BAND - 1 - e, -MAX_REL, MAX_REL) + MAX_REL
    r = lax.broadcasted_iota(jnp.int32, (REL_PAD, DIAGONALS), 0)
    return jnp.where(r == idx, 1.0, 0.0).astype(BF16)


def _bias_build(table, name):
    h = table.shape[0]
    tab = jnp.pad(table, ((0, 0), (0, REL_PAD - N_REL)))

    def body(t_ref, o_ref):
        oh = _diagonal_onehot()
        diag = jnp.zeros((h, DIAGONALS), F32)
        for piece in _split3(t_ref[...]):
            diag += jnp.dot(piece, oh, preferred_element_type=F32)
        col = lax.broadcasted_iota(jnp.int32, (h, PAIR_BAND), 1)
        for qi in range(PAIR_ROWS):
            row = pltpu.roll(diag, (qi - (PAIR_ROWS - 1)) % DIAGONALS, 1)[:, :PAIR_BAND]
            o_ref[qi] = jnp.where(_pair_valid(qi, col), row, NEG_INF)

    out = _call(
        body, name, (1,),
        [pl.BlockSpec((h, REL_PAD), lambda i: (0, 0))],
        pl.BlockSpec((PAIR_ROWS, h, PAIR_BAND), lambda i: (0, 0, 0)),
        jax.ShapeDtypeStruct((PAIR_ROWS, h, PAIR_BAND), F32),
        vmem_bytes=4 * _nbytes((PAIR_ROWS, h, PAIR_BAND), F32),
    )(tab)
    return jnp.transpose(out, (1, 0, 2))


def _bias_block(pair_bias):
    rest = K_BLOCK - PAIR_BAND
    return jnp.concatenate(
        [jnp.pad(pair_bias, ((0, 0), (0, 0), (p * PAIR_ROWS, rest - p * PAIR_ROWS)), constant_values=NEG_INF)
         for p in range(PAIRS_PER_BLOCK)], axis=1)


def _bias_grad(dbias, name):
    h = dbias.shape[0]
    db_t = jnp.transpose(dbias, (1, 0, 2))

    def body(d_ref, o_ref):
        diag = jnp.zeros((h, DIAGONALS), F32)
        for qi in range(PAIR_ROWS):
            diag += pltpu.roll(d_ref[qi], PAIR_ROWS - 1 - qi, 1)
        oh = _diagonal_onehot()
        acc = jnp.zeros((h, REL_PAD), F32)
        for piece in _split3(diag):
            acc += lax.dot_general(piece, oh, NT, preferred_element_type=F32)
        o_ref[...] = acc

    out = _call(
        body, name, (1,),
        [pl.BlockSpec((PAIR_ROWS, h, DIAGONALS), lambda i: (0, 0, 0))],
        pl.BlockSpec((h, REL_PAD), lambda i: (0, 0)),
        jax.ShapeDtypeStruct((h, REL_PAD), F32),
        vmem_bytes=4 * _nbytes((PAIR_ROWS, h, DIAGONALS), F32),
    )(db_t)
    return out[:, :N_REL]


def _head_masks():
    lane = lax.broadcasted_iota(jnp.int32, (Q_BLOCK, HEAD_PAIR), 1)
    return lane < HEAD_DIM, lane >= HEAD_DIM


def _block_scores(qm, kb, bias, valid):
    s = lax.dot_general(qm, kb, NT, preferred_element_type=F32) + bias
    return s if valid is None else jnp.where(valid, s, NEG_INF)


def _softmax_rows(s):
    e = jnp.exp(s - jnp.max(s, axis=-1, keepdims=True))
    return e * (1.0 / jnp.sum(e, axis=-1, keepdims=True))


def _padded_then_plain(step, n_blocks):
    n_padded = min(LEFT // Q_BLOCK, n_blocks)
    lax.fori_loop(0, n_padded, lambda j, c: step(j, c, True), 0, unroll=True)
    lax.fori_loop(n_padded, n_blocks, lambda j, c: step(j, c, False), 0, unroll=ATTN_UNROLL)


def _attn_fwd(q, kvpad, bias, name):
    t, d = q.shape
    n_pairs = d // HEAD_PAIR
    n_blocks = t // Q_BLOCK

    def body(q_ref, k_ref, v_ref, b_ref, o_ref):
        masks = _head_masks()
        key = lax.broadcasted_iota(jnp.int32, (Q_BLOCK, K_BLOCK), 1)

        def step(j, carry, padded):
            r0 = pl.multiple_of(j * Q_BLOCK, Q_BLOCK)
            q2 = q_ref[pl.ds(r0, Q_BLOCK), :].astype(F32)
            kb = k_ref[pl.ds(r0, K_BLOCK), :]
            vb = v_ref[pl.ds(r0, K_BLOCK), :]
            valid = key >= LEFT - j * Q_BLOCK if padded else None
            scores = [_block_scores(jnp.where(masks[a], q2, 0.0).astype(BF16), kb, b_ref[a], valid) for a in range(2)]
            probs = [_softmax_rows(s).astype(BF16) for s in scores]
            outs = [jnp.dot(p, vb, preferred_element_type=F32) for p in probs]
            o_ref[pl.ds(r0, Q_BLOCK), :] = jnp.where(masks[0], outs[0], outs[1]).astype(BF16)
            return carry

        _padded_then_plain(step, n_blocks)

    return _call(
        body, name, (n_pairs,),
        [pl.BlockSpec((t, HEAD_PAIR), lambda p: (0, p)),
         pl.BlockSpec((LEFT + t, HEAD_PAIR), lambda p: (0, p)),
         pl.BlockSpec((LEFT + t, HEAD_PAIR), lambda p: (0, n_pairs + p)),
         pl.BlockSpec((2, Q_BLOCK, K_BLOCK), lambda p: (p, 0, 0))],
        pl.BlockSpec((t, HEAD_PAIR), lambda p: (0, p)),
        jax.ShapeDtypeStruct((t, d), BF16),
        vmem_bytes=8 * _nbytes((LEFT + t, HEAD_PAIR), BF16) + 12 * _nbytes((2, Q_BLOCK, K_BLOCK), F32),
    )(q, kvpad, kvpad, bias)


def _attn_bwd(q, kvpad, bias, o, do, dk_in, dv_in, name):
    t, d = q.shape
    n_pairs = d // HEAD_PAIR
    n_blocks = t // Q_BLOCK
    has_in = dk_in is not None

    def body(*refs):
        refs = list(refs)
        q_ref, k_ref, v_ref, b_ref, o_ref, do_ref = refs[:6]
        refs = refs[6:]
        if has_in:
            dki_ref, dvi_ref = refs[:2]
            refs = refs[2:]
        dq_ref, dk_ref, dv_ref, db_ref = refs
        masks = _head_masks()
        key = lax.broadcasted_iota(jnp.int32, (Q_BLOCK, K_BLOCK), 1)
        if has_in:
            dk_ref[...] = dki_ref[...]
            dv_ref[...] = dvi_ref[...]
        else:
            dk_ref[...] = jnp.zeros_like(dk_ref)
            dv_ref[...] = jnp.zeros_like(dv_ref)
        db_ref[...] = jnp.zeros_like(db_ref)

        def step(j, carry, padded):
            r0 = pl.multiple_of(j * Q_BLOCK, Q_BLOCK)
            q2 = q_ref[pl.ds(r0, Q_BLOCK), :].astype(F32)
            do2 = do_ref[pl.ds(r0, Q_BLOCK), :].astype(F32)
            do_o = do2 * o_ref[pl.ds(r0, Q_BLOCK), :].astype(F32)
            kb = k_ref[pl.ds(r0, K_BLOCK), :]
            vb = v_ref[pl.ds(r0, K_BLOCK), :]
            valid = key >= LEFT - j * Q_BLOCK if padded else None
            heads = range(2)
            qms = [jnp.where(masks[a], q2, 0.0).astype(BF16) for a in heads]
            doms = [jnp.where(masks[a], do2, 0.0).astype(BF16) for a in heads]
            scores = [_block_scores(qms[a], kb, b_ref[a], valid) for a in heads]
            dps = [lax.dot_general(doms[a], vb, NT, preferred_element_type=F32) for a in heads]
            ps = [_softmax_rows(s) for s in scores]
            rows = [jnp.sum(jnp.where(masks[a], do_o, 0.0), axis=-1, keepdims=True) for a in heads]
            dss = [ps[a] * (dps[a] - rows[a]) for a in heads]
            for a in heads:
                for pair in range(PAIRS_PER_BLOCK):
                    lo = pair * PAIR_ROWS
                    db_ref[a, :, pl.ds(0, PAIR_BAND)] += dss[a][lo:lo + PAIR_ROWS, lo:lo + PAIR_BAND]
            dsbs = [ds.astype(BF16) for ds in dss]
            pbs = [p.astype(BF16) for p in ps]
            dqs = [jnp.dot(dsbs[a], kb, preferred_element_type=F32) for a in heads]
            dk_acc = sum(lax.dot_general(dsbs[a], qms[a], TN, preferred_element_type=F32) for a in heads)
            dv_acc = sum(lax.dot_general(pbs[a], doms[a], TN, preferred_element_type=F32) for a in heads)
            dq = jnp.where(masks[0], dqs[0], dqs[1]) * ATTN_SCALE
            dq_ref[pl.ds(r0, Q_BLOCK), :] = dq.astype(BF16)
            dk_ref[pl.ds(r0, K_BLOCK), :] += dk_acc
            dv_ref[pl.ds(r0, K_BLOCK), :] += dv_acc
            return carry

        _padded_then_plain(step, n_blocks)

    q_spec = pl.BlockSpec((t, HEAD_PAIR), lambda p: (0, p))
    kv_spec = pl.BlockSpec((LEFT + t, HEAD_PAIR), lambda p: (0, p))
    operands = [q, kvpad, kvpad, bias, o, do]
    in_specs = [q_spec, kv_spec, pl.BlockSpec((LEFT + t, HEAD_PAIR), lambda p: (0, n_pairs + p)),
                pl.BlockSpec((2, Q_BLOCK, K_BLOCK), lambda p: (p, 0, 0)), q_spec, q_spec]
    aliases = None
    if has_in:
        operands += [dk_in, dv_in]
        in_specs += [kv_spec, kv_spec]
        aliases = {6: 1, 7: 2}
    return _call(
        body, name, (n_pairs,),
        in_specs,
        [q_spec, kv_spec, kv_spec, pl.BlockSpec((2, PAIR_ROWS, DIAGONALS), lambda p: (p, 0, 0))],
        [jax.ShapeDtypeStruct((t, d), BF16),
         jax.ShapeDtypeStruct((LEFT + t, d), F32),
         jax.ShapeDtypeStruct((LEFT + t, d), F32),
         jax.ShapeDtypeStruct((d // HEAD_DIM, PAIR_ROWS, DIAGONALS), F32)],
        vmem_bytes=10 * _nbytes((LEFT + t, HEAD_PAIR), BF16) + 8 * _nbytes((LEFT + t, HEAD_PAIR), F32)
        + 16 * _nbytes((2, Q_BLOCK, K_BLOCK), F32),
        aliases=aliases,
    )(*operands)


def _loss_head(x, g, target, name):
    t, d = x.shape
    tm = _tile(t, 512)

    def body(x_ref, g_ref, t_ref, dx_ref, loss_ref, dg_ref):
        @pl.when(pl.program_id(0) == 0)
        def _():
            loss_ref[...] = jnp.zeros_like(loss_ref)
            dg_ref[...] = jnp.zeros_like(dg_ref)

        xf = x_ref[...]
        r = lax.rsqrt(jnp.mean(xf * xf, axis=-1, keepdims=True) + EPS)
        xhat = xf * r
        diff = xhat * g_ref[...] - t_ref[...]
        row_loss = jnp.mean(diff * diff, axis=-1, keepdims=True)
        loss_ref[...] += 0.5 * jnp.sum(row_loss, axis=0, keepdims=True)
        dy = diff * (1.0 / d)
        dg_ref[...] += jnp.sum(dy * xhat, axis=0, keepdims=True)
        dxhat = dy * g_ref[...]
        dx_ref[...] = r * (dxhat - xhat * jnp.mean(dxhat * xhat, axis=-1, keepdims=True))

    row = pl.BlockSpec((tm, d), lambda i: (i, 0))
    vec = pl.BlockSpec((1, d), lambda i: (0, 0))
    dx, loss, dg = _call(
        body, name, (t // tm,),
        [row, vec, row],
        [row, pl.BlockSpec((1, 1), lambda i: (0, 0)), vec],
        [jax.ShapeDtypeStruct((t, d), F32), jax.ShapeDtypeStruct((1, 1), F32), jax.ShapeDtypeStruct((1, d), F32)],
        vmem_bytes=10 * _nbytes((tm, d), F32),
    )(x, g.reshape(1, d), target)
    return dx, loss[0, 0], dg.reshape(d)


def _adamw_store(g, w_ref, m_ref, v_ref, g_ref, d_ref, nm_ref, nv_ref):
    c1 = 1.0 / (1.0 - ADAM_B1 ** ADAM_STEP)
    c2 = 1.0 / (1.0 - ADAM_B2 ** ADAM_STEP)
    nm = ADAM_B1 * m_ref[...] + (1.0 - ADAM_B1) * g
    nv = ADAM_B2 * v_ref[...] + (1.0 - ADAM_B2) * (g * g)
    g_ref[...] = g
    nm_ref[...] = nm
    nv_ref[...] = nv
    d_ref[...] = -ADAM_LR * ((nm * c1) / (jnp.sqrt(nv * c2) + ADAM_EPS) + ADAM_WD * w_ref[...])


def _adamw_layer(recv, own, w, m, v, layer, prev, me, name):
    n_src, r, c = recv.shape
    tr = _row_tile(r, max(BF16_SUBLANES, ADAMW_BLOCK_ELEMS // c), BF16_SUBLANES)
    first = prev is None

    def body(me_ref, recv_ref, own_ref, w_ref, m_ref, v_ref, *rest):
        mine = me_ref[0]
        own_part = own_ref[...].astype(F32)
        g = None
        for s in range(n_src):
            part = jnp.where(mine == s, own_part, recv_ref[s].astype(F32))
            g = part if g is None else g + part
        _adamw_store(g, w_ref, m_ref, v_ref, *rest[-4:])

    blk = pl.BlockSpec((None, tr, c), lambda i, me_ref: (layer, i, 0))
    any_spec = pl.BlockSpec(memory_space=pl.ANY)
    out = jax.ShapeDtypeStruct(w.shape, F32)
    operands = [me, recv, own, w, m, v] + ([] if first else list(prev))
    vmem = 2 * _nbytes((n_src + 1, tr, c), BF16) + 18 * _nbytes((tr, c), F32)
    return pl.pallas_call(
        body,
        name=name,
        grid_spec=pltpu.PrefetchScalarGridSpec(
            num_scalar_prefetch=1,
            grid=(r // tr,),
            in_specs=[pl.BlockSpec((n_src, tr, c), lambda i, me_ref: (0, i, 0)),
                      pl.BlockSpec((None, tr, c), lambda i, me_ref: (me_ref[0], i, 0)),
                      blk, blk, blk] + ([] if first else [any_spec] * 4),
            out_specs=[blk, blk, blk, blk],
        ),
        out_shape=[out, out, out, out],
        input_output_aliases={} if first else {6 + j: j for j in range(4)},
        compiler_params=pltpu.CompilerParams(
            dimension_semantics=("arbitrary",),
            vmem_limit_bytes=int(min(max(VMEM_FLOOR_BYTES, vmem * 5 // 4), VMEM_CEIL_BYTES))),
    )(*operands)


def _adamw(parts, w, m, v, name):
    n_layers, n_src, r, c = parts.shape
    mult = BF16_SUBLANES if parts.dtype == BF16 else F32_SUBLANES
    tr = _row_tile(r, max(mult, ADAMW_BLOCK_ELEMS // c), mult)

    def body(p_ref, w_ref, m_ref, v_ref, g_ref, d_ref, nm_ref, nv_ref):
        g = p_ref[0].astype(F32)
        for s in range(1, n_src):
            g = g + p_ref[s].astype(F32)
        _adamw_store(g, w_ref, m_ref, v_ref, g_ref, d_ref, nm_ref, nv_ref)

    blk = pl.BlockSpec((None, tr, c), lambda l, i: (l, i, 0))
    out = jax.ShapeDtypeStruct((n_layers, r, c), F32)
    return _call(
        body, name, (n_layers, r // tr),
        [pl.BlockSpec((None, n_src, tr, c), lambda l, i: (l, 0, i, 0)), blk, blk, blk],
        [blk, blk, blk, blk],
        [out, out, out, out],
        vmem_bytes=2 * _nbytes((n_src, tr, c), parts.dtype) + 18 * _nbytes((tr, c), F32),
    )(parts, w, m, v)


def _ordered_sum(parts, name):
    n_src, r, c = parts.shape

    def body(p_ref, o_ref):
        acc = p_ref[0]
        for s in range(1, n_src):
            acc = acc + p_ref[s]
        o_ref[...] = acc

    return _call(
        body, name, (1,),
        [pl.BlockSpec((n_src, r, c), lambda i: (0, 0, 0))],
        pl.BlockSpec((r, c), lambda i: (0, 0)),
        jax.ShapeDtypeStruct((r, c), F32),
        vmem_bytes=4 * _nbytes((n_src, r, c), F32),
    )(parts)


def _position():
    return lax.axis_index("x"), lax.axis_index("y"), lax.axis_index("c")


def _linear(p):
    return 4 * p[0] + 2 * p[1] + p[2]


def _all_gather(shards, name):
    n = len(shards)

    def body(*refs):
        ins, outs = refs[:n], refs[n:2 * n]
        send_sems, recv_sems, local_sems = refs[2 * n:]
        x, y, c = _position()
        me, sibling = (x, y, c), (x, y, 1 - c)
        chips = [(1 - x, y), (x, 1 - y), (1 - x, 1 - y)]

        def slab(t, p):
            return outs[t].at[:, _linear(p)]

        def copy(t, k, block, to, src=None):
            return pltpu.make_async_remote_copy(
                src_ref=slab(t, block) if src is None else src,
                dst_ref=slab(t, block),
                send_sem=send_sems.at[t, k],
                recv_sem=recv_sems.at[t, k],
                device_id=to,
                device_id_type=MESH,
            )

        started = []
        for t in range(n):
            mine = pltpu.make_async_copy(ins[t], slab(t, me), local_sems.at[t])
            mine.start()
            started.append(mine)
        sends = []
        for t in range(n):
            first = [copy(t, 0, me, sibling, src=ins[t])]
            first += [copy(t, 1 + j, me, (*chip, c), src=ins[t]) for j, chip in enumerate(chips)]
            for cp in first:
                cp.start()
            sends += first
        for t in range(n):
            for j, chip in enumerate(chips):
                copy(t, 1 + j, (*chip, c), me).wait_recv()
                passed = copy(t, 4 + j, (*chip, c), sibling)
                passed.start()
                sends.append(passed)
        for t in range(n):
            copy(t, 0, sibling, me).wait_recv()
            for j, chip in enumerate(chips):
                copy(t, 4 + j, (*chip, 1 - c), me).wait_recv()
        for cp in sends:
            cp.wait_send()
        for mine in started:
            mine.wait()

    out_shape = [jax.ShapeDtypeStruct((s.shape[0], N_DEV) + s.shape[1:], s.dtype) for s in shards]
    return pl.pallas_call(
        body,
        name=name,
        in_specs=[HBM_SPEC] * n,
        out_specs=[HBM_SPEC] * n,
        out_shape=out_shape,
        scratch_shapes=[
            pltpu.SemaphoreType.DMA((n, N_DEV - 1)),
            pltpu.SemaphoreType.DMA((n, N_DEV - 1)),
            pltpu.SemaphoreType.DMA((n,)),
        ],
    )(*shards)


def _exchange(blocks, name):
    n = len(blocks)

    def body(*refs):
        ins, outs = refs[:n], refs[n:2 * n]
        send_sems, recv_sems, local_sems = refs[2 * n:]
        x, y, c = _position()
        me = _linear((x, y, c))
        flips = [(fx, fy, fc) for fx in (0, 1) for fy in (0, 1) for fc in (0, 1)][1:]

        def peer_of(flip):
            fx, fy, fc = flip
            return (1 - x if fx else x, 1 - y if fy else y, 1 - c if fc else c)

        def copy(t, k, peer):
            return pltpu.make_async_remote_copy(
                src_ref=ins[t].at[:, _linear(peer)],
                dst_ref=outs[t].at[:, me],
                send_sem=send_sems.at[t, k],
                recv_sem=recv_sems.at[t, k],
                device_id=peer,
                device_id_type=MESH,
            )

        def arrival(t, k, peer):
            return pltpu.make_async_remote_copy(
                src_ref=ins[t].at[:, _linear(peer)],
                dst_ref=outs[t].at[:, _linear(peer)],
                send_sem=send_sems.at[t, k],
                recv_sem=recv_sems.at[t, k],
                device_id=peer,
                device_id_type=MESH,
            )

        own = []
        for t in range(n):
            cp = pltpu.make_async_copy(ins[t].at[:, me], outs[t].at[:, me], local_sems.at[t])
            cp.start()
            own.append(cp)
        sends = []
        for t in range(n):
            for k, flip in enumerate(flips):
                cp = copy(t, k, peer_of(flip))
                cp.start()
                sends.append(cp)
        for t in range(n):
            for k, flip in enumerate(flips):
                arrival(t, k, peer_of(flip)).wait_recv()
        for cp in sends:
            cp.wait_send()
        for cp in own:
            cp.wait()

    out_shape = [jax.ShapeDtypeStruct(b.shape, b.dtype) for b in blocks]
    return pl.pallas_call(
        body,
        name=name,
        in_specs=[HBM_SPEC] * n,
        out_specs=[HBM_SPEC] * n,
        out_shape=out_shape,
        scratch_shapes=[
            pltpu.SemaphoreType.DMA((n, N_DEV - 1)),
            pltpu.SemaphoreType.DMA((n, N_DEV - 1)),
            pltpu.SemaphoreType.DMA((n,)),
        ],
    )(*blocks)


def _peers():
    x, y, c = _position()
    flips = [(fx, fy, fc) for fx in (0, 1) for fy in (0, 1) for fc in (0, 1)][1:]
    return [(1 - x if fx else x, 1 - y if fy else y, 1 - c if fc else c) for fx, fy, fc in flips]


SIBLING, OTHER_CHIPS = (0,), (1, 3, 5)
COPY_PEERS = {"gather": tuple(range(N_DEV - 1)), "exchange": tuple(range(N_DEV - 1)),
              "chips": SIBLING + OTHER_CHIPS, "forward": OTHER_CHIPS}


def _split_copy(kind, src_ref, land_ref, k, send_sem, recv_sem, starting):
    peers = _peers()
    peer = peers[SIBLING[0]] if kind == "forward" else peers[k]
    me = _linear(_position())
    if kind == "forward":
        slab = _linear(peers[k]) if starting else 0
        src, dst = land_ref.at[slab], land_ref.at[slab]
    elif kind == "exchange":
        src, dst = src_ref.at[_linear(peer) if starting else 0], land_ref.at[me if starting else 0]
    else:
        src, dst = src_ref, land_ref.at[me if starting else 0]
    return pltpu.make_async_remote_copy(src_ref=src, dst_ref=dst, send_sem=send_sem, recv_sem=recv_sem,
                                        device_id=peer, device_id_type=MESH)


def _split_start(groups, carry, name):
    arrays = [a for _, srcs, lands in groups for a in list(srcs) + list(lands)] + [carry]

    def body(*refs):
        ins, sems = refs[:len(arrays)], refs[len(arrays):len(arrays) + 2 * len(groups)]
        at = 0
        for g, (kind, srcs, lands) in enumerate(groups):
            src_refs, land_refs = ins[at:at + len(srcs)], ins[at + len(srcs):at + len(srcs) + len(lands)]
            at += len(srcs) + len(lands)
            peers = COPY_PEERS[kind]
            for t in range(len(lands)):
                for slot, k in enumerate(peers):
                    sem = t * len(peers) + slot
                    _split_copy(kind, src_refs[t] if srcs else None, land_refs[t], k,
                                sems[2 * g].at[sem], sems[2 * g + 1].at[sem], True).start()

    sem_shapes = [pltpu.SemaphoreType.DMA((len(lands) * len(COPY_PEERS[kind]),))
                  for kind, _, lands in groups for _ in range(2)]
    out = pl.pallas_call(
        body,
        name=name,
        in_specs=[HBM_SPEC] * len(arrays),
        out_specs=[SEM_SPEC] * len(sem_shapes) + [HBM_SPEC] * len(arrays),
        out_shape=sem_shapes + [pltpu.HBM(a.shape, a.dtype) for a in arrays],
        input_output_aliases={i: len(sem_shapes) + i for i in range(len(arrays))},
        compiler_params=pltpu.CompilerParams(has_side_effects=pltpu.SideEffectType.DATAFLOW_SIDE_EFFECTING),
    )(*[pltpu.with_memory_space_constraint(a, pltpu.HBM) for a in arrays])
    sems, thru = out[:len(sem_shapes)], out[len(sem_shapes):]
    started, at = [], 0
    for g, (kind, srcs, lands) in enumerate(groups):
        n_s, n_l = len(srcs), len(lands)
        started.append((kind, sems[2 * g], sems[2 * g + 1], thru[at:at + n_s], thru[at + n_s:at + n_s + n_l]))
        at += n_s + n_l
    return started, thru[-1]


def _split_wait(started, after, name):
    kind, send_sems, recv_sems, srcs, lands = started
    n_s, n_l = len(srcs), len(lands)
    peers = COPY_PEERS[kind]

    def body(*refs):
        src_refs, land_refs = refs[:n_s], refs[n_s:n_s + n_l]
        send_ref, recv_ref = refs[n_s + n_l], refs[n_s + n_l + 1]
        for t in range(n_l):
            for slot, k in enumerate(peers):
                sem = t * len(peers) + slot
                copy = _split_copy(kind, src_refs[t] if n_s else None, land_refs[t], k,
                                   send_ref.at[sem], recv_ref.at[sem], False)
                copy.wait_send()
                copy.wait_recv()

    arrays = list(srcs) + list(lands)
    out = pl.pallas_call(
        body,
        name=name,
        in_specs=[HBM_SPEC] * len(arrays) + [SEM_SPEC, SEM_SPEC, pl.BlockSpec(memory_space=pl.ANY)],
        out_specs=[HBM_SPEC] * len(arrays),
        out_shape=[pltpu.HBM(a.shape, a.dtype) for a in arrays],
        input_output_aliases={i: i for i in range(len(arrays))},
        compiler_params=pltpu.CompilerParams(has_side_effects=pltpu.SideEffectType.DATAFLOW_SIDE_EFFECTING),
    )(*arrays, send_sems, recv_sems, after)
    return out[:n_s], out[n_s:]


def _pack(arrays, row_multiple):
    flat = jnp.concatenate([a.reshape(-1) for a in arrays])
    quantum = row_multiple * FLAT_LANES
    padded = -(-flat.shape[0] // quantum) * quantum
    return jnp.pad(flat, (0, padded - flat.shape[0])).reshape(-1, FLAT_LANES)


def _unpack(flat, like):
    flat = flat.reshape(-1)
    out, at = [], 0
    for a in like:
        size = math.prod(a.shape)
        out.append(flat[at:at + size].reshape(a.shape))
        at += size
    return out


def kernel(x, a_norm, a_w_in, a_sgu_norm, a_w_spatial, a_b_spatial, a_w_out, kv_norm, w_kv, b_norm, b_w_q, b_rel_bias, b_w_o, ffn_norm, ffn_w_gate_up, ffn_w_down, final_norm, loss_target, m_a_norm, m_a_w_in, m_a_sgu_norm, m_a_w_spatial, m_a_b_spatial, m_a_w_out, m_kv_norm, m_w_kv, m_b_norm, m_b_w_q, m_b_rel_bias, m_b_w_o, m_ffn_norm, m_ffn_w_gate_up, m_ffn_w_down, m_final_norm, v_a_norm, v_a_w_in, v_a_sgu_norm, v_a_w_spatial, v_a_b_spatial, v_a_w_out, v_kv_norm, v_w_kv, v_b_norm, v_b_w_q, v_b_rel_bias, v_b_w_o, v_ffn_norm, v_ffn_w_gate_up, v_ffn_w_down, v_final_norm):
    xs = x[0]
    target = loss_target[0]
    t, d = xs.shape
    n_a = a_w_in.shape[0]
    n_b = b_w_q.shape[0]
    depth = ffn_w_gate_up.shape[0]
    f_a = a_w_out.shape[1] * N_DEV
    gd = f_a // A_GROUPS
    nb_ffn = ffn_w_gate_up.shape[2]
    me = _linear(_position())

    small_rows = -(-(a_norm.size + a_sgu_norm.size) // (8 * 128)) * 8
    small = jnp.pad(jnp.concatenate([a_norm.reshape(-1), a_sgu_norm.reshape(-1)]),
                    (0, small_rows * 128 - a_norm.size - a_sgu_norm.size)).reshape(1, small_rows, 128)

    def shard(w, layer=None):
        return (w if layer is None else w[layer]).astype(BF16)

    stages = []
    for layer in range(depth):
        if layer == 0:
            stages += [("a0", [shard(a_w_in, 0)]), ("a0_out", [shard(a_w_out, 0)])]
        elif layer < n_a:
            stages.append((f"a{layer}", [shard(a_w_in, layer), shard(a_w_out, layer)]))
        else:
            i = layer - n_a
            shared = [shard(w_kv)] if i == 0 else []
            stages.append((f"b{i}", shared + [shard(b_w_q, i), shard(b_w_o, i)]))
        stages.append((f"f{layer}", [shard(ffn_w_gate_up, layer), shard(ffn_w_down, layer)]))
    first = _all_gather([s[None] for s in stages[0][1]] + [small], "gather_first")
    gathered = {stages[0][0]: [g[0] for g in first[:-1]]}
    small_g = first[-1].reshape(N_DEV, -1)
    a_norm_full = small_g[:, :a_norm.size].reshape(N_DEV, n_a, -1).transpose(1, 0, 2).reshape(n_a, d)
    a_sgu_full = small_g[:, a_norm.size:a_norm.size + a_sgu_norm.size].reshape(
        N_DEV, n_a, -1).transpose(1, 0, 2).reshape(n_a, f_a)
    two_level = ("f0", "a1", "f1")
    later = [("chips" if key in two_level else "gather", shards,
              [lax.dynamic_update_slice(lax.empty((N_DEV,) + s.shape, BF16), s[None], (me, 0, 0)) for s in shards])
             for key, shards in stages[1:]]
    started, a_norm_full = _split_start(later, a_norm_full, "gather_start")
    in_flight = {key: group for (key, _), group in zip(stages[1:], started)}

    def pass_on(key, carry):
        if key in two_level and key in in_flight and in_flight[key][0] == "chips":
            _, lands = _split_wait(in_flight.pop(key), carry, f"gather_wait_{key}_chips")
            (in_flight[key],), carry = _split_start([("forward", [], lands)], carry, f"gather_pass_on_{key}")
        return carry

    def weights(key, after):
        if key not in gathered:
            _, gathered[key] = _split_wait(in_flight.pop(key), after, f"gather_wait_{key}")
        return gathered[key]

    rows_down = ffn_w_down.shape[1]

    def mixer_a_weights(i, after):
        if i == 0:
            (w_in,), (w_out,) = weights("a0", after[0]), weights("a0_out", after[1])
        else:
            w_in, w_out = weights(f"a{i}", after[0])
        return w_in[None], w_out.reshape(1, f_a, d)

    def mixer_b_weights(i, after):
        ws = weights(f"b{i}", after)
        return ws[-2].reshape(1, d, d), ws[-1].reshape(1, d, d)

    def ffn_weights(layer, after):
        w_gu, w_dn = weights(f"f{layer}", after)
        return w_gu[None], w_dn.reshape(1, N_DEV // 2, 2 * rows_down, d)

    w_sp_t = jnp.swapaxes(a_w_spatial, -1, -2)
    b_full = jnp.repeat(jnp.swapaxes(a_b_spatial, -1, -2), gd, axis=-1)

    saved = []

    def ffn_fwd(xin, layer):
        hf = _rms_fwd(xin, ffn_norm[layer], f"ffn_norm_fwd_{layer}")
        w_gu, w_dn = ffn_weights(layer, xin)
        dact, act = _ffn_gate_up(f"ffn_gate_up_{layer}", hf, w_gu, 0)
        act = pass_on(f"a{layer + 1}", act)
        xout = _mm_down(f"ffn_down_{layer}", act, w_dn, 0, xin)
        return xout, (xin, hf, dact, act)

    for i in range(n_a):
        h = _rms_fwd(xs, a_norm_full[i], f"a_norm_fwd_{i}")
        zpre = _mm_colblock(f"a_in_{i}", h, weights(f"a{i}", xs)[0][None], 0)
        p, zs, dgs = _sgu_fwd(zpre, a_sgu_full[i], a_w_spatial[i], b_full[i], f"a_sgu_fwd_{i}")
        p = pass_on(f"f{i}", p)
        w_in, w_out = mixer_a_weights(i, (xs, p))
        x_mid = _mm_natural(f"a_out_{i}", p, w_out, 0, res=xs)
        x_out, ffn_saved = ffn_fwd(x_mid, i)
        saved.append((xs, h, zs, dgs, p, ffn_saved))
        xs = x_out

    x_kv = xs
    w_kv_g = weights("b0", x_kv)[0][None]
    h_kv = _rms_fwd(x_kv, kv_norm, "kv_norm_fwd")
    kv = _mm_colblock("kv_proj", h_kv, w_kv_g, 0)
    kvpad = jnp.pad(kv, ((LEFT, 0), (0, 0)))

    biases = [_bias_block(_bias_build(b_rel_bias[i], f"rel_bias_{i}")) for i in range(n_b)]
    for i in range(n_b):
        layer = n_a + i
        w_q, w_o = mixer_b_weights(i, xs)
        hb = _rms_fwd(xs, b_norm[i], f"b_norm_fwd_{i}")
        q = _mm_natural(f"b_q_{i}", hb, w_q, 0, out_dtype=BF16, scale=ATTN_SCALE)
        o = _attn_fwd(q, kvpad, biases[i], f"b_attn_fwd_{i}")
        x_mid = _mm_natural(f"b_o_{i}", o, w_o, 0, res=xs)
        x_out, ffn_saved = ffn_fwd(x_mid, layer)
        saved.append((xs, hb, q, o, ffn_saved))
        xs = x_out

    dx, loss_local, g_final = _loss_head(xs, final_norm, target, "loss_head")
    loss = lax.psum(loss_local, ("x", "y", "c"))

    big_grads = {}
    pending = []
    in_flight_grads = []

    def start_exchange(dx, tag):
        srcs = [big_grads[key] for key in pending]
        lands = [lax.empty(s.shape, BF16) for s in srcs]
        (group,), dx = _split_start([("exchange", srcs, lands)], dx, f"exchange_start_{tag}")
        in_flight_grads.append((list(pending), group, tag))
        pending.clear()
        return dx

    g_ffn_norm = [None] * depth
    g_a_norm = [None] * n_a
    g_a_sgu = [None] * n_a
    g_w_sp = [None] * n_a
    g_b_sp = [None] * n_a
    g_b_norm = [None] * n_b
    g_rel = [None] * n_b

    def ffn_bwd(dx, layer, ffn_saved):
        eager = layer < n_a
        xin, hf, dact, act = ffn_saved
        big_grads["ffn_w_down", layer] = _mm_dw_down(f"ffn_down_dw_{layer}", act, dx)
        pending.append(("ffn_w_down", layer))
        if eager:
            dx = start_exchange(dx, f"f{layer}_down")
        w_gu, w_dn = ffn_weights(layer, xin)
        dgu = _ffn_down_dx(f"ffn_down_dx_{layer}", dx, w_dn, 0, dact).reshape(N_DEV, t, nb_ffn)
        big_grads["ffn_w_gate_up", layer] = _mm_dw_colblock(
            f"ffn_gate_up_dw_{layer}", hf, dgu, blocked_in=True, transposed=True)
        pending.append(("ffn_w_gate_up", layer))
        if eager:
            dx = start_exchange(dx, f"f{layer}_gate_up")
        dx, g_ffn_norm[layer] = _mm_t_colblock_norm_bwd(
            f"ffn_gate_up_dx_{layer}", dgu, w_gu, 0, xin, ffn_norm[layer], dx, blocked_in=True)
        return dx

    dk = dv = None
    for i in reversed(range(n_b)):
        layer = n_a + i
        x_in, hb, q, o, ffn_saved = saved[layer]
        dx = ffn_bwd(dx, layer, ffn_saved)
        big_grads["b_w_o", i] = _mm_dw_natural(f"b_o_dw_{i}", o, dx)
        w_q, w_o = mixer_b_weights(i, x_in)
        do = _mm_t_natural(f"b_o_dx_{i}", dx, w_o, 0)
        dq, dk, dv, dbias = _attn_bwd(q, kvpad, biases[i], o, do, dk, dv, f"b_attn_bwd_{i}")
        g_rel[i] = _bias_grad(dbias, f"rel_bias_grad_{i}")
        big_grads["b_w_q", i] = _mm_dw_natural(f"b_q_dw_{i}", hb, dq)
        pending.extend([("b_w_o", i), ("b_w_q", i)])
        dx, g_b_norm[i] = _mm_t_natural_norm_bwd(f"b_q_dx_{i}", dq, w_q, 0, x_in, b_norm[i], dx)
        if i > 0:
            dx = start_exchange(dx, f"b{i}")

    dkv = jnp.concatenate([dk[LEFT:], dv[LEFT:]], axis=1).astype(BF16)
    big_grads["w_kv", 0] = _mm_dw_colblock("kv_proj_dw", h_kv, dkv)
    pending.append(("w_kv", 0))
    dx, g_kv_norm = _mm_t_colblock_norm_bwd("kv_proj_dx", dkv, w_kv_g, 0, x_kv, kv_norm, dx)
    dx = start_exchange(dx, "kv")

    for i in reversed(range(n_a)):
        x_in, h, zs, dgs, p, ffn_saved = saved[i]
        dx = ffn_bwd(dx, i, ffn_saved)
        big_grads["a_w_out", i] = _mm_dw_natural(f"a_out_dw_{i}", p, dx)
        pending.append(("a_w_out", i))
        dx = start_exchange(dx, f"a{i}_out")
        w_in, w_out = mixer_a_weights(i, (x_in, p))
        dp = _mm_t_natural(f"a_out_dx_{i}", dx, w_out, 0)
        dz, g_w_sp[i], g_b_sp[i], g_a_sgu[i] = _sgu_bwd(
            zs, dgs, dp, a_sgu_full[i], a_w_spatial[i], w_sp_t[i], b_full[i], f"a_sgu_bwd_{i}")
        big_grads["a_w_in", i] = _mm_dw_colblock(f"a_in_dw_{i}", h, dz)
        pending.append(("a_w_in", i))
        dx = start_exchange(dx, f"a{i}_in")
        dx, g_a_norm[i] = _mm_t_colblock_norm_bwd(f"a_in_dx_{i}", dz, w_in, 0, x_in, a_norm_full[i], dx)
    grad_x = dx[None]

    small_like = [jax.ShapeDtypeStruct((n_a, d), F32), jax.ShapeDtypeStruct((n_a, f_a), F32),
                  a_w_spatial, a_b_spatial, kv_norm, b_norm, b_rel_bias, ffn_norm, final_norm]
    small_partial = _pack(
        [jnp.stack(g_a_norm), jnp.stack(g_a_sgu), jnp.stack(g_w_sp), jnp.stack(g_b_sp), g_kv_norm,
         jnp.stack(g_b_norm), jnp.stack(g_rel), jnp.stack(g_ffn_norm), g_final], N_DEV * 8)
    chunk_rows = small_partial.shape[0] // N_DEV
    arrived = {}
    for keys, group, tag in in_flight_grads:
        srcs, lands = _split_wait(group, dx, f"exchange_wait_{tag}")
        for key, src, land in zip(keys, srcs, lands):
            arrived[key] = (land, src)
    small_got = _exchange([small_partial.reshape(1, N_DEV, chunk_rows, FLAT_LANES)], "exchange_small")[0]
    small_sum = _ordered_sum(small_got[0], "small_grad_sum")
    small_all = _all_gather([small_sum[None]], "gather_small_grads")[0]
    (ga_norm, ga_sgu, gw_sp, gb_sp, gkv_norm, gb_norm, g_relb, gffn_norm, gfinal) = _unpack(small_all, small_like)

    results = {}
    big_names = ["a_w_in", "a_w_out", "w_kv", "b_w_q", "b_w_o", "ffn_w_gate_up", "ffn_w_down"]
    big_wmv = [(a_w_in, m_a_w_in, v_a_w_in), (a_w_out, m_a_w_out, v_a_w_out),
               (w_kv[None], m_w_kv[None], v_w_kv[None]), (b_w_q, m_b_w_q, v_b_w_q), (b_w_o, m_b_w_o, v_b_w_o),
               tuple(jnp.swapaxes(a, 1, 2) for a in (ffn_w_gate_up, m_ffn_w_gate_up, v_ffn_w_gate_up)),
               (ffn_w_down, m_ffn_w_down, v_ffn_w_down)]
    me_arr = jnp.reshape(me, (1,)).astype(jnp.int32)
    for name, (w, m, v) in zip(big_names, big_wmv):
        outs = None
        for layer in range(w.shape[0]):
            got, own = arrived[name, layer]
            outs = _adamw_layer(got, own, w, m, v, layer, outs, me_arr, f"adamw_{name}_{layer}")
        if name == "w_kv":
            outs = [o[0] for o in outs]
        if name == "ffn_w_gate_up":
            outs = [jnp.swapaxes(o, 1, 2) for o in outs]
        results[name] = outs

    n_cols = a_norm.shape[1]
    s_cols = a_sgu_norm.shape[1]
    small_g_list = [lax.dynamic_slice(ga_norm, (0, me * n_cols), (n_a, n_cols)),
                    lax.dynamic_slice(ga_sgu, (0, me * s_cols), (n_a, s_cols)),
                    gw_sp, gb_sp, gkv_norm, gb_norm, g_relb, gffn_norm, gfinal]
    small_names = ["a_norm", "a_sgu_norm", "a_w_spatial", "a_b_spatial", "kv_norm", "b_norm", "b_rel_bias",
                   "ffn_norm", "final_norm"]
    small_w = [a_norm, a_sgu_norm, a_w_spatial, a_b_spatial, kv_norm, b_norm, b_rel_bias, ffn_norm, final_norm]
    small_m = [m_a_norm, m_a_sgu_norm, m_a_w_spatial, m_a_b_spatial, m_kv_norm, m_b_norm, m_b_rel_bias,
               m_ffn_norm, m_final_norm]
    small_v = [v_a_norm, v_a_sgu_norm, v_a_w_spatial, v_a_b_spatial, v_kv_norm, v_b_norm, v_b_rel_bias,
               v_ffn_norm, v_final_norm]
    flat_g = _pack(small_g_list, 8)
    flat_out = _adamw(flat_g[None, None], _pack(small_w, 8)[None], _pack(small_m, 8)[None],
                      _pack(small_v, 8)[None], "adamw_small")
    unpacked = [_unpack(o[0], small_w) for o in flat_out]
    for idx, name in enumerate(small_names):
        results[name] = [unpacked[kind][idx] for kind in range(4)]

    order = ["a_norm", "a_w_in", "a_sgu_norm", "a_w_spatial", "a_b_spatial", "a_w_out", "kv_norm", "w_kv",
             "b_norm", "b_w_q", "b_rel_bias", "b_w_o", "ffn_norm", "ffn_w_gate_up", "ffn_w_down", "final_norm"]
    outputs = [loss, grad_x]
    for kind in range(4):
        outputs += [results[name][kind] for name in order]
    return tuple(outputs)
```

```python
import math

import jax
import jax.numpy as jnp
from jax import lax
from jax.experimental import pallas as pl
from jax.experimental.pallas import tpu as pltpu

F32 = jnp.float32
BF16 = jnp.bfloat16
MESH = pl.DeviceIdType.MESH
HBM_SPEC = pl.BlockSpec(memory_space=pltpu.HBM)
SEM_SPEC = pl.BlockSpec(memory_space=pltpu.SEMAPHORE)

N_DEV = 8
CHUNK = 64
A_CHUNK = 128
A_GROUPS = 8
N_LEFT_CHUNKS = 8
LEFT = N_LEFT_CHUNKS * CHUNK
PAIR_ROWS = 2 * CHUNK
PAIR_BAND = PAIR_ROWS + LEFT
DIAGONALS = PAIR_BAND + PAIR_ROWS
PAIRS_PER_BLOCK = 2
Q_BLOCK = PAIRS_PER_BLOCK * PAIR_ROWS
K_BLOCK = Q_BLOCK + LEFT
ATTN_UNROLL = 7
MAX_REL = 256
N_REL = 2 * MAX_REL + 1
REL_PAD = 640
HEAD_DIM = 64
HEAD_PAIR = 2 * HEAD_DIM
ATTN_SCALE = HEAD_DIM ** -0.5
EPS = 1e-6
NEG_INF = -1e30
ADAM_LR = 0.001
ADAM_B1 = 0.9
ADAM_B2 = 0.999
ADAM_EPS = 1e-08
ADAM_WD = 0.01
ADAM_STEP = 10
FLAT_LANES = 1024
F32_SUBLANES = 8
BF16_SUBLANES = 16
ADAMW_BLOCK_ELEMS = 256 * 1024
V7X_VMEM_BYTES = 64 * 1024 * 1024
VMEM_FLOOR_BYTES = 32 * 1024 * 1024
VMEM_CEIL_BYTES = V7X_VMEM_BYTES - 8 * 1024 * 1024

NN = (((1,), (0,)), ((), ()))
NT = (((1,), (1,)), ((), ()))
TN = (((0,), (0,)), ((), ()))


def _tile(n, pref):
    return pref if n % pref == 0 else n


def _row_tile(n, pref, mult):
    best = None
    for t in range(mult, min(n, pref) + 1, mult):
        if n % t == 0:
            best = t
    return best if best is not None else n


def _nbytes(shape, dtype):
    n = 1
    for s in shape:
        if s is not None:
            n *= s
    return n * jnp.dtype(dtype).itemsize


def _call(body, name, grid, in_specs, out_specs, out_shape, scratch=(), vmem_bytes=0, aliases=None):
    limit = int(min(max(VMEM_FLOOR_BYTES, vmem_bytes * 5 // 4), VMEM_CEIL_BYTES))
    return pl.pallas_call(
        body,
        name=name,
        grid=grid,
        in_specs=in_specs,
        out_specs=out_specs,
        out_shape=out_shape,
        scratch_shapes=list(scratch),
        input_output_aliases=aliases or {},
        compiler_params=pltpu.CompilerParams(
            dimension_semantics=("arbitrary",) * len(grid), vmem_limit_bytes=limit),
    )


ERFC_P = 0.3275911 / math.sqrt(2.0)
ERFC_HALF_COEFFS = tuple(0.5 * a for a in (1.061405429, -1.453152027, 1.421413741, -0.284496736, 0.254829592))


def _gelu_and_grad(x):
    d = 1.0 + ERFC_P * jnp.abs(x)
    r = pl.reciprocal(d, approx=True)
    t = r * (2.0 - d * r)
    a5, a4, a3, a2, a1 = ERFC_HALF_COEFFS
    ex = jnp.exp(-0.5 * (x * x))
    tail = ((((a5 * t + a4) * t + a3) * t + a2) * t + a1) * t * ex
    cdf = jnp.where(x < 0, tail, 1.0 - tail)
    return x * cdf, cdf + x * ex * (1.0 / math.sqrt(2.0 * math.pi))


def _sigmoid(x):
    return 1.0 / (1.0 + jnp.exp(-x))


def _split3(x):
    hi = x.astype(BF16)
    r1 = x - hi.astype(F32)
    mid = r1.astype(BF16)
    lo = (r1 - mid.astype(F32)).astype(BF16)
    return hi, mid, lo


def _rms_fwd(x, g, name):
    t, d = x.shape
    tm = _tile(t, 512)

    def body(x_ref, g_ref, o_ref):
        xf = x_ref[...]
        r = lax.rsqrt(jnp.mean(xf * xf, axis=-1, keepdims=True) + EPS)
        o_ref[...] = (xf * r * g_ref[...]).astype(o_ref.dtype)

    return _call(
        body, name, (t // tm,),
        [pl.BlockSpec((tm, d), lambda i: (i, 0)), pl.BlockSpec((1, d), lambda i: (0, 0))],
        pl.BlockSpec((tm, d), lambda i: (i, 0)),
        jax.ShapeDtypeStruct((t, d), BF16),
        vmem_bytes=2 * (_nbytes((tm, d), F32) + _nbytes((tm, d), BF16)) + 4 * _nbytes((tm, d), F32),
    )(x, g.reshape(1, d))


def _mm(name, dims, a, b, *, grid, a_spec, b_spec, out_shape, out_spec, acc_shape,
        res=None, res_spec=None, scale=None):
    nk = grid[2]
    has_res = res is not None

    def body(*refs):
        refs = list(refs)
        a_ref = refs.pop(0)
        b_ref = refs.pop(0)
        r_ref = refs.pop(0) if has_res else None
        o_ref = refs.pop(0)
        part = lax.dot_general(a_ref[...].astype(BF16), b_ref[...].astype(BF16), dims,
                               preferred_element_type=F32)

        def finish(acc):
            if scale is not None:
                acc = acc * scale
            if has_res:
                acc = acc + r_ref[...]
            o_ref[...] = acc.astype(o_ref.dtype)

        if nk == 1:
            finish(part)
        else:
            acc_ref = refs.pop(0)
            k = pl.program_id(2)

            @pl.when(k == 0)
            def _():
                acc_ref[...] = part

            @pl.when(k > 0)
            def _():
                acc_ref[...] += part

            @pl.when(k == nk - 1)
            def _():
                finish(acc_ref[...])

    operands = [a, b]
    in_specs = [a_spec, b_spec]
    vmem = 2 * (_nbytes(a_spec.block_shape, a.dtype) + _nbytes(b_spec.block_shape, b.dtype)
                + _nbytes(out_spec.block_shape, out_shape.dtype))
    vmem += 3 * _nbytes(acc_shape, F32)
    if has_res:
        operands.append(res)
        in_specs.append(res_spec)
        vmem += 2 * _nbytes(res_spec.block_shape, res.dtype)
    scratch = [pltpu.VMEM(acc_shape, F32)] if nk > 1 else []
    return _call(body, name, grid, in_specs, out_spec, out_shape, scratch=scratch, vmem_bytes=vmem)(*operands)


def _mm_colblock(name, h, w_g, layer):
    t, k = h.shape
    nb = w_g.shape[3]
    tm = _tile(t, 2048)
    return _mm(
        name, NN, h, w_g, grid=(t // tm, N_DEV, 1),
        a_spec=pl.BlockSpec((tm, k), lambda i, j, kk: (i, 0)),
        b_spec=pl.BlockSpec((None, None, k, nb), lambda i, j, kk: (layer, j, 0, 0)),
        out_shape=jax.ShapeDtypeStruct((t, N_DEV * nb), BF16),
        out_spec=pl.BlockSpec((tm, nb), lambda i, j, kk: (i, j)), acc_shape=(tm, nb))


def _mm_natural(name, a, w, layer, *, res=None, out_dtype=F32, scale=None):
    t, k = a.shape
    n = w.shape[2]
    tm = _tile(t, 1024)
    tn = _tile(n, 1024 if k <= 1024 else 512)
    res_spec = None if res is None else pl.BlockSpec((tm, tn), lambda i, j, kk: (i, j))
    return _mm(
        name, NN, a, w, grid=(t // tm, n // tn, 1),
        a_spec=pl.BlockSpec((tm, k), lambda i, j, kk: (i, 0)),
        b_spec=pl.BlockSpec((None, k, tn), lambda i, j, kk: (layer, 0, j)),
        out_shape=jax.ShapeDtypeStruct((t, n), out_dtype),
        out_spec=pl.BlockSpec((tm, tn), lambda i, j, kk: (i, j)),
        acc_shape=(tm, tn), res=res, res_spec=res_spec, scale=scale)


def _mm_down(name, act, w4, layer, res):
    nblk, t, kb = act.shape
    n = w4.shape[3]
    tm = _tile(t, 1024)

    def body(a_ref, b_ref, r_ref, o_ref):
        acc = r_ref[...]
        for u in range(nblk):
            acc = acc + jnp.dot(a_ref[u], b_ref[u], preferred_element_type=F32)
        o_ref[...] = acc

    row = pl.BlockSpec((tm, n), lambda i: (i, 0))
    return _call(
        body, name, (t // tm,),
        [pl.BlockSpec((nblk, tm, kb), lambda i: (0, i, 0)),
         pl.BlockSpec((None, nblk, kb, n), lambda i: (layer, 0, 0, 0)),
         row],
        row,
        jax.ShapeDtypeStruct((t, n), F32),
        vmem_bytes=2 * (_nbytes((nblk, tm, kb), BF16) + _nbytes((nblk, kb, n), BF16)) + 6 * _nbytes((tm, n), F32),
    )(act, w4, res)


def _mm_t_colblock_norm_bwd(name, dz, w_g, layer, x, g, dx_up, blocked_in=False):
    k = w_g.shape[2]
    nb = w_g.shape[3]
    t = x.shape[0]
    tm = _tile(t, 512)
    if blocked_in:
        a_spec = pl.BlockSpec((N_DEV, tm, nb), lambda i: (0, i, 0))
    else:
        a_spec = pl.BlockSpec((tm, N_DEV * nb), lambda i: (i, 0))

    def body(a_ref, b_ref, x_ref, g_ref, up_ref, dx_ref, dg_ref):
        @pl.when(pl.program_id(0) == 0)
        def _():
            dg_ref[...] = jnp.zeros_like(dg_ref)

        dy = None
        for u in range(N_DEV):
            a = a_ref[u] if blocked_in else a_ref[:, u * nb:(u + 1) * nb]
            term = lax.dot_general(a.astype(BF16), b_ref[u].astype(BF16), NT, preferred_element_type=F32)
            dy = term if dy is None else dy + term
        xf = x_ref[...]
        r = lax.rsqrt(jnp.mean(xf * xf, axis=-1, keepdims=True) + EPS)
        xhat = xf * r
        dxhat = dy * g_ref[...]
        dg_ref[...] += jnp.sum(dy * xhat, axis=0, keepdims=True)
        dx_ref[...] = up_ref[...] + r * (dxhat - xhat * jnp.mean(dxhat * xhat, axis=-1, keepdims=True))

    row = pl.BlockSpec((tm, k), lambda i: (i, 0))
    vec = pl.BlockSpec((1, k), lambda i: (0, 0))
    dx, dg = _call(
        body, name, (t // tm,),
        [a_spec, pl.BlockSpec((None, N_DEV, k, nb), lambda i: (layer, 0, 0, 0)), row, vec, row],
        [row, vec],
        [jax.ShapeDtypeStruct((t, k), F32), jax.ShapeDtypeStruct((1, k), F32)],
        vmem_bytes=2 * N_DEV * (_nbytes((tm, nb), BF16) + _nbytes((k, nb), BF16)) + 10 * _nbytes((tm, k), F32),
    )(dz, w_g, x, g.reshape(1, k), dx_up)
    return dx, dg.reshape(k)


def _ffn_gate_up(name, h, w_g, layer):
    t, k = h.shape
    nb = w_g.shape[3]
    half = N_DEV // 2
    tm = _tile(t, 1024)

    def body(h_ref, wg_ref, wu_ref, dact_ref, act_ref):
        hb = h_ref[...]
        gate = jnp.dot(hb, wg_ref[...], preferred_element_type=F32)
        up = jnp.dot(hb, wu_ref[...], preferred_element_type=F32)
        sig = _sigmoid(gate)
        silu = gate * sig
        dact_ref[0] = (up * (sig * (1.0 + gate * (1.0 - sig)))).astype(BF16)
        dact_ref[1] = silu.astype(BF16)
        act_ref[...] = (silu * up).astype(BF16)

    return _call(
        body, name, (t // tm, half),
        [pl.BlockSpec((tm, k), lambda i, j: (i, 0)),
         pl.BlockSpec((None, None, k, nb), lambda i, j: (layer, j, 0, 0)),
         pl.BlockSpec((None, None, k, nb), lambda i, j: (layer, half + j, 0, 0))],
        [pl.BlockSpec((2, None, tm, nb), lambda i, j: (0, j, i, 0)),
         pl.BlockSpec((None, tm, nb), lambda i, j: (j, i, 0))],
        [jax.ShapeDtypeStruct((2, half, t, nb), BF16), jax.ShapeDtypeStruct((half, t, nb), BF16)],
        vmem_bytes=2 * (_nbytes((tm, k), BF16) + 2 * _nbytes((k, nb), BF16) + 3 * _nbytes((tm, nb), BF16))
        + 8 * _nbytes((tm, nb), F32),
    )(h, w_g, w_g)


def _ffn_down_dx(name, dy, w4, layer, dact):
    t, n = dy.shape
    nblk, kb = w4.shape[1], w4.shape[2]
    tm = _tile(t, 1024)

    def body(dy_ref, w_ref, dact_ref, dgu_ref):
        da = lax.dot_general(dy_ref[...].astype(BF16), w_ref[...], NT, preferred_element_type=F32)
        dgu_ref[0] = (da * dact_ref[0].astype(F32)).astype(BF16)
        dgu_ref[1] = (da * dact_ref[1].astype(F32)).astype(BF16)

    blk = pl.BlockSpec((2, None, tm, kb), lambda i, j: (0, j, i, 0))
    return _call(
        body, name, (t // tm, nblk),
        [pl.BlockSpec((tm, n), lambda i, j: (i, 0)),
         pl.BlockSpec((None, None, kb, n), lambda i, j: (layer, j, 0, 0)),
         blk],
        blk,
        jax.ShapeDtypeStruct((2, nblk, t, kb), BF16),
        vmem_bytes=2 * (_nbytes((tm, n), F32) + _nbytes((kb, n), BF16) + 4 * _nbytes((tm, kb), BF16))
        + 8 * _nbytes((tm, kb), F32),
    )(dy, w4, dact)


def _mm_t_natural(name, dy, w, layer):
    t, n = dy.shape
    k = w.shape[1]
    tm = _tile(t, 1024)
    tk = _tile(k, 1024)
    return _mm(
        name, NT, dy, w, grid=(t // tm, k // tk, 1),
        a_spec=pl.BlockSpec((tm, n), lambda i, j, kk: (i, 0)),
        b_spec=pl.BlockSpec((None, tk, n), lambda i, j, kk: (layer, j, 0)),
        out_shape=jax.ShapeDtypeStruct((t, k), BF16),
        out_spec=pl.BlockSpec((tm, tk), lambda i, j, kk: (i, j)),
        acc_shape=(tm, tk))


def _mm_t_natural_norm_bwd(name, dy, w, layer, x, g, dx_up):
    t, n = dy.shape
    k = w.shape[1]
    tm = _tile(t, 1024)

    def body(a_ref, b_ref, x_ref, g_ref, up_ref, dx_ref, dg_ref):
        @pl.when(pl.program_id(0) == 0)
        def _():
            dg_ref[...] = jnp.zeros_like(dg_ref)

        dh = lax.dot_general(a_ref[...].astype(BF16), b_ref[...], NT, preferred_element_type=F32)
        xf = x_ref[...]
        r = lax.rsqrt(jnp.mean(xf * xf, axis=-1, keepdims=True) + EPS)
        xhat = xf * r
        dxhat = dh * g_ref[...]
        dg_ref[...] += jnp.sum(dh * xhat, axis=0, keepdims=True)
        dx_ref[...] = up_ref[...] + r * (dxhat - xhat * jnp.mean(dxhat * xhat, axis=-1, keepdims=True))

    row = pl.BlockSpec((tm, k), lambda i: (i, 0))
    vec = pl.BlockSpec((1, k), lambda i: (0, 0))
    dx, dg = _call(
        body, name, (t // tm,),
        [pl.BlockSpec((tm, n), lambda i: (i, 0)), pl.BlockSpec((None, k, n), lambda i: (layer, 0, 0)), row, vec, row],
        [row, vec],
        [jax.ShapeDtypeStruct((t, k), F32), jax.ShapeDtypeStruct((1, k), F32)],
        vmem_bytes=2 * (_nbytes((tm, n), dy.dtype) + _nbytes((k, n), BF16)) + 10 * _nbytes((tm, k), F32),
    )(dy, w, x, g.reshape(1, k), dx_up)
    return dx, dg.reshape(k)


def _mm_dw_colblock(name, h, dz, blocked_in=False, transposed=False):
    t, k = h.shape
    nb = dz.shape[2] if blocked_in else dz.shape[1] // N_DEV
    tk = _tile(t, 4096)
    h_spec = pl.BlockSpec((tk, k), lambda i, j, kk: (kk, 0))
    if blocked_in:
        dz_spec = pl.BlockSpec((None, tk, nb), lambda i, j, kk: (j, kk, 0))
    else:
        dz_spec = pl.BlockSpec((tk, nb), lambda i, j, kk: (kk, j))
    rows, cols = (nb, k) if transposed else (k, nb)
    return _mm(
        name, TN, *((dz, h) if transposed else (h, dz)), grid=(1, N_DEV, t // tk),
        a_spec=dz_spec if transposed else h_spec,
        b_spec=h_spec if transposed else dz_spec,
        out_shape=jax.ShapeDtypeStruct((N_DEV, rows, cols), BF16),
        out_spec=pl.BlockSpec((None, rows, cols), lambda i, j, kk: (j, 0, 0)),
        acc_shape=(rows, cols))


def _mm_dw_natural(name, a, dy):
    t, k = a.shape
    n = dy.shape[1]
    tko = _tile(k, 1024)
    tt = _tile(t, 2048)
    out = _mm(
        name, TN, a, dy, grid=(k // tko, 1, t // tt),
        a_spec=pl.BlockSpec((tt, tko), lambda i, j, kk: (kk, i)),
        b_spec=pl.BlockSpec((tt, n), lambda i, j, kk: (kk, 0)),
        out_shape=jax.ShapeDtypeStruct((k, n), BF16),
        out_spec=pl.BlockSpec((tko, n), lambda i, j, kk: (i, 0)),
        acc_shape=(tko, n))
    return out.reshape(N_DEV, k // N_DEV, n)


def _mm_dw_down(name, act, dy):
    nblk, t, kb = act.shape
    n = dy.shape[1]
    tt = _tile(t, 2048)
    out = _mm(
        name, TN, act, dy, grid=(nblk, 1, t // tt),
        a_spec=pl.BlockSpec((None, tt, kb), lambda i, j, kk: (i, kk, 0)),
        b_spec=pl.BlockSpec((tt, n), lambda i, j, kk: (kk, 0)),
        out_shape=jax.ShapeDtypeStruct((nblk, kb, n), BF16),
        out_spec=pl.BlockSpec((None, kb, n), lambda i, j, kk: (i, 0, 0)),
        acc_shape=(kb, n))
    return out.reshape(N_DEV, (nblk * kb) // N_DEV, n)


def _spatial_mask(transposed=False):
    r = lax.broadcasted_iota(jnp.int32, (A_CHUNK, A_CHUNK), 0) // CHUNK
    c = lax.broadcasted_iota(jnp.int32, (A_CHUNK, A_CHUNK), 1) // CHUNK
    return c >= r if transposed else r >= c


def _sgu_tile(t):
    return _tile(t, 4 * A_CHUNK)


def _sgu_fwd(zpre, g_sgu, w_sp, b_full, name):
    t, f2 = zpre.shape
    f = f2 // 2
    gd = f // A_GROUPS
    tm = _sgu_tile(t)

    def body(z_ref, g_ref, w_ref, b_ref, p_ref, zs_ref, dg_ref):
        mask = _spatial_mask()
        wm = [jnp.where(mask, w_ref[g], 0.0).astype(BF16) for g in range(A_GROUPS)]
        for c in range(tm // A_CHUNK):
            rows = pl.ds(c * A_CHUNK, A_CHUNK)
            z, dgelu = _gelu_and_grad(z_ref[rows, :].astype(F32))
            zs_ref[rows, :] = z.astype(BF16)
            dg_ref[rows, :] = dgelu.astype(BF16)
            u = z[:, :f]
            v0 = z[:, f:]
            r = lax.rsqrt(jnp.mean(v0 * v0, axis=-1, keepdims=True) + EPS)
            v1 = (v0 * r * g_ref[...]).astype(BF16)
            for g in range(A_GROUPS):
                cols = slice(g * gd, (g + 1) * gd)
                v2 = jnp.dot(wm[g], v1[:, cols], preferred_element_type=F32) + b_ref[:, cols]
                p_ref[rows, cols] = (u[:, cols] * v2).astype(BF16)

    return _call(
        body, name, (t // tm,),
        [pl.BlockSpec((tm, f2), lambda i: (i, 0)),
         pl.BlockSpec((1, f), lambda i: (0, 0)),
         pl.BlockSpec((A_GROUPS, A_CHUNK, A_CHUNK), lambda i: (0, 0, 0)),
         pl.BlockSpec((A_CHUNK, f), lambda i: (0, 0))],
        [pl.BlockSpec((tm, f), lambda i: (i, 0)), pl.BlockSpec((tm, f2), lambda i: (i, 0)),
         pl.BlockSpec((tm, f2), lambda i: (i, 0))],
        [jax.ShapeDtypeStruct((t, f), BF16), jax.ShapeDtypeStruct((t, f2), BF16), jax.ShapeDtypeStruct((t, f2), BF16)],
        vmem_bytes=6 * _nbytes((tm, f2), BF16) + 2 * _nbytes((tm, f), BF16) + 8 * _nbytes((A_CHUNK, f2), F32),
    )(zpre, g_sgu.reshape(1, f), w_sp, b_full)


def _sgu_bwd(zs, dgs, dp, g_sgu, w_sp, w_sp_t, b_full, name):
    t, f2 = zs.shape
    f = f2 // 2
    gd = f // A_GROUPS
    tm = _sgu_tile(t)
    n_steps = t // tm

    def body(z_ref, dgelu_ref, dp_ref, g_ref, w_ref, wt_ref, b_ref, dz_ref, dw_ref, db_ref, dg_ref, dv1_ref, dbf_ref):
        step = pl.program_id(0)

        @pl.when(step == 0)
        def _():
            dw_ref[...] = jnp.zeros_like(dw_ref)
            dg_ref[...] = jnp.zeros_like(dg_ref)
            dbf_ref[...] = jnp.zeros_like(dbf_ref)

        mask = _spatial_mask()
        mask_t = _spatial_mask(transposed=True)
        wm = [jnp.where(mask, w_ref[g], 0.0).astype(BF16) for g in range(A_GROUPS)]
        wmt = [jnp.where(mask_t, wt_ref[g], 0.0).astype(BF16) for g in range(A_GROUPS)]
        gain = g_ref[...]
        for c in range(tm // A_CHUNK):
            rows = pl.ds(c * A_CHUNK, A_CHUNK)
            z = z_ref[rows, :].astype(F32)
            dgelu = dgelu_ref[rows, :].astype(F32)
            u = z[:, :f]
            v0 = z[:, f:]
            r = lax.rsqrt(jnp.mean(v0 * v0, axis=-1, keepdims=True) + EPS)
            xhat = v0 * r
            v1 = (xhat * gain).astype(BF16)
            dpf = dp_ref[rows, :].astype(F32)
            for g in range(A_GROUPS):
                cols = slice(g * gd, (g + 1) * gd)
                v1g = v1[:, cols]
                v2 = jnp.dot(wm[g], v1g, preferred_element_type=F32) + b_ref[:, cols]
                dpg = dpf[:, cols]
                dz_ref[rows, cols] = (dpg * v2 * dgelu[:, cols]).astype(BF16)
                dv2 = dpg * u[:, cols]
                dbf_ref[:, cols] += dv2
                dv2b = dv2.astype(BF16)
                dwg = lax.dot_general(dv2b, v1g, NT, preferred_element_type=F32)
                dw_ref[g] += jnp.where(mask, dwg, 0.0)
                dv1_ref[:, cols] = jnp.dot(wmt[g], dv2b, preferred_element_type=F32)
            dv1 = dv1_ref[...]
            dxhat = dv1 * gain
            dg_ref[...] += jnp.sum(dv1 * xhat, axis=0, keepdims=True)
            dv0 = r * (dxhat - xhat * jnp.mean(dxhat * xhat, axis=-1, keepdims=True))
            dz_ref[rows, pl.ds(f, f)] = (dv0 * dgelu[:, f:]).astype(BF16)

        @pl.when(step == n_steps - 1)
        def _():
            for g in range(A_GROUPS):
                db_ref[g] = jnp.sum(dbf_ref[:, g * gd:(g + 1) * gd], axis=1, keepdims=True)

    wspec = pl.BlockSpec((A_GROUPS, A_CHUNK, A_CHUNK), lambda i: (0, 0, 0))
    dz, dw, db, dg = _call(
        body, name, (n_steps,),
        [pl.BlockSpec((tm, f2), lambda i: (i, 0)),
         pl.BlockSpec((tm, f2), lambda i: (i, 0)),
         pl.BlockSpec((tm, f), lambda i: (i, 0)),
         pl.BlockSpec((1, f), lambda i: (0, 0)),
         wspec, wspec,
         pl.BlockSpec((A_CHUNK, f), lambda i: (0, 0))],
        [pl.BlockSpec((tm, f2), lambda i: (i, 0)),
         wspec,
         pl.BlockSpec((A_GROUPS, A_CHUNK, 1), lambda i: (0, 0, 0)),
         pl.BlockSpec((1, f), lambda i: (0, 0))],
        [jax.ShapeDtypeStruct((t, f2), BF16),
         jax.ShapeDtypeStruct((A_GROUPS, A_CHUNK, A_CHUNK), F32),
         jax.ShapeDtypeStruct((A_GROUPS, A_CHUNK, 1), F32),
         jax.ShapeDtypeStruct((1, f), F32)],
        scratch=[pltpu.VMEM((A_CHUNK, f), F32), pltpu.VMEM((A_CHUNK, f), F32)],
        vmem_bytes=6 * _nbytes((tm, f2), BF16) + 2 * _nbytes((tm, f), BF16) + 12 * _nbytes((A_CHUNK, f2), F32),
    )(zs, dgs, dp, g_sgu.reshape(1, f), w_sp, w_sp_t, b_full)
    return dz, dw, db.reshape(A_GROUPS, A_CHUNK), dg.reshape(f)


def _pair_valid(qi, col):
    qc = qi // CHUNK
    kc = col // CHUNK
    return (kc >= qc) & (kc <= qc + N_LEFT_CHUNKS)


def _diagonal_onehot():
    e = lax.broadcasted_iota(jnp.int32, (REL_PAD, DIAGONALS), 1)
    idx = jnp.clip(PAIR_BAND - 1 - e, -MAX_REL, MAX_REL) + MAX_REL
    r = lax.broadcasted_iota(jnp.int32, (REL_PAD, DIAGONALS), 0)
    return jnp.where(r == idx, 1.0, 0.0).astype(BF16)


def _bias_build(table, name):
    h = table.shape[0]
    tab = jnp.pad(table, ((0, 0), (0, REL_PAD - N_REL)))

    def body(t_ref, o_ref):
        oh = _diagonal_onehot()
        diag = jnp.zeros((h, DIAGONALS), F32)
        for piece in _split3(t_ref[...]):
            diag += jnp.dot(piece, oh, preferred_element_type=F32)
        col = lax.broadcasted_iota(jnp.int32, (h, PAIR_BAND), 1)
        for qi in range(PAIR_ROWS):
            row = pltpu.roll(diag, (qi - (PAIR_ROWS - 1)) % DIAGONALS, 1)[:, :PAIR_BAND]
            o_ref[qi] = jnp.where(_pair_valid(qi, col), row, NEG_INF)

    out = _call(
        body, name, (1,),
        [pl.BlockSpec((h, REL_PAD), lambda i: (0, 0))],
        pl.BlockSpec((PAIR_ROWS, h, PAIR_BAND), lambda i: (0, 0, 0)),
        jax.ShapeDtypeStruct((PAIR_ROWS, h, PAIR_BAND), F32),
        vmem_bytes=4 * _nbytes((PAIR_ROWS, h, PAIR_BAND), F32),
    )(tab)
    return jnp.transpose(out, (1, 0, 2))


def _bias_block(pair_bias):
    rest = K_BLOCK - PAIR_BAND
    return jnp.concatenate(
        [jnp.pad(pair_bias, ((0, 0), (0, 0), (p * PAIR_ROWS, rest - p * PAIR_ROWS)), constant_values=NEG_INF)
         for p in range(PAIRS_PER_BLOCK)], axis=1)


def _bias_grad(dbias, name):
    h = dbias.shape[0]
    db_t = jnp.transpose(dbias, (1, 0, 2))

    def body(d_ref, o_ref):
        diag = jnp.zeros((h, DIAGONALS), F32)
        for qi in range(PAIR_ROWS):
            diag += pltpu.roll(d_ref[qi], PAIR_ROWS - 1 - qi, 1)
        oh = _diagonal_onehot()
        acc = jnp.zeros((h, REL_PAD), F32)
        for piece in _split3(diag):
            acc += lax.dot_general(piece, oh, NT, preferred_element_type=F32)
        o_ref[...] = acc

    out = _call(
        body, name, (1,),
        [pl.BlockSpec((PAIR_ROWS, h, DIAGONALS), lambda i: (0, 0, 0))],
        pl.BlockSpec((h, REL_PAD), lambda i: (0, 0)),
        jax.ShapeDtypeStruct((h, REL_PAD), F32),
        vmem_bytes=4 * _nbytes((PAIR_ROWS, h, DIAGONALS), F32),
    )(db_t)
    return out[:, :N_REL]


def _head_masks():
    lane = lax.broadcasted_iota(jnp.int32, (Q_BLOCK, HEAD_PAIR), 1)
    return lane < HEAD_DIM, lane >= HEAD_DIM


def _block_scores(qm, kb, bias, valid):
    s = lax.dot_general(qm, kb, NT, preferred_element_type=F32) + bias
    return s if valid is None else jnp.where(valid, s, NEG_INF)


def _softmax_rows(s):
    e = jnp.exp(s - jnp.max(s, axis=-1, keepdims=True))
    return e * (1.0 / jnp.sum(e, axis=-1, keepdims=True))


def _padded_then_plain(step, n_blocks):
    n_padded = min(LEFT // Q_BLOCK, n_blocks)
    lax.fori_loop(0, n_padded, lambda j, c: step(j, c, True), 0, unroll=True)
    lax.fori_loop(n_padded, n_blocks, lambda j, c: step(j, c, False), 0, unroll=ATTN_UNROLL)


def _attn_fwd(q, kvpad, bias, name):
    t, d = q.shape
    n_pairs = d // HEAD_PAIR
    n_blocks = t // Q_BLOCK

    def body(q_ref, k_ref, v_ref, b_ref, o_ref):
        masks = _head_masks()
        key = lax.broadcasted_iota(jnp.int32, (Q_BLOCK, K_BLOCK), 1)

        def step(j, carry, padded):
            r0 = pl.multiple_of(j * Q_BLOCK, Q_BLOCK)
            q2 = q_ref[pl.ds(r0, Q_BLOCK), :].astype(F32)
            kb = k_ref[pl.ds(r0, K_BLOCK), :]
            vb = v_ref[pl.ds(r0, K_BLOCK), :]
            valid = key >= LEFT - j * Q_BLOCK if padded else None
            scores = [_block_scores(jnp.where(masks[a], q2, 0.0).astype(BF16), kb, b_ref[a], valid) for a in range(2)]
            probs = [_softmax_rows(s).astype(BF16) for s in scores]
            outs = [jnp.dot(p, vb, preferred_element_type=F32) for p in probs]
            o_ref[pl.ds(r0, Q_BLOCK), :] = jnp.where(masks[0], outs[0], outs[1]).astype(BF16)
            return carry

        _padded_then_plain(step, n_blocks)

    return _call(
        body, name, (n_pairs,),
        [pl.BlockSpec((t, HEAD_PAIR), lambda p: (0, p)),
         pl.BlockSpec((LEFT + t, HEAD_PAIR), lambda p: (0, p)),
         pl.BlockSpec((LEFT + t, HEAD_PAIR), lambda p: (0, n_pairs + p)),
         pl.BlockSpec((2, Q_BLOCK, K_BLOCK), lambda p: (p, 0, 0))],
        pl.BlockSpec((t, HEAD_PAIR), lambda p: (0, p)),
        jax.ShapeDtypeStruct((t, d), BF16),
        vmem_bytes=8 * _nbytes((LEFT + t, HEAD_PAIR), BF16) + 12 * _nbytes((2, Q_BLOCK, K_BLOCK), F32),
    )(q, kvpad, kvpad, bias)


def _attn_bwd(q, kvpad, bias, o, do, dk_in, dv_in, name):
    t, d = q.shape
    n_pairs = d // HEAD_PAIR
    n_blocks = t // Q_BLOCK
    has_in = dk_in is not None

    def body(*refs):
        refs = list(refs)
        q_ref, k_ref, v_ref, b_ref, o_ref, do_ref = refs[:6]
        refs = refs[6:]
        if has_in:
            dki_ref, dvi_ref = refs[:2]
            refs = refs[2:]
        dq_ref, dk_ref, dv_ref, db_ref = refs
        masks = _head_masks()
        key = lax.broadcasted_iota(jnp.int32, (Q_BLOCK, K_BLOCK), 1)
        if has_in:
            dk_ref[...] = dki_ref[...]
            dv_ref[...] = dvi_ref[...]
        else:
            dk_ref[...] = jnp.zeros_like(dk_ref)
            dv_ref[...] = jnp.zeros_like(dv_ref)
        db_ref[...] = jnp.zeros_like(db_ref)

        def step(j, carry, padded):
            r0 = pl.multiple_of(j * Q_BLOCK, Q_BLOCK)
            q2 = q_ref[pl.ds(r0, Q_BLOCK), :].astype(F32)
            do2 = do_ref[pl.ds(r0, Q_BLOCK), :].astype(F32)
            do_o = do2 * o_ref[pl.ds(r0, Q_BLOCK), :].astype(F32)
            kb = k_ref[pl.ds(r0, K_BLOCK), :]
            vb = v_ref[pl.ds(r0, K_BLOCK), :]
            valid = key >= LEFT - j * Q_BLOCK if padded else None
            heads = range(2)
            qms = [jnp.where(masks[a], q2, 0.0).astype(BF16) for a in heads]
            doms = [jnp.where(masks[a], do2, 0.0).astype(BF16) for a in heads]
            scores = [_block_scores(qms[a], kb, b_ref[a], valid) for a in heads]
            dps = [lax.dot_general(doms[a], vb, NT, preferred_element_type=F32) for a in heads]
            ps = [_softmax_rows(s) for s in scores]
            rows = [jnp.sum(jnp.where(masks[a], do_o, 0.0), axis=-1, keepdims=True) for a in heads]
            dss = [ps[a] * (dps[a] - rows[a]) for a in heads]
            for a in heads:
                for pair in range(PAIRS_PER_BLOCK):
                    lo = pair * PAIR_ROWS
                    db_ref[a, :, pl.ds(0, PAIR_BAND)] += dss[a][lo:lo + PAIR_ROWS, lo:lo + PAIR_BAND]
            dsbs = [ds.astype(BF16) for ds in dss]
            pbs = [p.astype(BF16) for p in ps]
            dqs = [jnp.dot(dsbs[a], kb, preferred_element_type=F32) for a in heads]
            dk_acc = sum(lax.dot_general(dsbs[a], qms[a], TN, preferred_element_type=F32) for a in heads)
            dv_acc = sum(lax.dot_general(pbs[a], doms[a], TN, preferred_element_type=F32) for a in heads)
            dq = jnp.where(masks[0], dqs[0], dqs[1]) * ATTN_SCALE
            dq_ref[pl.ds(r0, Q_BLOCK), :] = dq.astype(BF16)
            dk_ref[pl.ds(r0, K_BLOCK), :] += dk_acc
            dv_ref[pl.ds(r0, K_BLOCK), :] += dv_acc
            return carry

        _padded_then_plain(step, n_blocks)

    q_spec = pl.BlockSpec((t, HEAD_PAIR), lambda p: (0, p))
    kv_spec = pl.BlockSpec((LEFT + t, HEAD_PAIR), lambda p: (0, p))
    operands = [q, kvpad, kvpad, bias, o, do]
    in_specs = [q_spec, kv_spec, pl.BlockSpec((LEFT + t, HEAD_PAIR), lambda p: (0, n_pairs + p)),
                pl.BlockSpec((2, Q_BLOCK, K_BLOCK), lambda p: (p, 0, 0)), q_spec, q_spec]
    aliases = None
    if has_in:
        operands += [dk_in, dv_in]
        in_specs += [kv_spec, kv_spec]
        aliases = {6: 1, 7: 2}
    return _call(
        body, name, (n_pairs,),
        in_specs,
        [q_spec, kv_spec, kv_spec, pl.BlockSpec((2, PAIR_ROWS, DIAGONALS), lambda p: (p, 0, 0))],
        [jax.ShapeDtypeStruct((t, d), BF16),
         jax.ShapeDtypeStruct((LEFT + t, d), F32),
         jax.ShapeDtypeStruct((LEFT + t, d), F32),
         jax.ShapeDtypeStruct((d // HEAD_DIM, PAIR_ROWS, DIAGONALS), F32)],
        vmem_bytes=10 * _nbytes((LEFT + t, HEAD_PAIR), BF16) + 8 * _nbytes((LEFT + t, HEAD_PAIR), F32)
        + 16 * _nbytes((2, Q_BLOCK, K_BLOCK), F32),
        aliases=aliases,
    )(*operands)


def _loss_head(x, g, target, name):
    t, d = x.shape
    tm = _tile(t, 512)

    def body(x_ref, g_ref, t_ref, dx_ref, loss_ref, dg_ref):
        @pl.when(pl.program_id(0) == 0)
        def _():
            loss_ref[...] = jnp.zeros_like(loss_ref)
            dg_ref[...] = jnp.zeros_like(dg_ref)

        xf = x_ref[...]
        r = lax.rsqrt(jnp.mean(xf * xf, axis=-1, keepdims=True) + EPS)
        xhat = xf * r
        diff = xhat * g_ref[...] - t_ref[...]
        row_loss = jnp.mean(diff * diff, axis=-1, keepdims=True)
        loss_ref[...] += 0.5 * jnp.sum(row_loss, axis=0, keepdims=True)
        dy = diff * (1.0 / d)
        dg_ref[...] += jnp.sum(dy * xhat, axis=0, keepdims=True)
        dxhat = dy * g_ref[...]
        dx_ref[...] = r * (dxhat - xhat * jnp.mean(dxhat * xhat, axis=-1, keepdims=True))

    row = pl.BlockSpec((tm, d), lambda i: (i, 0))
    vec = pl.BlockSpec((1, d), lambda i: (0, 0))
    dx, loss, dg = _call(
        body, name, (t // tm,),
        [row, vec, row],
        [row, pl.BlockSpec((1, 1), lambda i: (0, 0)), vec],
        [jax.ShapeDtypeStruct((t, d), F32), jax.ShapeDtypeStruct((1, 1), F32), jax.ShapeDtypeStruct((1, d), F32)],
        vmem_bytes=10 * _nbytes((tm, d), F32),
    )(x, g.reshape(1, d), target)
    return dx, loss[0, 0], dg.reshape(d)


def _adamw_store(g, w_ref, m_ref, v_ref, g_ref, d_ref, nm_ref, nv_ref):
    c1 = 1.0 / (1.0 - ADAM_B1 ** ADAM_STEP)
    c2 = 1.0 / (1.0 - ADAM_B2 ** ADAM_STEP)
    nm = ADAM_B1 * m_ref[...] + (1.0 - ADAM_B1) * g
    nv = ADAM_B2 * v_ref[...] + (1.0 - ADAM_B2) * (g * g)
    g_ref[...] = g
    nm_ref[...] = nm
    nv_ref[...] = nv
    d_ref[...] = -ADAM_LR * ((nm * c1) / (jnp.sqrt(nv * c2) + ADAM_EPS) + ADAM_WD * w_ref[...])


def _adamw_layer(recv, own, w, m, v, layer, prev, me, name):
    n_src, r, c = recv.shape
    tr = _row_tile(r, max(BF16_SUBLANES, ADAMW_BLOCK_ELEMS // c), BF16_SUBLANES)
    first = prev is None

    def body(me_ref, recv_ref, own_ref, w_ref, m_ref, v_ref, *rest):
        mine = me_ref[0]
        own_part = own_ref[...].astype(F32)
        g = None
        for s in range(n_src):
            part = jnp.where(mine == s, own_part, recv_ref[s].astype(F32))
            g = part if g is None else g + part
        _adamw_store(g, w_ref, m_ref, v_ref, *rest[-4:])

    blk = pl.BlockSpec((None, tr, c), lambda i, me_ref: (layer, i, 0))
    any_spec = pl.BlockSpec(memory_space=pl.ANY)
    out = jax.ShapeDtypeStruct(w.shape, F32)
    operands = [me, recv, own, w, m, v] + ([] if first else list(prev))
    vmem = 2 * _nbytes((n_src + 1, tr, c), BF16) + 18 * _nbytes((tr, c), F32)
    return pl.pallas_call(
        body,
        name=name,
        grid_spec=pltpu.PrefetchScalarGridSpec(
            num_scalar_prefetch=1,
            grid=(r // tr,),
            in_specs=[pl.BlockSpec((n_src, tr, c), lambda i, me_ref: (0, i, 0)),
                      pl.BlockSpec((None, tr, c), lambda i, me_ref: (me_ref[0], i, 0)),
                      blk, blk, blk] + ([] if first else [any_spec] * 4),
            out_specs=[blk, blk, blk, blk],
        ),
        out_shape=[out, out, out, out],
        input_output_aliases={} if first else {6 + j: j for j in range(4)},
        compiler_params=pltpu.CompilerParams(
            dimension_semantics=("arbitrary",),
            vmem_limit_bytes=int(min(max(VMEM_FLOOR_BYTES, vmem * 5 // 4), VMEM_CEIL_BYTES))),
    )(*operands)


def _adamw(parts, w, m, v, name):
    n_layers, n_src, r, c = parts.shape
    mult = BF16_SUBLANES if parts.dtype == BF16 else F32_SUBLANES
    tr = _row_tile(r, max(mult, ADAMW_BLOCK_ELEMS // c), mult)

    def body(p_ref, w_ref, m_ref, v_ref, g_ref, d_ref, nm_ref, nv_ref):
        g = p_ref[0].astype(F32)
        for s in range(1, n_src):
            g = g + p_ref[s].astype(F32)
        _adamw_store(g, w_ref, m_ref, v_ref, g_ref, d_ref, nm_ref, nv_ref)

    blk = pl.BlockSpec((None, tr, c), lambda l, i: (l, i, 0))
    out = jax.ShapeDtypeStruct((n_layers, r, c), F32)
    return _call(
        body, name, (n_layers, r // tr),
        [pl.BlockSpec((None, n_src, tr, c), lambda l, i: (l, 0, i, 0)), blk, blk, blk],
        [blk, blk, blk, blk],
        [out, out, out, out],
        vmem_bytes=2 * _nbytes((n_src, tr, c), parts.dtype) + 18 * _nbytes((tr, c), F32),
    )(parts, w, m, v)


def _ordered_sum(parts, name):
    n_src, r, c = parts.shape

    def body(p_ref, o_ref):
        acc = p_ref[0]
        for s in range(1, n_src):
            acc = acc + p_ref[s]
        o_ref[...] = acc

    return _call(
        body, name, (1,),
        [pl.BlockSpec((n_src, r, c), lambda i: (0, 0, 0))],
        pl.BlockSpec((r, c), lambda i: (0, 0)),
        jax.ShapeDtypeStruct((r, c), F32),
        vmem_bytes=4 * _nbytes((n_src, r, c), F32),
    )(parts)


def _position():
    return lax.axis_index("x"), lax.axis_index("y"), lax.axis_index("c")


def _linear(p):
    return 4 * p[0] + 2 * p[1] + p[2]


def _all_gather(shards, name):
    n = len(shards)

    def body(*refs):
        ins, outs = refs[:n], refs[n:2 * n]
        send_sems, recv_sems, local_sems = refs[2 * n:]
        x, y, c = _position()
        me, sibling = (x, y, c), (x, y, 1 - c)
        chips = [(1 - x, y), (x, 1 - y), (1 - x, 1 - y)]

        def slab(t, p):
            return outs[t].at[:, _linear(p)]

        def copy(t, k, block, to, src=None):
            return pltpu.make_async_remote_copy(
                src_ref=slab(t, block) if src is None else src,
                dst_ref=slab(t, block),
                send_sem=send_sems.at[t, k],
                recv_sem=recv_sems.at[t, k],
                device_id=to,
                device_id_type=MESH,
            )

        started = []
        for t in range(n):
            mine = pltpu.make_async_copy(ins[t], slab(t, me), local_sems.at[t])
            mine.start()
            started.append(mine)
        sends = []
        for t in range(n):
            first = [copy(t, 0, me, sibling, src=ins[t])]
            first += [copy(t, 1 + j, me, (*chip, c), src=ins[t]) for j, chip in enumerate(chips)]
            for cp in first:
                cp.start()
            sends += first
        for t in range(n):
            for j, chip in enumerate(chips):
                copy(t, 1 + j, (*chip, c), me).wait_recv()
                passed = copy(t, 4 + j, (*chip, c), sibling)
                passed.start()
                sends.append(passed)
        for t in range(n):
            copy(t, 0, sibling, me).wait_recv()
            for j, chip in enumerate(chips):
                copy(t, 4 + j, (*chip, 1 - c), me).wait_recv()
        for cp in sends:
            cp.wait_send()
        for mine in started:
            mine.wait()

    out_shape = [jax.ShapeDtypeStruct((s.shape[0], N_DEV) + s.shape[1:], s.dtype) for s in shards]
    return pl.pallas_call(
        body,
        name=name,
        in_specs=[HBM_SPEC] * n,
        out_specs=[HBM_SPEC] * n,
        out_shape=out_shape,
        scratch_shapes=[
            pltpu.SemaphoreType.DMA((n, N_DEV - 1)),
            pltpu.SemaphoreType.DMA((n, N_DEV - 1)),
            pltpu.SemaphoreType.DMA((n,)),
        ],
    )(*shards)


def _exchange(blocks, name):
    n = len(blocks)

    def body(*refs):
        ins, outs = refs[:n], refs[n:2 * n]
        send_sems, recv_sems, local_sems = refs[2 * n:]
        x, y, c = _position()
        me = _linear((x, y, c))
        flips = [(fx, fy, fc) for fx in (0, 1) for fy in (0, 1) for fc in (0, 1)][1:]

        def peer_of(flip):
            fx, fy, fc = flip
            return (1 - x if fx else x, 1 - y if fy else y, 1 - c if fc else c)

        def copy(t, k, peer):
            return pltpu.make_async_remote_copy(
                src_ref=ins[t].at[:, _linear(peer)],
                dst_ref=outs[t].at[:, me],
                send_sem=send_sems.at[t, k],
                recv_sem=recv_sems.at[t, k],
                device_id=peer,
                device_id_type=MESH,
            )

        def arrival(t, k, peer):
            return pltpu.make_async_remote_copy(
                src_ref=ins[t].at[:, _linear(peer)],
                dst_ref=outs[t].at[:, _linear(peer)],
                send_sem=send_sems.at[t, k],
                recv_sem=recv_sems.at[t, k],
                device_id=peer,
                device_id_type=MESH,
            )

        own = []
        for t in range(n):
            cp = pltpu.make_async_copy(ins[t].at[:, me], outs[t].at[:, me], local_sems.at[t])
            cp.start()
            own.append(cp)
        sends = []
        for t in range(n):
            for k, flip in enumerate(flips):
                cp = copy(t, k, peer_of(flip))
                cp.start()
                sends.append(cp)
        for t in range(n):
            for k, flip in enumerate(flips):
                arrival(t, k, peer_of(flip)).wait_recv()
        for cp in sends:
            cp.wait_send()
        for cp in own:
            cp.wait()

    out_shape = [jax.ShapeDtypeStruct(b.shape, b.dtype) for b in blocks]
    return pl.pallas_call(
        body,
        name=name,
        in_specs=[HBM_SPEC] * n,
        out_specs=[HBM_SPEC] * n,
        out_shape=out_shape,
        scratch_shapes=[
            pltpu.SemaphoreType.DMA((n, N_DEV - 1)),
            pltpu.SemaphoreType.DMA((n, N_DEV - 1)),
            pltpu.SemaphoreType.DMA((n,)),
        ],
    )(*blocks)


def _peers():
    x, y, c = _position()
    flips = [(fx, fy, fc) for fx in (0, 1) for fy in (0, 1) for fc in (0, 1)][1:]
    return [(1 - x if fx else x, 1 - y if fy else y, 1 - c if fc else c) for fx, fy, fc in flips]


SIBLING, OTHER_CHIPS = (0,), (1, 3, 5)
COPY_PEERS = {"gather": tuple(range(N_DEV - 1)), "exchange": tuple(range(N_DEV - 1)),
              "chips": SIBLING + OTHER_CHIPS, "forward": OTHER_CHIPS}


def _split_copy(kind, src_ref, land_ref, k, send_sem, recv_sem, starting):
    peers = _peers()
    peer = peers[SIBLING[0]] if kind == "forward" else peers[k]
    me = _linear(_position())
    if kind == "forward":
        slab = _linear(peers[k]) if starting else 0
        src, dst = land_ref.at[slab], land_ref.at[slab]
    elif kind == "exchange":
        src, dst = src_ref.at[_linear(peer) if starting else 0], land_ref.at[me if starting else 0]
    else:
        src, dst = src_ref, land_ref.at[me if starting else 0]
    return pltpu.make_async_remote_copy(src_ref=src, dst_ref=dst, send_sem=send_sem, recv_sem=recv_sem,
                                        device_id=peer, device_id_type=MESH)


def _split_start(groups, carry, name):
    arrays = [a for _, srcs, lands in groups for a in list(srcs) + list(lands)] + [carry]

    def body(*refs):
        ins, sems = refs[:len(arrays)], refs[len(arrays):len(arrays) + 2 * len(groups)]
        at = 0
        for g, (kind, srcs, lands) in enumerate(groups):
            src_refs, land_refs = ins[at:at + len(srcs)], ins[at + len(srcs):at + len(srcs) + len(lands)]
            at += len(srcs) + len(lands)
            peers = COPY_PEERS[kind]
            for t in range(len(lands)):
                for slot, k in enumerate(peers):
                    sem = t * len(peers) + slot
                    _split_copy(kind, src_refs[t] if srcs else None, land_refs[t], k,
                                sems[2 * g].at[sem], sems[2 * g + 1].at[sem], True).start()

    sem_shapes = [pltpu.SemaphoreType.DMA((len(lands) * len(COPY_PEERS[kind]),))
                  for kind, _, lands in groups for _ in range(2)]
    out = pl.pallas_call(
        body,
        name=name,
        in_specs=[HBM_SPEC] * len(arrays),
        out_specs=[SEM_SPEC] * len(sem_shapes) + [HBM_SPEC] * len(arrays),
        out_shape=sem_shapes + [pltpu.HBM(a.shape, a.dtype) for a in arrays],
        input_output_aliases={i: len(sem_shapes) + i for i in range(len(arrays))},
        compiler_params=pltpu.CompilerParams(has_side_effects=pltpu.SideEffectType.DATAFLOW_SIDE_EFFECTING),
    )(*[pltpu.with_memory_space_constraint(a, pltpu.HBM) for a in arrays])
    sems, thru = out[:len(sem_shapes)], out[len(sem_shapes):]
    started, at = [], 0
    for g, (kind, srcs, lands) in enumerate(groups):
        n_s, n_l = len(srcs), len(lands)
        started.append((kind, sems[2 * g], sems[2 * g + 1], thru[at:at + n_s], thru[at + n_s:at + n_s + n_l]))
        at += n_s + n_l
    return started, thru[-1]


def _split_wait(started, after, name):
    kind, send_sems, recv_sems, srcs, lands = started
    n_s, n_l = len(srcs), len(lands)
    peers = COPY_PEERS[kind]

    def body(*refs):
        src_refs, land_refs = refs[:n_s], refs[n_s:n_s + n_l]
        send_ref, recv_ref = refs[n_s + n_l], refs[n_s + n_l + 1]
        for t in range(n_l):
            for slot, k in enumerate(peers):
                sem = t * len(peers) + slot
                copy = _split_copy(kind, src_refs[t] if n_s else None, land_refs[t], k,
                                   send_ref.at[sem], recv_ref.at[sem], False)
                copy.wait_send()
                copy.wait_recv()

    arrays = list(srcs) + list(lands)
    out = pl.pallas_call(
        body,
        name=name,
        in_specs=[HBM_SPEC] * len(arrays) + [SEM_SPEC, SEM_SPEC, pl.BlockSpec(memory_space=pl.ANY)],
        out_specs=[HBM_SPEC] * len(arrays),
        out_shape=[pltpu.HBM(a.shape, a.dtype) for a in arrays],
        input_output_aliases={i: i for i in range(len(arrays))},
        compiler_params=pltpu.CompilerParams(has_side_effects=pltpu.SideEffectType.DATAFLOW_SIDE_EFFECTING),
    )(*arrays, send_sems, recv_sems, after)
    return out[:n_s], out[n_s:]


def _pack(arrays, row_multiple):
    flat = jnp.concatenate([a.reshape(-1) for a in arrays])
    quantum = row_multiple * FLAT_LANES
    padded = -(-flat.shape[0] // quantum) * quantum
    return jnp.pad(flat, (0, padded - flat.shape[0])).reshape(-1, FLAT_LANES)


def _unpack(flat, like):
    flat = flat.reshape(-1)
    out, at = [], 0
    for a in like:
        size = math.prod(a.shape)
        out.append(flat[at:at + size].reshape(a.shape))
        at += size
    return out


def kernel(x, a_norm, a_w_in, a_sgu_norm, a_w_spatial, a_b_spatial, a_w_out, kv_norm, w_kv, b_norm, b_w_q, b_rel_bias, b_w_o, ffn_norm, ffn_w_gate_up, ffn_w_down, final_norm, loss_target, m_a_norm, m_a_w_in, m_a_sgu_norm, m_a_w_spatial, m_a_b_spatial, m_a_w_out, m_kv_norm, m_w_kv, m_b_norm, m_b_w_q, m_b_rel_bias, m_b_w_o, m_ffn_norm, m_ffn_w_gate_up, m_ffn_w_down, m_final_norm, v_a_norm, v_a_w_in, v_a_sgu_norm, v_a_w_spatial, v_a_b_spatial, v_a_w_out, v_kv_norm, v_w_kv, v_b_norm, v_b_w_q, v_b_rel_bias, v_b_w_o, v_ffn_norm, v_ffn_w_gate_up, v_ffn_w_down, v_final_norm):
    xs = x[0]
    target = loss_target[0]
    t, d = xs.shape
    n_a = a_w_in.shape[0]
    n_b = b_w_q.shape[0]
    depth = ffn_w_gate_up.shape[0]
    f_a = a_w_out.shape[1] * N_DEV
    gd = f_a // A_GROUPS
    nb_ffn = ffn_w_gate_up.shape[2]
    me = _linear(_position())

    small_rows = -(-(a_norm.size + a_sgu_norm.size) // (8 * 128)) * 8
    small = jnp.pad(jnp.concatenate([a_norm.reshape(-1), a_sgu_norm.reshape(-1)]),
                    (0, small_rows * 128 - a_norm.size - a_sgu_norm.size)).reshape(1, small_rows, 128)

    def shard(w, layer=None):
        return (w if layer is None else w[layer]).astype(BF16)

    stages = []
    for layer in range(depth):
        if layer == 0:
            stages += [("a0", [shard(a_w_in, 0)]), ("a0_out", [shard(a_w_out, 0)])]
        elif layer < n_a:
            stages.append((f"a{layer}", [shard(a_w_in, layer), shard(a_w_out, layer)]))
        else:
            i = layer - n_a
            shared = [shard(w_kv)] if i == 0 else []
            stages.append((f"b{i}", shared + [shard(b_w_q, i), shard(b_w_o, i)]))
        stages.append((f"f{layer}", [shard(ffn_w_gate_up, layer), shard(ffn_w_down, layer)]))
    first = _all_gather([s[None] for s in stages[0][1]] + [small], "gather_first")
    gathered = {stages[0][0]: [g[0] for g in first[:-1]]}
    small_g = first[-1].reshape(N_DEV, -1)
    a_norm_full = small_g[:, :a_norm.size].reshape(N_DEV, n_a, -1).transpose(1, 0, 2).reshape(n_a, d)
    a_sgu_full = small_g[:, a_norm.size:a_norm.size + a_sgu_norm.size].reshape(
        N_DEV, n_a, -1).transpose(1, 0, 2).reshape(n_a, f_a)
    two_level = ("f0", "a1", "f1")
    later = [("chips" if key in two_level else "gather", shards,
              [lax.dynamic_update_slice(lax.empty((N_DEV,) + s.shape, BF16), s[None], (me, 0, 0)) for s in shards])
             for key, shards in stages[1:]]
    started, a_norm_full = _split_start(later, a_norm_full, "gather_start")
    in_flight = {key: group for (key, _), group in zip(stages[1:], started)}

    def pass_on(key, carry):
        if key in two_level and key in in_flight and in_flight[key][0] == "chips":
            _, lands = _split_wait(in_flight.pop(key), carry, f"gather_wait_{key}_chips")
            (in_flight[key],), carry = _split_start([("forward", [], lands)], carry, f"gather_pass_on_{key}")
        return carry

    def weights(key, after):
        if key not in gathered:
            _, gathered[key] = _split_wait(in_flight.pop(key), after, f"gather_wait_{key}")
        return gathered[key]

    rows_down = ffn_w_down.shape[1]

    def mixer_a_weights(i, after):
        if i == 0:
            (w_in,), (w_out,) = weights("a0", after[0]), weights("a0_out", after[1])
        else:
            w_in, w_out = weights(f"a{i}", after[0])
        return w_in[None], w_out.reshape(1, f_a, d)

    def mixer_b_weights(i, after):
        ws = weights(f"b{i}", after)
        return ws[-2].reshape(1, d, d), ws[-1].reshape(1, d, d)

    def ffn_weights(layer, after):
        w_gu, w_dn = weights(f"f{layer}", after)
        return w_gu[None], w_dn.reshape(1, N_DEV // 2, 2 * rows_down, d)

    w_sp_t = jnp.swapaxes(a_w_spatial, -1, -2)
    b_full = jnp.repeat(jnp.swapaxes(a_b_spatial, -1, -2), gd, axis=-1)

    saved = []

    def ffn_fwd(xin, layer):
        hf = _rms_fwd(xin, ffn_norm[layer], f"ffn_norm_fwd_{layer}")
        w_gu, w_dn = ffn_weights(layer, xin)
        dact, act = _ffn_gate_up(f"ffn_gate_up_{layer}", hf, w_gu, 0)
        act = pass_on(f"a{layer + 1}", act)
        xout = _mm_down(f"ffn_down_{layer}", act, w_dn, 0, xin)
        return xout, (xin, hf, dact, act)

    for i in range(n_a):
        h = _rms_fwd(xs, a_norm_full[i], f"a_norm_fwd_{i}")
        zpre = _mm_colblock(f"a_in_{i}", h, weights(f"a{i}", xs)[0][None], 0)
        p, zs, dgs = _sgu_fwd(zpre, a_sgu_full[i], a_w_spatial[i], b_full[i], f"a_sgu_fwd_{i}")
        p = pass_on(f"f{i}", p)
        w_in, w_out = mixer_a_weights(i, (xs, p))
        x_mid = _mm_natural(f"a_out_{i}", p, w_out, 0, res=xs)
        x_out, ffn_saved = ffn_fwd(x_mid, i)
        saved.append((xs, h, zs, dgs, p, ffn_saved))
        xs = x_out

    x_kv = xs
    w_kv_g = weights("b0", x_kv)[0][None]
    h_kv = _rms_fwd(x_kv, kv_norm, "kv_norm_fwd")
    kv = _mm_colblock("kv_proj", h_kv, w_kv_g, 0)
    kvpad = jnp.pad(kv, ((LEFT, 0), (0, 0)))

    biases = [_bias_block(_bias_build(b_rel_bias[i], f"rel_bias_{i}")) for i in range(n_b)]
    for i in range(n_b):
        layer = n_a + i
        w_q, w_o = mixer_b_weights(i, xs)
        hb = _rms_fwd(xs, b_norm[i], f"b_norm_fwd_{i}")
        q = _mm_natural(f"b_q_{i}", hb, w_q, 0, out_dtype=BF16, scale=ATTN_SCALE)
        o = _attn_fwd(q, kvpad, biases[i], f"b_attn_fwd_{i}")
        x_mid = _mm_natural(f"b_o_{i}", o, w_o, 0, res=xs)
        x_out, ffn_saved = ffn_fwd(x_mid, layer)
        saved.append((xs, hb, q, o, ffn_saved))
        xs = x_out

    dx, loss_local, g_final = _loss_head(xs, final_norm, target, "loss_head")
    loss = lax.psum(loss_local, ("x", "y", "c"))

    big_grads = {}
    pending = []
    in_flight_grads = []

    def start_exchange(dx, tag):
        srcs = [big_grads[key] for key in pending]
        lands = [lax.empty(s.shape, BF16) for s in srcs]
        (group,), dx = _split_start([("exchange", srcs, lands)], dx, f"exchange_start_{tag}")
        in_flight_grads.append((list(pending), group, tag))
        pending.clear()
        return dx

    g_ffn_norm = [None] * depth
    g_a_norm = [None] * n_a
    g_a_sgu = [None] * n_a
    g_w_sp = [None] * n_a
    g_b_sp = [None] * n_a
    g_b_norm = [None] * n_b
    g_rel = [None] * n_b

    def ffn_bwd(dx, layer, ffn_saved):
        eager = layer < n_a
        xin, hf, dact, act = ffn_saved
        big_grads["ffn_w_down", layer] = _mm_dw_down(f"ffn_down_dw_{layer}", act, dx)
        pending.append(("ffn_w_down", layer))
        if eager:
            dx = start_exchange(dx, f"f{layer}_down")
        w_gu, w_dn = ffn_weights(layer, xin)
        dgu = _ffn_down_dx(f"ffn_down_dx_{layer}", dx, w_dn, 0, dact).reshape(N_DEV, t, nb_ffn)
        big_grads["ffn_w_gate_up", layer] = _mm_dw_colblock(
            f"ffn_gate_up_dw_{layer}", hf, dgu, blocked_in=True, transposed=True)
        pending.append(("ffn_w_gate_up", layer))
        if eager:
            dx = start_exchange(dx, f"f{layer}_gate_up")
        dx, g_ffn_norm[layer] = _mm_t_colblock_norm_bwd(
            f"ffn_gate_up_dx_{layer}", dgu, w_gu, 0, xin, ffn_norm[layer], dx, blocked_in=True)
        return dx

    dk = dv = None
    for i in reversed(range(n_b)):
        layer = n_a + i
        x_in, hb, q, o, ffn_saved = saved[layer]
        dx = ffn_bwd(dx, layer, ffn_saved)
        big_grads["b_w_o", i] = _mm_dw_natural(f"b_o_dw_{i}", o, dx)
        w_q, w_o = mixer_b_weights(i, x_in)
        do = _mm_t_natural(f"b_o_dx_{i}", dx, w_o, 0)
        dq, dk, dv, dbias = _attn_bwd(q, kvpad, biases[i], o, do, dk, dv, f"b_attn_bwd_{i}")
        g_rel[i] = _bias_grad(dbias, f"rel_bias_grad_{i}")
        big_grads["b_w_q", i] = _mm_dw_natural(f"b_q_dw_{i}", hb, dq)
        pending.extend([("b_w_o", i), ("b_w_q", i)])
        dx, g_b_norm[i] = _mm_t_natural_norm_bwd(f"b_q_dx_{i}", dq, w_q, 0, x_in, b_norm[i], dx)
        if i > 0:
            dx = start_exchange(dx, f"b{i}")

    dkv = jnp.concatenate([dk[LEFT:], dv[LEFT:]], axis=1).astype(BF16)
    big_grads["w_kv", 0] = _mm_dw_colblock("kv_proj_dw", h_kv, dkv)
    pending.append(("w_kv", 0))
    dx, g_kv_norm = _mm_t_colblock_norm_bwd("kv_proj_dx", dkv, w_kv_g, 0, x_kv, kv_norm, dx)
    dx = start_exchange(dx, "kv")

    for i in reversed(range(n_a)):
        x_in, h, zs, dgs, p, ffn_saved = saved[i]
        dx = ffn_bwd(dx, i, ffn_saved)
        big_grads["a_w_out", i] = _mm_dw_natural(f"a_out_dw_{i}", p, dx)
        pending.append(("a_w_out", i))
        dx = start_exchange(dx, f"a{i}_out")
        w_in, w_out = mixer_a_weights(i, (x_in, p))
        dp = _mm_t_natural(f"a_out_dx_{i}", dx, w_out, 0)
        dz, g_w_sp[i], g_b_sp[i], g_a_sgu[i] = _sgu_bwd(
            zs, dgs, dp, a_sgu_full[i], a_w_spatial[i], w_sp_t[i], b_full[i], f"a_sgu_bwd_{i}")
        big_grads["a_w_in", i] = _mm_dw_colblock(f"a_in_dw_{i}", h, dz)
        pending.append(("a_w_in", i))
        dx = start_exchange(dx, f"a{i}_in")
        dx, g_a_norm[i] = _mm_t_colblock_norm_bwd(f"a_in_dx_{i}", dz, w_in, 0, x_in, a_norm_full[i], dx)
    grad_x = dx[None]

    small_like = [jax.ShapeDtypeStruct((n_a, d), F32), jax.ShapeDtypeStruct((n_a, f_a), F32),
                  a_w_spatial, a_b_spatial, kv_norm, b_norm, b_rel_bias, ffn_norm, final_norm]
    small_partial = _pack(
        [jnp.stack(g_a_norm), jnp.stack(g_a_sgu), jnp.stack(g_w_sp), jnp.stack(g_b_sp), g_kv_norm,
         jnp.stack(g_b_norm), jnp.stack(g_rel), jnp.stack(g_ffn_norm), g_final], N_DEV * 8)
    chunk_rows = small_partial.shape[0] // N_DEV
    arrived = {}
    for keys, group, tag in in_flight_grads:
        srcs, lands = _split_wait(group, dx, f"exchange_wait_{tag}")
        for key, src, land in zip(keys, srcs, lands):
            arrived[key] = (land, src)
    small_got = _exchange([small_partial.reshape(1, N_DEV, chunk_rows, FLAT_LANES)], "exchange_small")[0]
    small_sum = _ordered_sum(small_got[0], "small_grad_sum")
    small_all = _all_gather([small_sum[None]], "gather_small_grads")[0]
    (ga_norm, ga_sgu, gw_sp, gb_sp, gkv_norm, gb_norm, g_relb, gffn_norm, gfinal) = _unpack(small_all, small_like)

    results = {}
    big_names = ["a_w_in", "a_w_out", "w_kv", "b_w_q", "b_w_o", "ffn_w_gate_up", "ffn_w_down"]
    big_wmv = [(a_w_in, m_a_w_in, v_a_w_in), (a_w_out, m_a_w_out, v_a_w_out),
               (w_kv[None], m_w_kv[None], v_w_kv[None]), (b_w_q, m_b_w_q, v_b_w_q), (b_w_o, m_b_w_o, v_b_w_o),
               tuple(jnp.swapaxes(a, 1, 2) for a in (ffn_w_gate_up, m_ffn_w_gate_up, v_ffn_w_gate_up)),
               (ffn_w_down, m_ffn_w_down, v_ffn_w_down)]
    me_arr = jnp.reshape(me, (1,)).astype(jnp.int32)
    for name, (w, m, v) in zip(big_names, big_wmv):
        outs = None
        for layer in range(w.shape[0]):
            got, own = arrived[name, layer]
            outs = _adamw_layer(got, own, w, m, v, layer, outs, me_arr, f"adamw_{name}_{layer}")
        if name == "w_kv":
            outs = [o[0] for o in outs]
        if name == "ffn_w_gate_up":
            outs = [jnp.swapaxes(o, 1, 2) for o in outs]
        results[name] = outs

    n_cols = a_norm.shape[1]
    s_cols = a_sgu_norm.shape[1]
    small_g_list = [lax.dynamic_slice(ga_norm, (0, me * n_cols), (n_a, n_cols)),
                    lax.dynamic_slice(ga_sgu, (0, me * s_cols), (n_a, s_cols)),
                    gw_sp, gb_sp, gkv_norm, gb_norm, g_relb, gffn_norm, gfinal]
    small_names = ["a_norm", "a_sgu_norm", "a_w_spatial", "a_b_spatial", "kv_norm", "b_norm", "b_rel_bias",
                   "ffn_norm", "final_norm"]
    small_w = [a_norm, a_sgu_norm, a_w_spatial, a_b_spatial, kv_norm, b_norm, b_rel_bias, ffn_norm, final_norm]
    small_m = [m_a_norm, m_a_sgu_norm, m_a_w_spatial, m_a_b_spatial, m_kv_norm, m_b_norm, m_b_rel_bias,
               m_ffn_norm, m_final_norm]
    small_v = [v_a_norm, v_a_sgu_norm, v_a_w_spatial, v_a_b_spatial, v_kv_norm, v_b_norm, v_b_rel_bias,
               v_ffn_norm, v_final_norm]
    flat_g = _pack(small_g_list, 8)
    flat_out = _adamw(flat_g[None, None], _pack(small_w, 8)[None], _pack(small_m, 8)[None],
                      _pack(small_v, 8)[None], "adamw_small")
    unpacked = [_unpack(o[0], small_w) for o in flat_out]
    for idx, name in enumerate(small_names):
        results[name] = [unpacked[kind][idx] for kind in range(4)]

    order = ["a_norm", "a_w_in", "a_sgu_norm", "a_w_spatial", "a_b_spatial", "a_w_out", "kv_norm", "w_kv",
             "b_norm", "b_w_q", "b_rel_bias", "b_w_o", "ffn_norm", "ffn_w_gate_up", "ffn_w_down", "final_norm"]
    outputs = [loss, grad_x]
    for kind in range(4):
        outputs += [results[name][kind] for name in order]
    return tuple(outputs)
```

```python
import math

import jax
import jax.numpy as jnp
from jax import lax
from jax.experimental import pallas as pl
from jax.experimental.pallas import tpu as pltpu

F32 = jnp.float32
BF16 = jnp.bfloat16
MESH = pl.DeviceIdType.MESH
HBM_SPEC = pl.BlockSpec(memory_space=pltpu.HBM)
SEM_SPEC = pl.BlockSpec(memory_space=pltpu.SEMAPHORE)

N_DEV = 8
CHUNK = 64
A_CHUNK = 128
A_GROUPS = 8
N_LEFT_CHUNKS = 8
LEFT = N_LEFT_CHUNKS * CHUNK
PAIR_ROWS = 2 * CHUNK
PAIR_BAND = PAIR_ROWS + LEFT
DIAGONALS = PAIR_BAND + PAIR_ROWS
PAIRS_PER_BLOCK = 2
Q_BLOCK = PAIRS_PER_BLOCK * PAIR_ROWS
K_BLOCK = Q_BLOCK + LEFT
ATTN_UNROLL = 7
MAX_REL = 256
N_REL = 2 * MAX_REL + 1
REL_PAD = 640
HEAD_DIM = 64
HEAD_PAIR = 2 * HEAD_DIM
ATTN_SCALE = HEAD_DIM ** -0.5
EPS = 1e-6
NEG_INF = -1e30
ADAM_LR = 0.001
ADAM_B1 = 0.9
ADAM_B2 = 0.999
ADAM_EPS = 1e-08
ADAM_WD = 0.01
ADAM_STEP = 10
FLAT_LANES = 1024
F32_SUBLANES = 8
BF16_SUBLANES = 16
ADAMW_BLOCK_ELEMS = 256 * 1024
V7X_VMEM_BYTES = 64 * 1024 * 1024
VMEM_FLOOR_BYTES = 32 * 1024 * 1024
VMEM_CEIL_BYTES = V7X_VMEM_BYTES - 8 * 1024 * 1024

NN = (((1,), (0,)), ((), ()))
NT = (((1,), (1,)), ((), ()))
TN = (((0,), (0,)), ((), ()))


def _tile(n, pref):
    return pref if n % pref == 0 else n


def _row_tile(n, pref, mult):
    best = None
    for t in range(mult, min(n, pref) + 1, mult):
        if n % t == 0:
            best = t
    return best if best is not None else n


def _nbytes(shape, dtype):
    n = 1
    for s in shape:
        if s is not None:
            n *= s
    return n * jnp.dtype(dtype).itemsize


def _call(body, name, grid, in_specs, out_specs, out_shape, scratch=(), vmem_bytes=0, aliases=None):
    limit = int(min(max(VMEM_FLOOR_BYTES, vmem_bytes * 5 // 4), VMEM_CEIL_BYTES))
    return pl.pallas_call(
        body,
        name=name,
        grid=grid,
        in_specs=in_specs,
        out_specs=out_specs,
        out_shape=out_shape,
        scratch_shapes=list(scratch),
        input_output_aliases=aliases or {},
        compiler_params=pltpu.CompilerParams(
            dimension_semantics=("arbitrary",) * len(grid), vmem_limit_bytes=limit),
    )


ERFC_P = 0.3275911 / math.sqrt(2.0)
ERFC_HALF_COEFFS = tuple(0.5 * a for a in (1.061405429, -1.453152027, 1.421413741, -0.284496736, 0.254829592))


def _gelu_and_grad(x):
    d = 1.0 + ERFC_P * jnp.abs(x)
    r = pl.reciprocal(d, approx=True)
    t = r * (2.0 - d * r)
    a5, a4, a3, a2, a1 = ERFC_HALF_COEFFS
    ex = jnp.exp(-0.5 * (x * x))
    tail = ((((a5 * t + a4) * t + a3) * t + a2) * t + a1) * t * ex
    cdf = jnp.where(x < 0, tail, 1.0 - tail)
    return x * cdf, cdf + x * ex * (1.0 / math.sqrt(2.0 * math.pi))


def _sigmoid(x):
    return 1.0 / (1.0 + jnp.exp(-x))


def _split3(x):
    hi = x.astype(BF16)
    r1 = x - hi.astype(F32)
    mid = r1.astype(BF16)
    lo = (r1 - mid.astype(F32)).astype(BF16)
    return hi, mid, lo


def _rms_fwd(x, g, name):
    t, d = x.shape
    tm = _tile(t, 512)

    def body(x_ref, g_ref, o_ref):
        xf = x_ref[...]
        r = lax.rsqrt(jnp.mean(xf * xf, axis=-1, keepdims=True) + EPS)
        o_ref[...] = (xf * r * g_ref[...]).astype(o_ref.dtype)

    return _call(
        body, name, (t // tm,),
        [pl.BlockSpec((tm, d), lambda i: (i, 0)), pl.BlockSpec((1, d), lambda i: (0, 0))],
        pl.BlockSpec((tm, d), lambda i: (i, 0)),
        jax.ShapeDtypeStruct((t, d), BF16),
        vmem_bytes=2 * (_nbytes((tm, d), F32) + _nbytes((tm, d), BF16)) + 4 * _nbytes((tm, d), F32),
    )(x, g.reshape(1, d))


def _mm(name, dims, a, b, *, grid, a_spec, b_spec, out_shape, out_spec, acc_shape,
        res=None, res_spec=None, scale=None):
    nk = grid[2]
    has_res = res is not None

    def body(*refs):
        refs = list(refs)
        a_ref = refs.pop(0)
        b_ref = refs.pop(0)
        r_ref = refs.pop(0) if has_res else None
        o_ref = refs.pop(0)
        part = lax.dot_general(a_ref[...].astype(BF16), b_ref[...].astype(BF16), dims,
                               preferred_element_type=F32)

        def finish(acc):
            if scale is not None:
                acc = acc * scale
            if has_res:
                acc = acc + r_ref[...]
            o_ref[...] = acc.astype(o_ref.dtype)

        if nk == 1:
            finish(part)
        else:
            acc_ref = refs.pop(0)
            k = pl.program_id(2)

            @pl.when(k == 0)
            def _():
                acc_ref[...] = part

            @pl.when(k > 0)
            def _():
                acc_ref[...] += part

            @pl.when(k == nk - 1)
            def _():
                finish(acc_ref[...])

    operands = [a, b]
    in_specs = [a_spec, b_spec]
    vmem = 2 * (_nbytes(a_spec.block_shape, a.dtype) + _nbytes(b_spec.block_shape, b.dtype)
                + _nbytes(out_spec.block_shape, out_shape.dtype))
    vmem += 3 * _nbytes(acc_shape, F32)
    if has_res:
        operands.append(res)
        in_specs.append(res_spec)
        vmem += 2 * _nbytes(res_spec.block_shape, res.dtype)
    scratch = [pltpu.VMEM(acc_shape, F32)] if nk > 1 else []
    return _call(body, name, grid, in_specs, out_spec, out_shape, scratch=scratch, vmem_bytes=vmem)(*operands)


def _mm_colblock(name, h, w_g, layer):
    t, k = h.shape
    nb = w_g.shape[3]
    tm = _tile(t, 2048)
    return _mm(
        name, NN, h, w_g, grid=(t // tm, N_DEV, 1),
        a_spec=pl.BlockSpec((tm, k), lambda i, j, kk: (i, 0)),
        b_spec=pl.BlockSpec((None, None, k, nb), lambda i, j, kk: (layer, j, 0, 0)),
        out_shape=jax.ShapeDtypeStruct((t, N_DEV * nb), BF16),
        out_spec=pl.BlockSpec((tm, nb), lambda i, j, kk: (i, j)), acc_shape=(tm, nb))


def _mm_natural(name, a, w, layer, *, res=None, out_dtype=F32, scale=None):
    t, k = a.shape
    n = w.shape[2]
    tm = _tile(t, 1024)
    tn = _tile(n, 1024 if k <= 1024 else 512)
    res_spec = None if res is None else pl.BlockSpec((tm, tn), lambda i, j, kk: (i, j))
    return _mm(
        name, NN, a, w, grid=(t // tm, n // tn, 1),
        a_spec=pl.BlockSpec((tm, k), lambda i, j, kk: (i, 0)),
        b_spec=pl.BlockSpec((None, k, tn), lambda i, j, kk: (layer, 0, j)),
        out_shape=jax.ShapeDtypeStruct((t, n), out_dtype),
        out_spec=pl.BlockSpec((tm, tn), lambda i, j, kk: (i, j)),
        acc_shape=(tm, tn), res=res, res_spec=res_spec, scale=scale)


def _mm_down(name, act, w4, layer, res):
    nblk, t, kb = act.shape
    n = w4.shape[3]
    tm = _tile(t, 1024)

    def body(a_ref, b_ref, r_ref, o_ref):
        acc = r_ref[...]
        for u in range(nblk):
            acc = acc + jnp.dot(a_ref[u], b_ref[u], preferred_element_type=F32)
        o_ref[...] = acc

    row = pl.BlockSpec((tm, n), lambda i: (i, 0))
    return _call(
        body, name, (t // tm,),
        [pl.BlockSpec((nblk, tm, kb), lambda i: (0, i, 0)),
         pl.BlockSpec((None, nblk, kb, n), lambda i: (layer, 0, 0, 0)),
         row],
        row,
        jax.ShapeDtypeStruct((t, n), F32),
        vmem_bytes=2 * (_nbytes((nblk, tm, kb), BF16) + _nbytes((nblk, kb, n), BF16)) + 6 * _nbytes((tm, n), F32),
    )(act, w4, res)


def _mm_t_colblock_norm_bwd(name, dz, w_g, layer, x, g, dx_up, blocked_in=False):
    k = w_g.shape[2]
    nb = w_g.shape[3]
    t = x.shape[0]
    tm = _tile(t, 512)
    if blocked_in:
        a_spec = pl.BlockSpec((N_DEV, tm, nb), lambda i: (0, i, 0))
    else:
        a_spec = pl.BlockSpec((tm, N_DEV * nb), lambda i: (i, 0))

    def body(a_ref, b_ref, x_ref, g_ref, up_ref, dx_ref, dg_ref):
        @pl.when(pl.program_id(0) == 0)
        def _():
            dg_ref[...] = jnp.zeros_like(dg_ref)

        dy = None
        for u in range(N_DEV):
            a = a_ref[u] if blocked_in else a_ref[:, u * nb:(u + 1) * nb]
            term = lax.dot_general(a.astype(BF16), b_ref[u].astype(BF16), NT, preferred_element_type=F32)
            dy = term if dy is None else dy + term
        xf = x_ref[...]
        r = lax.rsqrt(jnp.mean(xf * xf, axis=-1, keepdims=True) + EPS)
        xhat = xf * r
        dxhat = dy * g_ref[...]
        dg_ref[...] += jnp.sum(dy * xhat, axis=0, keepdims=True)
        dx_ref[...] = up_ref[...] + r * (dxhat - xhat * jnp.mean(dxhat * xhat, axis=-1, keepdims=True))

    row = pl.BlockSpec((tm, k), lambda i: (i, 0))
    vec = pl.BlockSpec((1, k), lambda i: (0, 0))
    dx, dg = _call(
        body, name, (t // tm,),
        [a_spec, pl.BlockSpec((None, N_DEV, k, nb), lambda i: (layer, 0, 0, 0)), row, vec, row],
        [row, vec],
        [jax.ShapeDtypeStruct((t, k), F32), jax.ShapeDtypeStruct((1, k), F32)],
        vmem_bytes=2 * N_DEV * (_nbytes((tm, nb), BF16) + _nbytes((k, nb), BF16)) + 10 * _nbytes((tm, k), F32),
    )(dz, w_g, x, g.reshape(1, k), dx_up)
    return dx, dg.reshape(k)


def _ffn_gate_up(name, h, w_g, layer):
    t, k = h.shape
    nb = w_g.shape[3]
    half = N_DEV // 2
    tm = _tile(t, 1024)

    def body(h_ref, wg_ref, wu_ref, dact_ref, act_ref):
        hb = h_ref[...]
        gate = jnp.dot(hb, wg_ref[...], preferred_element_type=F32)
        up = jnp.dot(hb, wu_ref[...], preferred_element_type=F32)
        sig = _sigmoid(gate)
        silu = gate * sig
        dact_ref[0] = (up * (sig * (1.0 + gate * (1.0 - sig)))).astype(BF16)
        dact_ref[1] = silu.astype(BF16)
        act_ref[...] = (silu * up).astype(BF16)

    return _call(
        body, name, (t // tm, half),
        [pl.BlockSpec((tm, k), lambda i, j: (i, 0)),
         pl.BlockSpec((None, None, k, nb), lambda i, j: (layer, j, 0, 0)),
         pl.BlockSpec((None, None, k, nb), lambda i, j: (layer, half + j, 0, 0))],
        [pl.BlockSpec((2, None, tm, nb), lambda i, j: (0, j, i, 0)),
         pl.BlockSpec((None, tm, nb), lambda i, j: (j, i, 0))],
        [jax.ShapeDtypeStruct((2, half, t, nb), BF16), jax.ShapeDtypeStruct((half, t, nb), BF16)],
        vmem_bytes=2 * (_nbytes((tm, k), BF16) + 2 * _nbytes((k, nb), BF16) + 3 * _nbytes((tm, nb), BF16))
        + 8 * _nbytes((tm, nb), F32),
    )(h, w_g, w_g)


def _ffn_down_dx(name, dy, w4, layer, dact):
    t, n = dy.shape
    nblk, kb = w4.shape[1], w4.shape[2]
    tm = _tile(t, 1024)

    def body(dy_ref, w_ref, dact_ref, dgu_ref):
        da = lax.dot_general(dy_ref[...].astype(BF16), w_ref[...], NT, preferred_element_type=F32)
        dgu_ref[0] = (da * dact_ref[0].astype(F32)).astype(BF16)
        dgu_ref[1] = (da * dact_ref[1].astype(F32)).astype(BF16)

    blk = pl.BlockSpec((2, None, tm, kb), lambda i, j: (0, j, i, 0))
    return _call(
        body, name, (t // tm, nblk),
        [pl.BlockSpec((tm, n), lambda i, j: (i, 0)),
         pl.BlockSpec((None, None, kb, n), lambda i, j: (layer, j, 0, 0)),
         blk],
        blk,
        jax.ShapeDtypeStruct((2, nblk, t, kb), BF16),
        vmem_bytes=2 * (_nbytes((tm, n), F32) + _nbytes((kb, n), BF16) + 4 * _nbytes((tm, kb), BF16))
        + 8 * _nbytes((tm, kb), F32),
    )(dy, w4, dact)


def _mm_t_natural(name, dy, w, layer):
    t, n = dy.shape
    k = w.shape[1]
    tm = _tile(t, 1024)
    tk = _tile(k, 1024)
    return _mm(
        name, NT, dy, w, grid=(t // tm, k // tk, 1),
        a_spec=pl.BlockSpec((tm, n), lambda i, j, kk: (i, 0)),
        b_spec=pl.BlockSpec((None, tk, n), lambda i, j, kk: (layer, j, 0)),
        out_shape=jax.ShapeDtypeStruct((t, k), BF16),
        out_spec=pl.BlockSpec((tm, tk), lambda i, j, kk: (i, j)),
        acc_shape=(tm, tk))


def _mm_t_natural_norm_bwd(name, dy, w, layer, x, g, dx_up):
    t, n = dy.shape
    k = w.shape[1]
    tm = _tile(t, 1024)

    def body(a_ref, b_ref, x_ref, g_ref, up_ref, dx_ref, dg_ref):
        @pl.when(pl.program_id(0) == 0)
        def _():
            dg_ref[...] = jnp.zeros_like(dg_ref)

        dh = lax.dot_general(a_ref[...].astype(BF16), b_ref[...], NT, preferred_element_type=F32)
        xf = x_ref[...]
        r = lax.rsqrt(jnp.mean(xf * xf, axis=-1, keepdims=True) + EPS)
        xhat = xf * r
        dxhat = dh * g_ref[...]
        dg_ref[...] += jnp.sum(dh * xhat, axis=0, keepdims=True)
        dx_ref[...] = up_ref[...] + r * (dxhat - xhat * jnp.mean(dxhat * xhat, axis=-1, keepdims=True))

    row = pl.BlockSpec((tm, k), lambda i: (i, 0))
    vec = pl.BlockSpec((1, k), lambda i: (0, 0))
    dx, dg = _call(
        body, name, (t // tm,),
        [pl.BlockSpec((tm, n), lambda i: (i, 0)), pl.BlockSpec((None, k, n), lambda i: (layer, 0, 0)), row, vec, row],
        [row, vec],
        [jax.ShapeDtypeStruct((t, k), F32), jax.ShapeDtypeStruct((1, k), F32)],
        vmem_bytes=2 * (_nbytes((tm, n), dy.dtype) + _nbytes((k, n), BF16)) + 10 * _nbytes((tm, k), F32),
    )(dy, w, x, g.reshape(1, k), dx_up)
    return dx, dg.reshape(k)


def _mm_dw_colblock(name, h, dz, blocked_in=False, transposed=False):
    t, k = h.shape
    nb = dz.shape[2] if blocked_in else dz.shape[1] // N_DEV
    tk = _tile(t, 4096)
    h_spec = pl.BlockSpec((tk, k), lambda i, j, kk: (kk, 0))
    if blocked_in:
        dz_spec = pl.BlockSpec((None, tk, nb), lambda i, j, kk: (j, kk, 0))
    else:
        dz_spec = pl.BlockSpec((tk, nb), lambda i, j, kk: (kk, j))
    rows, cols = (nb, k) if transposed else (k, nb)
    return _mm(
        name, TN, *((dz, h) if transposed else (h, dz)), grid=(1, N_DEV, t // tk),
        a_spec=dz_spec if transposed else h_spec,
        b_spec=h_spec if transposed else dz_spec,
        out_shape=jax.ShapeDtypeStruct((N_DEV, rows, cols), BF16),
        out_spec=pl.BlockSpec((None, rows, cols), lambda i, j, kk: (j, 0, 0)),
        acc_shape=(rows, cols))


def _mm_dw_natural(name, a, dy):
    t, k = a.shape
    n = dy.shape[1]
    tko = _tile(k, 1024)
    tt = _tile(t, 2048)
    out = _mm(
        name, TN, a, dy, grid=(k // tko, 1, t // tt),
        a_spec=pl.BlockSpec((tt, tko), lambda i, j, kk: (kk, i)),
        b_spec=pl.BlockSpec((tt, n), lambda i, j, kk: (kk, 0)),
        out_shape=jax.ShapeDtypeStruct((k, n), BF16),
        out_spec=pl.BlockSpec((tko, n), lambda i, j, kk: (i, 0)),
        acc_shape=(tko, n))
    return out.reshape(N_DEV, k // N_DEV, n)


def _mm_dw_down(name, act, dy):
    nblk, t, kb = act.shape
    n = dy.shape[1]
    tt = _tile(t, 2048)
    out = _mm(
        name, TN, act, dy, grid=(nblk, 1, t // tt),
        a_spec=pl.BlockSpec((None, tt, kb), lambda i, j, kk: (i, kk, 0)),
        b_spec=pl.BlockSpec((tt, n), lambda i, j, kk: (kk, 0)),
        out_shape=jax.ShapeDtypeStruct((nblk, kb, n), BF16),
        out_spec=pl.BlockSpec((None, kb, n), lambda i, j, kk: (i, 0, 0)),
        acc_shape=(kb, n))
    return out.reshape(N_DEV, (nblk * kb) // N_DEV, n)


def _spatial_mask(transposed=False):
    r = lax.broadcasted_iota(jnp.int32, (A_CHUNK, A_CHUNK), 0) // CHUNK
    c = lax.broadcasted_iota(jnp.int32, (A_CHUNK, A_CHUNK), 1) // CHUNK
    return c >= r if transposed else r >= c


def _sgu_tile(t):
    return _tile(t, 2 * A_CHUNK)


def _sgu_fwd(zpre, g_sgu, w_sp, b_full, name):
    t, f2 = zpre.shape
    f = f2 // 2
    gd = f // A_GROUPS
    tm = _sgu_tile(t)

    def body(z_ref, g_ref, w_ref, b_ref, p_ref, zs_ref, dg_ref):
        mask = _spatial_mask()
        wm = [jnp.where(mask, w_ref[g], 0.0).astype(BF16) for g in range(A_GROUPS)]
        for c in range(tm // A_CHUNK):
            rows = pl.ds(c * A_CHUNK, A_CHUNK)
            z, dgelu = _gelu_and_grad(z_ref[rows, :].astype(F32))
            zs_ref[rows, :] = z.astype(BF16)
            dg_ref[rows, :] = dgelu.astype(BF16)
            u = z[:, :f]
            v0 = z[:, f:]
            r = lax.rsqrt(jnp.mean(v0 * v0, axis=-1, keepdims=True) + EPS)
            v1 = (v0 * r * g_ref[...]).astype(BF16)
            for g in range(A_GROUPS):
                cols = slice(g * gd, (g + 1) * gd)
                v2 = jnp.dot(wm[g], v1[:, cols], preferred_element_type=F32) + b_ref[:, cols]
                p_ref[rows, cols] = (u[:, cols] * v2).astype(BF16)

    return _call(
        body, name, (t // tm,),
        [pl.BlockSpec((tm, f2), lambda i: (i, 0)),
         pl.BlockSpec((1, f), lambda i: (0, 0)),
         pl.BlockSpec((A_GROUPS, A_CHUNK, A_CHUNK), lambda i: (0, 0, 0)),
         pl.BlockSpec((A_CHUNK, f), lambda i: (0, 0))],
        [pl.BlockSpec((tm, f), lambda i: (i, 0)), pl.BlockSpec((tm, f2), lambda i: (i, 0)),
         pl.BlockSpec((tm, f2), lambda i: (i, 0))],
        [jax.ShapeDtypeStruct((t, f), BF16), jax.ShapeDtypeStruct((t, f2), BF16), jax.ShapeDtypeStruct((t, f2), BF16)],
        vmem_bytes=6 * _nbytes((tm, f2), BF16) + 2 * _nbytes((tm, f), BF16) + 8 * _nbytes((A_CHUNK, f2), F32),
    )(zpre, g_sgu.reshape(1, f), w_sp, b_full)


def _sgu_bwd(zs, dgs, dp, g_sgu, w_sp, w_sp_t, b_full, name):
    t, f2 = zs.shape
    f = f2 // 2
    gd = f // A_GROUPS
    tm = _tile(t, 4 * A_CHUNK)
    n_steps = t // tm

    def body(z_ref, dgelu_ref, dp_ref, g_ref, w_ref, wt_ref, b_ref, dz_ref, dw_ref, db_ref, dg_ref, dv1_ref, dbf_ref):
        step = pl.program_id(0)

        @pl.when(step == 0)
        def _():
            dw_ref[...] = jnp.zeros_like(dw_ref)
            dg_ref[...] = jnp.zeros_like(dg_ref)
            dbf_ref[...] = jnp.zeros_like(dbf_ref)

        mask = _spatial_mask()
        mask_t = _spatial_mask(transposed=True)
        wm = [jnp.where(mask, w_ref[g], 0.0).astype(BF16) for g in range(A_GROUPS)]
        wmt = [jnp.where(mask_t, wt_ref[g], 0.0).astype(BF16) for g in range(A_GROUPS)]
        gain = g_ref[...]
        for c in range(tm // A_CHUNK):
            rows = pl.ds(c * A_CHUNK, A_CHUNK)
            z = z_ref[rows, :].astype(F32)
            dgelu = dgelu_ref[rows, :].astype(F32)
            u = z[:, :f]
            v0 = z[:, f:]
            r = lax.rsqrt(jnp.mean(v0 * v0, axis=-1, keepdims=True) + EPS)
            xhat = v0 * r
            v1 = (xhat * gain).astype(BF16)
            dpf = dp_ref[rows, :].astype(F32)
            for g in range(A_GROUPS):
                cols = slice(g * gd, (g + 1) * gd)
                v1g = v1[:, cols]
                v2 = jnp.dot(wm[g], v1g, preferred_element_type=F32) + b_ref[:, cols]
                dpg = dpf[:, cols]
                dz_ref[rows, cols] = (dpg * v2 * dgelu[:, cols]).astype(BF16)
                dv2 = dpg * u[:, cols]
                dbf_ref[:, cols] += dv2
                dv2b = dv2.astype(BF16)
                dwg = lax.dot_general(dv2b, v1g, NT, preferred_element_type=F32)
                dw_ref[g] += jnp.where(mask, dwg, 0.0)
                dv1_ref[:, cols] = jnp.dot(wmt[g], dv2b, preferred_element_type=F32)
            dv1 = dv1_ref[...]
            dxhat = dv1 * gain
            dg_ref[...] += jnp.sum(dv1 * xhat, axis=0, keepdims=True)
            dv0 = r * (dxhat - xhat * jnp.mean(dxhat * xhat, axis=-1, keepdims=True))
            dz_ref[rows, pl.ds(f, f)] = (dv0 * dgelu[:, f:]).astype(BF16)

        @pl.when(step == n_steps - 1)
        def _():
            for g in range(A_GROUPS):
                db_ref[g] = jnp.sum(dbf_ref[:, g * gd:(g + 1) * gd], axis=1, keepdims=True)

    wspec = pl.BlockSpec((A_GROUPS, A_CHUNK, A_CHUNK), lambda i: (0, 0, 0))
    dz, dw, db, dg = _call(
        body, name, (n_steps,),
        [pl.BlockSpec((tm, f2), lambda i: (i, 0)),
         pl.BlockSpec((tm, f2), lambda i: (i, 0)),
         pl.BlockSpec((tm, f), lambda i: (i, 0)),
         pl.BlockSpec((1, f), lambda i: (0, 0)),
         wspec, wspec,
         pl.BlockSpec((A_CHUNK, f), lambda i: (0, 0))],
        [pl.BlockSpec((tm, f2), lambda i: (i, 0)),
         wspec,
         pl.BlockSpec((A_GROUPS, A_CHUNK, 1), lambda i: (0, 0, 0)),
         pl.BlockSpec((1, f), lambda i: (0, 0))],
        [jax.ShapeDtypeStruct((t, f2), BF16),
         jax.ShapeDtypeStruct((A_GROUPS, A_CHUNK, A_CHUNK), F32),
         jax.ShapeDtypeStruct((A_GROUPS, A_CHUNK, 1), F32),
         jax.ShapeDtypeStruct((1, f), F32)],
        scratch=[pltpu.VMEM((A_CHUNK, f), F32), pltpu.VMEM((A_CHUNK, f), F32)],
        vmem_bytes=6 * _nbytes((tm, f2), BF16) + 2 * _nbytes((tm, f), BF16) + 12 * _nbytes((A_CHUNK, f2), F32),
    )(zs, dgs, dp, g_sgu.reshape(1, f), w_sp, w_sp_t, b_full)
    return dz, dw, db.reshape(A_GROUPS, A_CHUNK), dg.reshape(f)


def _pair_valid(qi, col):
    qc = qi // CHUNK
    kc = col // CHUNK
    return (kc >= qc) & (kc <= qc + N_LEFT_CHUNKS)


def _diagonal_onehot():
    e = lax.broadcasted_iota(jnp.int32, (REL_PAD, DIAGONALS), 1)
    idx = jnp.clip(PAIR_BAND - 1 - e, -MAX_REL, MAX_REL) + MAX_REL
    r = lax.broadcasted_iota(jnp.int32, (REL_PAD, DIAGONALS), 0)
    return jnp.where(r == idx, 1.0, 0.0).astype(BF16)


def _bias_build(table, name):
    h = table.shape[0]
    tab = jnp.pad(table, ((0, 0), (0, REL_PAD - N_REL)))

    def body(t_ref, o_ref):
        oh = _diagonal_onehot()
        diag = jnp.zeros((h, DIAGONALS), F32)
        for piece in _split3(t_ref[...]):
            diag += jnp.dot(piece, oh, preferred_element_type=F32)
        col = lax.broadcasted_iota(jnp.int32, (h, PAIR_BAND), 1)
        for qi in range(PAIR_ROWS):
            row = pltpu.roll(diag, (qi - (PAIR_ROWS - 1)) % DIAGONALS, 1)[:, :PAIR_BAND]
            o_ref[qi] = jnp.where(_pair_valid(qi, col), row, NEG_INF)

    out = _call(
        body, name, (1,),
        [pl.BlockSpec((h, REL_PAD), lambda i: (0, 0))],
        pl.BlockSpec((PAIR_ROWS, h, PAIR_BAND), lambda i: (0, 0, 0)),
        jax.ShapeDtypeStruct((PAIR_ROWS, h, PAIR_BAND), F32),
        vmem_bytes=4 * _nbytes((PAIR_ROWS, h, PAIR_BAND), F32),
    )(tab)
    return jnp.transpose(out, (1, 0, 2))


def _bias_block(pair_bias):
    rest = K_BLOCK - PAIR_BAND
    return jnp.concatenate(
        [jnp.pad(pair_bias, ((0, 0), (0, 0), (p * PAIR_ROWS, rest - p * PAIR_ROWS)), constant_values=NEG_INF)
         for p in range(PAIRS_PER_BLOCK)], axis=1)


def _bias_grad(dbias, name):
    h = dbias.shape[0]
    db_t = jnp.transpose(dbias, (1, 0, 2))

    def body(d_ref, o_ref):
        diag = jnp.zeros((h, DIAGONALS), F32)
        for qi in range(PAIR_ROWS):
            diag += pltpu.roll(d_ref[qi], PAIR_ROWS - 1 - qi, 1)
        oh = _diagonal_onehot()
        acc = jnp.zeros((h, REL_PAD), F32)
        for piece in _split3(diag):
            acc += lax.dot_general(piece, oh, NT, preferred_element_type=F32)
        o_ref[...] = acc

    out = _call(
        body, name, (1,),
        [pl.BlockSpec((PAIR_ROWS, h, DIAGONALS), lambda i: (0, 0, 0))],
        pl.BlockSpec((h, REL_PAD), lambda i: (0, 0)),
        jax.ShapeDtypeStruct((h, REL_PAD), F32),
        vmem_bytes=4 * _nbytes((PAIR_ROWS, h, DIAGONALS), F32),
    )(db_t)
    return out[:, :N_REL]


def _head_masks():
    lane = lax.broadcasted_iota(jnp.int32, (Q_BLOCK, HEAD_PAIR), 1)
    return lane < HEAD_DIM, lane >= HEAD_DIM


def _block_scores(qm, kb, bias, valid):
    s = lax.dot_general(qm, kb, NT, preferred_element_type=F32) + bias
    return s if valid is None else jnp.where(valid, s, NEG_INF)


def _softmax_rows(s):
    e = jnp.exp(s - jnp.max(s, axis=-1, keepdims=True))
    return e * (1.0 / jnp.sum(e, axis=-1, keepdims=True))


def _padded_then_plain(step, n_blocks):
    n_padded = min(LEFT // Q_BLOCK, n_blocks)
    lax.fori_loop(0, n_padded, lambda j, c: step(j, c, True), 0, unroll=True)
    lax.fori_loop(n_padded, n_blocks, lambda j, c: step(j, c, False), 0, unroll=ATTN_UNROLL)


def _attn_fwd(q, kvpad, bias, name):
    t, d = q.shape
    n_pairs = d // HEAD_PAIR
    n_blocks = t // Q_BLOCK

    def body(q_ref, k_ref, v_ref, b_ref, o_ref):
        masks = _head_masks()
        key = lax.broadcasted_iota(jnp.int32, (Q_BLOCK, K_BLOCK), 1)

        def step(j, carry, padded):
            r0 = pl.multiple_of(j * Q_BLOCK, Q_BLOCK)
            q2 = q_ref[pl.ds(r0, Q_BLOCK), :].astype(F32)
            kb = k_ref[pl.ds(r0, K_BLOCK), :]
            vb = v_ref[pl.ds(r0, K_BLOCK), :]
            valid = key >= LEFT - j * Q_BLOCK if padded else None
            scores = [_block_scores(jnp.where(masks[a], q2, 0.0).astype(BF16), kb, b_ref[a], valid) for a in range(2)]
            probs = [_softmax_rows(s).astype(BF16) for s in scores]
            outs = [jnp.dot(p, vb, preferred_element_type=F32) for p in probs]
            o_ref[pl.ds(r0, Q_BLOCK), :] = jnp.where(masks[0], outs[0], outs[1]).astype(BF16)
            return carry

        _padded_then_plain(step, n_blocks)

    return _call(
        body, name, (n_pairs,),
        [pl.BlockSpec((t, HEAD_PAIR), lambda p: (0, p)),
         pl.BlockSpec((LEFT + t, HEAD_PAIR), lambda p: (0, p)),
         pl.BlockSpec((LEFT + t, HEAD_PAIR), lambda p: (0, n_pairs + p)),
         pl.BlockSpec((2, Q_BLOCK, K_BLOCK), lambda p: (p, 0, 0))],
        pl.BlockSpec((t, HEAD_PAIR), lambda p: (0, p)),
        jax.ShapeDtypeStruct((t, d), BF16),
        vmem_bytes=8 * _nbytes((LEFT + t, HEAD_PAIR), BF16) + 12 * _nbytes((2, Q_BLOCK, K_BLOCK), F32),
    )(q, kvpad, kvpad, bias)


def _attn_bwd(q, kvpad, bias, o, do, dk_in, dv_in, name):
    t, d = q.shape
    n_pairs = d // HEAD_PAIR
    n_blocks = t // Q_BLOCK
    has_in = dk_in is not None

    def body(*refs):
        refs = list(refs)
        q_ref, k_ref, v_ref, b_ref, o_ref, do_ref = refs[:6]
        refs = refs[6:]
        if has_in:
            dki_ref, dvi_ref = refs[:2]
            refs = refs[2:]
        dq_ref, dk_ref, dv_ref, db_ref = refs
        masks = _head_masks()
        key = lax.broadcasted_iota(jnp.int32, (Q_BLOCK, K_BLOCK), 1)
        if has_in:
            dk_ref[...] = dki_ref[...]
            dv_ref[...] = dvi_ref[...]
        else:
            dk_ref[...] = jnp.zeros_like(dk_ref)
            dv_ref[...] = jnp.zeros_like(dv_ref)
        db_ref[...] = jnp.zeros_like(db_ref)

        def step(j, carry, padded):
            r0 = pl.multiple_of(j * Q_BLOCK, Q_BLOCK)
            q2 = q_ref[pl.ds(r0, Q_BLOCK), :].astype(F32)
            do2 = do_ref[pl.ds(r0, Q_BLOCK), :].astype(F32)
            do_o = do2 * o_ref[pl.ds(r0, Q_BLOCK), :].astype(F32)
            kb = k_ref[pl.ds(r0, K_BLOCK), :]
            vb = v_ref[pl.ds(r0, K_BLOCK), :]
            valid = key >= LEFT - j * Q_BLOCK if padded else None
            heads = range(2)
            qms = [jnp.where(masks[a], q2, 0.0).astype(BF16) for a in heads]
            doms = [jnp.where(masks[a], do2, 0.0).astype(BF16) for a in heads]
            scores = [_block_scores(qms[a], kb, b_ref[a], valid) for a in heads]
            dps = [lax.dot_general(doms[a], vb, NT, preferred_element_type=F32) for a in heads]
            ps = [_softmax_rows(s) for s in scores]
            rows = [jnp.sum(jnp.where(masks[a], do_o, 0.0), axis=-1, keepdims=True) for a in heads]
            dss = [ps[a] * (dps[a] - rows[a]) for a in heads]
            for a in heads:
                for pair in range(PAIRS_PER_BLOCK):
                    lo = pair * PAIR_ROWS
                    db_ref[a, :, pl.ds(0, PAIR_BAND)] += dss[a][lo:lo + PAIR_ROWS, lo:lo + PAIR_BAND]
            dsbs = [ds.astype(BF16) for ds in dss]
            pbs = [p.astype(BF16) for p in ps]
            dqs = [jnp.dot(dsbs[a], kb, preferred_element_type=F32) for a in heads]
            dk_acc = sum(lax.dot_general(dsbs[a], qms[a], TN, preferred_element_type=F32) for a in heads)
            dv_acc = sum(lax.dot_general(pbs[a], doms[a], TN, preferred_element_type=F32) for a in heads)
            dq = jnp.where(masks[0], dqs[0], dqs[1]) * ATTN_SCALE
            dq_ref[pl.ds(r0, Q_BLOCK), :] = dq.astype(BF16)
            dk_ref[pl.ds(r0, K_BLOCK), :] += dk_acc
            dv_ref[pl.ds(r0, K_BLOCK), :] += dv_acc
            return carry

        _padded_then_plain(step, n_blocks)

    q_spec = pl.BlockSpec((t, HEAD_PAIR), lambda p: (0, p))
    kv_spec = pl.BlockSpec((LEFT + t, HEAD_PAIR), lambda p: (0, p))
    operands = [q, kvpad, kvpad, bias, o, do]
    in_specs = [q_spec, kv_spec, pl.BlockSpec((LEFT + t, HEAD_PAIR), lambda p: (0, n_pairs + p)),
                pl.BlockSpec((2, Q_BLOCK, K_BLOCK), lambda p: (p, 0, 0)), q_spec, q_spec]
    aliases = None
    if has_in:
        operands += [dk_in, dv_in]
        in_specs += [kv_spec, kv_spec]
        aliases = {6: 1, 7: 2}
    return _call(
        body, name, (n_pairs,),
        in_specs,
        [q_spec, kv_spec, kv_spec, pl.BlockSpec((2, PAIR_ROWS, DIAGONALS), lambda p: (p, 0, 0))],
        [jax.ShapeDtypeStruct((t, d), BF16),
         jax.ShapeDtypeStruct((LEFT + t, d), F32),
         jax.ShapeDtypeStruct((LEFT + t, d), F32),
         jax.ShapeDtypeStruct((d // HEAD_DIM, PAIR_ROWS, DIAGONALS), F32)],
        vmem_bytes=10 * _nbytes((LEFT + t, HEAD_PAIR), BF16) + 8 * _nbytes((LEFT + t, HEAD_PAIR), F32)
        + 16 * _nbytes((2, Q_BLOCK, K_BLOCK), F32),
        aliases=aliases,
    )(*operands)


def _loss_head(x, g, target, name):
    t, d = x.shape
    tm = _tile(t, 512)

    def body(x_ref, g_ref, t_ref, dx_ref, loss_ref, dg_ref):
        @pl.when(pl.program_id(0) == 0)
        def _():
            loss_ref[...] = jnp.zeros_like(loss_ref)
            dg_ref[...] = jnp.zeros_like(dg_ref)

        xf = x_ref[...]
        r = lax.rsqrt(jnp.mean(xf * xf, axis=-1, keepdims=True) + EPS)
        xhat = xf * r
        diff = xhat * g_ref[...] - t_ref[...]
        row_loss = jnp.mean(diff * diff, axis=-1, keepdims=True)
        loss_ref[...] += 0.5 * jnp.sum(row_loss, axis=0, keepdims=True)
        dy = diff * (1.0 / d)
        dg_ref[...] += jnp.sum(dy * xhat, axis=0, keepdims=True)
        dxhat = dy * g_ref[...]
        dx_ref[...] = r * (dxhat - xhat * jnp.mean(dxhat * xhat, axis=-1, keepdims=True))

    row = pl.BlockSpec((tm, d), lambda i: (i, 0))
    vec = pl.BlockSpec((1, d), lambda i: (0, 0))
    dx, loss, dg = _call(
        body, name, (t // tm,),
        [row, vec, row],
        [row, pl.BlockSpec((1, 1), lambda i: (0, 0)), vec],
        [jax.ShapeDtypeStruct((t, d), F32), jax.ShapeDtypeStruct((1, 1), F32), jax.ShapeDtypeStruct((1, d), F32)],
        vmem_bytes=10 * _nbytes((tm, d), F32),
    )(x, g.reshape(1, d), target)
    return dx, loss[0, 0], dg.reshape(d)


def _adamw_store(g, w_ref, m_ref, v_ref, g_ref, d_ref, nm_ref, nv_ref):
    c1 = 1.0 / (1.0 - ADAM_B1 ** ADAM_STEP)
    c2 = 1.0 / (1.0 - ADAM_B2 ** ADAM_STEP)
    nm = ADAM_B1 * m_ref[...] + (1.0 - ADAM_B1) * g
    nv = ADAM_B2 * v_ref[...] + (1.0 - ADAM_B2) * (g * g)
    g_ref[...] = g
    nm_ref[...] = nm
    nv_ref[...] = nv
    d_ref[...] = -ADAM_LR * ((nm * c1) / (jnp.sqrt(nv * c2) + ADAM_EPS) + ADAM_WD * w_ref[...])


def _adamw_layer(recv, own, w, m, v, layer, prev, me, name):
    n_src, r, c = recv.shape
    tr = _row_tile(r, max(BF16_SUBLANES, ADAMW_BLOCK_ELEMS // c), BF16_SUBLANES)
    first = prev is None

    def body(me_ref, recv_ref, own_ref, w_ref, m_ref, v_ref, *rest):
        mine = me_ref[0]
        own_part = own_ref[...].astype(F32)
        g = None
        for s in range(n_src):
            part = jnp.where(mine == s, own_part, recv_ref[s].astype(F32))
            g = part if g is None else g + part
        _adamw_store(g, w_ref, m_ref, v_ref, *rest[-4:])

    blk = pl.BlockSpec((None, tr, c), lambda i, me_ref: (layer, i, 0))
    any_spec = pl.BlockSpec(memory_space=pl.ANY)
    out = jax.ShapeDtypeStruct(w.shape, F32)
    operands = [me, recv, own, w, m, v] + ([] if first else list(prev))
    vmem = 2 * _nbytes((n_src + 1, tr, c), BF16) + 18 * _nbytes((tr, c), F32)
    return pl.pallas_call(
        body,
        name=name,
        grid_spec=pltpu.PrefetchScalarGridSpec(
            num_scalar_prefetch=1,
            grid=(r // tr,),
            in_specs=[pl.BlockSpec((n_src, tr, c), lambda i, me_ref: (0, i, 0)),
                      pl.BlockSpec((None, tr, c), lambda i, me_ref: (me_ref[0], i, 0)),
                      blk, blk, blk] + ([] if first else [any_spec] * 4),
            out_specs=[blk, blk, blk, blk],
        ),
        out_shape=[out, out, out, out],
        input_output_aliases={} if first else {6 + j: j for j in range(4)},
        compiler_params=pltpu.CompilerParams(
            dimension_semantics=("arbitrary",),
            vmem_limit_bytes=int(min(max(VMEM_FLOOR_BYTES, vmem * 5 // 4), VMEM_CEIL_BYTES))),
    )(*operands)


def _adamw(parts, w, m, v, name):
    n_layers, n_src, r, c = parts.shape
    mult = BF16_SUBLANES if parts.dtype == BF16 else F32_SUBLANES
    tr = _row_tile(r, max(mult, ADAMW_BLOCK_ELEMS // c), mult)

    def body(p_ref, w_ref, m_ref, v_ref, g_ref, d_ref, nm_ref, nv_ref):
        g = p_ref[0].astype(F32)
        for s in range(1, n_src):
            g = g + p_ref[s].astype(F32)
        _adamw_store(g, w_ref, m_ref, v_ref, g_ref, d_ref, nm_ref, nv_ref)

    blk = pl.BlockSpec((None, tr, c), lambda l, i: (l, i, 0))
    out = jax.ShapeDtypeStruct((n_layers, r, c), F32)
    return _call(
        body, name, (n_layers, r // tr),
        [pl.BlockSpec((None, n_src, tr, c), lambda l, i: (l, 0, i, 0)), blk, blk, blk],
        [blk, blk, blk, blk],
        [out, out, out, out],
        vmem_bytes=2 * _nbytes((n_src, tr, c), parts.dtype) + 18 * _nbytes((tr, c), F32),
    )(parts, w, m, v)


def _ordered_sum(parts, name):
    n_src, r, c = parts.shape

    def body(p_ref, o_ref):
        acc = p_ref[0]
        for s in range(1, n_src):
            acc = acc + p_ref[s]
        o_ref[...] = acc

    return _call(
        body, name, (1,),
        [pl.BlockSpec((n_src, r, c), lambda i: (0, 0, 0))],
        pl.BlockSpec((r, c), lambda i: (0, 0)),
        jax.ShapeDtypeStruct((r, c), F32),
        vmem_bytes=4 * _nbytes((n_src, r, c), F32),
    )(parts)


def _position():
    return lax.axis_index("x"), lax.axis_index("y"), lax.axis_index("c")


def _linear(p):
    return 4 * p[0] + 2 * p[1] + p[2]


def _all_gather(shards, name):
    n = len(shards)

    def body(*refs):
        ins, outs = refs[:n], refs[n:2 * n]
        send_sems, recv_sems, local_sems = refs[2 * n:]
        x, y, c = _position()
        me, sibling = (x, y, c), (x, y, 1 - c)
        chips = [(1 - x, y), (x, 1 - y), (1 - x, 1 - y)]

        def slab(t, p):
            return outs[t].at[:, _linear(p)]

        def copy(t, k, block, to, src=None):
            return pltpu.make_async_remote_copy(
                src_ref=slab(t, block) if src is None else src,
                dst_ref=slab(t, block),
                send_sem=send_sems.at[t, k],
                recv_sem=recv_sems.at[t, k],
                device_id=to,
                device_id_type=MESH,
            )

        started = []
        for t in range(n):
            mine = pltpu.make_async_copy(ins[t], slab(t, me), local_sems.at[t])
            mine.start()
            started.append(mine)
        sends = []
        for t in range(n):
            first = [copy(t, 0, me, sibling, src=ins[t])]
            first += [copy(t, 1 + j, me, (*chip, c), src=ins[t]) for j, chip in enumerate(chips)]
            for cp in first:
                cp.start()
            sends += first
        for t in range(n):
            for j, chip in enumerate(chips):
                copy(t, 1 + j, (*chip, c), me).wait_recv()
                passed = copy(t, 4 + j, (*chip, c), sibling)
                passed.start()
                sends.append(passed)
        for t in range(n):
            copy(t, 0, sibling, me).wait_recv()
            for j, chip in enumerate(chips):
                copy(t, 4 + j, (*chip, 1 - c), me).wait_recv()
        for cp in sends:
            cp.wait_send()
        for mine in started:
            mine.wait()

    out_shape = [jax.ShapeDtypeStruct((s.shape[0], N_DEV) + s.shape[1:], s.dtype) for s in shards]
    return pl.pallas_call(
        body,
        name=name,
        in_specs=[HBM_SPEC] * n,
        out_specs=[HBM_SPEC] * n,
        out_shape=out_shape,
        scratch_shapes=[
            pltpu.SemaphoreType.DMA((n, N_DEV - 1)),
            pltpu.SemaphoreType.DMA((n, N_DEV - 1)),
            pltpu.SemaphoreType.DMA((n,)),
        ],
    )(*shards)


def _exchange(blocks, name):
    n = len(blocks)

    def body(*refs):
        ins, outs = refs[:n], refs[n:2 * n]
        send_sems, recv_sems, local_sems = refs[2 * n:]
        x, y, c = _position()
        me = _linear((x, y, c))
        flips = [(fx, fy, fc) for fx in (0, 1) for fy in (0, 1) for fc in (0, 1)][1:]

        def peer_of(flip):
            fx, fy, fc = flip
            return (1 - x if fx else x, 1 - y if fy else y, 1 - c if fc else c)

        def copy(t, k, peer):
            return pltpu.make_async_remote_copy(
                src_ref=ins[t].at[:, _linear(peer)],
                dst_ref=outs[t].at[:, me],
                send_sem=send_sems.at[t, k],
                recv_sem=recv_sems.at[t, k],
                device_id=peer,
                device_id_type=MESH,
            )

        def arrival(t, k, peer):
            return pltpu.make_async_remote_copy(
                src_ref=ins[t].at[:, _linear(peer)],
                dst_ref=outs[t].at[:, _linear(peer)],
                send_sem=send_sems.at[t, k],
                recv_sem=recv_sems.at[t, k],
                device_id=peer,
                device_id_type=MESH,
            )

        own = []
        for t in range(n):
            cp = pltpu.make_async_copy(ins[t].at[:, me], outs[t].at[:, me], local_sems.at[t])
            cp.start()
            own.append(cp)
        sends = []
        for t in range(n):
            for k, flip in enumerate(flips):
                cp = copy(t, k, peer_of(flip))
                cp.start()
                sends.append(cp)
        for t in range(n):
            for k, flip in enumerate(flips):
                arrival(t, k, peer_of(flip)).wait_recv()
        for cp in sends:
            cp.wait_send()
        for cp in own:
            cp.wait()

    out_shape = [jax.ShapeDtypeStruct(b.shape, b.dtype) for b in blocks]
    return pl.pallas_call(
        body,
        name=name,
        in_specs=[HBM_SPEC] * n,
        out_specs=[HBM_SPEC] * n,
        out_shape=out_shape,
        scratch_shapes=[
            pltpu.SemaphoreType.DMA((n, N_DEV - 1)),
            pltpu.SemaphoreType.DMA((n, N_DEV - 1)),
            pltpu.SemaphoreType.DMA((n,)),
        ],
    )(*blocks)


def _peers():
    x, y, c = _position()
    flips = [(fx, fy, fc) for fx in (0, 1) for fy in (0, 1) for fc in (0, 1)][1:]
    return [(1 - x if fx else x, 1 - y if fy else y, 1 - c if fc else c) for fx, fy, fc in flips]


SIBLING, OTHER_CHIPS = (0,), (1, 3, 5)
COPY_PEERS = {"gather": tuple(range(N_DEV - 1)), "exchange": tuple(range(N_DEV - 1)),
              "chips": SIBLING + OTHER_CHIPS, "forward": OTHER_CHIPS}


def _split_copy(kind, src_ref, land_ref, k, send_sem, recv_sem, starting):
    peers = _peers()
    peer = peers[SIBLING[0]] if kind == "forward" else peers[k]
    me = _linear(_position())
    if kind == "forward":
        slab = _linear(peers[k]) if starting else 0
        src, dst = land_ref.at[slab], land_ref.at[slab]
    elif kind == "exchange":
        src, dst = src_ref.at[_linear(peer) if starting else 0], land_ref.at[me if starting else 0]
    else:
        src, dst = src_ref, land_ref.at[me if starting else 0]
    return pltpu.make_async_remote_copy(src_ref=src, dst_ref=dst, send_sem=send_sem, recv_sem=recv_sem,
                                        device_id=peer, device_id_type=MESH)


def _split_start(groups, carry, name):
    arrays = [a for _, srcs, lands in groups for a in list(srcs) + list(lands)] + [carry]

    def body(*refs):
        ins, sems = refs[:len(arrays)], refs[len(arrays):len(arrays) + 2 * len(groups)]
        at = 0
        for g, (kind, srcs, lands) in enumerate(groups):
            src_refs, land_refs = ins[at:at + len(srcs)], ins[at + len(srcs):at + len(srcs) + len(lands)]
            at += len(srcs) + len(lands)
            peers = COPY_PEERS[kind]
            for t in range(len(lands)):
                for slot, k in enumerate(peers):
                    sem = t * len(peers) + slot
                    _split_copy(kind, src_refs[t] if srcs else None, land_refs[t], k,
                                sems[2 * g].at[sem], sems[2 * g + 1].at[sem], True).start()

    sem_shapes = [pltpu.SemaphoreType.DMA((len(lands) * len(COPY_PEERS[kind]),))
                  for kind, _, lands in groups for _ in range(2)]
    out = pl.pallas_call(
        body,
        name=name,
        in_specs=[HBM_SPEC] * len(arrays),
        out_specs=[SEM_SPEC] * len(sem_shapes) + [HBM_SPEC] * len(arrays),
        out_shape=sem_shapes + [pltpu.HBM(a.shape, a.dtype) for a in arrays],
        input_output_aliases={i: len(sem_shapes) + i for i in range(len(arrays))},
        compiler_params=pltpu.CompilerParams(has_side_effects=pltpu.SideEffectType.DATAFLOW_SIDE_EFFECTING),
    )(*[pltpu.with_memory_space_constraint(a, pltpu.HBM) for a in arrays])
    sems, thru = out[:len(sem_shapes)], out[len(sem_shapes):]
    started, at = [], 0
    for g, (kind, srcs, lands) in enumerate(groups):
        n_s, n_l = len(srcs), len(lands)
        started.append((kind, sems[2 * g], sems[2 * g + 1], thru[at:at + n_s], thru[at + n_s:at + n_s + n_l]))
        at += n_s + n_l
    return started, thru[-1]


def _split_wait(started, after, name):
    kind, send_sems, recv_sems, srcs, lands = started
    n_s, n_l = len(srcs), len(lands)
    peers = COPY_PEERS[kind]

    def body(*refs):
        src_refs, land_refs = refs[:n_s], refs[n_s:n_s + n_l]
        send_ref, recv_ref = refs[n_s + n_l], refs[n_s + n_l + 1]
        for t in range(n_l):
            for slot, k in enumerate(peers):
                sem = t * len(peers) + slot
                copy = _split_copy(kind, src_refs[t] if n_s else None, land_refs[t], k,
                                   send_ref.at[sem], recv_ref.at[sem], False)
                copy.wait_send()
                copy.wait_recv()

    arrays = list(srcs) + list(lands)
    out = pl.pallas_call(
        body,
        name=name,
        in_specs=[HBM_SPEC] * len(arrays) + [SEM_SPEC, SEM_SPEC, pl.BlockSpec(memory_space=pl.ANY)],
        out_specs=[HBM_SPEC] * len(arrays),
        out_shape=[pltpu.HBM(a.shape, a.dtype) for a in arrays],
        input_output_aliases={i: i for i in range(len(arrays))},
        compiler_params=pltpu.CompilerParams(has_side_effects=pltpu.SideEffectType.DATAFLOW_SIDE_EFFECTING),
    )(*arrays, send_sems, recv_sems, after)
    return out[:n_s], out[n_s:]


def _pack(arrays, row_multiple):
    flat = jnp.concatenate([a.reshape(-1) for a in arrays])
    quantum = row_multiple * FLAT_LANES
    padded = -(-flat.shape[0] // quantum) * quantum
    return jnp.pad(flat, (0, padded - flat.shape[0])).reshape(-1, FLAT_LANES)


def _unpack(flat, like):
    flat = flat.reshape(-1)
    out, at = [], 0
    for a in like:
        size = math.prod(a.shape)
        out.append(flat[at:at + size].reshape(a.shape))
        at += size
    return out


def kernel(x, a_norm, a_w_in, a_sgu_norm, a_w_spatial, a_b_spatial, a_w_out, kv_norm, w_kv, b_norm, b_w_q, b_rel_bias, b_w_o, ffn_norm, ffn_w_gate_up, ffn_w_down, final_norm, loss_target, m_a_norm, m_a_w_in, m_a_sgu_norm, m_a_w_spatial, m_a_b_spatial, m_a_w_out, m_kv_norm, m_w_kv, m_b_norm, m_b_w_q, m_b_rel_bias, m_b_w_o, m_ffn_norm, m_ffn_w_gate_up, m_ffn_w_down, m_final_norm, v_a_norm, v_a_w_in, v_a_sgu_norm, v_a_w_spatial, v_a_b_spatial, v_a_w_out, v_kv_norm, v_w_kv, v_b_norm, v_b_w_q, v_b_rel_bias, v_b_w_o, v_ffn_norm, v_ffn_w_gate_up, v_ffn_w_down, v_final_norm):
    xs = x[0]
    target = loss_target[0]
    t, d = xs.shape
    n_a = a_w_in.shape[0]
    n_b = b_w_q.shape[0]
    depth = ffn_w_gate_up.shape[0]
    f_a = a_w_out.shape[1] * N_DEV
    gd = f_a // A_GROUPS
    nb_ffn = ffn_w_gate_up.shape[2]
    me = _linear(_position())

    small_rows = -(-(a_norm.size + a_sgu_norm.size) // (8 * 128)) * 8
    small = jnp.pad(jnp.concatenate([a_norm.reshape(-1), a_sgu_norm.reshape(-1)]),
                    (0, small_rows * 128 - a_norm.size - a_sgu_norm.size)).reshape(1, small_rows, 128)

    def shard(w, layer=None):
        return (w if layer is None else w[layer]).astype(BF16)

    stages = []
    for layer in range(depth):
        if layer == 0:
            stages += [("a0", [shard(a_w_in, 0)]), ("a0_out", [shard(a_w_out, 0)])]
        elif layer < n_a:
            stages.append((f"a{layer}", [shard(a_w_in, layer), shard(a_w_out, layer)]))
        else:
            i = layer - n_a
            shared = [shard(w_kv)] if i == 0 else []
            stages.append((f"b{i}", shared + [shard(b_w_q, i), shard(b_w_o, i)]))
        stages.append((f"f{layer}", [shard(ffn_w_gate_up, layer), shard(ffn_w_down, layer)]))
    first = _all_gather([s[None] for s in stages[0][1]] + [small], "gather_first")
    gathered = {stages[0][0]: [g[0] for g in first[:-1]]}
    small_g = first[-1].reshape(N_DEV, -1)
    a_norm_full = small_g[:, :a_norm.size].reshape(N_DEV, n_a, -1).transpose(1, 0, 2).reshape(n_a, d)
    a_sgu_full = small_g[:, a_norm.size:a_norm.size + a_sgu_norm.size].reshape(
        N_DEV, n_a, -1).transpose(1, 0, 2).reshape(n_a, f_a)
    two_level = ("f0", "a1", "f1")
    later = [("chips" if key in two_level else "gather", shards,
              [lax.dynamic_update_slice(lax.empty((N_DEV,) + s.shape, BF16), s[None], (me, 0, 0)) for s in shards])
             for key, shards in stages[1:]]
    started, a_norm_full = _split_start(later, a_norm_full, "gather_start")
    in_flight = {key: group for (key, _), group in zip(stages[1:], started)}

    def pass_on(key, carry):
        if key in two_level and key in in_flight and in_flight[key][0] == "chips":
            _, lands = _split_wait(in_flight.pop(key), carry, f"gather_wait_{key}_chips")
            (in_flight[key],), carry = _split_start([("forward", [], lands)], carry, f"gather_pass_on_{key}")
        return carry

    def weights(key, after):
        if key not in gathered:
            _, gathered[key] = _split_wait(in_flight.pop(key), after, f"gather_wait_{key}")
        return gathered[key]

    rows_down = ffn_w_down.shape[1]

    def mixer_a_weights(i, after):
        if i == 0:
            (w_in,), (w_out,) = weights("a0", after[0]), weights("a0_out", after[1])
        else:
            w_in, w_out = weights(f"a{i}", after[0])
        return w_in[None], w_out.reshape(1, f_a, d)

    def mixer_b_weights(i, after):
        ws = weights(f"b{i}", after)
        return ws[-2].reshape(1, d, d), ws[-1].reshape(1, d, d)

    def ffn_weights(layer, after):
        w_gu, w_dn = weights(f"f{layer}", after)
        return w_gu[None], w_dn.reshape(1, N_DEV // 2, 2 * rows_down, d)

    w_sp_t = jnp.swapaxes(a_w_spatial, -1, -2)
    b_full = jnp.repeat(jnp.swapaxes(a_b_spatial, -1, -2), gd, axis=-1)

    saved = []

    def ffn_fwd(xin, layer):
        hf = _rms_fwd(xin, ffn_norm[layer], f"ffn_norm_fwd_{layer}")
        w_gu, w_dn = ffn_weights(layer, xin)
        dact, act = _ffn_gate_up(f"ffn_gate_up_{layer}", hf, w_gu, 0)
        act = pass_on(f"a{layer + 1}", act)
        xout = _mm_down(f"ffn_down_{layer}", act, w_dn, 0, xin)
        return xout, (xin, hf, dact, act)

    for i in range(n_a):
        h = _rms_fwd(xs, a_norm_full[i], f"a_norm_fwd_{i}")
        zpre = _mm_colblock(f"a_in_{i}", h, weights(f"a{i}", xs)[0][None], 0)
        p, zs, dgs = _sgu_fwd(zpre, a_sgu_full[i], a_w_spatial[i], b_full[i], f"a_sgu_fwd_{i}")
        p = pass_on(f"f{i}", p)
        w_in, w_out = mixer_a_weights(i, (xs, p))
        x_mid = _mm_natural(f"a_out_{i}", p, w_out, 0, res=xs)
        x_out, ffn_saved = ffn_fwd(x_mid, i)
        saved.append((xs, h, zs, dgs, p, ffn_saved))
        xs = x_out

    x_kv = xs
    w_kv_g = weights("b0", x_kv)[0][None]
    h_kv = _rms_fwd(x_kv, kv_norm, "kv_norm_fwd")
    kv = _mm_colblock("kv_proj", h_kv, w_kv_g, 0)
    kvpad = jnp.pad(kv, ((LEFT, 0), (0, 0)))

    biases = [_bias_block(_bias_build(b_rel_bias[i], f"rel_bias_{i}")) for i in range(n_b)]
    for i in range(n_b):
        layer = n_a + i
        w_q, w_o = mixer_b_weights(i, xs)
        hb = _rms_fwd(xs, b_norm[i], f"b_norm_fwd_{i}")
        q = _mm_natural(f"b_q_{i}", hb, w_q, 0, out_dtype=BF16, scale=ATTN_SCALE)
        o = _attn_fwd(q, kvpad, biases[i], f"b_attn_fwd_{i}")
        x_mid = _mm_natural(f"b_o_{i}", o, w_o, 0, res=xs)
        x_out, ffn_saved = ffn_fwd(x_mid, layer)
        saved.append((xs, hb, q, o, ffn_saved))
        xs = x_out

    dx, loss_local, g_final = _loss_head(xs, final_norm, target, "loss_head")
    loss = lax.psum(loss_local, ("x", "y", "c"))

    big_grads = {}
    pending = []
    in_flight_grads = []

    def start_exchange(dx, tag):
        srcs = [big_grads[key] for key in pending]
        lands = [lax.empty(s.shape, BF16) for s in srcs]
        (group,), dx = _split_start([("exchange", srcs, lands)], dx, f"exchange_start_{tag}")
        in_flight_grads.append((list(pending), group, tag))
        pending.clear()
        return dx

    g_ffn_norm = [None] * depth
    g_a_norm = [None] * n_a
    g_a_sgu = [None] * n_a
    g_w_sp = [None] * n_a
    g_b_sp = [None] * n_a
    g_b_norm = [None] * n_b
    g_rel = [None] * n_b

    def ffn_bwd(dx, layer, ffn_saved):
        eager = layer < n_a
        xin, hf, dact, act = ffn_saved
        big_grads["ffn_w_down", layer] = _mm_dw_down(f"ffn_down_dw_{layer}", act, dx)
        pending.append(("ffn_w_down", layer))
        if eager:
            dx = start_exchange(dx, f"f{layer}_down")
        w_gu, w_dn = ffn_weights(layer, xin)
        dgu = _ffn_down_dx(f"ffn_down_dx_{layer}", dx, w_dn, 0, dact).reshape(N_DEV, t, nb_ffn)
        big_grads["ffn_w_gate_up", layer] = _mm_dw_colblock(
            f"ffn_gate_up_dw_{layer}", hf, dgu, blocked_in=True, transposed=True)
        pending.append(("ffn_w_gate_up", layer))
        if eager:
            dx = start_exchange(dx, f"f{layer}_gate_up")
        dx, g_ffn_norm[layer] = _mm_t_colblock_norm_bwd(
            f"ffn_gate_up_dx_{layer}", dgu, w_gu, 0, xin, ffn_norm[layer], dx, blocked_in=True)
        return dx

    dk = dv = None
    for i in reversed(range(n_b)):
        layer = n_a + i
        x_in, hb, q, o, ffn_saved = saved[layer]
        dx = ffn_bwd(dx, layer, ffn_saved)
        big_grads["b_w_o", i] = _mm_dw_natural(f"b_o_dw_{i}", o, dx)
        w_q, w_o = mixer_b_weights(i, x_in)
        do = _mm_t_natural(f"b_o_dx_{i}", dx, w_o, 0)
        dq, dk, dv, dbias = _attn_bwd(q, kvpad, biases[i], o, do, dk, dv, f"b_attn_bwd_{i}")
        g_rel[i] = _bias_grad(dbias, f"rel_bias_grad_{i}")
        big_grads["b_w_q", i] = _mm_dw_natural(f"b_q_dw_{i}", hb, dq)
        pending.extend([("b_w_o", i), ("b_w_q", i)])
        dx, g_b_norm[i] = _mm_t_natural_norm_bwd(f"b_q_dx_{i}", dq, w_q, 0, x_in, b_norm[i], dx)
        if i > 0:
            dx = start_exchange(dx, f"b{i}")

    dkv = jnp.concatenate([dk[LEFT:], dv[LEFT:]], axis=1).astype(BF16)
    big_grads["w_kv", 0] = _mm_dw_colblock("kv_proj_dw", h_kv, dkv)
    pending.append(("w_kv", 0))
    dx, g_kv_norm = _mm_t_colblock_norm_bwd("kv_proj_dx", dkv, w_kv_g, 0, x_kv, kv_norm, dx)
    dx = start_exchange(dx, "kv")

    for i in reversed(range(n_a)):
        x_in, h, zs, dgs, p, ffn_saved = saved[i]
        dx = ffn_bwd(dx, i, ffn_saved)
        big_grads["a_w_out", i] = _mm_dw_natural(f"a_out_dw_{i}", p, dx)
        pending.append(("a_w_out", i))
        dx = start_exchange(dx, f"a{i}_out")
        w_in, w_out = mixer_a_weights(i, (x_in, p))
        dp = _mm_t_natural(f"a_out_dx_{i}", dx, w_out, 0)
        dz, g_w_sp[i], g_b_sp[i], g_a_sgu[i] = _sgu_bwd(
            zs, dgs, dp, a_sgu_full[i], a_w_spatial[i], w_sp_t[i], b_full[i], f"a_sgu_bwd_{i}")
        big_grads["a_w_in", i] = _mm_dw_colblock(f"a_in_dw_{i}", h, dz)
        pending.append(("a_w_in", i))
        dx = start_exchange(dx, f"a{i}_in")
        dx, g_a_norm[i] = _mm_t_colblock_norm_bwd(f"a_in_dx_{i}", dz, w_in, 0, x_in, a_norm_full[i], dx)
    grad_x = dx[None]

    small_like = [jax.ShapeDtypeStruct((n_a, d), F32), jax.ShapeDtypeStruct((n_a, f_a), F32),
                  a_w_spatial, a_b_spatial, kv_norm, b_norm, b_rel_bias, ffn_norm, final_norm]
    small_partial = _pack(
        [jnp.stack(g_a_norm), jnp.stack(g_a_sgu), jnp.stack(g_w_sp), jnp.stack(g_b_sp), g_kv_norm,
         jnp.stack(g_b_norm), jnp.stack(g_rel), jnp.stack(g_ffn_norm), g_final], N_DEV * 8)
    chunk_rows = small_partial.shape[0] // N_DEV
    arrived = {}
    for keys, group, tag in in_flight_grads:
        srcs, lands = _split_wait(group, dx, f"exchange_wait_{tag}")
        for key, src, land in zip(keys, srcs, lands):
            arrived[key] = (land, src)
    small_got = _exchange([small_partial.reshape(1, N_DEV, chunk_rows, FLAT_LANES)], "exchange_small")[0]
    small_sum = _ordered_sum(small_got[0], "small_grad_sum")
    small_all = _all_gather([small_sum[None]], "gather_small_grads")[0]
    (ga_norm, ga_sgu, gw_sp, gb_sp, gkv_norm, gb_norm, g_relb, gffn_norm, gfinal) = _unpack(small_all, small_like)

    results = {}
    big_names = ["a_w_in", "a_w_out", "w_kv", "b_w_q", "b_w_o", "ffn_w_gate_up", "ffn_w_down"]
    big_wmv = [(a_w_in, m_a_w_in, v_a_w_in), (a_w_out, m_a_w_out, v_a_w_out),
               (w_kv[None], m_w_kv[None], v_w_kv[None]), (b_w_q, m_b_w_q, v_b_w_q), (b_w_o, m_b_w_o, v_b_w_o),
               tuple(jnp.swapaxes(a, 1, 2) for a in (ffn_w_gate_up, m_ffn_w_gate_up, v_ffn_w_gate_up)),
               (ffn_w_down, m_ffn_w_down, v_ffn_w_down)]
    me_arr = jnp.reshape(me, (1,)).astype(jnp.int32)
    for name, (w, m, v) in zip(big_names, big_wmv):
        outs = None
        for layer in range(w.shape[0]):
            got, own = arrived[name, layer]
            outs = _adamw_layer(got, own, w, m, v, layer, outs, me_arr, f"adamw_{name}_{layer}")
        if name == "w_kv":
            outs = [o[0] for o in outs]
        if name == "ffn_w_gate_up":
            outs = [jnp.swapaxes(o, 1, 2) for o in outs]
        results[name] = outs

    n_cols = a_norm.shape[1]
    s_cols = a_sgu_norm.shape[1]
    small_g_list = [lax.dynamic_slice(ga_norm, (0, me * n_cols), (n_a, n_cols)),
                    lax.dynamic_slice(ga_sgu, (0, me * s_cols), (n_a, s_cols)),
                    gw_sp, gb_sp, gkv_norm, gb_norm, g_relb, gffn_norm, gfinal]
    small_names = ["a_norm", "a_sgu_norm", "a_w_spatial", "a_b_spatial", "kv_norm", "b_norm", "b_rel_bias",
                   "ffn_norm", "final_norm"]
    small_w = [a_norm, a_sgu_norm, a_w_spatial, a_b_spatial, kv_norm, b_norm, b_rel_bias, ffn_norm, final_norm]
    small_m = [m_a_norm, m_a_sgu_norm, m_a_w_spatial, m_a_b_spatial, m_kv_norm, m_b_norm, m_b_rel_bias,
               m_ffn_norm, m_final_norm]
    small_v = [v_a_norm, v_a_sgu_norm, v_a_w_spatial, v_a_b_spatial, v_kv_norm, v_b_norm, v_b_rel_bias,
               v_ffn_norm, v_final_norm]
    flat_g = _pack(small_g_list, 8)
    flat_out = _adamw(flat_g[None, None], _pack(small_w, 8)[None], _pack(small_m, 8)[None],
                      _pack(small_v, 8)[None], "adamw_small")
    unpacked = [_unpack(o[0], small_w) for o in flat_out]
    for idx, name in enumerate(small_names):
        results[name] = [unpacked[kind][idx] for kind in range(4)]

    order = ["a_norm", "a_w_in", "a_sgu_norm", "a_w_spatial", "a_b_spatial", "a_w_out", "kv_norm", "w_kv",
             "b_norm", "b_w_q", "b_rel_bias", "b_w_o", "ffn_norm", "ffn_w_gate_up", "ffn_w_down", "final_norm"]
    outputs = [loss, grad_x]
    for kind in range(4):
        outputs += [results[name][kind] for name in order]
    return tuple(outputs)
```
